```python
import math
import jax
import jax.numpy as jnp
from jax import lax
import numpy as np

D_MODEL = 1024
BATCH = 8
SEQ = 2048
DEPTH = 2

GRID_W = 64
CTX_LEN = 256
HEAD_DIM = 64
DIFF_HEADS = 4
DIFF_V_DIM = 2 * HEAD_DIM
WIN_Q_HEADS = 8
WIN_KV_HEADS = 2
WINDOW = 128
WIN_BLOCK = 128
Q_BLOCK = 128
ROPE_BASE = 10000.0
DIFF_QK_W = DIFF_HEADS * 2 * HEAD_DIM
DIFF_VW = DIFF_HEADS * DIFF_V_DIM
WIN_Q_W = WIN_Q_HEADS * HEAD_DIM
WIN_KV_W = WIN_KV_HEADS * HEAD_DIM
Q_W = DIFF_QK_W + WIN_Q_W
KV_W = DIFF_QK_W + DIFF_VW + 2 * WIN_KV_W
ATTN_PROJ_W = Q_W + KV_W
MIX_W = DIFF_VW + WIN_Q_W
HYENA_ORDER = 2
SHORT_CONV = 3
FILTER_EMB = 33
FILTER_HID = 64
FILTER_OUT_SCALE = 0.05
DECAY_TARGET = 1e-2
FAST_DECAY_PCT = 0.3
SLOW_DECAY_PCT = 1.5
N_EXPERTS = 16
N_GROUPS = 4
EXPERTS_PER_GROUP = N_EXPERTS // N_GROUPS
TOP_K = 2
EXPERT_FF = 1024
MOE_BLOCK = 128
LN_EPS = 1e-5
DEEPNORM_ALPHA = (2 * DEPTH) ** 0.25
DEEPNORM_BETA = (8 * DEPTH) ** -0.25
NEG_INF = -1e30

kernel_name = 'hybrid_diffattn_wingqa_hyena_groupmoe'


def layer_norm(x, g, b):
    xf = x.astype(jnp.float32)
    mu = xf.mean(-1, keepdims=True)
    var = jnp.square(xf - mu).mean(-1, keepdims=True)
    y = (xf - mu) * lax.rsqrt(var + LN_EPS) * g.astype(jnp.float32) + b.astype(jnp.float32)
    return y.astype(x.dtype)


def rms_norm(x, g):
    xf = x.astype(jnp.float32)
    y = xf * lax.rsqrt(jnp.mean(jnp.square(xf), -1, keepdims=True) + LN_EPS) * g.astype(jnp.float32)
    return y.astype(x.dtype)


def axial_rope_tables(rows):
    r, col = jnp.meshgrid(jnp.arange(rows, dtype=jnp.float32), jnp.arange(GRID_W, dtype=jnp.float32), indexing='ij')
    axis_dim = HEAD_DIM // 2
    inv_freq = ROPE_BASE ** (-jnp.arange(0, axis_dim, 2, dtype=jnp.float32) / axis_dim)
    ang = jnp.concatenate([r.reshape(-1, 1) * inv_freq, col.reshape(-1, 1) * inv_freq], -1)
    ang = jnp.concatenate([ang, ang], -1)
    return jnp.cos(ang), jnp.sin(ang)


def apply_rope(x, cos, sin):
    shape = (1, cos.shape[0]) + (1,) * (x.ndim - 3) + (HEAD_DIM,)
    cos = cos.reshape(shape).astype(x.dtype)
    sin = sin.reshape(shape).astype(x.dtype)
    x1, x2 = jnp.split(x, 2, axis=-1)
    return x * cos + jnp.concatenate([-x2, x1], -1) * sin


def diff_attention(q, k, v, lam):
    b, n, h, _, d = q.shape
    nb = n // Q_BLOCK
    scale = d ** -0.5
    qb = jnp.moveaxis(q.reshape(b, nb, Q_BLOCK, h, 2, d), 1, 0)

    def block(q_blk):
        s = jnp.einsum('bqhmd,bkhmd->mbhqk', q_blk, k).astype(jnp.float32) * scale
        p = jax.nn.softmax(s, axis=-1)
        a = (p[0] - lam * p[1]).astype(v.dtype)
        return jnp.einsum('bhqk,bkhe->bqhe', a, v)

    o = lax.map(block, qb)
    return jnp.moveaxis(o, 0, 1).reshape(b, n, h, v.shape[-1])


def window_gqa_latent(q, k, v, k_ctx, v_ctx, sink):
    b, n, hq, d = q.shape
    nb = n // WIN_BLOCK
    g = hq // WIN_KV_HEADS
    scale = d ** -0.5
    qb = q.reshape(b, nb, WIN_BLOCK, WIN_KV_HEADS, g, d)

    def band(t):
        tp = jnp.pad(t, ((0, 0), (WIN_BLOCK, WIN_BLOCK), (0, 0), (0, 0))).reshape(b, nb + 2, WIN_BLOCK, WIN_KV_HEADS, d)
        return jnp.concatenate([tp[:, :-2], tp[:, 1:-1], tp[:, 2:]], axis=2)

    kb, vb = band(k), band(v)
    n_loc = 3 * WIN_BLOCK
    s_loc = jnp.einsum('bnqhgd,bnkhd->bnhgqk', qb, kb).astype(jnp.float32) * scale
    q_off = jnp.arange(WIN_BLOCK)[:, None] + WIN_BLOCK
    k_off = jnp.arange(n_loc)[None, :]
    k_abs = (jnp.arange(nb)[:, None] - 1) * WIN_BLOCK + k_off
    allowed = (jnp.abs(q_off - k_off) <= WINDOW)[None] & ((k_abs >= 0) & (k_abs < n))[:, None, :]
    s_loc = jnp.where(allowed[None, :, None, None], s_loc, NEG_INF)
    s_ctx = jnp.einsum('bnqhgd,bchd->bnhgqc', qb, k_ctx).astype(jnp.float32) * scale
    s_sink = jnp.broadcast_to(sink.astype(jnp.float32).reshape(WIN_KV_HEADS, g)[None, None, :, :, None, None], s_loc.shape[:-1] + (1,))
    p = jax.nn.softmax(jnp.concatenate([s_loc, s_ctx, s_sink], -1), axis=-1).astype(v.dtype)
    n_ctx = k_ctx.shape[1]
    o = jnp.einsum('bnhgqk,bnkhd->bnqhgd', p[..., :n_loc], vb) + jnp.einsum('bnhgqc,bchd->bnqhgd', p[..., n_loc:n_loc + n_ctx], v_ctx)
    return o.reshape(b, n, hq, d)


def gqa_sink_dense(q, k, v, sink):
    b, n, hq, d = q.shape
    g = hq // WIN_KV_HEADS
    qg = q.reshape(b, n, WIN_KV_HEADS, g, d)
    s = jnp.einsum('bqhgd,bkhd->bhgqk', qg, k).astype(jnp.float32) * d ** -0.5
    s_sink = jnp.broadcast_to(sink.astype(jnp.float32).reshape(WIN_KV_HEADS, g)[None, :, :, None, None], s.shape[:-1] + (1,))
    p = jax.nn.softmax(jnp.concatenate([s, s_sink], -1), axis=-1)[..., :-1].astype(v.dtype)
    return jnp.einsum('bhgqk,bkhd->bqhgd', p, v).reshape(b, n, hq, d)


def attention_group_mixer(h, h_ctx, w_in, lam_vec, subln_g, sink, w_out, lam_init, cos, sin, ctx_out):
    def split_q(cols):
        qa = cols[..., :DIFF_QK_W].reshape(cols.shape[:2] + (DIFF_HEADS, 2, HEAD_DIM))
        qw = cols[..., DIFF_QK_W:Q_W].reshape(cols.shape[:2] + (WIN_Q_HEADS, HEAD_DIM))
        return qa, qw

    def split_kv(cols):
        o1 = DIFF_QK_W
        o2 = o1 + DIFF_VW
        o3 = o2 + WIN_KV_W
        ka = cols[..., :o1].reshape(cols.shape[:2] + (DIFF_HEADS, 2, HEAD_DIM))
        va = cols[..., o1:o2].reshape(cols.shape[:2] + (DIFF_HEADS, DIFF_V_DIM))
        kw = cols[..., o2:o3].reshape(cols.shape[:2] + (WIN_KV_HEADS, HEAD_DIM))
        vw = cols[..., o3:].reshape(cols.shape[:2] + (WIN_KV_HEADS, HEAD_DIM))
        return ka, va, kw, vw

    lv = lam_vec.astype(jnp.float32)
    lam = jnp.exp(jnp.sum(lv[0] * lv[1])) - jnp.exp(jnp.sum(lv[2] * lv[3])) + lam_init

    def merge(oa, ow):
        oa = rms_norm(oa, subln_g) * (1.0 - lam_init)
        cat = jnp.concatenate([oa.reshape(oa.shape[:2] + (DIFF_VW,)), ow.reshape(ow.shape[:2] + (WIN_Q_W,))], -1)
        return cat @ w_out

    proj = h @ w_in
    proj_c = h_ctx @ (w_in if ctx_out else w_in[:, Q_W:])
    ka_c, va_c, kw_c, vw_c = split_kv(proj_c[..., -KV_W:])
    qa, qw = split_q(proj[..., :Q_W])
    ka, va, kw, vw = split_kv(proj[..., Q_W:])
    qa, ka, qw, kw = apply_rope(qa, cos, sin), apply_rope(ka, cos, sin), apply_rope(qw, cos, sin), apply_rope(kw, cos, sin)
    oa = diff_attention(qa, jnp.concatenate([ka, ka_c], 1), jnp.concatenate([va, va_c], 1), lam)
    ow = window_gqa_latent(qw, kw, vw, kw_c, vw_c, sink)
    y = merge(oa, ow)
    if ctx_out:
        qa_c, qw_c = split_q(proj_c[..., :Q_W])
        y_c = merge(diff_attention(qa_c, ka_c, va_c, lam), gqa_sink_dense(qw_c, kw_c, vw_c, sink))
        return y, y_c
    return y, None


def hyena_filter_spectra(n, ffn_w_in, ffn_w_hid, ffn_b, sin_freq, ffn_w_out):
    f32 = jnp.float32
    t = jnp.linspace(0.0, 1.0, n, dtype=f32)[:, None]
    bands = (FILTER_EMB - 1) // 2
    w = 2.0 * math.pi * jnp.arange(n, dtype=f32)[:, None] / n
    fr = jnp.linspace(1e-4, bands - 1, bands, dtype=f32)[None, :]
    z = jnp.concatenate([t, jnp.cos(fr * w), -jnp.sin(fr * w)], -1)
    sf, fb = sin_freq.astype(f32), ffn_b.astype(f32)
    hid = jnp.sin(sf[0] * (z @ ffn_w_in.astype(f32) + fb[0]))
    hid = jnp.sin(sf[1] * (hid @ ffn_w_hid[0].astype(f32) + fb[1]))
    hid = jnp.sin(sf[2] * (hid @ ffn_w_hid[1].astype(f32) + fb[2]))
    filt = (hid @ ffn_w_out.astype(f32)).reshape(n, 2, HYENA_ORDER, D_MODEL)
    deltas = jnp.abs(jnp.linspace(math.log(DECAY_TARGET) / SLOW_DECAY_PCT, math.log(DECAY_TARGET) / FAST_DECAY_PCT, D_MODEL, dtype=f32))
    filt = filt * jnp.exp(-t * deltas)[:, None, None, :]
    fwd, bwd = filt[:, 0], filt[:, 1]
    kern = jnp.concatenate([fwd, jnp.zeros_like(fwd[:1]), jnp.flip(bwd[1:], 0)], 0)
    return jnp.fft.rfft(kern, axis=0)


def hyena_mixer(h, w_in, conv_w, conv_b, ffn_w_in, ffn_w_hid, ffn_b, sin_freq, ffn_w_out, skip, w_out):
    n = h.shape[1]
    u = h @ w_in
    up = jnp.pad(u, ((0, 0), (1, 1), (0, 0)))
    u = up[:, :-2] * conv_w[0] + up[:, 1:-1] * conv_w[1] + up[:, 2:] * conv_w[2] + conv_b
    v, x1, x2 = jnp.split(u, 3, axis=-1)
    spec = hyena_filter_spectra(n, ffn_w_in, ffn_w_hid, ffn_b, sin_freq, ffn_w_out)
    sk = skip.astype(jnp.float32)
    z = v
    for o, gate in enumerate((x1, x2)):
        zf32 = z.astype(jnp.float32)
        conv = jnp.fft.irfft(jnp.fft.rfft(zf32, n=2 * n, axis=1) * spec[None, :, o], n=2 * n, axis=1)[:, :n]
        z = gate * (conv + zf32 * sk[o]).astype(h.dtype)
    return z @ w_out


def moe_ffn(h, router_w, router_bias, w_gate, w_up, w_down):
    b, n, d = h.shape
    t_count = b * n
    hf = h.reshape(t_count, d)
    scores = jax.nn.softmax((hf @ router_w).astype(jnp.float32), axis=-1)
    sel = (scores + router_bias.astype(jnp.float32)).reshape(t_count, N_GROUPS, EXPERTS_PER_GROUP)
    group_score = lax.top_k(sel, 2)[0].sum(-1)
    grp = jnp.argmax(group_score, axis=-1)
    in_group = jnp.take_along_axis(sel, grp[:, None, None], axis=1)[:, 0]
    _, local = lax.top_k(in_group, TOP_K)
    e_idx = grp[:, None] * EXPERTS_PER_GROUP + local
    wts = jnp.take_along_axis(scores, e_idx, axis=-1)
    wts = wts / wts.sum(-1, keepdims=True)
    n_assign = t_count * TOP_K
    flat_e = e_idx.reshape(n_assign)
    flat_tok = jnp.arange(n_assign, dtype=jnp.int32) // TOP_K
    order = jnp.argsort(flat_e)
    se = flat_e[order]
    counts = jnp.bincount(flat_e, length=N_EXPERTS)
    pcounts = (counts + MOE_BLOCK - 1) // MOE_BLOCK * MOE_BLOCK
    pends = jnp.cumsum(pcounts)
    pstarts = pends - pcounts
    starts = jnp.cumsum(counts) - counts
    dest = pstarts[se] + jnp.arange(n_assign) - starts[se]
    n_blk = -(-(n_assign + N_EXPERTS * (MOE_BLOCK - 1)) // MOE_BLOCK)
    slot_tok = jnp.full((n_blk * MOE_BLOCK,), t_count, jnp.int32).at[dest].set(flat_tok[order])
    slot_w = jnp.zeros((n_blk * MOE_BLOCK,), jnp.float32).at[dest].set(wts.reshape(n_assign)[order])
    blk_e = jnp.minimum(jnp.searchsorted(pends, jnp.arange(n_blk) * MOE_BLOCK, side='right'), N_EXPERTS - 1)
    xs = jnp.concatenate([hf, jnp.zeros((1, d), hf.dtype)], 0)[slot_tok].reshape(n_blk, MOE_BLOCK, d)

    def expert_block(args):
        xb, e = args
        return (jax.nn.silu(xb @ w_gate[e]) * (xb @ w_up[e])) @ w_down[e]

    ys = lax.map(expert_block, (xs, blk_e)).reshape(n_blk * MOE_BLOCK, d)
    out = jnp.zeros((t_count + 1, d), hf.dtype).at[slot_tok].add(ys * slot_w[:, None].astype(ys.dtype))
    return out[:t_count].reshape(b, n, d)


def setup_inputs(seed: int = 0) -> dict:
    key = jax.random.key(seed)
    keys = iter(jax.random.split(key, 32))

    def nrm(shape, scale):
        return jax.random.normal(next(keys), shape, jnp.float32) * scale

    d = D_MODEL
    n_even = (DEPTH + 1) // 2
    n_odd = DEPTH // 2
    return {
        'x': nrm((BATCH, SEQ, d), 1.0),
        'c': nrm((BATCH, d), 1.0),
        'ctx': nrm((BATCH, CTX_LEN, d), 1.0),
        'c_ctx': nrm((d,), 1.0),
        'mod_w': nrm((DEPTH, d, 6 * d), 0.5 * d ** -0.5),
        'mod_b': nrm((DEPTH, 6 * d), 0.02),
        'ln_g': 1.0 + nrm((DEPTH, 2, d), 0.02),
        'ln_b': nrm((DEPTH, 2, d), 0.02),
        'attn_w_in': nrm((n_even, d, ATTN_PROJ_W), d ** -0.5),
        'attn_lambda': nrm((n_even, 4, HEAD_DIM), 0.1),
        'attn_subln_g': 1.0 + nrm((n_even, DIFF_V_DIM), 0.02),
        'attn_sink': nrm((n_even, WIN_Q_HEADS), 1.0),
        'attn_w_out': nrm((n_even, MIX_W, d), MIX_W ** -0.5 * DEEPNORM_BETA),
        'hy_w_in': nrm((n_odd, d, 3 * d), d ** -0.5),
        'hy_conv_w': nrm((n_odd, SHORT_CONV, 3 * d), SHORT_CONV ** -0.5),
        'hy_conv_b': nrm((n_odd, 3 * d), 0.02),
        'hy_ffn_w_in': nrm((n_odd, FILTER_EMB, FILTER_HID), FILTER_EMB ** -0.5),
        'hy_ffn_w_hid': nrm((n_odd, 2, FILTER_HID, FILTER_HID), FILTER_HID ** -0.5),
        'hy_ffn_b': nrm((n_odd, 3, FILTER_HID), 0.02),
        'hy_sin_freq': 1.0 + nrm((n_odd, 3, FILTER_HID), 0.02),
        'hy_ffn_w_out': nrm((n_odd, FILTER_HID, 2 * HYENA_ORDER * d), FILTER_OUT_SCALE * FILTER_HID ** -0.5),
        'hy_skip': nrm((n_odd, HYENA_ORDER, d), 1.0),
        'hy_w_out': nrm((n_odd, d, d), d ** -0.5 * DEEPNORM_BETA),
        'router_w': nrm((d, N_EXPERTS), d ** -0.5),
        'router_bias': nrm((N_EXPERTS,), 0.01),
        'exp_w_gate': nrm((DEPTH, N_EXPERTS, d, EXPERT_FF), d ** -0.5),
        'exp_w_up': nrm((DEPTH, N_EXPERTS, d, EXPERT_FF), d ** -0.5),
        'exp_w_down': nrm((DEPTH, N_EXPERTS, EXPERT_FF, d), EXPERT_FF ** -0.5 * DEEPNORM_BETA),
    }


def reference(x, c, ctx, c_ctx, mod_w, mod_b, ln_g, ln_b, attn_w_in, attn_lambda, attn_subln_g, attn_sink, attn_w_out,
              hy_w_in, hy_conv_w, hy_conv_b, hy_ffn_w_in, hy_ffn_w_hid, hy_ffn_b, hy_sin_freq, hy_ffn_w_out, hy_skip, hy_w_out,
              router_w, router_bias, exp_w_gate, exp_w_up, exp_w_down):
    n = x.shape[1]
    rows = n // GRID_W
    cos, sin = axial_rope_tables(rows)
    xc = ctx
    for i in range(DEPTH):
        is_attn = i % 2 == 0
        ctx_live = any(j % 2 == 0 for j in range(i + 1, DEPTH))
        sh1, sc1, g1, sh2, sc2, g2 = jnp.split((jax.nn.silu(c) @ mod_w[i] + mod_b[i])[:, None, :], 6, axis=-1)
        h = x * (1 + sc1) + sh1
        if is_attn or ctx_live:
            csh1, csc1, cg1, csh2, csc2, cg2 = jnp.split(jax.nn.silu(c_ctx) @ mod_w[i] + mod_b[i], 6, axis=-1)
            hc = xc * (1 + csc1) + csh1
        if is_attn:
            e = i // 2
            lam_init = 0.8 - 0.6 * math.exp(-0.3 * i)
            y, yc = attention_group_mixer(h, hc, attn_w_in[e], attn_lambda[e], attn_subln_g[e], attn_sink[e], attn_w_out[e],
                                          lam_init, cos, sin, ctx_live)
        else:
            o = i // 2
            hy = (hy_w_in[o], hy_conv_w[o], hy_conv_b[o], hy_ffn_w_in[o], hy_ffn_w_hid[o], hy_ffn_b[o], hy_sin_freq[o],
                  hy_ffn_w_out[o], hy_skip[o], hy_w_out[o])
            y = hyena_mixer(h, *hy)
            yc = hyena_mixer(hc, *hy) if ctx_live else None
        x = layer_norm(DEEPNORM_ALPHA * x + g1 * y, ln_g[i, 0], ln_b[i, 0])
        f = moe_ffn(x * (1 + sc2) + sh2, router_w, router_bias, exp_w_gate[i], exp_w_up[i], exp_w_down[i])
        x = layer_norm(DEEPNORM_ALPHA * x + g2 * f, ln_g[i, 1], ln_b[i, 1])
        if ctx_live:
            xc = layer_norm(DEEPNORM_ALPHA * xc + cg1 * yc, ln_g[i, 0], ln_b[i, 0])
            fc = moe_ffn(xc * (1 + csc2) + csh2, router_w, router_bias, exp_w_gate[i], exp_w_up[i], exp_w_down[i])
            xc = layer_norm(DEEPNORM_ALPHA * xc + cg2 * fc, ln_g[i, 1], ln_b[i, 1])
    return x
```

```python
import functools
import math

import jax
import jax.numpy as jnp
import numpy as np
from jax import lax
from jax.experimental import pallas as pl
from jax.experimental.pallas import tpu as pltpu

F32 = jnp.float32
BF16 = jnp.bfloat16
I32 = jnp.int32

D_MODEL = 1024
DEPTH = 2
GRID_W = 64
HEAD_DIM = 64
DIFF_HEADS = 4
WIN_Q_HEADS = 8
WIN_KV_HEADS = 2
WINDOW = 128
WIN_BLOCK = 128
ROPE_BASE = 10000.0
DIFF_QK_W = DIFF_HEADS * 2 * HEAD_DIM
DIFF_VW = DIFF_HEADS * 2 * HEAD_DIM
WIN_Q_W = WIN_Q_HEADS * HEAD_DIM
WIN_KV_W = WIN_KV_HEADS * HEAD_DIM
Q_W = DIFF_QK_W + WIN_Q_W
KV_W = DIFF_QK_W + DIFF_VW + 2 * WIN_KV_W
ATTN_PROJ_W = Q_W + KV_W
FILTER_EMB = 33
FILTER_HID = 64
DECAY_TARGET = 1e-2
FAST_DECAY_PCT = 0.3
SLOW_DECAY_PCT = 1.5
N_EXPERTS = 16
N_GROUPS = 4
EXPERTS_PER_GROUP = N_EXPERTS // N_GROUPS
EXPERT_FF = 1024
LN_EPS = 1e-5
DEEPNORM_ALPHA = (2 * DEPTH) ** 0.25
NEG_INF = -1e30

LANES = 128
MOD_ROWS = 16
VMEM_LIMIT = 60 * 1024 * 1024
MOE_BM = 256
NT_DIMS = (((1,), (1,)), ((), ()))


def _cparams(n_axes):
    return pltpu.CompilerParams(dimension_semantics=("arbitrary",) * n_axes, vmem_limit_bytes=VMEM_LIMIT)


@functools.lru_cache(maxsize=None)
def _rope_tables(n):
    rows = n // GRID_W
    r, col = np.meshgrid(np.arange(rows, dtype=np.float32), np.arange(GRID_W, dtype=np.float32), indexing="ij")
    axis_dim = HEAD_DIM // 2
    inv_freq = (ROPE_BASE ** (-np.arange(0, axis_dim, 2, dtype=np.float32) / axis_dim)).astype(np.float32)
    ang = np.concatenate([r.reshape(-1, 1) * inv_freq, col.reshape(-1, 1) * inv_freq], -1)
    ang = np.concatenate([ang, ang], -1).astype(np.float32)
    cos, sin = np.cos(ang), np.sin(ang)
    half = np.arange(HEAD_DIM) < HEAD_DIM // 2
    sin_signed = np.where(half[None, :], -sin, sin)
    reps = LANES // HEAD_DIM
    return (np.tile(cos, (1, reps)).astype(np.float32), np.tile(sin_signed, (1, reps)).astype(np.float32))


@functools.lru_cache(maxsize=None)
def _dft_tables(n, fb):
    big = 2 * n
    k = np.arange(n, dtype=np.int64)[:, None]
    t = np.arange(n, dtype=np.int64)[None, :]
    ang = ((k * t) % big).astype(np.float64) * (2.0 * math.pi / big)
    c, s = np.cos(ang), np.sin(ang)
    alt = (1 - 2 * (np.arange(n) & 1)).astype(np.float64)
    s_f = s.copy()
    s_f[0, :] = alt
    kb = n // fb
    fwd = np.concatenate([c.reshape(kb, fb, n), s_f.reshape(kb, fb, n)], axis=1)
    ci = c.T * (2.0 / big)
    ci[:, 0] = 1.0 / big
    si = s.T * (2.0 / big)
    si[:, 0] = alt / big
    inv = np.concatenate([ci.reshape(n, kb, fb).transpose(1, 0, 2), si.reshape(n, kb, fb).transpose(1, 0, 2)], axis=2)
    return fwd.astype(np.float32), inv.astype(np.float32)


@functools.lru_cache(maxsize=None)
def _filter_tables(n):
    t = np.linspace(0.0, 1.0, n, dtype=np.float32)[:, None]
    bands = (FILTER_EMB - 1) // 2
    w = (2.0 * math.pi * np.arange(n, dtype=np.float32)[:, None] / n).astype(np.float32)
    fr = np.linspace(1e-4, bands - 1, bands, dtype=np.float32)[None, :]
    z = np.concatenate([t, np.cos(fr * w), -np.sin(fr * w)], -1).astype(np.float32)
    zp = np.zeros((n, FILTER_HID), np.float32)
    zp[:, :FILTER_EMB] = z
    deltas = np.abs(np.linspace(math.log(DECAY_TARGET) / SLOW_DECAY_PCT, math.log(DECAY_TARGET) / FAST_DECAY_PCT,
                                D_MODEL, dtype=np.float32))
    decay = np.exp(-t * deltas[None, :]).astype(np.float32)
    return zp, decay


def _mods_kernel(c_ref, w_ref, b_ref, o_ref):
    c = c_ref[...]
    a = (c * jax.nn.sigmoid(c)).astype(BF16)
    o_ref[...] = jnp.dot(a, w_ref[...].astype(BF16), preferred_element_type=F32) + b_ref[...]


def _mods(c_rows, mod_w, mod_b):
    d = D_MODEL
    tn = 1536
    return pl.pallas_call(
        _mods_kernel,
        grid=(DEPTH, 6 * d // tn),
        in_specs=[pl.BlockSpec((MOD_ROWS, d), lambda l, j: (0, 0)),
                  pl.BlockSpec((None, d, tn), lambda l, j: (l, 0, j)),
                  pl.BlockSpec((None, 1, tn), lambda l, j: (l, 0, j))],
        out_specs=pl.BlockSpec((None, MOD_ROWS, tn), lambda l, j: (l, 0, j)),
        out_shape=jax.ShapeDtypeStruct((DEPTH, MOD_ROWS, 6 * d), F32),
        compiler_params=_cparams(2), name="mods",
    )(c_rows, mod_w, mod_b.reshape(DEPTH, 1, 6 * d))


def _modmm_kernel(*refs, n_groups, rope_groups, scaled_groups, chunk):
    if rope_groups:
        x_ref, sc_ref, sh_ref, w_ref, cos_ref, sin_ref, o_ref = refs
        cos, sin = cos_ref[...], sin_ref[...]
        lane = lax.broadcasted_iota(I32, (1, LANES), 1)
        first_half = (lane % HEAD_DIM) < HEAD_DIM // 2
    else:
        x_ref, sc_ref, sh_ref, w_ref, o_ref = refs
    h = (x_ref[...] * (1.0 + sc_ref[...]) + sh_ref[...]).astype(BF16)
    gpc = chunk // LANES
    for c in range(n_groups // gpc):
        acc = jnp.dot(h, w_ref[:, c * chunk:(c + 1) * chunk], preferred_element_type=F32)
        for j in range(gpc):
            g = c * gpc + j
            blk = acc[:, j * LANES:(j + 1) * LANES]
            if g in rope_groups:
                rot = jnp.where(first_half, pltpu.roll(blk, LANES - HEAD_DIM // 2, 1), pltpu.roll(blk, HEAD_DIM // 2, 1))
                blk = blk * cos + rot * sin
                if g in scaled_groups:
                    blk = blk * (HEAD_DIM ** -0.5)
            o_ref[:, g * LANES:(g + 1) * LANES] = blk.astype(o_ref.dtype)


def _modmm(x2d, sc, sh, w, rows_per_mod, tm, rope=None, name="modmm"):
    t, d = x2d.shape
    n = w.shape[1]
    tiles_per_mod = rows_per_mod // tm
    nmod = sc.shape[0]
    in_specs = [pl.BlockSpec((tm, d), lambda i: (i, 0)),
                pl.BlockSpec((None, 1, d), lambda i: (i // tiles_per_mod, 0, 0)),
                pl.BlockSpec((None, 1, d), lambda i: (i // tiles_per_mod, 0, 0)),
                pl.BlockSpec((d, n), lambda i: (0, 0))]
    args = [x2d, sc.reshape(nmod, 1, d), sh.reshape(nmod, 1, d), w]
    rope_groups, scaled_groups = (), ()
    if rope is not None:
        cos, sin, rope_groups, scaled_groups = rope
        in_specs += [pl.BlockSpec((tm, LANES), lambda i: (i % tiles_per_mod, 0)),
                     pl.BlockSpec((tm, LANES), lambda i: (i % tiles_per_mod, 0))]
        args += [cos, sin]
    kern = functools.partial(_modmm_kernel, n_groups=n // LANES, rope_groups=tuple(rope_groups),
                             scaled_groups=tuple(scaled_groups), chunk=256)
    return pl.pallas_call(
        kern, grid=(t // tm,), in_specs=in_specs,
        out_specs=pl.BlockSpec((tm, n), lambda i: (i, 0)),
        out_shape=jax.ShapeDtypeStruct((t, n), BF16),
        compiler_params=_cparams(1), name=name,
    )(*args)


def _diff_attn_kernel(q_ref, k_ref, v_ref, kc_ref, vc_ref, lam_ref, g_ref, o_ref, *, lam_init):
    q = q_ref[...]
    lane = lax.broadcasted_iota(I32, (1, LANES), 1)
    zero = jnp.zeros_like(q)
    k, kc = k_ref[...], kc_ref[...]

    def probs(qm):
        sl = lax.dot_general(qm, k, NT_DIMS, preferred_element_type=F32)
        sc = lax.dot_general(qm, kc, NT_DIMS, preferred_element_type=F32)
        m = jnp.maximum(jnp.max(sl, -1, keepdims=True), jnp.max(sc, -1, keepdims=True))
        pl_, pc = jnp.exp(sl - m), jnp.exp(sc - m)
        den = jnp.sum(pl_, -1, keepdims=True) + jnp.sum(pc, -1, keepdims=True)
        return pl_, pc, 1.0 / den

    lv = lam_ref[...]
    lam = (jnp.exp(jnp.sum(lv[0:1] * lv[1:2], keepdims=True)) - jnp.exp(jnp.sum(lv[2:3] * lv[3:4], keepdims=True))
           + lam_init)
    p1l, p1c, r1 = probs(jnp.where(lane < HEAD_DIM, q, zero))
    p2l, p2c, r2 = probs(jnp.where(lane >= HEAD_DIM, q, zero))
    w2 = lam * r2
    al = (p1l * r1 - p2l * w2).astype(BF16)
    ac = (p1c * r1 - p2c * w2).astype(BF16)
    o = (jnp.dot(al, v_ref[...], preferred_element_type=F32) + jnp.dot(ac, vc_ref[...], preferred_element_type=F32))
    ms = jnp.mean(o * o, -1, keepdims=True)
    o_ref[...] = (o * lax.rsqrt(ms + LN_EPS) * g_ref[...] * (1.0 - lam_init)).astype(o_ref.dtype)


def _diff_attn(proj, proj_c, lam_vec, subln_g, lam_init, tq):
    b, n, _ = proj.shape
    nc = proj_c.shape[1]
    kcol = Q_W // LANES
    vcol = (Q_W + DIFF_QK_W) // LANES
    vccol = DIFF_QK_W // LANES
    kern = functools.partial(_diff_attn_kernel, lam_init=lam_init)
    return pl.pallas_call(
        kern, grid=(b, DIFF_HEADS, n // tq),
        in_specs=[pl.BlockSpec((None, tq, LANES), lambda bi, h, i: (bi, i, h)),
                  pl.BlockSpec((None, n, LANES), lambda bi, h, i: (bi, 0, kcol + h)),
                  pl.BlockSpec((None, n, LANES), lambda bi, h, i: (bi, 0, vcol + h)),
                  pl.BlockSpec((None, nc, LANES), lambda bi, h, i: (bi, 0, h)),
                  pl.BlockSpec((None, nc, LANES), lambda bi, h, i: (bi, 0, vccol + h)),
                  pl.BlockSpec((4, HEAD_DIM), lambda bi, h, i: (0, 0)),
                  pl.BlockSpec((1, LANES), lambda bi, h, i: (0, 0))],
        out_specs=pl.BlockSpec((None, tq, LANES), lambda bi, h, i: (bi, i, h)),
        out_shape=jax.ShapeDtypeStruct((b, n, DIFF_VW), BF16),
        compiler_params=_cparams(3), name="diff_attn",
    )(proj, proj, proj, proj_c, proj_c, lam_vec, subln_g.reshape(1, LANES))


def _win_attn_kernel(sink_ref, q_ref, kp_ref, kn_ref, kx_ref, vp_ref, vn_ref, vx_ref, kc_ref, vc_ref, o_ref, *, seq):
    g = pl.program_id(1)
    n = pl.program_id(2)
    wb = WIN_BLOCK
    gq = WIN_Q_HEADS // WIN_KV_HEADS
    lane = lax.broadcasted_iota(I32, (1, LANES), 1)
    in_g = (lane // HEAD_DIM) == g
    q = q_ref[...].astype(F32)
    parts = []
    for j in range(gq):
        x = q[:, (j // 2) * LANES:(j // 2 + 1) * LANES]
        x = jnp.where(g == (j % 2), x, pltpu.roll(x, HEAD_DIM, 1))
        parts.append(jnp.where(in_g, x, 0.0))
    qs = jnp.concatenate(parts, 0).astype(BF16)
    k_loc = jnp.concatenate([kp_ref[...], kn_ref[...], kx_ref[...]], 0)
    v_loc = jnp.concatenate([vp_ref[...], vn_ref[...], vx_ref[...]], 0)
    s_loc = lax.dot_general(qs, k_loc, NT_DIMS, preferred_element_type=F32)
    s_ctx = lax.dot_general(qs, kc_ref[...], NT_DIMS, preferred_element_type=F32)
    row = lax.broadcasted_iota(I32, (gq * wb, 1), 0)
    q_off = row % wb + wb
    k_off = lax.broadcasted_iota(I32, (1, 3 * wb), 1)
    k_abs = (n - 1) * wb + k_off
    allowed = (jnp.abs(q_off - k_off) <= WINDOW) & (k_abs >= 0) & (k_abs < seq)
    s_loc = jnp.where(allowed, s_loc, NEG_INF)
    head = row // wb
    sk = jnp.zeros((gq * wb, 1), F32)
    for j in range(gq):
        sk = jnp.where(head == j, sink_ref[g * gq + j], sk)
    m = jnp.maximum(jnp.maximum(jnp.max(s_loc, -1, keepdims=True), jnp.max(s_ctx, -1, keepdims=True)), sk)
    p_loc, p_ctx = jnp.exp(s_loc - m), jnp.exp(s_ctx - m)
    den = jnp.sum(p_loc, -1, keepdims=True) + jnp.sum(p_ctx, -1, keepdims=True) + jnp.exp(sk - m)
    o = (jnp.dot(p_loc.astype(BF16), v_loc, preferred_element_type=F32)
         + jnp.dot(p_ctx.astype(BF16), vc_ref[...], preferred_element_type=F32)) * (1.0 / den)
    for cb in range(gq // 2):
        a = o[(2 * cb) * wb:(2 * cb + 1) * wb]
        bb = o[(2 * cb + 1) * wb:(2 * cb + 2) * wb]
        a = jnp.where(g == 0, a, pltpu.roll(a, HEAD_DIM, 1))
        bb = jnp.where(g == 1, bb, pltpu.roll(bb, HEAD_DIM, 1))
        o_ref[:, cb * LANES:(cb + 1) * LANES] = jnp.where(lane < HEAD_DIM, a, bb).astype(o_ref.dtype)


def _win_attn(proj, proj_c, sink):
    b, n, _ = proj.shape
    nc = proj_c.shape[1]
    nb = n // WIN_BLOCK
    gw = (WIN_Q_HEADS // WIN_KV_HEADS) * HEAD_DIM
    qcol = DIFF_QK_W // gw
    kcol = (Q_W + DIFF_QK_W + DIFF_VW) // LANES
    vcol = kcol + 1
    kccol = (DIFF_QK_W + DIFF_VW) // LANES
    prev = lambda bi, g, i: (bi, jnp.maximum(i - 1, 0), kcol)
    cur = lambda bi, g, i: (bi, i, kcol)
    nxt = lambda bi, g, i: (bi, jnp.minimum(i + 1, nb - 1), kcol)
    vprev = lambda bi, g, i: (bi, jnp.maximum(i - 1, 0), vcol)
    vcur = lambda bi, g, i: (bi, i, vcol)
    vnxt = lambda bi, g, i: (bi, jnp.minimum(i + 1, nb - 1), vcol)
    blk = (None, WIN_BLOCK, LANES)
    kern = functools.partial(_win_attn_kernel, seq=n)
    return pl.pallas_call(
        kern, grid=(b, WIN_KV_HEADS, nb),
        in_specs=[pl.BlockSpec(memory_space=pltpu.SMEM),
                  pl.BlockSpec((None, WIN_BLOCK, gw), lambda bi, g, i: (bi, i, qcol + g)),
                  pl.BlockSpec(blk, prev), pl.BlockSpec(blk, cur), pl.BlockSpec(blk, nxt),
                  pl.BlockSpec(blk, vprev), pl.BlockSpec(blk, vcur), pl.BlockSpec(blk, vnxt),
                  pl.BlockSpec((None, nc, LANES), lambda bi, g, i: (bi, 0, kccol)),
                  pl.BlockSpec((None, nc, LANES), lambda bi, g, i: (bi, 0, kccol + 1))],
        out_specs=pl.BlockSpec((None, WIN_BLOCK, gw), lambda bi, g, i: (bi, i, g)),
        out_shape=jax.ShapeDtypeStruct((b, n, WIN_Q_W), BF16),
        compiler_params=_cparams(3), name="win_attn",
    )(sink, proj, proj, proj, proj, proj, proj, proj, proj_c, proj_c)


def _layer_norm(r, g, b):
    mu = jnp.mean(r, -1, keepdims=True)
    dlt = r - mu
    var = jnp.mean(dlt * dlt, -1, keepdims=True)
    return dlt * lax.rsqrt(var + LN_EPS) * g + b


def _proj_ln_kernel(*refs, n_in):
    a_refs = refs[:n_in]
    w_refs = refs[n_in:2 * n_in]
    x_ref, gate_ref, lng_ref, lnb_ref, sc_ref, sh_ref, rw_ref, x1_ref, h2_ref, lg_ref = refs[2 * n_in:]
    y = jnp.dot(a_refs[0][...], w_refs[0][...], preferred_element_type=F32)
    for a_ref, w_ref in zip(a_refs[1:], w_refs[1:]):
        y = y + jnp.dot(a_ref[...], w_ref[...], preferred_element_type=F32)
    xn = _layer_norm(DEEPNORM_ALPHA * x_ref[...] + gate_ref[...] * y, lng_ref[...], lnb_ref[...])
    x1_ref[...] = xn
    h2 = xn * (1.0 + sc_ref[...]) + sh_ref[...]
    h2_ref[...] = h2
    lg_ref[...] = lax.dot_general(rw_ref[...], h2, NT_DIMS, precision=lax.Precision.HIGHEST,
                                  preferred_element_type=F32)


def _proj_ln(acts, ws, x2d, gate, ln_g, ln_b, sc, sh, router_wt, rows_per_mod, tm, name):
    t, d = x2d.shape
    n_in = len(acts)
    tiles_per_mod = rows_per_mod // tm
    nmod = gate.shape[0]
    row = lambda i: (i, 0)
    full = lambda i: (0, 0)
    mod = lambda i: (i // tiles_per_mod, 0, 0)
    in_specs = ([pl.BlockSpec((tm, a.shape[1]), row) for a in acts]
                + [pl.BlockSpec(w.shape, full) for w in ws]
                + [pl.BlockSpec((tm, d), row), pl.BlockSpec((None, 1, d), mod),
                   pl.BlockSpec((1, d), full), pl.BlockSpec((1, d), full),
                   pl.BlockSpec((None, 1, d), mod), pl.BlockSpec((None, 1, d), mod),
                   pl.BlockSpec((N_EXPERTS, d), full)])
    return pl.pallas_call(
        functools.partial(_proj_ln_kernel, n_in=n_in), grid=(t // tm,), in_specs=in_specs,
        out_specs=[pl.BlockSpec((tm, d), row), pl.BlockSpec((tm, d), row),
                   pl.BlockSpec((N_EXPERTS, tm), lambda i: (0, i))],
        out_shape=[jax.ShapeDtypeStruct((t, d), F32), jax.ShapeDtypeStruct((t, d), F32),
                   jax.ShapeDtypeStruct((N_EXPERTS, t), F32)],
        compiler_params=_cparams(1), name=name,
    )(*acts, *ws, x2d, gate.reshape(nmod, 1, d), ln_g.reshape(1, d), ln_b.reshape(1, d),
      sc.reshape(nmod, 1, d), sh.reshape(nmod, 1, d), router_wt)


def _first_argmax(vals):
    idx = jnp.zeros(vals[0].shape, I32)
    best = vals[0]
    for j in range(1, len(vals)):
        upd = vals[j] > best
        idx = jnp.where(upd, j, idx)
        best = jnp.where(upd, vals[j], best)
    return idx, best


def _route_kernel(lg_ref, bias_ref, e_ref, w_ref, rank_ref, cnt_ref, carry_ref, *, tr):
    @pl.when(pl.program_id(0) == 0)
    def _():
        carry_ref[...] = jnp.zeros_like(carry_ref)

    lg = lg_ref[...]
    ex = jnp.exp(lg - jnp.max(lg, 0, keepdims=True))
    scores = ex / jnp.sum(ex, 0, keepdims=True)
    sel = scores + bias_ref[...]
    rows = [sel[e:e + 1] for e in range(N_EXPERTS)]
    group_scores = []
    for g in range(N_GROUPS):
        r = rows[g * EXPERTS_PER_GROUP:(g + 1) * EXPERTS_PER_GROUP]
        best = None
        for i in range(EXPERTS_PER_GROUP):
            for j in range(i + 1, EXPERTS_PER_GROUP):
                s = r[i] + r[j]
                best = s if best is None else jnp.maximum(best, s)
        group_scores.append(best)
    grp, _ = _first_argmax(group_scores)
    vals = []
    for j in range(EXPERTS_PER_GROUP):
        v = rows[(N_GROUPS - 1) * EXPERTS_PER_GROUP + j]
        for g in range(N_GROUPS - 2, -1, -1):
            v = jnp.where(grp == g, rows[g * EXPERTS_PER_GROUP + j], v)
        vals.append(v)
    i0, _ = _first_argmax(vals)
    i1, _ = _first_argmax([jnp.where(i0 == j, -jnp.inf, vals[j]) for j in range(EXPERTS_PER_GROUP)])
    e0 = grp * EXPERTS_PER_GROUP + i0
    e1 = grp * EXPERTS_PER_GROUP + i1
    eid = lax.broadcasted_iota(I32, (N_EXPERTS, 1), 0)
    oh0, oh1 = eid == e0, eid == e1
    s0 = jnp.sum(jnp.where(oh0, scores, 0.0), 0, keepdims=True)
    s1 = jnp.sum(jnp.where(oh1, scores, 0.0), 0, keepdims=True)
    den = s0 + s1
    member = jnp.where(oh0 | oh1, 1.0, 0.0)
    before = lax.broadcasted_iota(I32, (tr, tr), 0) < lax.broadcasted_iota(I32, (tr, tr), 1)
    upper = jnp.where(before, 1.0, 0.0).astype(BF16)
    cnt = jnp.dot(member.astype(BF16), upper, preferred_element_type=F32) + carry_ref[:, 0:1]
    r0 = jnp.sum(jnp.where(oh0, cnt, 0.0), 0, keepdims=True)
    r1 = jnp.sum(jnp.where(oh1, cnt, 0.0), 0, keepdims=True)
    carry_ref[...] = carry_ref[...] + jnp.sum(member, 1, keepdims=True)
    e_ref[0:1, :] = e0
    e_ref[1:2, :] = e1
    w_ref[0:1, :] = s0 / den
    w_ref[1:2, :] = s1 / den
    rank_ref[0:1, :] = r0.astype(I32)
    rank_ref[1:2, :] = r1.astype(I32)
    cnt_ref[...] = carry_ref[...]


def _route(logits_t, router_bias, tr):
    t = logits_t.shape[1]
    tok = lambda i: (0, i)
    return pl.pallas_call(
        functools.partial(_route_kernel, tr=tr), grid=(t // tr,),
        in_specs=[pl.BlockSpec((N_EXPERTS, tr), tok), pl.BlockSpec((N_EXPERTS, 1), lambda i: (0, 0))],
        out_specs=[pl.BlockSpec((2, tr), tok), pl.BlockSpec((2, tr), tok), pl.BlockSpec((2, tr), tok),
                   pl.BlockSpec((N_EXPERTS, LANES), lambda i: (0, 0))],
        out_shape=[jax.ShapeDtypeStruct((2, t), I32), jax.ShapeDtypeStruct((2, t), F32),
                   jax.ShapeDtypeStruct((2, t), I32), jax.ShapeDtypeStruct((N_EXPERTS, LANES), F32)],
        scratch_shapes=[pltpu.VMEM((N_EXPERTS, LANES), F32)],
        compiler_params=_cparams(1), name="route",
    )(logits_t, router_bias.reshape(N_EXPERTS, 1))


def _row_copy(src_ref, src_row, dst_ref, dst_row, sem):
    return pltpu.make_async_copy(src_ref.at[pl.ds(src_row, 1)], dst_ref.at[pl.ds(dst_row, 1)], sem)


def _dispatch_kernel(slot_ref, h_ref, xs_in_ref, xs_ref, sem, *, td, t):
    del xs_in_ref
    base = pl.program_id(0) * td

    def issue(r, carry):
        _row_copy(h_ref, r, xs_ref, slot_ref[base + r], sem).start()
        _row_copy(h_ref, r, xs_ref, slot_ref[t + base + r], sem).start()
        return carry

    lax.fori_loop(0, td, issue, 0)

    def drain(r, carry):
        _row_copy(h_ref, 0, xs_ref, 0, sem).wait()
        _row_copy(h_ref, 0, xs_ref, 0, sem).wait()
        return carry

    lax.fori_loop(0, td, drain, 0)


def _dispatch(slots_flat, h2, n_slots, td):
    t, d = h2.shape
    xs0 = jnp.zeros((n_slots, d), F32)
    return pl.pallas_call(
        functools.partial(_dispatch_kernel, td=td, t=t),
        grid_spec=pltpu.PrefetchScalarGridSpec(
            num_scalar_prefetch=1, grid=(t // td,),
            in_specs=[pl.BlockSpec((td, d), lambda i, s: (i, 0)), pl.BlockSpec(memory_space=pl.ANY)],
            out_specs=pl.BlockSpec(memory_space=pl.ANY),
            scratch_shapes=[pltpu.SemaphoreType.DMA(())]),
        out_shape=jax.ShapeDtypeStruct((n_slots, d), F32),
        input_output_aliases={2: 0},
        compiler_params=_cparams(1), name="moe_dispatch",
    )(slots_flat, h2, xs0)


def _ffn_kernel(blk_e_ref, nused_ref, x_ref, wg_ref, wu_ref, wd_ref, o_ref):
    del blk_e_ref
    i = pl.program_id(0)

    @pl.when(i < nused_ref[0])
    def _():
        x = x_ref[...].astype(BF16)
        gate = jnp.dot(x, wg_ref[...], preferred_element_type=F32)
        up = jnp.dot(x, wu_ref[...], preferred_element_type=F32)
        act = (gate * jax.nn.sigmoid(gate) * up).astype(BF16)
        o_ref[...] = jnp.dot(act, wd_ref[...], preferred_element_type=F32)

    @pl.when(i >= nused_ref[0])
    def _():
        o_ref[...] = jnp.zeros_like(o_ref)


def _ffn(blk_e, n_used, xs, wg, wu, wd, bm):
    n_slots, d = xs.shape
    ff = wg.shape[2]
    xrow = lambda i, be, nu: (jnp.minimum(i, nu[0] - 1), 0)
    wsel = lambda i, be, nu: (be[i], 0, 0)
    return pl.pallas_call(
        _ffn_kernel,
        grid_spec=pltpu.PrefetchScalarGridSpec(
            num_scalar_prefetch=2, grid=(n_slots // bm,),
            in_specs=[pl.BlockSpec((bm, d), xrow), pl.BlockSpec((None, d, ff), wsel),
                      pl.BlockSpec((None, d, ff), wsel), pl.BlockSpec((None, ff, d), wsel)],
            out_specs=pl.BlockSpec((bm, d), lambda i, be, nu: (i, 0))),
        out_shape=jax.ShapeDtypeStruct((n_slots, d), F32),
        compiler_params=_cparams(1), name="moe_ffn",
    )(blk_e, n_used, xs, wg, wu, wd)


def _combine_kernel(slot_ref, ys_ref, w_ref, x1_ref, gate_ref, lng_ref, lnb_ref, o_ref, ybuf, sem, *, tg, t):
    base = pl.program_id(0) * tg

    def issue(r, carry):
        _row_copy(ys_ref, slot_ref[base + r], ybuf.at[0], r, sem).start()
        _row_copy(ys_ref, slot_ref[t + base + r], ybuf.at[1], r, sem).start()
        return carry

    lax.fori_loop(0, tg, issue, 0)

    def drain(r, carry):
        _row_copy(ys_ref, 0, ybuf.at[0], 0, sem).wait()
        _row_copy(ys_ref, 0, ybuf.at[1], 0, sem).wait()
        return carry

    lax.fori_loop(0, tg, drain, 0)
    w = w_ref[...]
    f = ybuf[0] * w[:, 0:1] + ybuf[1] * w[:, 1:2]
    o_ref[...] = _layer_norm(DEEPNORM_ALPHA * x1_ref[...] + gate_ref[...] * f, lng_ref[...], lnb_ref[...])


def _combine(slots_flat, ys, wts_t, x1, gate, ln_g, ln_b, rows_per_mod, tg):
    t, d = x1.shape
    tiles_per_mod = rows_per_mod // tg
    nmod = gate.shape[0]
    return pl.pallas_call(
        functools.partial(_combine_kernel, tg=tg, t=t),
        grid_spec=pltpu.PrefetchScalarGridSpec(
            num_scalar_prefetch=1, grid=(t // tg,),
            in_specs=[pl.BlockSpec(memory_space=pl.ANY),
                      pl.BlockSpec((tg, 2), lambda i, s: (i, 0)),
                      pl.BlockSpec((tg, d), lambda i, s: (i, 0)),
                      pl.BlockSpec((None, 1, d), lambda i, s: (i // tiles_per_mod, 0, 0)),
                      pl.BlockSpec((1, d), lambda i, s: (0, 0)), pl.BlockSpec((1, d), lambda i, s: (0, 0))],
            out_specs=pl.BlockSpec((tg, d), lambda i, s: (i, 0)),
            scratch_shapes=[pltpu.VMEM((2, tg, d), F32), pltpu.SemaphoreType.DMA(())]),
        out_shape=jax.ShapeDtypeStruct((t, d), F32),
        compiler_params=_cparams(1), name="moe_combine",
    )(slots_flat, ys, wts_t, x1, gate.reshape(nmod, 1, d), ln_g.reshape(1, d), ln_b.reshape(1, d))


def _moe(h2, logits_t, x1, gate, ln_g, ln_b, router_bias, wg, wu, wd, rows_per_mod):
    t, d = h2.shape
    bm = MOE_BM
    e_idx, wts, rank, cnt = _route(logits_t, router_bias, tr=min(1024, t))
    counts = cnt[:, 0].astype(I32)
    pcounts = (counts + bm - 1) // bm * bm
    pends = jnp.cumsum(pcounts)
    pstarts = pends - pcounts
    n_blk = -(-(2 * t + N_EXPERTS * (bm - 1)) // bm)
    slots = (pstarts[e_idx] + rank).reshape(2 * t)
    n_used = pends[-1] // bm
    blk_ids = jnp.arange(n_blk, dtype=I32)
    blk_e = jnp.minimum(jnp.searchsorted(pends, blk_ids * bm, side="right"), N_EXPERTS - 1).astype(I32)
    blk_e = jnp.where(blk_ids < n_used, blk_e, blk_e[jnp.maximum(n_used - 1, 0)])
    xs = _dispatch(slots, h2, n_blk * bm, td=256)
    ys = _ffn(blk_e, n_used.reshape(1).astype(I32), xs, wg, wu, wd, bm)
    return _combine(slots, ys, wts.T, x1, gate, ln_g, ln_b, rows_per_mod, tg=256)


def _filter_mlp_kernel(z_ref, w1_ref, w2_ref, w3_ref, b_ref, sf_ref, o_ref):
    hp = lax.Precision.HIGHEST
    b, sf = b_ref[...], sf_ref[...]
    h = jnp.sin(sf[0:1] * (jnp.dot(z_ref[...], w1_ref[...], precision=hp, preferred_element_type=F32) + b[0:1]))
    h = jnp.sin(sf[1:2] * (jnp.dot(h, w2_ref[...], precision=hp, preferred_element_type=F32) + b[1:2]))
    o_ref[...] = jnp.sin(sf[2:3] * (jnp.dot(h, w3_ref[...], precision=hp, preferred_element_type=F32) + b[2:3]))


def _filter_mlp(zp, w_in_p, w_hid, b, sf):
    n = zp.shape[0]
    return pl.pallas_call(
        _filter_mlp_kernel, out_shape=jax.ShapeDtypeStruct((n, FILTER_HID), F32),
        compiler_params=pltpu.CompilerParams(vmem_limit_bytes=VMEM_LIMIT), name="hyena_filter_mlp",
    )(zp, w_in_p, w_hid[0], w_hid[1], b, sf)


def _filter_spec_kernel(hid_ref, wf_ref, wb_ref, dec_ref, fwd_ref, hr_ref, g_ref, nyq_ref, p_scr, q_scr, *, fb):
    k = pl.program_id(2)
    n = hid_ref.shape[0]

    @pl.when(k == 0)
    def _():
        hp = lax.Precision.HIGHEST
        hid, dec = hid_ref[...], dec_ref[...]
        row = lax.broadcasted_iota(I32, (n, 1), 0)
        fw = jnp.dot(hid, wf_ref[...], precision=hp, preferred_element_type=F32) * dec
        bw = jnp.dot(hid, wb_ref[...], precision=hp, preferred_element_type=F32) * dec
        bw = jnp.where(row == 0, 0.0, bw)
        p = fw + bw
        p_scr[...] = p.astype(BF16)
        q_scr[...] = (fw - bw).astype(BF16)
        sign = (1 - 2 * (row & 1)).astype(F32)
        nyq_ref[...] = jnp.sum(p * sign, 0, keepdims=True)

    hr_ref[...] = jnp.dot(fwd_ref[0:fb, :], p_scr[...], preferred_element_type=F32)
    gg = jnp.dot(fwd_ref[fb:2 * fb, :], q_scr[...], preferred_element_type=F32)
    rowb = lax.broadcasted_iota(I32, (fb, 1), 0)
    g_ref[...] = jnp.where((rowb == 0) & (k == 0), 0.0, gg)


def _filter_spec(hid, w_out4, decay, fwd_tab, fb, tc):
    n = hid.shape[0]
    d = D_MODEL
    kb = n // fb
    return pl.pallas_call(
        functools.partial(_filter_spec_kernel, fb=fb), grid=(2, d // tc, kb),
        in_specs=[pl.BlockSpec((n, FILTER_HID), lambda o, c, k: (0, 0)),
                  pl.BlockSpec((None, None, FILTER_HID, tc), lambda o, c, k: (0, o, 0, c)),
                  pl.BlockSpec((None, None, FILTER_HID, tc), lambda o, c, k: (1, o, 0, c)),
                  pl.BlockSpec((n, tc), lambda o, c, k: (0, c)),
                  pl.BlockSpec((None, 2 * fb, n), lambda o, c, k: (k, 0, 0))],
        out_specs=[pl.BlockSpec((None, fb, tc), lambda o, c, k: (o, k, c)),
                   pl.BlockSpec((None, fb, tc), lambda o, c, k: (o, k, c)),
                   pl.BlockSpec((None, 1, tc), lambda o, c, k: (o, 0, c))],
        out_shape=[jax.ShapeDtypeStruct((2, n, d), F32), jax.ShapeDtypeStruct((2, n, d), F32),
                   jax.ShapeDtypeStruct((2, 1, d), F32)],
        scratch_shapes=[pltpu.VMEM((n, tc), BF16), pltpu.VMEM((n, tc), BF16)],
        compiler_params=_cparams(3), name="hyena_filter_spec",
    )(hid, w_out4, w_out4, decay, fwd_tab)


def _short_conv(u_ref, cw_ref, cb_ref):
    n = u_ref.shape[0]
    u = u_ref[...].astype(F32)
    row = lax.broadcasted_iota(I32, (n, 1), 0)
    prev = jnp.where(row == 0, 0.0, pltpu.roll(u, 1, 0))
    nxt = jnp.where(row == n - 1, 0.0, pltpu.roll(u, n - 1, 0))
    cw = cw_ref[...]
    return prev * cw[0:1] + u * cw[1:2] + nxt * cw[2:3] + cb_ref[...]


def _hyena_conv_kernel(uv_ref, ux1_ref, ux2_ref, cwv_ref, cwx1_ref, cwx2_ref, cbv_ref, cbx1_ref, cbx2_ref,
                       fwd_ref, inv_ref, hr_ref, g_ref, nyq_ref, sk_ref, o_ref, zb_scr, z32_scr, acc_scr, *, fb, kb):
    o = pl.program_id(2)
    k = pl.program_id(3)

    @pl.when((o == 0) & (k == 0))
    def _():
        v = _short_conv(uv_ref, cwv_ref, cbv_ref)
        z32_scr[...] = v
        zb_scr[...] = v.astype(BF16)

    @pl.when(k == 0)
    def _():
        acc_scr[...] = jnp.zeros_like(acc_scr)

    zf = jnp.dot(fwd_ref[...], zb_scr[...], preferred_element_type=F32)
    zr, zi = zf[:fb], zf[fb:]
    hr, gg = hr_ref[...], g_ref[...]
    rowb = lax.broadcasted_iota(I32, (fb, 1), 0)
    hb = jnp.where((rowb == 0) & (k == 0), nyq_ref[...], hr)
    y = jnp.concatenate([zr * hr - zi * gg, zr * gg + zi * hb], 0).astype(BF16)
    acc_scr[...] += jnp.dot(inv_ref[...], y, preferred_element_type=F32)

    @pl.when((k == kb - 1) & (o == 0))
    def _():
        zn = _short_conv(ux1_ref, cwx1_ref, cbx1_ref) * (acc_scr[...] + z32_scr[...] * sk_ref[...])
        z32_scr[...] = zn
        zb_scr[...] = zn.astype(BF16)

    @pl.when((k == kb - 1) & (o == 1))
    def _():
        zn = _short_conv(ux2_ref, cwx2_ref, cbx2_ref) * (acc_scr[...] + z32_scr[...] * sk_ref[...])
        o_ref[...] = zn.astype(o_ref.dtype)


def _hyena_conv(u, conv_w, conv_b, fwd_tab, inv_tab, hr, gg, nyq, skip, fb, tc):
    b, n, _ = u.shape
    d = D_MODEL
    kb = n // fb
    ncb = d // tc
    ucol = lambda part: (lambda bi, c, o, k: (bi, 0, part * ncb + c))
    wcol = lambda part: (lambda bi, c, o, k: (0, part * ncb + c))
    spec = lambda bi, c, o, k: (o, k, c)
    per_o = lambda bi, c, o, k: (o, 0, c)
    return pl.pallas_call(
        functools.partial(_hyena_conv_kernel, fb=fb, kb=kb), grid=(b, ncb, 2, kb),
        in_specs=[pl.BlockSpec((None, n, tc), ucol(0)), pl.BlockSpec((None, n, tc), ucol(1)),
                  pl.BlockSpec((None, n, tc), ucol(2)),
                  pl.BlockSpec((3, tc), wcol(0)), pl.BlockSpec((3, tc), wcol(1)), pl.BlockSpec((3, tc), wcol(2)),
                  pl.BlockSpec((1, tc), wcol(0)), pl.BlockSpec((1, tc), wcol(1)), pl.BlockSpec((1, tc), wcol(2)),
                  pl.BlockSpec((None, 2 * fb, n), lambda bi, c, o, k: (k, 0, 0)),
                  pl.BlockSpec((None, n, 2 * fb), lambda bi, c, o, k: (k, 0, 0)),
                  pl.BlockSpec((None, fb, tc), spec), pl.BlockSpec((None, fb, tc), spec),
                  pl.BlockSpec((None, 1, tc), per_o), pl.BlockSpec((None, 1, tc), per_o)],
        out_specs=pl.BlockSpec((None, n, tc), lambda bi, c, o, k: (bi, 0, c)),
        out_shape=jax.ShapeDtypeStruct((b, n, d), BF16),
        scratch_shapes=[pltpu.VMEM((n, tc), BF16), pltpu.VMEM((n, tc), F32), pltpu.VMEM((n, tc), F32)],
        compiler_params=_cparams(4), name="hyena_conv",
    )(u, u, u, conv_w, conv_w, conv_w, conv_b, conv_b, conv_b, fwd_tab, inv_tab, hr, gg, nyq,
      skip.reshape(2, 1, d))


def kernel(x, c, ctx, c_ctx, mod_w, mod_b, ln_g, ln_b, attn_w_in, attn_lambda, attn_subln_g, attn_sink, attn_w_out,
           hy_w_in, hy_conv_w, hy_conv_b, hy_ffn_w_in, hy_ffn_w_hid, hy_ffn_b, hy_sin_freq, hy_ffn_w_out, hy_skip,
           hy_w_out, router_w, router_bias, exp_w_gate, exp_w_up, exp_w_down):
    b, n, d = x.shape
    nc = ctx.shape[1]
    t = b * n
    assert d == D_MODEL and b + 1 <= MOD_ROWS and n % 512 == 0 and nc % 256 == 0

    c_rows = jnp.zeros((MOD_ROWS, d), F32).at[:b].set(c).at[b].set(c_ctx)
    mods = _mods(c_rows, mod_w, mod_b)
    router_wt = router_w.T
    x2d = x.reshape(t, d)

    sh1, sc1, g1, sh2, sc2, g2 = jnp.split(mods[0], 6, axis=-1)
    cos, sin = _rope_tables(n)
    w_in = attn_w_in[0].astype(BF16)
    q_groups = tuple(range(Q_W // LANES))
    rope_groups = q_groups + tuple(range(Q_W // LANES, (Q_W + DIFF_QK_W) // LANES)) + (
        (Q_W + DIFF_QK_W + DIFF_VW) // LANES,)
    proj = _modmm(x2d, sc1[:b], sh1[:b], w_in, rows_per_mod=n, tm=512,
                  rope=(jnp.asarray(cos), jnp.asarray(sin), rope_groups, q_groups), name="attn_in_proj")
    proj_c = _modmm(ctx.reshape(b * nc, d), sc1[b:b + 1], sh1[b:b + 1], w_in[:, Q_W:], rows_per_mod=b * nc, tm=256,
                    name="ctx_in_proj")
    proj = proj.reshape(b, n, ATTN_PROJ_W)
    proj_c = proj_c.reshape(b, nc, KV_W)
    lam_init = 0.8 - 0.6 * math.exp(-0.3 * 0)
    oa = _diff_attn(proj, proj_c, attn_lambda[0], attn_subln_g[0], lam_init, tq=256)
    ow = _win_attn(proj, proj_c, attn_sink[0])
    w_out = attn_w_out[0].astype(BF16)
    x1, h2, lgt = _proj_ln([oa.reshape(t, DIFF_VW), ow.reshape(t, WIN_Q_W)], [w_out[:DIFF_VW], w_out[DIFF_VW:]],
                           x2d, g1[:b], ln_g[0, 0], ln_b[0, 0], sc2[:b], sh2[:b], router_wt, rows_per_mod=n, tm=512,
                           name="attn_out_proj_ln")
    x2 = _moe(h2, lgt, x1, g2[:b], ln_g[0, 1], ln_b[0, 1], router_bias, exp_w_gate[0].astype(BF16),
              exp_w_up[0].astype(BF16), exp_w_down[0].astype(BF16), rows_per_mod=n)

    sh1, sc1, g1, sh2, sc2, g2 = jnp.split(mods[1], 6, axis=-1)
    u = _modmm(x2, sc1[:b], sh1[:b], hy_w_in[0].astype(BF16), rows_per_mod=n, tm=512, name="hyena_in_proj")
    fb, tc = 256, 512
    fwd_np, inv_np = _dft_tables(n, fb)
    fwd_tab, inv_tab = jnp.asarray(fwd_np).astype(BF16), jnp.asarray(inv_np).astype(BF16)
    zp, decay = _filter_tables(n)
    w_in_p = jnp.zeros((FILTER_HID, FILTER_HID), F32).at[:FILTER_EMB].set(hy_ffn_w_in[0])
    hid = _filter_mlp(jnp.asarray(zp), w_in_p, hy_ffn_w_hid[0], hy_ffn_b[0], hy_sin_freq[0])
    hr, gg, nyq = _filter_spec(hid, hy_ffn_w_out[0].reshape(FILTER_HID, 2, 2, d).transpose(1, 2, 0, 3), jnp.asarray(decay), fwd_tab, fb, tc)
    z2 = _hyena_conv(u.reshape(b, n, 3 * d), hy_conv_w[0], hy_conv_b[0].reshape(1, 3 * d), fwd_tab, inv_tab,
                     hr, gg, nyq, hy_skip[0], fb, tc)
    x3, h2, lgt = _proj_ln([z2.reshape(t, d)], [hy_w_out[0].astype(BF16)], x2, g1[:b], ln_g[1, 0], ln_b[1, 0],
                           sc2[:b], sh2[:b], router_wt, rows_per_mod=n, tm=512, name="hyena_out_proj_ln")
    x4 = _moe(h2, lgt, x3, g2[:b], ln_g[1, 1], ln_b[1, 1], router_bias, exp_w_gate[1].astype(BF16),
              exp_w_up[1].astype(BF16), exp_w_down[1].astype(BF16), rows_per_mod=n)
    return x4.reshape(b, n, d)
```

```python
import functools
import math

import jax
import jax.numpy as jnp
import numpy as np
from jax import lax
from jax.experimental import pallas as pl
from jax.experimental.pallas import tpu as pltpu

F32 = jnp.float32
BF16 = jnp.bfloat16
I32 = jnp.int32

D_MODEL = 1024
DEPTH = 2
GRID_W = 64
HEAD_DIM = 64
DIFF_HEADS = 4
WIN_Q_HEADS = 8
WIN_KV_HEADS = 2
WINDOW = 128
WIN_BLOCK = 128
ROPE_BASE = 10000.0
DIFF_QK_W = DIFF_HEADS * 2 * HEAD_DIM
DIFF_VW = DIFF_HEADS * 2 * HEAD_DIM
WIN_Q_W = WIN_Q_HEADS * HEAD_DIM
WIN_KV_W = WIN_KV_HEADS * HEAD_DIM
Q_W = DIFF_QK_W + WIN_Q_W
KV_W = DIFF_QK_W + DIFF_VW + 2 * WIN_KV_W
ATTN_PROJ_W = Q_W + KV_W
FILTER_EMB = 33
FILTER_HID = 64
DECAY_TARGET = 1e-2
FAST_DECAY_PCT = 0.3
SLOW_DECAY_PCT = 1.5
N_EXPERTS = 16
N_GROUPS = 4
EXPERTS_PER_GROUP = N_EXPERTS // N_GROUPS
EXPERT_FF = 1024
LN_EPS = 1e-5
DEEPNORM_ALPHA = (2 * DEPTH) ** 0.25
NEG_INF = -1e30

LANES = 128
MOD_ROWS = 16
VMEM_LIMIT = 60 * 1024 * 1024
MOE_BM = 256
NT_DIMS = (((1,), (1,)), ((), ()))


def _cparams(n_axes):
    return pltpu.CompilerParams(dimension_semantics=("arbitrary",) * n_axes, vmem_limit_bytes=VMEM_LIMIT)


@functools.lru_cache(maxsize=None)
def _rope_tables(n):
    rows = n // GRID_W
    r, col = np.meshgrid(np.arange(rows, dtype=np.float32), np.arange(GRID_W, dtype=np.float32), indexing="ij")
    axis_dim = HEAD_DIM // 2
    inv_freq = (ROPE_BASE ** (-np.arange(0, axis_dim, 2, dtype=np.float32) / axis_dim)).astype(np.float32)
    ang = np.concatenate([r.reshape(-1, 1) * inv_freq, col.reshape(-1, 1) * inv_freq], -1)
    ang = np.concatenate([ang, ang], -1).astype(np.float32)
    cos, sin = np.cos(ang), np.sin(ang)
    half = np.arange(HEAD_DIM) < HEAD_DIM // 2
    sin_signed = np.where(half[None, :], -sin, sin)
    reps = LANES // HEAD_DIM
    return (np.tile(cos, (1, reps)).astype(np.float32), np.tile(sin_signed, (1, reps)).astype(np.float32))


@functools.lru_cache(maxsize=None)
def _dft_tables(n, fb):
    big = 2 * n
    k = np.arange(n, dtype=np.int64)[:, None]
    t = np.arange(n, dtype=np.int64)[None, :]
    ang = ((k * t) % big).astype(np.float64) * (2.0 * math.pi / big)
    c, s = np.cos(ang), np.sin(ang)
    alt = (1 - 2 * (np.arange(n) & 1)).astype(np.float64)
    s_f = s.copy()
    s_f[0, :] = alt
    kb = n // fb
    fwd = np.concatenate([c.reshape(kb, fb, n), s_f.reshape(kb, fb, n)], axis=1)
    ci = c.T * (2.0 / big)
    ci[:, 0] = 1.0 / big
    si = s.T * (2.0 / big)
    si[:, 0] = alt / big
    inv = np.concatenate([ci.reshape(n, kb, fb).transpose(1, 0, 2), si.reshape(n, kb, fb).transpose(1, 0, 2)], axis=2)
    return fwd.astype(np.float32), inv.astype(np.float32)


@functools.lru_cache(maxsize=None)
def _filter_tables(n):
    t = np.linspace(0.0, 1.0, n, dtype=np.float32)[:, None]
    bands = (FILTER_EMB - 1) // 2
    w = (2.0 * math.pi * np.arange(n, dtype=np.float32)[:, None] / n).astype(np.float32)
    fr = np.linspace(1e-4, bands - 1, bands, dtype=np.float32)[None, :]
    z = np.concatenate([t, np.cos(fr * w), -np.sin(fr * w)], -1).astype(np.float32)
    zp = np.zeros((n, FILTER_HID), np.float32)
    zp[:, :FILTER_EMB] = z
    deltas = np.abs(np.linspace(math.log(DECAY_TARGET) / SLOW_DECAY_PCT, math.log(DECAY_TARGET) / FAST_DECAY_PCT,
                                D_MODEL, dtype=np.float32))
    decay = np.exp(-t * deltas[None, :]).astype(np.float32)
    return zp, decay


def _mods_kernel(c_ref, w_ref, b_ref, o_ref):
    c = c_ref[...]
    a = (c * jax.nn.sigmoid(c)).astype(BF16)
    o_ref[...] = jnp.dot(a, w_ref[...].astype(BF16), preferred_element_type=F32) + b_ref[...]


def _mods(c_rows, mod_w, mod_b):
    d = D_MODEL
    tn = 1536
    return pl.pallas_call(
        _mods_kernel,
        grid=(DEPTH, 6 * d // tn),
        in_specs=[pl.BlockSpec((MOD_ROWS, d), lambda l, j: (0, 0)),
                  pl.BlockSpec((None, d, tn), lambda l, j: (l, 0, j)),
                  pl.BlockSpec((None, 1, tn), lambda l, j: (l, 0, j))],
        out_specs=pl.BlockSpec((None, MOD_ROWS, tn), lambda l, j: (l, 0, j)),
        out_shape=jax.ShapeDtypeStruct((DEPTH, MOD_ROWS, 6 * d), F32),
        compiler_params=_cparams(2), name="mods",
    )(c_rows, mod_w, mod_b.reshape(DEPTH, 1, 6 * d))


def _modmm_kernel(*refs, n_groups, rope_groups, scaled_groups, chunk):
    if rope_groups:
        x_ref, sc_ref, sh_ref, w_ref, cos_ref, sin_ref, o_ref = refs
        cos, sin = cos_ref[...], sin_ref[...]
        lane = lax.broadcasted_iota(I32, (1, LANES), 1)
        first_half = (lane % HEAD_DIM) < HEAD_DIM // 2
    else:
        x_ref, sc_ref, sh_ref, w_ref, o_ref = refs
    h = (x_ref[...] * (1.0 + sc_ref[...]) + sh_ref[...]).astype(BF16)
    gpc = chunk // LANES
    for c in range(n_groups // gpc):
        acc = jnp.dot(h, w_ref[:, c * chunk:(c + 1) * chunk], preferred_element_type=F32)
        for j in range(gpc):
            g = c * gpc + j
            blk = acc[:, j * LANES:(j + 1) * LANES]
            if g in rope_groups:
                rot = jnp.where(first_half, pltpu.roll(blk, LANES - HEAD_DIM // 2, 1), pltpu.roll(blk, HEAD_DIM // 2, 1))
                blk = blk * cos + rot * sin
                if g in scaled_groups:
                    blk = blk * (HEAD_DIM ** -0.5)
            o_ref[:, g * LANES:(g + 1) * LANES] = blk.astype(o_ref.dtype)


def _modmm(x2d, sc, sh, w, rows_per_mod, tm, rope=None, name="modmm"):
    t, d = x2d.shape
    n = w.shape[1]
    tiles_per_mod = rows_per_mod // tm
    nmod = sc.shape[0]
    in_specs = [pl.BlockSpec((tm, d), lambda i: (i, 0)),
                pl.BlockSpec((None, 1, d), lambda i: (i // tiles_per_mod, 0, 0)),
                pl.BlockSpec((None, 1, d), lambda i: (i // tiles_per_mod, 0, 0)),
                pl.BlockSpec((d, n), lambda i: (0, 0))]
    args = [x2d, sc.reshape(nmod, 1, d), sh.reshape(nmod, 1, d), w]
    rope_groups, scaled_groups = (), ()
    if rope is not None:
        cos, sin, rope_groups, scaled_groups = rope
        in_specs += [pl.BlockSpec((tm, LANES), lambda i: (i % tiles_per_mod, 0)),
                     pl.BlockSpec((tm, LANES), lambda i: (i % tiles_per_mod, 0))]
        args += [cos, sin]
    kern = functools.partial(_modmm_kernel, n_groups=n // LANES, rope_groups=tuple(rope_groups),
                             scaled_groups=tuple(scaled_groups), chunk=256)
    return pl.pallas_call(
        kern, grid=(t // tm,), in_specs=in_specs,
        out_specs=pl.BlockSpec((tm, n), lambda i: (i, 0)),
        out_shape=jax.ShapeDtypeStruct((t, n), BF16),
        compiler_params=_cparams(1), name=name,
    )(*args)


def _diff_attn_kernel(q_ref, k_ref, v_ref, kc_ref, vc_ref, lam_ref, g_ref, o_ref, *, lam_init):
    q = q_ref[...]
    lane = lax.broadcasted_iota(I32, (1, LANES), 1)
    zero = jnp.zeros_like(q)
    k, kc = k_ref[...], kc_ref[...]

    def probs(qm):
        sl = lax.dot_general(qm, k, NT_DIMS, preferred_element_type=F32)
        sc = lax.dot_general(qm, kc, NT_DIMS, preferred_element_type=F32)
        m = jnp.maximum(jnp.max(sl, -1, keepdims=True), jnp.max(sc, -1, keepdims=True))
        pl_, pc = jnp.exp(sl - m), jnp.exp(sc - m)
        den = jnp.sum(pl_, -1, keepdims=True) + jnp.sum(pc, -1, keepdims=True)
        return pl_, pc, 1.0 / den

    lv = lam_ref[...]
    lam = (jnp.exp(jnp.sum(lv[0:1] * lv[1:2], keepdims=True)) - jnp.exp(jnp.sum(lv[2:3] * lv[3:4], keepdims=True))
           + lam_init)
    p1l, p1c, r1 = probs(jnp.where(lane < HEAD_DIM, q, zero))
    p2l, p2c, r2 = probs(jnp.where(lane >= HEAD_DIM, q, zero))
    w2 = lam * r2
    al = (p1l * r1 - p2l * w2).astype(BF16)
    ac = (p1c * r1 - p2c * w2).astype(BF16)
    o = (jnp.dot(al, v_ref[...], preferred_element_type=F32) + jnp.dot(ac, vc_ref[...], preferred_element_type=F32))
    ms = jnp.mean(o * o, -1, keepdims=True)
    o_ref[...] = (o * lax.rsqrt(ms + LN_EPS) * g_ref[...] * (1.0 - lam_init)).astype(o_ref.dtype)


def _diff_attn(proj, proj_c, lam_vec, subln_g, lam_init, tq):
    b, n, _ = proj.shape
    nc = proj_c.shape[1]
    kcol = Q_W // LANES
    vcol = (Q_W + DIFF_QK_W) // LANES
    vccol = DIFF_QK_W // LANES
    kern = functools.partial(_diff_attn_kernel, lam_init=lam_init)
    return pl.pallas_call(
        kern, grid=(b, DIFF_HEADS, n // tq),
        in_specs=[pl.BlockSpec((None, tq, LANES), lambda bi, h, i: (bi, i, h)),
                  pl.BlockSpec((None, n, LANES), lambda bi, h, i: (bi, 0, kcol + h)),
                  pl.BlockSpec((None, n, LANES), lambda bi, h, i: (bi, 0, vcol + h)),
                  pl.BlockSpec((None, nc, LANES), lambda bi, h, i: (bi, 0, h)),
                  pl.BlockSpec((None, nc, LANES), lambda bi, h, i: (bi, 0, vccol + h)),
                  pl.BlockSpec((4, HEAD_DIM), lambda bi, h, i: (0, 0)),
                  pl.BlockSpec((1, LANES), lambda bi, h, i: (0, 0))],
        out_specs=pl.BlockSpec((None, tq, LANES), lambda bi, h, i: (bi, i, h)),
        out_shape=jax.ShapeDtypeStruct((b, n, DIFF_VW), BF16),
        compiler_params=_cparams(3), name="diff_attn",
    )(proj, proj, proj, proj_c, proj_c, lam_vec, subln_g.reshape(1, LANES))


def _win_attn_kernel(sink_ref, q_ref, kp_ref, kn_ref, kx_ref, vp_ref, vn_ref, vx_ref, kc_ref, vc_ref, o_ref, *, seq):
    g = pl.program_id(1)
    n = pl.program_id(2)
    wb = WIN_BLOCK
    gq = WIN_Q_HEADS // WIN_KV_HEADS
    lane = lax.broadcasted_iota(I32, (1, LANES), 1)
    in_g = (lane // HEAD_DIM) == g
    q = q_ref[...].astype(F32)
    parts = []
    for j in range(gq):
        x = q[:, (j // 2) * LANES:(j // 2 + 1) * LANES]
        x = jnp.where(g == (j % 2), x, pltpu.roll(x, HEAD_DIM, 1))
        parts.append(jnp.where(in_g, x, 0.0))
    qs = jnp.concatenate(parts, 0).astype(BF16)
    k_loc = jnp.concatenate([kp_ref[...], kn_ref[...], kx_ref[...]], 0)
    v_loc = jnp.concatenate([vp_ref[...], vn_ref[...], vx_ref[...]], 0)
    s_loc = lax.dot_general(qs, k_loc, NT_DIMS, preferred_element_type=F32)
    s_ctx = lax.dot_general(qs, kc_ref[...], NT_DIMS, preferred_element_type=F32)
    row = lax.broadcasted_iota(I32, (gq * wb, 1), 0)
    q_off = row % wb + wb
    k_off = lax.broadcasted_iota(I32, (1, 3 * wb), 1)
    k_abs = (n - 1) * wb + k_off
    allowed = (jnp.abs(q_off - k_off) <= WINDOW) & (k_abs >= 0) & (k_abs < seq)
    s_loc = jnp.where(allowed, s_loc, NEG_INF)
    head = row // wb
    sk = jnp.zeros((gq * wb, 1), F32)
    for j in range(gq):
        sk = jnp.where(head == j, sink_ref[g * gq + j], sk)
    m = jnp.maximum(jnp.maximum(jnp.max(s_loc, -1, keepdims=True), jnp.max(s_ctx, -1, keepdims=True)), sk)
    p_loc, p_ctx = jnp.exp(s_loc - m), jnp.exp(s_ctx - m)
    den = jnp.sum(p_loc, -1, keepdims=True) + jnp.sum(p_ctx, -1, keepdims=True) + jnp.exp(sk - m)
    o = (jnp.dot(p_loc.astype(BF16), v_loc, preferred_element_type=F32)
         + jnp.dot(p_ctx.astype(BF16), vc_ref[...], preferred_element_type=F32)) * (1.0 / den)
    for cb in range(gq // 2):
        a = o[(2 * cb) * wb:(2 * cb + 1) * wb]
        bb = o[(2 * cb + 1) * wb:(2 * cb + 2) * wb]
        a = jnp.where(g == 0, a, pltpu.roll(a, HEAD_DIM, 1))
        bb = jnp.where(g == 1, bb, pltpu.roll(bb, HEAD_DIM, 1))
        o_ref[:, cb * LANES:(cb + 1) * LANES] = jnp.where(lane < HEAD_DIM, a, bb).astype(o_ref.dtype)


def _win_attn(proj, proj_c, sink):
    b, n, _ = proj.shape
    nc = proj_c.shape[1]
    nb = n // WIN_BLOCK
    gw = (WIN_Q_HEADS // WIN_KV_HEADS) * HEAD_DIM
    qcol = DIFF_QK_W // gw
    kcol = (Q_W + DIFF_QK_W + DIFF_VW) // LANES
    vcol = kcol + 1
    kccol = (DIFF_QK_W + DIFF_VW) // LANES
    prev = lambda bi, g, i: (bi, jnp.maximum(i - 1, 0), kcol)
    cur = lambda bi, g, i: (bi, i, kcol)
    nxt = lambda bi, g, i: (bi, jnp.minimum(i + 1, nb - 1), kcol)
    vprev = lambda bi, g, i: (bi, jnp.maximum(i - 1, 0), vcol)
    vcur = lambda bi, g, i: (bi, i, vcol)
    vnxt = lambda bi, g, i: (bi, jnp.minimum(i + 1, nb - 1), vcol)
    blk = (None, WIN_BLOCK, LANES)
    kern = functools.partial(_win_attn_kernel, seq=n)
    return pl.pallas_call(
        kern, grid=(b, WIN_KV_HEADS, nb),
        in_specs=[pl.BlockSpec(memory_space=pltpu.SMEM),
                  pl.BlockSpec((None, WIN_BLOCK, gw), lambda bi, g, i: (bi, i, qcol + g)),
                  pl.BlockSpec(blk, prev), pl.BlockSpec(blk, cur), pl.BlockSpec(blk, nxt),
                  pl.BlockSpec(blk, vprev), pl.BlockSpec(blk, vcur), pl.BlockSpec(blk, vnxt),
                  pl.BlockSpec((None, nc, LANES), lambda bi, g, i: (bi, 0, kccol)),
                  pl.BlockSpec((None, nc, LANES), lambda bi, g, i: (bi, 0, kccol + 1))],
        out_specs=pl.BlockSpec((None, WIN_BLOCK, gw), lambda bi, g, i: (bi, i, g)),
        out_shape=jax.ShapeDtypeStruct((b, n, WIN_Q_W), BF16),
        compiler_params=_cparams(3), name="win_attn",
    )(sink, proj, proj, proj, proj, proj, proj, proj, proj_c, proj_c)


def _layer_norm(r, g, b):
    mu = jnp.mean(r, -1, keepdims=True)
    dlt = r - mu
    var = jnp.mean(dlt * dlt, -1, keepdims=True)
    return dlt * lax.rsqrt(var + LN_EPS) * g + b


def _proj_ln_kernel(*refs, n_in):
    a_refs = refs[:n_in]
    w_refs = refs[n_in:2 * n_in]
    x_ref, gate_ref, lng_ref, lnb_ref, sc_ref, sh_ref, rw_ref, x1_ref, h2_ref, lg_ref = refs[2 * n_in:]
    y = jnp.dot(a_refs[0][...], w_refs[0][...], preferred_element_type=F32)
    for a_ref, w_ref in zip(a_refs[1:], w_refs[1:]):
        y = y + jnp.dot(a_ref[...], w_ref[...], preferred_element_type=F32)
    xn = _layer_norm(DEEPNORM_ALPHA * x_ref[...] + gate_ref[...] * y, lng_ref[...], lnb_ref[...])
    x1_ref[...] = xn
    h2 = xn * (1.0 + sc_ref[...]) + sh_ref[...]
    h2_ref[...] = h2
    lg_ref[...] = lax.dot_general(rw_ref[...], h2, NT_DIMS, precision=lax.Precision.HIGHEST,
                                  preferred_element_type=F32)


def _proj_ln(acts, ws, x2d, gate, ln_g, ln_b, sc, sh, router_wt, rows_per_mod, tm, name):
    t, d = x2d.shape
    n_in = len(acts)
    tiles_per_mod = rows_per_mod // tm
    nmod = gate.shape[0]
    row = lambda i: (i, 0)
    full = lambda i: (0, 0)
    mod = lambda i: (i // tiles_per_mod, 0, 0)
    in_specs = ([pl.BlockSpec((tm, a.shape[1]), row) for a in acts]
                + [pl.BlockSpec(w.shape, full) for w in ws]
                + [pl.BlockSpec((tm, d), row), pl.BlockSpec((None, 1, d), mod),
                   pl.BlockSpec((1, d), full), pl.BlockSpec((1, d), full),
                   pl.BlockSpec((None, 1, d), mod), pl.BlockSpec((None, 1, d), mod),
                   pl.BlockSpec((N_EXPERTS, d), full)])
    return pl.pallas_call(
        functools.partial(_proj_ln_kernel, n_in=n_in), grid=(t // tm,), in_specs=in_specs,
        out_specs=[pl.BlockSpec((tm, d), row), pl.BlockSpec((tm, d), row),
                   pl.BlockSpec((N_EXPERTS, tm), lambda i: (0, i))],
        out_shape=[jax.ShapeDtypeStruct((t, d), F32), jax.ShapeDtypeStruct((t, d), F32),
                   jax.ShapeDtypeStruct((N_EXPERTS, t), F32)],
        compiler_params=_cparams(1), name=name,
    )(*acts, *ws, x2d, gate.reshape(nmod, 1, d), ln_g.reshape(1, d), ln_b.reshape(1, d),
      sc.reshape(nmod, 1, d), sh.reshape(nmod, 1, d), router_wt)


def _first_argmax(vals):
    idx = jnp.zeros(vals[0].shape, I32)
    best = vals[0]
    for j in range(1, len(vals)):
        upd = vals[j] > best
        idx = jnp.where(upd, j, idx)
        best = jnp.where(upd, vals[j], best)
    return idx, best


def _route_kernel(lg_ref, bias_ref, e_ref, w_ref, rank_ref, cnt_ref, carry_ref, *, tr):
    @pl.when(pl.program_id(0) == 0)
    def _():
        carry_ref[...] = jnp.zeros_like(carry_ref)

    lg = lg_ref[...]
    ex = jnp.exp(lg - jnp.max(lg, 0, keepdims=True))
    scores = ex / jnp.sum(ex, 0, keepdims=True)
    sel = scores + bias_ref[...]
    rows = [sel[e:e + 1] for e in range(N_EXPERTS)]
    group_scores = []
    for g in range(N_GROUPS):
        r = rows[g * EXPERTS_PER_GROUP:(g + 1) * EXPERTS_PER_GROUP]
        best = None
        for i in range(EXPERTS_PER_GROUP):
            for j in range(i + 1, EXPERTS_PER_GROUP):
                s = r[i] + r[j]
                best = s if best is None else jnp.maximum(best, s)
        group_scores.append(best)
    grp, _ = _first_argmax(group_scores)
    vals = []
    for j in range(EXPERTS_PER_GROUP):
        v = rows[(N_GROUPS - 1) * EXPERTS_PER_GROUP + j]
        for g in range(N_GROUPS - 2, -1, -1):
            v = jnp.where(grp == g, rows[g * EXPERTS_PER_GROUP + j], v)
        vals.append(v)
    i0, _ = _first_argmax(vals)
    i1, _ = _first_argmax([jnp.where(i0 == j, -jnp.inf, vals[j]) for j in range(EXPERTS_PER_GROUP)])
    e0 = grp * EXPERTS_PER_GROUP + i0
    e1 = grp * EXPERTS_PER_GROUP + i1
    eid = lax.broadcasted_iota(I32, (N_EXPERTS, 1), 0)
    oh0, oh1 = eid == e0, eid == e1
    s0 = jnp.sum(jnp.where(oh0, scores, 0.0), 0, keepdims=True)
    s1 = jnp.sum(jnp.where(oh1, scores, 0.0), 0, keepdims=True)
    den = s0 + s1
    member = jnp.where(oh0 | oh1, 1.0, 0.0)
    before = lax.broadcasted_iota(I32, (tr, tr), 0) < lax.broadcasted_iota(I32, (tr, tr), 1)
    upper = jnp.where(before, 1.0, 0.0).astype(BF16)
    cnt = jnp.dot(member.astype(BF16), upper, preferred_element_type=F32) + carry_ref[:, 0:1]
    r0 = jnp.sum(jnp.where(oh0, cnt, 0.0), 0, keepdims=True)
    r1 = jnp.sum(jnp.where(oh1, cnt, 0.0), 0, keepdims=True)
    carry_ref[...] = carry_ref[...] + jnp.sum(member, 1, keepdims=True)
    e_ref[0:1, :] = e0
    e_ref[1:2, :] = e1
    w_ref[0:1, :] = s0 / den
    w_ref[1:2, :] = s1 / den
    rank_ref[0:1, :] = r0.astype(I32)
    rank_ref[1:2, :] = r1.astype(I32)
    cnt_ref[...] = carry_ref[...]


def _route(logits_t, router_bias, tr):
    t = logits_t.shape[1]
    tok = lambda i: (0, i)
    return pl.pallas_call(
        functools.partial(_route_kernel, tr=tr), grid=(t // tr,),
        in_specs=[pl.BlockSpec((N_EXPERTS, tr), tok), pl.BlockSpec((N_EXPERTS, 1), lambda i: (0, 0))],
        out_specs=[pl.BlockSpec((2, tr), tok), pl.BlockSpec((2, tr), tok), pl.BlockSpec((2, tr), tok),
                   pl.BlockSpec((N_EXPERTS, LANES), lambda i: (0, 0))],
        out_shape=[jax.ShapeDtypeStruct((2, t), I32), jax.ShapeDtypeStruct((2, t), F32),
                   jax.ShapeDtypeStruct((2, t), I32), jax.ShapeDtypeStruct((N_EXPERTS, LANES), F32)],
        scratch_shapes=[pltpu.VMEM((N_EXPERTS, LANES), F32)],
        compiler_params=_cparams(1), name="route",
    )(logits_t, router_bias.reshape(N_EXPERTS, 1))


def _row_copy(src_ref, src_row, dst_ref, dst_row, sem):
    return pltpu.make_async_copy(src_ref.at[pl.ds(src_row, 1)], dst_ref.at[pl.ds(dst_row, 1)], sem)


def _dispatch_kernel(slot_ref, h_ref, xs_in_ref, xs_ref, sem, *, td, t):
    del xs_in_ref
    base = pl.program_id(0) * td

    def issue(r, carry):
        _row_copy(h_ref, r, xs_ref, slot_ref[base + r], sem).start()
        _row_copy(h_ref, r, xs_ref, slot_ref[t + base + r], sem).start()
        return carry

    lax.fori_loop(0, td, issue, 0)

    for _ in range(2):
        pltpu.make_async_copy(h_ref, xs_ref.at[pl.ds(0, td)], sem).wait()


def _dispatch(slots_flat, h2, n_slots, td):
    t, d = h2.shape
    xs0 = jnp.zeros((n_slots, d), F32)
    return pl.pallas_call(
        functools.partial(_dispatch_kernel, td=td, t=t),
        grid_spec=pltpu.PrefetchScalarGridSpec(
            num_scalar_prefetch=1, grid=(t // td,),
            in_specs=[pl.BlockSpec((td, d), lambda i, s: (i, 0)), pl.BlockSpec(memory_space=pl.ANY)],
            out_specs=pl.BlockSpec(memory_space=pl.ANY),
            scratch_shapes=[pltpu.SemaphoreType.DMA(())]),
        out_shape=jax.ShapeDtypeStruct((n_slots, d), F32),
        input_output_aliases={2: 0},
        compiler_params=_cparams(1), name="moe_dispatch",
    )(slots_flat, h2, xs0)


def _ffn_kernel(blk_e_ref, nused_ref, x_ref, wg_ref, wu_ref, wd_ref, o_ref, wg_bf, wu_bf, wd_bf):
    i = pl.program_id(0)
    new_expert = (i == 0) | (blk_e_ref[i] != blk_e_ref[jnp.maximum(i - 1, 0)])

    @pl.when(new_expert)
    def _():
        wg_bf[...] = wg_ref[...].astype(BF16)
        wu_bf[...] = wu_ref[...].astype(BF16)
        wd_bf[...] = wd_ref[...].astype(BF16)

    @pl.when(i < nused_ref[0])
    def _():
        x = x_ref[...].astype(BF16)
        gate = jnp.dot(x, wg_bf[...], preferred_element_type=F32)
        up = jnp.dot(x, wu_bf[...], preferred_element_type=F32)
        act = (gate * jax.nn.sigmoid(gate) * up).astype(BF16)
        o_ref[...] = jnp.dot(act, wd_bf[...], preferred_element_type=F32)

    @pl.when(i >= nused_ref[0])
    def _():
        o_ref[...] = jnp.zeros_like(o_ref)


def _ffn(blk_e, n_used, xs, wg, wu, wd, layer, bm):
    n_slots, d = xs.shape
    ff = wg.shape[3]
    xrow = lambda i, be, nu: (jnp.minimum(i, nu[0] - 1), 0)
    wsel = lambda i, be, nu: (layer, be[i], 0, 0)
    return pl.pallas_call(
        _ffn_kernel,
        grid_spec=pltpu.PrefetchScalarGridSpec(
            num_scalar_prefetch=2, grid=(n_slots // bm,),
            in_specs=[pl.BlockSpec((bm, d), xrow), pl.BlockSpec((None, None, d, ff), wsel),
                      pl.BlockSpec((None, None, d, ff), wsel), pl.BlockSpec((None, None, ff, d), wsel)],
            out_specs=pl.BlockSpec((bm, d), lambda i, be, nu: (i, 0)),
            scratch_shapes=[pltpu.VMEM((d, ff), BF16), pltpu.VMEM((d, ff), BF16), pltpu.VMEM((ff, d), BF16)]),
        out_shape=jax.ShapeDtypeStruct((n_slots, d), F32),
        compiler_params=_cparams(1), name="moe_ffn",
    )(blk_e, n_used, xs, wg, wu, wd)


def _combine_kernel(slot_ref, ys_ref, w_ref, x1_ref, gate_ref, lng_ref, lnb_ref, o_ref, ybuf, sem, *, tg, t):
    base = pl.program_id(0) * tg

    def issue(r, carry):
        _row_copy(ys_ref, slot_ref[base + r], ybuf.at[0], r, sem).start()
        _row_copy(ys_ref, slot_ref[t + base + r], ybuf.at[1], r, sem).start()
        return carry

    lax.fori_loop(0, tg, issue, 0)

    for k in range(2):
        pltpu.make_async_copy(ys_ref.at[pl.ds(0, tg)], ybuf.at[k], sem).wait()
    w = w_ref[...]
    f = ybuf[0] * w[:, 0:1] + ybuf[1] * w[:, 1:2]
    o_ref[...] = _layer_norm(DEEPNORM_ALPHA * x1_ref[...] + gate_ref[...] * f, lng_ref[...], lnb_ref[...])


def _combine(slots_flat, ys, wts_t, x1, gate, ln_g, ln_b, rows_per_mod, tg):
    t, d = x1.shape
    tiles_per_mod = rows_per_mod // tg
    nmod = gate.shape[0]
    return pl.pallas_call(
        functools.partial(_combine_kernel, tg=tg, t=t),
        grid_spec=pltpu.PrefetchScalarGridSpec(
            num_scalar_prefetch=1, grid=(t // tg,),
            in_specs=[pl.BlockSpec(memory_space=pl.ANY),
                      pl.BlockSpec((tg, 2), lambda i, s: (i, 0)),
                      pl.BlockSpec((tg, d), lambda i, s: (i, 0)),
                      pl.BlockSpec((None, 1, d), lambda i, s: (i // tiles_per_mod, 0, 0)),
                      pl.BlockSpec((1, d), lambda i, s: (0, 0)), pl.BlockSpec((1, d), lambda i, s: (0, 0))],
            out_specs=pl.BlockSpec((tg, d), lambda i, s: (i, 0)),
            scratch_shapes=[pltpu.VMEM((2, tg, d), F32), pltpu.SemaphoreType.DMA(())]),
        out_shape=jax.ShapeDtypeStruct((t, d), F32),
        compiler_params=_cparams(1), name="moe_combine",
    )(slots_flat, ys, wts_t, x1, gate.reshape(nmod, 1, d), ln_g.reshape(1, d), ln_b.reshape(1, d))


def _moe(h2, logits_t, x1, gate, ln_g, ln_b, router_bias, wg, wu, wd, layer, rows_per_mod):
    t, d = h2.shape
    bm = MOE_BM
    e_idx, wts, rank, cnt = _route(logits_t, router_bias, tr=min(1024, t))
    counts = cnt[:, 0].astype(I32)
    pcounts = (counts + bm - 1) // bm * bm
    pends = jnp.cumsum(pcounts)
    pstarts = pends - pcounts
    n_blk = -(-(2 * t + N_EXPERTS * (bm - 1)) // bm)
    eids = jnp.arange(N_EXPERTS, dtype=I32)
    start_of = jnp.sum(jnp.where(e_idx[None] == eids[:, None, None], pstarts[:, None, None], 0), 0)
    slots = (start_of + rank).reshape(2 * t)
    n_used = pends[-1] // bm
    blk_ids = jnp.minimum(jnp.arange(n_blk, dtype=I32), n_used - 1)
    blk_e = jnp.minimum(jnp.sum((blk_ids[:, None] * bm >= pends[None, :]).astype(I32), 1), N_EXPERTS - 1)
    xs = _dispatch(slots, h2, n_blk * bm, td=256)
    ys = _ffn(blk_e, n_used.reshape(1).astype(I32), xs, wg, wu, wd, layer, bm)
    return _combine(slots, ys, wts.T, x1, gate, ln_g, ln_b, rows_per_mod, tg=256)


def _filter_mlp_kernel(z_ref, w1_ref, w2_ref, w3_ref, b_ref, sf_ref, o_ref):
    hp = lax.Precision.HIGHEST
    b, sf = b_ref[...], sf_ref[...]
    h = jnp.sin(sf[0:1] * (jnp.dot(z_ref[...], w1_ref[...], precision=hp, preferred_element_type=F32) + b[0:1]))
    h = jnp.sin(sf[1:2] * (jnp.dot(h, w2_ref[...], precision=hp, preferred_element_type=F32) + b[1:2]))
    o_ref[...] = jnp.sin(sf[2:3] * (jnp.dot(h, w3_ref[...], precision=hp, preferred_element_type=F32) + b[2:3]))


def _filter_mlp(zp, w_in_p, w_hid, b, sf):
    n = zp.shape[0]
    return pl.pallas_call(
        _filter_mlp_kernel, out_shape=jax.ShapeDtypeStruct((n, FILTER_HID), F32),
        compiler_params=pltpu.CompilerParams(vmem_limit_bytes=VMEM_LIMIT), name="hyena_filter_mlp",
    )(zp, w_in_p, w_hid[0], w_hid[1], b, sf)


def _filter_spec_kernel(hid_ref, wf_ref, wb_ref, dec_ref, fwd_ref, hr_ref, g_ref, nyq_ref, p_scr, q_scr, *, fb):
    k = pl.program_id(2)
    n = hid_ref.shape[0]

    @pl.when(k == 0)
    def _():
        hp = lax.Precision.HIGHEST
        hid, dec = hid_ref[...], dec_ref[...]
        row = lax.broadcasted_iota(I32, (n, 1), 0)
        fw = jnp.dot(hid, wf_ref[...], precision=hp, preferred_element_type=F32) * dec
        bw = jnp.dot(hid, wb_ref[...], precision=hp, preferred_element_type=F32) * dec
        bw = jnp.where(row == 0, 0.0, bw)
        p = fw + bw
        p_scr[...] = p.astype(BF16)
        q_scr[...] = (fw - bw).astype(BF16)
        sign = (1 - 2 * (row & 1)).astype(F32)
        nyq_ref[...] = jnp.sum(p * sign, 0, keepdims=True)

    hr_ref[...] = jnp.dot(fwd_ref[0:fb, :], p_scr[...], preferred_element_type=F32)
    gg = jnp.dot(fwd_ref[fb:2 * fb, :], q_scr[...], preferred_element_type=F32)
    rowb = lax.broadcasted_iota(I32, (fb, 1), 0)
    g_ref[...] = jnp.where((rowb == 0) & (k == 0), 0.0, gg)


def _filter_spec(hid, w_out4, decay, fwd_tab, fb, tc):
    n = hid.shape[0]
    d = D_MODEL
    kb = n // fb
    return pl.pallas_call(
        functools.partial(_filter_spec_kernel, fb=fb), grid=(2, d // tc, kb),
        in_specs=[pl.BlockSpec((n, FILTER_HID), lambda o, c, k: (0, 0)),
                  pl.BlockSpec((None, None, FILTER_HID, tc), lambda o, c, k: (0, o, 0, c)),
                  pl.BlockSpec((None, None, FILTER_HID, tc), lambda o, c, k: (1, o, 0, c)),
                  pl.BlockSpec((n, tc), lambda o, c, k: (0, c)),
                  pl.BlockSpec((None, 2 * fb, n), lambda o, c, k: (k, 0, 0))],
        out_specs=[pl.BlockSpec((None, fb, tc), lambda o, c, k: (o, k, c)),
                   pl.BlockSpec((None, fb, tc), lambda o, c, k: (o, k, c)),
                   pl.BlockSpec((None, 1, tc), lambda o, c, k: (o, 0, c))],
        out_shape=[jax.ShapeDtypeStruct((2, n, d), F32), jax.ShapeDtypeStruct((2, n, d), F32),
                   jax.ShapeDtypeStruct((2, 1, d), F32)],
        scratch_shapes=[pltpu.VMEM((n, tc), BF16), pltpu.VMEM((n, tc), BF16)],
        compiler_params=_cparams(3), name="hyena_filter_spec",
    )(hid, w_out4, w_out4, decay, fwd_tab)


def _short_conv(u_ref, cw_ref, cb_ref):
    n = u_ref.shape[0]
    u = u_ref[...].astype(F32)
    row = lax.broadcasted_iota(I32, (n, 1), 0)
    prev = jnp.where(row == 0, 0.0, pltpu.roll(u, 1, 0))
    nxt = jnp.where(row == n - 1, 0.0, pltpu.roll(u, n - 1, 0))
    cw = cw_ref[...]
    return prev * cw[0:1] + u * cw[1:2] + nxt * cw[2:3] + cb_ref[...]


def _hyena_conv_kernel(uv_ref, ux1_ref, ux2_ref, cwv_ref, cwx1_ref, cwx2_ref, cbv_ref, cbx1_ref, cbx2_ref,
                       fwd_ref, inv_ref, hr_ref, g_ref, nyq_ref, sk_ref, o_ref, zb_scr, z32_scr, acc_scr, *, fb, kb):
    o = pl.program_id(2)
    k = pl.program_id(3)

    @pl.when((o == 0) & (k == 0))
    def _():
        v = _short_conv(uv_ref, cwv_ref, cbv_ref)
        z32_scr[...] = v
        zb_scr[...] = v.astype(BF16)

    @pl.when(k == 0)
    def _():
        acc_scr[...] = jnp.zeros_like(acc_scr)

    zf = jnp.dot(fwd_ref[...], zb_scr[...], preferred_element_type=F32)
    zr, zi = zf[:fb], zf[fb:]
    hr, gg = hr_ref[...], g_ref[...]
    rowb = lax.broadcasted_iota(I32, (fb, 1), 0)
    hb = jnp.where((rowb == 0) & (k == 0), nyq_ref[...], hr)
    y = jnp.concatenate([zr * hr - zi * gg, zr * gg + zi * hb], 0).astype(BF16)
    acc_scr[...] += jnp.dot(inv_ref[...], y, preferred_element_type=F32)

    @pl.when((k == kb - 1) & (o == 0))
    def _():
        zn = _short_conv(ux1_ref, cwx1_ref, cbx1_ref) * (acc_scr[...] + z32_scr[...] * sk_ref[...])
        z32_scr[...] = zn
        zb_scr[...] = zn.astype(BF16)

    @pl.when((k == kb - 1) & (o == 1))
    def _():
        zn = _short_conv(ux2_ref, cwx2_ref, cbx2_ref) * (acc_scr[...] + z32_scr[...] * sk_ref[...])
        o_ref[...] = zn.astype(o_ref.dtype)


def _hyena_conv(u, conv_w, conv_b, fwd_tab, inv_tab, hr, gg, nyq, skip, fb, tc):
    b, n, _ = u.shape
    d = D_MODEL
    kb = n // fb
    ncb = d // tc
    ucol = lambda part: (lambda bi, c, o, k: (bi, 0, part * ncb + c))
    wcol = lambda part: (lambda bi, c, o, k: (0, part * ncb + c))
    spec = lambda bi, c, o, k: (o, k, c)
    per_o = lambda bi, c, o, k: (o, 0, c)
    return pl.pallas_call(
        functools.partial(_hyena_conv_kernel, fb=fb, kb=kb), grid=(b, ncb, 2, kb),
        in_specs=[pl.BlockSpec((None, n, tc), ucol(0)), pl.BlockSpec((None, n, tc), ucol(1)),
                  pl.BlockSpec((None, n, tc), ucol(2)),
                  pl.BlockSpec((3, tc), wcol(0)), pl.BlockSpec((3, tc), wcol(1)), pl.BlockSpec((3, tc), wcol(2)),
                  pl.BlockSpec((1, tc), wcol(0)), pl.BlockSpec((1, tc), wcol(1)), pl.BlockSpec((1, tc), wcol(2)),
                  pl.BlockSpec((None, 2 * fb, n), lambda bi, c, o, k: (k, 0, 0)),
                  pl.BlockSpec((None, n, 2 * fb), lambda bi, c, o, k: (k, 0, 0)),
                  pl.BlockSpec((None, fb, tc), spec), pl.BlockSpec((None, fb, tc), spec),
                  pl.BlockSpec((None, 1, tc), per_o), pl.BlockSpec((None, 1, tc), per_o)],
        out_specs=pl.BlockSpec((None, n, tc), lambda bi, c, o, k: (bi, 0, c)),
        out_shape=jax.ShapeDtypeStruct((b, n, d), BF16),
        scratch_shapes=[pltpu.VMEM((n, tc), BF16), pltpu.VMEM((n, tc), F32), pltpu.VMEM((n, tc), F32)],
        compiler_params=_cparams(4), name="hyena_conv",
    )(u, u, u, conv_w, conv_w, conv_w, conv_b, conv_b, conv_b, fwd_tab, inv_tab, hr, gg, nyq,
      skip.reshape(2, 1, d))


def kernel(x, c, ctx, c_ctx, mod_w, mod_b, ln_g, ln_b, attn_w_in, attn_lambda, attn_subln_g, attn_sink, attn_w_out,
           hy_w_in, hy_conv_w, hy_conv_b, hy_ffn_w_in, hy_ffn_w_hid, hy_ffn_b, hy_sin_freq, hy_ffn_w_out, hy_skip,
           hy_w_out, router_w, router_bias, exp_w_gate, exp_w_up, exp_w_down):
    b, n, d = x.shape
    nc = ctx.shape[1]
    t = b * n
    assert d == D_MODEL and b + 1 <= MOD_ROWS and n % 512 == 0 and nc % 256 == 0

    c_rows = jnp.zeros((MOD_ROWS, d), F32).at[:b].set(c).at[b].set(c_ctx)
    mods = _mods(c_rows, mod_w, mod_b)
    router_wt = router_w.T
    x2d = x.reshape(t, d)

    sh1, sc1, g1, sh2, sc2, g2 = jnp.split(mods[0], 6, axis=-1)
    cos, sin = _rope_tables(n)
    w_in = attn_w_in[0].astype(BF16)
    q_groups = tuple(range(Q_W // LANES))
    rope_groups = q_groups + tuple(range(Q_W // LANES, (Q_W + DIFF_QK_W) // LANES)) + (
        (Q_W + DIFF_QK_W + DIFF_VW) // LANES,)
    proj = _modmm(x2d, sc1[:b], sh1[:b], w_in, rows_per_mod=n, tm=512,
                  rope=(jnp.asarray(cos), jnp.asarray(sin), rope_groups, q_groups), name="attn_in_proj")
    proj_c = _modmm(ctx.reshape(b * nc, d), sc1[b:b + 1], sh1[b:b + 1], w_in[:, Q_W:], rows_per_mod=b * nc, tm=256,
                    name="ctx_in_proj")
    proj = proj.reshape(b, n, ATTN_PROJ_W)
    proj_c = proj_c.reshape(b, nc, KV_W)
    lam_init = 0.8 - 0.6 * math.exp(-0.3 * 0)
    oa = _diff_attn(proj, proj_c, attn_lambda[0], attn_subln_g[0], lam_init, tq=256)
    ow = _win_attn(proj, proj_c, attn_sink[0])
    w_out = attn_w_out[0].astype(BF16)
    x1, h2, lgt = _proj_ln([oa.reshape(t, DIFF_VW), ow.reshape(t, WIN_Q_W)], [w_out[:DIFF_VW], w_out[DIFF_VW:]],
                           x2d, g1[:b], ln_g[0, 0], ln_b[0, 0], sc2[:b], sh2[:b], router_wt, rows_per_mod=n, tm=512,
                           name="attn_out_proj_ln")
    x2 = _moe(h2, lgt, x1, g2[:b], ln_g[0, 1], ln_b[0, 1], router_bias, exp_w_gate, exp_w_up, exp_w_down, layer=0,
              rows_per_mod=n)

    sh1, sc1, g1, sh2, sc2, g2 = jnp.split(mods[1], 6, axis=-1)
    u = _modmm(x2, sc1[:b], sh1[:b], hy_w_in[0].astype(BF16), rows_per_mod=n, tm=512, name="hyena_in_proj")
    fb, tc = 256, 512
    fwd_np, inv_np = _dft_tables(n, fb)
    fwd_tab, inv_tab = jnp.asarray(fwd_np).astype(BF16), jnp.asarray(inv_np).astype(BF16)
    zp, decay = _filter_tables(n)
    w_in_p = jnp.zeros((FILTER_HID, FILTER_HID), F32).at[:FILTER_EMB].set(hy_ffn_w_in[0])
    hid = _filter_mlp(jnp.asarray(zp), w_in_p, hy_ffn_w_hid[0], hy_ffn_b[0], hy_sin_freq[0])
    hr, gg, nyq = _filter_spec(hid, hy_ffn_w_out[0].reshape(FILTER_HID, 2, 2, d).transpose(1, 2, 0, 3), jnp.asarray(decay), fwd_tab, fb, tc)
    z2 = _hyena_conv(u.reshape(b, n, 3 * d), hy_conv_w[0], hy_conv_b[0].reshape(1, 3 * d), fwd_tab, inv_tab,
                     hr, gg, nyq, hy_skip[0], fb, tc)
    x3, h2, lgt = _proj_ln([z2.reshape(t, d)], [hy_w_out[0].astype(BF16)], x2, g1[:b], ln_g[1, 0], ln_b[1, 0],
                           sc2[:b], sh2[:b], router_wt, rows_per_mod=n, tm=512, name="hyena_out_proj_ln")
    x4 = _moe(h2, lgt, x3, g2[:b], ln_g[1, 1], ln_b[1, 1], router_bias, exp_w_gate, exp_w_up, exp_w_down, layer=1,
              rows_per_mod=n)
    return x4.reshape(b, n, d)
```

```python
import functools
import math

import jax
import jax.numpy as jnp
import numpy as np
from jax import lax
from jax.experimental import pallas as pl
from jax.experimental.pallas import tpu as pltpu

F32 = jnp.float32
BF16 = jnp.bfloat16
I32 = jnp.int32

D_MODEL = 1024
DEPTH = 2
GRID_W = 64
HEAD_DIM = 64
DIFF_HEADS = 4
WIN_Q_HEADS = 8
WIN_KV_HEADS = 2
WINDOW = 128
WIN_BLOCK = 128
ROPE_BASE = 10000.0
DIFF_QK_W = DIFF_HEADS * 2 * HEAD_DIM
DIFF_VW = DIFF_HEADS * 2 * HEAD_DIM
WIN_Q_W = WIN_Q_HEADS * HEAD_DIM
WIN_KV_W = WIN_KV_HEADS * HEAD_DIM
Q_W = DIFF_QK_W + WIN_Q_W
KV_W = DIFF_QK_W + DIFF_VW + 2 * WIN_KV_W
ATTN_PROJ_W = Q_W + KV_W
FILTER_EMB = 33
FILTER_HID = 64
DECAY_TARGET = 1e-2
FAST_DECAY_PCT = 0.3
SLOW_DECAY_PCT = 1.5
N_EXPERTS = 16
N_GROUPS = 4
EXPERTS_PER_GROUP = N_EXPERTS // N_GROUPS
EXPERT_FF = 1024
LN_EPS = 1e-5
DEEPNORM_ALPHA = (2 * DEPTH) ** 0.25
NEG_INF = -1e30

LANES = 128
MOD_ROWS = 16
VMEM_LIMIT = 60 * 1024 * 1024
MOE_BM = 256
MOE_TILE = 512
ROW_ALIGN = 8
NT_DIMS = (((1,), (1,)), ((), ()))


def _cparams(n_axes):
    return pltpu.CompilerParams(dimension_semantics=("arbitrary",) * n_axes, vmem_limit_bytes=VMEM_LIMIT)


@functools.lru_cache(maxsize=None)
def _rope_tables(n):
    rows = n // GRID_W
    r, col = np.meshgrid(np.arange(rows, dtype=np.float32), np.arange(GRID_W, dtype=np.float32), indexing="ij")
    axis_dim = HEAD_DIM // 2
    inv_freq = (ROPE_BASE ** (-np.arange(0, axis_dim, 2, dtype=np.float32) / axis_dim)).astype(np.float32)
    ang = np.concatenate([r.reshape(-1, 1) * inv_freq, col.reshape(-1, 1) * inv_freq], -1)
    ang = np.concatenate([ang, ang], -1).astype(np.float32)
    cos, sin = np.cos(ang), np.sin(ang)
    half = np.arange(HEAD_DIM) < HEAD_DIM // 2
    sin_signed = np.where(half[None, :], -sin, sin)
    reps = LANES // HEAD_DIM
    return (np.tile(cos, (1, reps)).astype(np.float32), np.tile(sin_signed, (1, reps)).astype(np.float32))


@functools.lru_cache(maxsize=None)
def _dft_tables(n, fb):
    big = 2 * n
    k = np.arange(n, dtype=np.int64)[:, None]
    t = np.arange(n, dtype=np.int64)[None, :]
    ang = ((k * t) % big).astype(np.float64) * (2.0 * math.pi / big)
    c, s = np.cos(ang), np.sin(ang)
    alt = (1 - 2 * (np.arange(n) & 1)).astype(np.float64)
    s_f = s.copy()
    s_f[0, :] = alt
    kb = n // fb
    fwd = np.concatenate([c.reshape(kb, fb, n), s_f.reshape(kb, fb, n)], axis=1)
    ci = c.T * (2.0 / big)
    ci[:, 0] = 1.0 / big
    si = s.T * (2.0 / big)
    si[:, 0] = alt / big
    inv = np.concatenate([ci.reshape(n, kb, fb).transpose(1, 0, 2), si.reshape(n, kb, fb).transpose(1, 0, 2)], axis=2)
    return fwd.astype(np.float32), inv.astype(np.float32)


@functools.lru_cache(maxsize=None)
def _filter_tables(n):
    t = np.linspace(0.0, 1.0, n, dtype=np.float32)[:, None]
    bands = (FILTER_EMB - 1) // 2
    w = (2.0 * math.pi * np.arange(n, dtype=np.float32)[:, None] / n).astype(np.float32)
    fr = np.linspace(1e-4, bands - 1, bands, dtype=np.float32)[None, :]
    z = np.concatenate([t, np.cos(fr * w), -np.sin(fr * w)], -1).astype(np.float32)
    zp = np.zeros((n, FILTER_HID), np.float32)
    zp[:, :FILTER_EMB] = z
    deltas = np.abs(np.linspace(math.log(DECAY_TARGET) / SLOW_DECAY_PCT, math.log(DECAY_TARGET) / FAST_DECAY_PCT,
                                D_MODEL, dtype=np.float32))
    decay = np.exp(-t * deltas[None, :]).astype(np.float32)
    return zp, decay


def _mods_kernel(c_ref, w_ref, b_ref, o_ref):
    c = c_ref[...]
    a = (c * jax.nn.sigmoid(c)).astype(BF16)
    o_ref[...] = jnp.dot(a, w_ref[...].astype(BF16), preferred_element_type=F32) + b_ref[...]


def _mods(c_rows, mod_w, mod_b):
    d = D_MODEL
    tn = 1536
    return pl.pallas_call(
        _mods_kernel,
        grid=(DEPTH, 6 * d // tn),
        in_specs=[pl.BlockSpec((MOD_ROWS, d), lambda l, j: (0, 0)),
                  pl.BlockSpec((None, d, tn), lambda l, j: (l, 0, j)),
                  pl.BlockSpec((None, 1, tn), lambda l, j: (l, 0, j))],
        out_specs=pl.BlockSpec((None, MOD_ROWS, tn), lambda l, j: (l, 0, j)),
        out_shape=jax.ShapeDtypeStruct((DEPTH, MOD_ROWS, 6 * d), F32),
        compiler_params=_cparams(2), name="mods",
    )(c_rows, mod_w, mod_b.reshape(DEPTH, 1, 6 * d))


def _modmm_kernel(*refs, n_groups, rope_groups, scaled_groups, chunk):
    if rope_groups:
        x_ref, sc_ref, sh_ref, w_ref, cos_ref, sin_ref, o_ref = refs
        cos, sin = cos_ref[...], sin_ref[...]
        lane = lax.broadcasted_iota(I32, (1, LANES), 1)
        first_half = (lane % HEAD_DIM) < HEAD_DIM // 2
    else:
        x_ref, sc_ref, sh_ref, w_ref, o_ref = refs
    h = (x_ref[...] * (1.0 + sc_ref[...]) + sh_ref[...]).astype(BF16)
    gpc = chunk // LANES
    for c in range(n_groups // gpc):
        acc = jnp.dot(h, w_ref[:, c * chunk:(c + 1) * chunk], preferred_element_type=F32)
        for j in range(gpc):
            g = c * gpc + j
            blk = acc[:, j * LANES:(j + 1) * LANES]
            if g in rope_groups:
                rot = jnp.where(first_half, pltpu.roll(blk, LANES - HEAD_DIM // 2, 1), pltpu.roll(blk, HEAD_DIM // 2, 1))
                blk = blk * cos + rot * sin
                if g in scaled_groups:
                    blk = blk * (HEAD_DIM ** -0.5)
            o_ref[:, g * LANES:(g + 1) * LANES] = blk.astype(o_ref.dtype)


def _modmm(x2d, sc, sh, w, rows_per_mod, tm, rope=None, name="modmm"):
    t, d = x2d.shape
    n = w.shape[1]
    tiles_per_mod = rows_per_mod // tm
    nmod = sc.shape[0]
    in_specs = [pl.BlockSpec((tm, d), lambda i: (i, 0)),
                pl.BlockSpec((None, 1, d), lambda i: (i // tiles_per_mod, 0, 0)),
                pl.BlockSpec((None, 1, d), lambda i: (i // tiles_per_mod, 0, 0)),
                pl.BlockSpec((d, n), lambda i: (0, 0))]
    args = [x2d, sc.reshape(nmod, 1, d), sh.reshape(nmod, 1, d), w]
    rope_groups, scaled_groups = (), ()
    if rope is not None:
        cos, sin, rope_groups, scaled_groups = rope
        in_specs += [pl.BlockSpec((tm, LANES), lambda i: (i % tiles_per_mod, 0)),
                     pl.BlockSpec((tm, LANES), lambda i: (i % tiles_per_mod, 0))]
        args += [cos, sin]
    kern = functools.partial(_modmm_kernel, n_groups=n // LANES, rope_groups=tuple(rope_groups),
                             scaled_groups=tuple(scaled_groups), chunk=256)
    return pl.pallas_call(
        kern, grid=(t // tm,), in_specs=in_specs,
        out_specs=pl.BlockSpec((tm, n), lambda i: (i, 0)),
        out_shape=jax.ShapeDtypeStruct((t, n), BF16),
        compiler_params=_cparams(1), name=name,
    )(*args)


def _diff_attn_kernel(q_ref, k_ref, v_ref, kc_ref, vc_ref, lam_ref, g_ref, o_ref, *, lam_init):
    q = q_ref[...]
    lane = lax.broadcasted_iota(I32, (1, LANES), 1)
    zero = jnp.zeros_like(q)
    k, kc = k_ref[...], kc_ref[...]

    def probs(qm):
        sl = lax.dot_general(qm, k, NT_DIMS, preferred_element_type=F32)
        sc = lax.dot_general(qm, kc, NT_DIMS, preferred_element_type=F32)
        m = jnp.maximum(jnp.max(sl, -1, keepdims=True), jnp.max(sc, -1, keepdims=True))
        pl_, pc = jnp.exp(sl - m), jnp.exp(sc - m)
        den = jnp.sum(pl_, -1, keepdims=True) + jnp.sum(pc, -1, keepdims=True)
        return pl_, pc, 1.0 / den

    lv = lam_ref[...]
    lam = (jnp.exp(jnp.sum(lv[0:1] * lv[1:2], keepdims=True)) - jnp.exp(jnp.sum(lv[2:3] * lv[3:4], keepdims=True))
           + lam_init)
    p1l, p1c, r1 = probs(jnp.where(lane < HEAD_DIM, q, zero))
    p2l, p2c, r2 = probs(jnp.where(lane >= HEAD_DIM, q, zero))
    w2 = lam * r2
    al = (p1l * r1 - p2l * w2).astype(BF16)
    ac = (p1c * r1 - p2c * w2).astype(BF16)
    o = (jnp.dot(al, v_ref[...], preferred_element_type=F32) + jnp.dot(ac, vc_ref[...], preferred_element_type=F32))
    ms = jnp.mean(o * o, -1, keepdims=True)
    o_ref[...] = (o * lax.rsqrt(ms + LN_EPS) * g_ref[...] * (1.0 - lam_init)).astype(o_ref.dtype)


def _diff_attn(proj, proj_c, lam_vec, subln_g, lam_init, tq):
    b, n, _ = proj.shape
    nc = proj_c.shape[1]
    kcol = Q_W // LANES
    vcol = (Q_W + DIFF_QK_W) // LANES
    vccol = DIFF_QK_W // LANES
    kern = functools.partial(_diff_attn_kernel, lam_init=lam_init)
    return pl.pallas_call(
        kern, grid=(b, DIFF_HEADS, n // tq),
        in_specs=[pl.BlockSpec((None, tq, LANES), lambda bi, h, i: (bi, i, h)),
                  pl.BlockSpec((None, n, LANES), lambda bi, h, i: (bi, 0, kcol + h)),
                  pl.BlockSpec((None, n, LANES), lambda bi, h, i: (bi, 0, vcol + h)),
                  pl.BlockSpec((None, nc, LANES), lambda bi, h, i: (bi, 0, h)),
                  pl.BlockSpec((None, nc, LANES), lambda bi, h, i: (bi, 0, vccol + h)),
                  pl.BlockSpec((4, HEAD_DIM), lambda bi, h, i: (0, 0)),
                  pl.BlockSpec((1, LANES), lambda bi, h, i: (0, 0))],
        out_specs=pl.BlockSpec((None, tq, LANES), lambda bi, h, i: (bi, i, h)),
        out_shape=jax.ShapeDtypeStruct((b, n, DIFF_VW), BF16),
        compiler_params=_cparams(3), name="diff_attn",
    )(proj, proj, proj, proj_c, proj_c, lam_vec, subln_g.reshape(1, LANES))


def _win_attn_kernel(sink_ref, q_ref, kp_ref, kn_ref, kx_ref, vp_ref, vn_ref, vx_ref, kc_ref, vc_ref, o_ref, *, seq):
    g = pl.program_id(1)
    n = pl.program_id(2)
    wb = WIN_BLOCK
    gq = WIN_Q_HEADS // WIN_KV_HEADS
    lane = lax.broadcasted_iota(I32, (1, LANES), 1)
    in_g = (lane // HEAD_DIM) == g
    q = q_ref[...].astype(F32)
    parts = []
    for j in range(gq):
        x = q[:, (j // 2) * LANES:(j // 2 + 1) * LANES]
        x = jnp.where(g == (j % 2), x, pltpu.roll(x, HEAD_DIM, 1))
        parts.append(jnp.where(in_g, x, 0.0))
    qs = jnp.concatenate(parts, 0).astype(BF16)
    k_loc = jnp.concatenate([kp_ref[...], kn_ref[...], kx_ref[...]], 0)
    v_loc = jnp.concatenate([vp_ref[...], vn_ref[...], vx_ref[...]], 0)
    s_loc = lax.dot_general(qs, k_loc, NT_DIMS, preferred_element_type=F32)
    s_ctx = lax.dot_general(qs, kc_ref[...], NT_DIMS, preferred_element_type=F32)
    row = lax.broadcasted_iota(I32, (gq * wb, 1), 0)
    q_off = row % wb + wb
    k_off = lax.broadcasted_iota(I32, (1, 3 * wb), 1)
    k_abs = (n - 1) * wb + k_off
    allowed = (jnp.abs(q_off - k_off) <= WINDOW) & (k_abs >= 0) & (k_abs < seq)
    s_loc = jnp.where(allowed, s_loc, NEG_INF)
    head = row // wb
    sk = jnp.zeros((gq * wb, 1), F32)
    for j in range(gq):
        sk = jnp.where(head == j, sink_ref[g * gq + j], sk)
    m = jnp.maximum(jnp.maximum(jnp.max(s_loc, -1, keepdims=True), jnp.max(s_ctx, -1, keepdims=True)), sk)
    p_loc, p_ctx = jnp.exp(s_loc - m), jnp.exp(s_ctx - m)
    den = jnp.sum(p_loc, -1, keepdims=True) + jnp.sum(p_ctx, -1, keepdims=True) + jnp.exp(sk - m)
    o = (jnp.dot(p_loc.astype(BF16), v_loc, preferred_element_type=F32)
         + jnp.dot(p_ctx.astype(BF16), vc_ref[...], preferred_element_type=F32)) * (1.0 / den)
    for cb in range(gq // 2):
        a = o[(2 * cb) * wb:(2 * cb + 1) * wb]
        bb = o[(2 * cb + 1) * wb:(2 * cb + 2) * wb]
        a = jnp.where(g == 0, a, pltpu.roll(a, HEAD_DIM, 1))
        bb = jnp.where(g == 1, bb, pltpu.roll(bb, HEAD_DIM, 1))
        o_ref[:, cb * LANES:(cb + 1) * LANES] = jnp.where(lane < HEAD_DIM, a, bb).astype(o_ref.dtype)


def _win_attn(proj, proj_c, sink):
    b, n, _ = proj.shape
    nc = proj_c.shape[1]
    nb = n // WIN_BLOCK
    gw = (WIN_Q_HEADS // WIN_KV_HEADS) * HEAD_DIM
    qcol = DIFF_QK_W // gw
    kcol = (Q_W + DIFF_QK_W + DIFF_VW) // LANES
    vcol = kcol + 1
    kccol = (DIFF_QK_W + DIFF_VW) // LANES
    prev = lambda bi, g, i: (bi, jnp.maximum(i - 1, 0), kcol)
    cur = lambda bi, g, i: (bi, i, kcol)
    nxt = lambda bi, g, i: (bi, jnp.minimum(i + 1, nb - 1), kcol)
    vprev = lambda bi, g, i: (bi, jnp.maximum(i - 1, 0), vcol)
    vcur = lambda bi, g, i: (bi, i, vcol)
    vnxt = lambda bi, g, i: (bi, jnp.minimum(i + 1, nb - 1), vcol)
    blk = (None, WIN_BLOCK, LANES)
    kern = functools.partial(_win_attn_kernel, seq=n)
    return pl.pallas_call(
        kern, grid=(b, WIN_KV_HEADS, nb),
        in_specs=[pl.BlockSpec(memory_space=pltpu.SMEM),
                  pl.BlockSpec((None, WIN_BLOCK, gw), lambda bi, g, i: (bi, i, qcol + g)),
                  pl.BlockSpec(blk, prev), pl.BlockSpec(blk, cur), pl.BlockSpec(blk, nxt),
                  pl.BlockSpec(blk, vprev), pl.BlockSpec(blk, vcur), pl.BlockSpec(blk, vnxt),
                  pl.BlockSpec((None, nc, LANES), lambda bi, g, i: (bi, 0, kccol)),
                  pl.BlockSpec((None, nc, LANES), lambda bi, g, i: (bi, 0, kccol + 1))],
        out_specs=pl.BlockSpec((None, WIN_BLOCK, gw), lambda bi, g, i: (bi, i, g)),
        out_shape=jax.ShapeDtypeStruct((b, n, WIN_Q_W), BF16),
        compiler_params=_cparams(3), name="win_attn",
    )(sink, proj, proj, proj, proj, proj, proj, proj, proj_c, proj_c)


def _layer_norm(r, g, b):
    mu = jnp.mean(r, -1, keepdims=True)
    dlt = r - mu
    var = jnp.mean(dlt * dlt, -1, keepdims=True)
    return dlt * lax.rsqrt(var + LN_EPS) * g + b


def _proj_ln_kernel(*refs, n_in):
    a_refs = refs[:n_in]
    w_refs = refs[n_in:2 * n_in]
    x_ref, gate_ref, lng_ref, lnb_ref, sc_ref, sh_ref, rw_ref, x1_ref, h2_ref, lg_ref = refs[2 * n_in:]
    y = jnp.dot(a_refs[0][...], w_refs[0][...], preferred_element_type=F32)
    for a_ref, w_ref in zip(a_refs[1:], w_refs[1:]):
        y = y + jnp.dot(a_ref[...], w_ref[...], preferred_element_type=F32)
    xn = _layer_norm(DEEPNORM_ALPHA * x_ref[...] + gate_ref[...] * y, lng_ref[...], lnb_ref[...])
    x1_ref[...] = xn
    h2 = xn * (1.0 + sc_ref[...]) + sh_ref[...]
    h2_ref[...] = h2
    lg_ref[...] = lax.dot_general(rw_ref[...], h2, NT_DIMS, precision=lax.Precision.HIGHEST,
                                  preferred_element_type=F32)


def _proj_ln(acts, ws, x2d, gate, ln_g, ln_b, sc, sh, router_wt, rows_per_mod, tm, name):
    t, d = x2d.shape
    n_in = len(acts)
    tiles_per_mod = rows_per_mod // tm
    nmod = gate.shape[0]
    row = lambda i: (i, 0)
    full = lambda i: (0, 0)
    mod = lambda i: (i // tiles_per_mod, 0, 0)
    in_specs = ([pl.BlockSpec((tm, a.shape[1]), row) for a in acts]
                + [pl.BlockSpec(w.shape, full) for w in ws]
                + [pl.BlockSpec((tm, d), row), pl.BlockSpec((None, 1, d), mod),
                   pl.BlockSpec((1, d), full), pl.BlockSpec((1, d), full),
                   pl.BlockSpec((None, 1, d), mod), pl.BlockSpec((None, 1, d), mod),
                   pl.BlockSpec((N_EXPERTS, d), full)])
    return pl.pallas_call(
        functools.partial(_proj_ln_kernel, n_in=n_in), grid=(t // tm,), in_specs=in_specs,
        out_specs=[pl.BlockSpec((tm, d), row), pl.BlockSpec((tm, d), row),
                   pl.BlockSpec((N_EXPERTS, tm), lambda i: (0, i))],
        out_shape=[jax.ShapeDtypeStruct((t, d), F32), jax.ShapeDtypeStruct((t, d), F32),
                   jax.ShapeDtypeStruct((N_EXPERTS, t), F32)],
        compiler_params=_cparams(1), name=name,
    )(*acts, *ws, x2d, gate.reshape(nmod, 1, d), ln_g.reshape(1, d), ln_b.reshape(1, d),
      sc.reshape(nmod, 1, d), sh.reshape(nmod, 1, d), router_wt)


def _first_argmax(vals):
    idx = jnp.zeros(vals[0].shape, I32)
    best = vals[0]
    for j in range(1, len(vals)):
        upd = vals[j] > best
        idx = jnp.where(upd, j, idx)
        best = jnp.where(upd, vals[j], best)
    return idx, best


def _route_kernel(lg_ref, bias_ref, pos_ref, w_ref, tab_ref, cnt_ref, carry_ref, off_ref, *, tr):
    @pl.when(pl.program_id(0) == 0)
    def _():
        carry_ref[...] = jnp.zeros_like(carry_ref)

    lg = lg_ref[...]
    ex = jnp.exp(lg - jnp.max(lg, 0, keepdims=True))
    scores = ex / jnp.sum(ex, 0, keepdims=True)
    sel = scores + bias_ref[...]
    rows = [sel[e:e + 1] for e in range(N_EXPERTS)]
    group_scores = []
    for g in range(N_GROUPS):
        r = rows[g * EXPERTS_PER_GROUP:(g + 1) * EXPERTS_PER_GROUP]
        best = None
        for i in range(EXPERTS_PER_GROUP):
            for j in range(i + 1, EXPERTS_PER_GROUP):
                s = r[i] + r[j]
                best = s if best is None else jnp.maximum(best, s)
        group_scores.append(best)
    grp, _ = _first_argmax(group_scores)
    vals = []
    for j in range(EXPERTS_PER_GROUP):
        v = rows[(N_GROUPS - 1) * EXPERTS_PER_GROUP + j]
        for g in range(N_GROUPS - 2, -1, -1):
            v = jnp.where(grp == g, rows[g * EXPERTS_PER_GROUP + j], v)
        vals.append(v)
    i0, _ = _first_argmax(vals)
    i1, _ = _first_argmax([jnp.where(i0 == j, -jnp.inf, vals[j]) for j in range(EXPERTS_PER_GROUP)])
    e0 = grp * EXPERTS_PER_GROUP + i0
    e1 = grp * EXPERTS_PER_GROUP + i1
    eid = lax.broadcasted_iota(I32, (N_EXPERTS, 1), 0)
    oh0, oh1 = eid == e0, eid == e1
    s0 = jnp.sum(jnp.where(oh0, scores, 0.0), 0, keepdims=True)
    s1 = jnp.sum(jnp.where(oh1, scores, 0.0), 0, keepdims=True)
    den = s0 + s1
    member = jnp.where(oh0 | oh1, 1.0, 0.0)
    before = lax.broadcasted_iota(I32, (tr, tr), 0) < lax.broadcasted_iota(I32, (tr, tr), 1)
    upper = jnp.where(before, 1.0, 0.0).astype(BF16)
    cnt = jnp.dot(member.astype(BF16), upper, preferred_element_type=F32)
    run = jnp.sum(member, 1, keepdims=True)
    run = jnp.floor((run + (ROW_ALIGN - 1)) * (1.0 / ROW_ALIGN)) * ROW_ALIGN
    run = jnp.broadcast_to(run, (N_EXPERTS, LANES))
    acc = jnp.zeros((1, LANES), F32)
    for e in range(N_EXPERTS):
        off_ref[e:e + 1, :] = acc
        acc = acc + run[e:e + 1]
    off = off_ref[...]
    at = off[:, 0:1] + cnt
    pos_ref[0:1, :] = jnp.sum(jnp.where(oh0, at, 0.0), 0, keepdims=True).astype(I32)
    pos_ref[1:2, :] = jnp.sum(jnp.where(oh1, at, 0.0), 0, keepdims=True).astype(I32)
    w_ref[0:1, :] = s0 / den
    w_ref[1:2, :] = s1 / den
    tab_ref[0] = off.astype(I32)
    tab_ref[1] = run.astype(I32)
    tab_ref[2] = carry_ref[...].astype(I32)
    carry_ref[...] = carry_ref[...] + run
    cnt_ref[...] = carry_ref[...]


def _route(logits_t, router_bias, tr):
    t = logits_t.shape[1]
    tok = lambda i: (0, i)
    return pl.pallas_call(
        functools.partial(_route_kernel, tr=tr), grid=(t // tr,),
        in_specs=[pl.BlockSpec((N_EXPERTS, tr), tok), pl.BlockSpec((N_EXPERTS, 1), lambda i: (0, 0))],
        out_specs=[pl.BlockSpec((2, tr), tok), pl.BlockSpec((2, tr), tok),
                   pl.BlockSpec((None, 3, N_EXPERTS, LANES), lambda i: (i, 0, 0, 0)),
                   pl.BlockSpec((N_EXPERTS, LANES), lambda i: (0, 0))],
        out_shape=[jax.ShapeDtypeStruct((2, t), I32), jax.ShapeDtypeStruct((2, t), F32),
                   jax.ShapeDtypeStruct((t // tr, 3, N_EXPERTS, LANES), I32),
                   jax.ShapeDtypeStruct((N_EXPERTS, LANES), F32)],
        scratch_shapes=[pltpu.VMEM((N_EXPERTS, LANES), F32), pltpu.VMEM((N_EXPERTS, LANES), F32)],
        compiler_params=_cparams(1), name="route",
    )(logits_t, router_bias.reshape(N_EXPERTS, 1))


def _run_copies(n, max_rows, make_copy, wait):
    sz = max_rows
    while sz >= ROW_ALIGN:
        start = (n // (2 * sz)) * (2 * sz)

        @pl.when((n & sz) != 0)
        def _(start=start, sz=sz):
            cp = make_copy(start, sz)
            cp.wait() if wait else cp.start()

        sz //= 2


def _rows(ref, start, size):
    return ref.at[pl.ds(pl.multiple_of(start, ROW_ALIGN), size)]


def _dispatch_kernel(tab_ref, h_ref, pos_ref, w_ref, xs_ref, buf, zbuf, sem, *, tt, n_tiles, bm):
    i = pl.program_id(0)
    d = h_ref.shape[1]
    sb = buf.shape[0]
    ne = N_EXPERTS
    fill = 3 * n_tiles * ne

    @pl.when(i == 0)
    def _():
        zbuf[...] = jnp.zeros_like(zbuf)
        for wait in (False, True):
            for e in range(ne):
                dst, n = tab_ref[fill + e], tab_ref[fill + ne + e]
                _run_copies(n, bm // 2, lambda s, z: pltpu.make_async_copy(
                    zbuf.at[pl.ds(0, z)], _rows(xs_ref, dst + s, z), sem), wait)

        def zero_block(j, carry):
            for half in range(2):
                cp = pltpu.make_async_copy(zbuf, _rows(xs_ref, j * bm + half * (bm // 2), bm // 2), sem)
                cp.start()
                cp.wait()
            return carry

        lax.fori_loop(tab_ref[fill + 2 * ne], xs_ref.shape[0] // bm, zero_block, 0)

    pos, w = pos_ref[...], w_ref[...]
    srow = lax.broadcasted_iota(I32, (sb, 1), 0)
    m0, m1 = srow == pos[0:1], srow == pos[1:2]
    perm = (jnp.where(m0, 1.0, 0.0) + jnp.where(m1, 1.0, 0.0)).astype(BF16)
    buf[:, :d] = jnp.dot(perm, h_ref[...].astype(BF16), preferred_element_type=F32)
    wrow = jnp.sum(jnp.where(m0, w[0:1], 0.0) + jnp.where(m1, w[1:2], 0.0), 1, keepdims=True)
    buf[:, d:] = jnp.broadcast_to(wrow, (sb, LANES))
    for wait in (False, True):
        for e in range(ne):
            off = tab_ref[i * ne + e]
            n = tab_ref[(n_tiles + i) * ne + e]
            dst = tab_ref[(2 * n_tiles + i) * ne + e]
            _run_copies(n, tt, lambda s, z: pltpu.make_async_copy(
                _rows(buf, off + s, z), _rows(xs_ref, dst + s, z), sem), wait)


def _dispatch(tab, h2, pos, wts, n_slots, tt, sb, bm):
    t, d = h2.shape
    n_tiles = t // tt
    return pl.pallas_call(
        functools.partial(_dispatch_kernel, tt=tt, n_tiles=n_tiles, bm=bm),
        grid_spec=pltpu.PrefetchScalarGridSpec(
            num_scalar_prefetch=1, grid=(n_tiles,),
            in_specs=[pl.BlockSpec((tt, d), lambda i, s: (i, 0)), pl.BlockSpec((2, tt), lambda i, s: (0, i)),
                      pl.BlockSpec((2, tt), lambda i, s: (0, i))],
            out_specs=pl.BlockSpec(memory_space=pl.ANY),
            scratch_shapes=[pltpu.VMEM((sb, d + LANES), F32), pltpu.VMEM((bm // 2, d + LANES), F32),
                            pltpu.SemaphoreType.DMA(())]),
        out_shape=jax.ShapeDtypeStruct((n_slots, d + LANES), F32),
        compiler_params=_cparams(1), name="moe_dispatch",
    )(tab, h2, pos, wts)


def _ffn_kernel(blk_e_ref, nused_ref, x_ref, wg_ref, wu_ref, wd_ref, o_ref, wg_bf, wu_bf, wd_bf):
    i = pl.program_id(0)
    new_expert = (i == 0) | (blk_e_ref[i] != blk_e_ref[jnp.maximum(i - 1, 0)])

    @pl.when(new_expert)
    def _():
        wg_bf[...] = wg_ref[...].astype(BF16)
        wu_bf[...] = wu_ref[...].astype(BF16)
        wd_bf[...] = wd_ref[...].astype(BF16)

    @pl.when(i < nused_ref[0])
    def _():
        d = wg_bf.shape[0]
        x = x_ref[:, :d].astype(BF16)
        gate = jnp.dot(x, wg_bf[...], preferred_element_type=F32)
        up = jnp.dot(x, wu_bf[...], preferred_element_type=F32)
        act = (gate * jax.nn.sigmoid(gate) * up).astype(BF16)
        o_ref[...] = jnp.dot(act, wd_bf[...], preferred_element_type=F32) * x_ref[:, d:d + 1]

    @pl.when(i >= nused_ref[0])
    def _():
        o_ref[...] = jnp.zeros_like(o_ref)


def _ffn(blk_e, n_used, xs, wg, wu, wd, layer, bm):
    n_slots, xw = xs.shape
    d, ff = wg.shape[2:]
    xrow = lambda i, be, nu: (jnp.minimum(i, nu[0] - 1), 0)
    wsel = lambda i, be, nu: (layer, be[i], 0, 0)
    return pl.pallas_call(
        _ffn_kernel,
        grid_spec=pltpu.PrefetchScalarGridSpec(
            num_scalar_prefetch=2, grid=(n_slots // bm,),
            in_specs=[pl.BlockSpec((bm, xw), xrow), pl.BlockSpec((None, None, d, ff), wsel),
                      pl.BlockSpec((None, None, d, ff), wsel), pl.BlockSpec((None, None, ff, d), wsel)],
            out_specs=pl.BlockSpec((bm, d), lambda i, be, nu: (i, 0)),
            scratch_shapes=[pltpu.VMEM((d, ff), BF16), pltpu.VMEM((d, ff), BF16), pltpu.VMEM((ff, d), BF16)]),
        out_shape=jax.ShapeDtypeStruct((n_slots, d), F32),
        compiler_params=_cparams(1), name="moe_ffn",
    )(blk_e, n_used, xs, wg, wu, wd)


def _combine_kernel(tab_ref, ys_ref, pos_ref, x1_ref, gate_ref, lng_ref, lnb_ref, o_ref, ybuf, sem, *, tt, n_tiles):
    i = pl.program_id(0)
    sb = ybuf.shape[0]
    ne = N_EXPERTS

    @pl.when(i == 0)
    def _():
        ybuf[...] = jnp.zeros_like(ybuf)

    for wait in (False, True):
        for e in range(ne):
            off = tab_ref[i * ne + e]
            n = tab_ref[(n_tiles + i) * ne + e]
            src = tab_ref[(2 * n_tiles + i) * ne + e]
            _run_copies(n, tt, lambda s, z: pltpu.make_async_copy(
                _rows(ys_ref, src + s, z), _rows(ybuf, off + s, z), sem), wait)
    pos = pos_ref[...]
    scol = lax.broadcasted_iota(I32, (1, sb), 1)
    unperm = (jnp.where(scol == pos[:, 0:1], 1.0, 0.0) + jnp.where(scol == pos[:, 1:2], 1.0, 0.0)).astype(BF16)
    f = jnp.dot(unperm, ybuf[...].astype(BF16), preferred_element_type=F32)
    o_ref[...] = _layer_norm(DEEPNORM_ALPHA * x1_ref[...] + gate_ref[...] * f, lng_ref[...], lnb_ref[...])


def _combine(tab, ys, pos_t, x1, gate, ln_g, ln_b, rows_per_mod, tt, sb):
    t, d = x1.shape
    tiles_per_mod = rows_per_mod // tt
    nmod = gate.shape[0]
    return pl.pallas_call(
        functools.partial(_combine_kernel, tt=tt, n_tiles=t // tt),
        grid_spec=pltpu.PrefetchScalarGridSpec(
            num_scalar_prefetch=1, grid=(t // tt,),
            in_specs=[pl.BlockSpec(memory_space=pl.ANY),
                      pl.BlockSpec((tt, 2), lambda i, s: (i, 0)),
                      pl.BlockSpec((tt, d), lambda i, s: (i, 0)),
                      pl.BlockSpec((None, 1, d), lambda i, s: (i // tiles_per_mod, 0, 0)),
                      pl.BlockSpec((1, d), lambda i, s: (0, 0)), pl.BlockSpec((1, d), lambda i, s: (0, 0))],
            out_specs=pl.BlockSpec((tt, d), lambda i, s: (i, 0)),
            scratch_shapes=[pltpu.VMEM((sb, d), F32), pltpu.SemaphoreType.DMA(())]),
        out_shape=jax.ShapeDtypeStruct((t, d), F32),
        compiler_params=_cparams(1), name="moe_combine",
    )(tab, ys, pos_t, x1, gate.reshape(nmod, 1, d), ln_g.reshape(1, d), ln_b.reshape(1, d))


def _moe(h2, logits_t, x1, gate, ln_g, ln_b, router_bias, wg, wu, wd, layer, rows_per_mod):
    t, d = h2.shape
    bm, tt, ne = MOE_BM, MOE_TILE, N_EXPERTS
    n_tiles = t // tt
    pad = ne * (ROW_ALIGN - 1)
    sb = -(-(2 * tt + pad) // LANES) * LANES
    pos, wts, tab3, cnt = _route(logits_t, router_bias, tr=tt)
    rows = cnt[:, 0].astype(I32)
    prows = (rows + bm - 1) // bm * bm
    pends = jnp.cumsum(prows)
    pstarts = pends - prows
    n_blk = -(-(2 * t + n_tiles * pad + ne * (bm - 1)) // bm)
    n_used = pends[-1] // bm
    blk_ids = jnp.minimum(jnp.arange(n_blk, dtype=I32), n_used - 1)
    blk_e = jnp.minimum(jnp.sum((blk_ids[:, None] * bm >= pends[None, :]).astype(I32), 1), ne - 1)
    tab3 = tab3[:, :, :, 0]
    tab = jnp.concatenate([tab3[:, 0].reshape(-1), tab3[:, 1].reshape(-1),
                           (tab3[:, 2] + pstarts[None, :]).reshape(-1), pstarts + rows, prows - rows,
                           n_used.reshape(1)]).astype(I32)
    xs = _dispatch(tab, h2, pos, wts, n_blk * bm, tt, sb, bm)
    ys = _ffn(blk_e, n_used.reshape(1).astype(I32), xs, wg, wu, wd, layer, bm)
    return _combine(tab, ys, pos.T, x1, gate, ln_g, ln_b, rows_per_mod, tt, sb)


def _filter_mlp_kernel(z_ref, w1_ref, w2_ref, w3_ref, b_ref, sf_ref, o_ref):
    hp = lax.Precision.HIGHEST
    b, sf = b_ref[...], sf_ref[...]
    h = jnp.sin(sf[0:1] * (jnp.dot(z_ref[...], w1_ref[...], precision=hp, preferred_element_type=F32) + b[0:1]))
    h = jnp.sin(sf[1:2] * (jnp.dot(h, w2_ref[...], precision=hp, preferred_element_type=F32) + b[1:2]))
    o_ref[...] = jnp.sin(sf[2:3] * (jnp.dot(h, w3_ref[...], precision=hp, preferred_element_type=F32) + b[2:3]))


def _filter_mlp(zp, w_in_p, w_hid, b, sf):
    n = zp.shape[0]
    return pl.pallas_call(
        _filter_mlp_kernel, out_shape=jax.ShapeDtypeStruct((n, FILTER_HID), F32),
        compiler_params=pltpu.CompilerParams(vmem_limit_bytes=VMEM_LIMIT), name="hyena_filter_mlp",
    )(zp, w_in_p, w_hid[0], w_hid[1], b, sf)


def _filter_spec_kernel(hid_ref, wf_ref, wb_ref, dec_ref, fwd_ref, hr_ref, g_ref, nyq_ref, p_scr, q_scr, *, fb):
    k = pl.program_id(2)
    n = hid_ref.shape[0]

    @pl.when(k == 0)
    def _():
        hp = lax.Precision.HIGHEST
        hid, dec = hid_ref[...], dec_ref[...]
        row = lax.broadcasted_iota(I32, (n, 1), 0)
        fw = jnp.dot(hid, wf_ref[...], precision=hp, preferred_element_type=F32) * dec
        bw = jnp.dot(hid, wb_ref[...], precision=hp, preferred_element_type=F32) * dec
        bw = jnp.where(row == 0, 0.0, bw)
        p = fw + bw
        p_scr[...] = p.astype(BF16)
        q_scr[...] = (fw - bw).astype(BF16)
        sign = (1 - 2 * (row & 1)).astype(F32)
        nyq_ref[...] = jnp.sum(p * sign, 0, keepdims=True)

    hr_ref[...] = jnp.dot(fwd_ref[0:fb, :], p_scr[...], preferred_element_type=F32)
    gg = jnp.dot(fwd_ref[fb:2 * fb, :], q_scr[...], preferred_element_type=F32)
    rowb = lax.broadcasted_iota(I32, (fb, 1), 0)
    g_ref[...] = jnp.where((rowb == 0) & (k == 0), 0.0, gg)


def _filter_spec(hid, w_out4, decay, fwd_tab, fb, tc):
    n = hid.shape[0]
    d = D_MODEL
    kb = n // fb
    return pl.pallas_call(
        functools.partial(_filter_spec_kernel, fb=fb), grid=(2, d // tc, kb),
        in_specs=[pl.BlockSpec((n, FILTER_HID), lambda o, c, k: (0, 0)),
                  pl.BlockSpec((None, None, FILTER_HID, tc), lambda o, c, k: (0, o, 0, c)),
                  pl.BlockSpec((None, None, FILTER_HID, tc), lambda o, c, k: (1, o, 0, c)),
                  pl.BlockSpec((n, tc), lambda o, c, k: (0, c)),
                  pl.BlockSpec((None, 2 * fb, n), lambda o, c, k: (k, 0, 0))],
        out_specs=[pl.BlockSpec((None, fb, tc), lambda o, c, k: (o, k, c)),
                   pl.BlockSpec((None, fb, tc), lambda o, c, k: (o, k, c)),
                   pl.BlockSpec((None, 1, tc), lambda o, c, k: (o, 0, c))],
        out_shape=[jax.ShapeDtypeStruct((2, n, d), F32), jax.ShapeDtypeStruct((2, n, d), F32),
                   jax.ShapeDtypeStruct((2, 1, d), F32)],
        scratch_shapes=[pltpu.VMEM((n, tc), BF16), pltpu.VMEM((n, tc), BF16)],
        compiler_params=_cparams(3), name="hyena_filter_spec",
    )(hid, w_out4, w_out4, decay, fwd_tab)


def _short_conv(u_ref, cw_ref, cb_ref):
    n = u_ref.shape[0]
    u = u_ref[...].astype(F32)
    row = lax.broadcasted_iota(I32, (n, 1), 0)
    prev = jnp.where(row == 0, 0.0, pltpu.roll(u, 1, 0))
    nxt = jnp.where(row == n - 1, 0.0, pltpu.roll(u, n - 1, 0))
    cw = cw_ref[...]
    return prev * cw[0:1] + u * cw[1:2] + nxt * cw[2:3] + cb_ref[...]


def _hyena_conv_kernel(uv_ref, ux1_ref, ux2_ref, cwv_ref, cwx1_ref, cwx2_ref, cbv_ref, cbx1_ref, cbx2_ref,
                       fwd_ref, inv_ref, hr_ref, g_ref, nyq_ref, sk_ref, o_ref, zb_scr, z32_scr, acc_scr, *, fb, kb):
    o = pl.program_id(2)
    k = pl.program_id(3)

    @pl.when((o == 0) & (k == 0))
    def _():
        v = _short_conv(uv_ref, cwv_ref, cbv_ref)
        z32_scr[...] = v
        zb_scr[...] = v.astype(BF16)

    @pl.when(k == 0)
    def _():
        acc_scr[...] = jnp.zeros_like(acc_scr)

    zf = jnp.dot(fwd_ref[...], zb_scr[...], preferred_element_type=F32)
    zr, zi = zf[:fb], zf[fb:]
    hr, gg = hr_ref[...], g_ref[...]
    rowb = lax.broadcasted_iota(I32, (fb, 1), 0)
    hb = jnp.where((rowb == 0) & (k == 0), nyq_ref[...], hr)
    y = jnp.concatenate([zr * hr - zi * gg, zr * gg + zi * hb], 0).astype(BF16)
    acc_scr[...] += jnp.dot(inv_ref[...], y, preferred_element_type=F32)

    @pl.when((k == kb - 1) & (o == 0))
    def _():
        zn = _short_conv(ux1_ref, cwx1_ref, cbx1_ref) * (acc_scr[...] + z32_scr[...] * sk_ref[...])
        z32_scr[...] = zn
        zb_scr[...] = zn.astype(BF16)

    @pl.when((k == kb - 1) & (o == 1))
    def _():
        zn = _short_conv(ux2_ref, cwx2_ref, cbx2_ref) * (acc_scr[...] + z32_scr[...] * sk_ref[...])
        o_ref[...] = zn.astype(o_ref.dtype)


def _hyena_conv(u, conv_w, conv_b, fwd_tab, inv_tab, hr, gg, nyq, skip, fb, tc):
    b, n, _ = u.shape
    d = D_MODEL
    kb = n // fb
    ncb = d // tc
    ucol = lambda part: (lambda bi, c, o, k: (bi, 0, part * ncb + c))
    wcol = lambda part: (lambda bi, c, o, k: (0, part * ncb + c))
    spec = lambda bi, c, o, k: (o, k, c)
    per_o = lambda bi, c, o, k: (o, 0, c)
    return pl.pallas_call(
        functools.partial(_hyena_conv_kernel, fb=fb, kb=kb), grid=(b, ncb, 2, kb),
        in_specs=[pl.BlockSpec((None, n, tc), ucol(0)), pl.BlockSpec((None, n, tc), ucol(1)),
                  pl.BlockSpec((None, n, tc), ucol(2)),
                  pl.BlockSpec((3, tc), wcol(0)), pl.BlockSpec((3, tc), wcol(1)), pl.BlockSpec((3, tc), wcol(2)),
                  pl.BlockSpec((1, tc), wcol(0)), pl.BlockSpec((1, tc), wcol(1)), pl.BlockSpec((1, tc), wcol(2)),
                  pl.BlockSpec((None, 2 * fb, n), lambda bi, c, o, k: (k, 0, 0)),
                  pl.BlockSpec((None, n, 2 * fb), lambda bi, c, o, k: (k, 0, 0)),
                  pl.BlockSpec((None, fb, tc), spec), pl.BlockSpec((None, fb, tc), spec),
                  pl.BlockSpec((None, 1, tc), per_o), pl.BlockSpec((None, 1, tc), per_o)],
        out_specs=pl.BlockSpec((None, n, tc), lambda bi, c, o, k: (bi, 0, c)),
        out_shape=jax.ShapeDtypeStruct((b, n, d), BF16),
        scratch_shapes=[pltpu.VMEM((n, tc), BF16), pltpu.VMEM((n, tc), F32), pltpu.VMEM((n, tc), F32)],
        compiler_params=_cparams(4), name="hyena_conv",
    )(u, u, u, conv_w, conv_w, conv_w, conv_b, conv_b, conv_b, fwd_tab, inv_tab, hr, gg, nyq,
      skip.reshape(2, 1, d))


def kernel(x, c, ctx, c_ctx, mod_w, mod_b, ln_g, ln_b, attn_w_in, attn_lambda, attn_subln_g, attn_sink, attn_w_out,
           hy_w_in, hy_conv_w, hy_conv_b, hy_ffn_w_in, hy_ffn_w_hid, hy_ffn_b, hy_sin_freq, hy_ffn_w_out, hy_skip,
           hy_w_out, router_w, router_bias, exp_w_gate, exp_w_up, exp_w_down):
    b, n, d = x.shape
    nc = ctx.shape[1]
    t = b * n
    assert d == D_MODEL and b + 1 <= MOD_ROWS and n % 512 == 0 and nc % 256 == 0

    c_rows = jnp.zeros((MOD_ROWS, d), F32).at[:b].set(c).at[b].set(c_ctx)
    mods = _mods(c_rows, mod_w, mod_b)
    router_wt = router_w.T
    x2d = x.reshape(t, d)

    sh1, sc1, g1, sh2, sc2, g2 = jnp.split(mods[0], 6, axis=-1)
    cos, sin = _rope_tables(n)
    w_in = attn_w_in[0].astype(BF16)
    q_groups = tuple(range(Q_W // LANES))
    rope_groups = q_groups + tuple(range(Q_W // LANES, (Q_W + DIFF_QK_W) // LANES)) + (
        (Q_W + DIFF_QK_W + DIFF_VW) // LANES,)
    proj = _modmm(x2d, sc1[:b], sh1[:b], w_in, rows_per_mod=n, tm=512,
                  rope=(jnp.asarray(cos), jnp.asarray(sin), rope_groups, q_groups), name="attn_in_proj")
    proj_c = _modmm(ctx.reshape(b * nc, d), sc1[b:b + 1], sh1[b:b + 1], w_in[:, Q_W:], rows_per_mod=b * nc, tm=256,
                    name="ctx_in_proj")
    proj = proj.reshape(b, n, ATTN_PROJ_W)
    proj_c = proj_c.reshape(b, nc, KV_W)
    lam_init = 0.8 - 0.6 * math.exp(-0.3 * 0)
    oa = _diff_attn(proj, proj_c, attn_lambda[0], attn_subln_g[0], lam_init, tq=256)
    ow = _win_attn(proj, proj_c, attn_sink[0])
    w_out = attn_w_out[0].astype(BF16)
    x1, h2, lgt = _proj_ln([oa.reshape(t, DIFF_VW), ow.reshape(t, WIN_Q_W)], [w_out[:DIFF_VW], w_out[DIFF_VW:]],
                           x2d, g1[:b], ln_g[0, 0], ln_b[0, 0], sc2[:b], sh2[:b], router_wt, rows_per_mod=n, tm=512,
                           name="attn_out_proj_ln")
    x2 = _moe(h2, lgt, x1, g2[:b], ln_g[0, 1], ln_b[0, 1], router_bias, exp_w_gate, exp_w_up, exp_w_down, layer=0,
              rows_per_mod=n)

    sh1, sc1, g1, sh2, sc2, g2 = jnp.split(mods[1], 6, axis=-1)
    u = _modmm(x2, sc1[:b], sh1[:b], hy_w_in[0].astype(BF16), rows_per_mod=n, tm=512, name="hyena_in_proj")
    fb, tc = 256, 512
    fwd_np, inv_np = _dft_tables(n, fb)
    fwd_tab, inv_tab = jnp.asarray(fwd_np).astype(BF16), jnp.asarray(inv_np).astype(BF16)
    zp, decay = _filter_tables(n)
    w_in_p = jnp.zeros((FILTER_HID, FILTER_HID), F32).at[:FILTER_EMB].set(hy_ffn_w_in[0])
    hid = _filter_mlp(jnp.asarray(zp), w_in_p, hy_ffn_w_hid[0], hy_ffn_b[0], hy_sin_freq[0])
    hr, gg, nyq = _filter_spec(hid, hy_ffn_w_out[0].reshape(FILTER_HID, 2, 2, d).transpose(1, 2, 0, 3), jnp.asarray(decay), fwd_tab, fb, tc)
    z2 = _hyena_conv(u.reshape(b, n, 3 * d), hy_conv_w[0], hy_conv_b[0].reshape(1, 3 * d), fwd_tab, inv_tab,
                     hr, gg, nyq, hy_skip[0], fb, tc)
    x3, h2, lgt = _proj_ln([z2.reshape(t, d)], [hy_w_out[0].astype(BF16)], x2, g1[:b], ln_g[1, 0], ln_b[1, 0],
                           sc2[:b], sh2[:b], router_wt, rows_per_mod=n, tm=512, name="hyena_out_proj_ln")
    x4 = _moe(h2, lgt, x3, g2[:b], ln_g[1, 1], ln_b[1, 1], router_bias, exp_w_gate, exp_w_up, exp_w_down, layer=1,
              rows_per_mod=n)
    return x4.reshape(b, n, d)
```

```python
import functools
import math

import jax
import jax.numpy as jnp
import numpy as np
from jax import lax
from jax.experimental import pallas as pl
from jax.experimental.pallas import tpu as pltpu

F32 = jnp.float32
BF16 = jnp.bfloat16
I32 = jnp.int32

D_MODEL = 1024
DEPTH = 2
GRID_W = 64
HEAD_DIM = 64
DIFF_HEADS = 4
WIN_Q_HEADS = 8
WIN_KV_HEADS = 2
WINDOW = 128
WIN_BLOCK = 128
ROPE_BASE = 10000.0
DIFF_QK_W = DIFF_HEADS * 2 * HEAD_DIM
DIFF_VW = DIFF_HEADS * 2 * HEAD_DIM
WIN_Q_W = WIN_Q_HEADS * HEAD_DIM
WIN_KV_W = WIN_KV_HEADS * HEAD_DIM
Q_W = DIFF_QK_W + WIN_Q_W
KV_W = DIFF_QK_W + DIFF_VW + 2 * WIN_KV_W
ATTN_PROJ_W = Q_W + KV_W
FILTER_EMB = 33
FILTER_HID = 64
DECAY_TARGET = 1e-2
FAST_DECAY_PCT = 0.3
SLOW_DECAY_PCT = 1.5
N_EXPERTS = 16
N_GROUPS = 4
EXPERTS_PER_GROUP = N_EXPERTS // N_GROUPS
EXPERT_FF = 1024
LN_EPS = 1e-5
DEEPNORM_ALPHA = (2 * DEPTH) ** 0.25
NEG_INF = -1e30
LOG2_E = 1.4426950408889634

LANES = 128
MOD_ROWS = 16
VMEM_LIMIT = 60 * 1024 * 1024
MOE_BM = 256
MOE_TILE = 512
ROW_ALIGN = 8
NT_DIMS = (((1,), (1,)), ((), ()))


def _cparams(n_axes):
    return pltpu.CompilerParams(dimension_semantics=("arbitrary",) * n_axes, vmem_limit_bytes=VMEM_LIMIT)


@functools.lru_cache(maxsize=None)
def _rope_tables(n):
    rows = n // GRID_W
    r, col = np.meshgrid(np.arange(rows, dtype=np.float32), np.arange(GRID_W, dtype=np.float32), indexing="ij")
    axis_dim = HEAD_DIM // 2
    inv_freq = (ROPE_BASE ** (-np.arange(0, axis_dim, 2, dtype=np.float32) / axis_dim)).astype(np.float32)
    ang = np.concatenate([r.reshape(-1, 1) * inv_freq, col.reshape(-1, 1) * inv_freq], -1)
    ang = np.concatenate([ang, ang], -1).astype(np.float32)
    cos, sin = np.cos(ang), np.sin(ang)
    half = np.arange(HEAD_DIM) < HEAD_DIM // 2
    sin_signed = np.where(half[None, :], -sin, sin)
    reps = LANES // HEAD_DIM
    return (np.tile(cos, (1, reps)).astype(np.float32), np.tile(sin_signed, (1, reps)).astype(np.float32))


@functools.lru_cache(maxsize=None)
def _dft_tables(n, fb):
    big = 2 * n
    k = np.arange(n, dtype=np.int64)[:, None]
    t = np.arange(n, dtype=np.int64)[None, :]
    ang = ((k * t) % big).astype(np.float64) * (2.0 * math.pi / big)
    c, s = np.cos(ang), np.sin(ang)
    alt = (1 - 2 * (np.arange(n) & 1)).astype(np.float64)
    s_f = s.copy()
    s_f[0, :] = alt
    kb = n // fb
    fwd = np.concatenate([c.reshape(kb, fb, n), s_f.reshape(kb, fb, n)], axis=1)
    ci = c.T * (2.0 / big)
    ci[:, 0] = 1.0 / big
    si = s.T * (2.0 / big)
    si[:, 0] = alt / big
    inv = np.concatenate([ci.reshape(n, kb, fb).transpose(1, 0, 2), si.reshape(n, kb, fb).transpose(1, 0, 2)], axis=2)
    return fwd.astype(np.float32), inv.astype(np.float32)


@functools.lru_cache(maxsize=None)
def _filter_tables(n):
    t = np.linspace(0.0, 1.0, n, dtype=np.float32)[:, None]
    bands = (FILTER_EMB - 1) // 2
    w = (2.0 * math.pi * np.arange(n, dtype=np.float32)[:, None] / n).astype(np.float32)
    fr = np.linspace(1e-4, bands - 1, bands, dtype=np.float32)[None, :]
    z = np.concatenate([t, np.cos(fr * w), -np.sin(fr * w)], -1).astype(np.float32)
    zp = np.zeros((n, FILTER_HID), np.float32)
    zp[:, :FILTER_EMB] = z
    deltas = np.abs(np.linspace(math.log(DECAY_TARGET) / SLOW_DECAY_PCT, math.log(DECAY_TARGET) / FAST_DECAY_PCT,
                                D_MODEL, dtype=np.float32))
    decay = np.exp(-t * deltas[None, :]).astype(np.float32)
    return zp, decay


def _mods_kernel(c_ref, w_ref, b_ref, o_ref):
    c = c_ref[...]
    a = (c * jax.nn.sigmoid(c)).astype(BF16)
    o_ref[...] = jnp.dot(a, w_ref[...].astype(BF16), preferred_element_type=F32) + b_ref[...]


def _mods(c_rows, mod_w, mod_b):
    d = D_MODEL
    tn = 1536
    return pl.pallas_call(
        _mods_kernel,
        grid=(DEPTH, 6 * d // tn),
        in_specs=[pl.BlockSpec((MOD_ROWS, d), lambda l, j: (0, 0)),
                  pl.BlockSpec((None, d, tn), lambda l, j: (l, 0, j)),
                  pl.BlockSpec((None, 1, tn), lambda l, j: (l, 0, j))],
        out_specs=pl.BlockSpec((None, MOD_ROWS, tn), lambda l, j: (l, 0, j)),
        out_shape=jax.ShapeDtypeStruct((DEPTH, MOD_ROWS, 6 * d), F32),
        compiler_params=_cparams(2), name="mods",
    )(c_rows, mod_w, mod_b.reshape(DEPTH, 1, 6 * d))


def _modmm_kernel(*refs, n_groups, rope_groups, scaled_groups, chunk):
    if rope_groups:
        x_ref, sc_ref, sh_ref, w_ref, cos_ref, sin_ref, o_ref = refs
        cos, sin = cos_ref[...], sin_ref[...]
        lane = lax.broadcasted_iota(I32, (1, LANES), 1)
        first_half = (lane % HEAD_DIM) < HEAD_DIM // 2
    else:
        x_ref, sc_ref, sh_ref, w_ref, o_ref = refs
    h = (x_ref[...] * (1.0 + sc_ref[...]) + sh_ref[...]).astype(BF16)
    scales = dict(scaled_groups)
    gpc = chunk // LANES
    for c in range(n_groups // gpc):
        acc = jnp.dot(h, w_ref[:, c * chunk:(c + 1) * chunk], preferred_element_type=F32)
        for j in range(gpc):
            g = c * gpc + j
            blk = acc[:, j * LANES:(j + 1) * LANES]
            if g in rope_groups:
                rot = jnp.where(first_half, pltpu.roll(blk, LANES - HEAD_DIM // 2, 1), pltpu.roll(blk, HEAD_DIM // 2, 1))
                blk = blk * cos + rot * sin
                if g in scales:
                    blk = blk * scales[g]
            o_ref[:, g * LANES:(g + 1) * LANES] = blk.astype(o_ref.dtype)


def _modmm(x2d, sc, sh, w, rows_per_mod, tm, rope=None, name="modmm"):
    t, d = x2d.shape
    n = w.shape[1]
    tiles_per_mod = rows_per_mod // tm
    nmod = sc.shape[0]
    in_specs = [pl.BlockSpec((tm, d), lambda i: (i, 0)),
                pl.BlockSpec((None, 1, d), lambda i: (i // tiles_per_mod, 0, 0)),
                pl.BlockSpec((None, 1, d), lambda i: (i // tiles_per_mod, 0, 0)),
                pl.BlockSpec((d, n), lambda i: (0, 0))]
    args = [x2d, sc.reshape(nmod, 1, d), sh.reshape(nmod, 1, d), w]
    rope_groups, scaled_groups = (), ()
    if rope is not None:
        cos, sin, rope_groups, scaled_groups = rope
        in_specs += [pl.BlockSpec((tm, LANES), lambda i: (i % tiles_per_mod, 0)),
                     pl.BlockSpec((tm, LANES), lambda i: (i % tiles_per_mod, 0))]
        args += [cos, sin]
    kern = functools.partial(_modmm_kernel, n_groups=n // LANES, rope_groups=tuple(rope_groups),
                             scaled_groups=tuple(scaled_groups), chunk=256)
    return pl.pallas_call(
        kern, grid=(t // tm,), in_specs=in_specs,
        out_specs=pl.BlockSpec((tm, n), lambda i: (i, 0)),
        out_shape=jax.ShapeDtypeStruct((t, n), BF16),
        compiler_params=_cparams(1), name=name,
    )(*args)


def _diff_attn_kernel(q_ref, k_ref, v_ref, kc_ref, vc_ref, lam_ref, g_ref, o_ref, *, lam_init, sub):
    lane = lax.broadcasted_iota(I32, (1, LANES), 1)
    k, kc = k_ref[...], kc_ref[...]
    v, vc = v_ref[...], vc_ref[...]

    def scores(r0):
        q = q_ref[r0:r0 + sub, :]
        zero = jnp.zeros_like(q)
        out = []
        for qm in (jnp.where(lane < HEAD_DIM, q, zero), jnp.where(lane >= HEAD_DIM, q, zero)):
            out.append((lax.dot_general(qm, k, NT_DIMS, preferred_element_type=F32),
                        lax.dot_general(qm, kc, NT_DIMS, preferred_element_type=F32)))
        return out

    def probs(sl, sc):
        m = jnp.maximum(jnp.max(sl, -1, keepdims=True), jnp.max(sc, -1, keepdims=True))
        pl_, pc = jnp.exp2(sl - m), jnp.exp2(sc - m)
        den = jnp.sum(pl_, -1, keepdims=True) + jnp.sum(pc, -1, keepdims=True)
        return pl_, pc, 1.0 / den

    lv = lam_ref[...]
    lam = (jnp.exp(jnp.sum(lv[0:1] * lv[1:2], keepdims=True)) - jnp.exp(jnp.sum(lv[2:3] * lv[3:4], keepdims=True))
           + lam_init)
    starts = list(range(0, q_ref.shape[0], sub))
    nxt = scores(starts[0])
    for i, r0 in enumerate(starts):
        cur = nxt
        if i + 1 < len(starts):
            nxt = scores(starts[i + 1])
        p1l, p1c, r1 = probs(*cur[0])
        p2l, p2c, r2 = probs(*cur[1])
        w2 = lam * r2
        al = (p1l * r1 - p2l * w2).astype(BF16)
        ac = (p1c * r1 - p2c * w2).astype(BF16)
        o = jnp.dot(al, v, preferred_element_type=F32) + jnp.dot(ac, vc, preferred_element_type=F32)
        ms = jnp.mean(o * o, -1, keepdims=True)
        o_ref[r0:r0 + sub, :] = (o * lax.rsqrt(ms + LN_EPS) * g_ref[...] * (1.0 - lam_init)).astype(o_ref.dtype)


def _diff_attn(proj, proj_c, lam_vec, subln_g, lam_init, tq, sub):
    b, n, _ = proj.shape
    nc = proj_c.shape[1]
    kcol = Q_W // LANES
    vcol = (Q_W + DIFF_QK_W) // LANES
    vccol = DIFF_QK_W // LANES
    kern = functools.partial(_diff_attn_kernel, lam_init=lam_init, sub=sub)
    return pl.pallas_call(
        kern, grid=(b, DIFF_HEADS, n // tq),
        in_specs=[pl.BlockSpec((None, tq, LANES), lambda bi, h, i: (bi, i, h)),
                  pl.BlockSpec((None, n, LANES), lambda bi, h, i: (bi, 0, kcol + h)),
                  pl.BlockSpec((None, n, LANES), lambda bi, h, i: (bi, 0, vcol + h)),
                  pl.BlockSpec((None, nc, LANES), lambda bi, h, i: (bi, 0, h)),
                  pl.BlockSpec((None, nc, LANES), lambda bi, h, i: (bi, 0, vccol + h)),
                  pl.BlockSpec((4, HEAD_DIM), lambda bi, h, i: (0, 0)),
                  pl.BlockSpec((1, LANES), lambda bi, h, i: (0, 0))],
        out_specs=pl.BlockSpec((None, tq, LANES), lambda bi, h, i: (bi, i, h)),
        out_shape=jax.ShapeDtypeStruct((b, n, DIFF_VW), BF16),
        compiler_params=_cparams(3), name="diff_attn",
    )(proj, proj, proj, proj_c, proj_c, lam_vec, subln_g.reshape(1, LANES))


def _win_attn_kernel(sink_ref, q_ref, k_ref, v_ref, kc_ref, vc_ref, o_ref, *, seq, tq):
    n = pl.program_id(1)
    gq = WIN_Q_HEADS // WIN_KV_HEADS
    kw = tq + 2 * WINDOW
    start = pl.multiple_of(jnp.clip(n * tq - WINDOW, 0, seq - kw), WINDOW)
    k_win, v_win = k_ref[pl.ds(start, kw), :], v_ref[pl.ds(start, kw), :]
    kc, vc = kc_ref[...], vc_ref[...]
    lane = lax.broadcasted_iota(I32, (1, LANES), 1)
    row = lax.broadcasted_iota(I32, (gq * tq, 1), 0)
    q_abs = n * tq + row % tq
    k_abs = start + lax.broadcasted_iota(I32, (1, kw), 1)
    allowed = jnp.abs(q_abs - k_abs) <= WINDOW
    head = row // tq
    q = q_ref[...].astype(F32)
    for g in range(WIN_KV_HEADS):
        in_g = (lane // HEAD_DIM) == g
        parts = []
        for j in range(gq):
            hq = g * gq + j
            x = q[:, (hq // 2) * LANES:(hq // 2 + 1) * LANES]
            if hq % 2 != g:
                x = pltpu.roll(x, HEAD_DIM, 1)
            parts.append(jnp.where(in_g, x, 0.0))
        qs = jnp.concatenate(parts, 0).astype(BF16)
        s_loc = lax.dot_general(qs, k_win, NT_DIMS, preferred_element_type=F32)
        s_ctx = lax.dot_general(qs, kc, NT_DIMS, preferred_element_type=F32)
        s_loc = jnp.where(allowed, s_loc, NEG_INF)
        sk = jnp.zeros((gq * tq, 1), F32)
        for j in range(gq):
            sk = jnp.where(head == j, sink_ref[g * gq + j], sk)
        m = jnp.maximum(jnp.maximum(jnp.max(s_loc, -1, keepdims=True), jnp.max(s_ctx, -1, keepdims=True)), sk)
        p_loc, p_ctx = jnp.exp(s_loc - m), jnp.exp(s_ctx - m)
        den = jnp.sum(p_loc, -1, keepdims=True) + jnp.sum(p_ctx, -1, keepdims=True) + jnp.exp(sk - m)
        o = (jnp.dot(p_loc.astype(BF16), v_win, preferred_element_type=F32)
             + jnp.dot(p_ctx.astype(BF16), vc, preferred_element_type=F32)) * (1.0 / den)
        for cb in range(gq // 2):
            pair = []
            for half in range(2):
                piece = o[(2 * cb + half) * tq:(2 * cb + half + 1) * tq]
                pair.append(piece if half == g else pltpu.roll(piece, HEAD_DIM, 1))
            col = (g * gq // 2 + cb) * LANES
            o_ref[:, col:col + LANES] = jnp.where(lane < HEAD_DIM, pair[0], pair[1]).astype(o_ref.dtype)


def _win_attn(proj, proj_c, sink, tq):
    b, n, _ = proj.shape
    nc = proj_c.shape[1]
    qcol = DIFF_QK_W // WIN_Q_W
    kcol = (Q_W + DIFF_QK_W + DIFF_VW) // LANES
    kccol = (DIFF_QK_W + DIFF_VW) // LANES
    kern = functools.partial(_win_attn_kernel, seq=n, tq=tq)
    return pl.pallas_call(
        kern, grid=(b, n // tq),
        in_specs=[pl.BlockSpec(memory_space=pltpu.SMEM),
                  pl.BlockSpec((None, tq, WIN_Q_W), lambda bi, i: (bi, i, qcol)),
                  pl.BlockSpec((None, n, LANES), lambda bi, i: (bi, 0, kcol)),
                  pl.BlockSpec((None, n, LANES), lambda bi, i: (bi, 0, kcol + 1)),
                  pl.BlockSpec((None, nc, LANES), lambda bi, i: (bi, 0, kccol)),
                  pl.BlockSpec((None, nc, LANES), lambda bi, i: (bi, 0, kccol + 1))],
        out_specs=pl.BlockSpec((None, tq, WIN_Q_W), lambda bi, i: (bi, i, 0)),
        out_shape=jax.ShapeDtypeStruct((b, n, WIN_Q_W), BF16),
        compiler_params=_cparams(2), name="win_attn",
    )(sink, proj, proj, proj, proj_c, proj_c)


def _layer_norm(r, g, b):
    mu = jnp.mean(r, -1, keepdims=True)
    dlt = r - mu
    var = jnp.mean(dlt * dlt, -1, keepdims=True)
    return dlt * lax.rsqrt(var + LN_EPS) * g + b


def _proj_ln_kernel(*refs, n_in):
    a_refs = refs[:n_in]
    w_refs = refs[n_in:2 * n_in]
    x_ref, gate_ref, lng_ref, lnb_ref, sc_ref, sh_ref, rw_ref, x1_ref, h2_ref, lg_ref = refs[2 * n_in:]
    y = jnp.dot(a_refs[0][...], w_refs[0][...], preferred_element_type=F32)
    for a_ref, w_ref in zip(a_refs[1:], w_refs[1:]):
        y = y + jnp.dot(a_ref[...], w_ref[...], preferred_element_type=F32)
    xn = _layer_norm(DEEPNORM_ALPHA * x_ref[...] + gate_ref[...] * y, lng_ref[...], lnb_ref[...])
    x1_ref[...] = xn
    h2 = xn * (1.0 + sc_ref[...]) + sh_ref[...]
    h2_ref[...] = h2.astype(h2_ref.dtype)
    lg_ref[...] = lax.dot_general(rw_ref[...], h2, NT_DIMS, precision=lax.Precision.HIGHEST,
                                  preferred_element_type=F32)


def _proj_ln(acts, ws, x2d, gate, ln_g, ln_b, sc, sh, router_wt, rows_per_mod, tm, name):
    t, d = x2d.shape
    n_in = len(acts)
    tiles_per_mod = rows_per_mod // tm
    nmod = gate.shape[0]
    row = lambda i: (i, 0)
    full = lambda i: (0, 0)
    mod = lambda i: (i // tiles_per_mod, 0, 0)
    in_specs = ([pl.BlockSpec((tm, a.shape[1]), row) for a in acts]
                + [pl.BlockSpec(w.shape, full) for w in ws]
                + [pl.BlockSpec((tm, d), row), pl.BlockSpec((None, 1, d), mod),
                   pl.BlockSpec((1, d), full), pl.BlockSpec((1, d), full),
                   pl.BlockSpec((None, 1, d), mod), pl.BlockSpec((None, 1, d), mod),
                   pl.BlockSpec((N_EXPERTS, d), full)])
    return pl.pallas_call(
        functools.partial(_proj_ln_kernel, n_in=n_in), grid=(t // tm,), in_specs=in_specs,
        out_specs=[pl.BlockSpec((tm, d), row), pl.BlockSpec((tm, d), row),
                   pl.BlockSpec((N_EXPERTS, tm), lambda i: (0, i))],
        out_shape=[jax.ShapeDtypeStruct((t, d), F32), jax.ShapeDtypeStruct((t, d), BF16),
                   jax.ShapeDtypeStruct((N_EXPERTS, t), F32)],
        compiler_params=_cparams(1), name=name,
    )(*acts, *ws, x2d, gate.reshape(nmod, 1, d), ln_g.reshape(1, d), ln_b.reshape(1, d),
      sc.reshape(nmod, 1, d), sh.reshape(nmod, 1, d), router_wt)


def _first_argmax(vals):
    idx = jnp.zeros(vals[0].shape, I32)
    best = vals[0]
    for j in range(1, len(vals)):
        upd = vals[j] > best
        idx = jnp.where(upd, j, idx)
        best = jnp.where(upd, vals[j], best)
    return idx, best


def _route_kernel(lg_ref, bias_ref, pos_ref, w_ref, tab_ref, cnt_ref, carry_ref, off_ref, *, tr):
    @pl.when(pl.program_id(0) == 0)
    def _():
        carry_ref[...] = jnp.zeros_like(carry_ref)

    lg = lg_ref[...]
    ex = jnp.exp(lg - jnp.max(lg, 0, keepdims=True))
    scores = ex / jnp.sum(ex, 0, keepdims=True)
    sel = scores + bias_ref[...]
    rows = [sel[e:e + 1] for e in range(N_EXPERTS)]
    group_scores = []
    for g in range(N_GROUPS):
        r = rows[g * EXPERTS_PER_GROUP:(g + 1) * EXPERTS_PER_GROUP]
        best = None
        for i in range(EXPERTS_PER_GROUP):
            for j in range(i + 1, EXPERTS_PER_GROUP):
                s = r[i] + r[j]
                best = s if best is None else jnp.maximum(best, s)
        group_scores.append(best)
    grp, _ = _first_argmax(group_scores)
    vals = []
    for j in range(EXPERTS_PER_GROUP):
        v = rows[(N_GROUPS - 1) * EXPERTS_PER_GROUP + j]
        for g in range(N_GROUPS - 2, -1, -1):
            v = jnp.where(grp == g, rows[g * EXPERTS_PER_GROUP + j], v)
        vals.append(v)
    i0, _ = _first_argmax(vals)
    i1, _ = _first_argmax([jnp.where(i0 == j, -jnp.inf, vals[j]) for j in range(EXPERTS_PER_GROUP)])
    e0 = grp * EXPERTS_PER_GROUP + i0
    e1 = grp * EXPERTS_PER_GROUP + i1
    eid = lax.broadcasted_iota(I32, (N_EXPERTS, 1), 0)
    oh0, oh1 = eid == e0, eid == e1
    s0 = jnp.sum(jnp.where(oh0, scores, 0.0), 0, keepdims=True)
    s1 = jnp.sum(jnp.where(oh1, scores, 0.0), 0, keepdims=True)
    den = s0 + s1
    member = jnp.where(oh0 | oh1, 1.0, 0.0)
    before = lax.broadcasted_iota(I32, (tr, tr), 0) < lax.broadcasted_iota(I32, (tr, tr), 1)
    upper = jnp.where(before, 1.0, 0.0).astype(BF16)
    cnt = jnp.dot(member.astype(BF16), upper, preferred_element_type=F32)
    run = jnp.sum(member, 1, keepdims=True)
    run = jnp.floor((run + (ROW_ALIGN - 1)) * (1.0 / ROW_ALIGN)) * ROW_ALIGN
    run = jnp.broadcast_to(run, (N_EXPERTS, LANES))
    acc = jnp.zeros((1, LANES), F32)
    for e in range(N_EXPERTS):
        off_ref[e:e + 1, :] = acc
        acc = acc + run[e:e + 1]
    off = off_ref[...]
    at = off[:, 0:1] + cnt
    pos_ref[0:1, :] = jnp.sum(jnp.where(oh0, at, 0.0), 0, keepdims=True).astype(I32)
    pos_ref[1:2, :] = jnp.sum(jnp.where(oh1, at, 0.0), 0, keepdims=True).astype(I32)
    w_ref[0:1, :] = s0 / den
    w_ref[1:2, :] = s1 / den
    tab_ref[0] = off.astype(I32)
    tab_ref[1] = run.astype(I32)
    tab_ref[2] = carry_ref[...].astype(I32)
    carry_ref[...] = carry_ref[...] + run
    cnt_ref[...] = carry_ref[...]


def _route(logits_t, router_bias, tr):
    t = logits_t.shape[1]
    tok = lambda i: (0, i)
    return pl.pallas_call(
        functools.partial(_route_kernel, tr=tr), grid=(t // tr,),
        in_specs=[pl.BlockSpec((N_EXPERTS, tr), tok), pl.BlockSpec((N_EXPERTS, 1), lambda i: (0, 0))],
        out_specs=[pl.BlockSpec((2, tr), tok), pl.BlockSpec((2, tr), tok),
                   pl.BlockSpec((None, 3, N_EXPERTS, LANES), lambda i: (i, 0, 0, 0)),
                   pl.BlockSpec((N_EXPERTS, LANES), lambda i: (0, 0))],
        out_shape=[jax.ShapeDtypeStruct((2, t), I32), jax.ShapeDtypeStruct((2, t), F32),
                   jax.ShapeDtypeStruct((t // tr, 3, N_EXPERTS, LANES), I32),
                   jax.ShapeDtypeStruct((N_EXPERTS, LANES), F32)],
        scratch_shapes=[pltpu.VMEM((N_EXPERTS, LANES), F32), pltpu.VMEM((N_EXPERTS, LANES), F32)],
        compiler_params=_cparams(1), name="route",
    )(logits_t, router_bias.reshape(N_EXPERTS, 1))


def _run_copies(n, max_rows, make_copy, wait):
    sz = max_rows
    while sz >= ROW_ALIGN:
        start = (n // (2 * sz)) * (2 * sz)

        @pl.when((n & sz) != 0)
        def _(start=start, sz=sz):
            cp = make_copy(start, sz)
            cp.wait() if wait else cp.start()

        sz //= 2


def _rows(ref, start, size):
    return ref.at[pl.ds(pl.multiple_of(start, ROW_ALIGN), size)]


def _dispatch_kernel(tab_ref, h_ref, pos_ref, w_ref, xs_ref, buf, zbuf, sem, *, tt, n_tiles, bm):
    i = pl.program_id(0)
    d = h_ref.shape[1]
    sb = buf.shape[0]
    ne = N_EXPERTS
    fill = 3 * n_tiles * ne

    @pl.when(i == 0)
    def _():
        zbuf[...] = jnp.zeros_like(zbuf)
        for wait in (False, True):
            for e in range(ne):
                dst, n = tab_ref[fill + e], tab_ref[fill + ne + e]
                _run_copies(n, bm // 2, lambda s, z: pltpu.make_async_copy(
                    zbuf.at[pl.ds(0, z)], _rows(xs_ref, dst + s, z), sem), wait)

        def zero_block(j, carry):
            for half in range(2):
                cp = pltpu.make_async_copy(zbuf, _rows(xs_ref, j * bm + half * (bm // 2), bm // 2), sem)
                cp.start()
                cp.wait()
            return carry

        lax.fori_loop(tab_ref[fill + 2 * ne], xs_ref.shape[0] // bm, zero_block, 0)

    pos, w = pos_ref[...], w_ref[...]
    srow = lax.broadcasted_iota(I32, (sb, 1), 0)
    m0, m1 = srow == pos[0:1], srow == pos[1:2]
    perm = (jnp.where(m0, 1.0, 0.0) + jnp.where(m1, 1.0, 0.0)).astype(BF16)
    buf[:, :d] = jnp.dot(perm, h_ref[...].astype(BF16), preferred_element_type=F32)
    wrow = jnp.sum(jnp.where(m0, w[0:1], 0.0) + jnp.where(m1, w[1:2], 0.0), 1, keepdims=True)
    buf[:, d:] = jnp.broadcast_to(wrow, (sb, LANES))
    for wait in (False, True):
        for e in range(ne):
            off = tab_ref[i * ne + e]
            n = tab_ref[(n_tiles + i) * ne + e]
            dst = tab_ref[(2 * n_tiles + i) * ne + e]
            _run_copies(n, tt, lambda s, z: pltpu.make_async_copy(
                _rows(buf, off + s, z), _rows(xs_ref, dst + s, z), sem), wait)


def _dispatch(tab, h2, pos, wts, n_slots, tt, sb, bm):
    t, d = h2.shape
    n_tiles = t // tt
    return pl.pallas_call(
        functools.partial(_dispatch_kernel, tt=tt, n_tiles=n_tiles, bm=bm),
        grid_spec=pltpu.PrefetchScalarGridSpec(
            num_scalar_prefetch=1, grid=(n_tiles,),
            in_specs=[pl.BlockSpec((tt, d), lambda i, s: (i, 0)), pl.BlockSpec((2, tt), lambda i, s: (0, i)),
                      pl.BlockSpec((2, tt), lambda i, s: (0, i))],
            out_specs=pl.BlockSpec(memory_space=pl.ANY),
            scratch_shapes=[pltpu.VMEM((sb, d + LANES), F32), pltpu.VMEM((bm // 2, d + LANES), F32),
                            pltpu.SemaphoreType.DMA(())]),
        out_shape=jax.ShapeDtypeStruct((n_slots, d + LANES), F32),
        compiler_params=_cparams(1), name="moe_dispatch",
    )(tab, h2, pos, wts)


def _ffn_kernel(blk_e_ref, nused_ref, x_ref, wg_ref, wu_ref, wd_ref, o_ref, wg_bf, wu_bf, wd_bf):
    i = pl.program_id(0)
    new_expert = (i == 0) | (blk_e_ref[i] != blk_e_ref[jnp.maximum(i - 1, 0)])

    @pl.when(new_expert)
    def _():
        wg_bf[...] = wg_ref[...].astype(BF16)
        wu_bf[...] = wu_ref[...].astype(BF16)
        wd_bf[...] = wd_ref[...].astype(BF16)

    @pl.when(i < nused_ref[0])
    def _():
        d = wg_bf.shape[0]
        x = x_ref[:, :d].astype(BF16)
        gate = jnp.dot(x, wg_bf[...], preferred_element_type=F32)
        up = jnp.dot(x, wu_bf[...], preferred_element_type=F32)
        act = (gate * jax.nn.sigmoid(gate) * up).astype(BF16)
        o_ref[...] = jnp.dot(act, wd_bf[...], preferred_element_type=F32) * x_ref[:, d:d + 1]

    @pl.when(i >= nused_ref[0])
    def _():
        o_ref[...] = jnp.zeros_like(o_ref)


def _ffn(blk_e, n_used, xs, wg, wu, wd, layer, bm):
    n_slots, xw = xs.shape
    d, ff = wg.shape[2:]
    xrow = lambda i, be, nu: (jnp.minimum(i, nu[0] - 1), 0)
    wsel = lambda i, be, nu: (layer, be[i], 0, 0)
    return pl.pallas_call(
        _ffn_kernel,
        grid_spec=pltpu.PrefetchScalarGridSpec(
            num_scalar_prefetch=2, grid=(n_slots // bm,),
            in_specs=[pl.BlockSpec((bm, xw), xrow), pl.BlockSpec((None, None, d, ff), wsel),
                      pl.BlockSpec((None, None, d, ff), wsel), pl.BlockSpec((None, None, ff, d), wsel)],
            out_specs=pl.BlockSpec((bm, d), lambda i, be, nu: (i, 0)),
            scratch_shapes=[pltpu.VMEM((d, ff), BF16), pltpu.VMEM((d, ff), BF16), pltpu.VMEM((ff, d), BF16)]),
        out_shape=jax.ShapeDtypeStruct((n_slots, d), F32),
        compiler_params=_cparams(1), name="moe_ffn",
    )(blk_e, n_used, xs, wg, wu, wd)


def _combine_kernel(tab_ref, ys_ref, pos_ref, x1_ref, gate_ref, lng_ref, lnb_ref, o_ref, ybuf, sem, *, tt, n_tiles):
    i = pl.program_id(0)
    sb = ybuf.shape[0]
    ne = N_EXPERTS

    @pl.when(i == 0)
    def _():
        ybuf[...] = jnp.zeros_like(ybuf)

    for wait in (False, True):
        for e in range(ne):
            off = tab_ref[i * ne + e]
            n = tab_ref[(n_tiles + i) * ne + e]
            src = tab_ref[(2 * n_tiles + i) * ne + e]
            _run_copies(n, tt, lambda s, z: pltpu.make_async_copy(
                _rows(ys_ref, src + s, z), _rows(ybuf, off + s, z), sem), wait)
    pos = pos_ref[...]
    scol = lax.broadcasted_iota(I32, (1, sb), 1)
    unperm = (jnp.where(scol == pos[:, 0:1], 1.0, 0.0) + jnp.where(scol == pos[:, 1:2], 1.0, 0.0)).astype(BF16)
    f = jnp.dot(unperm, ybuf[...].astype(BF16), preferred_element_type=F32)
    o_ref[...] = _layer_norm(DEEPNORM_ALPHA * x1_ref[...] + gate_ref[...] * f, lng_ref[...], lnb_ref[...])


def _combine(tab, ys, pos_t, x1, gate, ln_g, ln_b, rows_per_mod, tt, sb):
    t, d = x1.shape
    tiles_per_mod = rows_per_mod // tt
    nmod = gate.shape[0]
    return pl.pallas_call(
        functools.partial(_combine_kernel, tt=tt, n_tiles=t // tt),
        grid_spec=pltpu.PrefetchScalarGridSpec(
            num_scalar_prefetch=1, grid=(t // tt,),
            in_specs=[pl.BlockSpec(memory_space=pl.ANY),
                      pl.BlockSpec((tt, 2), lambda i, s: (i, 0)),
                      pl.BlockSpec((tt, d), lambda i, s: (i, 0)),
                      pl.BlockSpec((None, 1, d), lambda i, s: (i // tiles_per_mod, 0, 0)),
                      pl.BlockSpec((1, d), lambda i, s: (0, 0)), pl.BlockSpec((1, d), lambda i, s: (0, 0))],
            out_specs=pl.BlockSpec((tt, d), lambda i, s: (i, 0)),
            scratch_shapes=[pltpu.VMEM((sb, d), F32), pltpu.SemaphoreType.DMA(())]),
        out_shape=jax.ShapeDtypeStruct((t, d), F32),
        compiler_params=_cparams(1), name="moe_combine",
    )(tab, ys, pos_t, x1, gate.reshape(nmod, 1, d), ln_g.reshape(1, d), ln_b.reshape(1, d))


def _moe(h2, logits_t, x1, gate, ln_g, ln_b, router_bias, wg, wu, wd, layer, rows_per_mod):
    t, d = h2.shape
    bm, tt, ne = MOE_BM, MOE_TILE, N_EXPERTS
    n_tiles = t // tt
    pad = ne * (ROW_ALIGN - 1)
    sb = -(-(2 * tt + pad) // LANES) * LANES
    pos, wts, tab3, cnt = _route(logits_t, router_bias, tr=tt)
    rows = cnt[:, 0].astype(I32)
    prows = (rows + bm - 1) // bm * bm
    pends = jnp.cumsum(prows)
    pstarts = pends - prows
    n_blk = -(-(2 * t + n_tiles * pad + ne * (bm - 1)) // bm)
    n_used = pends[-1] // bm
    blk_ids = jnp.minimum(jnp.arange(n_blk, dtype=I32), n_used - 1)
    blk_e = jnp.minimum(jnp.sum((blk_ids[:, None] * bm >= pends[None, :]).astype(I32), 1), ne - 1)
    tab3 = tab3[:, :, :, 0]
    tab = jnp.concatenate([tab3[:, 0].reshape(-1), tab3[:, 1].reshape(-1),
                           (tab3[:, 2] + pstarts[None, :]).reshape(-1), pstarts + rows, prows - rows,
                           n_used.reshape(1)]).astype(I32)
    xs = _dispatch(tab, h2, pos, wts, n_blk * bm, tt, sb, bm)
    ys = _ffn(blk_e, n_used.reshape(1).astype(I32), xs, wg, wu, wd, layer, bm)
    return _combine(tab, ys, pos.T, x1, gate, ln_g, ln_b, rows_per_mod, tt, sb)


def _filter_mlp_kernel(z_ref, w1_ref, w2_ref, w3_ref, b_ref, sf_ref, o_ref):
    hp = lax.Precision.HIGHEST
    b, sf = b_ref[...], sf_ref[...]
    h = jnp.sin(sf[0:1] * (jnp.dot(z_ref[...], w1_ref[...], precision=hp, preferred_element_type=F32) + b[0:1]))
    h = jnp.sin(sf[1:2] * (jnp.dot(h, w2_ref[...], precision=hp, preferred_element_type=F32) + b[1:2]))
    o_ref[...] = jnp.sin(sf[2:3] * (jnp.dot(h, w3_ref[...], precision=hp, preferred_element_type=F32) + b[2:3]))


def _filter_mlp(zp, w_in_p, w_hid, b, sf):
    n = zp.shape[0]
    return pl.pallas_call(
        _filter_mlp_kernel, out_shape=jax.ShapeDtypeStruct((n, FILTER_HID), F32),
        compiler_params=pltpu.CompilerParams(vmem_limit_bytes=VMEM_LIMIT), name="hyena_filter_mlp",
    )(zp, w_in_p, w_hid[0], w_hid[1], b, sf)


def _filter_spec_kernel(hid_ref, wf_ref, wb_ref, dec_ref, fwd_ref, hr_ref, g_ref, nyq_ref, p_scr, q_scr, *, fb):
    k = pl.program_id(2)
    n = hid_ref.shape[0]

    @pl.when(k == 0)
    def _():
        hp = lax.Precision.HIGHEST
        hid, dec = hid_ref[...], dec_ref[...]
        row = lax.broadcasted_iota(I32, (n, 1), 0)
        fw = jnp.dot(hid, wf_ref[...], precision=hp, preferred_element_type=F32) * dec
        bw = jnp.dot(hid, wb_ref[...], precision=hp, preferred_element_type=F32) * dec
        bw = jnp.where(row == 0, 0.0, bw)
        p = fw + bw
        p_scr[...] = p.astype(BF16)
        q_scr[...] = (fw - bw).astype(BF16)
        sign = (1 - 2 * (row & 1)).astype(F32)
        nyq_ref[...] = jnp.sum(p * sign, 0, keepdims=True)

    hr_ref[...] = jnp.dot(fwd_ref[0:fb, :], p_scr[...], preferred_element_type=F32)
    gg = jnp.dot(fwd_ref[fb:2 * fb, :], q_scr[...], preferred_element_type=F32)
    rowb = lax.broadcasted_iota(I32, (fb, 1), 0)
    g_ref[...] = jnp.where((rowb == 0) & (k == 0), 0.0, gg)


def _filter_spec(hid, w_out4, decay, fwd_tab, fb, tc):
    n = hid.shape[0]
    d = D_MODEL
    kb = n // fb
    return pl.pallas_call(
        functools.partial(_filter_spec_kernel, fb=fb), grid=(2, d // tc, kb),
        in_specs=[pl.BlockSpec((n, FILTER_HID), lambda o, c, k: (0, 0)),
                  pl.BlockSpec((None, None, FILTER_HID, tc), lambda o, c, k: (0, o, 0, c)),
                  pl.BlockSpec((None, None, FILTER_HID, tc), lambda o, c, k: (1, o, 0, c)),
                  pl.BlockSpec((n, tc), lambda o, c, k: (0, c)),
                  pl.BlockSpec((None, 2 * fb, n), lambda o, c, k: (k, 0, 0))],
        out_specs=[pl.BlockSpec((None, fb, tc), lambda o, c, k: (o, k, c)),
                   pl.BlockSpec((None, fb, tc), lambda o, c, k: (o, k, c)),
                   pl.BlockSpec((None, 1, tc), lambda o, c, k: (o, 0, c))],
        out_shape=[jax.ShapeDtypeStruct((2, n, d), F32), jax.ShapeDtypeStruct((2, n, d), F32),
                   jax.ShapeDtypeStruct((2, 1, d), F32)],
        scratch_shapes=[pltpu.VMEM((n, tc), BF16), pltpu.VMEM((n, tc), BF16)],
        compiler_params=_cparams(3), name="hyena_filter_spec",
    )(hid, w_out4, w_out4, decay, fwd_tab)


def _short_conv(u_ref, cw_ref, cb_ref):
    n = u_ref.shape[0]
    u = u_ref[...].astype(F32)
    row = lax.broadcasted_iota(I32, (n, 1), 0)
    prev = jnp.where(row == 0, 0.0, pltpu.roll(u, 1, 0))
    nxt = jnp.where(row == n - 1, 0.0, pltpu.roll(u, n - 1, 0))
    cw = cw_ref[...]
    return prev * cw[0:1] + u * cw[1:2] + nxt * cw[2:3] + cb_ref[...]


def _hyena_conv_kernel(uv_ref, ux1_ref, ux2_ref, cwv_ref, cwx1_ref, cwx2_ref, cbv_ref, cbx1_ref, cbx2_ref,
                       fwd_ref, inv_ref, hr_ref, g_ref, nyq_ref, sk_ref, o_ref, zb_scr, z32_scr, acc_scr, *, fb, kb):
    o = pl.program_id(2)
    k = pl.program_id(3)

    @pl.when((o == 0) & (k == 0))
    def _():
        v = _short_conv(uv_ref, cwv_ref, cbv_ref)
        z32_scr[...] = v
        zb_scr[...] = v.astype(BF16)

    @pl.when(k == 0)
    def _():
        acc_scr[...] = jnp.zeros_like(acc_scr)

    zf = jnp.dot(fwd_ref[...], zb_scr[...], preferred_element_type=F32)
    zr, zi = zf[:fb], zf[fb:]
    hr, gg = hr_ref[...], g_ref[...]
    rowb = lax.broadcasted_iota(I32, (fb, 1), 0)
    hb = jnp.where((rowb == 0) & (k == 0), nyq_ref[...], hr)
    y = jnp.concatenate([zr * hr - zi * gg, zr * gg + zi * hb], 0).astype(BF16)
    acc_scr[...] += jnp.dot(inv_ref[...], y, preferred_element_type=F32)

    @pl.when((k == kb - 1) & (o == 0))
    def _():
        zn = _short_conv(ux1_ref, cwx1_ref, cbx1_ref) * (acc_scr[...] + z32_scr[...] * sk_ref[...])
        z32_scr[...] = zn
        zb_scr[...] = zn.astype(BF16)

    @pl.when((k == kb - 1) & (o == 1))
    def _():
        zn = _short_conv(ux2_ref, cwx2_ref, cbx2_ref) * (acc_scr[...] + z32_scr[...] * sk_ref[...])
        o_ref[...] = zn.astype(o_ref.dtype)


def _hyena_conv(u, conv_w, conv_b, fwd_tab, inv_tab, hr, gg, nyq, skip, fb, tc):
    b, n, _ = u.shape
    d = D_MODEL
    kb = n // fb
    ncb = d // tc
    ucol = lambda part: (lambda bi, c, o, k: (bi, 0, part * ncb + c))
    wcol = lambda part: (lambda bi, c, o, k: (0, part * ncb + c))
    spec = lambda bi, c, o, k: (o, k, c)
    per_o = lambda bi, c, o, k: (o, 0, c)
    return pl.pallas_call(
        functools.partial(_hyena_conv_kernel, fb=fb, kb=kb), grid=(b, ncb, 2, kb),
        in_specs=[pl.BlockSpec((None, n, tc), ucol(0)), pl.BlockSpec((None, n, tc), ucol(1)),
                  pl.BlockSpec((None, n, tc), ucol(2)),
                  pl.BlockSpec((3, tc), wcol(0)), pl.BlockSpec((3, tc), wcol(1)), pl.BlockSpec((3, tc), wcol(2)),
                  pl.BlockSpec((1, tc), wcol(0)), pl.BlockSpec((1, tc), wcol(1)), pl.BlockSpec((1, tc), wcol(2)),
                  pl.BlockSpec((None, 2 * fb, n), lambda bi, c, o, k: (k, 0, 0)),
                  pl.BlockSpec((None, n, 2 * fb), lambda bi, c, o, k: (k, 0, 0)),
                  pl.BlockSpec((None, fb, tc), spec), pl.BlockSpec((None, fb, tc), spec),
                  pl.BlockSpec((None, 1, tc), per_o), pl.BlockSpec((None, 1, tc), per_o)],
        out_specs=pl.BlockSpec((None, n, tc), lambda bi, c, o, k: (bi, 0, c)),
        out_shape=jax.ShapeDtypeStruct((b, n, d), BF16),
        scratch_shapes=[pltpu.VMEM((n, tc), BF16), pltpu.VMEM((n, tc), F32), pltpu.VMEM((n, tc), F32)],
        compiler_params=_cparams(4), name="hyena_conv",
    )(u, u, u, conv_w, conv_w, conv_w, conv_b, conv_b, conv_b, fwd_tab, inv_tab, hr, gg, nyq,
      skip.reshape(2, 1, d))


def kernel(x, c, ctx, c_ctx, mod_w, mod_b, ln_g, ln_b, attn_w_in, attn_lambda, attn_subln_g, attn_sink, attn_w_out,
           hy_w_in, hy_conv_w, hy_conv_b, hy_ffn_w_in, hy_ffn_w_hid, hy_ffn_b, hy_sin_freq, hy_ffn_w_out, hy_skip,
           hy_w_out, router_w, router_bias, exp_w_gate, exp_w_up, exp_w_down):
    b, n, d = x.shape
    nc = ctx.shape[1]
    t = b * n
    assert d == D_MODEL and b + 1 <= MOD_ROWS and n % 512 == 0 and nc % 256 == 0

    c_rows = jnp.zeros((MOD_ROWS, d), F32).at[:b].set(c).at[b].set(c_ctx)
    mods = _mods(c_rows, mod_w, mod_b)
    router_wt = router_w.T
    x2d = x.reshape(t, d)

    sh1, sc1, g1, sh2, sc2, g2 = jnp.split(mods[0], 6, axis=-1)
    cos, sin = _rope_tables(n)
    w_in = attn_w_in[0].astype(BF16)
    q_groups = tuple(range(Q_W // LANES))
    rope_groups = q_groups + tuple(range(Q_W // LANES, (Q_W + DIFF_QK_W) // LANES)) + (
        (Q_W + DIFF_QK_W + DIFF_VW) // LANES,)
    q_scales = tuple((g, HEAD_DIM ** -0.5 * (LOG2_E if g < DIFF_QK_W // LANES else 1.0)) for g in q_groups)
    proj = _modmm(x2d, sc1[:b], sh1[:b], w_in, rows_per_mod=n, tm=512,
                  rope=(jnp.asarray(cos), jnp.asarray(sin), rope_groups, q_scales), name="attn_in_proj")
    proj_c = _modmm(ctx.reshape(b * nc, d), sc1[b:b + 1], sh1[b:b + 1], w_in[:, Q_W:], rows_per_mod=b * nc, tm=256,
                    name="ctx_in_proj")
    proj = proj.reshape(b, n, ATTN_PROJ_W)
    proj_c = proj_c.reshape(b, nc, KV_W)
    lam_init = 0.8 - 0.6 * math.exp(-0.3 * 0)
    oa = _diff_attn(proj, proj_c, attn_lambda[0], attn_subln_g[0], lam_init, tq=512, sub=128)
    ow = _win_attn(proj, proj_c, attn_sink[0], tq=256)
    w_out = attn_w_out[0].astype(BF16)
    x1, h2, lgt = _proj_ln([oa.reshape(t, DIFF_VW), ow.reshape(t, WIN_Q_W)], [w_out[:DIFF_VW], w_out[DIFF_VW:]],
                           x2d, g1[:b], ln_g[0, 0], ln_b[0, 0], sc2[:b], sh2[:b], router_wt, rows_per_mod=n, tm=512,
                           name="attn_out_proj_ln")
    x2 = _moe(h2, lgt, x1, g2[:b], ln_g[0, 1], ln_b[0, 1], router_bias, exp_w_gate, exp_w_up, exp_w_down, layer=0,
              rows_per_mod=n)

    sh1, sc1, g1, sh2, sc2, g2 = jnp.split(mods[1], 6, axis=-1)
    u = _modmm(x2, sc1[:b], sh1[:b], hy_w_in[0].astype(BF16), rows_per_mod=n, tm=512, name="hyena_in_proj")
    fb, tc = 256, 512
    fwd_np, inv_np = _dft_tables(n, fb)
    fwd_tab, inv_tab = jnp.asarray(fwd_np).astype(BF16), jnp.asarray(inv_np).astype(BF16)
    zp, decay = _filter_tables(n)
    w_in_p = jnp.zeros((FILTER_HID, FILTER_HID), F32).at[:FILTER_EMB].set(hy_ffn_w_in[0])
    hid = _filter_mlp(jnp.asarray(zp), w_in_p, hy_ffn_w_hid[0], hy_ffn_b[0], hy_sin_freq[0])
    hr, gg, nyq = _filter_spec(hid, hy_ffn_w_out[0].reshape(FILTER_HID, 2, 2, d).transpose(1, 2, 0, 3), jnp.asarray(decay), fwd_tab, fb, tc)
    z2 = _hyena_conv(u.reshape(b, n, 3 * d), hy_conv_w[0], hy_conv_b[0].reshape(1, 3 * d), fwd_tab, inv_tab,
                     hr, gg, nyq, hy_skip[0], fb, tc)
    x3, h2, lgt = _proj_ln([z2.reshape(t, d)], [hy_w_out[0].astype(BF16)], x2, g1[:b], ln_g[1, 0], ln_b[1, 0],
                           sc2[:b], sh2[:b], router_wt, rows_per_mod=n, tm=512, name="hyena_out_proj_ln")
    x4 = _moe(h2, lgt, x3, g2[:b], ln_g[1, 1], ln_b[1, 1], router_bias, exp_w_gate, exp_w_up, exp_w_down, layer=1,
              rows_per_mod=n)
    return x4.reshape(b, n, d)
```

```python
import functools
import math

import jax
import jax.numpy as jnp
import numpy as np
from jax import lax
from jax.experimental import pallas as pl
from jax.experimental.pallas import tpu as pltpu

F32 = jnp.float32
BF16 = jnp.bfloat16
I32 = jnp.int32

D_MODEL = 1024
DEPTH = 2
GRID_W = 64
HEAD_DIM = 64
DIFF_HEADS = 4
WIN_Q_HEADS = 8
WIN_KV_HEADS = 2
WINDOW = 128
WIN_BLOCK = 128
ROPE_BASE = 10000.0
DIFF_QK_W = DIFF_HEADS * 2 * HEAD_DIM
DIFF_VW = DIFF_HEADS * 2 * HEAD_DIM
WIN_Q_W = WIN_Q_HEADS * HEAD_DIM
WIN_KV_W = WIN_KV_HEADS * HEAD_DIM
Q_W = DIFF_QK_W + WIN_Q_W
KV_W = DIFF_QK_W + DIFF_VW + 2 * WIN_KV_W
ATTN_PROJ_W = Q_W + KV_W
FILTER_EMB = 33
FILTER_HID = 64
DECAY_TARGET = 1e-2
FAST_DECAY_PCT = 0.3
SLOW_DECAY_PCT = 1.5
N_EXPERTS = 16
N_GROUPS = 4
EXPERTS_PER_GROUP = N_EXPERTS // N_GROUPS
EXPERT_FF = 1024
LN_EPS = 1e-5
DEEPNORM_ALPHA = (2 * DEPTH) ** 0.25
NEG_INF = -1e30
LOG2_E = 1.4426950408889634

LANES = 128
MOD_ROWS = 16
VMEM_LIMIT = 60 * 1024 * 1024
MOE_BM = 256
MOE_TILE = 512
ROW_ALIGN = 8
NT_DIMS = (((1,), (1,)), ((), ()))


def _cparams(n_axes):
    return pltpu.CompilerParams(dimension_semantics=("arbitrary",) * n_axes, vmem_limit_bytes=VMEM_LIMIT)


@functools.lru_cache(maxsize=None)
def _rope_tables(n):
    rows = n // GRID_W
    r, col = np.meshgrid(np.arange(rows, dtype=np.float32), np.arange(GRID_W, dtype=np.float32), indexing="ij")
    axis_dim = HEAD_DIM // 2
    inv_freq = (ROPE_BASE ** (-np.arange(0, axis_dim, 2, dtype=np.float32) / axis_dim)).astype(np.float32)
    ang = np.concatenate([r.reshape(-1, 1) * inv_freq, col.reshape(-1, 1) * inv_freq], -1)
    ang = np.concatenate([ang, ang], -1).astype(np.float32)
    cos, sin = np.cos(ang), np.sin(ang)
    half = np.arange(HEAD_DIM) < HEAD_DIM // 2
    sin_signed = np.where(half[None, :], -sin, sin)
    reps = LANES // HEAD_DIM
    return (np.tile(cos, (1, reps)).astype(np.float32), np.tile(sin_signed, (1, reps)).astype(np.float32))


@functools.lru_cache(maxsize=None)
def _dft_tables(n, fb):
    big = 2 * n
    k = np.arange(n, dtype=np.int64)[:, None]
    t = np.arange(n, dtype=np.int64)[None, :]
    ang = ((k * t) % big).astype(np.float64) * (2.0 * math.pi / big)
    c, s = np.cos(ang), np.sin(ang)
    alt = (1 - 2 * (np.arange(n) & 1)).astype(np.float64)
    s_f = s.copy()
    s_f[0, :] = alt
    kb = n // fb
    fwd = np.concatenate([c.reshape(kb, fb, n), s_f.reshape(kb, fb, n)], axis=1)
    ci = c.T * (2.0 / big)
    ci[:, 0] = 1.0 / big
    si = s.T * (2.0 / big)
    si[:, 0] = alt / big
    inv = np.concatenate([ci.reshape(n, kb, fb).transpose(1, 0, 2), si.reshape(n, kb, fb).transpose(1, 0, 2)], axis=2)
    return fwd.astype(np.float32), inv.astype(np.float32)


@functools.lru_cache(maxsize=None)
def _filter_tables(n):
    t = np.linspace(0.0, 1.0, n, dtype=np.float32)[:, None]
    bands = (FILTER_EMB - 1) // 2
    w = (2.0 * math.pi * np.arange(n, dtype=np.float32)[:, None] / n).astype(np.float32)
    fr = np.linspace(1e-4, bands - 1, bands, dtype=np.float32)[None, :]
    z = np.concatenate([t, np.cos(fr * w), -np.sin(fr * w)], -1).astype(np.float32)
    zp = np.zeros((n, FILTER_HID), np.float32)
    zp[:, :FILTER_EMB] = z
    deltas = np.abs(np.linspace(math.log(DECAY_TARGET) / SLOW_DECAY_PCT, math.log(DECAY_TARGET) / FAST_DECAY_PCT,
                                D_MODEL, dtype=np.float32))
    decay = np.exp(-t * deltas[None, :]).astype(np.float32)
    return zp, decay


def _mods_kernel(c_ref, w_ref, b_ref, o_ref):
    c = c_ref[...]
    a = (c * jax.nn.sigmoid(c)).astype(BF16)
    o_ref[...] = jnp.dot(a, w_ref[...].astype(BF16), preferred_element_type=F32) + b_ref[...]


def _mods(c_rows, mod_w, mod_b):
    d = D_MODEL
    tn = 1536
    return pl.pallas_call(
        _mods_kernel,
        grid=(DEPTH, 6 * d // tn),
        in_specs=[pl.BlockSpec((MOD_ROWS, d), lambda l, j: (0, 0)),
                  pl.BlockSpec((None, d, tn), lambda l, j: (l, 0, j)),
                  pl.BlockSpec((None, 1, tn), lambda l, j: (l, 0, j))],
        out_specs=pl.BlockSpec((None, MOD_ROWS, tn), lambda l, j: (l, 0, j)),
        out_shape=jax.ShapeDtypeStruct((DEPTH, MOD_ROWS, 6 * d), F32),
        compiler_params=_cparams(2), name="mods",
    )(c_rows, mod_w, mod_b.reshape(DEPTH, 1, 6 * d))


def _modmm_kernel(*refs, n_groups, rope_groups, scaled_groups, chunk):
    if rope_groups:
        x_ref, sc_ref, sh_ref, w_ref, cos_ref, sin_ref, o_ref = refs
        cos, sin = cos_ref[...], sin_ref[...]
        lane = lax.broadcasted_iota(I32, (1, LANES), 1)
        first_half = (lane % HEAD_DIM) < HEAD_DIM // 2
    else:
        x_ref, sc_ref, sh_ref, w_ref, o_ref = refs
    h = (x_ref[...] * (1.0 + sc_ref[...]) + sh_ref[...]).astype(BF16)
    scales = dict(scaled_groups)
    gpc = chunk // LANES
    for c in range(n_groups // gpc):
        acc = jnp.dot(h, w_ref[:, c * chunk:(c + 1) * chunk], preferred_element_type=F32)
        for j in range(gpc):
            g = c * gpc + j
            blk = acc[:, j * LANES:(j + 1) * LANES]
            if g in rope_groups:
                rot = jnp.where(first_half, pltpu.roll(blk, LANES - HEAD_DIM // 2, 1), pltpu.roll(blk, HEAD_DIM // 2, 1))
                blk = blk * cos + rot * sin
                if g in scales:
                    blk = blk * scales[g]
            o_ref[:, g * LANES:(g + 1) * LANES] = blk.astype(o_ref.dtype)


def _modmm(x2d, sc, sh, w, rows_per_mod, tm, rope=None, name="modmm"):
    t, d = x2d.shape
    n = w.shape[1]
    tiles_per_mod = rows_per_mod // tm
    nmod = sc.shape[0]
    in_specs = [pl.BlockSpec((tm, d), lambda i: (i, 0)),
                pl.BlockSpec((None, 1, d), lambda i: (i // tiles_per_mod, 0, 0)),
                pl.BlockSpec((None, 1, d), lambda i: (i // tiles_per_mod, 0, 0)),
                pl.BlockSpec((d, n), lambda i: (0, 0))]
    args = [x2d, sc.reshape(nmod, 1, d), sh.reshape(nmod, 1, d), w]
    rope_groups, scaled_groups = (), ()
    if rope is not None:
        cos, sin, rope_groups, scaled_groups = rope
        in_specs += [pl.BlockSpec((tm, LANES), lambda i: (i % tiles_per_mod, 0)),
                     pl.BlockSpec((tm, LANES), lambda i: (i % tiles_per_mod, 0))]
        args += [cos, sin]
    kern = functools.partial(_modmm_kernel, n_groups=n // LANES, rope_groups=tuple(rope_groups),
                             scaled_groups=tuple(scaled_groups), chunk=256)
    return pl.pallas_call(
        kern, grid=(t // tm,), in_specs=in_specs,
        out_specs=pl.BlockSpec((tm, n), lambda i: (i, 0)),
        out_shape=jax.ShapeDtypeStruct((t, n), BF16),
        compiler_params=_cparams(1), name=name,
    )(*args)


def _diff_attn_kernel(q_ref, k_ref, v_ref, kc_ref, vc_ref, lam_ref, g_ref, o_ref, *, lam_init, sub):
    lane = lax.broadcasted_iota(I32, (1, LANES), 1)
    k, kc = k_ref[...], kc_ref[...]
    v, vc = v_ref[...], vc_ref[...]

    def scores(r0):
        q = q_ref[r0:r0 + sub, :]
        zero = jnp.zeros_like(q)
        out = []
        for qm in (jnp.where(lane < HEAD_DIM, q, zero), jnp.where(lane >= HEAD_DIM, q, zero)):
            out.append((lax.dot_general(qm, k, NT_DIMS, preferred_element_type=F32),
                        lax.dot_general(qm, kc, NT_DIMS, preferred_element_type=F32)))
        return out

    def probs(sl, sc):
        m = jnp.maximum(jnp.max(sl, -1, keepdims=True), jnp.max(sc, -1, keepdims=True))
        pl_, pc = jnp.exp2(sl - m), jnp.exp2(sc - m)
        den = jnp.sum(pl_, -1, keepdims=True) + jnp.sum(pc, -1, keepdims=True)
        return pl_, pc, 1.0 / den

    lv = lam_ref[...]
    lam = (jnp.exp(jnp.sum(lv[0:1] * lv[1:2], keepdims=True)) - jnp.exp(jnp.sum(lv[2:3] * lv[3:4], keepdims=True))
           + lam_init)
    starts = list(range(0, q_ref.shape[0], sub))
    nxt = scores(starts[0])
    for i, r0 in enumerate(starts):
        cur = nxt
        if i + 1 < len(starts):
            nxt = scores(starts[i + 1])
        p1l, p1c, r1 = probs(*cur[0])
        p2l, p2c, r2 = probs(*cur[1])
        w2 = lam * r2
        al = (p1l * r1 - p2l * w2).astype(BF16)
        ac = (p1c * r1 - p2c * w2).astype(BF16)
        o = jnp.dot(al, v, preferred_element_type=F32) + jnp.dot(ac, vc, preferred_element_type=F32)
        ms = jnp.mean(o * o, -1, keepdims=True)
        o_ref[r0:r0 + sub, :] = (o * lax.rsqrt(ms + LN_EPS) * g_ref[...] * (1.0 - lam_init)).astype(o_ref.dtype)


def _diff_attn(proj, proj_c, lam_vec, subln_g, lam_init, tq, sub):
    b, n, _ = proj.shape
    nc = proj_c.shape[1]
    kcol = Q_W // LANES
    vcol = (Q_W + DIFF_QK_W) // LANES
    vccol = DIFF_QK_W // LANES
    kern = functools.partial(_diff_attn_kernel, lam_init=lam_init, sub=sub)
    return pl.pallas_call(
        kern, grid=(b, DIFF_HEADS, n // tq),
        in_specs=[pl.BlockSpec((None, tq, LANES), lambda bi, h, i: (bi, i, h)),
                  pl.BlockSpec((None, n, LANES), lambda bi, h, i: (bi, 0, kcol + h)),
                  pl.BlockSpec((None, n, LANES), lambda bi, h, i: (bi, 0, vcol + h)),
                  pl.BlockSpec((None, nc, LANES), lambda bi, h, i: (bi, 0, h)),
                  pl.BlockSpec((None, nc, LANES), lambda bi, h, i: (bi, 0, vccol + h)),
                  pl.BlockSpec((4, HEAD_DIM), lambda bi, h, i: (0, 0)),
                  pl.BlockSpec((1, LANES), lambda bi, h, i: (0, 0))],
        out_specs=pl.BlockSpec((None, tq, LANES), lambda bi, h, i: (bi, i, h)),
        out_shape=jax.ShapeDtypeStruct((b, n, DIFF_VW), BF16),
        compiler_params=_cparams(3), name="diff_attn",
    )(proj, proj, proj, proj_c, proj_c, lam_vec, subln_g.reshape(1, LANES))


def _win_attn_kernel(sink_ref, q_ref, k_ref, v_ref, kc_ref, vc_ref, o_ref, *, seq, tq):
    n = pl.program_id(1)
    gq = WIN_Q_HEADS // WIN_KV_HEADS
    kw = tq + 2 * WINDOW
    start = pl.multiple_of(jnp.clip(n * tq - WINDOW, 0, seq - kw), WINDOW)
    k_win, v_win = k_ref[pl.ds(start, kw), :], v_ref[pl.ds(start, kw), :]
    kc, vc = kc_ref[...], vc_ref[...]
    lane = lax.broadcasted_iota(I32, (1, LANES), 1)
    row = lax.broadcasted_iota(I32, (gq * tq, 1), 0)
    q_abs = n * tq + row % tq
    k_abs = start + lax.broadcasted_iota(I32, (1, kw), 1)
    allowed = jnp.abs(q_abs - k_abs) <= WINDOW
    head = row // tq
    q = q_ref[...].astype(F32)
    for g in range(WIN_KV_HEADS):
        in_g = (lane // HEAD_DIM) == g
        parts = []
        for j in range(gq):
            hq = g * gq + j
            x = q[:, (hq // 2) * LANES:(hq // 2 + 1) * LANES]
            if hq % 2 != g:
                x = pltpu.roll(x, HEAD_DIM, 1)
            parts.append(jnp.where(in_g, x, 0.0))
        qs = jnp.concatenate(parts, 0).astype(BF16)
        s_loc = lax.dot_general(qs, k_win, NT_DIMS, preferred_element_type=F32)
        s_ctx = lax.dot_general(qs, kc, NT_DIMS, preferred_element_type=F32)
        s_loc = jnp.where(allowed, s_loc, NEG_INF)
        sk = jnp.zeros((gq * tq, 1), F32)
        for j in range(gq):
            sk = jnp.where(head == j, sink_ref[g * gq + j], sk)
        m = jnp.maximum(jnp.maximum(jnp.max(s_loc, -1, keepdims=True), jnp.max(s_ctx, -1, keepdims=True)), sk)
        p_loc, p_ctx = jnp.exp(s_loc - m), jnp.exp(s_ctx - m)
        den = jnp.sum(p_loc, -1, keepdims=True) + jnp.sum(p_ctx, -1, keepdims=True) + jnp.exp(sk - m)
        o = (jnp.dot(p_loc.astype(BF16), v_win, preferred_element_type=F32)
             + jnp.dot(p_ctx.astype(BF16), vc, preferred_element_type=F32)) * (1.0 / den)
        for cb in range(gq // 2):
            pair = []
            for half in range(2):
                piece = o[(2 * cb + half) * tq:(2 * cb + half + 1) * tq]
                pair.append(piece if half == g else pltpu.roll(piece, HEAD_DIM, 1))
            col = (g * gq // 2 + cb) * LANES
            o_ref[:, col:col + LANES] = jnp.where(lane < HEAD_DIM, pair[0], pair[1]).astype(o_ref.dtype)


def _win_attn(proj, proj_c, sink, tq):
    b, n, _ = proj.shape
    nc = proj_c.shape[1]
    qcol = DIFF_QK_W // WIN_Q_W
    kcol = (Q_W + DIFF_QK_W + DIFF_VW) // LANES
    kccol = (DIFF_QK_W + DIFF_VW) // LANES
    kern = functools.partial(_win_attn_kernel, seq=n, tq=tq)
    return pl.pallas_call(
        kern, grid=(b, n // tq),
        in_specs=[pl.BlockSpec(memory_space=pltpu.SMEM),
                  pl.BlockSpec((None, tq, WIN_Q_W), lambda bi, i: (bi, i, qcol)),
                  pl.BlockSpec((None, n, LANES), lambda bi, i: (bi, 0, kcol)),
                  pl.BlockSpec((None, n, LANES), lambda bi, i: (bi, 0, kcol + 1)),
                  pl.BlockSpec((None, nc, LANES), lambda bi, i: (bi, 0, kccol)),
                  pl.BlockSpec((None, nc, LANES), lambda bi, i: (bi, 0, kccol + 1))],
        out_specs=pl.BlockSpec((None, tq, WIN_Q_W), lambda bi, i: (bi, i, 0)),
        out_shape=jax.ShapeDtypeStruct((b, n, WIN_Q_W), BF16),
        compiler_params=_cparams(2), name="win_attn",
    )(sink, proj, proj, proj, proj_c, proj_c)


def _layer_norm(r, g, b):
    mu = jnp.mean(r, -1, keepdims=True)
    dlt = r - mu
    var = jnp.mean(dlt * dlt, -1, keepdims=True)
    return dlt * lax.rsqrt(var + LN_EPS) * g + b


def _proj_ln_kernel(*refs, n_in):
    a_refs = refs[:n_in]
    w_refs = refs[n_in:2 * n_in]
    x_ref, gate_ref, lng_ref, lnb_ref, sc_ref, sh_ref, rw_ref, x1_ref, h2_ref, lg_ref = refs[2 * n_in:]
    y = jnp.dot(a_refs[0][...], w_refs[0][...], preferred_element_type=F32)
    for a_ref, w_ref in zip(a_refs[1:], w_refs[1:]):
        y = y + jnp.dot(a_ref[...], w_ref[...], preferred_element_type=F32)
    xn = _layer_norm(DEEPNORM_ALPHA * x_ref[...] + gate_ref[...] * y, lng_ref[...], lnb_ref[...])
    x1_ref[...] = xn
    h2 = xn * (1.0 + sc_ref[...]) + sh_ref[...]
    h2_ref[...] = h2.astype(h2_ref.dtype)
    lg_ref[...] = lax.dot_general(rw_ref[...], h2, NT_DIMS, precision=lax.Precision.HIGHEST,
                                  preferred_element_type=F32)


def _proj_ln(acts, ws, x2d, gate, ln_g, ln_b, sc, sh, router_wt, rows_per_mod, tm, name):
    t, d = x2d.shape
    n_in = len(acts)
    tiles_per_mod = rows_per_mod // tm
    nmod = gate.shape[0]
    row = lambda i: (i, 0)
    full = lambda i: (0, 0)
    mod = lambda i: (i // tiles_per_mod, 0, 0)
    in_specs = ([pl.BlockSpec((tm, a.shape[1]), row) for a in acts]
                + [pl.BlockSpec(w.shape, full) for w in ws]
                + [pl.BlockSpec((tm, d), row), pl.BlockSpec((None, 1, d), mod),
                   pl.BlockSpec((1, d), full), pl.BlockSpec((1, d), full),
                   pl.BlockSpec((None, 1, d), mod), pl.BlockSpec((None, 1, d), mod),
                   pl.BlockSpec((N_EXPERTS, d), full)])
    return pl.pallas_call(
        functools.partial(_proj_ln_kernel, n_in=n_in), grid=(t // tm,), in_specs=in_specs,
        out_specs=[pl.BlockSpec((tm, d), row), pl.BlockSpec((tm, d), row),
                   pl.BlockSpec((N_EXPERTS, tm), lambda i: (0, i))],
        out_shape=[jax.ShapeDtypeStruct((t, d), F32), jax.ShapeDtypeStruct((t, d), BF16),
                   jax.ShapeDtypeStruct((N_EXPERTS, t), F32)],
        compiler_params=_cparams(1), name=name,
    )(*acts, *ws, x2d, gate.reshape(nmod, 1, d), ln_g.reshape(1, d), ln_b.reshape(1, d),
      sc.reshape(nmod, 1, d), sh.reshape(nmod, 1, d), router_wt)


def _first_argmax(vals):
    idx = jnp.zeros(vals[0].shape, I32)
    best = vals[0]
    for j in range(1, len(vals)):
        upd = vals[j] > best
        idx = jnp.where(upd, j, idx)
        best = jnp.where(upd, vals[j], best)
    return idx, best


def _route_kernel(lg_ref, bias_ref, pos_ref, w_ref, tab_ref, cnt_ref, carry_ref, off_ref, *, tr):
    @pl.when(pl.program_id(0) == 0)
    def _():
        carry_ref[...] = jnp.zeros_like(carry_ref)

    lg = lg_ref[...]
    ex = jnp.exp(lg - jnp.max(lg, 0, keepdims=True))
    scores = ex / jnp.sum(ex, 0, keepdims=True)
    sel = scores + bias_ref[...]
    rows = [sel[e:e + 1] for e in range(N_EXPERTS)]
    group_scores = []
    for g in range(N_GROUPS):
        r = rows[g * EXPERTS_PER_GROUP:(g + 1) * EXPERTS_PER_GROUP]
        best = None
        for i in range(EXPERTS_PER_GROUP):
            for j in range(i + 1, EXPERTS_PER_GROUP):
                s = r[i] + r[j]
                best = s if best is None else jnp.maximum(best, s)
        group_scores.append(best)
    grp, _ = _first_argmax(group_scores)
    vals = []
    for j in range(EXPERTS_PER_GROUP):
        v = rows[(N_GROUPS - 1) * EXPERTS_PER_GROUP + j]
        for g in range(N_GROUPS - 2, -1, -1):
            v = jnp.where(grp == g, rows[g * EXPERTS_PER_GROUP + j], v)
        vals.append(v)
    i0, _ = _first_argmax(vals)
    i1, _ = _first_argmax([jnp.where(i0 == j, -jnp.inf, vals[j]) for j in range(EXPERTS_PER_GROUP)])
    e0 = grp * EXPERTS_PER_GROUP + i0
    e1 = grp * EXPERTS_PER_GROUP + i1
    eid = lax.broadcasted_iota(I32, (N_EXPERTS, 1), 0)
    oh0, oh1 = eid == e0, eid == e1
    s0 = jnp.sum(jnp.where(oh0, scores, 0.0), 0, keepdims=True)
    s1 = jnp.sum(jnp.where(oh1, scores, 0.0), 0, keepdims=True)
    den = s0 + s1
    member = jnp.where(oh0 | oh1, 1.0, 0.0)
    before = lax.broadcasted_iota(I32, (tr, tr), 0) < lax.broadcasted_iota(I32, (tr, tr), 1)
    upper = jnp.where(before, 1.0, 0.0).astype(BF16)
    cnt = jnp.dot(member.astype(BF16), upper, preferred_element_type=F32)
    run = jnp.sum(member, 1, keepdims=True)
    run = jnp.floor((run + (ROW_ALIGN - 1)) * (1.0 / ROW_ALIGN)) * ROW_ALIGN
    run = jnp.broadcast_to(run, (N_EXPERTS, LANES))
    acc = jnp.zeros((1, LANES), F32)
    for e in range(N_EXPERTS):
        off_ref[e:e + 1, :] = acc
        acc = acc + run[e:e + 1]
    off = off_ref[...]
    at = off[:, 0:1] + cnt
    pos_ref[0:1, :] = jnp.sum(jnp.where(oh0, at, 0.0), 0, keepdims=True).astype(I32)
    pos_ref[1:2, :] = jnp.sum(jnp.where(oh1, at, 0.0), 0, keepdims=True).astype(I32)
    w_ref[0:1, :] = s0 / den
    w_ref[1:2, :] = s1 / den
    tab_ref[0] = off.astype(I32)
    tab_ref[1] = run.astype(I32)
    tab_ref[2] = carry_ref[...].astype(I32)
    carry_ref[...] = carry_ref[...] + run
    cnt_ref[...] = carry_ref[...]


def _route(logits_t, router_bias, tr):
    t = logits_t.shape[1]
    tok = lambda i: (0, i)
    return pl.pallas_call(
        functools.partial(_route_kernel, tr=tr), grid=(t // tr,),
        in_specs=[pl.BlockSpec((N_EXPERTS, tr), tok), pl.BlockSpec((N_EXPERTS, 1), lambda i: (0, 0))],
        out_specs=[pl.BlockSpec((2, tr), tok), pl.BlockSpec((2, tr), tok),
                   pl.BlockSpec((None, 3, N_EXPERTS, LANES), lambda i: (i, 0, 0, 0)),
                   pl.BlockSpec((N_EXPERTS, LANES), lambda i: (0, 0))],
        out_shape=[jax.ShapeDtypeStruct((2, t), I32), jax.ShapeDtypeStruct((2, t), F32),
                   jax.ShapeDtypeStruct((t // tr, 3, N_EXPERTS, LANES), I32),
                   jax.ShapeDtypeStruct((N_EXPERTS, LANES), F32)],
        scratch_shapes=[pltpu.VMEM((N_EXPERTS, LANES), F32), pltpu.VMEM((N_EXPERTS, LANES), F32)],
        compiler_params=_cparams(1), name="route",
    )(logits_t, router_bias.reshape(N_EXPERTS, 1))


def _run_copies(n, max_rows, make_copy, wait):
    sz = max_rows
    while sz >= ROW_ALIGN:
        start = (n // (2 * sz)) * (2 * sz)

        @pl.when((n & sz) != 0)
        def _(start=start, sz=sz):
            cp = make_copy(start, sz)
            cp.wait() if wait else cp.start()

        sz //= 2


def _rows(ref, start, size):
    return ref.at[pl.ds(pl.multiple_of(start, ROW_ALIGN), size)]


def _dispatch_kernel(tab_ref, h_ref, pos_ref, w_ref, xs_ref, buf, zbuf, sem, *, tt, n_tiles, bm):
    i = pl.program_id(0)
    d = h_ref.shape[1]
    sb = buf.shape[1]
    ne = N_EXPERTS
    fill = 3 * n_tiles * ne

    @pl.when(i == 0)
    def _():
        zbuf[...] = jnp.zeros_like(zbuf)
        for wait in (False, True):
            for e in range(ne):
                dst, n = tab_ref[fill + e], tab_ref[fill + ne + e]
                _run_copies(n, bm // 2, lambda s, z: pltpu.make_async_copy(
                    zbuf.at[pl.ds(0, z)], _rows(xs_ref, dst + s, z), sem.at[2]), wait)

        def zero_block(j, carry):
            for half in range(2):
                cp = pltpu.make_async_copy(zbuf, _rows(xs_ref, j * bm + half * (bm // 2), bm // 2), sem.at[2])
                cp.start()
                cp.wait()
            return carry

        lax.fori_loop(tab_ref[fill + 2 * ne], xs_ref.shape[0] // bm, zero_block, 0)

    def tile_copies(tile, slot, wait):
        for e in range(ne):
            off = tab_ref[tile * ne + e]
            n = tab_ref[(n_tiles + tile) * ne + e]
            dst = tab_ref[(2 * n_tiles + tile) * ne + e]
            _run_copies(n, tt, lambda s, z: pltpu.make_async_copy(
                _rows(buf.at[slot], off + s, z), _rows(xs_ref, dst + s, z), sem.at[slot]), wait)

    slot = i % 2
    pos, w = pos_ref[...], w_ref[...]
    srow = lax.broadcasted_iota(I32, (sb, 1), 0)
    m0, m1 = srow == pos[0:1], srow == pos[1:2]
    perm = (jnp.where(m0, 1.0, 0.0) + jnp.where(m1, 1.0, 0.0)).astype(BF16)
    buf[slot, :, :d] = jnp.dot(perm, h_ref[...].astype(BF16), preferred_element_type=F32)
    wrow = jnp.sum(jnp.where(m0, w[0:1], 0.0) + jnp.where(m1, w[1:2], 0.0), 1, keepdims=True)
    buf[slot, :, d:] = jnp.broadcast_to(wrow, (sb, LANES))

    @pl.when(i > 0)
    def _():
        tile_copies(i - 1, 1 - slot, True)

    tile_copies(i, slot, False)

    @pl.when(i == n_tiles - 1)
    def _():
        tile_copies(i, slot, True)


def _dispatch(tab, h2, pos, wts, n_slots, tt, sb, bm):
    t, d = h2.shape
    n_tiles = t // tt
    return pl.pallas_call(
        functools.partial(_dispatch_kernel, tt=tt, n_tiles=n_tiles, bm=bm),
        grid_spec=pltpu.PrefetchScalarGridSpec(
            num_scalar_prefetch=1, grid=(n_tiles,),
            in_specs=[pl.BlockSpec((tt, d), lambda i, s: (i, 0)), pl.BlockSpec((2, tt), lambda i, s: (0, i)),
                      pl.BlockSpec((2, tt), lambda i, s: (0, i))],
            out_specs=pl.BlockSpec(memory_space=pl.ANY),
            scratch_shapes=[pltpu.VMEM((2, sb, d + LANES), F32), pltpu.VMEM((bm // 2, d + LANES), F32),
                            pltpu.SemaphoreType.DMA((3,))]),
        out_shape=jax.ShapeDtypeStruct((n_slots, d + LANES), F32),
        compiler_params=_cparams(1), name="moe_dispatch",
    )(tab, h2, pos, wts)


def _ffn_kernel(blk_e_ref, nused_ref, x_ref, wg_ref, wu_ref, wd_ref, o_ref, wg_bf, wu_bf, wd_bf):
    i = pl.program_id(0)
    new_expert = (i == 0) | (blk_e_ref[i] != blk_e_ref[jnp.maximum(i - 1, 0)])

    @pl.when(new_expert)
    def _():
        wg_bf[...] = wg_ref[...].astype(BF16)
        wu_bf[...] = wu_ref[...].astype(BF16)
        wd_bf[...] = wd_ref[...].astype(BF16)

    @pl.when(i < nused_ref[0])
    def _():
        d = wg_bf.shape[0]
        x = x_ref[:, :d].astype(BF16)
        gate = jnp.dot(x, wg_bf[...], preferred_element_type=F32)
        up = jnp.dot(x, wu_bf[...], preferred_element_type=F32)
        act = (gate * jax.nn.sigmoid(gate) * up).astype(BF16)
        o_ref[...] = jnp.dot(act, wd_bf[...], preferred_element_type=F32) * x_ref[:, d:d + 1]

    @pl.when(i >= nused_ref[0])
    def _():
        o_ref[...] = jnp.zeros_like(o_ref)


def _ffn(blk_e, n_used, xs, wg, wu, wd, layer, bm):
    n_slots, xw = xs.shape
    d, ff = wg.shape[2:]
    xrow = lambda i, be, nu: (jnp.minimum(i, nu[0] - 1), 0)
    wsel = lambda i, be, nu: (layer, be[i], 0, 0)
    return pl.pallas_call(
        _ffn_kernel,
        grid_spec=pltpu.PrefetchScalarGridSpec(
            num_scalar_prefetch=2, grid=(n_slots // bm,),
            in_specs=[pl.BlockSpec((bm, xw), xrow), pl.BlockSpec((None, None, d, ff), wsel),
                      pl.BlockSpec((None, None, d, ff), wsel), pl.BlockSpec((None, None, ff, d), wsel)],
            out_specs=pl.BlockSpec((bm, d), lambda i, be, nu: (i, 0)),
            scratch_shapes=[pltpu.VMEM((d, ff), BF16), pltpu.VMEM((d, ff), BF16), pltpu.VMEM((ff, d), BF16)]),
        out_shape=jax.ShapeDtypeStruct((n_slots, d), F32),
        compiler_params=_cparams(1), name="moe_ffn",
    )(blk_e, n_used, xs, wg, wu, wd)


def _combine_kernel(tab_ref, ys_ref, pos_ref, x1_ref, gate_ref, lng_ref, lnb_ref, o_ref, ybuf, sem, *, tt, n_tiles):
    i = pl.program_id(0)
    sb = ybuf.shape[1]
    ne = N_EXPERTS
    slot = i % 2

    def tile_copies(tile, slot, wait):
        for e in range(ne):
            off = tab_ref[tile * ne + e]
            n = tab_ref[(n_tiles + tile) * ne + e]
            src = tab_ref[(2 * n_tiles + tile) * ne + e]
            _run_copies(n, tt, lambda s, z: pltpu.make_async_copy(
                _rows(ys_ref, src + s, z), _rows(ybuf.at[slot], off + s, z), sem.at[slot]), wait)

    @pl.when(i == 0)
    def _():
        ybuf[...] = jnp.zeros_like(ybuf)
        tile_copies(0, 0, False)

    @pl.when(i + 1 < n_tiles)
    def _():
        tile_copies(i + 1, 1 - slot, False)

    tile_copies(i, slot, True)
    pos = pos_ref[...]
    scol = lax.broadcasted_iota(I32, (1, sb), 1)
    unperm = (jnp.where(scol == pos[:, 0:1], 1.0, 0.0) + jnp.where(scol == pos[:, 1:2], 1.0, 0.0)).astype(BF16)
    f = jnp.dot(unperm, ybuf[slot].astype(BF16), preferred_element_type=F32)
    o_ref[...] = _layer_norm(DEEPNORM_ALPHA * x1_ref[...] + gate_ref[...] * f, lng_ref[...], lnb_ref[...])


def _combine(tab, ys, pos_t, x1, gate, ln_g, ln_b, rows_per_mod, tt, sb):
    t, d = x1.shape
    tiles_per_mod = rows_per_mod // tt
    nmod = gate.shape[0]
    return pl.pallas_call(
        functools.partial(_combine_kernel, tt=tt, n_tiles=t // tt),
        grid_spec=pltpu.PrefetchScalarGridSpec(
            num_scalar_prefetch=1, grid=(t // tt,),
            in_specs=[pl.BlockSpec(memory_space=pl.ANY),
                      pl.BlockSpec((tt, 2), lambda i, s: (i, 0)),
                      pl.BlockSpec((tt, d), lambda i, s: (i, 0)),
                      pl.BlockSpec((None, 1, d), lambda i, s: (i // tiles_per_mod, 0, 0)),
                      pl.BlockSpec((1, d), lambda i, s: (0, 0)), pl.BlockSpec((1, d), lambda i, s: (0, 0))],
            out_specs=pl.BlockSpec((tt, d), lambda i, s: (i, 0)),
            scratch_shapes=[pltpu.VMEM((2, sb, d), F32), pltpu.SemaphoreType.DMA((2,))]),
        out_shape=jax.ShapeDtypeStruct((t, d), F32),
        compiler_params=_cparams(1), name="moe_combine",
    )(tab, ys, pos_t, x1, gate.reshape(nmod, 1, d), ln_g.reshape(1, d), ln_b.reshape(1, d))


def _moe(h2, logits_t, x1, gate, ln_g, ln_b, router_bias, wg, wu, wd, layer, rows_per_mod):
    t, d = h2.shape
    bm, tt, ne = MOE_BM, MOE_TILE, N_EXPERTS
    n_tiles = t // tt
    pad = ne * (ROW_ALIGN - 1)
    sb = -(-(2 * tt + pad) // LANES) * LANES
    pos, wts, tab3, cnt = _route(logits_t, router_bias, tr=tt)
    rows = cnt[:, 0].astype(I32)
    prows = (rows + bm - 1) // bm * bm
    pends = jnp.cumsum(prows)
    pstarts = pends - prows
    n_blk = -(-(2 * t + n_tiles * pad + ne * (bm - 1)) // bm)
    n_used = pends[-1] // bm
    blk_ids = jnp.minimum(jnp.arange(n_blk, dtype=I32), n_used - 1)
    blk_e = jnp.minimum(jnp.sum((blk_ids[:, None] * bm >= pends[None, :]).astype(I32), 1), ne - 1)
    tab3 = tab3[:, :, :, 0]
    tab = jnp.concatenate([tab3[:, 0].reshape(-1), tab3[:, 1].reshape(-1),
                           (tab3[:, 2] + pstarts[None, :]).reshape(-1), pstarts + rows, prows - rows,
                           n_used.reshape(1)]).astype(I32)
    xs = _dispatch(tab, h2, pos, wts, n_blk * bm, tt, sb, bm)
    ys = _ffn(blk_e, n_used.reshape(1).astype(I32), xs, wg, wu, wd, layer, bm)
    return _combine(tab, ys, pos.T, x1, gate, ln_g, ln_b, rows_per_mod, tt, sb)


def _filter_mlp_kernel(z_ref, w1_ref, w2_ref, w3_ref, b_ref, sf_ref, o_ref):
    hp = lax.Precision.HIGHEST
    b, sf = b_ref[...], sf_ref[...]
    h = jnp.sin(sf[0:1] * (jnp.dot(z_ref[...], w1_ref[...], precision=hp, preferred_element_type=F32) + b[0:1]))
    h = jnp.sin(sf[1:2] * (jnp.dot(h, w2_ref[...], precision=hp, preferred_element_type=F32) + b[1:2]))
    o_ref[...] = jnp.sin(sf[2:3] * (jnp.dot(h, w3_ref[...], precision=hp, preferred_element_type=F32) + b[2:3]))


def _filter_mlp(zp, w_in_p, w_hid, b, sf):
    n = zp.shape[0]
    return pl.pallas_call(
        _filter_mlp_kernel, out_shape=jax.ShapeDtypeStruct((n, FILTER_HID), F32),
        compiler_params=pltpu.CompilerParams(vmem_limit_bytes=VMEM_LIMIT), name="hyena_filter_mlp",
    )(zp, w_in_p, w_hid[0], w_hid[1], b, sf)


def _filter_spec_kernel(hid_ref, wf_ref, wb_ref, dec_ref, fwd_ref, hr_ref, g_ref, nyq_ref, p_scr, q_scr, *, fb):
    k = pl.program_id(2)
    n = hid_ref.shape[0]

    @pl.when(k == 0)
    def _():
        hp = lax.Precision.HIGHEST
        hid, dec = hid_ref[...], dec_ref[...]
        row = lax.broadcasted_iota(I32, (n, 1), 0)
        fw = jnp.dot(hid, wf_ref[...], precision=hp, preferred_element_type=F32) * dec
        bw = jnp.dot(hid, wb_ref[...], precision=hp, preferred_element_type=F32) * dec
        bw = jnp.where(row == 0, 0.0, bw)
        p = fw + bw
        p_scr[...] = p.astype(BF16)
        q_scr[...] = (fw - bw).astype(BF16)
        sign = (1 - 2 * (row & 1)).astype(F32)
        nyq_ref[...] = jnp.sum(p * sign, 0, keepdims=True)

    hr_ref[...] = jnp.dot(fwd_ref[0:fb, :], p_scr[...], preferred_element_type=F32)
    gg = jnp.dot(fwd_ref[fb:2 * fb, :], q_scr[...], preferred_element_type=F32)
    rowb = lax.broadcasted_iota(I32, (fb, 1), 0)
    g_ref[...] = jnp.where((rowb == 0) & (k == 0), 0.0, gg)


def _filter_spec(hid, w_out4, decay, fwd_tab, fb, tc):
    n = hid.shape[0]
    d = D_MODEL
    kb = n // fb
    return pl.pallas_call(
        functools.partial(_filter_spec_kernel, fb=fb), grid=(2, d // tc, kb),
        in_specs=[pl.BlockSpec((n, FILTER_HID), lambda o, c, k: (0, 0)),
                  pl.BlockSpec((None, None, FILTER_HID, tc), lambda o, c, k: (0, o, 0, c)),
                  pl.BlockSpec((None, None, FILTER_HID, tc), lambda o, c, k: (1, o, 0, c)),
                  pl.BlockSpec((n, tc), lambda o, c, k: (0, c)),
                  pl.BlockSpec((None, 2 * fb, n), lambda o, c, k: (k, 0, 0))],
        out_specs=[pl.BlockSpec((None, fb, tc), lambda o, c, k: (o, k, c)),
                   pl.BlockSpec((None, fb, tc), lambda o, c, k: (o, k, c)),
                   pl.BlockSpec((None, 1, tc), lambda o, c, k: (o, 0, c))],
        out_shape=[jax.ShapeDtypeStruct((2, n, d), F32), jax.ShapeDtypeStruct((2, n, d), F32),
                   jax.ShapeDtypeStruct((2, 1, d), F32)],
        scratch_shapes=[pltpu.VMEM((n, tc), BF16), pltpu.VMEM((n, tc), BF16)],
        compiler_params=_cparams(3), name="hyena_filter_spec",
    )(hid, w_out4, w_out4, decay, fwd_tab)


def _short_conv(u_ref, cw_ref, cb_ref):
    n = u_ref.shape[0]
    u = u_ref[...].astype(F32)
    row = lax.broadcasted_iota(I32, (n, 1), 0)
    prev = jnp.where(row == 0, 0.0, pltpu.roll(u, 1, 0))
    nxt = jnp.where(row == n - 1, 0.0, pltpu.roll(u, n - 1, 0))
    cw = cw_ref[...]
    return prev * cw[0:1] + u * cw[1:2] + nxt * cw[2:3] + cb_ref[...]


def _hyena_conv_kernel(uv_ref, ux1_ref, ux2_ref, cwv_ref, cwx1_ref, cwx2_ref, cbv_ref, cbx1_ref, cbx2_ref,
                       fwd_ref, inv_ref, hr_ref, g_ref, nyq_ref, sk_ref, o_ref, zb_scr, z32_scr, acc_scr, *, fb, kb,
                       kps):
    o = pl.program_id(2)
    k = pl.program_id(3)

    @pl.when((o == 0) & (k == 0))
    def _():
        v = _short_conv(uv_ref, cwv_ref, cbv_ref)
        z32_scr[...] = v
        zb_scr[...] = v.astype(BF16)

    @pl.when(k == 0)
    def _():
        acc_scr[...] = jnp.zeros_like(acc_scr)

    zb = zb_scr[...]
    zfs = [jnp.dot(fwd_ref[j], zb, preferred_element_type=F32) for j in range(kps)]
    rowb = lax.broadcasted_iota(I32, (fb, 1), 0)
    for j in range(kps):
        zr, zi = zfs[j][:fb], zfs[j][fb:]
        hr, gg = hr_ref[j * fb:(j + 1) * fb, :], g_ref[j * fb:(j + 1) * fb, :]
        hb = jnp.where((rowb == 0) & (k == 0), nyq_ref[...], hr) if j == 0 else hr
        y = jnp.concatenate([zr * hr - zi * gg, zr * gg + zi * hb], 0).astype(BF16)
        acc_scr[...] += jnp.dot(inv_ref[j], y, preferred_element_type=F32)

    @pl.when((k == kb - 1) & (o == 0))
    def _():
        zn = _short_conv(ux1_ref, cwx1_ref, cbx1_ref) * (acc_scr[...] + z32_scr[...] * sk_ref[...])
        z32_scr[...] = zn
        zb_scr[...] = zn.astype(BF16)

    @pl.when((k == kb - 1) & (o == 1))
    def _():
        zn = _short_conv(ux2_ref, cwx2_ref, cbx2_ref) * (acc_scr[...] + z32_scr[...] * sk_ref[...])
        o_ref[...] = zn.astype(o_ref.dtype)


def _hyena_conv(u, conv_w, conv_b, fwd_tab, inv_tab, hr, gg, nyq, skip, fb, tc, kps):
    b, n, _ = u.shape
    d = D_MODEL
    kb = n // (fb * kps)
    ncb = d // tc
    ucol = lambda part: (lambda bi, c, o, k: (bi, 0, part * ncb + c))
    wcol = lambda part: (lambda bi, c, o, k: (0, part * ncb + c))
    spec = lambda bi, c, o, k: (o, k, c)
    per_o = lambda bi, c, o, k: (o, 0, c)
    return pl.pallas_call(
        functools.partial(_hyena_conv_kernel, fb=fb, kb=kb, kps=kps), grid=(b, ncb, 2, kb),
        in_specs=[pl.BlockSpec((None, n, tc), ucol(0)), pl.BlockSpec((None, n, tc), ucol(1)),
                  pl.BlockSpec((None, n, tc), ucol(2)),
                  pl.BlockSpec((3, tc), wcol(0)), pl.BlockSpec((3, tc), wcol(1)), pl.BlockSpec((3, tc), wcol(2)),
                  pl.BlockSpec((1, tc), wcol(0)), pl.BlockSpec((1, tc), wcol(1)), pl.BlockSpec((1, tc), wcol(2)),
                  pl.BlockSpec((kps, 2 * fb, n), lambda bi, c, o, k: (k, 0, 0)),
                  pl.BlockSpec((kps, n, 2 * fb), lambda bi, c, o, k: (k, 0, 0)),
                  pl.BlockSpec((None, kps * fb, tc), spec), pl.BlockSpec((None, kps * fb, tc), spec),
                  pl.BlockSpec((None, 1, tc), per_o), pl.BlockSpec((None, 1, tc), per_o)],
        out_specs=pl.BlockSpec((None, n, tc), lambda bi, c, o, k: (bi, 0, c)),
        out_shape=jax.ShapeDtypeStruct((b, n, d), BF16),
        scratch_shapes=[pltpu.VMEM((n, tc), BF16), pltpu.VMEM((n, tc), F32), pltpu.VMEM((n, tc), F32)],
        compiler_params=_cparams(4), name="hyena_conv",
    )(u, u, u, conv_w, conv_w, conv_w, conv_b, conv_b, conv_b, fwd_tab, inv_tab, hr, gg, nyq,
      skip.reshape(2, 1, d))


def kernel(x, c, ctx, c_ctx, mod_w, mod_b, ln_g, ln_b, attn_w_in, attn_lambda, attn_subln_g, attn_sink, attn_w_out,
           hy_w_in, hy_conv_w, hy_conv_b, hy_ffn_w_in, hy_ffn_w_hid, hy_ffn_b, hy_sin_freq, hy_ffn_w_out, hy_skip,
           hy_w_out, router_w, router_bias, exp_w_gate, exp_w_up, exp_w_down):
    b, n, d = x.shape
    nc = ctx.shape[1]
    t = b * n
    assert d == D_MODEL and b + 1 <= MOD_ROWS and n % 512 == 0 and nc % 256 == 0

    c_rows = jnp.zeros((MOD_ROWS, d), F32).at[:b].set(c).at[b].set(c_ctx)
    mods = _mods(c_rows, mod_w, mod_b)
    router_wt = router_w.T
    x2d = x.reshape(t, d)

    sh1, sc1, g1, sh2, sc2, g2 = jnp.split(mods[0], 6, axis=-1)
    cos, sin = _rope_tables(n)
    w_in = attn_w_in[0].astype(BF16)
    q_groups = tuple(range(Q_W // LANES))
    rope_groups = q_groups + tuple(range(Q_W // LANES, (Q_W + DIFF_QK_W) // LANES)) + (
        (Q_W + DIFF_QK_W + DIFF_VW) // LANES,)
    q_scales = tuple((g, HEAD_DIM ** -0.5 * (LOG2_E if g < DIFF_QK_W // LANES else 1.0)) for g in q_groups)
    proj = _modmm(x2d, sc1[:b], sh1[:b], w_in, rows_per_mod=n, tm=512,
                  rope=(jnp.asarray(cos), jnp.asarray(sin), rope_groups, q_scales), name="attn_in_proj")
    proj_c = _modmm(ctx.reshape(b * nc, d), sc1[b:b + 1], sh1[b:b + 1], w_in[:, Q_W:], rows_per_mod=b * nc, tm=256,
                    name="ctx_in_proj")
    proj = proj.reshape(b, n, ATTN_PROJ_W)
    proj_c = proj_c.reshape(b, nc, KV_W)
    lam_init = 0.8 - 0.6 * math.exp(-0.3 * 0)
    oa = _diff_attn(proj, proj_c, attn_lambda[0], attn_subln_g[0], lam_init, tq=512, sub=128)
    ow = _win_attn(proj, proj_c, attn_sink[0], tq=256)
    w_out = attn_w_out[0].astype(BF16)
    x1, h2, lgt = _proj_ln([oa.reshape(t, DIFF_VW), ow.reshape(t, WIN_Q_W)], [w_out[:DIFF_VW], w_out[DIFF_VW:]],
                           x2d, g1[:b], ln_g[0, 0], ln_b[0, 0], sc2[:b], sh2[:b], router_wt, rows_per_mod=n, tm=512,
                           name="attn_out_proj_ln")
    x2 = _moe(h2, lgt, x1, g2[:b], ln_g[0, 1], ln_b[0, 1], router_bias, exp_w_gate, exp_w_up, exp_w_down, layer=0,
              rows_per_mod=n)

    sh1, sc1, g1, sh2, sc2, g2 = jnp.split(mods[1], 6, axis=-1)
    u = _modmm(x2, sc1[:b], sh1[:b], hy_w_in[0].astype(BF16), rows_per_mod=n, tm=512, name="hyena_in_proj")
    fb, tc = 256, 512
    fwd_np, inv_np = _dft_tables(n, fb)
    fwd_tab, inv_tab = jnp.asarray(fwd_np).astype(BF16), jnp.asarray(inv_np).astype(BF16)
    zp, decay = _filter_tables(n)
    w_in_p = jnp.zeros((FILTER_HID, FILTER_HID), F32).at[:FILTER_EMB].set(hy_ffn_w_in[0])
    hid = _filter_mlp(jnp.asarray(zp), w_in_p, hy_ffn_w_hid[0], hy_ffn_b[0], hy_sin_freq[0])
    hr, gg, nyq = _filter_spec(hid, hy_ffn_w_out[0].reshape(FILTER_HID, 2, 2, d).transpose(1, 2, 0, 3), jnp.asarray(decay), fwd_tab, fb, tc)
    z2 = _hyena_conv(u.reshape(b, n, 3 * d), hy_conv_w[0], hy_conv_b[0].reshape(1, 3 * d), fwd_tab, inv_tab,
                     hr, gg, nyq, hy_skip[0], fb, tc, kps=2)
    x3, h2, lgt = _proj_ln([z2.reshape(t, d)], [hy_w_out[0].astype(BF16)], x2, g1[:b], ln_g[1, 0], ln_b[1, 0],
                           sc2[:b], sh2[:b], router_wt, rows_per_mod=n, tm=512, name="hyena_out_proj_ln")
    x4 = _moe(h2, lgt, x3, g2[:b], ln_g[1, 1], ln_b[1, 1], router_bias, exp_w_gate, exp_w_up, exp_w_down, layer=1,
              rows_per_mod=n)
    return x4.reshape(b, n, d)
```

```python
import functools
import math

import jax
import jax.numpy as jnp
import numpy as np
from jax import lax
from jax.experimental import pallas as pl
from jax.experimental.pallas import tpu as pltpu

F32 = jnp.float32
BF16 = jnp.bfloat16
I32 = jnp.int32

D_MODEL = 1024
DEPTH = 2
GRID_W = 64
HEAD_DIM = 64
DIFF_HEADS = 4
WIN_Q_HEADS = 8
WIN_KV_HEADS = 2
WINDOW = 128
WIN_BLOCK = 128
ROPE_BASE = 10000.0
DIFF_QK_W = DIFF_HEADS * 2 * HEAD_DIM
DIFF_VW = DIFF_HEADS * 2 * HEAD_DIM
WIN_Q_W = WIN_Q_HEADS * HEAD_DIM
WIN_KV_W = WIN_KV_HEADS * HEAD_DIM
Q_W = DIFF_QK_W + WIN_Q_W
KV_W = DIFF_QK_W + DIFF_VW + 2 * WIN_KV_W
ATTN_PROJ_W = Q_W + KV_W
FILTER_EMB = 33
FILTER_HID = 64
DECAY_TARGET = 1e-2
FAST_DECAY_PCT = 0.3
SLOW_DECAY_PCT = 1.5
N_EXPERTS = 16
N_GROUPS = 4
EXPERTS_PER_GROUP = N_EXPERTS // N_GROUPS
EXPERT_FF = 1024
LN_EPS = 1e-5
DEEPNORM_ALPHA = (2 * DEPTH) ** 0.25
NEG_INF = -1e30
LOG2_E = 1.4426950408889634

LANES = 128
MOD_ROWS = 16
VMEM_LIMIT = 60 * 1024 * 1024
MOE_BM = 256
MOE_TILE = 512
ROW_ALIGN = 8
NT_DIMS = (((1,), (1,)), ((), ()))


def _cparams(n_axes):
    return pltpu.CompilerParams(dimension_semantics=("arbitrary",) * n_axes, vmem_limit_bytes=VMEM_LIMIT)


@functools.lru_cache(maxsize=None)
def _rope_tables(n):
    rows = n // GRID_W
    r, col = np.meshgrid(np.arange(rows, dtype=np.float32), np.arange(GRID_W, dtype=np.float32), indexing="ij")
    axis_dim = HEAD_DIM // 2
    inv_freq = (ROPE_BASE ** (-np.arange(0, axis_dim, 2, dtype=np.float32) / axis_dim)).astype(np.float32)
    ang = np.concatenate([r.reshape(-1, 1) * inv_freq, col.reshape(-1, 1) * inv_freq], -1)
    ang = np.concatenate([ang, ang], -1).astype(np.float32)
    cos, sin = np.cos(ang), np.sin(ang)
    half = np.arange(HEAD_DIM) < HEAD_DIM // 2
    sin_signed = np.where(half[None, :], -sin, sin)
    reps = LANES // HEAD_DIM
    return (np.tile(cos, (1, reps)).astype(np.float32), np.tile(sin_signed, (1, reps)).astype(np.float32))


@functools.lru_cache(maxsize=None)
def _dft_tables(n, fb):
    big = 2 * n
    k = np.arange(n, dtype=np.int64)[:, None]
    t = np.arange(n, dtype=np.int64)[None, :]
    ang = ((k * t) % big).astype(np.float64) * (2.0 * math.pi / big)
    c, s = np.cos(ang), np.sin(ang)
    alt = (1 - 2 * (np.arange(n) & 1)).astype(np.float64)
    s_f = s.copy()
    s_f[0, :] = alt
    kb = n // fb
    fwd = np.concatenate([c.reshape(kb, fb, n), s_f.reshape(kb, fb, n)], axis=1)
    ci = c.T * (2.0 / big)
    ci[:, 0] = 1.0 / big
    si = s.T * (2.0 / big)
    si[:, 0] = alt / big
    inv = np.concatenate([ci.reshape(n, kb, fb).transpose(1, 0, 2), si.reshape(n, kb, fb).transpose(1, 0, 2)], axis=2)
    return fwd.astype(np.float32), inv.astype(np.float32)


@functools.lru_cache(maxsize=None)
def _filter_tables(n):
    t = np.linspace(0.0, 1.0, n, dtype=np.float32)[:, None]
    bands = (FILTER_EMB - 1) // 2
    w = (2.0 * math.pi * np.arange(n, dtype=np.float32)[:, None] / n).astype(np.float32)
    fr = np.linspace(1e-4, bands - 1, bands, dtype=np.float32)[None, :]
    z = np.concatenate([t, np.cos(fr * w), -np.sin(fr * w)], -1).astype(np.float32)
    zp = np.zeros((n, FILTER_HID), np.float32)
    zp[:, :FILTER_EMB] = z
    deltas = np.abs(np.linspace(math.log(DECAY_TARGET) / SLOW_DECAY_PCT, math.log(DECAY_TARGET) / FAST_DECAY_PCT,
                                D_MODEL, dtype=np.float32))
    decay = np.exp(-t * deltas[None, :]).astype(np.float32)
    return zp, decay


def _mods_kernel(c_ref, w_ref, b_ref, o_ref):
    c = c_ref[...]
    a = (c * jax.nn.sigmoid(c)).astype(BF16)
    o_ref[...] = jnp.dot(a, w_ref[...].astype(BF16), preferred_element_type=F32) + b_ref[...]


def _mods(c_rows, mod_w, mod_b):
    d = D_MODEL
    tn = 1536
    return pl.pallas_call(
        _mods_kernel,
        grid=(DEPTH, 6 * d // tn),
        in_specs=[pl.BlockSpec((MOD_ROWS, d), lambda l, j: (0, 0)),
                  pl.BlockSpec((None, d, tn), lambda l, j: (l, 0, j)),
                  pl.BlockSpec((None, 1, tn), lambda l, j: (l, 0, j))],
        out_specs=pl.BlockSpec((None, MOD_ROWS, tn), lambda l, j: (l, 0, j)),
        out_shape=jax.ShapeDtypeStruct((DEPTH, MOD_ROWS, 6 * d), F32),
        compiler_params=_cparams(2), name="mods",
    )(c_rows, mod_w, mod_b.reshape(DEPTH, 1, 6 * d))


def _modmm_kernel(*refs, n_groups, rope_groups, scaled_groups, chunk):
    if rope_groups:
        x_ref, sc_ref, sh_ref, w_ref, cos_ref, sin_ref, o_ref = refs
        cos, sin = cos_ref[...], sin_ref[...]
        lane = lax.broadcasted_iota(I32, (1, LANES), 1)
        first_half = (lane % HEAD_DIM) < HEAD_DIM // 2
    else:
        x_ref, sc_ref, sh_ref, w_ref, o_ref = refs
    h = (x_ref[...] * (1.0 + sc_ref[...]) + sh_ref[...]).astype(BF16)
    scales = dict(scaled_groups)
    gpc = chunk // LANES
    for c in range(n_groups // gpc):
        acc = jnp.dot(h, w_ref[:, c * chunk:(c + 1) * chunk], preferred_element_type=F32)
        for j in range(gpc):
            g = c * gpc + j
            blk = acc[:, j * LANES:(j + 1) * LANES]
            if g in rope_groups:
                rot = jnp.where(first_half, pltpu.roll(blk, LANES - HEAD_DIM // 2, 1), pltpu.roll(blk, HEAD_DIM // 2, 1))
                blk = blk * cos + rot * sin
                if g in scales:
                    blk = blk * scales[g]
            o_ref[:, g * LANES:(g + 1) * LANES] = blk.astype(o_ref.dtype)


def _modmm(x2d, sc, sh, w, rows_per_mod, tm, rope=None, name="modmm"):
    t, d = x2d.shape
    n = w.shape[1]
    tiles_per_mod = rows_per_mod // tm
    nmod = sc.shape[0]
    in_specs = [pl.BlockSpec((tm, d), lambda i: (i, 0)),
                pl.BlockSpec((None, 1, d), lambda i: (i // tiles_per_mod, 0, 0)),
                pl.BlockSpec((None, 1, d), lambda i: (i // tiles_per_mod, 0, 0)),
                pl.BlockSpec((d, n), lambda i: (0, 0))]
    args = [x2d, sc.reshape(nmod, 1, d), sh.reshape(nmod, 1, d), w]
    rope_groups, scaled_groups = (), ()
    if rope is not None:
        cos, sin, rope_groups, scaled_groups = rope
        in_specs += [pl.BlockSpec((tm, LANES), lambda i: (i % tiles_per_mod, 0)),
                     pl.BlockSpec((tm, LANES), lambda i: (i % tiles_per_mod, 0))]
        args += [cos, sin]
    kern = functools.partial(_modmm_kernel, n_groups=n // LANES, rope_groups=tuple(rope_groups),
                             scaled_groups=tuple(scaled_groups), chunk=256)
    return pl.pallas_call(
        kern, grid=(t // tm,), in_specs=in_specs,
        out_specs=pl.BlockSpec((tm, n), lambda i: (i, 0)),
        out_shape=jax.ShapeDtypeStruct((t, n), BF16),
        compiler_params=_cparams(1), name=name,
    )(*args)


def _diff_attn_kernel(q_ref, k_ref, v_ref, kc_ref, vc_ref, lam_ref, g_ref, o_ref, *, lam_init, sub):
    lane = lax.broadcasted_iota(I32, (1, LANES), 1)
    k, kc = k_ref[...], kc_ref[...]
    v, vc = v_ref[...], vc_ref[...]

    def scores(r0):
        q = q_ref[r0:r0 + sub, :]
        zero = jnp.zeros_like(q)
        out = []
        for qm in (jnp.where(lane < HEAD_DIM, q, zero), jnp.where(lane >= HEAD_DIM, q, zero)):
            out.append((lax.dot_general(qm, k, NT_DIMS, preferred_element_type=F32),
                        lax.dot_general(qm, kc, NT_DIMS, preferred_element_type=F32)))
        return out

    def probs(sl, sc):
        m = jnp.maximum(jnp.max(sl, -1, keepdims=True), jnp.max(sc, -1, keepdims=True))
        pl_, pc = jnp.exp2(sl - m), jnp.exp2(sc - m)
        den = jnp.sum(pl_, -1, keepdims=True) + jnp.sum(pc, -1, keepdims=True)
        return pl_, pc, 1.0 / den

    lv = lam_ref[...]
    lam = (jnp.exp(jnp.sum(lv[0:1] * lv[1:2], keepdims=True)) - jnp.exp(jnp.sum(lv[2:3] * lv[3:4], keepdims=True))
           + lam_init)
    starts = list(range(0, q_ref.shape[0], sub))
    nxt = scores(starts[0])
    for i, r0 in enumerate(starts):
        cur = nxt
        if i + 1 < len(starts):
            nxt = scores(starts[i + 1])
        p1l, p1c, r1 = probs(*cur[0])
        p2l, p2c, r2 = probs(*cur[1])
        w2 = lam * r2
        al = (p1l * r1 - p2l * w2).astype(BF16)
        ac = (p1c * r1 - p2c * w2).astype(BF16)
        o = jnp.dot(al, v, preferred_element_type=F32) + jnp.dot(ac, vc, preferred_element_type=F32)
        ms = jnp.mean(o * o, -1, keepdims=True)
        o_ref[r0:r0 + sub, :] = (o * lax.rsqrt(ms + LN_EPS) * g_ref[...] * (1.0 - lam_init)).astype(o_ref.dtype)


def _diff_attn(proj, proj_c, lam_vec, subln_g, lam_init, tq, sub):
    b, n, _ = proj.shape
    nc = proj_c.shape[1]
    kcol = Q_W // LANES
    vcol = (Q_W + DIFF_QK_W) // LANES
    vccol = DIFF_QK_W // LANES
    kern = functools.partial(_diff_attn_kernel, lam_init=lam_init, sub=sub)
    return pl.pallas_call(
        kern, grid=(b, DIFF_HEADS, n // tq),
        in_specs=[pl.BlockSpec((None, tq, LANES), lambda bi, h, i: (bi, i, h)),
                  pl.BlockSpec((None, n, LANES), lambda bi, h, i: (bi, 0, kcol + h)),
                  pl.BlockSpec((None, n, LANES), lambda bi, h, i: (bi, 0, vcol + h)),
                  pl.BlockSpec((None, nc, LANES), lambda bi, h, i: (bi, 0, h)),
                  pl.BlockSpec((None, nc, LANES), lambda bi, h, i: (bi, 0, vccol + h)),
                  pl.BlockSpec((4, HEAD_DIM), lambda bi, h, i: (0, 0)),
                  pl.BlockSpec((1, LANES), lambda bi, h, i: (0, 0))],
        out_specs=pl.BlockSpec((None, tq, LANES), lambda bi, h, i: (bi, i, h)),
        out_shape=jax.ShapeDtypeStruct((b, n, DIFF_VW), BF16),
        compiler_params=_cparams(3), name="diff_attn",
    )(proj, proj, proj, proj_c, proj_c, lam_vec, subln_g.reshape(1, LANES))


def _win_attn_kernel(sink_ref, q_ref, k_ref, v_ref, kc_ref, vc_ref, o_ref, *, seq, tq):
    n = pl.program_id(1)
    gq = WIN_Q_HEADS // WIN_KV_HEADS
    kw = tq + 2 * WINDOW
    start = pl.multiple_of(jnp.clip(n * tq - WINDOW, 0, seq - kw), WINDOW)
    k_win, v_win = k_ref[pl.ds(start, kw), :], v_ref[pl.ds(start, kw), :]
    kc, vc = kc_ref[...], vc_ref[...]
    lane = lax.broadcasted_iota(I32, (1, LANES), 1)
    row = lax.broadcasted_iota(I32, (gq * tq, 1), 0)
    q_abs = n * tq + row % tq
    k_abs = start + lax.broadcasted_iota(I32, (1, kw), 1)
    allowed = jnp.abs(q_abs - k_abs) <= WINDOW
    head = row // tq
    q = q_ref[...].astype(F32)
    for g in range(WIN_KV_HEADS):
        in_g = (lane // HEAD_DIM) == g
        parts = []
        for j in range(gq):
            hq = g * gq + j
            x = q[:, (hq // 2) * LANES:(hq // 2 + 1) * LANES]
            if hq % 2 != g:
                x = pltpu.roll(x, HEAD_DIM, 1)
            parts.append(jnp.where(in_g, x, 0.0))
        qs = jnp.concatenate(parts, 0).astype(BF16)
        s_loc = lax.dot_general(qs, k_win, NT_DIMS, preferred_element_type=F32)
        s_ctx = lax.dot_general(qs, kc, NT_DIMS, preferred_element_type=F32)
        s_loc = jnp.where(allowed, s_loc, NEG_INF)
        sk = jnp.zeros((gq * tq, 1), F32)
        for j in range(gq):
            sk = jnp.where(head == j, sink_ref[g * gq + j], sk)
        m = jnp.maximum(jnp.maximum(jnp.max(s_loc, -1, keepdims=True), jnp.max(s_ctx, -1, keepdims=True)), sk)
        p_loc, p_ctx = jnp.exp(s_loc - m), jnp.exp(s_ctx - m)
        den = jnp.sum(p_loc, -1, keepdims=True) + jnp.sum(p_ctx, -1, keepdims=True) + jnp.exp(sk - m)
        o = (jnp.dot(p_loc.astype(BF16), v_win, preferred_element_type=F32)
             + jnp.dot(p_ctx.astype(BF16), vc, preferred_element_type=F32)) * (1.0 / den)
        for cb in range(gq // 2):
            pair = []
            for half in range(2):
                piece = o[(2 * cb + half) * tq:(2 * cb + half + 1) * tq]
                pair.append(piece if half == g else pltpu.roll(piece, HEAD_DIM, 1))
            col = (g * gq // 2 + cb) * LANES
            o_ref[:, col:col + LANES] = jnp.where(lane < HEAD_DIM, pair[0], pair[1]).astype(o_ref.dtype)


def _win_attn(proj, proj_c, sink, tq):
    b, n, _ = proj.shape
    nc = proj_c.shape[1]
    qcol = DIFF_QK_W // WIN_Q_W
    kcol = (Q_W + DIFF_QK_W + DIFF_VW) // LANES
    kccol = (DIFF_QK_W + DIFF_VW) // LANES
    kern = functools.partial(_win_attn_kernel, seq=n, tq=tq)
    return pl.pallas_call(
        kern, grid=(b, n // tq),
        in_specs=[pl.BlockSpec(memory_space=pltpu.SMEM),
                  pl.BlockSpec((None, tq, WIN_Q_W), lambda bi, i: (bi, i, qcol)),
                  pl.BlockSpec((None, n, LANES), lambda bi, i: (bi, 0, kcol)),
                  pl.BlockSpec((None, n, LANES), lambda bi, i: (bi, 0, kcol + 1)),
                  pl.BlockSpec((None, nc, LANES), lambda bi, i: (bi, 0, kccol)),
                  pl.BlockSpec((None, nc, LANES), lambda bi, i: (bi, 0, kccol + 1))],
        out_specs=pl.BlockSpec((None, tq, WIN_Q_W), lambda bi, i: (bi, i, 0)),
        out_shape=jax.ShapeDtypeStruct((b, n, WIN_Q_W), BF16),
        compiler_params=_cparams(2), name="win_attn",
    )(sink, proj, proj, proj, proj_c, proj_c)


def _layer_norm(r, g, b):
    mu = jnp.mean(r, -1, keepdims=True)
    dlt = r - mu
    var = jnp.mean(dlt * dlt, -1, keepdims=True)
    return dlt * lax.rsqrt(var + LN_EPS) * g + b


def _proj_ln_kernel(*refs, n_in, sub):
    a_refs = refs[:n_in]
    w_refs = refs[n_in:2 * n_in]
    x_ref, gate_ref, lng_ref, lnb_ref, sc_ref, sh_ref, rw_ref, x1_ref, h2_ref, lg_ref = refs[2 * n_in:]

    def split(v):
        hi = v.astype(BF16)
        return hi, (v - hi.astype(F32)).astype(BF16)

    def project(r0):
        y = jnp.dot(a_refs[0][r0:r0 + sub, :], w_refs[0][...], preferred_element_type=F32)
        for a_ref, w_ref in zip(a_refs[1:], w_refs[1:]):
            y = y + jnp.dot(a_ref[r0:r0 + sub, :], w_ref[...], preferred_element_type=F32)
        return y

    rw_hi, rw_lo = split(rw_ref[...])
    rw_both = jnp.concatenate([rw_hi, rw_lo], 1)
    starts = list(range(0, x_ref.shape[0], sub))
    nxt = project(starts[0])
    for i, r0 in enumerate(starts):
        y = nxt
        if i + 1 < len(starts):
            nxt = project(starts[i + 1])
        xn = _layer_norm(DEEPNORM_ALPHA * x_ref[r0:r0 + sub, :] + gate_ref[...] * y, lng_ref[...], lnb_ref[...])
        x1_ref[r0:r0 + sub, :] = xn
        h2 = xn * (1.0 + sc_ref[...]) + sh_ref[...]
        h2_ref[r0:r0 + sub, :] = h2.astype(h2_ref.dtype)
        h_hi, h_lo = split(h2)
        hh = jnp.dot(h_hi, rw_both, preferred_element_type=F32)
        lg = hh[:, :LANES] + hh[:, LANES:] + jnp.dot(h_lo, rw_hi, preferred_element_type=F32)
        lg_ref[:, r0:r0 + sub] = lg.T[:N_EXPERTS, :]


def _proj_ln(acts, ws, x2d, gate, ln_g, ln_b, sc, sh, router_wt, rows_per_mod, tm, name):
    t, d = x2d.shape
    n_in = len(acts)
    tiles_per_mod = rows_per_mod // tm
    nmod = gate.shape[0]
    row = lambda i: (i, 0)
    full = lambda i: (0, 0)
    mod = lambda i: (i // tiles_per_mod, 0, 0)
    in_specs = ([pl.BlockSpec((tm, a.shape[1]), row) for a in acts]
                + [pl.BlockSpec(w.shape, full) for w in ws]
                + [pl.BlockSpec((tm, d), row), pl.BlockSpec((None, 1, d), mod),
                   pl.BlockSpec((1, d), full), pl.BlockSpec((1, d), full),
                   pl.BlockSpec((None, 1, d), mod), pl.BlockSpec((None, 1, d), mod),
                   pl.BlockSpec((d, LANES), full)])
    return pl.pallas_call(
        functools.partial(_proj_ln_kernel, n_in=n_in, sub=LANES), grid=(t // tm,), in_specs=in_specs,
        out_specs=[pl.BlockSpec((tm, d), row), pl.BlockSpec((tm, d), row),
                   pl.BlockSpec((N_EXPERTS, tm), lambda i: (0, i))],
        out_shape=[jax.ShapeDtypeStruct((t, d), F32), jax.ShapeDtypeStruct((t, d), BF16),
                   jax.ShapeDtypeStruct((N_EXPERTS, t), F32)],
        compiler_params=_cparams(1), name=name,
    )(*acts, *ws, x2d, gate.reshape(nmod, 1, d), ln_g.reshape(1, d), ln_b.reshape(1, d),
      sc.reshape(nmod, 1, d), sh.reshape(nmod, 1, d), router_wt)


def _first_argmax(vals):
    idx = jnp.zeros(vals[0].shape, I32)
    best = vals[0]
    for j in range(1, len(vals)):
        upd = vals[j] > best
        idx = jnp.where(upd, j, idx)
        best = jnp.where(upd, vals[j], best)
    return idx, best


def _route_kernel(lg_ref, bias_ref, pos_ref, w_ref, tab_ref, cnt_ref, carry_ref, off_ref, *, tr):
    @pl.when(pl.program_id(0) == 0)
    def _():
        carry_ref[...] = jnp.zeros_like(carry_ref)

    lg = lg_ref[...]
    ex = jnp.exp(lg - jnp.max(lg, 0, keepdims=True))
    scores = ex / jnp.sum(ex, 0, keepdims=True)
    sel = scores + bias_ref[...]
    rows = [sel[e:e + 1] for e in range(N_EXPERTS)]
    group_scores = []
    for g in range(N_GROUPS):
        r = rows[g * EXPERTS_PER_GROUP:(g + 1) * EXPERTS_PER_GROUP]
        best = None
        for i in range(EXPERTS_PER_GROUP):
            for j in range(i + 1, EXPERTS_PER_GROUP):
                s = r[i] + r[j]
                best = s if best is None else jnp.maximum(best, s)
        group_scores.append(best)
    grp, _ = _first_argmax(group_scores)
    vals = []
    for j in range(EXPERTS_PER_GROUP):
        v = rows[(N_GROUPS - 1) * EXPERTS_PER_GROUP + j]
        for g in range(N_GROUPS - 2, -1, -1):
            v = jnp.where(grp == g, rows[g * EXPERTS_PER_GROUP + j], v)
        vals.append(v)
    i0, _ = _first_argmax(vals)
    i1, _ = _first_argmax([jnp.where(i0 == j, -jnp.inf, vals[j]) for j in range(EXPERTS_PER_GROUP)])
    e0 = grp * EXPERTS_PER_GROUP + i0
    e1 = grp * EXPERTS_PER_GROUP + i1
    eid = lax.broadcasted_iota(I32, (N_EXPERTS, 1), 0)
    oh0, oh1 = eid == e0, eid == e1
    s0 = jnp.sum(jnp.where(oh0, scores, 0.0), 0, keepdims=True)
    s1 = jnp.sum(jnp.where(oh1, scores, 0.0), 0, keepdims=True)
    den = s0 + s1
    member = jnp.where(oh0 | oh1, 1.0, 0.0)
    before = lax.broadcasted_iota(I32, (tr, tr), 0) < lax.broadcasted_iota(I32, (tr, tr), 1)
    upper = jnp.where(before, 1.0, 0.0).astype(BF16)
    cnt = jnp.dot(member.astype(BF16), upper, preferred_element_type=F32)
    run = jnp.sum(member, 1, keepdims=True)
    run = jnp.floor((run + (ROW_ALIGN - 1)) * (1.0 / ROW_ALIGN)) * ROW_ALIGN
    run = jnp.broadcast_to(run, (N_EXPERTS, LANES))
    acc = jnp.zeros((1, LANES), F32)
    for e in range(N_EXPERTS):
        off_ref[e:e + 1, :] = acc
        acc = acc + run[e:e + 1]
    off = off_ref[...]
    at = off[:, 0:1] + cnt
    pos_ref[0:1, :] = jnp.sum(jnp.where(oh0, at, 0.0), 0, keepdims=True).astype(I32)
    pos_ref[1:2, :] = jnp.sum(jnp.where(oh1, at, 0.0), 0, keepdims=True).astype(I32)
    w_ref[0:1, :] = s0 / den
    w_ref[1:2, :] = s1 / den
    tab_ref[0] = off.astype(I32)
    tab_ref[1] = run.astype(I32)
    tab_ref[2] = carry_ref[...].astype(I32)
    carry_ref[...] = carry_ref[...] + run
    cnt_ref[...] = carry_ref[...]


def _route(logits_t, router_bias, tr):
    t = logits_t.shape[1]
    tok = lambda i: (0, i)
    return pl.pallas_call(
        functools.partial(_route_kernel, tr=tr), grid=(t // tr,),
        in_specs=[pl.BlockSpec((N_EXPERTS, tr), tok), pl.BlockSpec((N_EXPERTS, 1), lambda i: (0, 0))],
        out_specs=[pl.BlockSpec((2, tr), tok), pl.BlockSpec((2, tr), tok),
                   pl.BlockSpec((None, 3, N_EXPERTS, LANES), lambda i: (i, 0, 0, 0)),
                   pl.BlockSpec((N_EXPERTS, LANES), lambda i: (0, 0))],
        out_shape=[jax.ShapeDtypeStruct((2, t), I32), jax.ShapeDtypeStruct((2, t), F32),
                   jax.ShapeDtypeStruct((t // tr, 3, N_EXPERTS, LANES), I32),
                   jax.ShapeDtypeStruct((N_EXPERTS, LANES), F32)],
        scratch_shapes=[pltpu.VMEM((N_EXPERTS, LANES), F32), pltpu.VMEM((N_EXPERTS, LANES), F32)],
        compiler_params=_cparams(1), name="route",
    )(logits_t, router_bias.reshape(N_EXPERTS, 1))


def _run_copies(n, max_rows, make_copy, wait):
    sz = max_rows
    while sz >= ROW_ALIGN:
        start = (n // (2 * sz)) * (2 * sz)

        @pl.when((n & sz) != 0)
        def _(start=start, sz=sz):
            cp = make_copy(start, sz)
            cp.wait() if wait else cp.start()

        sz //= 2


def _rows(ref, start, size):
    return ref.at[pl.ds(pl.multiple_of(start, ROW_ALIGN), size)]


def _dispatch_kernel(tab_ref, h_ref, pos_ref, w_ref, xs_ref, buf, zbuf, sem, *, tt, n_tiles, bm):
    i = pl.program_id(0)
    d = h_ref.shape[1]
    sb = buf.shape[1]
    ne = N_EXPERTS
    fill = 3 * n_tiles * ne

    @pl.when(i == 0)
    def _():
        zbuf[...] = jnp.zeros_like(zbuf)
        for wait in (False, True):
            for e in range(ne):
                dst, n = tab_ref[fill + e], tab_ref[fill + ne + e]
                _run_copies(n, bm // 2, lambda s, z: pltpu.make_async_copy(
                    zbuf.at[pl.ds(0, z)], _rows(xs_ref, dst + s, z), sem.at[2]), wait)

        def zero_block(j, carry):
            for half in range(2):
                cp = pltpu.make_async_copy(zbuf, _rows(xs_ref, j * bm + half * (bm // 2), bm // 2), sem.at[2])
                cp.start()
                cp.wait()
            return carry

        lax.fori_loop(tab_ref[fill + 2 * ne], xs_ref.shape[0] // bm, zero_block, 0)

    def tile_copies(tile, slot, wait):
        for e in range(ne):
            off = tab_ref[tile * ne + e]
            n = tab_ref[(n_tiles + tile) * ne + e]
            dst = tab_ref[(2 * n_tiles + tile) * ne + e]
            _run_copies(n, tt, lambda s, z: pltpu.make_async_copy(
                _rows(buf.at[slot], off + s, z), _rows(xs_ref, dst + s, z), sem.at[slot]), wait)

    slot = i % 2
    pos, w = pos_ref[...], w_ref[...]
    srow = lax.broadcasted_iota(I32, (sb, 1), 0)
    m0, m1 = srow == pos[0:1], srow == pos[1:2]
    perm = (jnp.where(m0, 1.0, 0.0) + jnp.where(m1, 1.0, 0.0)).astype(BF16)
    buf[slot, :, :d] = jnp.dot(perm, h_ref[...].astype(BF16), preferred_element_type=F32)
    wrow = jnp.sum(jnp.where(m0, w[0:1], 0.0) + jnp.where(m1, w[1:2], 0.0), 1, keepdims=True)
    buf[slot, :, d:] = jnp.broadcast_to(wrow, (sb, LANES))

    @pl.when(i > 0)
    def _():
        tile_copies(i - 1, 1 - slot, True)

    tile_copies(i, slot, False)

    @pl.when(i == n_tiles - 1)
    def _():
        tile_copies(i, slot, True)


def _dispatch(tab, h2, pos, wts, n_slots, tt, sb, bm):
    t, d = h2.shape
    n_tiles = t // tt
    return pl.pallas_call(
        functools.partial(_dispatch_kernel, tt=tt, n_tiles=n_tiles, bm=bm),
        grid_spec=pltpu.PrefetchScalarGridSpec(
            num_scalar_prefetch=1, grid=(n_tiles,),
            in_specs=[pl.BlockSpec((tt, d), lambda i, s: (i, 0)), pl.BlockSpec((2, tt), lambda i, s: (0, i)),
                      pl.BlockSpec((2, tt), lambda i, s: (0, i))],
            out_specs=pl.BlockSpec(memory_space=pl.ANY),
            scratch_shapes=[pltpu.VMEM((2, sb, d + LANES), F32), pltpu.VMEM((bm // 2, d + LANES), F32),
                            pltpu.SemaphoreType.DMA((3,))]),
        out_shape=jax.ShapeDtypeStruct((n_slots, d + LANES), F32),
        compiler_params=_cparams(1), name="moe_dispatch",
    )(tab, h2, pos, wts)


def _ffn_kernel(blk_e_ref, nused_ref, x_ref, wg_ref, wu_ref, wd_ref, o_ref, wg_bf, wu_bf, wd_bf):
    i = pl.program_id(0)
    new_expert = (i == 0) | (blk_e_ref[i] != blk_e_ref[jnp.maximum(i - 1, 0)])

    @pl.when(new_expert)
    def _():
        wg_bf[...] = wg_ref[...].astype(BF16)
        wu_bf[...] = wu_ref[...].astype(BF16)
        wd_bf[...] = wd_ref[...].astype(BF16)

    @pl.when(i < nused_ref[0])
    def _():
        d = wg_bf.shape[0]
        x = x_ref[:, :d].astype(BF16)
        gate = jnp.dot(x, wg_bf[...], preferred_element_type=F32)
        up = jnp.dot(x, wu_bf[...], preferred_element_type=F32)
        act = (gate * jax.nn.sigmoid(gate) * up).astype(BF16)
        o_ref[...] = jnp.dot(act, wd_bf[...], preferred_element_type=F32) * x_ref[:, d:d + 1]

    @pl.when(i >= nused_ref[0])
    def _():
        o_ref[...] = jnp.zeros_like(o_ref)


def _ffn(blk_e, n_used, xs, wg, wu, wd, layer, bm):
    n_slots, xw = xs.shape
    d, ff = wg.shape[2:]
    xrow = lambda i, be, nu: (jnp.minimum(i, nu[0] - 1), 0)
    wsel = lambda i, be, nu: (layer, be[i], 0, 0)
    return pl.pallas_call(
        _ffn_kernel,
        grid_spec=pltpu.PrefetchScalarGridSpec(
            num_scalar_prefetch=2, grid=(n_slots // bm,),
            in_specs=[pl.BlockSpec((bm, xw), xrow), pl.BlockSpec((None, None, d, ff), wsel),
                      pl.BlockSpec((None, None, d, ff), wsel), pl.BlockSpec((None, None, ff, d), wsel)],
            out_specs=pl.BlockSpec((bm, d), lambda i, be, nu: (i, 0)),
            scratch_shapes=[pltpu.VMEM((d, ff), BF16), pltpu.VMEM((d, ff), BF16), pltpu.VMEM((ff, d), BF16)]),
        out_shape=jax.ShapeDtypeStruct((n_slots, d), F32),
        compiler_params=_cparams(1), name="moe_ffn",
    )(blk_e, n_used, xs, wg, wu, wd)


def _combine_kernel(tab_ref, ys_ref, pos_ref, x1_ref, gate_ref, lng_ref, lnb_ref, o_ref, ybuf, sem, *, tt, n_tiles):
    i = pl.program_id(0)
    sb = ybuf.shape[1]
    ne = N_EXPERTS
    slot = i % 2

    def tile_copies(tile, slot, wait):
        for e in range(ne):
            off = tab_ref[tile * ne + e]
            n = tab_ref[(n_tiles + tile) * ne + e]
            src = tab_ref[(2 * n_tiles + tile) * ne + e]
            _run_copies(n, tt, lambda s, z: pltpu.make_async_copy(
                _rows(ys_ref, src + s, z), _rows(ybuf.at[slot], off + s, z), sem.at[slot]), wait)

    @pl.when(i == 0)
    def _():
        ybuf[...] = jnp.zeros_like(ybuf)
        tile_copies(0, 0, False)

    @pl.when(i + 1 < n_tiles)
    def _():
        tile_copies(i + 1, 1 - slot, False)

    tile_copies(i, slot, True)
    pos = pos_ref[...]
    scol = lax.broadcasted_iota(I32, (1, sb), 1)
    unperm = (jnp.where(scol == pos[:, 0:1], 1.0, 0.0) + jnp.where(scol == pos[:, 1:2], 1.0, 0.0)).astype(BF16)
    f = jnp.dot(unperm, ybuf[slot].astype(BF16), preferred_element_type=F32)
    o_ref[...] = _layer_norm(DEEPNORM_ALPHA * x1_ref[...] + gate_ref[...] * f, lng_ref[...], lnb_ref[...])


def _combine(tab, ys, pos_t, x1, gate, ln_g, ln_b, rows_per_mod, tt, sb):
    t, d = x1.shape
    tiles_per_mod = rows_per_mod // tt
    nmod = gate.shape[0]
    return pl.pallas_call(
        functools.partial(_combine_kernel, tt=tt, n_tiles=t // tt),
        grid_spec=pltpu.PrefetchScalarGridSpec(
            num_scalar_prefetch=1, grid=(t // tt,),
            in_specs=[pl.BlockSpec(memory_space=pl.ANY),
                      pl.BlockSpec((tt, 2), lambda i, s: (i, 0)),
                      pl.BlockSpec((tt, d), lambda i, s: (i, 0)),
                      pl.BlockSpec((None, 1, d), lambda i, s: (i // tiles_per_mod, 0, 0)),
                      pl.BlockSpec((1, d), lambda i, s: (0, 0)), pl.BlockSpec((1, d), lambda i, s: (0, 0))],
            out_specs=pl.BlockSpec((tt, d), lambda i, s: (i, 0)),
            scratch_shapes=[pltpu.VMEM((2, sb, d), F32), pltpu.SemaphoreType.DMA((2,))]),
        out_shape=jax.ShapeDtypeStruct((t, d), F32),
        compiler_params=_cparams(1), name="moe_combine",
    )(tab, ys, pos_t, x1, gate.reshape(nmod, 1, d), ln_g.reshape(1, d), ln_b.reshape(1, d))


def _moe(h2, logits_t, x1, gate, ln_g, ln_b, router_bias, wg, wu, wd, layer, rows_per_mod):
    t, d = h2.shape
    bm, tt, ne = MOE_BM, MOE_TILE, N_EXPERTS
    n_tiles = t // tt
    pad = ne * (ROW_ALIGN - 1)
    sb = -(-(2 * tt + pad) // LANES) * LANES
    pos, wts, tab3, cnt = _route(logits_t, router_bias, tr=tt)
    rows = cnt[:, 0].astype(I32)
    prows = (rows + bm - 1) // bm * bm
    pends = jnp.cumsum(prows)
    pstarts = pends - prows
    n_blk = -(-(2 * t + n_tiles * pad + ne * (bm - 1)) // bm)
    n_used = pends[-1] // bm
    blk_ids = jnp.minimum(jnp.arange(n_blk, dtype=I32), n_used - 1)
    blk_e = jnp.minimum(jnp.sum((blk_ids[:, None] * bm >= pends[None, :]).astype(I32), 1), ne - 1)
    tab3 = tab3[:, :, :, 0]
    tab = jnp.concatenate([tab3[:, 0].reshape(-1), tab3[:, 1].reshape(-1),
                           (tab3[:, 2] + pstarts[None, :]).reshape(-1), pstarts + rows, prows - rows,
                           n_used.reshape(1)]).astype(I32)
    xs = _dispatch(tab, h2, pos, wts, n_blk * bm, tt, sb, bm)
    ys = _ffn(blk_e, n_used.reshape(1).astype(I32), xs, wg, wu, wd, layer, bm)
    return _combine(tab, ys, pos.T, x1, gate, ln_g, ln_b, rows_per_mod, tt, sb)


def _filter_mlp_kernel(z_ref, w1_ref, w2_ref, w3_ref, b_ref, sf_ref, o_ref):
    hp = lax.Precision.HIGHEST
    b, sf = b_ref[...], sf_ref[...]
    h = jnp.sin(sf[0:1] * (jnp.dot(z_ref[...], w1_ref[...], precision=hp, preferred_element_type=F32) + b[0:1]))
    h = jnp.sin(sf[1:2] * (jnp.dot(h, w2_ref[...], precision=hp, preferred_element_type=F32) + b[1:2]))
    o_ref[...] = jnp.sin(sf[2:3] * (jnp.dot(h, w3_ref[...], precision=hp, preferred_element_type=F32) + b[2:3]))


def _filter_mlp(zp, w_in_p, w_hid, b, sf):
    n = zp.shape[0]
    return pl.pallas_call(
        _filter_mlp_kernel, out_shape=jax.ShapeDtypeStruct((n, FILTER_HID), F32),
        compiler_params=pltpu.CompilerParams(vmem_limit_bytes=VMEM_LIMIT), name="hyena_filter_mlp",
    )(zp, w_in_p, w_hid[0], w_hid[1], b, sf)


def _filter_spec_kernel(hid_ref, wf_ref, wb_ref, dec_ref, fwd_ref, hr_ref, g_ref, nyq_ref, p_scr, q_scr, *, fb):
    k = pl.program_id(2)
    n = hid_ref.shape[0]

    @pl.when(k == 0)
    def _():
        hp = lax.Precision.HIGHEST
        hid, dec = hid_ref[...], dec_ref[...]
        row = lax.broadcasted_iota(I32, (n, 1), 0)
        fw = jnp.dot(hid, wf_ref[...], precision=hp, preferred_element_type=F32) * dec
        bw = jnp.dot(hid, wb_ref[...], precision=hp, preferred_element_type=F32) * dec
        bw = jnp.where(row == 0, 0.0, bw)
        p = fw + bw
        p_scr[...] = p.astype(BF16)
        q_scr[...] = (fw - bw).astype(BF16)
        sign = (1 - 2 * (row & 1)).astype(F32)
        nyq_ref[...] = jnp.sum(p * sign, 0, keepdims=True)

    hr_ref[...] = jnp.dot(fwd_ref[0:fb, :], p_scr[...], preferred_element_type=F32)
    gg = jnp.dot(fwd_ref[fb:2 * fb, :], q_scr[...], preferred_element_type=F32)
    rowb = lax.broadcasted_iota(I32, (fb, 1), 0)
    g_ref[...] = jnp.where((rowb == 0) & (k == 0), 0.0, gg)


def _filter_spec(hid, w_out4, decay, fwd_tab, fb, tc):
    n = hid.shape[0]
    d = D_MODEL
    kb = n // fb
    return pl.pallas_call(
        functools.partial(_filter_spec_kernel, fb=fb), grid=(2, d // tc, kb),
        in_specs=[pl.BlockSpec((n, FILTER_HID), lambda o, c, k: (0, 0)),
                  pl.BlockSpec((None, None, FILTER_HID, tc), lambda o, c, k: (0, o, 0, c)),
                  pl.BlockSpec((None, None, FILTER_HID, tc), lambda o, c, k: (1, o, 0, c)),
                  pl.BlockSpec((n, tc), lambda o, c, k: (0, c)),
                  pl.BlockSpec((None, 2 * fb, n), lambda o, c, k: (k, 0, 0))],
        out_specs=[pl.BlockSpec((None, fb, tc), lambda o, c, k: (o, k, c)),
                   pl.BlockSpec((None, fb, tc), lambda o, c, k: (o, k, c)),
                   pl.BlockSpec((None, 1, tc), lambda o, c, k: (o, 0, c))],
        out_shape=[jax.ShapeDtypeStruct((2, n, d), F32), jax.ShapeDtypeStruct((2, n, d), F32),
                   jax.ShapeDtypeStruct((2, 1, d), F32)],
        scratch_shapes=[pltpu.VMEM((n, tc), BF16), pltpu.VMEM((n, tc), BF16)],
        compiler_params=_cparams(3), name="hyena_filter_spec",
    )(hid, w_out4, w_out4, decay, fwd_tab)


def _short_conv(u_ref, cw_ref, cb_ref):
    n = u_ref.shape[0]
    u = u_ref[...].astype(F32)
    row = lax.broadcasted_iota(I32, (n, 1), 0)
    prev = jnp.where(row == 0, 0.0, pltpu.roll(u, 1, 0))
    nxt = jnp.where(row == n - 1, 0.0, pltpu.roll(u, n - 1, 0))
    cw = cw_ref[...]
    return prev * cw[0:1] + u * cw[1:2] + nxt * cw[2:3] + cb_ref[...]


def _hyena_conv_kernel(uv_ref, ux1_ref, ux2_ref, cwv_ref, cwx1_ref, cwx2_ref, cbv_ref, cbx1_ref, cbx2_ref,
                       fwd_ref, inv_ref, hr_ref, g_ref, nyq_ref, sk_ref, o_ref, zb_scr, z32_scr, acc_scr, *, fb, kb,
                       kps):
    o = pl.program_id(2)
    k = pl.program_id(3)

    @pl.when((o == 0) & (k == 0))
    def _():
        v = _short_conv(uv_ref, cwv_ref, cbv_ref)
        z32_scr[...] = v
        zb_scr[...] = v.astype(BF16)

    @pl.when(k == 0)
    def _():
        acc_scr[...] = jnp.zeros_like(acc_scr)

    zb = zb_scr[...]
    zfs = [jnp.dot(fwd_ref[j], zb, preferred_element_type=F32) for j in range(kps)]
    rowb = lax.broadcasted_iota(I32, (fb, 1), 0)
    for j in range(kps):
        zr, zi = zfs[j][:fb], zfs[j][fb:]
        hr, gg = hr_ref[j * fb:(j + 1) * fb, :], g_ref[j * fb:(j + 1) * fb, :]
        hb = jnp.where((rowb == 0) & (k == 0), nyq_ref[...], hr) if j == 0 else hr
        y = jnp.concatenate([zr * hr - zi * gg, zr * gg + zi * hb], 0).astype(BF16)
        acc_scr[...] += jnp.dot(inv_ref[j], y, preferred_element_type=F32)

    @pl.when((k == kb - 1) & (o == 0))
    def _():
        zn = _short_conv(ux1_ref, cwx1_ref, cbx1_ref) * (acc_scr[...] + z32_scr[...] * sk_ref[...])
        z32_scr[...] = zn
        zb_scr[...] = zn.astype(BF16)

    @pl.when((k == kb - 1) & (o == 1))
    def _():
        zn = _short_conv(ux2_ref, cwx2_ref, cbx2_ref) * (acc_scr[...] + z32_scr[...] * sk_ref[...])
        o_ref[...] = zn.astype(o_ref.dtype)


def _hyena_conv(u, conv_w, conv_b, fwd_tab, inv_tab, hr, gg, nyq, skip, fb, tc, kps):
    b, n, _ = u.shape
    d = D_MODEL
    kb = n // (fb * kps)
    ncb = d // tc
    ucol = lambda part: (lambda bi, c, o, k: (bi, 0, part * ncb + c))
    wcol = lambda part: (lambda bi, c, o, k: (0, part * ncb + c))
    spec = lambda bi, c, o, k: (o, k, c)
    per_o = lambda bi, c, o, k: (o, 0, c)
    return pl.pallas_call(
        functools.partial(_hyena_conv_kernel, fb=fb, kb=kb, kps=kps), grid=(b, ncb, 2, kb),
        in_specs=[pl.BlockSpec((None, n, tc), ucol(0)), pl.BlockSpec((None, n, tc), ucol(1)),
                  pl.BlockSpec((None, n, tc), ucol(2)),
                  pl.BlockSpec((3, tc), wcol(0)), pl.BlockSpec((3, tc), wcol(1)), pl.BlockSpec((3, tc), wcol(2)),
                  pl.BlockSpec((1, tc), wcol(0)), pl.BlockSpec((1, tc), wcol(1)), pl.BlockSpec((1, tc), wcol(2)),
                  pl.BlockSpec((kps, 2 * fb, n), lambda bi, c, o, k: (k, 0, 0)),
                  pl.BlockSpec((kps, n, 2 * fb), lambda bi, c, o, k: (k, 0, 0)),
                  pl.BlockSpec((None, kps * fb, tc), spec), pl.BlockSpec((None, kps * fb, tc), spec),
                  pl.BlockSpec((None, 1, tc), per_o), pl.BlockSpec((None, 1, tc), per_o)],
        out_specs=pl.BlockSpec((None, n, tc), lambda bi, c, o, k: (bi, 0, c)),
        out_shape=jax.ShapeDtypeStruct((b, n, d), BF16),
        scratch_shapes=[pltpu.VMEM((n, tc), BF16), pltpu.VMEM((n, tc), F32), pltpu.VMEM((n, tc), F32)],
        compiler_params=_cparams(4), name="hyena_conv",
    )(u, u, u, conv_w, conv_w, conv_w, conv_b, conv_b, conv_b, fwd_tab, inv_tab, hr, gg, nyq,
      skip.reshape(2, 1, d))


def kernel(x, c, ctx, c_ctx, mod_w, mod_b, ln_g, ln_b, attn_w_in, attn_lambda, attn_subln_g, attn_sink, attn_w_out,
           hy_w_in, hy_conv_w, hy_conv_b, hy_ffn_w_in, hy_ffn_w_hid, hy_ffn_b, hy_sin_freq, hy_ffn_w_out, hy_skip,
           hy_w_out, router_w, router_bias, exp_w_gate, exp_w_up, exp_w_down):
    b, n, d = x.shape
    nc = ctx.shape[1]
    t = b * n
    assert d == D_MODEL and b + 1 <= MOD_ROWS and n % 512 == 0 and nc % 256 == 0

    c_rows = jnp.zeros((MOD_ROWS, d), F32).at[:b].set(c).at[b].set(c_ctx)
    mods = _mods(c_rows, mod_w, mod_b)
    router_wt = jnp.pad(router_w, ((0, 0), (0, LANES - N_EXPERTS)))
    x2d = x.reshape(t, d)

    sh1, sc1, g1, sh2, sc2, g2 = jnp.split(mods[0], 6, axis=-1)
    cos, sin = _rope_tables(n)
    w_in = attn_w_in[0].astype(BF16)
    q_groups = tuple(range(Q_W // LANES))
    rope_groups = q_groups + tuple(range(Q_W // LANES, (Q_W + DIFF_QK_W) // LANES)) + (
        (Q_W + DIFF_QK_W + DIFF_VW) // LANES,)
    q_scales = tuple((g, HEAD_DIM ** -0.5 * (LOG2_E if g < DIFF_QK_W // LANES else 1.0)) for g in q_groups)
    proj = _modmm(x2d, sc1[:b], sh1[:b], w_in, rows_per_mod=n, tm=512,
                  rope=(jnp.asarray(cos), jnp.asarray(sin), rope_groups, q_scales), name="attn_in_proj")
    proj_c = _modmm(ctx.reshape(b * nc, d), sc1[b:b + 1], sh1[b:b + 1], w_in[:, Q_W:], rows_per_mod=b * nc, tm=256,
                    name="ctx_in_proj")
    proj = proj.reshape(b, n, ATTN_PROJ_W)
    proj_c = proj_c.reshape(b, nc, KV_W)
    lam_init = 0.8 - 0.6 * math.exp(-0.3 * 0)
    oa = _diff_attn(proj, proj_c, attn_lambda[0], attn_subln_g[0], lam_init, tq=512, sub=128)
    ow = _win_attn(proj, proj_c, attn_sink[0], tq=256)
    w_out = attn_w_out[0].astype(BF16)
    x1, h2, lgt = _proj_ln([oa.reshape(t, DIFF_VW), ow.reshape(t, WIN_Q_W)], [w_out[:DIFF_VW], w_out[DIFF_VW:]],
                           x2d, g1[:b], ln_g[0, 0], ln_b[0, 0], sc2[:b], sh2[:b], router_wt, rows_per_mod=n, tm=512,
                           name="attn_out_proj_ln")
    x2 = _moe(h2, lgt, x1, g2[:b], ln_g[0, 1], ln_b[0, 1], router_bias, exp_w_gate, exp_w_up, exp_w_down, layer=0,
              rows_per_mod=n)

    sh1, sc1, g1, sh2, sc2, g2 = jnp.split(mods[1], 6, axis=-1)
    u = _modmm(x2, sc1[:b], sh1[:b], hy_w_in[0].astype(BF16), rows_per_mod=n, tm=512, name="hyena_in_proj")
    fb, tc = 256, 512
    fwd_np, inv_np = _dft_tables(n, fb)
    fwd_tab, inv_tab = jnp.asarray(fwd_np).astype(BF16), jnp.asarray(inv_np).astype(BF16)
    zp, decay = _filter_tables(n)
    w_in_p = jnp.zeros((FILTER_HID, FILTER_HID), F32).at[:FILTER_EMB].set(hy_ffn_w_in[0])
    hid = _filter_mlp(jnp.asarray(zp), w_in_p, hy_ffn_w_hid[0], hy_ffn_b[0], hy_sin_freq[0])
    hr, gg, nyq = _filter_spec(hid, hy_ffn_w_out[0].reshape(FILTER_HID, 2, 2, d).transpose(1, 2, 0, 3), jnp.asarray(decay), fwd_tab, fb, tc)
    z2 = _hyena_conv(u.reshape(b, n, 3 * d), hy_conv_w[0], hy_conv_b[0].reshape(1, 3 * d), fwd_tab, inv_tab,
                     hr, gg, nyq, hy_skip[0], fb, tc, kps=2)
    x3, h2, lgt = _proj_ln([z2.reshape(t, d)], [hy_w_out[0].astype(BF16)], x2, g1[:b], ln_g[1, 0], ln_b[1, 0],
                           sc2[:b], sh2[:b], router_wt, rows_per_mod=n, tm=512, name="hyena_out_proj_ln")
    x4 = _moe(h2, lgt, x3, g2[:b], ln_g[1, 1], ln_b[1, 1], router_bias, exp_w_gate, exp_w_up, exp_w_down, layer=1,
              rows_per_mod=n)
    return x4.reshape(b, n, d)
```

```python
import functools
import math

import jax
import jax.numpy as jnp
import numpy as np
from jax import lax
from jax.experimental import pallas as pl
from jax.experimental.pallas import tpu as pltpu

F32 = jnp.float32
BF16 = jnp.bfloat16
I32 = jnp.int32

D_MODEL = 1024
DEPTH = 2
GRID_W = 64
HEAD_DIM = 64
DIFF_HEADS = 4
WIN_Q_HEADS = 8
WIN_KV_HEADS = 2
WINDOW = 128
WIN_BLOCK = 128
ROPE_BASE = 10000.0
DIFF_QK_W = DIFF_HEADS * 2 * HEAD_DIM
DIFF_VW = DIFF_HEADS * 2 * HEAD_DIM
WIN_Q_W = WIN_Q_HEADS * HEAD_DIM
WIN_KV_W = WIN_KV_HEADS * HEAD_DIM
Q_W = DIFF_QK_W + WIN_Q_W
KV_W = DIFF_QK_W + DIFF_VW + 2 * WIN_KV_W
ATTN_PROJ_W = Q_W + KV_W
FILTER_EMB = 33
FILTER_HID = 64
DECAY_TARGET = 1e-2
FAST_DECAY_PCT = 0.3
SLOW_DECAY_PCT = 1.5
N_EXPERTS = 16
N_GROUPS = 4
EXPERTS_PER_GROUP = N_EXPERTS // N_GROUPS
EXPERT_FF = 1024
LN_EPS = 1e-5
DEEPNORM_ALPHA = (2 * DEPTH) ** 0.25
NEG_INF = -1e30
LOG2_E = 1.4426950408889634

LANES = 128
MOD_ROWS = 16
VMEM_LIMIT = 60 * 1024 * 1024
MOE_BM = 512
MOE_TILE = 512
ROW_ALIGN = 8
NT_DIMS = (((1,), (1,)), ((), ()))


def _cparams(n_axes):
    return pltpu.CompilerParams(dimension_semantics=("arbitrary",) * n_axes, vmem_limit_bytes=VMEM_LIMIT)


@functools.lru_cache(maxsize=None)
def _rope_tables(n):
    rows = n // GRID_W
    r, col = np.meshgrid(np.arange(rows, dtype=np.float32), np.arange(GRID_W, dtype=np.float32), indexing="ij")
    axis_dim = HEAD_DIM // 2
    inv_freq = (ROPE_BASE ** (-np.arange(0, axis_dim, 2, dtype=np.float32) / axis_dim)).astype(np.float32)
    ang = np.concatenate([r.reshape(-1, 1) * inv_freq, col.reshape(-1, 1) * inv_freq], -1)
    ang = np.concatenate([ang, ang], -1).astype(np.float32)
    cos, sin = np.cos(ang), np.sin(ang)
    half = np.arange(HEAD_DIM) < HEAD_DIM // 2
    sin_signed = np.where(half[None, :], -sin, sin)
    reps = LANES // HEAD_DIM
    return (np.tile(cos, (1, reps)).astype(np.float32), np.tile(sin_signed, (1, reps)).astype(np.float32))


@functools.lru_cache(maxsize=None)
def _dft_tables(n, fb):
    big = 2 * n
    k = np.arange(n, dtype=np.int64)[:, None]
    t = np.arange(n, dtype=np.int64)[None, :]
    ang = ((k * t) % big).astype(np.float64) * (2.0 * math.pi / big)
    c, s = np.cos(ang), np.sin(ang)
    alt = (1 - 2 * (np.arange(n) & 1)).astype(np.float64)
    s_f = s.copy()
    s_f[0, :] = alt
    kb = n // fb
    fwd = np.concatenate([c.reshape(kb, fb, n), s_f.reshape(kb, fb, n)], axis=1)
    ci = c.T * (2.0 / big)
    ci[:, 0] = 1.0 / big
    si = s.T * (2.0 / big)
    si[:, 0] = alt / big
    inv = np.concatenate([ci.reshape(n, kb, fb).transpose(1, 0, 2), si.reshape(n, kb, fb).transpose(1, 0, 2)], axis=2)
    return fwd.astype(np.float32), inv.astype(np.float32)


@functools.lru_cache(maxsize=None)
def _filter_tables(n):
    t = np.linspace(0.0, 1.0, n, dtype=np.float32)[:, None]
    bands = (FILTER_EMB - 1) // 2
    w = (2.0 * math.pi * np.arange(n, dtype=np.float32)[:, None] / n).astype(np.float32)
    fr = np.linspace(1e-4, bands - 1, bands, dtype=np.float32)[None, :]
    z = np.concatenate([t, np.cos(fr * w), -np.sin(fr * w)], -1).astype(np.float32)
    zp = np.zeros((n, FILTER_HID), np.float32)
    zp[:, :FILTER_EMB] = z
    deltas = np.abs(np.linspace(math.log(DECAY_TARGET) / SLOW_DECAY_PCT, math.log(DECAY_TARGET) / FAST_DECAY_PCT,
                                D_MODEL, dtype=np.float32))
    decay = np.exp(-t * deltas[None, :]).astype(np.float32)
    return zp, decay


def _mods_kernel(c_ref, w_ref, b_ref, o_ref):
    c = c_ref[...]
    a = (c * jax.nn.sigmoid(c)).astype(BF16)
    o_ref[...] = jnp.dot(a, w_ref[...].astype(BF16), preferred_element_type=F32) + b_ref[...]


def _mods(c_rows, mod_w, mod_b):
    d = D_MODEL
    tn = 1536
    return pl.pallas_call(
        _mods_kernel,
        grid=(DEPTH, 6 * d // tn),
        in_specs=[pl.BlockSpec((MOD_ROWS, d), lambda l, j: (0, 0)),
                  pl.BlockSpec((None, d, tn), lambda l, j: (l, 0, j)),
                  pl.BlockSpec((None, 1, tn), lambda l, j: (l, 0, j))],
        out_specs=pl.BlockSpec((None, MOD_ROWS, tn), lambda l, j: (l, 0, j)),
        out_shape=jax.ShapeDtypeStruct((DEPTH, MOD_ROWS, 6 * d), F32),
        compiler_params=_cparams(2), name="mods",
    )(c_rows, mod_w, mod_b.reshape(DEPTH, 1, 6 * d))


def _modmm_kernel(*refs, n_groups, rope_groups, scaled_groups, chunk):
    if rope_groups:
        x_ref, sc_ref, sh_ref, w_ref, cos_ref, sin_ref, o_ref = refs
        cos, sin = cos_ref[...], sin_ref[...]
        lane = lax.broadcasted_iota(I32, (1, LANES), 1)
        first_half = (lane % HEAD_DIM) < HEAD_DIM // 2
    else:
        x_ref, sc_ref, sh_ref, w_ref, o_ref = refs
    h = (x_ref[...] * (1.0 + sc_ref[...]) + sh_ref[...]).astype(BF16)
    scales = dict(scaled_groups)
    gpc = chunk // LANES
    for c in range(n_groups // gpc):
        acc = jnp.dot(h, w_ref[:, c * chunk:(c + 1) * chunk], preferred_element_type=F32)
        for j in range(gpc):
            g = c * gpc + j
            blk = acc[:, j * LANES:(j + 1) * LANES]
            if g in rope_groups:
                rot = jnp.where(first_half, pltpu.roll(blk, LANES - HEAD_DIM // 2, 1), pltpu.roll(blk, HEAD_DIM // 2, 1))
                blk = blk * cos + rot * sin
                if g in scales:
                    blk = blk * scales[g]
            o_ref[:, g * LANES:(g + 1) * LANES] = blk.astype(o_ref.dtype)


def _modmm(x2d, sc, sh, w, rows_per_mod, tm, rope=None, name="modmm"):
    t, d = x2d.shape
    n = w.shape[1]
    tiles_per_mod = rows_per_mod // tm
    nmod = sc.shape[0]
    in_specs = [pl.BlockSpec((tm, d), lambda i: (i, 0)),
                pl.BlockSpec((None, 1, d), lambda i: (i // tiles_per_mod, 0, 0)),
                pl.BlockSpec((None, 1, d), lambda i: (i // tiles_per_mod, 0, 0)),
                pl.BlockSpec((d, n), lambda i: (0, 0))]
    args = [x2d, sc.reshape(nmod, 1, d), sh.reshape(nmod, 1, d), w]
    rope_groups, scaled_groups = (), ()
    if rope is not None:
        cos, sin, rope_groups, scaled_groups = rope
        in_specs += [pl.BlockSpec((tm, LANES), lambda i: (i % tiles_per_mod, 0)),
                     pl.BlockSpec((tm, LANES), lambda i: (i % tiles_per_mod, 0))]
        args += [cos, sin]
    kern = functools.partial(_modmm_kernel, n_groups=n // LANES, rope_groups=tuple(rope_groups),
                             scaled_groups=tuple(scaled_groups), chunk=256)
    return pl.pallas_call(
        kern, grid=(t // tm,), in_specs=in_specs,
        out_specs=pl.BlockSpec((tm, n), lambda i: (i, 0)),
        out_shape=jax.ShapeDtypeStruct((t, n), BF16),
        compiler_params=_cparams(1), name=name,
    )(*args)


def _diff_attn_kernel(q_ref, k_ref, v_ref, kc_ref, vc_ref, lam_ref, g_ref, o_ref, *, lam_init, sub):
    lane = lax.broadcasted_iota(I32, (1, LANES), 1)
    k, kc = k_ref[...], kc_ref[...]
    v, vc = v_ref[...], vc_ref[...]

    def scores(r0):
        q = q_ref[r0:r0 + sub, :]
        zero = jnp.zeros_like(q)
        out = []
        for qm in (jnp.where(lane < HEAD_DIM, q, zero), jnp.where(lane >= HEAD_DIM, q, zero)):
            out.append((lax.dot_general(qm, k, NT_DIMS, preferred_element_type=F32),
                        lax.dot_general(qm, kc, NT_DIMS, preferred_element_type=F32)))
        return out

    def probs(sl, sc):
        m = jnp.maximum(jnp.max(sl, -1, keepdims=True), jnp.max(sc, -1, keepdims=True))
        pl_, pc = jnp.exp2(sl - m), jnp.exp2(sc - m)
        den = jnp.sum(pl_, -1, keepdims=True) + jnp.sum(pc, -1, keepdims=True)
        return pl_, pc, 1.0 / den

    lv = lam_ref[...]
    lam = (jnp.exp(jnp.sum(lv[0:1] * lv[1:2], keepdims=True)) - jnp.exp(jnp.sum(lv[2:3] * lv[3:4], keepdims=True))
           + lam_init)
    starts = list(range(0, q_ref.shape[0], sub))
    nxt = scores(starts[0])
    for i, r0 in enumerate(starts):
        cur = nxt
        if i + 1 < len(starts):
            nxt = scores(starts[i + 1])
        p1l, p1c, r1 = probs(*cur[0])
        p2l, p2c, r2 = probs(*cur[1])
        w2 = lam * r2
        al = (p1l * r1 - p2l * w2).astype(BF16)
        ac = (p1c * r1 - p2c * w2).astype(BF16)
        o = jnp.dot(al, v, preferred_element_type=F32) + jnp.dot(ac, vc, preferred_element_type=F32)
        ms = jnp.mean(o * o, -1, keepdims=True)
        o_ref[r0:r0 + sub, :] = (o * lax.rsqrt(ms + LN_EPS) * g_ref[...] * (1.0 - lam_init)).astype(o_ref.dtype)


def _diff_attn(proj, proj_c, lam_vec, subln_g, lam_init, tq, sub):
    b, n, _ = proj.shape
    nc = proj_c.shape[1]
    kcol = Q_W // LANES
    vcol = (Q_W + DIFF_QK_W) // LANES
    vccol = DIFF_QK_W // LANES
    kern = functools.partial(_diff_attn_kernel, lam_init=lam_init, sub=sub)
    return pl.pallas_call(
        kern, grid=(b, DIFF_HEADS, n // tq),
        in_specs=[pl.BlockSpec((None, tq, LANES), lambda bi, h, i: (bi, i, h)),
                  pl.BlockSpec((None, n, LANES), lambda bi, h, i: (bi, 0, kcol + h)),
                  pl.BlockSpec((None, n, LANES), lambda bi, h, i: (bi, 0, vcol + h)),
                  pl.BlockSpec((None, nc, LANES), lambda bi, h, i: (bi, 0, h)),
                  pl.BlockSpec((None, nc, LANES), lambda bi, h, i: (bi, 0, vccol + h)),
                  pl.BlockSpec((4, HEAD_DIM), lambda bi, h, i: (0, 0)),
                  pl.BlockSpec((1, LANES), lambda bi, h, i: (0, 0))],
        out_specs=pl.BlockSpec((None, tq, LANES), lambda bi, h, i: (bi, i, h)),
        out_shape=jax.ShapeDtypeStruct((b, n, DIFF_VW), BF16),
        compiler_params=_cparams(3), name="diff_attn",
    )(proj, proj, proj, proj_c, proj_c, lam_vec, subln_g.reshape(1, LANES))


def _win_attn_kernel(sink_ref, q_ref, k_ref, v_ref, kc_ref, vc_ref, o_ref, *, seq, tq):
    n = pl.program_id(1)
    gq = WIN_Q_HEADS // WIN_KV_HEADS
    kw = tq + 2 * WINDOW
    start = pl.multiple_of(jnp.clip(n * tq - WINDOW, 0, seq - kw), WINDOW)
    k_win, v_win = k_ref[pl.ds(start, kw), :], v_ref[pl.ds(start, kw), :]
    kc, vc = kc_ref[...], vc_ref[...]
    lane = lax.broadcasted_iota(I32, (1, LANES), 1)
    row = lax.broadcasted_iota(I32, (gq * tq, 1), 0)
    q_abs = n * tq + row % tq
    k_abs = start + lax.broadcasted_iota(I32, (1, kw), 1)
    allowed = jnp.abs(q_abs - k_abs) <= WINDOW
    head = row // tq
    q = q_ref[...].astype(F32)
    for g in range(WIN_KV_HEADS):
        in_g = (lane // HEAD_DIM) == g
        parts = []
        for j in range(gq):
            hq = g * gq + j
            x = q[:, (hq // 2) * LANES:(hq // 2 + 1) * LANES]
            if hq % 2 != g:
                x = pltpu.roll(x, HEAD_DIM, 1)
            parts.append(jnp.where(in_g, x, 0.0))
        qs = jnp.concatenate(parts, 0).astype(BF16)
        s_loc = lax.dot_general(qs, k_win, NT_DIMS, preferred_element_type=F32)
        s_ctx = lax.dot_general(qs, kc, NT_DIMS, preferred_element_type=F32)
        s_loc = jnp.where(allowed, s_loc, NEG_INF)
        sk = jnp.zeros((gq * tq, 1), F32)
        for j in range(gq):
            sk = jnp.where(head == j, sink_ref[g * gq + j], sk)
        m = jnp.maximum(jnp.maximum(jnp.max(s_loc, -1, keepdims=True), jnp.max(s_ctx, -1, keepdims=True)), sk)
        p_loc, p_ctx = jnp.exp(s_loc - m), jnp.exp(s_ctx - m)
        den = jnp.sum(p_loc, -1, keepdims=True) + jnp.sum(p_ctx, -1, keepdims=True) + jnp.exp(sk - m)
        o = (jnp.dot(p_loc.astype(BF16), v_win, preferred_element_type=F32)
             + jnp.dot(p_ctx.astype(BF16), vc, preferred_element_type=F32)) * (1.0 / den)
        for cb in range(gq // 2):
            pair = []
            for half in range(2):
                piece = o[(2 * cb + half) * tq:(2 * cb + half + 1) * tq]
                pair.append(piece if half == g else pltpu.roll(piece, HEAD_DIM, 1))
            col = (g * gq // 2 + cb) * LANES
            o_ref[:, col:col + LANES] = jnp.where(lane < HEAD_DIM, pair[0], pair[1]).astype(o_ref.dtype)


def _win_attn(proj, proj_c, sink, tq):
    b, n, _ = proj.shape
    nc = proj_c.shape[1]
    qcol = DIFF_QK_W // WIN_Q_W
    kcol = (Q_W + DIFF_QK_W + DIFF_VW) // LANES
    kccol = (DIFF_QK_W + DIFF_VW) // LANES
    kern = functools.partial(_win_attn_kernel, seq=n, tq=tq)
    return pl.pallas_call(
        kern, grid=(b, n // tq),
        in_specs=[pl.BlockSpec(memory_space=pltpu.SMEM),
                  pl.BlockSpec((None, tq, WIN_Q_W), lambda bi, i: (bi, i, qcol)),
                  pl.BlockSpec((None, n, LANES), lambda bi, i: (bi, 0, kcol)),
                  pl.BlockSpec((None, n, LANES), lambda bi, i: (bi, 0, kcol + 1)),
                  pl.BlockSpec((None, nc, LANES), lambda bi, i: (bi, 0, kccol)),
                  pl.BlockSpec((None, nc, LANES), lambda bi, i: (bi, 0, kccol + 1))],
        out_specs=pl.BlockSpec((None, tq, WIN_Q_W), lambda bi, i: (bi, i, 0)),
        out_shape=jax.ShapeDtypeStruct((b, n, WIN_Q_W), BF16),
        compiler_params=_cparams(2), name="win_attn",
    )(sink, proj, proj, proj, proj_c, proj_c)


def _layer_norm(r, g, b):
    mu = jnp.mean(r, -1, keepdims=True)
    dlt = r - mu
    var = jnp.mean(dlt * dlt, -1, keepdims=True)
    return dlt * lax.rsqrt(var + LN_EPS) * g + b


def _proj_ln_kernel(*refs, n_in, sub):
    a_refs = refs[:n_in]
    w_refs = refs[n_in:2 * n_in]
    x_ref, gate_ref, lng_ref, lnb_ref, sc_ref, sh_ref, rw_ref, x1_ref, h2_ref, lg_ref = refs[2 * n_in:]

    def split(v):
        hi = v.astype(BF16)
        return hi, (v - hi.astype(F32)).astype(BF16)

    def project(r0):
        y = jnp.dot(a_refs[0][r0:r0 + sub, :], w_refs[0][...], preferred_element_type=F32)
        for a_ref, w_ref in zip(a_refs[1:], w_refs[1:]):
            y = y + jnp.dot(a_ref[r0:r0 + sub, :], w_ref[...], preferred_element_type=F32)
        return y

    rw_hi, rw_lo = split(rw_ref[...])
    rw_both = jnp.concatenate([rw_hi, rw_lo], 1)
    starts = list(range(0, x_ref.shape[0], sub))
    nxt = project(starts[0])
    for i, r0 in enumerate(starts):
        y = nxt
        if i + 1 < len(starts):
            nxt = project(starts[i + 1])
        xn = _layer_norm(DEEPNORM_ALPHA * x_ref[r0:r0 + sub, :] + gate_ref[...] * y, lng_ref[...], lnb_ref[...])
        x1_ref[r0:r0 + sub, :] = xn
        h2 = xn * (1.0 + sc_ref[...]) + sh_ref[...]
        h2_ref[r0:r0 + sub, :] = h2.astype(h2_ref.dtype)
        h_hi, h_lo = split(h2)
        hh = jnp.dot(h_hi, rw_both, preferred_element_type=F32)
        lg = hh[:, :LANES] + hh[:, LANES:] + jnp.dot(h_lo, rw_hi, preferred_element_type=F32)
        lg_ref[:, r0:r0 + sub] = lg.T[:N_EXPERTS, :]


def _proj_ln(acts, ws, x2d, gate, ln_g, ln_b, sc, sh, router_wt, rows_per_mod, tm, name):
    t, d = x2d.shape
    n_in = len(acts)
    tiles_per_mod = rows_per_mod // tm
    nmod = gate.shape[0]
    row = lambda i: (i, 0)
    full = lambda i: (0, 0)
    mod = lambda i: (i // tiles_per_mod, 0, 0)
    in_specs = ([pl.BlockSpec((tm, a.shape[1]), row) for a in acts]
                + [pl.BlockSpec(w.shape, full) for w in ws]
                + [pl.BlockSpec((tm, d), row), pl.BlockSpec((None, 1, d), mod),
                   pl.BlockSpec((1, d), full), pl.BlockSpec((1, d), full),
                   pl.BlockSpec((None, 1, d), mod), pl.BlockSpec((None, 1, d), mod),
                   pl.BlockSpec((d, LANES), full)])
    return pl.pallas_call(
        functools.partial(_proj_ln_kernel, n_in=n_in, sub=LANES), grid=(t // tm,), in_specs=in_specs,
        out_specs=[pl.BlockSpec((tm, d), row), pl.BlockSpec((tm, d), row),
                   pl.BlockSpec((N_EXPERTS, tm), lambda i: (0, i))],
        out_shape=[jax.ShapeDtypeStruct((t, d), F32), jax.ShapeDtypeStruct((t, d), BF16),
                   jax.ShapeDtypeStruct((N_EXPERTS, t), F32)],
        compiler_params=_cparams(1), name=name,
    )(*acts, *ws, x2d, gate.reshape(nmod, 1, d), ln_g.reshape(1, d), ln_b.reshape(1, d),
      sc.reshape(nmod, 1, d), sh.reshape(nmod, 1, d), router_wt)


def _first_argmax(vals):
    idx = jnp.zeros(vals[0].shape, I32)
    best = vals[0]
    for j in range(1, len(vals)):
        upd = vals[j] > best
        idx = jnp.where(upd, j, idx)
        best = jnp.where(upd, vals[j], best)
    return idx, best


def _route_kernel(lg_ref, bias_ref, pos_ref, w_ref, tab_ref, cnt_ref, carry_ref, off_ref, *, tr):
    @pl.when(pl.program_id(0) == 0)
    def _():
        carry_ref[...] = jnp.zeros_like(carry_ref)

    lg = lg_ref[...]
    ex = jnp.exp(lg - jnp.max(lg, 0, keepdims=True))
    scores = ex / jnp.sum(ex, 0, keepdims=True)
    sel = scores + bias_ref[...]
    rows = [sel[e:e + 1] for e in range(N_EXPERTS)]
    group_scores = []
    for g in range(N_GROUPS):
        r = rows[g * EXPERTS_PER_GROUP:(g + 1) * EXPERTS_PER_GROUP]
        best = None
        for i in range(EXPERTS_PER_GROUP):
            for j in range(i + 1, EXPERTS_PER_GROUP):
                s = r[i] + r[j]
                best = s if best is None else jnp.maximum(best, s)
        group_scores.append(best)
    grp, _ = _first_argmax(group_scores)
    vals = []
    for j in range(EXPERTS_PER_GROUP):
        v = rows[(N_GROUPS - 1) * EXPERTS_PER_GROUP + j]
        for g in range(N_GROUPS - 2, -1, -1):
            v = jnp.where(grp == g, rows[g * EXPERTS_PER_GROUP + j], v)
        vals.append(v)
    i0, _ = _first_argmax(vals)
    i1, _ = _first_argmax([jnp.where(i0 == j, -jnp.inf, vals[j]) for j in range(EXPERTS_PER_GROUP)])
    e0 = grp * EXPERTS_PER_GROUP + i0
    e1 = grp * EXPERTS_PER_GROUP + i1
    eid = lax.broadcasted_iota(I32, (N_EXPERTS, 1), 0)
    oh0, oh1 = eid == e0, eid == e1
    s0 = jnp.sum(jnp.where(oh0, scores, 0.0), 0, keepdims=True)
    s1 = jnp.sum(jnp.where(oh1, scores, 0.0), 0, keepdims=True)
    den = s0 + s1
    member = jnp.where(oh0 | oh1, 1.0, 0.0)
    before = lax.broadcasted_iota(I32, (tr, tr), 0) < lax.broadcasted_iota(I32, (tr, tr), 1)
    upper = jnp.where(before, 1.0, 0.0).astype(BF16)
    cnt = jnp.dot(member.astype(BF16), upper, preferred_element_type=F32)
    run = jnp.sum(member, 1, keepdims=True)
    run = jnp.floor((run + (ROW_ALIGN - 1)) * (1.0 / ROW_ALIGN)) * ROW_ALIGN
    run = jnp.broadcast_to(run, (N_EXPERTS, LANES))
    acc = jnp.zeros((1, LANES), F32)
    for e in range(N_EXPERTS):
        off_ref[e:e + 1, :] = acc
        acc = acc + run[e:e + 1]
    off = off_ref[...]
    at = off[:, 0:1] + cnt
    pos_ref[0:1, :] = jnp.sum(jnp.where(oh0, at, 0.0), 0, keepdims=True).astype(I32)
    pos_ref[1:2, :] = jnp.sum(jnp.where(oh1, at, 0.0), 0, keepdims=True).astype(I32)
    w_ref[0:1, :] = s0 / den
    w_ref[1:2, :] = s1 / den
    tab_ref[0] = off.astype(I32)
    tab_ref[1] = run.astype(I32)
    tab_ref[2] = carry_ref[...].astype(I32)
    carry_ref[...] = carry_ref[...] + run
    cnt_ref[...] = carry_ref[...]


def _route(logits_t, router_bias, tr):
    t = logits_t.shape[1]
    tok = lambda i: (0, i)
    return pl.pallas_call(
        functools.partial(_route_kernel, tr=tr), grid=(t // tr,),
        in_specs=[pl.BlockSpec((N_EXPERTS, tr), tok), pl.BlockSpec((N_EXPERTS, 1), lambda i: (0, 0))],
        out_specs=[pl.BlockSpec((2, tr), tok), pl.BlockSpec((2, tr), tok),
                   pl.BlockSpec((None, 3, N_EXPERTS, LANES), lambda i: (i, 0, 0, 0)),
                   pl.BlockSpec((N_EXPERTS, LANES), lambda i: (0, 0))],
        out_shape=[jax.ShapeDtypeStruct((2, t), I32), jax.ShapeDtypeStruct((2, t), F32),
                   jax.ShapeDtypeStruct((t // tr, 3, N_EXPERTS, LANES), I32),
                   jax.ShapeDtypeStruct((N_EXPERTS, LANES), F32)],
        scratch_shapes=[pltpu.VMEM((N_EXPERTS, LANES), F32), pltpu.VMEM((N_EXPERTS, LANES), F32)],
        compiler_params=_cparams(1), name="route",
    )(logits_t, router_bias.reshape(N_EXPERTS, 1))


def _run_copies(n, max_rows, make_copy, wait):
    sz = max_rows
    while sz >= ROW_ALIGN:
        start = (n // (2 * sz)) * (2 * sz)

        @pl.when((n & sz) != 0)
        def _(start=start, sz=sz):
            cp = make_copy(start, sz)
            cp.wait() if wait else cp.start()

        sz //= 2


def _wait_rows(total, max_rows, make_copy):
    sz = max_rows
    while sz >= ROW_ALIGN:
        @pl.when((total & sz) != 0)
        def _(sz=sz):
            make_copy(sz).wait()

        sz //= 2


def _rows(ref, start, size):
    return ref.at[pl.ds(pl.multiple_of(start, ROW_ALIGN), size)]


def _dispatch_kernel(tab_ref, h_ref, pos_ref, w_ref, xs_ref, buf, zbuf, sem, *, tt, n_tiles, bm):
    i = pl.program_id(0)
    d = h_ref.shape[1]
    sb = buf.shape[1]
    ne = N_EXPERTS
    fill = 3 * n_tiles * ne

    @pl.when(i == 0)
    def _():
        zbuf[...] = jnp.zeros_like(zbuf)
        for wait in (False, True):
            for e in range(ne):
                dst, n = tab_ref[fill + e], tab_ref[fill + ne + e]
                _run_copies(n, bm // 2, lambda s, z: pltpu.make_async_copy(
                    zbuf.at[pl.ds(0, z)], _rows(xs_ref, dst + s, z), sem.at[2]), wait)

        def zero_block(j, carry):
            for half in range(2):
                cp = pltpu.make_async_copy(zbuf, _rows(xs_ref, j * bm + half * (bm // 2), bm // 2), sem.at[2])
                cp.start()
                cp.wait()
            return carry

        lax.fori_loop(tab_ref[fill + 2 * ne], xs_ref.shape[0] // bm, zero_block, 0)

    def tile_copies(tile, slot, wait):
        for e in range(ne):
            off = tab_ref[tile * ne + e]
            n = tab_ref[(n_tiles + tile) * ne + e]
            dst = tab_ref[(2 * n_tiles + tile) * ne + e]
            _run_copies(n, tt, lambda s, z: pltpu.make_async_copy(
                _rows(buf.at[slot], off + s, z), _rows(xs_ref, dst + s, z), sem.at[slot]), wait)

    slot = i % 2
    pos, w = pos_ref[...], w_ref[...]
    srow = lax.broadcasted_iota(I32, (sb, 1), 0)
    m0, m1 = srow == pos[0:1], srow == pos[1:2]
    perm = (jnp.where(m0, 1.0, 0.0) + jnp.where(m1, 1.0, 0.0)).astype(BF16)
    buf[slot, :, :d] = jnp.dot(perm, h_ref[...].astype(BF16), preferred_element_type=F32)
    wrow = jnp.sum(jnp.where(m0, w[0:1], 0.0) + jnp.where(m1, w[1:2], 0.0), 1, keepdims=True)
    buf[slot, :, d:] = jnp.broadcast_to(wrow, (sb, LANES))

    def wait_tile(tile, slot):
        total = tab_ref[fill + 2 * ne + 1 + tile]
        _wait_rows(total, tt * 2, lambda z: pltpu.make_async_copy(
            buf.at[slot, pl.ds(0, z)], xs_ref.at[pl.ds(0, z)], sem.at[slot]))

    @pl.when(i > 0)
    def _():
        wait_tile(i - 1, 1 - slot)

    tile_copies(i, slot, False)

    @pl.when(i == n_tiles - 1)
    def _():
        wait_tile(i, slot)


def _dispatch(tab, h2, pos, wts, n_slots, tt, sb, bm):
    t, d = h2.shape
    n_tiles = t // tt
    return pl.pallas_call(
        functools.partial(_dispatch_kernel, tt=tt, n_tiles=n_tiles, bm=bm),
        grid_spec=pltpu.PrefetchScalarGridSpec(
            num_scalar_prefetch=1, grid=(n_tiles,),
            in_specs=[pl.BlockSpec((tt, d), lambda i, s: (i, 0)), pl.BlockSpec((2, tt), lambda i, s: (0, i)),
                      pl.BlockSpec((2, tt), lambda i, s: (0, i))],
            out_specs=pl.BlockSpec(memory_space=pl.ANY),
            scratch_shapes=[pltpu.VMEM((2, sb, d + LANES), F32), pltpu.VMEM((bm // 2, d + LANES), F32),
                            pltpu.SemaphoreType.DMA((3,))]),
        out_shape=jax.ShapeDtypeStruct((n_slots, d + LANES), F32),
        compiler_params=_cparams(1), name="moe_dispatch",
    )(tab, h2, pos, wts)


def _ffn_kernel(blk_e_ref, blk_rows_ref, next_e_ref, nused_ref, x_ref, wg_hbm, wu_hbm, wd_hbm, o_ref,
                stage, wg_bf, wu_bf, wd_bf, sem, *, layer, parts):
    del nused_ref
    i = pl.program_id(0)
    e = blk_e_ref[i]
    new_expert = (i == 0) | (e != blk_e_ref[jnp.maximum(i - 1, 0)])

    def fetch(expert, wait):
        for j, src in enumerate((wg_hbm, wu_hbm, wd_hbm)):
            cp = pltpu.make_async_copy(src.at[layer, expert], stage.at[j], sem.at[j])
            cp.wait() if wait else cp.start()

    @pl.when(i == 0)
    def _():
        fetch(e, False)

    @pl.when(new_expert)
    def _():
        fetch(e, True)
        wg_bf[...] = stage[0].astype(BF16)
        wu_bf[...] = stage[1].astype(BF16)
        wd_bf[...] = stage[2].astype(BF16)

        @pl.when(next_e_ref[i] >= 0)
        def _():
            fetch(next_e_ref[i], False)

    d = wg_bf.shape[0]
    rows = blk_rows_ref[i]
    part = x_ref.shape[0] // parts
    for p in range(parts):
        rs = slice(p * part, (p + 1) * part)

        @pl.when(rows > p * part)
        def _(rs=rs):
            x = x_ref[rs, :d].astype(BF16)
            gate = jnp.dot(x, wg_bf[...], preferred_element_type=F32)
            up = jnp.dot(x, wu_bf[...], preferred_element_type=F32)
            act = (gate * jax.nn.sigmoid(gate) * up).astype(BF16)
            o_ref[rs, :] = jnp.dot(act, wd_bf[...], preferred_element_type=F32) * x_ref[rs, d:d + 1]

        @pl.when(rows <= p * part)
        def _(rs=rs):
            o_ref[rs, :] = jnp.zeros((part, d), F32)


def _ffn(blk_e, blk_rows, next_e, n_used, xs, wg, wu, wd, layer, bm, parts):
    n_slots, xw = xs.shape
    d, ff = wg.shape[2:]
    assert d == ff
    xrow = lambda i, be, br, ne, nu: (jnp.minimum(i, nu[0] - 1), 0)
    hbm = pl.BlockSpec(memory_space=pl.ANY)
    return pl.pallas_call(
        functools.partial(_ffn_kernel, layer=layer, parts=parts),
        grid_spec=pltpu.PrefetchScalarGridSpec(
            num_scalar_prefetch=4, grid=(n_slots // bm,),
            in_specs=[pl.BlockSpec((bm, xw), xrow), hbm, hbm, hbm],
            out_specs=pl.BlockSpec((bm, d), lambda i, be, br, ne, nu: (i, 0)),
            scratch_shapes=[pltpu.VMEM((3, d, ff), F32), pltpu.VMEM((d, ff), BF16), pltpu.VMEM((d, ff), BF16),
                            pltpu.VMEM((ff, d), BF16), pltpu.SemaphoreType.DMA((3,))]),
        out_shape=jax.ShapeDtypeStruct((n_slots, d), F32),
        compiler_params=_cparams(1), name="moe_ffn",
    )(blk_e, blk_rows, next_e, n_used, xs, wg, wu, wd)


def _combine_kernel(tab_ref, ys_ref, pos_ref, x1_ref, gate_ref, lng_ref, lnb_ref, o_ref, ybuf, sem, *, tt, n_tiles):
    i = pl.program_id(0)
    sb = ybuf.shape[1]
    ne = N_EXPERTS
    slot = i % 2

    def tile_copies(tile, slot, wait):
        for e in range(ne):
            off = tab_ref[tile * ne + e]
            n = tab_ref[(n_tiles + tile) * ne + e]
            src = tab_ref[(2 * n_tiles + tile) * ne + e]
            _run_copies(n, tt, lambda s, z: pltpu.make_async_copy(
                _rows(ys_ref, src + s, z), _rows(ybuf.at[slot], off + s, z), sem.at[slot]), wait)

    @pl.when(i == 0)
    def _():
        ybuf[...] = jnp.zeros_like(ybuf)
        tile_copies(0, 0, False)

    @pl.when(i + 1 < n_tiles)
    def _():
        tile_copies(i + 1, 1 - slot, False)

    total = tab_ref[3 * n_tiles * ne + 2 * ne + 1 + i]
    _wait_rows(total, tt * 2, lambda z: pltpu.make_async_copy(
        ys_ref.at[pl.ds(0, z)], ybuf.at[slot, pl.ds(0, z)], sem.at[slot]))
    pos = pos_ref[...]
    scol = lax.broadcasted_iota(I32, (1, sb), 1)
    unperm = (jnp.where(scol == pos[:, 0:1], 1.0, 0.0) + jnp.where(scol == pos[:, 1:2], 1.0, 0.0)).astype(BF16)
    f = jnp.dot(unperm, ybuf[slot].astype(BF16), preferred_element_type=F32)
    o_ref[...] = _layer_norm(DEEPNORM_ALPHA * x1_ref[...] + gate_ref[...] * f, lng_ref[...], lnb_ref[...])


def _combine(tab, ys, pos_t, x1, gate, ln_g, ln_b, rows_per_mod, tt, sb):
    t, d = x1.shape
    tiles_per_mod = rows_per_mod // tt
    nmod = gate.shape[0]
    return pl.pallas_call(
        functools.partial(_combine_kernel, tt=tt, n_tiles=t // tt),
        grid_spec=pltpu.PrefetchScalarGridSpec(
            num_scalar_prefetch=1, grid=(t // tt,),
            in_specs=[pl.BlockSpec(memory_space=pl.ANY),
                      pl.BlockSpec((tt, 2), lambda i, s: (i, 0)),
                      pl.BlockSpec((tt, d), lambda i, s: (i, 0)),
                      pl.BlockSpec((None, 1, d), lambda i, s: (i // tiles_per_mod, 0, 0)),
                      pl.BlockSpec((1, d), lambda i, s: (0, 0)), pl.BlockSpec((1, d), lambda i, s: (0, 0))],
            out_specs=pl.BlockSpec((tt, d), lambda i, s: (i, 0)),
            scratch_shapes=[pltpu.VMEM((2, sb, d), F32), pltpu.SemaphoreType.DMA((2,))]),
        out_shape=jax.ShapeDtypeStruct((t, d), F32),
        compiler_params=_cparams(1), name="moe_combine",
    )(tab, ys, pos_t, x1, gate.reshape(nmod, 1, d), ln_g.reshape(1, d), ln_b.reshape(1, d))


def _moe(h2, logits_t, x1, gate, ln_g, ln_b, router_bias, wg, wu, wd, layer, rows_per_mod):
    t, d = h2.shape
    bm, tt, ne = MOE_BM, MOE_TILE, N_EXPERTS
    n_tiles = t // tt
    pad = ne * (ROW_ALIGN - 1)
    sb = -(-(2 * tt + pad) // LANES) * LANES
    pos, wts, tab3, cnt = _route(logits_t, router_bias, tr=tt)
    rows = cnt[:, 0].astype(I32)
    prows = (rows + bm - 1) // bm * bm
    pends = jnp.cumsum(prows)
    pstarts = pends - prows
    n_blk = -(-(2 * t + n_tiles * pad + ne * (bm - 1)) // bm)
    n_used = pends[-1] // bm
    blk_ids = jnp.minimum(jnp.arange(n_blk, dtype=I32), n_used - 1)
    blk_e = jnp.minimum(jnp.sum((blk_ids[:, None] * bm >= pends[None, :]).astype(I32), 1), ne - 1)
    eids = jnp.arange(ne, dtype=I32)
    of_blk = lambda per_expert: jnp.sum(jnp.where(blk_e[:, None] == eids[None, :], per_expert[None, :], 0), 1)
    raw_ids = jnp.arange(n_blk, dtype=I32)
    blk_rows = jnp.where(raw_ids < n_used, jnp.clip(of_blk(pstarts + rows) - raw_ids * bm, 0, bm), 0)
    later_used = (eids[None, :] > eids[:, None]) & (prows[None, :] > 0)
    next_used = jnp.min(jnp.where(later_used, eids[None, :], ne), 1)
    next_e = of_blk(jnp.where(next_used < ne, next_used, -1))
    tab3 = tab3[:, :, :, 0]
    tab = jnp.concatenate([tab3[:, 0].reshape(-1), tab3[:, 1].reshape(-1),
                           (tab3[:, 2] + pstarts[None, :]).reshape(-1), pstarts + rows, prows - rows,
                           n_used.reshape(1), jnp.sum(tab3[:, 1], 1)]).astype(I32)
    xs = _dispatch(tab, h2, pos, wts, n_blk * bm, tt, sb, bm)
    ys = _ffn(blk_e, blk_rows.astype(I32), next_e.astype(I32), n_used.reshape(1).astype(I32), xs, wg, wu, wd,
              layer, bm, parts=2)
    return _combine(tab, ys, pos.T, x1, gate, ln_g, ln_b, rows_per_mod, tt, sb)


def _filter_mlp_kernel(z_ref, w1_ref, w2_ref, w3_ref, b_ref, sf_ref, o_ref):
    hp = lax.Precision.HIGHEST
    b, sf = b_ref[...], sf_ref[...]
    h = jnp.sin(sf[0:1] * (jnp.dot(z_ref[...], w1_ref[...], precision=hp, preferred_element_type=F32) + b[0:1]))
    h = jnp.sin(sf[1:2] * (jnp.dot(h, w2_ref[...], precision=hp, preferred_element_type=F32) + b[1:2]))
    o_ref[...] = jnp.sin(sf[2:3] * (jnp.dot(h, w3_ref[...], precision=hp, preferred_element_type=F32) + b[2:3]))


def _filter_mlp(zp, w_in_p, w_hid, b, sf):
    n = zp.shape[0]
    return pl.pallas_call(
        _filter_mlp_kernel, out_shape=jax.ShapeDtypeStruct((n, FILTER_HID), F32),
        compiler_params=pltpu.CompilerParams(vmem_limit_bytes=VMEM_LIMIT), name="hyena_filter_mlp",
    )(zp, w_in_p, w_hid[0], w_hid[1], b, sf)


def _filter_spec_kernel(hid_ref, wf_ref, wb_ref, dec_ref, fwd_ref, hr_ref, g_ref, nyq_ref, p_scr, q_scr, *, fb):
    k = pl.program_id(2)
    n = hid_ref.shape[0]

    @pl.when(k == 0)
    def _():
        hp = lax.Precision.HIGHEST
        hid, dec = hid_ref[...], dec_ref[...]
        row = lax.broadcasted_iota(I32, (n, 1), 0)
        fw = jnp.dot(hid, wf_ref[...], precision=hp, preferred_element_type=F32) * dec
        bw = jnp.dot(hid, wb_ref[...], precision=hp, preferred_element_type=F32) * dec
        bw = jnp.where(row == 0, 0.0, bw)
        p = fw + bw
        p_scr[...] = p.astype(BF16)
        q_scr[...] = (fw - bw).astype(BF16)
        sign = (1 - 2 * (row & 1)).astype(F32)
        nyq_ref[...] = jnp.sum(p * sign, 0, keepdims=True)

    hr_ref[...] = jnp.dot(fwd_ref[0:fb, :], p_scr[...], preferred_element_type=F32)
    gg = jnp.dot(fwd_ref[fb:2 * fb, :], q_scr[...], preferred_element_type=F32)
    rowb = lax.broadcasted_iota(I32, (fb, 1), 0)
    g_ref[...] = jnp.where((rowb == 0) & (k == 0), 0.0, gg)


def _filter_spec(hid, w_out4, decay, fwd_tab, fb, tc):
    n = hid.shape[0]
    d = D_MODEL
    kb = n // fb
    return pl.pallas_call(
        functools.partial(_filter_spec_kernel, fb=fb), grid=(2, d // tc, kb),
        in_specs=[pl.BlockSpec((n, FILTER_HID), lambda o, c, k: (0, 0)),
                  pl.BlockSpec((None, None, FILTER_HID, tc), lambda o, c, k: (0, o, 0, c)),
                  pl.BlockSpec((None, None, FILTER_HID, tc), lambda o, c, k: (1, o, 0, c)),
                  pl.BlockSpec((n, tc), lambda o, c, k: (0, c)),
                  pl.BlockSpec((None, 2 * fb, n), lambda o, c, k: (k, 0, 0))],
        out_specs=[pl.BlockSpec((None, fb, tc), lambda o, c, k: (o, k, c)),
                   pl.BlockSpec((None, fb, tc), lambda o, c, k: (o, k, c)),
                   pl.BlockSpec((None, 1, tc), lambda o, c, k: (o, 0, c))],
        out_shape=[jax.ShapeDtypeStruct((2, n, d), F32), jax.ShapeDtypeStruct((2, n, d), F32),
                   jax.ShapeDtypeStruct((2, 1, d), F32)],
        scratch_shapes=[pltpu.VMEM((n, tc), BF16), pltpu.VMEM((n, tc), BF16)],
        compiler_params=_cparams(3), name="hyena_filter_spec",
    )(hid, w_out4, w_out4, decay, fwd_tab)


def _short_conv(u_ref, cw_ref, cb_ref):
    n = u_ref.shape[0]
    u = u_ref[...].astype(F32)
    row = lax.broadcasted_iota(I32, (n, 1), 0)
    prev = jnp.where(row == 0, 0.0, pltpu.roll(u, 1, 0))
    nxt = jnp.where(row == n - 1, 0.0, pltpu.roll(u, n - 1, 0))
    cw = cw_ref[...]
    return prev * cw[0:1] + u * cw[1:2] + nxt * cw[2:3] + cb_ref[...]


def _hyena_conv_kernel(uv_ref, ux1_ref, ux2_ref, cwv_ref, cwx1_ref, cwx2_ref, cbv_ref, cbx1_ref, cbx2_ref,
                       fwd_ref, inv_ref, hr_ref, g_ref, nyq_ref, sk_ref, o_ref, zb_scr, z32_scr, acc_scr, *, fb, kb,
                       kps):
    o = pl.program_id(2)
    k = pl.program_id(3)

    @pl.when((o == 0) & (k == 0))
    def _():
        v = _short_conv(uv_ref, cwv_ref, cbv_ref)
        z32_scr[...] = v
        zb_scr[...] = v.astype(BF16)

    @pl.when(k == 0)
    def _():
        acc_scr[...] = jnp.zeros_like(acc_scr)

    zb = zb_scr[...]
    zfs = [jnp.dot(fwd_ref[j], zb, preferred_element_type=F32) for j in range(kps)]
    rowb = lax.broadcasted_iota(I32, (fb, 1), 0)
    for j in range(kps):
        zr, zi = zfs[j][:fb], zfs[j][fb:]
        hr, gg = hr_ref[j * fb:(j + 1) * fb, :], g_ref[j * fb:(j + 1) * fb, :]
        hb = jnp.where((rowb == 0) & (k == 0), nyq_ref[...], hr) if j == 0 else hr
        y = jnp.concatenate([zr * hr - zi * gg, zr * gg + zi * hb], 0).astype(BF16)
        acc_scr[...] += jnp.dot(inv_ref[j], y, preferred_element_type=F32)

    @pl.when((k == kb - 1) & (o == 0))
    def _():
        zn = _short_conv(ux1_ref, cwx1_ref, cbx1_ref) * (acc_scr[...] + z32_scr[...] * sk_ref[...])
        z32_scr[...] = zn
        zb_scr[...] = zn.astype(BF16)

    @pl.when((k == kb - 1) & (o == 1))
    def _():
        zn = _short_conv(ux2_ref, cwx2_ref, cbx2_ref) * (acc_scr[...] + z32_scr[...] * sk_ref[...])
        o_ref[...] = zn.astype(o_ref.dtype)


def _hyena_conv(u, conv_w, conv_b, fwd_tab, inv_tab, hr, gg, nyq, skip, fb, tc, kps):
    b, n, _ = u.shape
    d = D_MODEL
    kb = n // (fb * kps)
    ncb = d // tc
    ucol = lambda part: (lambda bi, c, o, k: (bi, 0, part * ncb + c))
    wcol = lambda part: (lambda bi, c, o, k: (0, part * ncb + c))
    spec = lambda bi, c, o, k: (o, k, c)
    per_o = lambda bi, c, o, k: (o, 0, c)
    return pl.pallas_call(
        functools.partial(_hyena_conv_kernel, fb=fb, kb=kb, kps=kps), grid=(b, ncb, 2, kb),
        in_specs=[pl.BlockSpec((None, n, tc), ucol(0)), pl.BlockSpec((None, n, tc), ucol(1)),
                  pl.BlockSpec((None, n, tc), ucol(2)),
                  pl.BlockSpec((3, tc), wcol(0)), pl.BlockSpec((3, tc), wcol(1)), pl.BlockSpec((3, tc), wcol(2)),
                  pl.BlockSpec((1, tc), wcol(0)), pl.BlockSpec((1, tc), wcol(1)), pl.BlockSpec((1, tc), wcol(2)),
                  pl.BlockSpec((kps, 2 * fb, n), lambda bi, c, o, k: (k, 0, 0)),
                  pl.BlockSpec((kps, n, 2 * fb), lambda bi, c, o, k: (k, 0, 0)),
                  pl.BlockSpec((None, kps * fb, tc), spec), pl.BlockSpec((None, kps * fb, tc), spec),
                  pl.BlockSpec((None, 1, tc), per_o), pl.BlockSpec((None, 1, tc), per_o)],
        out_specs=pl.BlockSpec((None, n, tc), lambda bi, c, o, k: (bi, 0, c)),
        out_shape=jax.ShapeDtypeStruct((b, n, d), BF16),
        scratch_shapes=[pltpu.VMEM((n, tc), BF16), pltpu.VMEM((n, tc), F32), pltpu.VMEM((n, tc), F32)],
        compiler_params=_cparams(4), name="hyena_conv",
    )(u, u, u, conv_w, conv_w, conv_w, conv_b, conv_b, conv_b, fwd_tab, inv_tab, hr, gg, nyq,
      skip.reshape(2, 1, d))


def kernel(x, c, ctx, c_ctx, mod_w, mod_b, ln_g, ln_b, attn_w_in, attn_lambda, attn_subln_g, attn_sink, attn_w_out,
           hy_w_in, hy_conv_w, hy_conv_b, hy_ffn_w_in, hy_ffn_w_hid, hy_ffn_b, hy_sin_freq, hy_ffn_w_out, hy_skip,
           hy_w_out, router_w, router_bias, exp_w_gate, exp_w_up, exp_w_down):
    b, n, d = x.shape
    nc = ctx.shape[1]
    t = b * n
    assert d == D_MODEL and b + 1 <= MOD_ROWS and n % 512 == 0 and nc % 256 == 0

    c_rows = jnp.zeros((MOD_ROWS, d), F32).at[:b].set(c).at[b].set(c_ctx)
    mods = _mods(c_rows, mod_w, mod_b)
    router_wt = jnp.pad(router_w, ((0, 0), (0, LANES - N_EXPERTS)))
    x2d = x.reshape(t, d)

    sh1, sc1, g1, sh2, sc2, g2 = jnp.split(mods[0], 6, axis=-1)
    cos, sin = _rope_tables(n)
    w_in = attn_w_in[0].astype(BF16)
    q_groups = tuple(range(Q_W // LANES))
    rope_groups = q_groups + tuple(range(Q_W // LANES, (Q_W + DIFF_QK_W) // LANES)) + (
        (Q_W + DIFF_QK_W + DIFF_VW) // LANES,)
    q_scales = tuple((g, HEAD_DIM ** -0.5 * (LOG2_E if g < DIFF_QK_W // LANES else 1.0)) for g in q_groups)
    proj = _modmm(x2d, sc1[:b], sh1[:b], w_in, rows_per_mod=n, tm=512,
                  rope=(jnp.asarray(cos), jnp.asarray(sin), rope_groups, q_scales), name="attn_in_proj")
    proj_c = _modmm(ctx.reshape(b * nc, d), sc1[b:b + 1], sh1[b:b + 1], w_in[:, Q_W:], rows_per_mod=b * nc, tm=256,
                    name="ctx_in_proj")
    proj = proj.reshape(b, n, ATTN_PROJ_W)
    proj_c = proj_c.reshape(b, nc, KV_W)
    lam_init = 0.8 - 0.6 * math.exp(-0.3 * 0)
    oa = _diff_attn(proj, proj_c, attn_lambda[0], attn_subln_g[0], lam_init, tq=512, sub=128)
    ow = _win_attn(proj, proj_c, attn_sink[0], tq=256)
    w_out = attn_w_out[0].astype(BF16)
    x1, h2, lgt = _proj_ln([oa.reshape(t, DIFF_VW), ow.reshape(t, WIN_Q_W)], [w_out[:DIFF_VW], w_out[DIFF_VW:]],
                           x2d, g1[:b], ln_g[0, 0], ln_b[0, 0], sc2[:b], sh2[:b], router_wt, rows_per_mod=n, tm=512,
                           name="attn_out_proj_ln")
    x2 = _moe(h2, lgt, x1, g2[:b], ln_g[0, 1], ln_b[0, 1], router_bias, exp_w_gate, exp_w_up, exp_w_down, layer=0,
              rows_per_mod=n)

    sh1, sc1, g1, sh2, sc2, g2 = jnp.split(mods[1], 6, axis=-1)
    u = _modmm(x2, sc1[:b], sh1[:b], hy_w_in[0].astype(BF16), rows_per_mod=n, tm=512, name="hyena_in_proj")
    fb, tc = 256, 512
    fwd_np, inv_np = _dft_tables(n, fb)
    fwd_tab, inv_tab = jnp.asarray(fwd_np).astype(BF16), jnp.asarray(inv_np).astype(BF16)
    zp, decay = _filter_tables(n)
    w_in_p = jnp.zeros((FILTER_HID, FILTER_HID), F32).at[:FILTER_EMB].set(hy_ffn_w_in[0])
    hid = _filter_mlp(jnp.asarray(zp), w_in_p, hy_ffn_w_hid[0], hy_ffn_b[0], hy_sin_freq[0])
    hr, gg, nyq = _filter_spec(hid, hy_ffn_w_out[0].reshape(FILTER_HID, 2, 2, d).transpose(1, 2, 0, 3), jnp.asarray(decay), fwd_tab, fb, tc)
    z2 = _hyena_conv(u.reshape(b, n, 3 * d), hy_conv_w[0], hy_conv_b[0].reshape(1, 3 * d), fwd_tab, inv_tab,
                     hr, gg, nyq, hy_skip[0], fb, tc, kps=2)
    x3, h2, lgt = _proj_ln([z2.reshape(t, d)], [hy_w_out[0].astype(BF16)], x2, g1[:b], ln_g[1, 0], ln_b[1, 0],
                           sc2[:b], sh2[:b], router_wt, rows_per_mod=n, tm=512, name="hyena_out_proj_ln")
    x4 = _moe(h2, lgt, x3, g2[:b], ln_g[1, 1], ln_b[1, 1], router_bias, exp_w_gate, exp_w_up, exp_w_down, layer=1,
              rows_per_mod=n)
    return x4.reshape(b, n, d)
```

```python
import functools
import math

import jax
import jax.numpy as jnp
import numpy as np
from jax import lax
from jax.experimental import pallas as pl
from jax.experimental.pallas import tpu as pltpu

F32 = jnp.float32
BF16 = jnp.bfloat16
I32 = jnp.int32

D_MODEL = 1024
DEPTH = 2
GRID_W = 64
HEAD_DIM = 64
DIFF_HEADS = 4
WIN_Q_HEADS = 8
WIN_KV_HEADS = 2
WINDOW = 128
WIN_BLOCK = 128
ROPE_BASE = 10000.0
DIFF_QK_W = DIFF_HEADS * 2 * HEAD_DIM
DIFF_VW = DIFF_HEADS * 2 * HEAD_DIM
WIN_Q_W = WIN_Q_HEADS * HEAD_DIM
WIN_KV_W = WIN_KV_HEADS * HEAD_DIM
Q_W = DIFF_QK_W + WIN_Q_W
KV_W = DIFF_QK_W + DIFF_VW + 2 * WIN_KV_W
ATTN_PROJ_W = Q_W + KV_W
FILTER_EMB = 33
FILTER_HID = 64
DECAY_TARGET = 1e-2
FAST_DECAY_PCT = 0.3
SLOW_DECAY_PCT = 1.5
N_EXPERTS = 16
N_GROUPS = 4
EXPERTS_PER_GROUP = N_EXPERTS // N_GROUPS
EXPERT_FF = 1024
LN_EPS = 1e-5
DEEPNORM_ALPHA = (2 * DEPTH) ** 0.25
NEG_INF = -1e30
LOG2_E = 1.4426950408889634

LANES = 128
MOD_ROWS = 16
VMEM_LIMIT = 60 * 1024 * 1024
MOE_BM = 512
MOE_TILE = 512
ROW_ALIGN = 8
NT_DIMS = (((1,), (1,)), ((), ()))


def _cparams(n_axes):
    return pltpu.CompilerParams(dimension_semantics=("arbitrary",) * n_axes, vmem_limit_bytes=VMEM_LIMIT)


@functools.lru_cache(maxsize=None)
def _rope_tables(n):
    rows = n // GRID_W
    r, col = np.meshgrid(np.arange(rows, dtype=np.float32), np.arange(GRID_W, dtype=np.float32), indexing="ij")
    axis_dim = HEAD_DIM // 2
    inv_freq = (ROPE_BASE ** (-np.arange(0, axis_dim, 2, dtype=np.float32) / axis_dim)).astype(np.float32)
    ang = np.concatenate([r.reshape(-1, 1) * inv_freq, col.reshape(-1, 1) * inv_freq], -1)
    ang = np.concatenate([ang, ang], -1).astype(np.float32)
    cos, sin = np.cos(ang), np.sin(ang)
    half = np.arange(HEAD_DIM) < HEAD_DIM // 2
    sin_signed = np.where(half[None, :], -sin, sin)
    reps = LANES // HEAD_DIM
    return (np.tile(cos, (1, reps)).astype(np.float32), np.tile(sin_signed, (1, reps)).astype(np.float32))


@functools.lru_cache(maxsize=None)
def _dft_tables(n, fb):
    big = 2 * n
    k = np.arange(n, dtype=np.int64)[:, None]
    t = np.arange(n, dtype=np.int64)[None, :]
    ang = ((k * t) % big).astype(np.float64) * (2.0 * math.pi / big)
    c, s = np.cos(ang), np.sin(ang)
    alt = (1 - 2 * (np.arange(n) & 1)).astype(np.float64)
    s_f = s.copy()
    s_f[0, :] = alt
    kb = n // fb
    fwd = np.concatenate([c.reshape(kb, fb, n), s_f.reshape(kb, fb, n)], axis=1)
    ci = c.T * (2.0 / big)
    ci[:, 0] = 1.0 / big
    si = s.T * (2.0 / big)
    si[:, 0] = alt / big
    inv = np.concatenate([ci.reshape(n, kb, fb).transpose(1, 0, 2), si.reshape(n, kb, fb).transpose(1, 0, 2)], axis=2)
    return fwd.astype(np.float32), inv.astype(np.float32)


@functools.lru_cache(maxsize=None)
def _filter_tables(n):
    t = np.linspace(0.0, 1.0, n, dtype=np.float32)[:, None]
    bands = (FILTER_EMB - 1) // 2
    w = (2.0 * math.pi * np.arange(n, dtype=np.float32)[:, None] / n).astype(np.float32)
    fr = np.linspace(1e-4, bands - 1, bands, dtype=np.float32)[None, :]
    z = np.concatenate([t, np.cos(fr * w), -np.sin(fr * w)], -1).astype(np.float32)
    zp = np.zeros((n, FILTER_HID), np.float32)
    zp[:, :FILTER_EMB] = z
    deltas = np.abs(np.linspace(math.log(DECAY_TARGET) / SLOW_DECAY_PCT, math.log(DECAY_TARGET) / FAST_DECAY_PCT,
                                D_MODEL, dtype=np.float32))
    decay = np.exp(-t * deltas[None, :]).astype(np.float32)
    return zp, decay


def _mods_kernel(c_ref, w_ref, b_ref, o_ref):
    c = c_ref[...]
    a = (c * jax.nn.sigmoid(c)).astype(BF16)
    o_ref[...] = jnp.dot(a, w_ref[...].astype(BF16), preferred_element_type=F32) + b_ref[...]


def _mods(c_rows, mod_w, mod_b):
    d = D_MODEL
    tn = 1536
    return pl.pallas_call(
        _mods_kernel,
        grid=(DEPTH, 6 * d // tn),
        in_specs=[pl.BlockSpec((MOD_ROWS, d), lambda l, j: (0, 0)),
                  pl.BlockSpec((None, d, tn), lambda l, j: (l, 0, j)),
                  pl.BlockSpec((None, 1, tn), lambda l, j: (l, 0, j))],
        out_specs=pl.BlockSpec((None, MOD_ROWS, tn), lambda l, j: (l, 0, j)),
        out_shape=jax.ShapeDtypeStruct((DEPTH, MOD_ROWS, 6 * d), F32),
        compiler_params=_cparams(2), name="mods",
    )(c_rows, mod_w, mod_b.reshape(DEPTH, 1, 6 * d))


def _modmm_kernel(*refs, n_groups, rope_groups, scaled_groups, chunk):
    if rope_groups:
        x_ref, sc_ref, sh_ref, w_ref, cos_ref, sin_ref, o_ref = refs
        cos, sin = cos_ref[...], sin_ref[...]
        lane = lax.broadcasted_iota(I32, (1, LANES), 1)
        first_half = (lane % HEAD_DIM) < HEAD_DIM // 2
    else:
        x_ref, sc_ref, sh_ref, w_ref, o_ref = refs
    h = (x_ref[...] * (1.0 + sc_ref[...]) + sh_ref[...]).astype(BF16)
    scales = dict(scaled_groups)
    gpc = chunk // LANES
    for c in range(n_groups // gpc):
        acc = jnp.dot(h, w_ref[:, c * chunk:(c + 1) * chunk], preferred_element_type=F32)
        for j in range(gpc):
            g = c * gpc + j
            blk = acc[:, j * LANES:(j + 1) * LANES]
            if g in rope_groups:
                rot = jnp.where(first_half, pltpu.roll(blk, LANES - HEAD_DIM // 2, 1), pltpu.roll(blk, HEAD_DIM // 2, 1))
                blk = blk * cos + rot * sin
                if g in scales:
                    blk = blk * scales[g]
            o_ref[:, g * LANES:(g + 1) * LANES] = blk.astype(o_ref.dtype)


def _modmm(x2d, sc, sh, w, rows_per_mod, tm, rope=None, name="modmm"):
    t, d = x2d.shape
    n = w.shape[1]
    tiles_per_mod = rows_per_mod // tm
    nmod = sc.shape[0]
    in_specs = [pl.BlockSpec((tm, d), lambda i: (i, 0)),
                pl.BlockSpec((None, 1, d), lambda i: (i // tiles_per_mod, 0, 0)),
                pl.BlockSpec((None, 1, d), lambda i: (i // tiles_per_mod, 0, 0)),
                pl.BlockSpec((d, n), lambda i: (0, 0))]
    args = [x2d, sc.reshape(nmod, 1, d), sh.reshape(nmod, 1, d), w]
    rope_groups, scaled_groups = (), ()
    if rope is not None:
        cos, sin, rope_groups, scaled_groups = rope
        in_specs += [pl.BlockSpec((tm, LANES), lambda i: (i % tiles_per_mod, 0)),
                     pl.BlockSpec((tm, LANES), lambda i: (i % tiles_per_mod, 0))]
        args += [cos, sin]
    kern = functools.partial(_modmm_kernel, n_groups=n // LANES, rope_groups=tuple(rope_groups),
                             scaled_groups=tuple(scaled_groups), chunk=256)
    return pl.pallas_call(
        kern, grid=(t // tm,), in_specs=in_specs,
        out_specs=pl.BlockSpec((tm, n), lambda i: (i, 0)),
        out_shape=jax.ShapeDtypeStruct((t, n), BF16),
        compiler_params=_cparams(1), name=name,
    )(*args)


def _diff_attn_kernel(q_ref, k_ref, v_ref, kc_ref, vc_ref, lam_ref, g_ref, o_ref, *, lam_init, sub):
    lane = lax.broadcasted_iota(I32, (1, LANES), 1)
    k, kc = k_ref[...], kc_ref[...]
    v, vc = v_ref[...], vc_ref[...]

    def scores(r0):
        q = q_ref[r0:r0 + sub, :]
        zero = jnp.zeros_like(q)
        out = []
        for qm in (jnp.where(lane < HEAD_DIM, q, zero), jnp.where(lane >= HEAD_DIM, q, zero)):
            out.append((lax.dot_general(qm, k, NT_DIMS, preferred_element_type=F32),
                        lax.dot_general(qm, kc, NT_DIMS, preferred_element_type=F32)))
        return out

    def probs(sl, sc):
        m = jnp.maximum(jnp.max(sl, -1, keepdims=True), jnp.max(sc, -1, keepdims=True))
        pl_, pc = jnp.exp2(sl - m), jnp.exp2(sc - m)
        den = jnp.sum(pl_, -1, keepdims=True) + jnp.sum(pc, -1, keepdims=True)
        return pl_, pc, 1.0 / den

    lv = lam_ref[...]
    lam = (jnp.exp(jnp.sum(lv[0:1] * lv[1:2], keepdims=True)) - jnp.exp(jnp.sum(lv[2:3] * lv[3:4], keepdims=True))
           + lam_init)
    starts = list(range(0, q_ref.shape[0], sub))
    nxt = scores(starts[0])
    for i, r0 in enumerate(starts):
        cur = nxt
        if i + 1 < len(starts):
            nxt = scores(starts[i + 1])
        p1l, p1c, r1 = probs(*cur[0])
        p2l, p2c, r2 = probs(*cur[1])
        w2 = lam * r2
        al = (p1l * r1 - p2l * w2).astype(BF16)
        ac = (p1c * r1 - p2c * w2).astype(BF16)
        o = jnp.dot(al, v, preferred_element_type=F32) + jnp.dot(ac, vc, preferred_element_type=F32)
        ms = jnp.mean(o * o, -1, keepdims=True)
        o_ref[r0:r0 + sub, :] = (o * lax.rsqrt(ms + LN_EPS) * g_ref[...] * (1.0 - lam_init)).astype(o_ref.dtype)


def _diff_attn(proj, proj_c, lam_vec, subln_g, lam_init, tq, sub):
    b, n, _ = proj.shape
    nc = proj_c.shape[1]
    kcol = Q_W // LANES
    vcol = (Q_W + DIFF_QK_W) // LANES
    vccol = DIFF_QK_W // LANES
    kern = functools.partial(_diff_attn_kernel, lam_init=lam_init, sub=sub)
    return pl.pallas_call(
        kern, grid=(b, DIFF_HEADS, n // tq),
        in_specs=[pl.BlockSpec((None, tq, LANES), lambda bi, h, i: (bi, i, h)),
                  pl.BlockSpec((None, n, LANES), lambda bi, h, i: (bi, 0, kcol + h)),
                  pl.BlockSpec((None, n, LANES), lambda bi, h, i: (bi, 0, vcol + h)),
                  pl.BlockSpec((None, nc, LANES), lambda bi, h, i: (bi, 0, h)),
                  pl.BlockSpec((None, nc, LANES), lambda bi, h, i: (bi, 0, vccol + h)),
                  pl.BlockSpec((4, HEAD_DIM), lambda bi, h, i: (0, 0)),
                  pl.BlockSpec((1, LANES), lambda bi, h, i: (0, 0))],
        out_specs=pl.BlockSpec((None, tq, LANES), lambda bi, h, i: (bi, i, h)),
        out_shape=jax.ShapeDtypeStruct((b, n, DIFF_VW), BF16),
        compiler_params=_cparams(3), name="diff_attn",
    )(proj, proj, proj, proj_c, proj_c, lam_vec, subln_g.reshape(1, LANES))


def _win_attn_kernel(sink_ref, q_ref, k_ref, v_ref, kc_ref, vc_ref, o_ref, *, seq, tq):
    n = pl.program_id(1)
    gq = WIN_Q_HEADS // WIN_KV_HEADS
    kw = tq + 2 * WINDOW
    start = pl.multiple_of(jnp.clip(n * tq - WINDOW, 0, seq - kw), WINDOW)
    k_win, v_win = k_ref[pl.ds(start, kw), :], v_ref[pl.ds(start, kw), :]
    kc, vc = kc_ref[...], vc_ref[...]
    lane = lax.broadcasted_iota(I32, (1, LANES), 1)
    row = lax.broadcasted_iota(I32, (gq * tq, 1), 0)
    q_abs = n * tq + row % tq
    k_abs = start + lax.broadcasted_iota(I32, (1, kw), 1)
    allowed = jnp.abs(q_abs - k_abs) <= WINDOW
    head = row // tq
    q = q_ref[...].astype(F32)
    for g in range(WIN_KV_HEADS):
        in_g = (lane // HEAD_DIM) == g
        parts = []
        for j in range(gq):
            hq = g * gq + j
            x = q[:, (hq // 2) * LANES:(hq // 2 + 1) * LANES]
            if hq % 2 != g:
                x = pltpu.roll(x, HEAD_DIM, 1)
            parts.append(jnp.where(in_g, x, 0.0))
        qs = jnp.concatenate(parts, 0).astype(BF16)
        s_loc = lax.dot_general(qs, k_win, NT_DIMS, preferred_element_type=F32)
        s_ctx = lax.dot_general(qs, kc, NT_DIMS, preferred_element_type=F32)
        s_loc = jnp.where(allowed, s_loc, NEG_INF)
        sk = jnp.zeros((gq * tq, 1), F32)
        for j in range(gq):
            sk = jnp.where(head == j, sink_ref[g * gq + j] * LOG2_E, sk)
        m = jnp.maximum(jnp.maximum(jnp.max(s_loc, -1, keepdims=True), jnp.max(s_ctx, -1, keepdims=True)), sk)
        p_loc, p_ctx = jnp.exp2(s_loc - m), jnp.exp2(s_ctx - m)
        den = jnp.sum(p_loc, -1, keepdims=True) + jnp.sum(p_ctx, -1, keepdims=True) + jnp.exp2(sk - m)
        o = (jnp.dot(p_loc.astype(BF16), v_win, preferred_element_type=F32)
             + jnp.dot(p_ctx.astype(BF16), vc, preferred_element_type=F32)) * (1.0 / den)
        for cb in range(gq // 2):
            pair = []
            for half in range(2):
                piece = o[(2 * cb + half) * tq:(2 * cb + half + 1) * tq]
                pair.append(piece if half == g else pltpu.roll(piece, HEAD_DIM, 1))
            col = (g * gq // 2 + cb) * LANES
            o_ref[:, col:col + LANES] = jnp.where(lane < HEAD_DIM, pair[0], pair[1]).astype(o_ref.dtype)


def _win_attn(proj, proj_c, sink, tq):
    b, n, _ = proj.shape
    nc = proj_c.shape[1]
    qcol = DIFF_QK_W // WIN_Q_W
    kcol = (Q_W + DIFF_QK_W + DIFF_VW) // LANES
    kccol = (DIFF_QK_W + DIFF_VW) // LANES
    kern = functools.partial(_win_attn_kernel, seq=n, tq=tq)
    return pl.pallas_call(
        kern, grid=(b, n // tq),
        in_specs=[pl.BlockSpec(memory_space=pltpu.SMEM),
                  pl.BlockSpec((None, tq, WIN_Q_W), lambda bi, i: (bi, i, qcol)),
                  pl.BlockSpec((None, n, LANES), lambda bi, i: (bi, 0, kcol)),
                  pl.BlockSpec((None, n, LANES), lambda bi, i: (bi, 0, kcol + 1)),
                  pl.BlockSpec((None, nc, LANES), lambda bi, i: (bi, 0, kccol)),
                  pl.BlockSpec((None, nc, LANES), lambda bi, i: (bi, 0, kccol + 1))],
        out_specs=pl.BlockSpec((None, tq, WIN_Q_W), lambda bi, i: (bi, i, 0)),
        out_shape=jax.ShapeDtypeStruct((b, n, WIN_Q_W), BF16),
        compiler_params=_cparams(2), name="win_attn",
    )(sink, proj, proj, proj, proj_c, proj_c)


def _layer_norm(r, g, b):
    mu = jnp.mean(r, -1, keepdims=True)
    dlt = r - mu
    var = jnp.mean(dlt * dlt, -1, keepdims=True)
    return dlt * lax.rsqrt(var + LN_EPS) * g + b


def _proj_ln_kernel(*refs, n_in, sub):
    a_refs = refs[:n_in]
    w_refs = refs[n_in:2 * n_in]
    x_ref, gate_ref, lng_ref, lnb_ref, sc_ref, sh_ref, rw_ref, x1_ref, h2_ref, lg_ref = refs[2 * n_in:]

    def split(v):
        hi = v.astype(BF16)
        return hi, (v - hi.astype(F32)).astype(BF16)

    def project(r0):
        y = jnp.dot(a_refs[0][r0:r0 + sub, :], w_refs[0][...], preferred_element_type=F32)
        for a_ref, w_ref in zip(a_refs[1:], w_refs[1:]):
            y = y + jnp.dot(a_ref[r0:r0 + sub, :], w_ref[...], preferred_element_type=F32)
        return y

    rw_hi, rw_lo = split(rw_ref[...])
    rw_both = jnp.concatenate([rw_hi, rw_lo], 1)
    starts = list(range(0, x_ref.shape[0], sub))
    nxt = project(starts[0])
    for i, r0 in enumerate(starts):
        y = nxt
        if i + 1 < len(starts):
            nxt = project(starts[i + 1])
        xn = _layer_norm(DEEPNORM_ALPHA * x_ref[r0:r0 + sub, :] + gate_ref[...] * y, lng_ref[...], lnb_ref[...])
        x1_ref[r0:r0 + sub, :] = xn
        h2 = xn * (1.0 + sc_ref[...]) + sh_ref[...]
        h2_ref[r0:r0 + sub, :] = h2.astype(h2_ref.dtype)
        h_hi, h_lo = split(h2)
        hh = jnp.dot(h_hi, rw_both, preferred_element_type=F32)
        lg = hh[:, :LANES] + hh[:, LANES:] + jnp.dot(h_lo, rw_hi, preferred_element_type=F32)
        lg_ref[:, r0:r0 + sub] = lg.T[:N_EXPERTS, :]


def _proj_ln(acts, ws, x2d, gate, ln_g, ln_b, sc, sh, router_wt, rows_per_mod, tm, name):
    t, d = x2d.shape
    n_in = len(acts)
    tiles_per_mod = rows_per_mod // tm
    nmod = gate.shape[0]
    row = lambda i: (i, 0)
    full = lambda i: (0, 0)
    mod = lambda i: (i // tiles_per_mod, 0, 0)
    in_specs = ([pl.BlockSpec((tm, a.shape[1]), row) for a in acts]
                + [pl.BlockSpec(w.shape, full) for w in ws]
                + [pl.BlockSpec((tm, d), row), pl.BlockSpec((None, 1, d), mod),
                   pl.BlockSpec((1, d), full), pl.BlockSpec((1, d), full),
                   pl.BlockSpec((None, 1, d), mod), pl.BlockSpec((None, 1, d), mod),
                   pl.BlockSpec((d, LANES), full)])
    return pl.pallas_call(
        functools.partial(_proj_ln_kernel, n_in=n_in, sub=LANES), grid=(t // tm,), in_specs=in_specs,
        out_specs=[pl.BlockSpec((tm, d), row), pl.BlockSpec((tm, d), row),
                   pl.BlockSpec((N_EXPERTS, tm), lambda i: (0, i))],
        out_shape=[jax.ShapeDtypeStruct((t, d), F32), jax.ShapeDtypeStruct((t, d), BF16),
                   jax.ShapeDtypeStruct((N_EXPERTS, t), F32)],
        compiler_params=_cparams(1), name=name,
    )(*acts, *ws, x2d, gate.reshape(nmod, 1, d), ln_g.reshape(1, d), ln_b.reshape(1, d),
      sc.reshape(nmod, 1, d), sh.reshape(nmod, 1, d), router_wt)


def _first_argmax(vals):
    idx = jnp.zeros(vals[0].shape, I32)
    best = vals[0]
    for j in range(1, len(vals)):
        upd = vals[j] > best
        idx = jnp.where(upd, j, idx)
        best = jnp.where(upd, vals[j], best)
    return idx, best


def _route_kernel(lg_ref, bias_ref, pos_ref, w_ref, tab_ref, cnt_ref, carry_ref, off_ref, *, tr):
    @pl.when(pl.program_id(0) == 0)
    def _():
        carry_ref[...] = jnp.zeros_like(carry_ref)

    lg = lg_ref[...]
    ex = jnp.exp(lg - jnp.max(lg, 0, keepdims=True))
    scores = ex / jnp.sum(ex, 0, keepdims=True)
    sel = scores + bias_ref[...]
    rows = [sel[e:e + 1] for e in range(N_EXPERTS)]
    group_scores = []
    for g in range(N_GROUPS):
        r = rows[g * EXPERTS_PER_GROUP:(g + 1) * EXPERTS_PER_GROUP]
        best = None
        for i in range(EXPERTS_PER_GROUP):
            for j in range(i + 1, EXPERTS_PER_GROUP):
                s = r[i] + r[j]
                best = s if best is None else jnp.maximum(best, s)
        group_scores.append(best)
    grp, _ = _first_argmax(group_scores)
    vals = []
    for j in range(EXPERTS_PER_GROUP):
        v = rows[(N_GROUPS - 1) * EXPERTS_PER_GROUP + j]
        for g in range(N_GROUPS - 2, -1, -1):
            v = jnp.where(grp == g, rows[g * EXPERTS_PER_GROUP + j], v)
        vals.append(v)
    i0, _ = _first_argmax(vals)
    i1, _ = _first_argmax([jnp.where(i0 == j, -jnp.inf, vals[j]) for j in range(EXPERTS_PER_GROUP)])
    e0 = grp * EXPERTS_PER_GROUP + i0
    e1 = grp * EXPERTS_PER_GROUP + i1
    eid = lax.broadcasted_iota(I32, (N_EXPERTS, 1), 0)
    oh0, oh1 = eid == e0, eid == e1
    s0 = jnp.sum(jnp.where(oh0, scores, 0.0), 0, keepdims=True)
    s1 = jnp.sum(jnp.where(oh1, scores, 0.0), 0, keepdims=True)
    den = s0 + s1
    member = jnp.where(oh0 | oh1, 1.0, 0.0)
    before = lax.broadcasted_iota(I32, (tr, tr), 0) < lax.broadcasted_iota(I32, (tr, tr), 1)
    upper = jnp.where(before, 1.0, 0.0).astype(BF16)
    cnt = jnp.dot(member.astype(BF16), upper, preferred_element_type=F32)
    run = jnp.sum(member, 1, keepdims=True)
    run = jnp.floor((run + (ROW_ALIGN - 1)) * (1.0 / ROW_ALIGN)) * ROW_ALIGN
    run = jnp.broadcast_to(run, (N_EXPERTS, LANES))
    acc = jnp.zeros((1, LANES), F32)
    for e in range(N_EXPERTS):
        off_ref[e:e + 1, :] = acc
        acc = acc + run[e:e + 1]
    off = off_ref[...]
    at = off[:, 0:1] + cnt
    pos_ref[0:1, :] = jnp.sum(jnp.where(oh0, at, 0.0), 0, keepdims=True).astype(I32)
    pos_ref[1:2, :] = jnp.sum(jnp.where(oh1, at, 0.0), 0, keepdims=True).astype(I32)
    w_ref[0:1, :] = s0 / den
    w_ref[1:2, :] = s1 / den
    tab_ref[0] = off.astype(I32)
    tab_ref[1] = run.astype(I32)
    tab_ref[2] = carry_ref[...].astype(I32)
    carry_ref[...] = carry_ref[...] + run
    cnt_ref[...] = carry_ref[...]


def _route(logits_t, router_bias, tr):
    t = logits_t.shape[1]
    tok = lambda i: (0, i)
    return pl.pallas_call(
        functools.partial(_route_kernel, tr=tr), grid=(t // tr,),
        in_specs=[pl.BlockSpec((N_EXPERTS, tr), tok), pl.BlockSpec((N_EXPERTS, 1), lambda i: (0, 0))],
        out_specs=[pl.BlockSpec((2, tr), tok), pl.BlockSpec((2, tr), tok),
                   pl.BlockSpec((None, 3, N_EXPERTS, LANES), lambda i: (i, 0, 0, 0)),
                   pl.BlockSpec((N_EXPERTS, LANES), lambda i: (0, 0))],
        out_shape=[jax.ShapeDtypeStruct((2, t), I32), jax.ShapeDtypeStruct((2, t), F32),
                   jax.ShapeDtypeStruct((t // tr, 3, N_EXPERTS, LANES), I32),
                   jax.ShapeDtypeStruct((N_EXPERTS, LANES), F32)],
        scratch_shapes=[pltpu.VMEM((N_EXPERTS, LANES), F32), pltpu.VMEM((N_EXPERTS, LANES), F32)],
        compiler_params=_cparams(1), name="route",
    )(logits_t, router_bias.reshape(N_EXPERTS, 1))


def _run_copies(n, max_rows, make_copy, wait):
    sz = max_rows
    while sz >= ROW_ALIGN:
        start = (n // (2 * sz)) * (2 * sz)

        @pl.when((n & sz) != 0)
        def _(start=start, sz=sz):
            cp = make_copy(start, sz)
            cp.wait() if wait else cp.start()

        sz //= 2


def _wait_rows(total, max_rows, make_copy):
    sz = max_rows
    while sz >= ROW_ALIGN:
        @pl.when((total & sz) != 0)
        def _(sz=sz):
            make_copy(sz).wait()

        sz //= 2


def _rows(ref, start, size):
    return ref.at[pl.ds(pl.multiple_of(start, ROW_ALIGN), size)]


def _dispatch_kernel(tab_ref, h_ref, pos_ref, w_ref, xs_ref, buf, zbuf, sem, *, tt, n_tiles, bm):
    i = pl.program_id(0)
    d = h_ref.shape[1]
    sb = buf.shape[1]
    ne = N_EXPERTS
    fill = 3 * n_tiles * ne

    @pl.when(i == 0)
    def _():
        zbuf[...] = jnp.zeros_like(zbuf)
        for wait in (False, True):
            for e in range(ne):
                dst, n = tab_ref[fill + e], tab_ref[fill + ne + e]
                _run_copies(n, bm // 2, lambda s, z: pltpu.make_async_copy(
                    zbuf.at[pl.ds(0, z)], _rows(xs_ref, dst + s, z), sem.at[2]), wait)

        def zero_block(j, carry):
            for half in range(2):
                cp = pltpu.make_async_copy(zbuf, _rows(xs_ref, j * bm + half * (bm // 2), bm // 2), sem.at[2])
                cp.start()
                cp.wait()
            return carry

        lax.fori_loop(tab_ref[fill + 2 * ne], xs_ref.shape[0] // bm, zero_block, 0)

    def tile_copies(tile, slot, wait):
        for e in range(ne):
            off = tab_ref[tile * ne + e]
            n = tab_ref[(n_tiles + tile) * ne + e]
            dst = tab_ref[(2 * n_tiles + tile) * ne + e]
            _run_copies(n, tt, lambda s, z: pltpu.make_async_copy(
                _rows(buf.at[slot], off + s, z), _rows(xs_ref, dst + s, z), sem.at[slot]), wait)

    slot = i % 2
    pos, w = pos_ref[...], w_ref[...]
    hb = h_ref[...].astype(BF16)
    for r0 in range(0, sb, LANES):
        srow = r0 + lax.broadcasted_iota(I32, (LANES, 1), 0)
        m0, m1 = srow == pos[0:1], srow == pos[1:2]
        perm = (jnp.where(m0, 1.0, 0.0) + jnp.where(m1, 1.0, 0.0)).astype(BF16)
        buf[slot, r0:r0 + LANES, :d] = jnp.dot(perm, hb, preferred_element_type=F32)
        wrow = jnp.sum(jnp.where(m0, w[0:1], 0.0) + jnp.where(m1, w[1:2], 0.0), 1, keepdims=True)
        buf[slot, r0:r0 + LANES, d:] = jnp.broadcast_to(wrow, (LANES, LANES))

    def wait_tile(tile, slot):
        total = tab_ref[fill + 2 * ne + 1 + tile]
        _wait_rows(total, tt * 2, lambda z: pltpu.make_async_copy(
            buf.at[slot, pl.ds(0, z)], xs_ref.at[pl.ds(0, z)], sem.at[slot]))

    @pl.when(i > 0)
    def _():
        wait_tile(i - 1, 1 - slot)

    tile_copies(i, slot, False)

    @pl.when(i == n_tiles - 1)
    def _():
        wait_tile(i, slot)


def _dispatch(tab, h2, pos, wts, n_slots, tt, sb, bm):
    t, d = h2.shape
    n_tiles = t // tt
    return pl.pallas_call(
        functools.partial(_dispatch_kernel, tt=tt, n_tiles=n_tiles, bm=bm),
        grid_spec=pltpu.PrefetchScalarGridSpec(
            num_scalar_prefetch=1, grid=(n_tiles,),
            in_specs=[pl.BlockSpec((tt, d), lambda i, s: (i, 0)), pl.BlockSpec((2, tt), lambda i, s: (0, i)),
                      pl.BlockSpec((2, tt), lambda i, s: (0, i))],
            out_specs=pl.BlockSpec(memory_space=pl.ANY),
            scratch_shapes=[pltpu.VMEM((2, sb, d + LANES), F32), pltpu.VMEM((bm // 2, d + LANES), F32),
                            pltpu.SemaphoreType.DMA((3,))]),
        out_shape=jax.ShapeDtypeStruct((n_slots, d + LANES), F32),
        compiler_params=_cparams(1), name="moe_dispatch",
    )(tab, h2, pos, wts)


def _ffn_kernel(blk_e_ref, blk_rows_ref, next_e_ref, nused_ref, x_ref, wg_hbm, wu_hbm, wd_hbm, o_ref,
                stage, wg_bf, wu_bf, wd_bf, sem, *, layer, parts):
    del nused_ref
    i = pl.program_id(0)
    e = blk_e_ref[i]
    new_expert = (i == 0) | (e != blk_e_ref[jnp.maximum(i - 1, 0)])

    def fetch(expert, wait):
        for j, src in enumerate((wg_hbm, wu_hbm, wd_hbm)):
            cp = pltpu.make_async_copy(src.at[layer, expert], stage.at[j], sem.at[j])
            cp.wait() if wait else cp.start()

    @pl.when(i == 0)
    def _():
        fetch(e, False)

    @pl.when(new_expert)
    def _():
        fetch(e, True)
        wg_bf[...] = stage[0].astype(BF16)
        wu_bf[...] = stage[1].astype(BF16)
        wd_bf[...] = stage[2].astype(BF16)

        @pl.when(next_e_ref[i] >= 0)
        def _():
            fetch(next_e_ref[i], False)

    d = wg_bf.shape[0]
    rows = blk_rows_ref[i]
    part = x_ref.shape[0] // parts
    for p in range(parts):
        rs = slice(p * part, (p + 1) * part)

        @pl.when(rows > p * part)
        def _(rs=rs):
            x = x_ref[rs, :d].astype(BF16)
            gate = jnp.dot(x, wg_bf[...], preferred_element_type=F32)
            up = jnp.dot(x, wu_bf[...], preferred_element_type=F32)
            act = (gate * jax.nn.sigmoid(gate) * up).astype(BF16)
            o_ref[rs, :] = jnp.dot(act, wd_bf[...], preferred_element_type=F32) * x_ref[rs, d:d + 1]

        @pl.when(rows <= p * part)
        def _(rs=rs):
            o_ref[rs, :] = jnp.zeros((part, d), F32)


def _ffn(blk_e, blk_rows, next_e, n_used, xs, wg, wu, wd, layer, bm, parts):
    n_slots, xw = xs.shape
    d, ff = wg.shape[2:]
    assert d == ff
    xrow = lambda i, be, br, ne, nu: (jnp.minimum(i, nu[0] - 1), 0)
    hbm = pl.BlockSpec(memory_space=pl.ANY)
    return pl.pallas_call(
        functools.partial(_ffn_kernel, layer=layer, parts=parts),
        grid_spec=pltpu.PrefetchScalarGridSpec(
            num_scalar_prefetch=4, grid=(n_slots // bm,),
            in_specs=[pl.BlockSpec((bm, xw), xrow), hbm, hbm, hbm],
            out_specs=pl.BlockSpec((bm, d), lambda i, be, br, ne, nu: (i, 0)),
            scratch_shapes=[pltpu.VMEM((3, d, ff), F32), pltpu.VMEM((d, ff), BF16), pltpu.VMEM((d, ff), BF16),
                            pltpu.VMEM((ff, d), BF16), pltpu.SemaphoreType.DMA((3,))]),
        out_shape=jax.ShapeDtypeStruct((n_slots, d), F32),
        compiler_params=_cparams(1), name="moe_ffn",
    )(blk_e, blk_rows, next_e, n_used, xs, wg, wu, wd)


def _combine_kernel(tab_ref, ys_ref, pos_ref, x1_ref, gate_ref, lng_ref, lnb_ref, o_ref, ybuf, sem, *, tt, n_tiles):
    i = pl.program_id(0)
    sb = ybuf.shape[1]
    ne = N_EXPERTS
    slot = i % 2

    def tile_copies(tile, slot, wait):
        for e in range(ne):
            off = tab_ref[tile * ne + e]
            n = tab_ref[(n_tiles + tile) * ne + e]
            src = tab_ref[(2 * n_tiles + tile) * ne + e]
            _run_copies(n, tt, lambda s, z: pltpu.make_async_copy(
                _rows(ys_ref, src + s, z), _rows(ybuf.at[slot], off + s, z), sem.at[slot]), wait)

    @pl.when(i == 0)
    def _():
        ybuf[...] = jnp.zeros_like(ybuf)
        tile_copies(0, 0, False)

    @pl.when(i + 1 < n_tiles)
    def _():
        tile_copies(i + 1, 1 - slot, False)

    total = tab_ref[3 * n_tiles * ne + 2 * ne + 1 + i]
    _wait_rows(total, tt * 2, lambda z: pltpu.make_async_copy(
        ys_ref.at[pl.ds(0, z)], ybuf.at[slot, pl.ds(0, z)], sem.at[slot]))
    scol = lax.broadcasted_iota(I32, (1, sb), 1)
    yb = ybuf[slot].astype(BF16)

    def moe_rows(r0):
        pos = pos_ref[r0:r0 + LANES, :]
        unperm = (jnp.where(scol == pos[:, 0:1], 1.0, 0.0) + jnp.where(scol == pos[:, 1:2], 1.0, 0.0)).astype(BF16)
        return jnp.dot(unperm, yb, preferred_element_type=F32)

    starts = list(range(0, tt, LANES))
    nxt = moe_rows(starts[0])
    for j, r0 in enumerate(starts):
        f = nxt
        if j + 1 < len(starts):
            nxt = moe_rows(starts[j + 1])
        o_ref[r0:r0 + LANES, :] = _layer_norm(DEEPNORM_ALPHA * x1_ref[r0:r0 + LANES, :] + gate_ref[...] * f,
                                              lng_ref[...], lnb_ref[...])


def _combine(tab, ys, pos_t, x1, gate, ln_g, ln_b, rows_per_mod, tt, sb):
    t, d = x1.shape
    tiles_per_mod = rows_per_mod // tt
    nmod = gate.shape[0]
    return pl.pallas_call(
        functools.partial(_combine_kernel, tt=tt, n_tiles=t // tt),
        grid_spec=pltpu.PrefetchScalarGridSpec(
            num_scalar_prefetch=1, grid=(t // tt,),
            in_specs=[pl.BlockSpec(memory_space=pl.ANY),
                      pl.BlockSpec((tt, 2), lambda i, s: (i, 0)),
                      pl.BlockSpec((tt, d), lambda i, s: (i, 0)),
                      pl.BlockSpec((None, 1, d), lambda i, s: (i // tiles_per_mod, 0, 0)),
                      pl.BlockSpec((1, d), lambda i, s: (0, 0)), pl.BlockSpec((1, d), lambda i, s: (0, 0))],
            out_specs=pl.BlockSpec((tt, d), lambda i, s: (i, 0)),
            scratch_shapes=[pltpu.VMEM((2, sb, d), F32), pltpu.SemaphoreType.DMA((2,))]),
        out_shape=jax.ShapeDtypeStruct((t, d), F32),
        compiler_params=_cparams(1), name="moe_combine",
    )(tab, ys, pos_t, x1, gate.reshape(nmod, 1, d), ln_g.reshape(1, d), ln_b.reshape(1, d))


def _moe(h2, logits_t, x1, gate, ln_g, ln_b, router_bias, wg, wu, wd, layer, rows_per_mod):
    t, d = h2.shape
    bm, tt, ne = MOE_BM, MOE_TILE, N_EXPERTS
    n_tiles = t // tt
    pad = ne * (ROW_ALIGN - 1)
    sb = -(-(2 * tt + pad) // LANES) * LANES
    pos, wts, tab3, cnt = _route(logits_t, router_bias, tr=tt)
    rows = cnt[:, 0].astype(I32)
    prows = (rows + bm - 1) // bm * bm
    pends = jnp.cumsum(prows)
    pstarts = pends - prows
    n_blk = -(-(2 * t + n_tiles * pad + ne * (bm - 1)) // bm)
    n_used = pends[-1] // bm
    blk_ids = jnp.minimum(jnp.arange(n_blk, dtype=I32), n_used - 1)
    blk_e = jnp.minimum(jnp.sum((blk_ids[:, None] * bm >= pends[None, :]).astype(I32), 1), ne - 1)
    eids = jnp.arange(ne, dtype=I32)
    of_blk = lambda per_expert: jnp.sum(jnp.where(blk_e[:, None] == eids[None, :], per_expert[None, :], 0), 1)
    raw_ids = jnp.arange(n_blk, dtype=I32)
    blk_rows = jnp.where(raw_ids < n_used, jnp.clip(of_blk(pstarts + rows) - raw_ids * bm, 0, bm), 0)
    later_used = (eids[None, :] > eids[:, None]) & (prows[None, :] > 0)
    next_used = jnp.min(jnp.where(later_used, eids[None, :], ne), 1)
    next_e = of_blk(jnp.where(next_used < ne, next_used, -1))
    tab3 = tab3[:, :, :, 0]
    tab = jnp.concatenate([tab3[:, 0].reshape(-1), tab3[:, 1].reshape(-1),
                           (tab3[:, 2] + pstarts[None, :]).reshape(-1), pstarts + rows, prows - rows,
                           n_used.reshape(1), jnp.sum(tab3[:, 1], 1)]).astype(I32)
    xs = _dispatch(tab, h2, pos, wts, n_blk * bm, tt, sb, bm)
    ys = _ffn(blk_e, blk_rows.astype(I32), next_e.astype(I32), n_used.reshape(1).astype(I32), xs, wg, wu, wd,
              layer, bm, parts=2)
    return _combine(tab, ys, pos.T, x1, gate, ln_g, ln_b, rows_per_mod, tt, sb)


def _filter_mlp_kernel(z_ref, w1_ref, w2_ref, w3_ref, b_ref, sf_ref, o_ref):
    hp = lax.Precision.HIGHEST
    b, sf = b_ref[...], sf_ref[...]
    h = jnp.sin(sf[0:1] * (jnp.dot(z_ref[...], w1_ref[...], precision=hp, preferred_element_type=F32) + b[0:1]))
    h = jnp.sin(sf[1:2] * (jnp.dot(h, w2_ref[...], precision=hp, preferred_element_type=F32) + b[1:2]))
    o_ref[...] = jnp.sin(sf[2:3] * (jnp.dot(h, w3_ref[...], precision=hp, preferred_element_type=F32) + b[2:3]))


def _filter_mlp(zp, w_in_p, w_hid, b, sf):
    n = zp.shape[0]
    return pl.pallas_call(
        _filter_mlp_kernel, out_shape=jax.ShapeDtypeStruct((n, FILTER_HID), F32),
        compiler_params=pltpu.CompilerParams(vmem_limit_bytes=VMEM_LIMIT), name="hyena_filter_mlp",
    )(zp, w_in_p, w_hid[0], w_hid[1], b, sf)


def _filter_spec_kernel(hid_ref, wf_ref, wb_ref, dec_ref, fwd_ref, hr_ref, g_ref, nyq_ref, p_scr, q_scr, *, fb, kps):
    k = pl.program_id(2)
    n = hid_ref.shape[0]

    @pl.when(k == 0)
    def _():
        hp = lax.Precision.HIGHEST
        hid, dec = hid_ref[...], dec_ref[...]
        row = lax.broadcasted_iota(I32, (n, 1), 0)
        fw = jnp.dot(hid, wf_ref[...], precision=hp, preferred_element_type=F32) * dec
        bw = jnp.dot(hid, wb_ref[...], precision=hp, preferred_element_type=F32) * dec
        bw = jnp.where(row == 0, 0.0, bw)
        p = fw + bw
        p_scr[...] = p.astype(BF16)
        q_scr[...] = (fw - bw).astype(BF16)
        sign = (1 - 2 * (row & 1)).astype(F32)
        nyq_ref[...] = jnp.sum(p * sign, 0, keepdims=True)

    rowb = lax.broadcasted_iota(I32, (fb, 1), 0)
    for j in range(kps):
        hr_ref[j * fb:(j + 1) * fb, :] = jnp.dot(fwd_ref[j, 0:fb, :], p_scr[...], preferred_element_type=F32)
        gg = jnp.dot(fwd_ref[j, fb:2 * fb, :], q_scr[...], preferred_element_type=F32)
        if j == 0:
            gg = jnp.where((rowb == 0) & (k == 0), 0.0, gg)
        g_ref[j * fb:(j + 1) * fb, :] = gg


def _filter_spec(hid, w_out4, decay, fwd_tab, fb, tc, kps):
    n = hid.shape[0]
    d = D_MODEL
    kb = n // (fb * kps)
    return pl.pallas_call(
        functools.partial(_filter_spec_kernel, fb=fb, kps=kps), grid=(2, d // tc, kb),
        in_specs=[pl.BlockSpec((n, FILTER_HID), lambda o, c, k: (0, 0)),
                  pl.BlockSpec((None, None, FILTER_HID, tc), lambda o, c, k: (0, o, 0, c)),
                  pl.BlockSpec((None, None, FILTER_HID, tc), lambda o, c, k: (1, o, 0, c)),
                  pl.BlockSpec((n, tc), lambda o, c, k: (0, c)),
                  pl.BlockSpec((kps, 2 * fb, n), lambda o, c, k: (k, 0, 0))],
        out_specs=[pl.BlockSpec((None, kps * fb, tc), lambda o, c, k: (o, k, c)),
                   pl.BlockSpec((None, kps * fb, tc), lambda o, c, k: (o, k, c)),
                   pl.BlockSpec((None, 1, tc), lambda o, c, k: (o, 0, c))],
        out_shape=[jax.ShapeDtypeStruct((2, n, d), F32), jax.ShapeDtypeStruct((2, n, d), F32),
                   jax.ShapeDtypeStruct((2, 1, d), F32)],
        scratch_shapes=[pltpu.VMEM((n, tc), BF16), pltpu.VMEM((n, tc), BF16)],
        compiler_params=_cparams(3), name="hyena_filter_spec",
    )(hid, w_out4, w_out4, decay, fwd_tab)


def _short_conv(u_ref, cw_ref, cb_ref):
    n = u_ref.shape[0]
    u = u_ref[...].astype(F32)
    row = lax.broadcasted_iota(I32, (n, 1), 0)
    prev = jnp.where(row == 0, 0.0, pltpu.roll(u, 1, 0))
    nxt = jnp.where(row == n - 1, 0.0, pltpu.roll(u, n - 1, 0))
    cw = cw_ref[...]
    return prev * cw[0:1] + u * cw[1:2] + nxt * cw[2:3] + cb_ref[...]


def _hyena_conv_kernel(uv_ref, ux1_ref, ux2_ref, cwv_ref, cwx1_ref, cwx2_ref, cbv_ref, cbx1_ref, cbx2_ref,
                       fwd_ref, inv_ref, hr_ref, g_ref, nyq_ref, sk_ref, o_ref, zb_scr, z32_scr, acc_scr, *, fb, kb,
                       kps):
    o = pl.program_id(2)
    k = pl.program_id(3)

    @pl.when((o == 0) & (k == 0))
    def _():
        v = _short_conv(uv_ref, cwv_ref, cbv_ref)
        z32_scr[...] = v
        zb_scr[...] = v.astype(BF16)

    @pl.when(k == 0)
    def _():
        acc_scr[...] = jnp.zeros_like(acc_scr)

    zb = zb_scr[...]
    zfs = [jnp.dot(fwd_ref[j], zb, preferred_element_type=F32) for j in range(kps)]
    rowb = lax.broadcasted_iota(I32, (fb, 1), 0)
    for j in range(kps):
        zr, zi = zfs[j][:fb], zfs[j][fb:]
        hr, gg = hr_ref[j * fb:(j + 1) * fb, :], g_ref[j * fb:(j + 1) * fb, :]
        hb = jnp.where((rowb == 0) & (k == 0), nyq_ref[...], hr) if j == 0 else hr
        y = jnp.concatenate([zr * hr - zi * gg, zr * gg + zi * hb], 0).astype(BF16)
        acc_scr[...] += jnp.dot(inv_ref[j], y, preferred_element_type=F32)

    @pl.when((k == kb - 1) & (o == 0))
    def _():
        zn = _short_conv(ux1_ref, cwx1_ref, cbx1_ref) * (acc_scr[...] + z32_scr[...] * sk_ref[...])
        z32_scr[...] = zn
        zb_scr[...] = zn.astype(BF16)

    @pl.when((k == kb - 1) & (o == 1))
    def _():
        zn = _short_conv(ux2_ref, cwx2_ref, cbx2_ref) * (acc_scr[...] + z32_scr[...] * sk_ref[...])
        o_ref[...] = zn.astype(o_ref.dtype)


def _hyena_conv(u, conv_w, conv_b, fwd_tab, inv_tab, hr, gg, nyq, skip, fb, tc, kps):
    b, n, _ = u.shape
    d = D_MODEL
    kb = n // (fb * kps)
    ncb = d // tc
    ucol = lambda part: (lambda bi, c, o, k: (bi, 0, part * ncb + c))
    wcol = lambda part: (lambda bi, c, o, k: (0, part * ncb + c))
    spec = lambda bi, c, o, k: (o, k, c)
    per_o = lambda bi, c, o, k: (o, 0, c)
    return pl.pallas_call(
        functools.partial(_hyena_conv_kernel, fb=fb, kb=kb, kps=kps), grid=(b, ncb, 2, kb),
        in_specs=[pl.BlockSpec((None, n, tc), ucol(0)), pl.BlockSpec((None, n, tc), ucol(1)),
                  pl.BlockSpec((None, n, tc), ucol(2)),
                  pl.BlockSpec((3, tc), wcol(0)), pl.BlockSpec((3, tc), wcol(1)), pl.BlockSpec((3, tc), wcol(2)),
                  pl.BlockSpec((1, tc), wcol(0)), pl.BlockSpec((1, tc), wcol(1)), pl.BlockSpec((1, tc), wcol(2)),
                  pl.BlockSpec((kps, 2 * fb, n), lambda bi, c, o, k: (k, 0, 0)),
                  pl.BlockSpec((kps, n, 2 * fb), lambda bi, c, o, k: (k, 0, 0)),
                  pl.BlockSpec((None, kps * fb, tc), spec), pl.BlockSpec((None, kps * fb, tc), spec),
                  pl.BlockSpec((None, 1, tc), per_o), pl.BlockSpec((None, 1, tc), per_o)],
        out_specs=pl.BlockSpec((None, n, tc), lambda bi, c, o, k: (bi, 0, c)),
        out_shape=jax.ShapeDtypeStruct((b, n, d), BF16),
        scratch_shapes=[pltpu.VMEM((n, tc), BF16), pltpu.VMEM((n, tc), F32), pltpu.VMEM((n, tc), F32)],
        compiler_params=_cparams(4), name="hyena_conv",
    )(u, u, u, conv_w, conv_w, conv_w, conv_b, conv_b, conv_b, fwd_tab, inv_tab, hr, gg, nyq,
      skip.reshape(2, 1, d))


def kernel(x, c, ctx, c_ctx, mod_w, mod_b, ln_g, ln_b, attn_w_in, attn_lambda, attn_subln_g, attn_sink, attn_w_out,
           hy_w_in, hy_conv_w, hy_conv_b, hy_ffn_w_in, hy_ffn_w_hid, hy_ffn_b, hy_sin_freq, hy_ffn_w_out, hy_skip,
           hy_w_out, router_w, router_bias, exp_w_gate, exp_w_up, exp_w_down):
    b, n, d = x.shape
    nc = ctx.shape[1]
    t = b * n
    assert d == D_MODEL and b + 1 <= MOD_ROWS and n % 512 == 0 and nc % 256 == 0

    c_rows = jnp.zeros((MOD_ROWS, d), F32).at[:b].set(c).at[b].set(c_ctx)
    mods = _mods(c_rows, mod_w, mod_b)
    router_wt = jnp.pad(router_w, ((0, 0), (0, LANES - N_EXPERTS)))
    x2d = x.reshape(t, d)

    sh1, sc1, g1, sh2, sc2, g2 = jnp.split(mods[0], 6, axis=-1)
    cos, sin = _rope_tables(n)
    w_in = attn_w_in[0].astype(BF16)
    q_groups = tuple(range(Q_W // LANES))
    rope_groups = q_groups + tuple(range(Q_W // LANES, (Q_W + DIFF_QK_W) // LANES)) + (
        (Q_W + DIFF_QK_W + DIFF_VW) // LANES,)
    q_scales = tuple((g, HEAD_DIM ** -0.5 * LOG2_E) for g in q_groups)
    proj = _modmm(x2d, sc1[:b], sh1[:b], w_in, rows_per_mod=n, tm=512,
                  rope=(jnp.asarray(cos), jnp.asarray(sin), rope_groups, q_scales), name="attn_in_proj")
    proj_c = _modmm(ctx.reshape(b * nc, d), sc1[b:b + 1], sh1[b:b + 1], w_in[:, Q_W:], rows_per_mod=b * nc, tm=256,
                    name="ctx_in_proj")
    proj = proj.reshape(b, n, ATTN_PROJ_W)
    proj_c = proj_c.reshape(b, nc, KV_W)
    lam_init = 0.8 - 0.6 * math.exp(-0.3 * 0)
    oa = _diff_attn(proj, proj_c, attn_lambda[0], attn_subln_g[0], lam_init, tq=512, sub=128)
    ow = _win_attn(proj, proj_c, attn_sink[0], tq=256)
    w_out = attn_w_out[0].astype(BF16)
    x1, h2, lgt = _proj_ln([oa.reshape(t, DIFF_VW), ow.reshape(t, WIN_Q_W)], [w_out[:DIFF_VW], w_out[DIFF_VW:]],
                           x2d, g1[:b], ln_g[0, 0], ln_b[0, 0], sc2[:b], sh2[:b], router_wt, rows_per_mod=n, tm=512,
                           name="attn_out_proj_ln")
    x2 = _moe(h2, lgt, x1, g2[:b], ln_g[0, 1], ln_b[0, 1], router_bias, exp_w_gate, exp_w_up, exp_w_down, layer=0,
              rows_per_mod=n)

    sh1, sc1, g1, sh2, sc2, g2 = jnp.split(mods[1], 6, axis=-1)
    u = _modmm(x2, sc1[:b], sh1[:b], hy_w_in[0].astype(BF16), rows_per_mod=n, tm=512, name="hyena_in_proj")
    fb, tc = 256, 512
    fwd_np, inv_np = _dft_tables(n, fb)
    fwd_tab, inv_tab = jnp.asarray(fwd_np).astype(BF16), jnp.asarray(inv_np).astype(BF16)
    zp, decay = _filter_tables(n)
    w_in_p = jnp.zeros((FILTER_HID, FILTER_HID), F32).at[:FILTER_EMB].set(hy_ffn_w_in[0])
    hid = _filter_mlp(jnp.asarray(zp), w_in_p, hy_ffn_w_hid[0], hy_ffn_b[0], hy_sin_freq[0])
    hr, gg, nyq = _filter_spec(hid, hy_ffn_w_out[0].reshape(FILTER_HID, 2, 2, d).transpose(1, 2, 0, 3), jnp.asarray(decay), fwd_tab, fb, tc, kps=4)
    z2 = _hyena_conv(u.reshape(b, n, 3 * d), hy_conv_w[0], hy_conv_b[0].reshape(1, 3 * d), fwd_tab, inv_tab,
                     hr, gg, nyq, hy_skip[0], fb, tc, kps=2)
    x3, h2, lgt = _proj_ln([z2.reshape(t, d)], [hy_w_out[0].astype(BF16)], x2, g1[:b], ln_g[1, 0], ln_b[1, 0],
                           sc2[:b], sh2[:b], router_wt, rows_per_mod=n, tm=512, name="hyena_out_proj_ln")
    x4 = _moe(h2, lgt, x3, g2[:b], ln_g[1, 1], ln_b[1, 1], router_bias, exp_w_gate, exp_w_up, exp_w_down, layer=1,
              rows_per_mod=n)
    return x4.reshape(b, n, d)
```

```python
import functools
import math

import jax
import jax.numpy as jnp
import numpy as np
from jax import lax
from jax.experimental import pallas as pl
from jax.experimental.pallas import tpu as pltpu

F32 = jnp.float32
BF16 = jnp.bfloat16
I32 = jnp.int32

D_MODEL = 1024
DEPTH = 2
GRID_W = 64
HEAD_DIM = 64
DIFF_HEADS = 4
WIN_Q_HEADS = 8
WIN_KV_HEADS = 2
WINDOW = 128
WIN_BLOCK = 128
ROPE_BASE = 10000.0
DIFF_QK_W = DIFF_HEADS * 2 * HEAD_DIM
DIFF_VW = DIFF_HEADS * 2 * HEAD_DIM
WIN_Q_W = WIN_Q_HEADS * HEAD_DIM
WIN_KV_W = WIN_KV_HEADS * HEAD_DIM
Q_W = DIFF_QK_W + WIN_Q_W
KV_W = DIFF_QK_W + DIFF_VW + 2 * WIN_KV_W
ATTN_PROJ_W = Q_W + KV_W
FILTER_EMB = 33
FILTER_HID = 64
DECAY_TARGET = 1e-2
FAST_DECAY_PCT = 0.3
SLOW_DECAY_PCT = 1.5
N_EXPERTS = 16
N_GROUPS = 4
EXPERTS_PER_GROUP = N_EXPERTS // N_GROUPS
EXPERT_FF = 1024
LN_EPS = 1e-5
DEEPNORM_ALPHA = (2 * DEPTH) ** 0.25
NEG_INF = -1e30
LOG2_E = 1.4426950408889634

LANES = 128
MOD_ROWS = 16
VMEM_LIMIT = 60 * 1024 * 1024
MOE_BM = 512
MOE_TILE = 512
ROW_ALIGN = 8
NT_DIMS = (((1,), (1,)), ((), ()))


def _cparams(n_axes):
    return pltpu.CompilerParams(dimension_semantics=("arbitrary",) * n_axes, vmem_limit_bytes=VMEM_LIMIT)


@functools.lru_cache(maxsize=None)
def _rope_tables(n):
    rows = n // GRID_W
    r, col = np.meshgrid(np.arange(rows, dtype=np.float32), np.arange(GRID_W, dtype=np.float32), indexing="ij")
    axis_dim = HEAD_DIM // 2
    inv_freq = (ROPE_BASE ** (-np.arange(0, axis_dim, 2, dtype=np.float32) / axis_dim)).astype(np.float32)
    ang = np.concatenate([r.reshape(-1, 1) * inv_freq, col.reshape(-1, 1) * inv_freq], -1)
    ang = np.concatenate([ang, ang], -1).astype(np.float32)
    cos, sin = np.cos(ang), np.sin(ang)
    half = np.arange(HEAD_DIM) < HEAD_DIM // 2
    sin_signed = np.where(half[None, :], -sin, sin)
    reps = LANES // HEAD_DIM
    return (np.tile(cos, (1, reps)).astype(np.float32), np.tile(sin_signed, (1, reps)).astype(np.float32))


@functools.lru_cache(maxsize=None)
def _dft_tables(n, fb):
    big = 2 * n
    k = np.arange(n, dtype=np.int64)[:, None]
    t = np.arange(n, dtype=np.int64)[None, :]
    ang = ((k * t) % big).astype(np.float64) * (2.0 * math.pi / big)
    c, s = np.cos(ang), np.sin(ang)
    alt = (1 - 2 * (np.arange(n) & 1)).astype(np.float64)
    s_f = s.copy()
    s_f[0, :] = alt
    kb = n // fb
    fwd = np.concatenate([c.reshape(kb, fb, n), s_f.reshape(kb, fb, n)], axis=1)
    ci = c.T * (2.0 / big)
    ci[:, 0] = 1.0 / big
    si = s.T * (2.0 / big)
    si[:, 0] = alt / big
    inv = np.concatenate([ci.reshape(n, kb, fb).transpose(1, 0, 2), si.reshape(n, kb, fb).transpose(1, 0, 2)], axis=2)
    return fwd.astype(np.float32), inv.astype(np.float32)


@functools.lru_cache(maxsize=None)
def _filter_tables(n):
    t = np.linspace(0.0, 1.0, n, dtype=np.float32)[:, None]
    bands = (FILTER_EMB - 1) // 2
    w = (2.0 * math.pi * np.arange(n, dtype=np.float32)[:, None] / n).astype(np.float32)
    fr = np.linspace(1e-4, bands - 1, bands, dtype=np.float32)[None, :]
    z = np.concatenate([t, np.cos(fr * w), -np.sin(fr * w)], -1).astype(np.float32)
    zp = np.zeros((n, FILTER_HID), np.float32)
    zp[:, :FILTER_EMB] = z
    deltas = np.abs(np.linspace(math.log(DECAY_TARGET) / SLOW_DECAY_PCT, math.log(DECAY_TARGET) / FAST_DECAY_PCT,
                                D_MODEL, dtype=np.float32))
    decay = np.exp(-t * deltas[None, :]).astype(np.float32)
    return zp, decay


def _mods_kernel(c_ref, w_ref, b_ref, o_ref):
    c = c_ref[...]
    a = (c * jax.nn.sigmoid(c)).astype(BF16)
    o_ref[...] = jnp.dot(a, w_ref[...].astype(BF16), preferred_element_type=F32) + b_ref[...]


def _mods(c_rows, mod_w, mod_b):
    d = D_MODEL
    tn = 1536
    return pl.pallas_call(
        _mods_kernel,
        grid=(DEPTH, 6 * d // tn),
        in_specs=[pl.BlockSpec((MOD_ROWS, d), lambda l, j: (0, 0)),
                  pl.BlockSpec((None, d, tn), lambda l, j: (l, 0, j)),
                  pl.BlockSpec((None, 1, tn), lambda l, j: (l, 0, j))],
        out_specs=pl.BlockSpec((None, MOD_ROWS, tn), lambda l, j: (l, 0, j)),
        out_shape=jax.ShapeDtypeStruct((DEPTH, MOD_ROWS, 6 * d), F32),
        compiler_params=_cparams(2), name="mods",
    )(c_rows, mod_w, mod_b.reshape(DEPTH, 1, 6 * d))


def _modmm_kernel(*refs, n_groups, rope_groups, scaled_groups, chunk):
    if rope_groups:
        x_ref, sc_ref, sh_ref, w_ref, cos_ref, sin_ref, o_ref = refs
        cos, sin = cos_ref[...], sin_ref[...]
        lane = lax.broadcasted_iota(I32, (1, LANES), 1)
        first_half = (lane % HEAD_DIM) < HEAD_DIM // 2
    else:
        x_ref, sc_ref, sh_ref, w_ref, o_ref = refs
    h = (x_ref[...] * (1.0 + sc_ref[...]) + sh_ref[...]).astype(BF16)
    scales = dict(scaled_groups)
    gpc = chunk // LANES
    for c in range(n_groups // gpc):
        acc = jnp.dot(h, w_ref[:, c * chunk:(c + 1) * chunk], preferred_element_type=F32)
        for j in range(gpc):
            g = c * gpc + j
            blk = acc[:, j * LANES:(j + 1) * LANES]
            if g in rope_groups:
                rot = jnp.where(first_half, pltpu.roll(blk, LANES - HEAD_DIM // 2, 1), pltpu.roll(blk, HEAD_DIM // 2, 1))
                blk = blk * cos + rot * sin
                if g in scales:
                    blk = blk * scales[g]
            o_ref[:, g * LANES:(g + 1) * LANES] = blk.astype(o_ref.dtype)


def _modmm(x2d, sc, sh, w, rows_per_mod, tm, rope=None, name="modmm"):
    t, d = x2d.shape
    n = w.shape[1]
    tiles_per_mod = rows_per_mod // tm
    nmod = sc.shape[0]
    in_specs = [pl.BlockSpec((tm, d), lambda i: (i, 0)),
                pl.BlockSpec((None, 1, d), lambda i: (i // tiles_per_mod, 0, 0)),
                pl.BlockSpec((None, 1, d), lambda i: (i // tiles_per_mod, 0, 0)),
                pl.BlockSpec((d, n), lambda i: (0, 0))]
    args = [x2d, sc.reshape(nmod, 1, d), sh.reshape(nmod, 1, d), w]
    rope_groups, scaled_groups = (), ()
    if rope is not None:
        cos, sin, rope_groups, scaled_groups = rope
        in_specs += [pl.BlockSpec((tm, LANES), lambda i: (i % tiles_per_mod, 0)),
                     pl.BlockSpec((tm, LANES), lambda i: (i % tiles_per_mod, 0))]
        args += [cos, sin]
    kern = functools.partial(_modmm_kernel, n_groups=n // LANES, rope_groups=tuple(rope_groups),
                             scaled_groups=tuple(scaled_groups), chunk=256)
    return pl.pallas_call(
        kern, grid=(t // tm,), in_specs=in_specs,
        out_specs=pl.BlockSpec((tm, n), lambda i: (i, 0)),
        out_shape=jax.ShapeDtypeStruct((t, n), BF16),
        compiler_params=_cparams(1), name=name,
    )(*args)


def _diff_attn_kernel(q_ref, k_ref, v_ref, kc_ref, vc_ref, lam_ref, g_ref, o_ref, *, lam_init, sub):
    lane = lax.broadcasted_iota(I32, (1, LANES), 1)
    k, kc = k_ref[...], kc_ref[...]
    v, vc = v_ref[...], vc_ref[...]

    def scores(r0):
        q = q_ref[r0:r0 + sub, :]
        zero = jnp.zeros_like(q)
        out = []
        for qm in (jnp.where(lane < HEAD_DIM, q, zero), jnp.where(lane >= HEAD_DIM, q, zero)):
            out.append((lax.dot_general(qm, k, NT_DIMS, preferred_element_type=F32),
                        lax.dot_general(qm, kc, NT_DIMS, preferred_element_type=F32)))
        return out

    def probs(sl, sc):
        m = jnp.maximum(jnp.max(sl, -1, keepdims=True), jnp.max(sc, -1, keepdims=True))
        pl_, pc = jnp.exp2(sl - m), jnp.exp2(sc - m)
        den = jnp.sum(pl_, -1, keepdims=True) + jnp.sum(pc, -1, keepdims=True)
        return pl_, pc, 1.0 / den

    lv = lam_ref[...]
    lam = (jnp.exp(jnp.sum(lv[0:1] * lv[1:2], keepdims=True)) - jnp.exp(jnp.sum(lv[2:3] * lv[3:4], keepdims=True))
           + lam_init)
    starts = list(range(0, q_ref.shape[0], sub))
    nxt = scores(starts[0])
    for i, r0 in enumerate(starts):
        cur = nxt
        if i + 1 < len(starts):
            nxt = scores(starts[i + 1])
        p1l, p1c, r1 = probs(*cur[0])
        p2l, p2c, r2 = probs(*cur[1])
        w2 = lam * r2
        al = (p1l * r1 - p2l * w2).astype(BF16)
        ac = (p1c * r1 - p2c * w2).astype(BF16)
        o = jnp.dot(al, v, preferred_element_type=F32) + jnp.dot(ac, vc, preferred_element_type=F32)
        ms = jnp.mean(o * o, -1, keepdims=True)
        o_ref[r0:r0 + sub, :] = (o * lax.rsqrt(ms + LN_EPS) * g_ref[...] * (1.0 - lam_init)).astype(o_ref.dtype)


def _diff_attn(proj, proj_c, lam_vec, subln_g, lam_init, tq, sub):
    b, n, _ = proj.shape
    nc = proj_c.shape[1]
    kcol = Q_W // LANES
    vcol = (Q_W + DIFF_QK_W) // LANES
    vccol = DIFF_QK_W // LANES
    kern = functools.partial(_diff_attn_kernel, lam_init=lam_init, sub=sub)
    return pl.pallas_call(
        kern, grid=(b, DIFF_HEADS, n // tq),
        in_specs=[pl.BlockSpec((None, tq, LANES), lambda bi, h, i: (bi, i, h)),
                  pl.BlockSpec((None, n, LANES), lambda bi, h, i: (bi, 0, kcol + h)),
                  pl.BlockSpec((None, n, LANES), lambda bi, h, i: (bi, 0, vcol + h)),
                  pl.BlockSpec((None, nc, LANES), lambda bi, h, i: (bi, 0, h)),
                  pl.BlockSpec((None, nc, LANES), lambda bi, h, i: (bi, 0, vccol + h)),
                  pl.BlockSpec((4, HEAD_DIM), lambda bi, h, i: (0, 0)),
                  pl.BlockSpec((1, LANES), lambda bi, h, i: (0, 0))],
        out_specs=pl.BlockSpec((None, tq, LANES), lambda bi, h, i: (bi, i, h)),
        out_shape=jax.ShapeDtypeStruct((b, n, DIFF_VW), BF16),
        compiler_params=_cparams(3), name="diff_attn",
    )(proj, proj, proj, proj_c, proj_c, lam_vec, subln_g.reshape(1, LANES))


def _win_attn_kernel(sink_ref, q_ref, k_ref, v_ref, kc_ref, vc_ref, o_ref, *, seq, tq):
    n = pl.program_id(1)
    gq = WIN_Q_HEADS // WIN_KV_HEADS
    kw = tq + 2 * WINDOW
    start = pl.multiple_of(jnp.clip(n * tq - WINDOW, 0, seq - kw), WINDOW)
    k_win, v_win = k_ref[pl.ds(start, kw), :], v_ref[pl.ds(start, kw), :]
    kc, vc = kc_ref[...], vc_ref[...]
    lane = lax.broadcasted_iota(I32, (1, LANES), 1)
    row = lax.broadcasted_iota(I32, (gq * tq, 1), 0)
    q_abs = n * tq + row % tq
    k_abs = start + lax.broadcasted_iota(I32, (1, kw), 1)
    allowed = jnp.abs(q_abs - k_abs) <= WINDOW
    head = row // tq
    q = q_ref[...].astype(F32)
    for g in range(WIN_KV_HEADS):
        in_g = (lane // HEAD_DIM) == g
        parts = []
        for j in range(gq):
            hq = g * gq + j
            x = q[:, (hq // 2) * LANES:(hq // 2 + 1) * LANES]
            if hq % 2 != g:
                x = pltpu.roll(x, HEAD_DIM, 1)
            parts.append(jnp.where(in_g, x, 0.0))
        qs = jnp.concatenate(parts, 0).astype(BF16)
        s_loc = lax.dot_general(qs, k_win, NT_DIMS, preferred_element_type=F32)
        s_ctx = lax.dot_general(qs, kc, NT_DIMS, preferred_element_type=F32)
        s_loc = jnp.where(allowed, s_loc, NEG_INF)
        sk = jnp.zeros((gq * tq, 1), F32)
        for j in range(gq):
            sk = jnp.where(head == j, sink_ref[g * gq + j] * LOG2_E, sk)
        m = jnp.maximum(jnp.maximum(jnp.max(s_loc, -1, keepdims=True), jnp.max(s_ctx, -1, keepdims=True)), sk)
        p_loc, p_ctx = jnp.exp2(s_loc - m), jnp.exp2(s_ctx - m)
        den = jnp.sum(p_loc, -1, keepdims=True) + jnp.sum(p_ctx, -1, keepdims=True) + jnp.exp2(sk - m)
        o = (jnp.dot(p_loc.astype(BF16), v_win, preferred_element_type=F32)
             + jnp.dot(p_ctx.astype(BF16), vc, preferred_element_type=F32)) * (1.0 / den)
        for cb in range(gq // 2):
            pair = []
            for half in range(2):
                piece = o[(2 * cb + half) * tq:(2 * cb + half + 1) * tq]
                pair.append(piece if half == g else pltpu.roll(piece, HEAD_DIM, 1))
            col = (g * gq // 2 + cb) * LANES
            o_ref[:, col:col + LANES] = jnp.where(lane < HEAD_DIM, pair[0], pair[1]).astype(o_ref.dtype)


def _win_attn(proj, proj_c, sink, tq):
    b, n, _ = proj.shape
    nc = proj_c.shape[1]
    qcol = DIFF_QK_W // WIN_Q_W
    kcol = (Q_W + DIFF_QK_W + DIFF_VW) // LANES
    kccol = (DIFF_QK_W + DIFF_VW) // LANES
    kern = functools.partial(_win_attn_kernel, seq=n, tq=tq)
    return pl.pallas_call(
        kern, grid=(b, n // tq),
        in_specs=[pl.BlockSpec(memory_space=pltpu.SMEM),
                  pl.BlockSpec((None, tq, WIN_Q_W), lambda bi, i: (bi, i, qcol)),
                  pl.BlockSpec((None, n, LANES), lambda bi, i: (bi, 0, kcol)),
                  pl.BlockSpec((None, n, LANES), lambda bi, i: (bi, 0, kcol + 1)),
                  pl.BlockSpec((None, nc, LANES), lambda bi, i: (bi, 0, kccol)),
                  pl.BlockSpec((None, nc, LANES), lambda bi, i: (bi, 0, kccol + 1))],
        out_specs=pl.BlockSpec((None, tq, WIN_Q_W), lambda bi, i: (bi, i, 0)),
        out_shape=jax.ShapeDtypeStruct((b, n, WIN_Q_W), BF16),
        compiler_params=_cparams(2), name="win_attn",
    )(sink, proj, proj, proj, proj_c, proj_c)


def _layer_norm(r, g, b):
    mu = jnp.mean(r, -1, keepdims=True)
    dlt = r - mu
    var = jnp.mean(dlt * dlt, -1, keepdims=True)
    return dlt * lax.rsqrt(var + LN_EPS) * g + b


def _proj_ln_kernel(*refs, n_in, sub):
    a_refs = refs[:n_in]
    w_refs = refs[n_in:2 * n_in]
    x_ref, gate_ref, lng_ref, lnb_ref, sc_ref, sh_ref, rw_ref, x1_ref, h2_ref, lg_ref = refs[2 * n_in:]

    def split(v):
        hi = v.astype(BF16)
        return hi, (v - hi.astype(F32)).astype(BF16)

    def project(r0):
        y = jnp.dot(a_refs[0][r0:r0 + sub, :], w_refs[0][...], preferred_element_type=F32)
        for a_ref, w_ref in zip(a_refs[1:], w_refs[1:]):
            y = y + jnp.dot(a_ref[r0:r0 + sub, :], w_ref[...], preferred_element_type=F32)
        return y

    rw_hi, rw_lo = split(rw_ref[...])
    rw_both = jnp.concatenate([rw_hi, rw_lo], 1)
    starts = list(range(0, x_ref.shape[0], sub))
    nxt = project(starts[0])
    for i, r0 in enumerate(starts):
        y = nxt
        if i + 1 < len(starts):
            nxt = project(starts[i + 1])
        xn = _layer_norm(DEEPNORM_ALPHA * x_ref[r0:r0 + sub, :] + gate_ref[...] * y, lng_ref[...], lnb_ref[...])
        x1_ref[r0:r0 + sub, :] = xn
        h2 = xn * (1.0 + sc_ref[...]) + sh_ref[...]
        h2_ref[r0:r0 + sub, :] = h2.astype(h2_ref.dtype)
        h_hi, h_lo = split(h2)
        hh = jnp.dot(h_hi, rw_both, preferred_element_type=F32)
        lg = hh[:, :LANES] + hh[:, LANES:] + jnp.dot(h_lo, rw_hi, preferred_element_type=F32)
        lg_ref[:, r0:r0 + sub] = lg.T[:N_EXPERTS, :]


def _proj_ln(acts, ws, x2d, gate, ln_g, ln_b, sc, sh, router_wt, rows_per_mod, tm, name):
    t, d = x2d.shape
    n_in = len(acts)
    tiles_per_mod = rows_per_mod // tm
    nmod = gate.shape[0]
    row = lambda i: (i, 0)
    full = lambda i: (0, 0)
    mod = lambda i: (i // tiles_per_mod, 0, 0)
    in_specs = ([pl.BlockSpec((tm, a.shape[1]), row) for a in acts]
                + [pl.BlockSpec(w.shape, full) for w in ws]
                + [pl.BlockSpec((tm, d), row), pl.BlockSpec((None, 1, d), mod),
                   pl.BlockSpec((1, d), full), pl.BlockSpec((1, d), full),
                   pl.BlockSpec((None, 1, d), mod), pl.BlockSpec((None, 1, d), mod),
                   pl.BlockSpec((d, LANES), full)])
    return pl.pallas_call(
        functools.partial(_proj_ln_kernel, n_in=n_in, sub=LANES), grid=(t // tm,), in_specs=in_specs,
        out_specs=[pl.BlockSpec((tm, d), row), pl.BlockSpec((tm, d), row),
                   pl.BlockSpec((N_EXPERTS, tm), lambda i: (0, i))],
        out_shape=[jax.ShapeDtypeStruct((t, d), F32), jax.ShapeDtypeStruct((t, d), BF16),
                   jax.ShapeDtypeStruct((N_EXPERTS, t), F32)],
        compiler_params=_cparams(1), name=name,
    )(*acts, *ws, x2d, gate.reshape(nmod, 1, d), ln_g.reshape(1, d), ln_b.reshape(1, d),
      sc.reshape(nmod, 1, d), sh.reshape(nmod, 1, d), router_wt)


def _first_argmax(vals):
    idx = jnp.zeros(vals[0].shape, I32)
    best = vals[0]
    for j in range(1, len(vals)):
        upd = vals[j] > best
        idx = jnp.where(upd, j, idx)
        best = jnp.where(upd, vals[j], best)
    return idx, best


def _route_kernel(lg_ref, bias_ref, pos_ref, w_ref, tab_ref, cnt_ref, carry_ref, off_ref, *, tr):
    @pl.when(pl.program_id(0) == 0)
    def _():
        carry_ref[...] = jnp.zeros_like(carry_ref)

    lg = lg_ref[...]
    ex = jnp.exp(lg - jnp.max(lg, 0, keepdims=True))
    scores = ex / jnp.sum(ex, 0, keepdims=True)
    sel = scores + bias_ref[...]
    rows = [sel[e:e + 1] for e in range(N_EXPERTS)]
    group_scores = []
    for g in range(N_GROUPS):
        r = rows[g * EXPERTS_PER_GROUP:(g + 1) * EXPERTS_PER_GROUP]
        best = None
        for i in range(EXPERTS_PER_GROUP):
            for j in range(i + 1, EXPERTS_PER_GROUP):
                s = r[i] + r[j]
                best = s if best is None else jnp.maximum(best, s)
        group_scores.append(best)
    grp, _ = _first_argmax(group_scores)
    vals = []
    for j in range(EXPERTS_PER_GROUP):
        v = rows[(N_GROUPS - 1) * EXPERTS_PER_GROUP + j]
        for g in range(N_GROUPS - 2, -1, -1):
            v = jnp.where(grp == g, rows[g * EXPERTS_PER_GROUP + j], v)
        vals.append(v)
    i0, _ = _first_argmax(vals)
    i1, _ = _first_argmax([jnp.where(i0 == j, -jnp.inf, vals[j]) for j in range(EXPERTS_PER_GROUP)])
    e0 = grp * EXPERTS_PER_GROUP + i0
    e1 = grp * EXPERTS_PER_GROUP + i1
    eid = lax.broadcasted_iota(I32, (N_EXPERTS, 1), 0)
    oh0, oh1 = eid == e0, eid == e1
    s0 = jnp.sum(jnp.where(oh0, scores, 0.0), 0, keepdims=True)
    s1 = jnp.sum(jnp.where(oh1, scores, 0.0), 0, keepdims=True)
    den = s0 + s1
    member = jnp.where(oh0 | oh1, 1.0, 0.0)
    before = lax.broadcasted_iota(I32, (tr, tr), 0) < lax.broadcasted_iota(I32, (tr, tr), 1)
    upper = jnp.where(before, 1.0, 0.0).astype(BF16)
    cnt = jnp.dot(member.astype(BF16), upper, preferred_element_type=F32)
    run = jnp.sum(member, 1, keepdims=True)
    run = jnp.floor((run + (ROW_ALIGN - 1)) * (1.0 / ROW_ALIGN)) * ROW_ALIGN
    run = jnp.broadcast_to(run, (N_EXPERTS, LANES))
    acc = jnp.zeros((1, LANES), F32)
    for e in range(N_EXPERTS):
        off_ref[e:e + 1, :] = acc
        acc = acc + run[e:e + 1]
    off = off_ref[...]
    at = off[:, 0:1] + cnt
    pos_ref[0:1, :] = jnp.sum(jnp.where(oh0, at, 0.0), 0, keepdims=True).astype(I32)
    pos_ref[1:2, :] = jnp.sum(jnp.where(oh1, at, 0.0), 0, keepdims=True).astype(I32)
    w_ref[0:1, :] = s0 / den
    w_ref[1:2, :] = s1 / den
    tab_ref[0] = off.astype(I32)
    tab_ref[1] = run.astype(I32)
    tab_ref[2] = carry_ref[...].astype(I32)
    carry_ref[...] = carry_ref[...] + run
    cnt_ref[...] = carry_ref[...]


def _route(logits_t, router_bias, tr):
    t = logits_t.shape[1]
    tok = lambda i: (0, i)
    return pl.pallas_call(
        functools.partial(_route_kernel, tr=tr), grid=(t // tr,),
        in_specs=[pl.BlockSpec((N_EXPERTS, tr), tok), pl.BlockSpec((N_EXPERTS, 1), lambda i: (0, 0))],
        out_specs=[pl.BlockSpec((2, tr), tok), pl.BlockSpec((2, tr), tok),
                   pl.BlockSpec((None, 3, N_EXPERTS, LANES), lambda i: (i, 0, 0, 0)),
                   pl.BlockSpec((N_EXPERTS, LANES), lambda i: (0, 0))],
        out_shape=[jax.ShapeDtypeStruct((2, t), I32), jax.ShapeDtypeStruct((2, t), F32),
                   jax.ShapeDtypeStruct((t // tr, 3, N_EXPERTS, LANES), I32),
                   jax.ShapeDtypeStruct((N_EXPERTS, LANES), F32)],
        scratch_shapes=[pltpu.VMEM((N_EXPERTS, LANES), F32), pltpu.VMEM((N_EXPERTS, LANES), F32)],
        compiler_params=_cparams(1), name="route",
    )(logits_t, router_bias.reshape(N_EXPERTS, 1))


def _run_copies(n, max_rows, make_copy, wait):
    sz = max_rows
    while sz >= ROW_ALIGN:
        start = (n // (2 * sz)) * (2 * sz)

        @pl.when((n & sz) != 0)
        def _(start=start, sz=sz):
            cp = make_copy(start, sz)
            cp.wait() if wait else cp.start()

        sz //= 2


def _wait_rows(total, max_rows, make_copy):
    sz = max_rows
    while sz >= ROW_ALIGN:
        @pl.when((total & sz) != 0)
        def _(sz=sz):
            make_copy(sz).wait()

        sz //= 2


def _rows(ref, start, size):
    return ref.at[pl.ds(pl.multiple_of(start, ROW_ALIGN), size)]


def _dispatch_kernel(tab_ref, h_ref, pos_ref, w_ref, xs_ref, buf, zbuf, sem, *, tt, n_tiles, bm):
    i = pl.program_id(0)
    d = h_ref.shape[1]
    sb = buf.shape[1]
    ne = N_EXPERTS
    fill = 3 * n_tiles * ne

    @pl.when(i == 0)
    def _():
        zbuf[...] = jnp.zeros_like(zbuf)
        for wait in (False, True):
            for e in range(ne):
                dst, n = tab_ref[fill + e], tab_ref[fill + ne + e]
                _run_copies(n, bm // 2, lambda s, z: pltpu.make_async_copy(
                    zbuf.at[pl.ds(0, z)], _rows(xs_ref, dst + s, z), sem.at[2]), wait)

        def zero_block(j, carry):
            for half in range(2):
                cp = pltpu.make_async_copy(zbuf, _rows(xs_ref, j * bm + half * (bm // 2), bm // 2), sem.at[2])
                cp.start()
                cp.wait()
            return carry

        lax.fori_loop(tab_ref[fill + 2 * ne], xs_ref.shape[0] // bm, zero_block, 0)

    def tile_copies(tile, slot, wait):
        for e in range(ne):
            off = tab_ref[tile * ne + e]
            n = tab_ref[(n_tiles + tile) * ne + e]
            dst = tab_ref[(2 * n_tiles + tile) * ne + e]
            _run_copies(n, tt, lambda s, z: pltpu.make_async_copy(
                _rows(buf.at[slot], off + s, z), _rows(xs_ref, dst + s, z), sem.at[slot]), wait)

    slot = i % 2
    pos, w = pos_ref[...], w_ref[...]
    hb = h_ref[...].astype(BF16)
    for r0 in range(0, sb, LANES):
        srow = r0 + lax.broadcasted_iota(I32, (LANES, 1), 0)
        m0, m1 = srow == pos[0:1], srow == pos[1:2]
        perm = (jnp.where(m0, 1.0, 0.0) + jnp.where(m1, 1.0, 0.0)).astype(BF16)
        buf[slot, r0:r0 + LANES, :d] = jnp.dot(perm, hb, preferred_element_type=F32)
        wrow = jnp.sum(jnp.where(m0, w[0:1], 0.0) + jnp.where(m1, w[1:2], 0.0), 1, keepdims=True)
        buf[slot, r0:r0 + LANES, d:] = jnp.broadcast_to(wrow, (LANES, LANES))

    def wait_tile(tile, slot):
        total = tab_ref[fill + 2 * ne + 1 + tile]
        _wait_rows(total, tt * 2, lambda z: pltpu.make_async_copy(
            buf.at[slot, pl.ds(0, z)], xs_ref.at[pl.ds(0, z)], sem.at[slot]))

    @pl.when(i > 0)
    def _():
        wait_tile(i - 1, 1 - slot)

    tile_copies(i, slot, False)

    @pl.when(i == n_tiles - 1)
    def _():
        wait_tile(i, slot)


def _dispatch(tab, h2, pos, wts, n_slots, tt, sb, bm):
    t, d = h2.shape
    n_tiles = t // tt
    return pl.pallas_call(
        functools.partial(_dispatch_kernel, tt=tt, n_tiles=n_tiles, bm=bm),
        grid_spec=pltpu.PrefetchScalarGridSpec(
            num_scalar_prefetch=1, grid=(n_tiles,),
            in_specs=[pl.BlockSpec((tt, d), lambda i, s: (i, 0)), pl.BlockSpec((2, tt), lambda i, s: (0, i)),
                      pl.BlockSpec((2, tt), lambda i, s: (0, i))],
            out_specs=pl.BlockSpec(memory_space=pl.ANY),
            scratch_shapes=[pltpu.VMEM((2, sb, d + LANES), F32), pltpu.VMEM((bm // 2, d + LANES), F32),
                            pltpu.SemaphoreType.DMA((3,))]),
        out_shape=jax.ShapeDtypeStruct((n_slots, d + LANES), F32),
        compiler_params=_cparams(1), name="moe_dispatch",
    )(tab, h2, pos, wts)


def _ffn_kernel(blk_e_ref, blk_rows_ref, next_e_ref, nused_ref, x_ref, wg_hbm, wu_hbm, wd_hbm, o_ref,
                stage, wg_bf, wu_bf, wd_bf, sem, *, layer, parts):
    del nused_ref
    i = pl.program_id(0)
    e = blk_e_ref[i]
    new_expert = (i == 0) | (e != blk_e_ref[jnp.maximum(i - 1, 0)])

    def fetch(expert, wait):
        for j, src in enumerate((wg_hbm, wu_hbm, wd_hbm)):
            cp = pltpu.make_async_copy(src.at[layer, expert], stage.at[j], sem.at[j])
            cp.wait() if wait else cp.start()

    @pl.when(i == 0)
    def _():
        fetch(e, False)

    @pl.when(new_expert)
    def _():
        fetch(e, True)
        wg_bf[...] = stage[0].astype(BF16)
        wu_bf[...] = stage[1].astype(BF16)
        wd_bf[...] = stage[2].astype(BF16)

        @pl.when(next_e_ref[i] >= 0)
        def _():
            fetch(next_e_ref[i], False)

    d = wg_bf.shape[0]
    rows = blk_rows_ref[i]
    part = x_ref.shape[0] // parts
    spans = [slice(p * part, (p + 1) * part) for p in range(parts)]

    def gate_up(rs):
        x = x_ref[rs, :d].astype(BF16)
        return (jnp.dot(x, wg_bf[...], preferred_element_type=F32), jnp.dot(x, wu_bf[...], preferred_element_type=F32))

    def finish(rs, gate, up):
        act = (gate * jax.nn.sigmoid(gate) * up).astype(BF16)
        o_ref[rs, :] = jnp.dot(act, wd_bf[...], preferred_element_type=F32) * x_ref[rs, d:d + 1]

    all_parts = rows > (parts - 1) * part

    @pl.when(all_parts)
    def _():
        nxt = gate_up(spans[0])
        for p, rs in enumerate(spans):
            cur = nxt
            if p + 1 < parts:
                nxt = gate_up(spans[p + 1])
            finish(rs, *cur)

    for p, rs in enumerate(spans):
        @pl.when(jnp.logical_not(all_parts) & (rows > p * part))
        def _(rs=rs):
            finish(rs, *gate_up(rs))

        @pl.when(rows <= p * part)
        def _(rs=rs):
            o_ref[rs, :] = jnp.zeros((part, d), F32)


def _ffn(blk_e, blk_rows, next_e, n_used, xs, wg, wu, wd, layer, bm, parts):
    n_slots, xw = xs.shape
    d, ff = wg.shape[2:]
    assert d == ff
    xrow = lambda i, be, br, ne, nu: (jnp.minimum(i, nu[0] - 1), 0)
    hbm = pl.BlockSpec(memory_space=pl.ANY)
    return pl.pallas_call(
        functools.partial(_ffn_kernel, layer=layer, parts=parts),
        grid_spec=pltpu.PrefetchScalarGridSpec(
            num_scalar_prefetch=4, grid=(n_slots // bm,),
            in_specs=[pl.BlockSpec((bm, xw), xrow), hbm, hbm, hbm],
            out_specs=pl.BlockSpec((bm, d), lambda i, be, br, ne, nu: (i, 0)),
            scratch_shapes=[pltpu.VMEM((3, d, ff), F32), pltpu.VMEM((d, ff), BF16), pltpu.VMEM((d, ff), BF16),
                            pltpu.VMEM((ff, d), BF16), pltpu.SemaphoreType.DMA((3,))]),
        out_shape=jax.ShapeDtypeStruct((n_slots, d), F32),
        compiler_params=_cparams(1), name="moe_ffn",
    )(blk_e, blk_rows, next_e, n_used, xs, wg, wu, wd)


def _combine_kernel(tab_ref, ys_ref, pos_ref, x1_ref, gate_ref, lng_ref, lnb_ref, o_ref, ybuf, sem, *, tt, n_tiles):
    i = pl.program_id(0)
    sb = ybuf.shape[1]
    ne = N_EXPERTS
    slot = i % 2

    def tile_copies(tile, slot, wait):
        for e in range(ne):
            off = tab_ref[tile * ne + e]
            n = tab_ref[(n_tiles + tile) * ne + e]
            src = tab_ref[(2 * n_tiles + tile) * ne + e]
            _run_copies(n, tt, lambda s, z: pltpu.make_async_copy(
                _rows(ys_ref, src + s, z), _rows(ybuf.at[slot], off + s, z), sem.at[slot]), wait)

    @pl.when(i == 0)
    def _():
        ybuf[...] = jnp.zeros_like(ybuf)
        tile_copies(0, 0, False)

    @pl.when(i + 1 < n_tiles)
    def _():
        tile_copies(i + 1, 1 - slot, False)

    total = tab_ref[3 * n_tiles * ne + 2 * ne + 1 + i]
    _wait_rows(total, tt * 2, lambda z: pltpu.make_async_copy(
        ys_ref.at[pl.ds(0, z)], ybuf.at[slot, pl.ds(0, z)], sem.at[slot]))
    scol = lax.broadcasted_iota(I32, (1, sb), 1)
    yb = ybuf[slot].astype(BF16)

    def moe_rows(r0):
        pos = pos_ref[r0:r0 + LANES, :]
        unperm = (jnp.where(scol == pos[:, 0:1], 1.0, 0.0) + jnp.where(scol == pos[:, 1:2], 1.0, 0.0)).astype(BF16)
        return jnp.dot(unperm, yb, preferred_element_type=F32)

    starts = list(range(0, tt, LANES))
    nxt = moe_rows(starts[0])
    for j, r0 in enumerate(starts):
        f = nxt
        if j + 1 < len(starts):
            nxt = moe_rows(starts[j + 1])
        o_ref[r0:r0 + LANES, :] = _layer_norm(DEEPNORM_ALPHA * x1_ref[r0:r0 + LANES, :] + gate_ref[...] * f,
                                              lng_ref[...], lnb_ref[...])


def _combine(tab, ys, pos_t, x1, gate, ln_g, ln_b, rows_per_mod, tt, sb):
    t, d = x1.shape
    tiles_per_mod = rows_per_mod // tt
    nmod = gate.shape[0]
    return pl.pallas_call(
        functools.partial(_combine_kernel, tt=tt, n_tiles=t // tt),
        grid_spec=pltpu.PrefetchScalarGridSpec(
            num_scalar_prefetch=1, grid=(t // tt,),
            in_specs=[pl.BlockSpec(memory_space=pl.ANY),
                      pl.BlockSpec((tt, 2), lambda i, s: (i, 0)),
                      pl.BlockSpec((tt, d), lambda i, s: (i, 0)),
                      pl.BlockSpec((None, 1, d), lambda i, s: (i // tiles_per_mod, 0, 0)),
                      pl.BlockSpec((1, d), lambda i, s: (0, 0)), pl.BlockSpec((1, d), lambda i, s: (0, 0))],
            out_specs=pl.BlockSpec((tt, d), lambda i, s: (i, 0)),
            scratch_shapes=[pltpu.VMEM((2, sb, d), F32), pltpu.SemaphoreType.DMA((2,))]),
        out_shape=jax.ShapeDtypeStruct((t, d), F32),
        compiler_params=_cparams(1), name="moe_combine",
    )(tab, ys, pos_t, x1, gate.reshape(nmod, 1, d), ln_g.reshape(1, d), ln_b.reshape(1, d))


def _moe(h2, logits_t, x1, gate, ln_g, ln_b, router_bias, wg, wu, wd, layer, rows_per_mod):
    t, d = h2.shape
    bm, tt, ne = MOE_BM, MOE_TILE, N_EXPERTS
    n_tiles = t // tt
    pad = ne * (ROW_ALIGN - 1)
    sb = -(-(2 * tt + pad) // LANES) * LANES
    pos, wts, tab3, cnt = _route(logits_t, router_bias, tr=tt)
    rows = cnt[:, 0].astype(I32)
    prows = (rows + bm - 1) // bm * bm
    pends = jnp.cumsum(prows)
    pstarts = pends - prows
    n_blk = -(-(2 * t + n_tiles * pad + ne * (bm - 1)) // bm)
    n_used = pends[-1] // bm
    blk_ids = jnp.minimum(jnp.arange(n_blk, dtype=I32), n_used - 1)
    blk_e = jnp.minimum(jnp.sum((blk_ids[:, None] * bm >= pends[None, :]).astype(I32), 1), ne - 1)
    eids = jnp.arange(ne, dtype=I32)
    of_blk = lambda per_expert: jnp.sum(jnp.where(blk_e[:, None] == eids[None, :], per_expert[None, :], 0), 1)
    raw_ids = jnp.arange(n_blk, dtype=I32)
    blk_rows = jnp.where(raw_ids < n_used, jnp.clip(of_blk(pstarts + rows) - raw_ids * bm, 0, bm), 0)
    later_used = (eids[None, :] > eids[:, None]) & (prows[None, :] > 0)
    next_used = jnp.min(jnp.where(later_used, eids[None, :], ne), 1)
    next_e = of_blk(jnp.where(next_used < ne, next_used, -1))
    tab3 = tab3[:, :, :, 0]
    tab = jnp.concatenate([tab3[:, 0].reshape(-1), tab3[:, 1].reshape(-1),
                           (tab3[:, 2] + pstarts[None, :]).reshape(-1), pstarts + rows, prows - rows,
                           n_used.reshape(1), jnp.sum(tab3[:, 1], 1)]).astype(I32)
    xs = _dispatch(tab, h2, pos, wts, n_blk * bm, tt, sb, bm)
    ys = _ffn(blk_e, blk_rows.astype(I32), next_e.astype(I32), n_used.reshape(1).astype(I32), xs, wg, wu, wd,
              layer, bm, parts=2)
    return _combine(tab, ys, pos.T, x1, gate, ln_g, ln_b, rows_per_mod, tt, sb)


def _filter_mlp_kernel(z_ref, w1_ref, w2_ref, w3_ref, b_ref, sf_ref, o_ref):
    hp = lax.Precision.HIGHEST
    b, sf = b_ref[...], sf_ref[...]
    h = jnp.sin(sf[0:1] * (jnp.dot(z_ref[...], w1_ref[...], precision=hp, preferred_element_type=F32) + b[0:1]))
    h = jnp.sin(sf[1:2] * (jnp.dot(h, w2_ref[...], precision=hp, preferred_element_type=F32) + b[1:2]))
    o_ref[...] = jnp.sin(sf[2:3] * (jnp.dot(h, w3_ref[...], precision=hp, preferred_element_type=F32) + b[2:3]))


def _filter_mlp(zp, w_in_p, w_hid, b, sf):
    n = zp.shape[0]
    return pl.pallas_call(
        _filter_mlp_kernel, out_shape=jax.ShapeDtypeStruct((n, FILTER_HID), F32),
        compiler_params=pltpu.CompilerParams(vmem_limit_bytes=VMEM_LIMIT), name="hyena_filter_mlp",
    )(zp, w_in_p, w_hid[0], w_hid[1], b, sf)


def _filter_spec_kernel(hid_ref, wf_ref, wb_ref, dec_ref, fwd_ref, hr_ref, g_ref, nyq_ref, p_scr, q_scr, *, fb, kps):
    k = pl.program_id(2)
    n = hid_ref.shape[0]

    @pl.when(k == 0)
    def _():
        hp = lax.Precision.HIGHEST
        hid, dec = hid_ref[...], dec_ref[...]
        row = lax.broadcasted_iota(I32, (n, 1), 0)
        fw = jnp.dot(hid, wf_ref[...], precision=hp, preferred_element_type=F32) * dec
        bw = jnp.dot(hid, wb_ref[...], precision=hp, preferred_element_type=F32) * dec
        bw = jnp.where(row == 0, 0.0, bw)
        p = fw + bw
        p_scr[...] = p.astype(BF16)
        q_scr[...] = (fw - bw).astype(BF16)
        sign = (1 - 2 * (row & 1)).astype(F32)
        nyq_ref[...] = jnp.sum(p * sign, 0, keepdims=True)

    rowb = lax.broadcasted_iota(I32, (fb, 1), 0)
    for j in range(kps):
        hr_ref[j * fb:(j + 1) * fb, :] = jnp.dot(fwd_ref[j, 0:fb, :], p_scr[...], preferred_element_type=F32)
        gg = jnp.dot(fwd_ref[j, fb:2 * fb, :], q_scr[...], preferred_element_type=F32)
        if j == 0:
            gg = jnp.where((rowb == 0) & (k == 0), 0.0, gg)
        g_ref[j * fb:(j + 1) * fb, :] = gg


def _filter_spec(hid, w_out4, decay, fwd_tab, fb, tc, kps):
    n = hid.shape[0]
    d = D_MODEL
    kb = n // (fb * kps)
    return pl.pallas_call(
        functools.partial(_filter_spec_kernel, fb=fb, kps=kps), grid=(2, d // tc, kb),
        in_specs=[pl.BlockSpec((n, FILTER_HID), lambda o, c, k: (0, 0)),
                  pl.BlockSpec((None, None, FILTER_HID, tc), lambda o, c, k: (0, o, 0, c)),
                  pl.BlockSpec((None, None, FILTER_HID, tc), lambda o, c, k: (1, o, 0, c)),
                  pl.BlockSpec((n, tc), lambda o, c, k: (0, c)),
                  pl.BlockSpec((kps, 2 * fb, n), lambda o, c, k: (k, 0, 0))],
        out_specs=[pl.BlockSpec((None, kps * fb, tc), lambda o, c, k: (o, k, c)),
                   pl.BlockSpec((None, kps * fb, tc), lambda o, c, k: (o, k, c)),
                   pl.BlockSpec((None, 1, tc), lambda o, c, k: (o, 0, c))],
        out_shape=[jax.ShapeDtypeStruct((2, n, d), F32), jax.ShapeDtypeStruct((2, n, d), F32),
                   jax.ShapeDtypeStruct((2, 1, d), F32)],
        scratch_shapes=[pltpu.VMEM((n, tc), BF16), pltpu.VMEM((n, tc), BF16)],
        compiler_params=_cparams(3), name="hyena_filter_spec",
    )(hid, w_out4, w_out4, decay, fwd_tab)


def _short_conv(u_ref, cw_ref, cb_ref):
    n = u_ref.shape[0]
    u = u_ref[...].astype(F32)
    row = lax.broadcasted_iota(I32, (n, 1), 0)
    prev = jnp.where(row == 0, 0.0, pltpu.roll(u, 1, 0))
    nxt = jnp.where(row == n - 1, 0.0, pltpu.roll(u, n - 1, 0))
    cw = cw_ref[...]
    return prev * cw[0:1] + u * cw[1:2] + nxt * cw[2:3] + cb_ref[...]


def _hyena_conv_kernel(uv_ref, ux1_ref, ux2_ref, cwv_ref, cwx1_ref, cwx2_ref, cbv_ref, cbx1_ref, cbx2_ref,
                       fwd_ref, inv_ref, hr_ref, g_ref, nyq_ref, sk_ref, o_ref, zb_scr, z32_scr, acc_scr, *, fb, kb,
                       kps):
    o = pl.program_id(2)
    k = pl.program_id(3)

    @pl.when((o == 0) & (k == 0))
    def _():
        v = _short_conv(uv_ref, cwv_ref, cbv_ref)
        z32_scr[...] = v
        zb_scr[...] = v.astype(BF16)

    @pl.when(k == 0)
    def _():
        acc_scr[...] = jnp.zeros_like(acc_scr)

    zb = zb_scr[...]
    zfs = [jnp.dot(fwd_ref[j], zb, preferred_element_type=F32) for j in range(kps)]
    rowb = lax.broadcasted_iota(I32, (fb, 1), 0)
    for j in range(kps):
        zr, zi = zfs[j][:fb], zfs[j][fb:]
        hr, gg = hr_ref[j * fb:(j + 1) * fb, :], g_ref[j * fb:(j + 1) * fb, :]
        hb = jnp.where((rowb == 0) & (k == 0), nyq_ref[...], hr) if j == 0 else hr
        y = jnp.concatenate([zr * hr - zi * gg, zr * gg + zi * hb], 0).astype(BF16)
        acc_scr[...] += jnp.dot(inv_ref[j], y, preferred_element_type=F32)

    @pl.when((k == kb - 1) & (o == 0))
    def _():
        zn = _short_conv(ux1_ref, cwx1_ref, cbx1_ref) * (acc_scr[...] + z32_scr[...] * sk_ref[...])
        z32_scr[...] = zn
        zb_scr[...] = zn.astype(BF16)

    @pl.when((k == kb - 1) & (o == 1))
    def _():
        zn = _short_conv(ux2_ref, cwx2_ref, cbx2_ref) * (acc_scr[...] + z32_scr[...] * sk_ref[...])
        o_ref[...] = zn.astype(o_ref.dtype)


def _hyena_conv(u, conv_w, conv_b, fwd_tab, inv_tab, hr, gg, nyq, skip, fb, tc, kps):
    b, n, _ = u.shape
    d = D_MODEL
    kb = n // (fb * kps)
    ncb = d // tc
    ucol = lambda part: (lambda bi, c, o, k: (bi, 0, part * ncb + c))
    wcol = lambda part: (lambda bi, c, o, k: (0, part * ncb + c))
    spec = lambda bi, c, o, k: (o, k, c)
    per_o = lambda bi, c, o, k: (o, 0, c)
    return pl.pallas_call(
        functools.partial(_hyena_conv_kernel, fb=fb, kb=kb, kps=kps), grid=(b, ncb, 2, kb),
        in_specs=[pl.BlockSpec((None, n, tc), ucol(0)), pl.BlockSpec((None, n, tc), ucol(1)),
                  pl.BlockSpec((None, n, tc), ucol(2)),
                  pl.BlockSpec((3, tc), wcol(0)), pl.BlockSpec((3, tc), wcol(1)), pl.BlockSpec((3, tc), wcol(2)),
                  pl.BlockSpec((1, tc), wcol(0)), pl.BlockSpec((1, tc), wcol(1)), pl.BlockSpec((1, tc), wcol(2)),
                  pl.BlockSpec((kps, 2 * fb, n), lambda bi, c, o, k: (k, 0, 0)),
                  pl.BlockSpec((kps, n, 2 * fb), lambda bi, c, o, k: (k, 0, 0)),
                  pl.BlockSpec((None, kps * fb, tc), spec), pl.BlockSpec((None, kps * fb, tc), spec),
                  pl.BlockSpec((None, 1, tc), per_o), pl.BlockSpec((None, 1, tc), per_o)],
        out_specs=pl.BlockSpec((None, n, tc), lambda bi, c, o, k: (bi, 0, c)),
        out_shape=jax.ShapeDtypeStruct((b, n, d), BF16),
        scratch_shapes=[pltpu.VMEM((n, tc), BF16), pltpu.VMEM((n, tc), F32), pltpu.VMEM((n, tc), F32)],
        compiler_params=_cparams(4), name="hyena_conv",
    )(u, u, u, conv_w, conv_w, conv_w, conv_b, conv_b, conv_b, fwd_tab, inv_tab, hr, gg, nyq,
      skip.reshape(2, 1, d))


def kernel(x, c, ctx, c_ctx, mod_w, mod_b, ln_g, ln_b, attn_w_in, attn_lambda, attn_subln_g, attn_sink, attn_w_out,
           hy_w_in, hy_conv_w, hy_conv_b, hy_ffn_w_in, hy_ffn_w_hid, hy_ffn_b, hy_sin_freq, hy_ffn_w_out, hy_skip,
           hy_w_out, router_w, router_bias, exp_w_gate, exp_w_up, exp_w_down):
    b, n, d = x.shape
    nc = ctx.shape[1]
    t = b * n
    assert d == D_MODEL and b + 1 <= MOD_ROWS and n % 512 == 0 and nc % 256 == 0

    c_rows = jnp.zeros((MOD_ROWS, d), F32).at[:b].set(c).at[b].set(c_ctx)
    mods = _mods(c_rows, mod_w, mod_b)
    router_wt = jnp.pad(router_w, ((0, 0), (0, LANES - N_EXPERTS)))
    x2d = x.reshape(t, d)

    sh1, sc1, g1, sh2, sc2, g2 = jnp.split(mods[0], 6, axis=-1)
    cos, sin = _rope_tables(n)
    w_in = attn_w_in[0].astype(BF16)
    q_groups = tuple(range(Q_W // LANES))
    rope_groups = q_groups + tuple(range(Q_W // LANES, (Q_W + DIFF_QK_W) // LANES)) + (
        (Q_W + DIFF_QK_W + DIFF_VW) // LANES,)
    q_scales = tuple((g, HEAD_DIM ** -0.5 * LOG2_E) for g in q_groups)
    proj = _modmm(x2d, sc1[:b], sh1[:b], w_in, rows_per_mod=n, tm=1024,
                  rope=(jnp.asarray(cos), jnp.asarray(sin), rope_groups, q_scales), name="attn_in_proj")
    proj_c = _modmm(ctx.reshape(b * nc, d), sc1[b:b + 1], sh1[b:b + 1], w_in[:, Q_W:], rows_per_mod=b * nc, tm=256,
                    name="ctx_in_proj")
    proj = proj.reshape(b, n, ATTN_PROJ_W)
    proj_c = proj_c.reshape(b, nc, KV_W)
    lam_init = 0.8 - 0.6 * math.exp(-0.3 * 0)
    oa = _diff_attn(proj, proj_c, attn_lambda[0], attn_subln_g[0], lam_init, tq=1024, sub=128)
    ow = _win_attn(proj, proj_c, attn_sink[0], tq=256)
    w_out = attn_w_out[0].astype(BF16)
    x1, h2, lgt = _proj_ln([oa.reshape(t, DIFF_VW), ow.reshape(t, WIN_Q_W)], [w_out[:DIFF_VW], w_out[DIFF_VW:]],
                           x2d, g1[:b], ln_g[0, 0], ln_b[0, 0], sc2[:b], sh2[:b], router_wt, rows_per_mod=n, tm=1024,
                           name="attn_out_proj_ln")
    x2 = _moe(h2, lgt, x1, g2[:b], ln_g[0, 1], ln_b[0, 1], router_bias, exp_w_gate, exp_w_up, exp_w_down, layer=0,
              rows_per_mod=n)

    sh1, sc1, g1, sh2, sc2, g2 = jnp.split(mods[1], 6, axis=-1)
    u = _modmm(x2, sc1[:b], sh1[:b], hy_w_in[0].astype(BF16), rows_per_mod=n, tm=1024, name="hyena_in_proj")
    fb, tc = 256, 512
    fwd_np, inv_np = _dft_tables(n, fb)
    fwd_tab, inv_tab = jnp.asarray(fwd_np).astype(BF16), jnp.asarray(inv_np).astype(BF16)
    zp, decay = _filter_tables(n)
    w_in_p = jnp.zeros((FILTER_HID, FILTER_HID), F32).at[:FILTER_EMB].set(hy_ffn_w_in[0])
    hid = _filter_mlp(jnp.asarray(zp), w_in_p, hy_ffn_w_hid[0], hy_ffn_b[0], hy_sin_freq[0])
    hr, gg, nyq = _filter_spec(hid, hy_ffn_w_out[0].reshape(FILTER_HID, 2, 2, d).transpose(1, 2, 0, 3), jnp.asarray(decay), fwd_tab, fb, tc, kps=4)
    z2 = _hyena_conv(u.reshape(b, n, 3 * d), hy_conv_w[0], hy_conv_b[0].reshape(1, 3 * d), fwd_tab, inv_tab,
                     hr, gg, nyq, hy_skip[0], fb, tc, kps=2)
    x3, h2, lgt = _proj_ln([z2.reshape(t, d)], [hy_w_out[0].astype(BF16)], x2, g1[:b], ln_g[1, 0], ln_b[1, 0],
                           sc2[:b], sh2[:b], router_wt, rows_per_mod=n, tm=1024, name="hyena_out_proj_ln")
    x4 = _moe(h2, lgt, x3, g2[:b], ln_g[1, 1], ln_b[1, 1], router_bias, exp_w_gate, exp_w_up, exp_w_down, layer=1,
              rows_per_mod=n)
    return x4.reshape(b, n, d)
```

```python
import functools
import math

import jax
import jax.numpy as jnp
import numpy as np
from jax import lax
from jax.experimental import pallas as pl
from jax.experimental.pallas import tpu as pltpu

F32 = jnp.float32
BF16 = jnp.bfloat16
I32 = jnp.int32

D_MODEL = 1024
DEPTH = 2
GRID_W = 64
HEAD_DIM = 64
DIFF_HEADS = 4
WIN_Q_HEADS = 8
WIN_KV_HEADS = 2
WINDOW = 128
WIN_BLOCK = 128
ROPE_BASE = 10000.0
DIFF_QK_W = DIFF_HEADS * 2 * HEAD_DIM
DIFF_VW = DIFF_HEADS * 2 * HEAD_DIM
WIN_Q_W = WIN_Q_HEADS * HEAD_DIM
WIN_KV_W = WIN_KV_HEADS * HEAD_DIM
Q_W = DIFF_QK_W + WIN_Q_W
KV_W = DIFF_QK_W + DIFF_VW + 2 * WIN_KV_W
ATTN_PROJ_W = Q_W + KV_W
FILTER_EMB = 33
FILTER_HID = 64
DECAY_TARGET = 1e-2
FAST_DECAY_PCT = 0.3
SLOW_DECAY_PCT = 1.5
N_EXPERTS = 16
N_GROUPS = 4
EXPERTS_PER_GROUP = N_EXPERTS // N_GROUPS
EXPERT_FF = 1024
LN_EPS = 1e-5
DEEPNORM_ALPHA = (2 * DEPTH) ** 0.25
NEG_INF = -1e30
LOG2_E = 1.4426950408889634

LANES = 128
MOD_ROWS = 16
VMEM_LIMIT = 60 * 1024 * 1024
MOE_BM = 512
MOE_TILE = 512
HYENA_PART = 512
ROW_ALIGN = 8
NT_DIMS = (((1,), (1,)), ((), ()))


def _cparams(n_axes):
    return pltpu.CompilerParams(dimension_semantics=("arbitrary",) * n_axes, vmem_limit_bytes=VMEM_LIMIT)


@functools.lru_cache(maxsize=None)
def _rope_tables(n):
    rows = n // GRID_W
    r, col = np.meshgrid(np.arange(rows, dtype=np.float32), np.arange(GRID_W, dtype=np.float32), indexing="ij")
    axis_dim = HEAD_DIM // 2
    inv_freq = (ROPE_BASE ** (-np.arange(0, axis_dim, 2, dtype=np.float32) / axis_dim)).astype(np.float32)
    ang = np.concatenate([r.reshape(-1, 1) * inv_freq, col.reshape(-1, 1) * inv_freq], -1)
    ang = np.concatenate([ang, ang], -1).astype(np.float32)
    cos, sin = np.cos(ang), np.sin(ang)
    half = np.arange(HEAD_DIM) < HEAD_DIM // 2
    sin_signed = np.where(half[None, :], -sin, sin)
    reps = LANES // HEAD_DIM
    return (np.tile(cos, (1, reps)).astype(np.float32), np.tile(sin_signed, (1, reps)).astype(np.float32))


@functools.lru_cache(maxsize=None)
def _dft_tables(n, fb):
    big = 2 * n
    k = np.arange(n, dtype=np.int64)[:, None]
    t = np.arange(n, dtype=np.int64)[None, :]
    ang = ((k * t) % big).astype(np.float64) * (2.0 * math.pi / big)
    c, s = np.cos(ang), np.sin(ang)
    alt = (1 - 2 * (np.arange(n) & 1)).astype(np.float64)
    s_f = s.copy()
    s_f[0, :] = alt
    kb = n // fb
    fwd = np.concatenate([c.reshape(kb, fb, n), s_f.reshape(kb, fb, n)], axis=1)
    ci = c.T * (2.0 / big)
    ci[:, 0] = 1.0 / big
    si = s.T * (2.0 / big)
    si[:, 0] = alt / big
    inv = np.concatenate([ci.reshape(n, kb, fb).transpose(1, 0, 2), si.reshape(n, kb, fb).transpose(1, 0, 2)], axis=2)
    return fwd.astype(np.float32), inv.astype(np.float32)


@functools.lru_cache(maxsize=None)
def _filter_tables(n):
    t = np.linspace(0.0, 1.0, n, dtype=np.float32)[:, None]
    bands = (FILTER_EMB - 1) // 2
    w = (2.0 * math.pi * np.arange(n, dtype=np.float32)[:, None] / n).astype(np.float32)
    fr = np.linspace(1e-4, bands - 1, bands, dtype=np.float32)[None, :]
    z = np.concatenate([t, np.cos(fr * w), -np.sin(fr * w)], -1).astype(np.float32)
    zp = np.zeros((n, FILTER_HID), np.float32)
    zp[:, :FILTER_EMB] = z
    deltas = np.abs(np.linspace(math.log(DECAY_TARGET) / SLOW_DECAY_PCT, math.log(DECAY_TARGET) / FAST_DECAY_PCT,
                                D_MODEL, dtype=np.float32))
    decay = np.exp(-t * deltas[None, :]).astype(np.float32)
    return zp, decay


@functools.lru_cache(maxsize=None)
def _partition_tables(s):
    big = 2 * s
    k = np.arange(s, dtype=np.int64)[:, None]
    r = np.arange(s, dtype=np.int64)[None, :]

    def phase(pr, sin_sign, drop_r0):
        ang = ((k * pr) % big).astype(np.float64) * (2.0 * math.pi / big)
        c, sn = np.cos(ang), sin_sign * np.sin(ang)
        sn[0, :] = 1.0 - 2.0 * (pr[0] & 1)
        if drop_r0:
            c[:, 0] = 0.0
            sn[:, 0] = 0.0
        return np.concatenate([c, sn], 0)

    phases = np.stack([phase(r, 1.0, False), phase(s - r, -1.0, True), phase(r, -1.0, False), phase(s - r, 1.0, True)])
    ang = ((r.T * k.T) % big).astype(np.float64) * (2.0 * math.pi / big)
    ci = np.cos(ang) * (2.0 / big)
    ci[:, 0] = 1.0 / big
    si = np.sin(ang) * (2.0 / big)
    si[:, 0] = (1.0 - 2.0 * (np.arange(s) & 1)) / big
    return phases.astype(np.float32), np.concatenate([ci, si], 1).astype(np.float32)


def _mods_kernel(c_ref, w_ref, b_ref, o_ref):
    c = c_ref[...]
    a = (c * jax.nn.sigmoid(c)).astype(BF16)
    o_ref[...] = jnp.dot(a, w_ref[...].astype(BF16), preferred_element_type=F32) + b_ref[...]


def _mods(c_rows, mod_w, mod_b):
    d = D_MODEL
    tn = 1536
    return pl.pallas_call(
        _mods_kernel,
        grid=(DEPTH, 6 * d // tn),
        in_specs=[pl.BlockSpec((MOD_ROWS, d), lambda l, j: (0, 0)),
                  pl.BlockSpec((None, d, tn), lambda l, j: (l, 0, j)),
                  pl.BlockSpec((None, 1, tn), lambda l, j: (l, 0, j))],
        out_specs=pl.BlockSpec((None, MOD_ROWS, tn), lambda l, j: (l, 0, j)),
        out_shape=jax.ShapeDtypeStruct((DEPTH, MOD_ROWS, 6 * d), F32),
        compiler_params=_cparams(2), name="mods",
    )(c_rows, mod_w, mod_b.reshape(DEPTH, 1, 6 * d))


def _modmm_kernel(*refs, n_groups, rope_groups, scaled_groups, chunk):
    if rope_groups:
        x_ref, sc_ref, sh_ref, w_ref, cos_ref, sin_ref, o_ref = refs
        cos, sin = cos_ref[...], sin_ref[...]
        lane = lax.broadcasted_iota(I32, (1, LANES), 1)
        first_half = (lane % HEAD_DIM) < HEAD_DIM // 2
    else:
        x_ref, sc_ref, sh_ref, w_ref, o_ref = refs
    h = (x_ref[...] * (1.0 + sc_ref[...]) + sh_ref[...]).astype(BF16)
    scales = dict(scaled_groups)
    gpc = chunk // LANES
    for c in range(n_groups // gpc):
        acc = jnp.dot(h, w_ref[:, c * chunk:(c + 1) * chunk], preferred_element_type=F32)
        for j in range(gpc):
            g = c * gpc + j
            blk = acc[:, j * LANES:(j + 1) * LANES]
            if g in rope_groups:
                rot = jnp.where(first_half, pltpu.roll(blk, LANES - HEAD_DIM // 2, 1), pltpu.roll(blk, HEAD_DIM // 2, 1))
                blk = blk * cos + rot * sin
                if g in scales:
                    blk = blk * scales[g]
            o_ref[:, g * LANES:(g + 1) * LANES] = blk.astype(o_ref.dtype)


def _modmm(x2d, sc, sh, w, rows_per_mod, tm, rope=None, name="modmm"):
    t, d = x2d.shape
    n = w.shape[1]
    tiles_per_mod = rows_per_mod // tm
    nmod = sc.shape[0]
    in_specs = [pl.BlockSpec((tm, d), lambda i: (i, 0)),
                pl.BlockSpec((None, 1, d), lambda i: (i // tiles_per_mod, 0, 0)),
                pl.BlockSpec((None, 1, d), lambda i: (i // tiles_per_mod, 0, 0)),
                pl.BlockSpec((d, n), lambda i: (0, 0))]
    args = [x2d, sc.reshape(nmod, 1, d), sh.reshape(nmod, 1, d), w]
    rope_groups, scaled_groups = (), ()
    if rope is not None:
        cos, sin, rope_groups, scaled_groups = rope
        in_specs += [pl.BlockSpec((tm, LANES), lambda i: (i % tiles_per_mod, 0)),
                     pl.BlockSpec((tm, LANES), lambda i: (i % tiles_per_mod, 0))]
        args += [cos, sin]
    kern = functools.partial(_modmm_kernel, n_groups=n // LANES, rope_groups=tuple(rope_groups),
                             scaled_groups=tuple(scaled_groups), chunk=256)
    return pl.pallas_call(
        kern, grid=(t // tm,), in_specs=in_specs,
        out_specs=pl.BlockSpec((tm, n), lambda i: (i, 0)),
        out_shape=jax.ShapeDtypeStruct((t, n), BF16),
        compiler_params=_cparams(1), name=name,
    )(*args)


def _diff_attn_kernel(q_ref, k_ref, v_ref, kc_ref, vc_ref, lam_ref, g_ref, o_ref, *, lam_init, sub):
    lane = lax.broadcasted_iota(I32, (1, LANES), 1)
    k, kc = k_ref[...], kc_ref[...]
    v, vc = v_ref[...], vc_ref[...]

    def scores(r0):
        q = q_ref[r0:r0 + sub, :]
        zero = jnp.zeros_like(q)
        out = []
        for qm in (jnp.where(lane < HEAD_DIM, q, zero), jnp.where(lane >= HEAD_DIM, q, zero)):
            out.append((lax.dot_general(qm, k, NT_DIMS, preferred_element_type=F32),
                        lax.dot_general(qm, kc, NT_DIMS, preferred_element_type=F32)))
        return out

    def probs(sl, sc):
        m = jnp.maximum(jnp.max(sl, -1, keepdims=True), jnp.max(sc, -1, keepdims=True))
        pl_, pc = jnp.exp2(sl - m), jnp.exp2(sc - m)
        den = jnp.sum(pl_, -1, keepdims=True) + jnp.sum(pc, -1, keepdims=True)
        return pl_, pc, 1.0 / den

    lv = lam_ref[...]
    lam = (jnp.exp(jnp.sum(lv[0:1] * lv[1:2], keepdims=True)) - jnp.exp(jnp.sum(lv[2:3] * lv[3:4], keepdims=True))
           + lam_init)
    starts = list(range(0, q_ref.shape[0], sub))
    nxt = scores(starts[0])
    for i, r0 in enumerate(starts):
        cur = nxt
        if i + 1 < len(starts):
            nxt = scores(starts[i + 1])
        p1l, p1c, r1 = probs(*cur[0])
        p2l, p2c, r2 = probs(*cur[1])
        w2 = lam * r2
        al = (p1l * r1 - p2l * w2).astype(BF16)
        ac = (p1c * r1 - p2c * w2).astype(BF16)
        o = jnp.dot(al, v, preferred_element_type=F32) + jnp.dot(ac, vc, preferred_element_type=F32)
        ms = jnp.mean(o * o, -1, keepdims=True)
        o_ref[r0:r0 + sub, :] = (o * lax.rsqrt(ms + LN_EPS) * g_ref[...] * (1.0 - lam_init)).astype(o_ref.dtype)


def _diff_attn(proj, proj_c, lam_vec, subln_g, lam_init, tq, sub):
    b, n, _ = proj.shape
    nc = proj_c.shape[1]
    kcol = Q_W // LANES
    vcol = (Q_W + DIFF_QK_W) // LANES
    vccol = DIFF_QK_W // LANES
    kern = functools.partial(_diff_attn_kernel, lam_init=lam_init, sub=sub)
    return pl.pallas_call(
        kern, grid=(b, DIFF_HEADS, n // tq),
        in_specs=[pl.BlockSpec((None, tq, LANES), lambda bi, h, i: (bi, i, h)),
                  pl.BlockSpec((None, n, LANES), lambda bi, h, i: (bi, 0, kcol + h)),
                  pl.BlockSpec((None, n, LANES), lambda bi, h, i: (bi, 0, vcol + h)),
                  pl.BlockSpec((None, nc, LANES), lambda bi, h, i: (bi, 0, h)),
                  pl.BlockSpec((None, nc, LANES), lambda bi, h, i: (bi, 0, vccol + h)),
                  pl.BlockSpec((4, HEAD_DIM), lambda bi, h, i: (0, 0)),
                  pl.BlockSpec((1, LANES), lambda bi, h, i: (0, 0))],
        out_specs=pl.BlockSpec((None, tq, LANES), lambda bi, h, i: (bi, i, h)),
        out_shape=jax.ShapeDtypeStruct((b, n, DIFF_VW), BF16),
        compiler_params=_cparams(3), name="diff_attn",
    )(proj, proj, proj, proj_c, proj_c, lam_vec, subln_g.reshape(1, LANES))


def _win_attn_kernel(sink_ref, q_ref, k_ref, v_ref, kc_ref, vc_ref, o_ref, *, seq, tq):
    n = pl.program_id(1)
    gq = WIN_Q_HEADS // WIN_KV_HEADS
    kw = tq + 2 * WINDOW
    start = pl.multiple_of(jnp.clip(n * tq - WINDOW, 0, seq - kw), WINDOW)
    k_win, v_win = k_ref[pl.ds(start, kw), :], v_ref[pl.ds(start, kw), :]
    kc, vc = kc_ref[...], vc_ref[...]
    lane = lax.broadcasted_iota(I32, (1, LANES), 1)
    row = lax.broadcasted_iota(I32, (gq * tq, 1), 0)
    q_abs = n * tq + row % tq
    k_abs = start + lax.broadcasted_iota(I32, (1, kw), 1)
    allowed = jnp.abs(q_abs - k_abs) <= WINDOW
    head = row // tq
    q = q_ref[...].astype(F32)
    for g in range(WIN_KV_HEADS):
        in_g = (lane // HEAD_DIM) == g
        parts = []
        for j in range(gq):
            hq = g * gq + j
            x = q[:, (hq // 2) * LANES:(hq // 2 + 1) * LANES]
            if hq % 2 != g:
                x = pltpu.roll(x, HEAD_DIM, 1)
            parts.append(jnp.where(in_g, x, 0.0))
        qs = jnp.concatenate(parts, 0).astype(BF16)
        s_loc = lax.dot_general(qs, k_win, NT_DIMS, preferred_element_type=F32)
        s_ctx = lax.dot_general(qs, kc, NT_DIMS, preferred_element_type=F32)
        s_loc = jnp.where(allowed, s_loc, NEG_INF)
        sk = jnp.zeros((gq * tq, 1), F32)
        for j in range(gq):
            sk = jnp.where(head == j, sink_ref[g * gq + j] * LOG2_E, sk)
        m = jnp.maximum(jnp.maximum(jnp.max(s_loc, -1, keepdims=True), jnp.max(s_ctx, -1, keepdims=True)), sk)
        p_loc, p_ctx = jnp.exp2(s_loc - m), jnp.exp2(s_ctx - m)
        den = jnp.sum(p_loc, -1, keepdims=True) + jnp.sum(p_ctx, -1, keepdims=True) + jnp.exp2(sk - m)
        o = (jnp.dot(p_loc.astype(BF16), v_win, preferred_element_type=F32)
             + jnp.dot(p_ctx.astype(BF16), vc, preferred_element_type=F32)) * (1.0 / den)
        for cb in range(gq // 2):
            pair = []
            for half in range(2):
                piece = o[(2 * cb + half) * tq:(2 * cb + half + 1) * tq]
                pair.append(piece if half == g else pltpu.roll(piece, HEAD_DIM, 1))
            col = (g * gq // 2 + cb) * LANES
            o_ref[:, col:col + LANES] = jnp.where(lane < HEAD_DIM, pair[0], pair[1]).astype(o_ref.dtype)


def _win_attn(proj, proj_c, sink, tq):
    b, n, _ = proj.shape
    nc = proj_c.shape[1]
    qcol = DIFF_QK_W // WIN_Q_W
    kcol = (Q_W + DIFF_QK_W + DIFF_VW) // LANES
    kccol = (DIFF_QK_W + DIFF_VW) // LANES
    kern = functools.partial(_win_attn_kernel, seq=n, tq=tq)
    return pl.pallas_call(
        kern, grid=(b, n // tq),
        in_specs=[pl.BlockSpec(memory_space=pltpu.SMEM),
                  pl.BlockSpec((None, tq, WIN_Q_W), lambda bi, i: (bi, i, qcol)),
                  pl.BlockSpec((None, n, LANES), lambda bi, i: (bi, 0, kcol)),
                  pl.BlockSpec((None, n, LANES), lambda bi, i: (bi, 0, kcol + 1)),
                  pl.BlockSpec((None, nc, LANES), lambda bi, i: (bi, 0, kccol)),
                  pl.BlockSpec((None, nc, LANES), lambda bi, i: (bi, 0, kccol + 1))],
        out_specs=pl.BlockSpec((None, tq, WIN_Q_W), lambda bi, i: (bi, i, 0)),
        out_shape=jax.ShapeDtypeStruct((b, n, WIN_Q_W), BF16),
        compiler_params=_cparams(2), name="win_attn",
    )(sink, proj, proj, proj, proj_c, proj_c)


def _layer_norm(r, g, b):
    mu = jnp.mean(r, -1, keepdims=True)
    dlt = r - mu
    var = jnp.mean(dlt * dlt, -1, keepdims=True)
    return dlt * lax.rsqrt(var + LN_EPS) * g + b


def _proj_ln_kernel(*refs, n_in, sub):
    a_refs = refs[:n_in]
    w_refs = refs[n_in:2 * n_in]
    x_ref, gate_ref, lng_ref, lnb_ref, sc_ref, sh_ref, rw_ref, x1_ref, h2_ref, lg_ref = refs[2 * n_in:]

    def split(v):
        hi = v.astype(BF16)
        return hi, (v - hi.astype(F32)).astype(BF16)

    def project(r0):
        y = jnp.dot(a_refs[0][r0:r0 + sub, :], w_refs[0][...], preferred_element_type=F32)
        for a_ref, w_ref in zip(a_refs[1:], w_refs[1:]):
            y = y + jnp.dot(a_ref[r0:r0 + sub, :], w_ref[...], preferred_element_type=F32)
        return y

    rw_hi, rw_lo = split(rw_ref[...])
    rw_both = jnp.concatenate([rw_hi, rw_lo], 1)
    starts = list(range(0, x_ref.shape[0], sub))
    nxt = project(starts[0])
    for i, r0 in enumerate(starts):
        y = nxt
        if i + 1 < len(starts):
            nxt = project(starts[i + 1])
        xn = _layer_norm(DEEPNORM_ALPHA * x_ref[r0:r0 + sub, :] + gate_ref[...] * y, lng_ref[...], lnb_ref[...])
        x1_ref[r0:r0 + sub, :] = xn
        h2 = xn * (1.0 + sc_ref[...]) + sh_ref[...]
        h2_ref[r0:r0 + sub, :] = h2.astype(h2_ref.dtype)
        h_hi, h_lo = split(h2)
        hh = jnp.dot(h_hi, rw_both, preferred_element_type=F32)
        lg = hh[:, :LANES] + hh[:, LANES:] + jnp.dot(h_lo, rw_hi, preferred_element_type=F32)
        lg_ref[:, r0:r0 + sub] = lg.T[:N_EXPERTS, :]


def _proj_ln(acts, ws, x2d, gate, ln_g, ln_b, sc, sh, router_wt, rows_per_mod, tm, name):
    t, d = x2d.shape
    n_in = len(acts)
    tiles_per_mod = rows_per_mod // tm
    nmod = gate.shape[0]
    row = lambda i: (i, 0)
    full = lambda i: (0, 0)
    mod = lambda i: (i // tiles_per_mod, 0, 0)
    in_specs = ([pl.BlockSpec((tm, a.shape[1]), row) for a in acts]
                + [pl.BlockSpec(w.shape, full) for w in ws]
                + [pl.BlockSpec((tm, d), row), pl.BlockSpec((None, 1, d), mod),
                   pl.BlockSpec((1, d), full), pl.BlockSpec((1, d), full),
                   pl.BlockSpec((None, 1, d), mod), pl.BlockSpec((None, 1, d), mod),
                   pl.BlockSpec((d, LANES), full)])
    return pl.pallas_call(
        functools.partial(_proj_ln_kernel, n_in=n_in, sub=LANES), grid=(t // tm,), in_specs=in_specs,
        out_specs=[pl.BlockSpec((tm, d), row), pl.BlockSpec((tm, d), row),
                   pl.BlockSpec((N_EXPERTS, tm), lambda i: (0, i))],
        out_shape=[jax.ShapeDtypeStruct((t, d), F32), jax.ShapeDtypeStruct((t, d), BF16),
                   jax.ShapeDtypeStruct((N_EXPERTS, t), F32)],
        compiler_params=_cparams(1), name=name,
    )(*acts, *ws, x2d, gate.reshape(nmod, 1, d), ln_g.reshape(1, d), ln_b.reshape(1, d),
      sc.reshape(nmod, 1, d), sh.reshape(nmod, 1, d), router_wt)


def _first_argmax(vals):
    idx = jnp.zeros(vals[0].shape, I32)
    best = vals[0]
    for j in range(1, len(vals)):
        upd = vals[j] > best
        idx = jnp.where(upd, j, idx)
        best = jnp.where(upd, vals[j], best)
    return idx, best


def _route_kernel(lg_ref, bias_ref, pos_ref, w_ref, tab_ref, cnt_ref, carry_ref, off_ref, *, tr):
    @pl.when(pl.program_id(0) == 0)
    def _():
        carry_ref[...] = jnp.zeros_like(carry_ref)

    lg = lg_ref[...]
    ex = jnp.exp(lg - jnp.max(lg, 0, keepdims=True))
    scores = ex / jnp.sum(ex, 0, keepdims=True)
    sel = scores + bias_ref[...]
    rows = [sel[e:e + 1] for e in range(N_EXPERTS)]
    group_scores = []
    for g in range(N_GROUPS):
        r = rows[g * EXPERTS_PER_GROUP:(g + 1) * EXPERTS_PER_GROUP]
        best = None
        for i in range(EXPERTS_PER_GROUP):
            for j in range(i + 1, EXPERTS_PER_GROUP):
                s = r[i] + r[j]
                best = s if best is None else jnp.maximum(best, s)
        group_scores.append(best)
    grp, _ = _first_argmax(group_scores)
    vals = []
    for j in range(EXPERTS_PER_GROUP):
        v = rows[(N_GROUPS - 1) * EXPERTS_PER_GROUP + j]
        for g in range(N_GROUPS - 2, -1, -1):
            v = jnp.where(grp == g, rows[g * EXPERTS_PER_GROUP + j], v)
        vals.append(v)
    i0, _ = _first_argmax(vals)
    i1, _ = _first_argmax([jnp.where(i0 == j, -jnp.inf, vals[j]) for j in range(EXPERTS_PER_GROUP)])
    e0 = grp * EXPERTS_PER_GROUP + i0
    e1 = grp * EXPERTS_PER_GROUP + i1
    eid = lax.broadcasted_iota(I32, (N_EXPERTS, 1), 0)
    oh0, oh1 = eid == e0, eid == e1
    s0 = jnp.sum(jnp.where(oh0, scores, 0.0), 0, keepdims=True)
    s1 = jnp.sum(jnp.where(oh1, scores, 0.0), 0, keepdims=True)
    den = s0 + s1
    member = jnp.where(oh0 | oh1, 1.0, 0.0)
    before = lax.broadcasted_iota(I32, (tr, tr), 0) < lax.broadcasted_iota(I32, (tr, tr), 1)
    upper = jnp.where(before, 1.0, 0.0).astype(BF16)
    cnt = jnp.dot(member.astype(BF16), upper, preferred_element_type=F32)
    run = jnp.sum(member, 1, keepdims=True)
    run = jnp.floor((run + (ROW_ALIGN - 1)) * (1.0 / ROW_ALIGN)) * ROW_ALIGN
    run = jnp.broadcast_to(run, (N_EXPERTS, LANES))
    acc = jnp.zeros((1, LANES), F32)
    for e in range(N_EXPERTS):
        off_ref[e:e + 1, :] = acc
        acc = acc + run[e:e + 1]
    off = off_ref[...]
    at = off[:, 0:1] + cnt
    pos_ref[0:1, :] = jnp.sum(jnp.where(oh0, at, 0.0), 0, keepdims=True).astype(I32)
    pos_ref[1:2, :] = jnp.sum(jnp.where(oh1, at, 0.0), 0, keepdims=True).astype(I32)
    w_ref[0:1, :] = s0 / den
    w_ref[1:2, :] = s1 / den
    tab_ref[0] = off.astype(I32)
    tab_ref[1] = run.astype(I32)
    tab_ref[2] = carry_ref[...].astype(I32)
    carry_ref[...] = carry_ref[...] + run
    cnt_ref[...] = carry_ref[...]


def _route(logits_t, router_bias, tr):
    t = logits_t.shape[1]
    tok = lambda i: (0, i)
    return pl.pallas_call(
        functools.partial(_route_kernel, tr=tr), grid=(t // tr,),
        in_specs=[pl.BlockSpec((N_EXPERTS, tr), tok), pl.BlockSpec((N_EXPERTS, 1), lambda i: (0, 0))],
        out_specs=[pl.BlockSpec((2, tr), tok), pl.BlockSpec((2, tr), tok),
                   pl.BlockSpec((None, 3, N_EXPERTS, LANES), lambda i: (i, 0, 0, 0)),
                   pl.BlockSpec((N_EXPERTS, LANES), lambda i: (0, 0))],
        out_shape=[jax.ShapeDtypeStruct((2, t), I32), jax.ShapeDtypeStruct((2, t), F32),
                   jax.ShapeDtypeStruct((t // tr, 3, N_EXPERTS, LANES), I32),
                   jax.ShapeDtypeStruct((N_EXPERTS, LANES), F32)],
        scratch_shapes=[pltpu.VMEM((N_EXPERTS, LANES), F32), pltpu.VMEM((N_EXPERTS, LANES), F32)],
        compiler_params=_cparams(1), name="route",
    )(logits_t, router_bias.reshape(N_EXPERTS, 1))


def _run_copies(n, max_rows, make_copy, wait):
    sz = max_rows
    while sz >= ROW_ALIGN:
        start = (n // (2 * sz)) * (2 * sz)

        @pl.when((n & sz) != 0)
        def _(start=start, sz=sz):
            cp = make_copy(start, sz)
            cp.wait() if wait else cp.start()

        sz //= 2


def _wait_rows(total, max_rows, make_copy):
    sz = max_rows
    while sz >= ROW_ALIGN:
        @pl.when((total & sz) != 0)
        def _(sz=sz):
            make_copy(sz).wait()

        sz //= 2


def _rows(ref, start, size):
    return ref.at[pl.ds(pl.multiple_of(start, ROW_ALIGN), size)]


def _dispatch_kernel(tab_ref, h_ref, pos_ref, w_ref, xs_ref, buf, zbuf, sem, *, tt, n_tiles, bm):
    i = pl.program_id(0)
    d = h_ref.shape[1]
    sb = buf.shape[1]
    ne = N_EXPERTS
    fill = 3 * n_tiles * ne

    @pl.when(i == 0)
    def _():
        zbuf[...] = jnp.zeros_like(zbuf)
        for wait in (False, True):
            for e in range(ne):
                dst, n = tab_ref[fill + e], tab_ref[fill + ne + e]
                _run_copies(n, bm // 2, lambda s, z: pltpu.make_async_copy(
                    zbuf.at[pl.ds(0, z)], _rows(xs_ref, dst + s, z), sem.at[2]), wait)

        def zero_block(j, carry):
            for half in range(2):
                cp = pltpu.make_async_copy(zbuf, _rows(xs_ref, j * bm + half * (bm // 2), bm // 2), sem.at[2])
                cp.start()
                cp.wait()
            return carry

        lax.fori_loop(tab_ref[fill + 2 * ne], xs_ref.shape[0] // bm, zero_block, 0)

    def tile_copies(tile, slot, wait):
        for e in range(ne):
            off = tab_ref[tile * ne + e]
            n = tab_ref[(n_tiles + tile) * ne + e]
            dst = tab_ref[(2 * n_tiles + tile) * ne + e]
            _run_copies(n, tt, lambda s, z: pltpu.make_async_copy(
                _rows(buf.at[slot], off + s, z), _rows(xs_ref, dst + s, z), sem.at[slot]), wait)

    slot = i % 2
    pos, w = pos_ref[...], w_ref[...]
    hb = h_ref[...].astype(BF16)
    for r0 in range(0, sb, LANES):
        srow = r0 + lax.broadcasted_iota(I32, (LANES, 1), 0)
        m0, m1 = srow == pos[0:1], srow == pos[1:2]
        perm = (jnp.where(m0, 1.0, 0.0) + jnp.where(m1, 1.0, 0.0)).astype(BF16)
        buf[slot, r0:r0 + LANES, :d] = jnp.dot(perm, hb, preferred_element_type=F32)
        wrow = jnp.sum(jnp.where(m0, w[0:1], 0.0) + jnp.where(m1, w[1:2], 0.0), 1, keepdims=True)
        buf[slot, r0:r0 + LANES, d:] = jnp.broadcast_to(wrow, (LANES, LANES))

    def wait_tile(tile, slot):
        total = tab_ref[fill + 2 * ne + 1 + tile]
        _wait_rows(total, tt * 2, lambda z: pltpu.make_async_copy(
            buf.at[slot, pl.ds(0, z)], xs_ref.at[pl.ds(0, z)], sem.at[slot]))

    @pl.when(i > 0)
    def _():
        wait_tile(i - 1, 1 - slot)

    tile_copies(i, slot, False)

    @pl.when(i == n_tiles - 1)
    def _():
        wait_tile(i, slot)


def _dispatch(tab, h2, pos, wts, n_slots, tt, sb, bm):
    t, d = h2.shape
    n_tiles = t // tt
    return pl.pallas_call(
        functools.partial(_dispatch_kernel, tt=tt, n_tiles=n_tiles, bm=bm),
        grid_spec=pltpu.PrefetchScalarGridSpec(
            num_scalar_prefetch=1, grid=(n_tiles,),
            in_specs=[pl.BlockSpec((tt, d), lambda i, s: (i, 0)), pl.BlockSpec((2, tt), lambda i, s: (0, i)),
                      pl.BlockSpec((2, tt), lambda i, s: (0, i))],
            out_specs=pl.BlockSpec(memory_space=pl.ANY),
            scratch_shapes=[pltpu.VMEM((2, sb, d + LANES), F32), pltpu.VMEM((bm // 2, d + LANES), F32),
                            pltpu.SemaphoreType.DMA((3,))]),
        out_shape=jax.ShapeDtypeStruct((n_slots, d + LANES), F32),
        compiler_params=_cparams(1), name="moe_dispatch",
    )(tab, h2, pos, wts)


def _ffn_kernel(blk_e_ref, blk_rows_ref, next_e_ref, nused_ref, x_ref, wg_hbm, wu_hbm, wd_hbm, o_ref,
                stage, wg_bf, wu_bf, wd_bf, sem, *, layer, parts):
    del nused_ref
    i = pl.program_id(0)
    e = blk_e_ref[i]
    new_expert = (i == 0) | (e != blk_e_ref[jnp.maximum(i - 1, 0)])

    def fetch(expert, wait):
        for j, src in enumerate((wg_hbm, wu_hbm, wd_hbm)):
            cp = pltpu.make_async_copy(src.at[layer, expert], stage.at[j], sem.at[j])
            cp.wait() if wait else cp.start()

    @pl.when(i == 0)
    def _():
        fetch(e, False)

    @pl.when(new_expert)
    def _():
        fetch(e, True)
        wg_bf[...] = stage[0].astype(BF16)
        wu_bf[...] = stage[1].astype(BF16)
        wd_bf[...] = stage[2].astype(BF16)

        @pl.when(next_e_ref[i] >= 0)
        def _():
            fetch(next_e_ref[i], False)

    d = wg_bf.shape[0]
    rows = blk_rows_ref[i]
    part = x_ref.shape[0] // parts
    spans = [slice(p * part, (p + 1) * part) for p in range(parts)]

    def gate_up(rs):
        x = x_ref[rs, :d].astype(BF16)
        return (jnp.dot(x, wg_bf[...], preferred_element_type=F32), jnp.dot(x, wu_bf[...], preferred_element_type=F32))

    def finish(rs, gate, up):
        act = (gate * jax.nn.sigmoid(gate) * up).astype(BF16)
        o_ref[rs, :] = jnp.dot(act, wd_bf[...], preferred_element_type=F32) * x_ref[rs, d:d + 1]

    all_parts = rows > (parts - 1) * part

    @pl.when(all_parts)
    def _():
        nxt = gate_up(spans[0])
        for p, rs in enumerate(spans):
            cur = nxt
            if p + 1 < parts:
                nxt = gate_up(spans[p + 1])
            finish(rs, *cur)

    for p, rs in enumerate(spans):
        @pl.when(jnp.logical_not(all_parts) & (rows > p * part))
        def _(rs=rs):
            finish(rs, *gate_up(rs))

        @pl.when(rows <= p * part)
        def _(rs=rs):
            o_ref[rs, :] = jnp.zeros((part, d), F32)


def _ffn(blk_e, blk_rows, next_e, n_used, xs, wg, wu, wd, layer, bm, parts):
    n_slots, xw = xs.shape
    d, ff = wg.shape[2:]
    assert d == ff
    xrow = lambda i, be, br, ne, nu: (jnp.minimum(i, nu[0] - 1), 0)
    hbm = pl.BlockSpec(memory_space=pl.ANY)
    return pl.pallas_call(
        functools.partial(_ffn_kernel, layer=layer, parts=parts),
        grid_spec=pltpu.PrefetchScalarGridSpec(
            num_scalar_prefetch=4, grid=(n_slots // bm,),
            in_specs=[pl.BlockSpec((bm, xw), xrow), hbm, hbm, hbm],
            out_specs=pl.BlockSpec((bm, d), lambda i, be, br, ne, nu: (i, 0)),
            scratch_shapes=[pltpu.VMEM((3, d, ff), F32), pltpu.VMEM((d, ff), BF16), pltpu.VMEM((d, ff), BF16),
                            pltpu.VMEM((ff, d), BF16), pltpu.SemaphoreType.DMA((3,))]),
        out_shape=jax.ShapeDtypeStruct((n_slots, d), F32),
        compiler_params=_cparams(1), name="moe_ffn",
    )(blk_e, blk_rows, next_e, n_used, xs, wg, wu, wd)


def _combine_kernel(tab_ref, ys_ref, pos_ref, x1_ref, gate_ref, lng_ref, lnb_ref, o_ref, ybuf, sem, *, tt, n_tiles):
    i = pl.program_id(0)
    sb = ybuf.shape[1]
    ne = N_EXPERTS
    slot = i % 2

    def tile_copies(tile, slot, wait):
        for e in range(ne):
            off = tab_ref[tile * ne + e]
            n = tab_ref[(n_tiles + tile) * ne + e]
            src = tab_ref[(2 * n_tiles + tile) * ne + e]
            _run_copies(n, tt, lambda s, z: pltpu.make_async_copy(
                _rows(ys_ref, src + s, z), _rows(ybuf.at[slot], off + s, z), sem.at[slot]), wait)

    @pl.when(i == 0)
    def _():
        ybuf[...] = jnp.zeros_like(ybuf)
        tile_copies(0, 0, False)

    @pl.when(i + 1 < n_tiles)
    def _():
        tile_copies(i + 1, 1 - slot, False)

    total = tab_ref[3 * n_tiles * ne + 2 * ne + 1 + i]
    _wait_rows(total, tt * 2, lambda z: pltpu.make_async_copy(
        ys_ref.at[pl.ds(0, z)], ybuf.at[slot, pl.ds(0, z)], sem.at[slot]))
    scol = lax.broadcasted_iota(I32, (1, sb), 1)
    yb = ybuf[slot].astype(BF16)

    def moe_rows(r0):
        pos = pos_ref[r0:r0 + LANES, :]
        unperm = (jnp.where(scol == pos[:, 0:1], 1.0, 0.0) + jnp.where(scol == pos[:, 1:2], 1.0, 0.0)).astype(BF16)
        return jnp.dot(unperm, yb, preferred_element_type=F32)

    starts = list(range(0, tt, LANES))
    nxt = moe_rows(starts[0])
    for j, r0 in enumerate(starts):
        f = nxt
        if j + 1 < len(starts):
            nxt = moe_rows(starts[j + 1])
        o_ref[r0:r0 + LANES, :] = _layer_norm(DEEPNORM_ALPHA * x1_ref[r0:r0 + LANES, :] + gate_ref[...] * f,
                                              lng_ref[...], lnb_ref[...])


def _combine(tab, ys, pos_t, x1, gate, ln_g, ln_b, rows_per_mod, tt, sb):
    t, d = x1.shape
    tiles_per_mod = rows_per_mod // tt
    nmod = gate.shape[0]
    return pl.pallas_call(
        functools.partial(_combine_kernel, tt=tt, n_tiles=t // tt),
        grid_spec=pltpu.PrefetchScalarGridSpec(
            num_scalar_prefetch=1, grid=(t // tt,),
            in_specs=[pl.BlockSpec(memory_space=pl.ANY),
                      pl.BlockSpec((tt, 2), lambda i, s: (i, 0)),
                      pl.BlockSpec((tt, d), lambda i, s: (i, 0)),
                      pl.BlockSpec((None, 1, d), lambda i, s: (i // tiles_per_mod, 0, 0)),
                      pl.BlockSpec((1, d), lambda i, s: (0, 0)), pl.BlockSpec((1, d), lambda i, s: (0, 0))],
            out_specs=pl.BlockSpec((tt, d), lambda i, s: (i, 0)),
            scratch_shapes=[pltpu.VMEM((2, sb, d), F32), pltpu.SemaphoreType.DMA((2,))]),
        out_shape=jax.ShapeDtypeStruct((t, d), F32),
        compiler_params=_cparams(1), name="moe_combine",
    )(tab, ys, pos_t, x1, gate.reshape(nmod, 1, d), ln_g.reshape(1, d), ln_b.reshape(1, d))


def _moe(h2, logits_t, x1, gate, ln_g, ln_b, router_bias, wg, wu, wd, layer, rows_per_mod):
    t, d = h2.shape
    bm, tt, ne = MOE_BM, MOE_TILE, N_EXPERTS
    n_tiles = t // tt
    pad = ne * (ROW_ALIGN - 1)
    sb = -(-(2 * tt + pad) // LANES) * LANES
    pos, wts, tab3, cnt = _route(logits_t, router_bias, tr=tt)
    rows = cnt[:, 0].astype(I32)
    prows = (rows + bm - 1) // bm * bm
    pends = jnp.cumsum(prows)
    pstarts = pends - prows
    n_blk = -(-(2 * t + n_tiles * pad + ne * (bm - 1)) // bm)
    n_used = pends[-1] // bm
    blk_ids = jnp.minimum(jnp.arange(n_blk, dtype=I32), n_used - 1)
    blk_e = jnp.minimum(jnp.sum((blk_ids[:, None] * bm >= pends[None, :]).astype(I32), 1), ne - 1)
    eids = jnp.arange(ne, dtype=I32)
    of_blk = lambda per_expert: jnp.sum(jnp.where(blk_e[:, None] == eids[None, :], per_expert[None, :], 0), 1)
    raw_ids = jnp.arange(n_blk, dtype=I32)
    blk_rows = jnp.where(raw_ids < n_used, jnp.clip(of_blk(pstarts + rows) - raw_ids * bm, 0, bm), 0)
    later_used = (eids[None, :] > eids[:, None]) & (prows[None, :] > 0)
    next_used = jnp.min(jnp.where(later_used, eids[None, :], ne), 1)
    next_e = of_blk(jnp.where(next_used < ne, next_used, -1))
    tab3 = tab3[:, :, :, 0]
    tab = jnp.concatenate([tab3[:, 0].reshape(-1), tab3[:, 1].reshape(-1),
                           (tab3[:, 2] + pstarts[None, :]).reshape(-1), pstarts + rows, prows - rows,
                           n_used.reshape(1), jnp.sum(tab3[:, 1], 1)]).astype(I32)
    xs = _dispatch(tab, h2, pos, wts, n_blk * bm, tt, sb, bm)
    ys = _ffn(blk_e, blk_rows.astype(I32), next_e.astype(I32), n_used.reshape(1).astype(I32), xs, wg, wu, wd,
              layer, bm, parts=2)
    return _combine(tab, ys, pos.T, x1, gate, ln_g, ln_b, rows_per_mod, tt, sb)


def _filter_mlp_kernel(z_ref, w1_ref, w2_ref, w3_ref, b_ref, sf_ref, o_ref):
    hp = lax.Precision.HIGHEST
    b, sf = b_ref[...], sf_ref[...]
    h = jnp.sin(sf[0:1] * (jnp.dot(z_ref[...], w1_ref[...], precision=hp, preferred_element_type=F32) + b[0:1]))
    h = jnp.sin(sf[1:2] * (jnp.dot(h, w2_ref[...], precision=hp, preferred_element_type=F32) + b[1:2]))
    o_ref[...] = jnp.sin(sf[2:3] * (jnp.dot(h, w3_ref[...], precision=hp, preferred_element_type=F32) + b[2:3]))


def _filter_mlp(zp, w_in_p, w_hid, b, sf):
    n = zp.shape[0]
    return pl.pallas_call(
        _filter_mlp_kernel, out_shape=jax.ShapeDtypeStruct((n, FILTER_HID), F32),
        compiler_params=pltpu.CompilerParams(vmem_limit_bytes=VMEM_LIMIT), name="hyena_filter_mlp",
    )(zp, w_in_p, w_hid[0], w_hid[1], b, sf)


def _filter_spec_kernel(hid_ref, wf_ref, wb_ref, dec_ref, fwd_ref, hr_ref, g_ref, nyq_ref, p_scr, q_scr, *, fb, kps):
    k = pl.program_id(2)
    n = hid_ref.shape[0]

    @pl.when(k == 0)
    def _():
        hp = lax.Precision.HIGHEST
        hid, dec = hid_ref[...], dec_ref[...]
        row = lax.broadcasted_iota(I32, (n, 1), 0)
        fw = jnp.dot(hid, wf_ref[...], precision=hp, preferred_element_type=F32) * dec
        bw = jnp.dot(hid, wb_ref[...], precision=hp, preferred_element_type=F32) * dec
        bw = jnp.where(row == 0, 0.0, bw)
        p = fw + bw
        p_scr[...] = p.astype(BF16)
        q_scr[...] = (fw - bw).astype(BF16)
        sign = (1 - 2 * (row & 1)).astype(F32)
        nyq_ref[...] = jnp.sum(p * sign, 0, keepdims=True)

    rowb = lax.broadcasted_iota(I32, (fb, 1), 0)
    for j in range(kps):
        hr_ref[j * fb:(j + 1) * fb, :] = jnp.dot(fwd_ref[j, 0:fb, :], p_scr[...], preferred_element_type=F32)
        gg = jnp.dot(fwd_ref[j, fb:2 * fb, :], q_scr[...], preferred_element_type=F32)
        if j == 0:
            gg = jnp.where((rowb == 0) & (k == 0), 0.0, gg)
        g_ref[j * fb:(j + 1) * fb, :] = gg


def _filter_spec(hid, w_out4, decay, fwd_tab, fb, tc, kps):
    n = hid.shape[0]
    d = D_MODEL
    kb = n // (fb * kps)
    return pl.pallas_call(
        functools.partial(_filter_spec_kernel, fb=fb, kps=kps), grid=(2, d // tc, kb),
        in_specs=[pl.BlockSpec((n, FILTER_HID), lambda o, c, k: (0, 0)),
                  pl.BlockSpec((None, None, FILTER_HID, tc), lambda o, c, k: (0, o, 0, c)),
                  pl.BlockSpec((None, None, FILTER_HID, tc), lambda o, c, k: (1, o, 0, c)),
                  pl.BlockSpec((n, tc), lambda o, c, k: (0, c)),
                  pl.BlockSpec((kps, 2 * fb, n), lambda o, c, k: (k, 0, 0))],
        out_specs=[pl.BlockSpec((None, kps * fb, tc), lambda o, c, k: (o, k, c)),
                   pl.BlockSpec((None, kps * fb, tc), lambda o, c, k: (o, k, c)),
                   pl.BlockSpec((None, 1, tc), lambda o, c, k: (o, 0, c))],
        out_shape=[jax.ShapeDtypeStruct((2, n, d), F32), jax.ShapeDtypeStruct((2, n, d), F32),
                   jax.ShapeDtypeStruct((2, 1, d), F32)],
        scratch_shapes=[pltpu.VMEM((n, tc), BF16), pltpu.VMEM((n, tc), BF16)],
        compiler_params=_cparams(3), name="hyena_filter_spec",
    )(hid, w_out4, w_out4, decay, fwd_tab)


def _short_conv(u_ref, cw_ref, cb_ref):
    n = u_ref.shape[0]
    u = u_ref[...].astype(F32)
    row = lax.broadcasted_iota(I32, (n, 1), 0)
    prev = jnp.where(row == 0, 0.0, pltpu.roll(u, 1, 0))
    nxt = jnp.where(row == n - 1, 0.0, pltpu.roll(u, n - 1, 0))
    cw = cw_ref[...]
    return prev * cw[0:1] + u * cw[1:2] + nxt * cw[2:3] + cb_ref[...]


def _hyena_conv_kernel(uv_ref, ux1_ref, ux2_ref, cwv_ref, cwx1_ref, cwx2_ref, cbv_ref, cbx1_ref, cbx2_ref,
                       fwd_ref, inv_ref, hr_ref, g_ref, nyq_ref, sk_ref, o_ref, zb_scr, z32_scr, acc_scr, *, fb, kb,
                       kps):
    o = pl.program_id(2)
    k = pl.program_id(3)

    @pl.when((o == 0) & (k == 0))
    def _():
        v = _short_conv(uv_ref, cwv_ref, cbv_ref)
        z32_scr[...] = v
        zb_scr[...] = v.astype(BF16)

    @pl.when(k == 0)
    def _():
        acc_scr[...] = jnp.zeros_like(acc_scr)

    zb = zb_scr[...]
    zfs = [jnp.dot(fwd_ref[j], zb, preferred_element_type=F32) for j in range(kps)]
    rowb = lax.broadcasted_iota(I32, (fb, 1), 0)
    for j in range(kps):
        zr, zi = zfs[j][:fb], zfs[j][fb:]
        hr, gg = hr_ref[j * fb:(j + 1) * fb, :], g_ref[j * fb:(j + 1) * fb, :]
        hb = jnp.where((rowb == 0) & (k == 0), nyq_ref[...], hr) if j == 0 else hr
        y = jnp.concatenate([zr * hr - zi * gg, zr * gg + zi * hb], 0).astype(BF16)
        acc_scr[...] += jnp.dot(inv_ref[j], y, preferred_element_type=F32)

    @pl.when((k == kb - 1) & (o == 0))
    def _():
        zn = _short_conv(ux1_ref, cwx1_ref, cbx1_ref) * (acc_scr[...] + z32_scr[...] * sk_ref[...])
        z32_scr[...] = zn
        zb_scr[...] = zn.astype(BF16)

    @pl.when((k == kb - 1) & (o == 1))
    def _():
        zn = _short_conv(ux2_ref, cwx2_ref, cbx2_ref) * (acc_scr[...] + z32_scr[...] * sk_ref[...])
        o_ref[...] = zn.astype(o_ref.dtype)


def _hyena_conv(u, conv_w, conv_b, fwd_tab, inv_tab, hr, gg, nyq, skip, fb, tc, kps):
    b, n, _ = u.shape
    d = D_MODEL
    kb = n // (fb * kps)
    ncb = d // tc
    ucol = lambda part: (lambda bi, c, o, k: (bi, 0, part * ncb + c))
    wcol = lambda part: (lambda bi, c, o, k: (0, part * ncb + c))
    spec = lambda bi, c, o, k: (o, k, c)
    per_o = lambda bi, c, o, k: (o, 0, c)
    return pl.pallas_call(
        functools.partial(_hyena_conv_kernel, fb=fb, kb=kb, kps=kps), grid=(b, ncb, 2, kb),
        in_specs=[pl.BlockSpec((None, n, tc), ucol(0)), pl.BlockSpec((None, n, tc), ucol(1)),
                  pl.BlockSpec((None, n, tc), ucol(2)),
                  pl.BlockSpec((3, tc), wcol(0)), pl.BlockSpec((3, tc), wcol(1)), pl.BlockSpec((3, tc), wcol(2)),
                  pl.BlockSpec((1, tc), wcol(0)), pl.BlockSpec((1, tc), wcol(1)), pl.BlockSpec((1, tc), wcol(2)),
                  pl.BlockSpec((kps, 2 * fb, n), lambda bi, c, o, k: (k, 0, 0)),
                  pl.BlockSpec((kps, n, 2 * fb), lambda bi, c, o, k: (k, 0, 0)),
                  pl.BlockSpec((None, kps * fb, tc), spec), pl.BlockSpec((None, kps * fb, tc), spec),
                  pl.BlockSpec((None, 1, tc), per_o), pl.BlockSpec((None, 1, tc), per_o)],
        out_specs=pl.BlockSpec((None, n, tc), lambda bi, c, o, k: (bi, 0, c)),
        out_shape=jax.ShapeDtypeStruct((b, n, d), BF16),
        scratch_shapes=[pltpu.VMEM((n, tc), BF16), pltpu.VMEM((n, tc), F32), pltpu.VMEM((n, tc), F32)],
        compiler_params=_cparams(4), name="hyena_conv",
    )(u, u, u, conv_w, conv_w, conv_w, conv_b, conv_b, conv_b, fwd_tab, inv_tab, hr, gg, nyq,
      skip.reshape(2, 1, d))


def _lag_spec_kernel(hid_ref, wf_ref, wb_ref, dec_ref, ph_ref, h_ref, fw_scr, bw_scr, *, s, nb):
    lag = pl.program_id(2)
    n = hid_ref.shape[0]

    @pl.when(lag == 0)
    def _():
        hp = lax.Precision.HIGHEST
        hid, dec = hid_ref[...], dec_ref[...]
        row = lax.broadcasted_iota(I32, (n, 1), 0)
        fw = jnp.dot(hid, wf_ref[...], precision=hp, preferred_element_type=F32) * dec
        bw = jnp.dot(hid, wb_ref[...], precision=hp, preferred_element_type=F32) * dec
        fw_scr[...] = fw.astype(BF16)
        bw_scr[...] = jnp.where(row == 0, 0.0, bw).astype(BF16)

    def piece(ph, scr, j):
        return jnp.dot(ph_ref[ph], scr[j * s:(j + 1) * s, :], preferred_element_type=F32)

    for idx in range(2 * nb - 1):
        m = idx - (nb - 1)

        @pl.when(lag == idx)
        def _(m=m):
            if m >= 1:
                h_ref[...] = piece(0, fw_scr, m) + piece(1, fw_scr, m - 1)
            elif m == 0:
                h_ref[...] = piece(0, fw_scr, 0) + piece(2, bw_scr, 0)
            else:
                h_ref[...] = piece(2, bw_scr, -m) + piece(3, bw_scr, -m - 1)


def _lag_spec(hid, w_out4, decay, phases, s, tc):
    n = hid.shape[0]
    d = D_MODEL
    nb = n // s
    return pl.pallas_call(
        functools.partial(_lag_spec_kernel, s=s, nb=nb), grid=(2, d // tc, 2 * nb - 1),
        in_specs=[pl.BlockSpec((n, FILTER_HID), lambda o, c, l: (0, 0)),
                  pl.BlockSpec((None, None, FILTER_HID, tc), lambda o, c, l: (0, o, 0, c)),
                  pl.BlockSpec((None, None, FILTER_HID, tc), lambda o, c, l: (1, o, 0, c)),
                  pl.BlockSpec((n, tc), lambda o, c, l: (0, c)),
                  pl.BlockSpec((4, 2 * s, s), lambda o, c, l: (0, 0, 0))],
        out_specs=pl.BlockSpec((None, None, 2 * s, tc), lambda o, c, l: (o, l, 0, c)),
        out_shape=jax.ShapeDtypeStruct((2, 2 * nb - 1, 2 * s, d), F32),
        scratch_shapes=[pltpu.VMEM((n, tc), BF16), pltpu.VMEM((n, tc), BF16)],
        compiler_params=_cparams(3), name="hyena_lag_spec",
    )(hid, w_out4, w_out4, decay, phases)


def _part_conv_kernel(*refs, s, nb, src_conv, rc):
    if src_conv:
        src_ref, gate_ref, cws_ref, cbs_ref, cwg_ref, cbg_ref, f_ref, finv_ref, h_ref, sk_ref, o_ref = refs[:11]
    else:
        src_ref, gate_ref, cwg_ref, cbg_ref, f_ref, finv_ref, h_ref, sk_ref, o_ref = refs[:9]
    src_scr, gate_scr, zb_scr, zall, ybuf = refs[-5:]
    src = _short_conv(src_ref, cws_ref, cbs_ref) if src_conv else src_ref[...].astype(F32)
    src_scr[...] = src
    zb_scr[...] = src.astype(BF16)
    gate_scr[...] = _short_conv(gate_ref, cwg_ref, cbg_ref)
    for j in range(nb):
        zall[j] = jnp.dot(f_ref[...], zb_scr[j * s:(j + 1) * s, :], preferred_element_type=F32)
    sk = sk_ref[...]
    for i in range(nb):
        pairs = [(j, i - j + nb - 1) for j in range(nb)]
        for r0 in range(0, s, rc):
            re, im = slice(r0, r0 + rc), slice(s + r0, s + r0 + rc)
            yr = yi = None
            for j, m in pairs:
                zr, zi, hr, g = zall[j, re, :], zall[j, im, :], h_ref[m, re, :], h_ref[m, im, :]
                tr, ti = zr * hr - zi * g, zr * g + zi * hr
                yr, yi = (tr, ti) if yr is None else (yr + tr, yi + ti)
            ybuf[re, :] = yr
            ybuf[im, :] = yi
        dc = nyq = None
        for j, m in pairs:
            t0, t1 = zall[j, 0:1, :] * h_ref[m, 0:1, :], zall[j, s:s + 1, :] * h_ref[m, s:s + 1, :]
            dc, nyq = (t0, t1) if dc is None else (dc + t0, nyq + t1)
        ybuf[0:1, :] = dc
        ybuf[s:s + 1, :] = nyq
        y = jnp.dot(finv_ref[...], ybuf[...].astype(BF16), preferred_element_type=F32)
        rows = slice(i * s, (i + 1) * s)
        o_ref[rows, :] = (gate_scr[rows, :] * (y + src_scr[rows, :] * sk)).astype(o_ref.dtype)


def _part_conv(src, src_part, u, gate_part, conv_w, conv_b, f_tab, finv_tab, spec, skip, order, s, tc):
    b, n, _ = u.shape
    d = D_MODEL
    ncb = d // tc
    nb = n // s
    src_conv = src_part is not None
    col = lambda part: (lambda c, bi: (bi, 0, part * ncb + c))
    wcol = lambda part: (lambda c, bi: (0, part * ncb + c))
    const = lambda c, bi: (0, 0)
    args = [src, u]
    in_specs = [pl.BlockSpec((None, n, tc), col(src_part if src_conv else 0)), pl.BlockSpec((None, n, tc), col(gate_part))]
    if src_conv:
        args += [conv_w, conv_b]
        in_specs += [pl.BlockSpec((3, tc), wcol(src_part)), pl.BlockSpec((1, tc), wcol(src_part))]
    args += [conv_w, conv_b, f_tab, finv_tab, spec, skip.reshape(2, 1, d)]
    in_specs += [pl.BlockSpec((3, tc), wcol(gate_part)), pl.BlockSpec((1, tc), wcol(gate_part)),
                 pl.BlockSpec((2 * s, s), const), pl.BlockSpec((s, 2 * s), const),
                 pl.BlockSpec((None, 2 * nb - 1, 2 * s, tc), lambda c, bi: (order, 0, 0, c),
                              pipeline_mode=pl.Buffered(1)),
                 pl.BlockSpec((None, 1, tc), lambda c, bi: (order, 0, c))]
    return pl.pallas_call(
        functools.partial(_part_conv_kernel, s=s, nb=nb, src_conv=src_conv, rc=32), grid=(ncb, b),
        in_specs=in_specs,
        out_specs=pl.BlockSpec((None, n, tc), lambda c, bi: (bi, 0, c)),
        out_shape=jax.ShapeDtypeStruct((b, n, d), BF16),
        scratch_shapes=[pltpu.VMEM((n, tc), F32), pltpu.VMEM((n, tc), F32), pltpu.VMEM((n, tc), BF16),
                        pltpu.VMEM((nb, 2 * s, tc), F32), pltpu.VMEM((2 * s, tc), F32)],
        compiler_params=_cparams(2), name="hyena_part_conv",
    )(*args)


def kernel(x, c, ctx, c_ctx, mod_w, mod_b, ln_g, ln_b, attn_w_in, attn_lambda, attn_subln_g, attn_sink, attn_w_out,
           hy_w_in, hy_conv_w, hy_conv_b, hy_ffn_w_in, hy_ffn_w_hid, hy_ffn_b, hy_sin_freq, hy_ffn_w_out, hy_skip,
           hy_w_out, router_w, router_bias, exp_w_gate, exp_w_up, exp_w_down):
    b, n, d = x.shape
    nc = ctx.shape[1]
    t = b * n
    assert d == D_MODEL and b + 1 <= MOD_ROWS and n % 512 == 0 and nc % 256 == 0

    c_rows = jnp.zeros((MOD_ROWS, d), F32).at[:b].set(c).at[b].set(c_ctx)
    mods = _mods(c_rows, mod_w, mod_b)
    router_wt = jnp.pad(router_w, ((0, 0), (0, LANES - N_EXPERTS)))
    x2d = x.reshape(t, d)

    sh1, sc1, g1, sh2, sc2, g2 = jnp.split(mods[0], 6, axis=-1)
    cos, sin = _rope_tables(n)
    w_in = attn_w_in[0].astype(BF16)
    q_groups = tuple(range(Q_W // LANES))
    rope_groups = q_groups + tuple(range(Q_W // LANES, (Q_W + DIFF_QK_W) // LANES)) + (
        (Q_W + DIFF_QK_W + DIFF_VW) // LANES,)
    q_scales = tuple((g, HEAD_DIM ** -0.5 * LOG2_E) for g in q_groups)
    proj = _modmm(x2d, sc1[:b], sh1[:b], w_in, rows_per_mod=n, tm=1024,
                  rope=(jnp.asarray(cos), jnp.asarray(sin), rope_groups, q_scales), name="attn_in_proj")
    proj_c = _modmm(ctx.reshape(b * nc, d), sc1[b:b + 1], sh1[b:b + 1], w_in[:, Q_W:], rows_per_mod=b * nc, tm=256,
                    name="ctx_in_proj")
    proj = proj.reshape(b, n, ATTN_PROJ_W)
    proj_c = proj_c.reshape(b, nc, KV_W)
    lam_init = 0.8 - 0.6 * math.exp(-0.3 * 0)
    oa = _diff_attn(proj, proj_c, attn_lambda[0], attn_subln_g[0], lam_init, tq=1024, sub=128)
    ow = _win_attn(proj, proj_c, attn_sink[0], tq=256)
    w_out = attn_w_out[0].astype(BF16)
    x1, h2, lgt = _proj_ln([oa.reshape(t, DIFF_VW), ow.reshape(t, WIN_Q_W)], [w_out[:DIFF_VW], w_out[DIFF_VW:]],
                           x2d, g1[:b], ln_g[0, 0], ln_b[0, 0], sc2[:b], sh2[:b], router_wt, rows_per_mod=n, tm=1024,
                           name="attn_out_proj_ln")
    x2 = _moe(h2, lgt, x1, g2[:b], ln_g[0, 1], ln_b[0, 1], router_bias, exp_w_gate, exp_w_up, exp_w_down, layer=0,
              rows_per_mod=n)

    sh1, sc1, g1, sh2, sc2, g2 = jnp.split(mods[1], 6, axis=-1)
    u = _modmm(x2, sc1[:b], sh1[:b], hy_w_in[0].astype(BF16), rows_per_mod=n, tm=1024, name="hyena_in_proj")
    part, tc = HYENA_PART, 256
    phases_np, finv_np = _partition_tables(part)
    phases, finv_tab = jnp.asarray(phases_np).astype(BF16), jnp.asarray(finv_np).astype(BF16)
    zp, decay = _filter_tables(n)
    w_in_p = jnp.zeros((FILTER_HID, FILTER_HID), F32).at[:FILTER_EMB].set(hy_ffn_w_in[0])
    hid = _filter_mlp(jnp.asarray(zp), w_in_p, hy_ffn_w_hid[0], hy_ffn_b[0], hy_sin_freq[0])
    spec = _lag_spec(hid, hy_ffn_w_out[0].reshape(FILTER_HID, 2, 2, d).transpose(1, 2, 0, 3), jnp.asarray(decay),
                     phases, part, tc=512)
    u3 = u.reshape(b, n, 3 * d)
    conv_b = hy_conv_b[0].reshape(1, 3 * d)
    z1 = _part_conv(u3, 0, u3, 1, hy_conv_w[0], conv_b, phases[0], finv_tab, spec, hy_skip[0], 0, part, tc)
    z2 = _part_conv(z1, None, u3, 2, hy_conv_w[0], conv_b, phases[0], finv_tab, spec, hy_skip[0], 1, part, tc)
    x3, h2, lgt = _proj_ln([z2.reshape(t, d)], [hy_w_out[0].astype(BF16)], x2, g1[:b], ln_g[1, 0], ln_b[1, 0],
                           sc2[:b], sh2[:b], router_wt, rows_per_mod=n, tm=1024, name="hyena_out_proj_ln")
    x4 = _moe(h2, lgt, x3, g2[:b], ln_g[1, 1], ln_b[1, 1], router_bias, exp_w_gate, exp_w_up, exp_w_down, layer=1,
              rows_per_mod=n)
    return x4.reshape(b, n, d)
```

```python
import functools
import math

import jax
import jax.numpy as jnp
import numpy as np
from jax import lax
from jax.experimental import pallas as pl
from jax.experimental.pallas import tpu as pltpu

F32 = jnp.float32
BF16 = jnp.bfloat16
I32 = jnp.int32

D_MODEL = 1024
DEPTH = 2
GRID_W = 64
HEAD_DIM = 64
DIFF_HEADS = 4
WIN_Q_HEADS = 8
WIN_KV_HEADS = 2
WINDOW = 128
WIN_BLOCK = 128
ROPE_BASE = 10000.0
DIFF_QK_W = DIFF_HEADS * 2 * HEAD_DIM
DIFF_VW = DIFF_HEADS * 2 * HEAD_DIM
WIN_Q_W = WIN_Q_HEADS * HEAD_DIM
WIN_KV_W = WIN_KV_HEADS * HEAD_DIM
Q_W = DIFF_QK_W + WIN_Q_W
KV_W = DIFF_QK_W + DIFF_VW + 2 * WIN_KV_W
ATTN_PROJ_W = Q_W + KV_W
FILTER_EMB = 33
FILTER_HID = 64
DECAY_TARGET = 1e-2
FAST_DECAY_PCT = 0.3
SLOW_DECAY_PCT = 1.5
N_EXPERTS = 16
N_GROUPS = 4
EXPERTS_PER_GROUP = N_EXPERTS // N_GROUPS
EXPERT_FF = 1024
LN_EPS = 1e-5
DEEPNORM_ALPHA = (2 * DEPTH) ** 0.25
NEG_INF = -1e30
LOG2_E = 1.4426950408889634

LANES = 128
MOD_ROWS = 16
VMEM_LIMIT = 60 * 1024 * 1024
MOE_BM = 512
MOE_TILE = 512
HYENA_PART = 512
ROW_ALIGN = 8
NT_DIMS = (((1,), (1,)), ((), ()))


def _cparams(n_axes):
    return pltpu.CompilerParams(dimension_semantics=("arbitrary",) * n_axes, vmem_limit_bytes=VMEM_LIMIT)


@functools.lru_cache(maxsize=None)
def _rope_tables(n):
    rows = n // GRID_W
    r, col = np.meshgrid(np.arange(rows, dtype=np.float32), np.arange(GRID_W, dtype=np.float32), indexing="ij")
    axis_dim = HEAD_DIM // 2
    inv_freq = (ROPE_BASE ** (-np.arange(0, axis_dim, 2, dtype=np.float32) / axis_dim)).astype(np.float32)
    ang = np.concatenate([r.reshape(-1, 1) * inv_freq, col.reshape(-1, 1) * inv_freq], -1)
    ang = np.concatenate([ang, ang], -1).astype(np.float32)
    cos, sin = np.cos(ang), np.sin(ang)
    half = np.arange(HEAD_DIM) < HEAD_DIM // 2
    sin_signed = np.where(half[None, :], -sin, sin)
    reps = LANES // HEAD_DIM
    return (np.tile(cos, (1, reps)).astype(np.float32), np.tile(sin_signed, (1, reps)).astype(np.float32))


@functools.lru_cache(maxsize=None)
def _dft_tables(n, fb):
    big = 2 * n
    k = np.arange(n, dtype=np.int64)[:, None]
    t = np.arange(n, dtype=np.int64)[None, :]
    ang = ((k * t) % big).astype(np.float64) * (2.0 * math.pi / big)
    c, s = np.cos(ang), np.sin(ang)
    alt = (1 - 2 * (np.arange(n) & 1)).astype(np.float64)
    s_f = s.copy()
    s_f[0, :] = alt
    kb = n // fb
    fwd = np.concatenate([c.reshape(kb, fb, n), s_f.reshape(kb, fb, n)], axis=1)
    ci = c.T * (2.0 / big)
    ci[:, 0] = 1.0 / big
    si = s.T * (2.0 / big)
    si[:, 0] = alt / big
    inv = np.concatenate([ci.reshape(n, kb, fb).transpose(1, 0, 2), si.reshape(n, kb, fb).transpose(1, 0, 2)], axis=2)
    return fwd.astype(np.float32), inv.astype(np.float32)


@functools.lru_cache(maxsize=None)
def _filter_tables(n):
    t = np.linspace(0.0, 1.0, n, dtype=np.float32)[:, None]
    bands = (FILTER_EMB - 1) // 2
    w = (2.0 * math.pi * np.arange(n, dtype=np.float32)[:, None] / n).astype(np.float32)
    fr = np.linspace(1e-4, bands - 1, bands, dtype=np.float32)[None, :]
    z = np.concatenate([t, np.cos(fr * w), -np.sin(fr * w)], -1).astype(np.float32)
    zp = np.zeros((n, FILTER_HID), np.float32)
    zp[:, :FILTER_EMB] = z
    deltas = np.abs(np.linspace(math.log(DECAY_TARGET) / SLOW_DECAY_PCT, math.log(DECAY_TARGET) / FAST_DECAY_PCT,
                                D_MODEL, dtype=np.float32))
    decay = np.exp(-t * deltas[None, :]).astype(np.float32)
    return zp, decay


@functools.lru_cache(maxsize=None)
def _partition_tables(s):
    big = 2 * s
    k = np.arange(s, dtype=np.int64)[:, None]
    r = np.arange(s, dtype=np.int64)[None, :]

    def phase(pr, sin_sign, drop_r0):
        ang = ((k * pr) % big).astype(np.float64) * (2.0 * math.pi / big)
        c, sn = np.cos(ang), sin_sign * np.sin(ang)
        sn[0, :] = 1.0 - 2.0 * (pr[0] & 1)
        if drop_r0:
            c[:, 0] = 0.0
            sn[:, 0] = 0.0
        return np.concatenate([c, sn], 0)

    phases = np.stack([phase(r, 1.0, False), phase(s - r, -1.0, True), phase(r, -1.0, False), phase(s - r, 1.0, True)])
    ang = ((r.T * k.T) % big).astype(np.float64) * (2.0 * math.pi / big)
    ci = np.cos(ang) * (2.0 / big)
    ci[:, 0] = 1.0 / big
    si = np.sin(ang) * (2.0 / big)
    si[:, 0] = (1.0 - 2.0 * (np.arange(s) & 1)) / big
    return phases.astype(np.float32), np.concatenate([ci, si], 1).astype(np.float32)


def _mods_kernel(c_ref, w_ref, b_ref, o_ref):
    c = c_ref[...]
    a = (c * jax.nn.sigmoid(c)).astype(BF16)
    o_ref[...] = jnp.dot(a, w_ref[...].astype(BF16), preferred_element_type=F32) + b_ref[...]


def _mods(c_rows, mod_w, mod_b):
    d = D_MODEL
    tn = 1536
    return pl.pallas_call(
        _mods_kernel,
        grid=(DEPTH, 6 * d // tn),
        in_specs=[pl.BlockSpec((MOD_ROWS, d), lambda l, j: (0, 0)),
                  pl.BlockSpec((None, d, tn), lambda l, j: (l, 0, j)),
                  pl.BlockSpec((None, 1, tn), lambda l, j: (l, 0, j))],
        out_specs=pl.BlockSpec((None, MOD_ROWS, tn), lambda l, j: (l, 0, j)),
        out_shape=jax.ShapeDtypeStruct((DEPTH, MOD_ROWS, 6 * d), F32),
        compiler_params=_cparams(2), name="mods",
    )(c_rows, mod_w, mod_b.reshape(DEPTH, 1, 6 * d))


def _modmm_kernel(*refs, n_groups, rope_groups, scaled_groups, chunk):
    if rope_groups:
        x_ref, sc_ref, sh_ref, w_ref, cos_ref, sin_ref, o_ref = refs
        cos, sin = cos_ref[...], sin_ref[...]
        lane = lax.broadcasted_iota(I32, (1, LANES), 1)
        first_half = (lane % HEAD_DIM) < HEAD_DIM // 2
    else:
        x_ref, sc_ref, sh_ref, w_ref, o_ref = refs
    h = (x_ref[...] * (1.0 + sc_ref[...]) + sh_ref[...]).astype(BF16)
    scales = dict(scaled_groups)
    gpc = chunk // LANES
    for c in range(n_groups // gpc):
        acc = jnp.dot(h, w_ref[:, c * chunk:(c + 1) * chunk], preferred_element_type=F32)
        for j in range(gpc):
            g = c * gpc + j
            blk = acc[:, j * LANES:(j + 1) * LANES]
            if g in rope_groups:
                rot = jnp.where(first_half, pltpu.roll(blk, LANES - HEAD_DIM // 2, 1), pltpu.roll(blk, HEAD_DIM // 2, 1))
                blk = blk * cos + rot * sin
                if g in scales:
                    blk = blk * scales[g]
            o_ref[:, g * LANES:(g + 1) * LANES] = blk.astype(o_ref.dtype)


def _modmm(x2d, sc, sh, w, rows_per_mod, tm, rope=None, name="modmm"):
    t, d = x2d.shape
    n = w.shape[1]
    tiles_per_mod = rows_per_mod // tm
    nmod = sc.shape[0]
    in_specs = [pl.BlockSpec((tm, d), lambda i: (i, 0)),
                pl.BlockSpec((None, 1, d), lambda i: (i // tiles_per_mod, 0, 0)),
                pl.BlockSpec((None, 1, d), lambda i: (i // tiles_per_mod, 0, 0)),
                pl.BlockSpec((d, n), lambda i: (0, 0))]
    args = [x2d, sc.reshape(nmod, 1, d), sh.reshape(nmod, 1, d), w]
    rope_groups, scaled_groups = (), ()
    if rope is not None:
        cos, sin, rope_groups, scaled_groups = rope
        in_specs += [pl.BlockSpec((tm, LANES), lambda i: (i % tiles_per_mod, 0)),
                     pl.BlockSpec((tm, LANES), lambda i: (i % tiles_per_mod, 0))]
        args += [cos, sin]
    kern = functools.partial(_modmm_kernel, n_groups=n // LANES, rope_groups=tuple(rope_groups),
                             scaled_groups=tuple(scaled_groups), chunk=256)
    return pl.pallas_call(
        kern, grid=(t // tm,), in_specs=in_specs,
        out_specs=pl.BlockSpec((tm, n), lambda i: (i, 0)),
        out_shape=jax.ShapeDtypeStruct((t, n), BF16),
        compiler_params=_cparams(1), name=name,
    )(*args)


def _diff_attn_kernel(q_ref, k_ref, v_ref, kc_ref, vc_ref, lam_ref, g_ref, o_ref, *, lam_init, sub):
    lane = lax.broadcasted_iota(I32, (1, LANES), 1)
    k, kc = k_ref[...], kc_ref[...]
    v, vc = v_ref[...], vc_ref[...]

    def scores(r0):
        q = q_ref[r0:r0 + sub, :]
        zero = jnp.zeros_like(q)
        out = []
        for qm in (jnp.where(lane < HEAD_DIM, q, zero), jnp.where(lane >= HEAD_DIM, q, zero)):
            out.append((lax.dot_general(qm, k, NT_DIMS, preferred_element_type=F32),
                        lax.dot_general(qm, kc, NT_DIMS, preferred_element_type=F32)))
        return out

    def probs(sl, sc):
        m = jnp.maximum(jnp.max(sl, -1, keepdims=True), jnp.max(sc, -1, keepdims=True))
        pl_, pc = jnp.exp2(sl - m), jnp.exp2(sc - m)
        den = jnp.sum(pl_, -1, keepdims=True) + jnp.sum(pc, -1, keepdims=True)
        return pl_, pc, 1.0 / den

    lv = lam_ref[...]
    lam = (jnp.exp(jnp.sum(lv[0:1] * lv[1:2], keepdims=True)) - jnp.exp(jnp.sum(lv[2:3] * lv[3:4], keepdims=True))
           + lam_init)
    starts = list(range(0, q_ref.shape[0], sub))
    nxt = scores(starts[0])
    for i, r0 in enumerate(starts):
        cur = nxt
        if i + 1 < len(starts):
            nxt = scores(starts[i + 1])
        p1l, p1c, r1 = probs(*cur[0])
        p2l, p2c, r2 = probs(*cur[1])
        w2 = lam * r2
        al = (p1l * r1 - p2l * w2).astype(BF16)
        ac = (p1c * r1 - p2c * w2).astype(BF16)
        o = jnp.dot(al, v, preferred_element_type=F32) + jnp.dot(ac, vc, preferred_element_type=F32)
        ms = jnp.mean(o * o, -1, keepdims=True)
        o_ref[r0:r0 + sub, :] = (o * lax.rsqrt(ms + LN_EPS) * g_ref[...] * (1.0 - lam_init)).astype(o_ref.dtype)


def _diff_attn(proj, proj_c, lam_vec, subln_g, lam_init, tq, sub):
    b, n, _ = proj.shape
    nc = proj_c.shape[1]
    kcol = Q_W // LANES
    vcol = (Q_W + DIFF_QK_W) // LANES
    vccol = DIFF_QK_W // LANES
    kern = functools.partial(_diff_attn_kernel, lam_init=lam_init, sub=sub)
    return pl.pallas_call(
        kern, grid=(b, DIFF_HEADS, n // tq),
        in_specs=[pl.BlockSpec((None, tq, LANES), lambda bi, h, i: (bi, i, h)),
                  pl.BlockSpec((None, n, LANES), lambda bi, h, i: (bi, 0, kcol + h)),
                  pl.BlockSpec((None, n, LANES), lambda bi, h, i: (bi, 0, vcol + h)),
                  pl.BlockSpec((None, nc, LANES), lambda bi, h, i: (bi, 0, h)),
                  pl.BlockSpec((None, nc, LANES), lambda bi, h, i: (bi, 0, vccol + h)),
                  pl.BlockSpec((4, HEAD_DIM), lambda bi, h, i: (0, 0)),
                  pl.BlockSpec((1, LANES), lambda bi, h, i: (0, 0))],
        out_specs=pl.BlockSpec((None, tq, LANES), lambda bi, h, i: (bi, i, h)),
        out_shape=jax.ShapeDtypeStruct((b, n, DIFF_VW), BF16),
        compiler_params=_cparams(3), name="diff_attn",
    )(proj, proj, proj, proj_c, proj_c, lam_vec, subln_g.reshape(1, LANES))


def _win_attn_kernel(sink_ref, q_ref, k_ref, v_ref, kc_ref, vc_ref, o_ref, *, seq, tq):
    n = pl.program_id(1)
    gq = WIN_Q_HEADS // WIN_KV_HEADS
    kw = tq + 2 * WINDOW
    start = pl.multiple_of(jnp.clip(n * tq - WINDOW, 0, seq - kw), WINDOW)
    k_win, v_win = k_ref[pl.ds(start, kw), :], v_ref[pl.ds(start, kw), :]
    kc, vc = kc_ref[...], vc_ref[...]
    lane = lax.broadcasted_iota(I32, (1, LANES), 1)
    row = lax.broadcasted_iota(I32, (gq * tq, 1), 0)
    q_abs = n * tq + row % tq
    k_abs = start + lax.broadcasted_iota(I32, (1, kw), 1)
    allowed = jnp.abs(q_abs - k_abs) <= WINDOW
    head = row // tq
    q = q_ref[...].astype(F32)
    for g in range(WIN_KV_HEADS):
        in_g = (lane // HEAD_DIM) == g
        parts = []
        for j in range(gq):
            hq = g * gq + j
            x = q[:, (hq // 2) * LANES:(hq // 2 + 1) * LANES]
            if hq % 2 != g:
                x = pltpu.roll(x, HEAD_DIM, 1)
            parts.append(jnp.where(in_g, x, 0.0))
        qs = jnp.concatenate(parts, 0).astype(BF16)
        s_loc = lax.dot_general(qs, k_win, NT_DIMS, preferred_element_type=F32)
        s_ctx = lax.dot_general(qs, kc, NT_DIMS, preferred_element_type=F32)
        s_loc = jnp.where(allowed, s_loc, NEG_INF)
        sk = jnp.zeros((gq * tq, 1), F32)
        for j in range(gq):
            sk = jnp.where(head == j, sink_ref[g * gq + j] * LOG2_E, sk)
        m = jnp.maximum(jnp.maximum(jnp.max(s_loc, -1, keepdims=True), jnp.max(s_ctx, -1, keepdims=True)), sk)
        p_loc, p_ctx = jnp.exp2(s_loc - m), jnp.exp2(s_ctx - m)
        den = jnp.sum(p_loc, -1, keepdims=True) + jnp.sum(p_ctx, -1, keepdims=True) + jnp.exp2(sk - m)
        o = (jnp.dot(p_loc.astype(BF16), v_win, preferred_element_type=F32)
             + jnp.dot(p_ctx.astype(BF16), vc, preferred_element_type=F32)) * (1.0 / den)
        for cb in range(gq // 2):
            pair = []
            for half in range(2):
                piece = o[(2 * cb + half) * tq:(2 * cb + half + 1) * tq]
                pair.append(piece if half == g else pltpu.roll(piece, HEAD_DIM, 1))
            col = (g * gq // 2 + cb) * LANES
            o_ref[:, col:col + LANES] = jnp.where(lane < HEAD_DIM, pair[0], pair[1]).astype(o_ref.dtype)


def _win_attn(proj, proj_c, sink, tq):
    b, n, _ = proj.shape
    nc = proj_c.shape[1]
    qcol = DIFF_QK_W // WIN_Q_W
    kcol = (Q_W + DIFF_QK_W + DIFF_VW) // LANES
    kccol = (DIFF_QK_W + DIFF_VW) // LANES
    kern = functools.partial(_win_attn_kernel, seq=n, tq=tq)
    return pl.pallas_call(
        kern, grid=(b, n // tq),
        in_specs=[pl.BlockSpec(memory_space=pltpu.SMEM),
                  pl.BlockSpec((None, tq, WIN_Q_W), lambda bi, i: (bi, i, qcol)),
                  pl.BlockSpec((None, n, LANES), lambda bi, i: (bi, 0, kcol)),
                  pl.BlockSpec((None, n, LANES), lambda bi, i: (bi, 0, kcol + 1)),
                  pl.BlockSpec((None, nc, LANES), lambda bi, i: (bi, 0, kccol)),
                  pl.BlockSpec((None, nc, LANES), lambda bi, i: (bi, 0, kccol + 1))],
        out_specs=pl.BlockSpec((None, tq, WIN_Q_W), lambda bi, i: (bi, i, 0)),
        out_shape=jax.ShapeDtypeStruct((b, n, WIN_Q_W), BF16),
        compiler_params=_cparams(2), name="win_attn",
    )(sink, proj, proj, proj, proj_c, proj_c)


def _layer_norm(r, g, b):
    mu = jnp.mean(r, -1, keepdims=True)
    dlt = r - mu
    var = jnp.mean(dlt * dlt, -1, keepdims=True)
    return dlt * lax.rsqrt(var + LN_EPS) * g + b


def _proj_ln_kernel(*refs, n_in, sub):
    a_refs = refs[:n_in]
    w_refs = refs[n_in:2 * n_in]
    x_ref, gate_ref, lng_ref, lnb_ref, sc_ref, sh_ref, rw_ref, x1_ref, h2_ref, lg_ref = refs[2 * n_in:]

    def split(v):
        hi = v.astype(BF16)
        return hi, (v - hi.astype(F32)).astype(BF16)

    def project(r0):
        y = jnp.dot(a_refs[0][r0:r0 + sub, :], w_refs[0][...], preferred_element_type=F32)
        for a_ref, w_ref in zip(a_refs[1:], w_refs[1:]):
            y = y + jnp.dot(a_ref[r0:r0 + sub, :], w_ref[...], preferred_element_type=F32)
        return y

    rw_hi, rw_lo = split(rw_ref[...])
    rw_both = jnp.concatenate([rw_hi, rw_lo], 1)
    starts = list(range(0, x_ref.shape[0], sub))
    nxt = project(starts[0])
    for i, r0 in enumerate(starts):
        y = nxt
        if i + 1 < len(starts):
            nxt = project(starts[i + 1])
        xn = _layer_norm(DEEPNORM_ALPHA * x_ref[r0:r0 + sub, :] + gate_ref[...] * y, lng_ref[...], lnb_ref[...])
        x1_ref[r0:r0 + sub, :] = xn
        h2 = xn * (1.0 + sc_ref[...]) + sh_ref[...]
        h2_ref[r0:r0 + sub, :] = h2.astype(h2_ref.dtype)
        h_hi, h_lo = split(h2)
        hh = jnp.dot(h_hi, rw_both, preferred_element_type=F32)
        lg = hh[:, :LANES] + hh[:, LANES:] + jnp.dot(h_lo, rw_hi, preferred_element_type=F32)
        lg_ref[:, r0:r0 + sub] = lg.T[:N_EXPERTS, :]


def _proj_ln(acts, ws, x2d, gate, ln_g, ln_b, sc, sh, router_wt, rows_per_mod, tm, name):
    t, d = x2d.shape
    n_in = len(acts)
    tiles_per_mod = rows_per_mod // tm
    nmod = gate.shape[0]
    row = lambda i: (i, 0)
    full = lambda i: (0, 0)
    mod = lambda i: (i // tiles_per_mod, 0, 0)
    in_specs = ([pl.BlockSpec((tm, a.shape[1]), row) for a in acts]
                + [pl.BlockSpec(w.shape, full) for w in ws]
                + [pl.BlockSpec((tm, d), row), pl.BlockSpec((None, 1, d), mod),
                   pl.BlockSpec((1, d), full), pl.BlockSpec((1, d), full),
                   pl.BlockSpec((None, 1, d), mod), pl.BlockSpec((None, 1, d), mod),
                   pl.BlockSpec((d, LANES), full)])
    return pl.pallas_call(
        functools.partial(_proj_ln_kernel, n_in=n_in, sub=LANES), grid=(t // tm,), in_specs=in_specs,
        out_specs=[pl.BlockSpec((tm, d), row), pl.BlockSpec((tm, d), row),
                   pl.BlockSpec((N_EXPERTS, tm), lambda i: (0, i))],
        out_shape=[jax.ShapeDtypeStruct((t, d), F32), jax.ShapeDtypeStruct((t, d), BF16),
                   jax.ShapeDtypeStruct((N_EXPERTS, t), F32)],
        compiler_params=_cparams(1), name=name,
    )(*acts, *ws, x2d, gate.reshape(nmod, 1, d), ln_g.reshape(1, d), ln_b.reshape(1, d),
      sc.reshape(nmod, 1, d), sh.reshape(nmod, 1, d), router_wt)


def _first_argmax(vals):
    idx = jnp.zeros(vals[0].shape, I32)
    best = vals[0]
    for j in range(1, len(vals)):
        upd = vals[j] > best
        idx = jnp.where(upd, j, idx)
        best = jnp.where(upd, vals[j], best)
    return idx, best


def _route_kernel(lg_ref, bias_ref, pos_ref, w_ref, tab_ref, cnt_ref, carry_ref, off_ref, *, tr):
    @pl.when(pl.program_id(0) == 0)
    def _():
        carry_ref[...] = jnp.zeros_like(carry_ref)

    lg = lg_ref[...]
    ex = jnp.exp(lg - jnp.max(lg, 0, keepdims=True))
    scores = ex / jnp.sum(ex, 0, keepdims=True)
    sel = scores + bias_ref[...]
    rows = [sel[e:e + 1] for e in range(N_EXPERTS)]
    group_scores = []
    for g in range(N_GROUPS):
        r = rows[g * EXPERTS_PER_GROUP:(g + 1) * EXPERTS_PER_GROUP]
        best = None
        for i in range(EXPERTS_PER_GROUP):
            for j in range(i + 1, EXPERTS_PER_GROUP):
                s = r[i] + r[j]
                best = s if best is None else jnp.maximum(best, s)
        group_scores.append(best)
    grp, _ = _first_argmax(group_scores)
    vals = []
    for j in range(EXPERTS_PER_GROUP):
        v = rows[(N_GROUPS - 1) * EXPERTS_PER_GROUP + j]
        for g in range(N_GROUPS - 2, -1, -1):
            v = jnp.where(grp == g, rows[g * EXPERTS_PER_GROUP + j], v)
        vals.append(v)
    i0, _ = _first_argmax(vals)
    i1, _ = _first_argmax([jnp.where(i0 == j, -jnp.inf, vals[j]) for j in range(EXPERTS_PER_GROUP)])
    e0 = grp * EXPERTS_PER_GROUP + i0
    e1 = grp * EXPERTS_PER_GROUP + i1
    eid = lax.broadcasted_iota(I32, (N_EXPERTS, 1), 0)
    oh0, oh1 = eid == e0, eid == e1
    s0 = jnp.sum(jnp.where(oh0, scores, 0.0), 0, keepdims=True)
    s1 = jnp.sum(jnp.where(oh1, scores, 0.0), 0, keepdims=True)
    den = s0 + s1
    member = jnp.where(oh0 | oh1, 1.0, 0.0)
    before = lax.broadcasted_iota(I32, (tr, tr), 0) < lax.broadcasted_iota(I32, (tr, tr), 1)
    upper = jnp.where(before, 1.0, 0.0).astype(BF16)
    cnt = jnp.dot(member.astype(BF16), upper, preferred_element_type=F32)
    run = jnp.sum(member, 1, keepdims=True)
    run = jnp.floor((run + (ROW_ALIGN - 1)) * (1.0 / ROW_ALIGN)) * ROW_ALIGN
    run = jnp.broadcast_to(run, (N_EXPERTS, LANES))
    acc = jnp.zeros((1, LANES), F32)
    for e in range(N_EXPERTS):
        off_ref[e:e + 1, :] = acc
        acc = acc + run[e:e + 1]
    off = off_ref[...]
    at = off[:, 0:1] + cnt
    pos_ref[0:1, :] = jnp.sum(jnp.where(oh0, at, 0.0), 0, keepdims=True).astype(I32)
    pos_ref[1:2, :] = jnp.sum(jnp.where(oh1, at, 0.0), 0, keepdims=True).astype(I32)
    w_ref[0:1, :] = s0 / den
    w_ref[1:2, :] = s1 / den
    tab_ref[0] = off.astype(I32)
    tab_ref[1] = run.astype(I32)
    tab_ref[2] = carry_ref[...].astype(I32)
    carry_ref[...] = carry_ref[...] + run
    cnt_ref[...] = carry_ref[...]


def _route(logits_t, router_bias, tr):
    t = logits_t.shape[1]
    tok = lambda i: (0, i)
    return pl.pallas_call(
        functools.partial(_route_kernel, tr=tr), grid=(t // tr,),
        in_specs=[pl.BlockSpec((N_EXPERTS, tr), tok), pl.BlockSpec((N_EXPERTS, 1), lambda i: (0, 0))],
        out_specs=[pl.BlockSpec((2, tr), tok), pl.BlockSpec((2, tr), tok),
                   pl.BlockSpec((None, 3, N_EXPERTS, LANES), lambda i: (i, 0, 0, 0)),
                   pl.BlockSpec((N_EXPERTS, LANES), lambda i: (0, 0))],
        out_shape=[jax.ShapeDtypeStruct((2, t), I32), jax.ShapeDtypeStruct((2, t), F32),
                   jax.ShapeDtypeStruct((t // tr, 3, N_EXPERTS, LANES), I32),
                   jax.ShapeDtypeStruct((N_EXPERTS, LANES), F32)],
        scratch_shapes=[pltpu.VMEM((N_EXPERTS, LANES), F32), pltpu.VMEM((N_EXPERTS, LANES), F32)],
        compiler_params=_cparams(1), name="route",
    )(logits_t, router_bias.reshape(N_EXPERTS, 1))


def _run_copies(n, max_rows, make_copy, wait):
    sz = max_rows
    while sz >= ROW_ALIGN:
        start = (n // (2 * sz)) * (2 * sz)

        @pl.when((n & sz) != 0)
        def _(start=start, sz=sz):
            cp = make_copy(start, sz)
            cp.wait() if wait else cp.start()

        sz //= 2


def _wait_rows(total, max_rows, make_copy):
    sz = max_rows
    while sz >= ROW_ALIGN:
        @pl.when((total & sz) != 0)
        def _(sz=sz):
            make_copy(sz).wait()

        sz //= 2


def _rows(ref, start, size):
    return ref.at[pl.ds(pl.multiple_of(start, ROW_ALIGN), size)]


def _dispatch_kernel(tab_ref, h_ref, pos_ref, w_ref, xs_ref, buf, zbuf, sem, *, tt, n_tiles, bm):
    i = pl.program_id(0)
    d = h_ref.shape[1]
    sb = buf.shape[1]
    ne = N_EXPERTS
    fill = 3 * n_tiles * ne

    @pl.when(i == 0)
    def _():
        zbuf[...] = jnp.zeros_like(zbuf)
        for wait in (False, True):
            for e in range(ne):
                dst, n = tab_ref[fill + e], tab_ref[fill + ne + e]
                _run_copies(n, bm // 2, lambda s, z: pltpu.make_async_copy(
                    zbuf.at[pl.ds(0, z)], _rows(xs_ref, dst + s, z), sem.at[2]), wait)

        def zero_block(j, carry):
            for half in range(2):
                cp = pltpu.make_async_copy(zbuf, _rows(xs_ref, j * bm + half * (bm // 2), bm // 2), sem.at[2])
                cp.start()
                cp.wait()
            return carry

        lax.fori_loop(tab_ref[fill + 2 * ne], xs_ref.shape[0] // bm, zero_block, 0)

    def tile_copies(tile, slot, wait):
        for e in range(ne):
            off = tab_ref[tile * ne + e]
            n = tab_ref[(n_tiles + tile) * ne + e]
            dst = tab_ref[(2 * n_tiles + tile) * ne + e]
            _run_copies(n, tt, lambda s, z: pltpu.make_async_copy(
                _rows(buf.at[slot], off + s, z), _rows(xs_ref, dst + s, z), sem.at[slot]), wait)

    slot = i % 2
    pos, w = pos_ref[...], w_ref[...]
    hb = h_ref[...].astype(BF16)
    for r0 in range(0, sb, LANES):
        srow = r0 + lax.broadcasted_iota(I32, (LANES, 1), 0)
        m0, m1 = srow == pos[0:1], srow == pos[1:2]
        perm = (jnp.where(m0, 1.0, 0.0) + jnp.where(m1, 1.0, 0.0)).astype(BF16)
        buf[slot, r0:r0 + LANES, :d] = jnp.dot(perm, hb, preferred_element_type=F32)
        wrow = jnp.sum(jnp.where(m0, w[0:1], 0.0) + jnp.where(m1, w[1:2], 0.0), 1, keepdims=True)
        buf[slot, r0:r0 + LANES, d:] = jnp.broadcast_to(wrow, (LANES, LANES))

    def wait_tile(tile, slot):
        total = tab_ref[fill + 2 * ne + 1 + tile]
        _wait_rows(total, tt * 2, lambda z: pltpu.make_async_copy(
            buf.at[slot, pl.ds(0, z)], xs_ref.at[pl.ds(0, z)], sem.at[slot]))

    @pl.when(i > 0)
    def _():
        wait_tile(i - 1, 1 - slot)

    tile_copies(i, slot, False)

    @pl.when(i == n_tiles - 1)
    def _():
        wait_tile(i, slot)


def _dispatch(tab, h2, pos, wts, n_slots, tt, sb, bm):
    t, d = h2.shape
    n_tiles = t // tt
    return pl.pallas_call(
        functools.partial(_dispatch_kernel, tt=tt, n_tiles=n_tiles, bm=bm),
        grid_spec=pltpu.PrefetchScalarGridSpec(
            num_scalar_prefetch=1, grid=(n_tiles,),
            in_specs=[pl.BlockSpec((tt, d), lambda i, s: (i, 0)), pl.BlockSpec((2, tt), lambda i, s: (0, i)),
                      pl.BlockSpec((2, tt), lambda i, s: (0, i))],
            out_specs=pl.BlockSpec(memory_space=pl.ANY),
            scratch_shapes=[pltpu.VMEM((2, sb, d + LANES), F32), pltpu.VMEM((bm // 2, d + LANES), F32),
                            pltpu.SemaphoreType.DMA((3,))]),
        out_shape=jax.ShapeDtypeStruct((n_slots, d + LANES), F32),
        compiler_params=_cparams(1), name="moe_dispatch",
    )(tab, h2, pos, wts)


def _ffn_kernel(blk_e_ref, blk_rows_ref, next_e_ref, nused_ref, x_ref, wg_hbm, wu_hbm, wd_hbm, o_ref,
                stage, wg_bf, wu_bf, wd_bf, sem, *, layer, parts):
    del nused_ref
    i = pl.program_id(0)
    e = blk_e_ref[i]
    new_expert = (i == 0) | (e != blk_e_ref[jnp.maximum(i - 1, 0)])

    def fetch(expert, wait):
        for j, src in enumerate((wg_hbm, wu_hbm, wd_hbm)):
            cp = pltpu.make_async_copy(src.at[layer, expert], stage.at[j], sem.at[j])
            cp.wait() if wait else cp.start()

    @pl.when(i == 0)
    def _():
        fetch(e, False)

    @pl.when(new_expert)
    def _():
        fetch(e, True)
        wg_bf[...] = stage[0].astype(BF16)
        wu_bf[...] = stage[1].astype(BF16)
        wd_bf[...] = stage[2].astype(BF16)

        @pl.when(next_e_ref[i] >= 0)
        def _():
            fetch(next_e_ref[i], False)

    d = wg_bf.shape[0]
    rows = blk_rows_ref[i]
    part = x_ref.shape[0] // parts
    spans = [slice(p * part, (p + 1) * part) for p in range(parts)]

    def gate_up(rs):
        x = x_ref[rs, :d].astype(BF16)
        return (jnp.dot(x, wg_bf[...], preferred_element_type=F32), jnp.dot(x, wu_bf[...], preferred_element_type=F32))

    def finish(rs, gate, up):
        act = (gate * jax.nn.sigmoid(gate) * up).astype(BF16)
        o_ref[rs, :] = jnp.dot(act, wd_bf[...], preferred_element_type=F32) * x_ref[rs, d:d + 1]

    all_parts = rows > (parts - 1) * part

    @pl.when(all_parts)
    def _():
        nxt = gate_up(spans[0])
        for p, rs in enumerate(spans):
            cur = nxt
            if p + 1 < parts:
                nxt = gate_up(spans[p + 1])
            finish(rs, *cur)

    for p, rs in enumerate(spans):
        @pl.when(jnp.logical_not(all_parts) & (rows > p * part))
        def _(rs=rs):
            finish(rs, *gate_up(rs))

        @pl.when(rows <= p * part)
        def _(rs=rs):
            o_ref[rs, :] = jnp.zeros((part, d), F32)


def _ffn(blk_e, blk_rows, next_e, n_used, xs, wg, wu, wd, layer, bm, parts):
    n_slots, xw = xs.shape
    d, ff = wg.shape[2:]
    assert d == ff
    xrow = lambda i, be, br, ne, nu: (jnp.minimum(i, nu[0] - 1), 0)
    hbm = pl.BlockSpec(memory_space=pl.ANY)
    return pl.pallas_call(
        functools.partial(_ffn_kernel, layer=layer, parts=parts),
        grid_spec=pltpu.PrefetchScalarGridSpec(
            num_scalar_prefetch=4, grid=(n_slots // bm,),
            in_specs=[pl.BlockSpec((bm, xw), xrow), hbm, hbm, hbm],
            out_specs=pl.BlockSpec((bm, d), lambda i, be, br, ne, nu: (i, 0)),
            scratch_shapes=[pltpu.VMEM((3, d, ff), F32), pltpu.VMEM((d, ff), BF16), pltpu.VMEM((d, ff), BF16),
                            pltpu.VMEM((ff, d), BF16), pltpu.SemaphoreType.DMA((3,))]),
        out_shape=jax.ShapeDtypeStruct((n_slots, d), F32),
        compiler_params=_cparams(1), name="moe_ffn",
    )(blk_e, blk_rows, next_e, n_used, xs, wg, wu, wd)


def _combine_kernel(tab_ref, ys_ref, pos_ref, x1_ref, gate_ref, lng_ref, lnb_ref, o_ref, ybuf, sem, *, tt, n_tiles):
    i = pl.program_id(0)
    sb = ybuf.shape[1]
    ne = N_EXPERTS
    slot = i % 2

    def tile_copies(tile, slot, wait):
        for e in range(ne):
            off = tab_ref[tile * ne + e]
            n = tab_ref[(n_tiles + tile) * ne + e]
            src = tab_ref[(2 * n_tiles + tile) * ne + e]
            _run_copies(n, tt, lambda s, z: pltpu.make_async_copy(
                _rows(ys_ref, src + s, z), _rows(ybuf.at[slot], off + s, z), sem.at[slot]), wait)

    @pl.when(i == 0)
    def _():
        ybuf[...] = jnp.zeros_like(ybuf)
        tile_copies(0, 0, False)

    @pl.when(i + 1 < n_tiles)
    def _():
        tile_copies(i + 1, 1 - slot, False)

    total = tab_ref[3 * n_tiles * ne + 2 * ne + 1 + i]
    _wait_rows(total, tt * 2, lambda z: pltpu.make_async_copy(
        ys_ref.at[pl.ds(0, z)], ybuf.at[slot, pl.ds(0, z)], sem.at[slot]))
    scol = lax.broadcasted_iota(I32, (1, sb), 1)
    yb = ybuf[slot].astype(BF16)

    def moe_rows(r0):
        pos = pos_ref[r0:r0 + LANES, :]
        unperm = (jnp.where(scol == pos[:, 0:1], 1.0, 0.0) + jnp.where(scol == pos[:, 1:2], 1.0, 0.0)).astype(BF16)
        return jnp.dot(unperm, yb, preferred_element_type=F32)

    starts = list(range(0, tt, LANES))
    nxt = moe_rows(starts[0])
    for j, r0 in enumerate(starts):
        f = nxt
        if j + 1 < len(starts):
            nxt = moe_rows(starts[j + 1])
        o_ref[r0:r0 + LANES, :] = _layer_norm(DEEPNORM_ALPHA * x1_ref[r0:r0 + LANES, :] + gate_ref[...] * f,
                                              lng_ref[...], lnb_ref[...])


def _combine(tab, ys, pos_t, x1, gate, ln_g, ln_b, rows_per_mod, tt, sb):
    t, d = x1.shape
    tiles_per_mod = rows_per_mod // tt
    nmod = gate.shape[0]
    return pl.pallas_call(
        functools.partial(_combine_kernel, tt=tt, n_tiles=t // tt),
        grid_spec=pltpu.PrefetchScalarGridSpec(
            num_scalar_prefetch=1, grid=(t // tt,),
            in_specs=[pl.BlockSpec(memory_space=pl.ANY),
                      pl.BlockSpec((tt, 2), lambda i, s: (i, 0)),
                      pl.BlockSpec((tt, d), lambda i, s: (i, 0)),
                      pl.BlockSpec((None, 1, d), lambda i, s: (i // tiles_per_mod, 0, 0)),
                      pl.BlockSpec((1, d), lambda i, s: (0, 0)), pl.BlockSpec((1, d), lambda i, s: (0, 0))],
            out_specs=pl.BlockSpec((tt, d), lambda i, s: (i, 0)),
            scratch_shapes=[pltpu.VMEM((2, sb, d), F32), pltpu.SemaphoreType.DMA((2,))]),
        out_shape=jax.ShapeDtypeStruct((t, d), F32),
        compiler_params=_cparams(1), name="moe_combine",
    )(tab, ys, pos_t, x1, gate.reshape(nmod, 1, d), ln_g.reshape(1, d), ln_b.reshape(1, d))


def _moe(h2, logits_t, x1, gate, ln_g, ln_b, router_bias, wg, wu, wd, layer, rows_per_mod):
    t, d = h2.shape
    bm, tt, ne = MOE_BM, MOE_TILE, N_EXPERTS
    n_tiles = t // tt
    pad = ne * (ROW_ALIGN - 1)
    sb = -(-(2 * tt + pad) // LANES) * LANES
    pos, wts, tab3, cnt = _route(logits_t, router_bias, tr=tt)
    rows = cnt[:, 0].astype(I32)
    prows = (rows + bm - 1) // bm * bm
    pends = jnp.cumsum(prows)
    pstarts = pends - prows
    n_blk = -(-(2 * t + n_tiles * pad + ne * (bm - 1)) // bm)
    n_used = pends[-1] // bm
    blk_ids = jnp.minimum(jnp.arange(n_blk, dtype=I32), n_used - 1)
    blk_e = jnp.minimum(jnp.sum((blk_ids[:, None] * bm >= pends[None, :]).astype(I32), 1), ne - 1)
    eids = jnp.arange(ne, dtype=I32)
    of_blk = lambda per_expert: jnp.sum(jnp.where(blk_e[:, None] == eids[None, :], per_expert[None, :], 0), 1)
    raw_ids = jnp.arange(n_blk, dtype=I32)
    blk_rows = jnp.where(raw_ids < n_used, jnp.clip(of_blk(pstarts + rows) - raw_ids * bm, 0, bm), 0)
    later_used = (eids[None, :] > eids[:, None]) & (prows[None, :] > 0)
    next_used = jnp.min(jnp.where(later_used, eids[None, :], ne), 1)
    next_e = of_blk(jnp.where(next_used < ne, next_used, -1))
    tab3 = tab3[:, :, :, 0]
    tab = jnp.concatenate([tab3[:, 0].reshape(-1), tab3[:, 1].reshape(-1),
                           (tab3[:, 2] + pstarts[None, :]).reshape(-1), pstarts + rows, prows - rows,
                           n_used.reshape(1), jnp.sum(tab3[:, 1], 1)]).astype(I32)
    xs = _dispatch(tab, h2, pos, wts, n_blk * bm, tt, sb, bm)
    ys = _ffn(blk_e, blk_rows.astype(I32), next_e.astype(I32), n_used.reshape(1).astype(I32), xs, wg, wu, wd,
              layer, bm, parts=2)
    return _combine(tab, ys, pos.T, x1, gate, ln_g, ln_b, rows_per_mod, tt, sb)


def _filter_mlp_kernel(z_ref, w1_ref, w2_ref, w3_ref, b_ref, sf_ref, o_ref):
    hp = lax.Precision.HIGHEST
    b, sf = b_ref[...], sf_ref[...]
    h = jnp.sin(sf[0:1] * (jnp.dot(z_ref[...], w1_ref[...], precision=hp, preferred_element_type=F32) + b[0:1]))
    h = jnp.sin(sf[1:2] * (jnp.dot(h, w2_ref[...], precision=hp, preferred_element_type=F32) + b[1:2]))
    o_ref[...] = jnp.sin(sf[2:3] * (jnp.dot(h, w3_ref[...], precision=hp, preferred_element_type=F32) + b[2:3]))


def _filter_mlp(zp, w_in_p, w_hid, b, sf):
    n = zp.shape[0]
    return pl.pallas_call(
        _filter_mlp_kernel, out_shape=jax.ShapeDtypeStruct((n, FILTER_HID), F32),
        compiler_params=pltpu.CompilerParams(vmem_limit_bytes=VMEM_LIMIT), name="hyena_filter_mlp",
    )(zp, w_in_p, w_hid[0], w_hid[1], b, sf)


def _filter_spec_kernel(hid_ref, wf_ref, wb_ref, dec_ref, fwd_ref, hr_ref, g_ref, nyq_ref, p_scr, q_scr, *, fb, kps):
    k = pl.program_id(2)
    n = hid_ref.shape[0]

    @pl.when(k == 0)
    def _():
        hp = lax.Precision.HIGHEST
        hid, dec = hid_ref[...], dec_ref[...]
        row = lax.broadcasted_iota(I32, (n, 1), 0)
        fw = jnp.dot(hid, wf_ref[...], precision=hp, preferred_element_type=F32) * dec
        bw = jnp.dot(hid, wb_ref[...], precision=hp, preferred_element_type=F32) * dec
        bw = jnp.where(row == 0, 0.0, bw)
        p = fw + bw
        p_scr[...] = p.astype(BF16)
        q_scr[...] = (fw - bw).astype(BF16)
        sign = (1 - 2 * (row & 1)).astype(F32)
        nyq_ref[...] = jnp.sum(p * sign, 0, keepdims=True)

    rowb = lax.broadcasted_iota(I32, (fb, 1), 0)
    for j in range(kps):
        hr_ref[j * fb:(j + 1) * fb, :] = jnp.dot(fwd_ref[j, 0:fb, :], p_scr[...], preferred_element_type=F32)
        gg = jnp.dot(fwd_ref[j, fb:2 * fb, :], q_scr[...], preferred_element_type=F32)
        if j == 0:
            gg = jnp.where((rowb == 0) & (k == 0), 0.0, gg)
        g_ref[j * fb:(j + 1) * fb, :] = gg


def _filter_spec(hid, w_out4, decay, fwd_tab, fb, tc, kps):
    n = hid.shape[0]
    d = D_MODEL
    kb = n // (fb * kps)
    return pl.pallas_call(
        functools.partial(_filter_spec_kernel, fb=fb, kps=kps), grid=(2, d // tc, kb),
        in_specs=[pl.BlockSpec((n, FILTER_HID), lambda o, c, k: (0, 0)),
                  pl.BlockSpec((None, None, FILTER_HID, tc), lambda o, c, k: (0, o, 0, c)),
                  pl.BlockSpec((None, None, FILTER_HID, tc), lambda o, c, k: (1, o, 0, c)),
                  pl.BlockSpec((n, tc), lambda o, c, k: (0, c)),
                  pl.BlockSpec((kps, 2 * fb, n), lambda o, c, k: (k, 0, 0))],
        out_specs=[pl.BlockSpec((None, kps * fb, tc), lambda o, c, k: (o, k, c)),
                   pl.BlockSpec((None, kps * fb, tc), lambda o, c, k: (o, k, c)),
                   pl.BlockSpec((None, 1, tc), lambda o, c, k: (o, 0, c))],
        out_shape=[jax.ShapeDtypeStruct((2, n, d), F32), jax.ShapeDtypeStruct((2, n, d), F32),
                   jax.ShapeDtypeStruct((2, 1, d), F32)],
        scratch_shapes=[pltpu.VMEM((n, tc), BF16), pltpu.VMEM((n, tc), BF16)],
        compiler_params=_cparams(3), name="hyena_filter_spec",
    )(hid, w_out4, w_out4, decay, fwd_tab)


def _short_conv(u_ref, cw_ref, cb_ref):
    n = u_ref.shape[0]
    u = u_ref[...].astype(F32)
    row = lax.broadcasted_iota(I32, (n, 1), 0)
    prev = jnp.where(row == 0, 0.0, pltpu.roll(u, 1, 0))
    nxt = jnp.where(row == n - 1, 0.0, pltpu.roll(u, n - 1, 0))
    cw = cw_ref[...]
    return prev * cw[0:1] + u * cw[1:2] + nxt * cw[2:3] + cb_ref[...]


def _hyena_conv_kernel(uv_ref, ux1_ref, ux2_ref, cwv_ref, cwx1_ref, cwx2_ref, cbv_ref, cbx1_ref, cbx2_ref,
                       fwd_ref, inv_ref, hr_ref, g_ref, nyq_ref, sk_ref, o_ref, zb_scr, z32_scr, acc_scr, *, fb, kb,
                       kps):
    o = pl.program_id(2)
    k = pl.program_id(3)

    @pl.when((o == 0) & (k == 0))
    def _():
        v = _short_conv(uv_ref, cwv_ref, cbv_ref)
        z32_scr[...] = v
        zb_scr[...] = v.astype(BF16)

    @pl.when(k == 0)
    def _():
        acc_scr[...] = jnp.zeros_like(acc_scr)

    zb = zb_scr[...]
    zfs = [jnp.dot(fwd_ref[j], zb, preferred_element_type=F32) for j in range(kps)]
    rowb = lax.broadcasted_iota(I32, (fb, 1), 0)
    for j in range(kps):
        zr, zi = zfs[j][:fb], zfs[j][fb:]
        hr, gg = hr_ref[j * fb:(j + 1) * fb, :], g_ref[j * fb:(j + 1) * fb, :]
        hb = jnp.where((rowb == 0) & (k == 0), nyq_ref[...], hr) if j == 0 else hr
        y = jnp.concatenate([zr * hr - zi * gg, zr * gg + zi * hb], 0).astype(BF16)
        acc_scr[...] += jnp.dot(inv_ref[j], y, preferred_element_type=F32)

    @pl.when((k == kb - 1) & (o == 0))
    def _():
        zn = _short_conv(ux1_ref, cwx1_ref, cbx1_ref) * (acc_scr[...] + z32_scr[...] * sk_ref[...])
        z32_scr[...] = zn
        zb_scr[...] = zn.astype(BF16)

    @pl.when((k == kb - 1) & (o == 1))
    def _():
        zn = _short_conv(ux2_ref, cwx2_ref, cbx2_ref) * (acc_scr[...] + z32_scr[...] * sk_ref[...])
        o_ref[...] = zn.astype(o_ref.dtype)


def _hyena_conv(u, conv_w, conv_b, fwd_tab, inv_tab, hr, gg, nyq, skip, fb, tc, kps):
    b, n, _ = u.shape
    d = D_MODEL
    kb = n // (fb * kps)
    ncb = d // tc
    ucol = lambda part: (lambda bi, c, o, k: (bi, 0, part * ncb + c))
    wcol = lambda part: (lambda bi, c, o, k: (0, part * ncb + c))
    spec = lambda bi, c, o, k: (o, k, c)
    per_o = lambda bi, c, o, k: (o, 0, c)
    return pl.pallas_call(
        functools.partial(_hyena_conv_kernel, fb=fb, kb=kb, kps=kps), grid=(b, ncb, 2, kb),
        in_specs=[pl.BlockSpec((None, n, tc), ucol(0)), pl.BlockSpec((None, n, tc), ucol(1)),
                  pl.BlockSpec((None, n, tc), ucol(2)),
                  pl.BlockSpec((3, tc), wcol(0)), pl.BlockSpec((3, tc), wcol(1)), pl.BlockSpec((3, tc), wcol(2)),
                  pl.BlockSpec((1, tc), wcol(0)), pl.BlockSpec((1, tc), wcol(1)), pl.BlockSpec((1, tc), wcol(2)),
                  pl.BlockSpec((kps, 2 * fb, n), lambda bi, c, o, k: (k, 0, 0)),
                  pl.BlockSpec((kps, n, 2 * fb), lambda bi, c, o, k: (k, 0, 0)),
                  pl.BlockSpec((None, kps * fb, tc), spec), pl.BlockSpec((None, kps * fb, tc), spec),
                  pl.BlockSpec((None, 1, tc), per_o), pl.BlockSpec((None, 1, tc), per_o)],
        out_specs=pl.BlockSpec((None, n, tc), lambda bi, c, o, k: (bi, 0, c)),
        out_shape=jax.ShapeDtypeStruct((b, n, d), BF16),
        scratch_shapes=[pltpu.VMEM((n, tc), BF16), pltpu.VMEM((n, tc), F32), pltpu.VMEM((n, tc), F32)],
        compiler_params=_cparams(4), name="hyena_conv",
    )(u, u, u, conv_w, conv_w, conv_w, conv_b, conv_b, conv_b, fwd_tab, inv_tab, hr, gg, nyq,
      skip.reshape(2, 1, d))


def _lag_spec_kernel(hid_ref, wf_ref, wb_ref, dec_ref, ph_ref, h_ref, fw_scr, bw_scr, *, s, nb):
    lag = pl.program_id(2)
    n = hid_ref.shape[0]

    @pl.when(lag == 0)
    def _():
        def split(v):
            hi = v.astype(BF16)
            return hi, (v - hi.astype(F32)).astype(BF16)

        def precise_dot(a_hi, a_lo, w):
            w_hi, w_lo = split(w)
            return (jnp.dot(a_hi, w_hi, preferred_element_type=F32) + jnp.dot(a_lo, w_hi, preferred_element_type=F32)
                    + jnp.dot(a_hi, w_lo, preferred_element_type=F32))

        dec = dec_ref[...]
        hid_hi, hid_lo = split(hid_ref[...])
        row = lax.broadcasted_iota(I32, (n, 1), 0)
        fw = precise_dot(hid_hi, hid_lo, wf_ref[...]) * dec
        bw = precise_dot(hid_hi, hid_lo, wb_ref[...]) * dec
        fw_scr[...] = fw.astype(BF16)
        bw_scr[...] = jnp.where(row == 0, 0.0, bw).astype(BF16)

    def piece(ph, scr, j):
        return jnp.dot(ph_ref[ph], scr[j * s:(j + 1) * s, :], preferred_element_type=F32)

    for idx in range(2 * nb - 1):
        m = idx - (nb - 1)

        @pl.when(lag == idx)
        def _(m=m):
            if m >= 1:
                h_ref[...] = piece(0, fw_scr, m) + piece(1, fw_scr, m - 1)
            elif m == 0:
                h_ref[...] = piece(0, fw_scr, 0) + piece(2, bw_scr, 0)
            else:
                h_ref[...] = piece(2, bw_scr, -m) + piece(3, bw_scr, -m - 1)


def _lag_spec(hid, w_out4, decay, phases, s, tc):
    n = hid.shape[0]
    d = D_MODEL
    nb = n // s
    return pl.pallas_call(
        functools.partial(_lag_spec_kernel, s=s, nb=nb), grid=(2, d // tc, 2 * nb - 1),
        in_specs=[pl.BlockSpec((n, FILTER_HID), lambda o, c, l: (0, 0)),
                  pl.BlockSpec((None, None, FILTER_HID, tc), lambda o, c, l: (0, o, 0, c)),
                  pl.BlockSpec((None, None, FILTER_HID, tc), lambda o, c, l: (1, o, 0, c)),
                  pl.BlockSpec((n, tc), lambda o, c, l: (0, c)),
                  pl.BlockSpec((4, 2 * s, s), lambda o, c, l: (0, 0, 0))],
        out_specs=pl.BlockSpec((None, None, 2 * s, tc), lambda o, c, l: (o, l, 0, c)),
        out_shape=jax.ShapeDtypeStruct((2, 2 * nb - 1, 2 * s, d), F32),
        scratch_shapes=[pltpu.VMEM((n, tc), BF16), pltpu.VMEM((n, tc), BF16)],
        compiler_params=_cparams(3), name="hyena_lag_spec",
    )(hid, w_out4, w_out4, decay, phases)


def _part_conv_kernel(uv_ref, ux1_ref, ux2_ref, cwv_ref, cwx1_ref, cwx2_ref, cbv_ref, cbx1_ref, cbx2_ref,
                      f_ref, finv_ref, h_ref, sk_ref, o_ref, src_scr, gate_scr, zb_scr, zall, ybuf, *, s, nb, rc):
    src_scr[...] = _short_conv(uv_ref, cwv_ref, cbv_ref)
    gates = ((ux1_ref, cwx1_ref, cbx1_ref), (ux2_ref, cwx2_ref, cbx2_ref))
    for o, gate in enumerate(gates):
        zb_scr[...] = src_scr[...].astype(BF16)
        gate_scr[...] = _short_conv(*gate)
        for j in range(nb):
            zall[j] = jnp.dot(f_ref[...], zb_scr[j * s:(j + 1) * s, :], preferred_element_type=F32)
        sk = sk_ref[o]
        for i in range(nb):
            pairs = [(j, i - j + nb - 1) for j in range(nb)]
            for r0 in range(0, s, rc):
                re, im = slice(r0, r0 + rc), slice(s + r0, s + r0 + rc)
                yr = yi = None
                for j, m in pairs:
                    zr, zi, hr, g = zall[j, re, :], zall[j, im, :], h_ref[o, m, re, :], h_ref[o, m, im, :]
                    tr, ti = zr * hr - zi * g, zr * g + zi * hr
                    yr, yi = (tr, ti) if yr is None else (yr + tr, yi + ti)
                ybuf[re, :] = yr
                ybuf[im, :] = yi
            dc = nyq = None
            for j, m in pairs:
                t0 = zall[j, 0:1, :] * h_ref[o, m, 0:1, :]
                t1 = zall[j, s:s + 1, :] * h_ref[o, m, s:s + 1, :]
                dc, nyq = (t0, t1) if dc is None else (dc + t0, nyq + t1)
            ybuf[0:1, :] = dc
            ybuf[s:s + 1, :] = nyq
            y = jnp.dot(finv_ref[...], ybuf[...].astype(BF16), preferred_element_type=F32)
            rows = slice(i * s, (i + 1) * s)
            zn = gate_scr[rows, :] * (y + src_scr[rows, :] * sk)
            if o + 1 < len(gates):
                src_scr[rows, :] = zn
            else:
                o_ref[rows, :] = zn.astype(o_ref.dtype)


def _part_conv(u, conv_w, conv_b, f_tab, finv_tab, spec, skip, s, tc):
    b, n, _ = u.shape
    d = D_MODEL
    ncb = d // tc
    nb = n // s
    col = lambda part: (lambda c, bi: (bi, 0, part * ncb + c))
    wcol = lambda part: (lambda c, bi: (0, part * ncb + c))
    const = lambda c, bi: (0, 0)
    in_specs = ([pl.BlockSpec((None, n, tc), col(p)) for p in range(3)]
                + [pl.BlockSpec((3, tc), wcol(p)) for p in range(3)]
                + [pl.BlockSpec((1, tc), wcol(p)) for p in range(3)]
                + [pl.BlockSpec((2 * s, s), const), pl.BlockSpec((s, 2 * s), const),
                   pl.BlockSpec((2, 2 * nb - 1, 2 * s, tc), lambda c, bi: (0, 0, 0, c), pipeline_mode=pl.Buffered(1)),
                   pl.BlockSpec((2, 1, tc), lambda c, bi: (0, 0, c))])
    return pl.pallas_call(
        functools.partial(_part_conv_kernel, s=s, nb=nb, rc=32), grid=(ncb, b),
        in_specs=in_specs,
        out_specs=pl.BlockSpec((None, n, tc), lambda c, bi: (bi, 0, c)),
        out_shape=jax.ShapeDtypeStruct((b, n, d), BF16),
        scratch_shapes=[pltpu.VMEM((n, tc), F32), pltpu.VMEM((n, tc), F32), pltpu.VMEM((n, tc), BF16),
                        pltpu.VMEM((nb, 2 * s, tc), F32), pltpu.VMEM((2 * s, tc), F32)],
        compiler_params=_cparams(2), name="hyena_part_conv",
    )(u, u, u, conv_w, conv_w, conv_w, conv_b, conv_b, conv_b, f_tab, finv_tab, spec, skip.reshape(2, 1, d))


def kernel(x, c, ctx, c_ctx, mod_w, mod_b, ln_g, ln_b, attn_w_in, attn_lambda, attn_subln_g, attn_sink, attn_w_out,
           hy_w_in, hy_conv_w, hy_conv_b, hy_ffn_w_in, hy_ffn_w_hid, hy_ffn_b, hy_sin_freq, hy_ffn_w_out, hy_skip,
           hy_w_out, router_w, router_bias, exp_w_gate, exp_w_up, exp_w_down):
    b, n, d = x.shape
    nc = ctx.shape[1]
    t = b * n
    assert d == D_MODEL and b + 1 <= MOD_ROWS and n % 512 == 0 and nc % 256 == 0

    c_rows = jnp.zeros((MOD_ROWS, d), F32).at[:b].set(c).at[b].set(c_ctx)
    mods = _mods(c_rows, mod_w, mod_b)
    router_wt = jnp.pad(router_w, ((0, 0), (0, LANES - N_EXPERTS)))
    x2d = x.reshape(t, d)

    sh1, sc1, g1, sh2, sc2, g2 = jnp.split(mods[0], 6, axis=-1)
    cos, sin = _rope_tables(n)
    w_in = attn_w_in[0].astype(BF16)
    q_groups = tuple(range(Q_W // LANES))
    rope_groups = q_groups + tuple(range(Q_W // LANES, (Q_W + DIFF_QK_W) // LANES)) + (
        (Q_W + DIFF_QK_W + DIFF_VW) // LANES,)
    q_scales = tuple((g, HEAD_DIM ** -0.5 * LOG2_E) for g in q_groups)
    proj = _modmm(x2d, sc1[:b], sh1[:b], w_in, rows_per_mod=n, tm=1024,
                  rope=(jnp.asarray(cos), jnp.asarray(sin), rope_groups, q_scales), name="attn_in_proj")
    proj_c = _modmm(ctx.reshape(b * nc, d), sc1[b:b + 1], sh1[b:b + 1], w_in[:, Q_W:], rows_per_mod=b * nc, tm=256,
                    name="ctx_in_proj")
    proj = proj.reshape(b, n, ATTN_PROJ_W)
    proj_c = proj_c.reshape(b, nc, KV_W)
    lam_init = 0.8 - 0.6 * math.exp(-0.3 * 0)
    oa = _diff_attn(proj, proj_c, attn_lambda[0], attn_subln_g[0], lam_init, tq=1024, sub=128)
    ow = _win_attn(proj, proj_c, attn_sink[0], tq=256)
    w_out = attn_w_out[0].astype(BF16)
    x1, h2, lgt = _proj_ln([oa.reshape(t, DIFF_VW), ow.reshape(t, WIN_Q_W)], [w_out[:DIFF_VW], w_out[DIFF_VW:]],
                           x2d, g1[:b], ln_g[0, 0], ln_b[0, 0], sc2[:b], sh2[:b], router_wt, rows_per_mod=n, tm=1024,
                           name="attn_out_proj_ln")
    x2 = _moe(h2, lgt, x1, g2[:b], ln_g[0, 1], ln_b[0, 1], router_bias, exp_w_gate, exp_w_up, exp_w_down, layer=0,
              rows_per_mod=n)

    sh1, sc1, g1, sh2, sc2, g2 = jnp.split(mods[1], 6, axis=-1)
    u = _modmm(x2, sc1[:b], sh1[:b], hy_w_in[0].astype(BF16), rows_per_mod=n, tm=1024, name="hyena_in_proj")
    part, tc = HYENA_PART, 256
    phases_np, finv_np = _partition_tables(part)
    phases, finv_tab = jnp.asarray(phases_np).astype(BF16), jnp.asarray(finv_np).astype(BF16)
    zp, decay = _filter_tables(n)
    w_in_p = jnp.zeros((FILTER_HID, FILTER_HID), F32).at[:FILTER_EMB].set(hy_ffn_w_in[0])
    hid = _filter_mlp(jnp.asarray(zp), w_in_p, hy_ffn_w_hid[0], hy_ffn_b[0], hy_sin_freq[0])
    spec = _lag_spec(hid, hy_ffn_w_out[0].reshape(FILTER_HID, 2, 2, d).transpose(1, 2, 0, 3), jnp.asarray(decay),
                     phases, part, tc=512)
    u3 = u.reshape(b, n, 3 * d)
    conv_b = hy_conv_b[0].reshape(1, 3 * d)
    z2 = _part_conv(u3, hy_conv_w[0], conv_b, phases[0], finv_tab, spec, hy_skip[0], part, tc)
    x3, h2, lgt = _proj_ln([z2.reshape(t, d)], [hy_w_out[0].astype(BF16)], x2, g1[:b], ln_g[1, 0], ln_b[1, 0],
                           sc2[:b], sh2[:b], router_wt, rows_per_mod=n, tm=1024, name="hyena_out_proj_ln")
    x4 = _moe(h2, lgt, x3, g2[:b], ln_g[1, 1], ln_b[1, 1], router_bias, exp_w_gate, exp_w_up, exp_w_down, layer=1,
              rows_per_mod=n)
    return x4.reshape(b, n, d)
```

```python
import functools
import math

import jax
import jax.numpy as jnp
import numpy as np
from jax import lax
from jax.experimental import pallas as pl
from jax.experimental.pallas import tpu as pltpu

F32 = jnp.float32
BF16 = jnp.bfloat16
I32 = jnp.int32

D_MODEL = 1024
DEPTH = 2
GRID_W = 64
HEAD_DIM = 64
DIFF_HEADS = 4
WIN_Q_HEADS = 8
WIN_KV_HEADS = 2
WINDOW = 128
WIN_BLOCK = 128
ROPE_BASE = 10000.0
DIFF_QK_W = DIFF_HEADS * 2 * HEAD_DIM
DIFF_VW = DIFF_HEADS * 2 * HEAD_DIM
WIN_Q_W = WIN_Q_HEADS * HEAD_DIM
WIN_KV_W = WIN_KV_HEADS * HEAD_DIM
Q_W = DIFF_QK_W + WIN_Q_W
KV_W = DIFF_QK_W + DIFF_VW + 2 * WIN_KV_W
ATTN_PROJ_W = Q_W + KV_W
FILTER_EMB = 33
FILTER_HID = 64
DECAY_TARGET = 1e-2
FAST_DECAY_PCT = 0.3
SLOW_DECAY_PCT = 1.5
N_EXPERTS = 16
N_GROUPS = 4
EXPERTS_PER_GROUP = N_EXPERTS // N_GROUPS
EXPERT_FF = 1024
LN_EPS = 1e-5
DEEPNORM_ALPHA = (2 * DEPTH) ** 0.25
NEG_INF = -1e30
LOG2_E = 1.4426950408889634

LANES = 128
SUBLANES = 8
MOD_ROWS = 16
VMEM_LIMIT = 60 * 1024 * 1024
MOE_BM = 1024
MOE_PART = 256
MOE_TILE = 512
HYENA_PART = 512
ROW_ALIGN = 8
NT_DIMS = (((1,), (1,)), ((), ()))


def _cparams(n_axes):
    return pltpu.CompilerParams(dimension_semantics=("arbitrary",) * n_axes, vmem_limit_bytes=VMEM_LIMIT)


@functools.lru_cache(maxsize=None)
def _rope_tables(n):
    rows = n // GRID_W
    r, col = np.meshgrid(np.arange(rows, dtype=np.float32), np.arange(GRID_W, dtype=np.float32), indexing="ij")
    axis_dim = HEAD_DIM // 2
    inv_freq = (ROPE_BASE ** (-np.arange(0, axis_dim, 2, dtype=np.float32) / axis_dim)).astype(np.float32)
    ang = np.concatenate([r.reshape(-1, 1) * inv_freq, col.reshape(-1, 1) * inv_freq], -1)
    ang = np.concatenate([ang, ang], -1).astype(np.float32)
    cos, sin = np.cos(ang), np.sin(ang)
    half = np.arange(HEAD_DIM) < HEAD_DIM // 2
    sin_signed = np.where(half[None, :], -sin, sin)
    reps = LANES // HEAD_DIM
    return (np.tile(cos, (1, reps)).astype(np.float32), np.tile(sin_signed, (1, reps)).astype(np.float32))


@functools.lru_cache(maxsize=None)
def _dft_tables(n, fb):
    big = 2 * n
    k = np.arange(n, dtype=np.int64)[:, None]
    t = np.arange(n, dtype=np.int64)[None, :]
    ang = ((k * t) % big).astype(np.float64) * (2.0 * math.pi / big)
    c, s = np.cos(ang), np.sin(ang)
    alt = (1 - 2 * (np.arange(n) & 1)).astype(np.float64)
    s_f = s.copy()
    s_f[0, :] = alt
    kb = n // fb
    fwd = np.concatenate([c.reshape(kb, fb, n), s_f.reshape(kb, fb, n)], axis=1)
    ci = c.T * (2.0 / big)
    ci[:, 0] = 1.0 / big
    si = s.T * (2.0 / big)
    si[:, 0] = alt / big
    inv = np.concatenate([ci.reshape(n, kb, fb).transpose(1, 0, 2), si.reshape(n, kb, fb).transpose(1, 0, 2)], axis=2)
    return fwd.astype(np.float32), inv.astype(np.float32)


@functools.lru_cache(maxsize=None)
def _filter_tables(n):
    t = np.linspace(0.0, 1.0, n, dtype=np.float32)[:, None]
    bands = (FILTER_EMB - 1) // 2
    w = (2.0 * math.pi * np.arange(n, dtype=np.float32)[:, None] / n).astype(np.float32)
    fr = np.linspace(1e-4, bands - 1, bands, dtype=np.float32)[None, :]
    z = np.concatenate([t, np.cos(fr * w), -np.sin(fr * w)], -1).astype(np.float32)
    zp = np.zeros((n, FILTER_HID), np.float32)
    zp[:, :FILTER_EMB] = z
    deltas = np.abs(np.linspace(math.log(DECAY_TARGET) / SLOW_DECAY_PCT, math.log(DECAY_TARGET) / FAST_DECAY_PCT,
                                D_MODEL, dtype=np.float32))
    decay = np.exp(-t * deltas[None, :]).astype(np.float32)
    return zp, decay


@functools.lru_cache(maxsize=None)
def _partition_tables(s):
    big = 2 * s
    k = np.arange(s, dtype=np.int64)[:, None]
    r = np.arange(s, dtype=np.int64)[None, :]

    def phase(pr, sin_sign, drop_r0):
        ang = ((k * pr) % big).astype(np.float64) * (2.0 * math.pi / big)
        c, sn = np.cos(ang), sin_sign * np.sin(ang)
        sn[0, :] = 1.0 - 2.0 * (pr[0] & 1)
        if drop_r0:
            c[:, 0] = 0.0
            sn[:, 0] = 0.0
        return np.concatenate([c, sn], 0)

    phases = np.stack([phase(r, 1.0, False), phase(s - r, -1.0, True), phase(r, -1.0, False), phase(s - r, 1.0, True)])
    ang = ((r.T * k.T) % big).astype(np.float64) * (2.0 * math.pi / big)
    ci = np.cos(ang) * (2.0 / big)
    ci[:, 0] = 1.0 / big
    si = np.sin(ang) * (2.0 / big)
    si[:, 0] = (1.0 - 2.0 * (np.arange(s) & 1)) / big
    return phases.astype(np.float32), np.concatenate([ci, si], 1).astype(np.float32)


def _mods_kernel(c_ref, w_ref, b_ref, o_ref):
    c = c_ref[...]
    a = (c * jax.nn.sigmoid(c)).astype(BF16)
    o_ref[...] = jnp.dot(a, w_ref[...].astype(BF16), preferred_element_type=F32) + b_ref[...]


def _mods(c_rows, mod_w, mod_b):
    d = D_MODEL
    tn = 1536
    return pl.pallas_call(
        _mods_kernel,
        grid=(DEPTH, 6 * d // tn),
        in_specs=[pl.BlockSpec((MOD_ROWS, d), lambda l, j: (0, 0)),
                  pl.BlockSpec((None, d, tn), lambda l, j: (l, 0, j)),
                  pl.BlockSpec((None, 1, tn), lambda l, j: (l, 0, j))],
        out_specs=pl.BlockSpec((None, MOD_ROWS, tn), lambda l, j: (l, 0, j)),
        out_shape=jax.ShapeDtypeStruct((DEPTH, MOD_ROWS, 6 * d), F32),
        compiler_params=_cparams(2), name="mods",
    )(c_rows, mod_w, mod_b.reshape(DEPTH, 1, 6 * d))


def _modmm_kernel(*refs, n_groups, rope_groups, scaled_groups, chunk, conv_tiles):
    if rope_groups:
        x_ref, sc_ref, sh_ref, w_ref, cos_ref, sin_ref, o_ref = refs
        cos, sin = cos_ref[...], sin_ref[...]
        lane = lax.broadcasted_iota(I32, (1, LANES), 1)
        first_half = (lane % HEAD_DIM) < HEAD_DIM // 2
    elif conv_tiles:
        x_ref, sc_ref, sh_ref, w_ref, xp_ref, xn_ref, cw_ref, cb_ref, o_ref = refs
    else:
        x_ref, sc_ref, sh_ref, w_ref, o_ref = refs
    mod = lambda x: x * (1.0 + sc_ref[...]) + sh_ref[...]
    tm = x_ref.shape[0]
    if conv_tiles:
        pos = pl.program_id(0) % conv_tiles
        before = jnp.where(pos == 0, 0.0, mod(xp_ref[...]))
        after = jnp.where(pos == conv_tiles - 1, 0.0, mod(xn_ref[...]))
        h = jnp.concatenate([before, mod(x_ref[...]), after], 0).astype(BF16)
        ext = h.shape[0]
    else:
        h = mod(x_ref[...]).astype(BF16)
    scales = dict(scaled_groups)
    gpc = chunk // LANES
    n_chunks = n_groups // gpc
    project = lambda c: jnp.dot(h, w_ref[:, c * chunk:(c + 1) * chunk], preferred_element_type=F32)
    nxt = project(0)
    for c in range(n_chunks):
        acc = nxt
        if c + 1 < n_chunks:
            nxt = project(c + 1)
        if conv_tiles:
            cw, cb = cw_ref[:, c * chunk:(c + 1) * chunk], cb_ref[:, c * chunk:(c + 1) * chunk]
            acc = (pltpu.roll(acc, 1, 0) * cw[0:1] + acc * cw[1:2] + pltpu.roll(acc, ext - 1, 0) * cw[2:3]
                   + cb)[SUBLANES:SUBLANES + tm]
        for j in range(gpc):
            g = c * gpc + j
            blk = acc[:, j * LANES:(j + 1) * LANES]
            if g in rope_groups:
                rot = jnp.where(first_half, pltpu.roll(blk, LANES - HEAD_DIM // 2, 1), pltpu.roll(blk, HEAD_DIM // 2, 1))
                blk = blk * cos + rot * sin
                if g in scales:
                    blk = blk * scales[g]
            o_ref[:, g * LANES:(g + 1) * LANES] = blk.astype(o_ref.dtype)


def _modmm(x2d, sc, sh, w, rows_per_mod, tm, rope=None, conv=None, name="modmm"):
    t, d = x2d.shape
    n = w.shape[1]
    tiles_per_mod = rows_per_mod // tm
    nmod = sc.shape[0]
    in_specs = [pl.BlockSpec((tm, d), lambda i: (i, 0)),
                pl.BlockSpec((None, 1, d), lambda i: (i // tiles_per_mod, 0, 0)),
                pl.BlockSpec((None, 1, d), lambda i: (i // tiles_per_mod, 0, 0)),
                pl.BlockSpec((d, n), lambda i: (0, 0))]
    args = [x2d, sc.reshape(nmod, 1, d), sh.reshape(nmod, 1, d), w]
    rope_groups, scaled_groups = (), ()
    if rope is not None:
        cos, sin, rope_groups, scaled_groups = rope
        in_specs += [pl.BlockSpec((tm, LANES), lambda i: (i % tiles_per_mod, 0)),
                     pl.BlockSpec((tm, LANES), lambda i: (i % tiles_per_mod, 0))]
        args += [cos, sin]
    if conv is not None:
        per_tile = tm // SUBLANES
        last = t // SUBLANES - 1
        in_specs += [pl.BlockSpec((SUBLANES, d), lambda i: (jnp.maximum(i * per_tile - 1, 0), 0)),
                     pl.BlockSpec((SUBLANES, d), lambda i: (jnp.minimum((i + 1) * per_tile, last), 0)),
                     pl.BlockSpec((3, n), lambda i: (0, 0)), pl.BlockSpec((1, n), lambda i: (0, 0))]
        args += [x2d, x2d, conv[0], conv[1]]
    kern = functools.partial(_modmm_kernel, n_groups=n // LANES, rope_groups=tuple(rope_groups),
                             scaled_groups=tuple(scaled_groups), chunk=256,
                             conv_tiles=tiles_per_mod if conv is not None else 0)
    return pl.pallas_call(
        kern, grid=(t // tm,), in_specs=in_specs,
        out_specs=pl.BlockSpec((tm, n), lambda i: (i, 0)),
        out_shape=jax.ShapeDtypeStruct((t, n), BF16),
        compiler_params=_cparams(1), name=name,
    )(*args)


def _diff_attn_kernel(q_ref, k_ref, v_ref, kc_ref, vc_ref, lam_ref, g_ref, o_ref, *, lam_init, sub):
    lane = lax.broadcasted_iota(I32, (1, LANES), 1)
    k, kc = k_ref[...], kc_ref[...]
    v, vc = v_ref[...], vc_ref[...]

    def scores(r0):
        q = q_ref[r0:r0 + sub, :]
        zero = jnp.zeros_like(q)
        out = []
        for qm in (jnp.where(lane < HEAD_DIM, q, zero), jnp.where(lane >= HEAD_DIM, q, zero)):
            out.append((lax.dot_general(qm, k, NT_DIMS, preferred_element_type=F32),
                        lax.dot_general(qm, kc, NT_DIMS, preferred_element_type=F32)))
        return out

    def probs(sl, sc):
        m = jnp.maximum(jnp.max(sl, -1, keepdims=True), jnp.max(sc, -1, keepdims=True))
        pl_, pc = jnp.exp2(sl - m), jnp.exp2(sc - m)
        den = jnp.sum(pl_, -1, keepdims=True) + jnp.sum(pc, -1, keepdims=True)
        return pl_, pc, 1.0 / den

    lv = lam_ref[...]
    lam = (jnp.exp(jnp.sum(lv[0:1] * lv[1:2], keepdims=True)) - jnp.exp(jnp.sum(lv[2:3] * lv[3:4], keepdims=True))
           + lam_init)
    starts = list(range(0, q_ref.shape[0], sub))
    nxt = scores(starts[0])
    for i, r0 in enumerate(starts):
        cur = nxt
        if i + 1 < len(starts):
            nxt = scores(starts[i + 1])
        p1l, p1c, r1 = probs(*cur[0])
        p2l, p2c, r2 = probs(*cur[1])
        w2 = lam * r2
        al = (p1l * r1 - p2l * w2).astype(BF16)
        ac = (p1c * r1 - p2c * w2).astype(BF16)
        o = jnp.dot(al, v, preferred_element_type=F32) + jnp.dot(ac, vc, preferred_element_type=F32)
        ms = jnp.mean(o * o, -1, keepdims=True)
        o_ref[r0:r0 + sub, :] = (o * lax.rsqrt(ms + LN_EPS) * g_ref[...] * (1.0 - lam_init)).astype(o_ref.dtype)


def _diff_attn(proj, proj_c, lam_vec, subln_g, lam_init, tq, sub):
    b, n, _ = proj.shape
    nc = proj_c.shape[1]
    kcol = Q_W // LANES
    vcol = (Q_W + DIFF_QK_W) // LANES
    vccol = DIFF_QK_W // LANES
    kern = functools.partial(_diff_attn_kernel, lam_init=lam_init, sub=sub)
    return pl.pallas_call(
        kern, grid=(b, DIFF_HEADS, n // tq),
        in_specs=[pl.BlockSpec((None, tq, LANES), lambda bi, h, i: (bi, i, h)),
                  pl.BlockSpec((None, n, LANES), lambda bi, h, i: (bi, 0, kcol + h)),
                  pl.BlockSpec((None, n, LANES), lambda bi, h, i: (bi, 0, vcol + h)),
                  pl.BlockSpec((None, nc, LANES), lambda bi, h, i: (bi, 0, h)),
                  pl.BlockSpec((None, nc, LANES), lambda bi, h, i: (bi, 0, vccol + h)),
                  pl.BlockSpec((4, HEAD_DIM), lambda bi, h, i: (0, 0)),
                  pl.BlockSpec((1, LANES), lambda bi, h, i: (0, 0))],
        out_specs=pl.BlockSpec((None, tq, LANES), lambda bi, h, i: (bi, i, h)),
        out_shape=jax.ShapeDtypeStruct((b, n, DIFF_VW), BF16),
        compiler_params=_cparams(3), name="diff_attn",
    )(proj, proj, proj, proj_c, proj_c, lam_vec, subln_g.reshape(1, LANES))


def _win_attn_kernel(sink_ref, q_ref, k_ref, v_ref, kc_ref, vc_ref, o_ref, *, seq, tq):
    n = pl.program_id(1)
    gq = WIN_Q_HEADS // WIN_KV_HEADS
    kw = tq + 2 * WINDOW
    start = pl.multiple_of(jnp.clip(n * tq - WINDOW, 0, seq - kw), WINDOW)
    k_win, v_win = k_ref[pl.ds(start, kw), :], v_ref[pl.ds(start, kw), :]
    kc, vc = kc_ref[...], vc_ref[...]
    lane = lax.broadcasted_iota(I32, (1, LANES), 1)
    row = lax.broadcasted_iota(I32, (gq * tq, 1), 0)
    q_abs = n * tq + row % tq
    k_abs = start + lax.broadcasted_iota(I32, (1, kw), 1)
    allowed = jnp.abs(q_abs - k_abs) <= WINDOW
    head = row // tq
    q = q_ref[...].astype(F32)
    for g in range(WIN_KV_HEADS):
        in_g = (lane // HEAD_DIM) == g
        parts = []
        for j in range(gq):
            hq = g * gq + j
            x = q[:, (hq // 2) * LANES:(hq // 2 + 1) * LANES]
            if hq % 2 != g:
                x = pltpu.roll(x, HEAD_DIM, 1)
            parts.append(jnp.where(in_g, x, 0.0))
        qs = jnp.concatenate(parts, 0).astype(BF16)
        s_loc = lax.dot_general(qs, k_win, NT_DIMS, preferred_element_type=F32)
        s_ctx = lax.dot_general(qs, kc, NT_DIMS, preferred_element_type=F32)
        s_loc = jnp.where(allowed, s_loc, NEG_INF)
        sk = jnp.zeros((gq * tq, 1), F32)
        for j in range(gq):
            sk = jnp.where(head == j, sink_ref[g * gq + j] * LOG2_E, sk)
        m = jnp.maximum(jnp.maximum(jnp.max(s_loc, -1, keepdims=True), jnp.max(s_ctx, -1, keepdims=True)), sk)
        p_loc, p_ctx = jnp.exp2(s_loc - m), jnp.exp2(s_ctx - m)
        den = jnp.sum(p_loc, -1, keepdims=True) + jnp.sum(p_ctx, -1, keepdims=True) + jnp.exp2(sk - m)
        o = (jnp.dot(p_loc.astype(BF16), v_win, preferred_element_type=F32)
             + jnp.dot(p_ctx.astype(BF16), vc, preferred_element_type=F32)) * (1.0 / den)
        for cb in range(gq // 2):
            pair = []
            for half in range(2):
                piece = o[(2 * cb + half) * tq:(2 * cb + half + 1) * tq]
                pair.append(piece if half == g else pltpu.roll(piece, HEAD_DIM, 1))
            col = (g * gq // 2 + cb) * LANES
            o_ref[:, col:col + LANES] = jnp.where(lane < HEAD_DIM, pair[0], pair[1]).astype(o_ref.dtype)


def _win_attn(proj, proj_c, sink, tq):
    b, n, _ = proj.shape
    nc = proj_c.shape[1]
    qcol = DIFF_QK_W // WIN_Q_W
    kcol = (Q_W + DIFF_QK_W + DIFF_VW) // LANES
    kccol = (DIFF_QK_W + DIFF_VW) // LANES
    kern = functools.partial(_win_attn_kernel, seq=n, tq=tq)
    return pl.pallas_call(
        kern, grid=(b, n // tq),
        in_specs=[pl.BlockSpec(memory_space=pltpu.SMEM),
                  pl.BlockSpec((None, tq, WIN_Q_W), lambda bi, i: (bi, i, qcol)),
                  pl.BlockSpec((None, n, LANES), lambda bi, i: (bi, 0, kcol)),
                  pl.BlockSpec((None, n, LANES), lambda bi, i: (bi, 0, kcol + 1)),
                  pl.BlockSpec((None, nc, LANES), lambda bi, i: (bi, 0, kccol)),
                  pl.BlockSpec((None, nc, LANES), lambda bi, i: (bi, 0, kccol + 1))],
        out_specs=pl.BlockSpec((None, tq, WIN_Q_W), lambda bi, i: (bi, i, 0)),
        out_shape=jax.ShapeDtypeStruct((b, n, WIN_Q_W), BF16),
        compiler_params=_cparams(2), name="win_attn",
    )(sink, proj, proj, proj, proj_c, proj_c)


def _layer_norm(r, g, b):
    mu = jnp.mean(r, -1, keepdims=True)
    dlt = r - mu
    var = jnp.mean(dlt * dlt, -1, keepdims=True)
    return dlt * lax.rsqrt(var + LN_EPS) * g + b


def _proj_ln_kernel(*refs, n_in, sub):
    a_refs = refs[:n_in]
    w_refs = refs[n_in:2 * n_in]
    x_ref, gate_ref, lng_ref, lnb_ref, sc_ref, sh_ref, rw_ref, x1_ref, h2_ref, lg_ref = refs[2 * n_in:]

    def split(v):
        hi = v.astype(BF16)
        return hi, (v - hi.astype(F32)).astype(BF16)

    def project(r0):
        y = jnp.dot(a_refs[0][r0:r0 + sub, :], w_refs[0][...], preferred_element_type=F32)
        for a_ref, w_ref in zip(a_refs[1:], w_refs[1:]):
            y = y + jnp.dot(a_ref[r0:r0 + sub, :], w_ref[...], preferred_element_type=F32)
        return y

    rw_hi, rw_lo = split(rw_ref[...])
    rw_both = jnp.concatenate([rw_hi, rw_lo], 1)
    starts = list(range(0, x_ref.shape[0], sub))
    nxt = project(starts[0])
    for i, r0 in enumerate(starts):
        y = nxt
        if i + 1 < len(starts):
            nxt = project(starts[i + 1])
        xn = _layer_norm(DEEPNORM_ALPHA * x_ref[r0:r0 + sub, :] + gate_ref[...] * y, lng_ref[...], lnb_ref[...])
        x1_ref[r0:r0 + sub, :] = xn
        h2 = xn * (1.0 + sc_ref[...]) + sh_ref[...]
        h2_ref[r0:r0 + sub, :] = h2.astype(h2_ref.dtype)
        h_hi, h_lo = split(h2)
        hh = jnp.dot(h_hi, rw_both, preferred_element_type=F32)
        lg = hh[:, :LANES] + hh[:, LANES:] + jnp.dot(h_lo, rw_hi, preferred_element_type=F32)
        lg_ref[:, r0:r0 + sub] = lg.T[:N_EXPERTS, :]


def _proj_ln(acts, ws, x2d, gate, ln_g, ln_b, sc, sh, router_wt, rows_per_mod, tm, name):
    t, d = x2d.shape
    n_in = len(acts)
    tiles_per_mod = rows_per_mod // tm
    nmod = gate.shape[0]
    row = lambda i: (i, 0)
    full = lambda i: (0, 0)
    mod = lambda i: (i // tiles_per_mod, 0, 0)
    in_specs = ([pl.BlockSpec((tm, a.shape[1]), row) for a in acts]
                + [pl.BlockSpec(w.shape, full) for w in ws]
                + [pl.BlockSpec((tm, d), row), pl.BlockSpec((None, 1, d), mod),
                   pl.BlockSpec((1, d), full), pl.BlockSpec((1, d), full),
                   pl.BlockSpec((None, 1, d), mod), pl.BlockSpec((None, 1, d), mod),
                   pl.BlockSpec((d, LANES), full)])
    return pl.pallas_call(
        functools.partial(_proj_ln_kernel, n_in=n_in, sub=LANES), grid=(t // tm,), in_specs=in_specs,
        out_specs=[pl.BlockSpec((tm, d), row), pl.BlockSpec((tm, d), row),
                   pl.BlockSpec((N_EXPERTS, tm), lambda i: (0, i))],
        out_shape=[jax.ShapeDtypeStruct((t, d), F32), jax.ShapeDtypeStruct((t, d), BF16),
                   jax.ShapeDtypeStruct((N_EXPERTS, t), F32)],
        compiler_params=_cparams(1), name=name,
    )(*acts, *ws, x2d, gate.reshape(nmod, 1, d), ln_g.reshape(1, d), ln_b.reshape(1, d),
      sc.reshape(nmod, 1, d), sh.reshape(nmod, 1, d), router_wt)


def _first_argmax(vals):
    idx = jnp.zeros(vals[0].shape, I32)
    best = vals[0]
    for j in range(1, len(vals)):
        upd = vals[j] > best
        idx = jnp.where(upd, j, idx)
        best = jnp.where(upd, vals[j], best)
    return idx, best


def _route_kernel(lg_ref, bias_ref, pos_ref, w_ref, tab_ref, cnt_ref, carry_ref, off_ref, *, tr):
    @pl.when(pl.program_id(0) == 0)
    def _():
        carry_ref[...] = jnp.zeros_like(carry_ref)

    lg = lg_ref[...]
    ex = jnp.exp(lg - jnp.max(lg, 0, keepdims=True))
    scores = ex / jnp.sum(ex, 0, keepdims=True)
    sel = scores + bias_ref[...]
    rows = [sel[e:e + 1] for e in range(N_EXPERTS)]
    group_scores = []
    for g in range(N_GROUPS):
        r = rows[g * EXPERTS_PER_GROUP:(g + 1) * EXPERTS_PER_GROUP]
        best = None
        for i in range(EXPERTS_PER_GROUP):
            for j in range(i + 1, EXPERTS_PER_GROUP):
                s = r[i] + r[j]
                best = s if best is None else jnp.maximum(best, s)
        group_scores.append(best)
    grp, _ = _first_argmax(group_scores)
    vals = []
    for j in range(EXPERTS_PER_GROUP):
        v = rows[(N_GROUPS - 1) * EXPERTS_PER_GROUP + j]
        for g in range(N_GROUPS - 2, -1, -1):
            v = jnp.where(grp == g, rows[g * EXPERTS_PER_GROUP + j], v)
        vals.append(v)
    i0, _ = _first_argmax(vals)
    i1, _ = _first_argmax([jnp.where(i0 == j, -jnp.inf, vals[j]) for j in range(EXPERTS_PER_GROUP)])
    e0 = grp * EXPERTS_PER_GROUP + i0
    e1 = grp * EXPERTS_PER_GROUP + i1
    eid = lax.broadcasted_iota(I32, (N_EXPERTS, 1), 0)
    oh0, oh1 = eid == e0, eid == e1
    s0 = jnp.sum(jnp.where(oh0, scores, 0.0), 0, keepdims=True)
    s1 = jnp.sum(jnp.where(oh1, scores, 0.0), 0, keepdims=True)
    den = s0 + s1
    member = jnp.where(oh0 | oh1, 1.0, 0.0)
    before = lax.broadcasted_iota(I32, (tr, tr), 0) < lax.broadcasted_iota(I32, (tr, tr), 1)
    upper = jnp.where(before, 1.0, 0.0).astype(BF16)
    cnt = jnp.dot(member.astype(BF16), upper, preferred_element_type=F32)
    run = jnp.sum(member, 1, keepdims=True)
    run = jnp.floor((run + (ROW_ALIGN - 1)) * (1.0 / ROW_ALIGN)) * ROW_ALIGN
    run = jnp.broadcast_to(run, (N_EXPERTS, LANES))
    acc = jnp.zeros((1, LANES), F32)
    for e in range(N_EXPERTS):
        off_ref[e:e + 1, :] = acc
        acc = acc + run[e:e + 1]
    off = off_ref[...]
    at = off[:, 0:1] + cnt
    pos_ref[0:1, :] = jnp.sum(jnp.where(oh0, at, 0.0), 0, keepdims=True).astype(I32)
    pos_ref[1:2, :] = jnp.sum(jnp.where(oh1, at, 0.0), 0, keepdims=True).astype(I32)
    w_ref[0:1, :] = s0 / den
    w_ref[1:2, :] = s1 / den
    tab_ref[0] = off.astype(I32)
    tab_ref[1] = run.astype(I32)
    tab_ref[2] = carry_ref[...].astype(I32)
    carry_ref[...] = carry_ref[...] + run
    cnt_ref[...] = carry_ref[...]


def _route(logits_t, router_bias, tr):
    t = logits_t.shape[1]
    tok = lambda i: (0, i)
    return pl.pallas_call(
        functools.partial(_route_kernel, tr=tr), grid=(t // tr,),
        in_specs=[pl.BlockSpec((N_EXPERTS, tr), tok), pl.BlockSpec((N_EXPERTS, 1), lambda i: (0, 0))],
        out_specs=[pl.BlockSpec((2, tr), tok), pl.BlockSpec((2, tr), tok),
                   pl.BlockSpec((None, 3, N_EXPERTS, LANES), lambda i: (i, 0, 0, 0)),
                   pl.BlockSpec((N_EXPERTS, LANES), lambda i: (0, 0))],
        out_shape=[jax.ShapeDtypeStruct((2, t), I32), jax.ShapeDtypeStruct((2, t), F32),
                   jax.ShapeDtypeStruct((t // tr, 3, N_EXPERTS, LANES), I32),
                   jax.ShapeDtypeStruct((N_EXPERTS, LANES), F32)],
        scratch_shapes=[pltpu.VMEM((N_EXPERTS, LANES), F32), pltpu.VMEM((N_EXPERTS, LANES), F32)],
        compiler_params=_cparams(1), name="route",
    )(logits_t, router_bias.reshape(N_EXPERTS, 1))


def _run_copies(n, max_rows, make_copy, wait):
    sz = max_rows
    while sz >= ROW_ALIGN:
        start = (n // (2 * sz)) * (2 * sz)

        @pl.when((n & sz) != 0)
        def _(start=start, sz=sz):
            cp = make_copy(start, sz)
            cp.wait() if wait else cp.start()

        sz //= 2


def _wait_rows(total, max_rows, make_copy):
    sz = max_rows
    while sz >= ROW_ALIGN:
        @pl.when((total & sz) != 0)
        def _(sz=sz):
            make_copy(sz).wait()

        sz //= 2


def _rows(ref, start, size):
    return ref.at[pl.ds(pl.multiple_of(start, ROW_ALIGN), size)]


def _dispatch_kernel(tab_ref, h_ref, pos_ref, w_ref, xs_ref, buf, zbuf, sem, *, tt, n_tiles, bm):
    i = pl.program_id(0)
    d = h_ref.shape[1]
    sb = buf.shape[1]
    ne = N_EXPERTS
    fill = 3 * n_tiles * ne

    @pl.when(i == 0)
    def _():
        zbuf[...] = jnp.zeros_like(zbuf)
        for wait in (False, True):
            for e in range(ne):
                dst, n = tab_ref[fill + e], tab_ref[fill + ne + e]
                _run_copies(n, bm // 2, lambda s, z: pltpu.make_async_copy(
                    zbuf.at[pl.ds(0, z)], _rows(xs_ref, dst + s, z), sem.at[2]), wait)

        def zero_block(j, carry):
            for half in range(2):
                cp = pltpu.make_async_copy(zbuf, _rows(xs_ref, j * bm + half * (bm // 2), bm // 2), sem.at[2])
                cp.start()
                cp.wait()
            return carry

        lax.fori_loop(tab_ref[fill + 2 * ne], xs_ref.shape[0] // bm, zero_block, 0)

    def tile_copies(tile, slot, wait):
        for e in range(ne):
            off = tab_ref[tile * ne + e]
            n = tab_ref[(n_tiles + tile) * ne + e]
            dst = tab_ref[(2 * n_tiles + tile) * ne + e]
            _run_copies(n, tt, lambda s, z: pltpu.make_async_copy(
                _rows(buf.at[slot], off + s, z), _rows(xs_ref, dst + s, z), sem.at[slot]), wait)

    slot = i % 2
    pos, w = pos_ref[...], w_ref[...]
    hb = h_ref[...].astype(BF16)
    for r0 in range(0, sb, LANES):
        srow = r0 + lax.broadcasted_iota(I32, (LANES, 1), 0)
        m0, m1 = srow == pos[0:1], srow == pos[1:2]
        perm = (jnp.where(m0, 1.0, 0.0) + jnp.where(m1, 1.0, 0.0)).astype(BF16)
        buf[slot, r0:r0 + LANES, :d] = jnp.dot(perm, hb, preferred_element_type=F32)
        wrow = jnp.sum(jnp.where(m0, w[0:1], 0.0) + jnp.where(m1, w[1:2], 0.0), 1, keepdims=True)
        buf[slot, r0:r0 + LANES, d:] = jnp.broadcast_to(wrow, (LANES, LANES))

    def wait_tile(tile, slot):
        total = tab_ref[fill + 2 * ne + 1 + tile]
        _wait_rows(total, tt * 2, lambda z: pltpu.make_async_copy(
            buf.at[slot, pl.ds(0, z)], xs_ref.at[pl.ds(0, z)], sem.at[slot]))

    @pl.when(i > 0)
    def _():
        wait_tile(i - 1, 1 - slot)

    tile_copies(i, slot, False)

    @pl.when(i == n_tiles - 1)
    def _():
        wait_tile(i, slot)


def _dispatch(tab, h2, pos, wts, n_slots, tt, sb, bm):
    t, d = h2.shape
    n_tiles = t // tt
    return pl.pallas_call(
        functools.partial(_dispatch_kernel, tt=tt, n_tiles=n_tiles, bm=bm),
        grid_spec=pltpu.PrefetchScalarGridSpec(
            num_scalar_prefetch=1, grid=(n_tiles,),
            in_specs=[pl.BlockSpec((tt, d), lambda i, s: (i, 0)), pl.BlockSpec((2, tt), lambda i, s: (0, i)),
                      pl.BlockSpec((2, tt), lambda i, s: (0, i))],
            out_specs=pl.BlockSpec(memory_space=pl.ANY),
            scratch_shapes=[pltpu.VMEM((2, sb, d + LANES), F32), pltpu.VMEM((bm // 2, d + LANES), F32),
                            pltpu.SemaphoreType.DMA((3,))]),
        out_shape=jax.ShapeDtypeStruct((n_slots, d + LANES), F32),
        compiler_params=_cparams(1), name="moe_dispatch",
    )(tab, h2, pos, wts)


def _ffn_kernel(blk_e_ref, blk_rows_ref, next_e_ref, nused_ref, x_ref, wg_hbm, wu_hbm, wd_hbm, o_ref,
                stage, wg_bf, wu_bf, wd_bf, sem, *, layer, parts):
    del nused_ref
    i = pl.program_id(0)
    e = blk_e_ref[i]
    new_expert = (i == 0) | (e != blk_e_ref[jnp.maximum(i - 1, 0)])

    def fetch(expert, wait):
        for j, src in enumerate((wg_hbm, wu_hbm, wd_hbm)):
            cp = pltpu.make_async_copy(src.at[layer, expert], stage.at[j], sem.at[j])
            cp.wait() if wait else cp.start()

    @pl.when(i == 0)
    def _():
        fetch(e, False)

    @pl.when(new_expert)
    def _():
        fetch(e, True)
        wg_bf[...] = stage[0].astype(BF16)
        wu_bf[...] = stage[1].astype(BF16)
        wd_bf[...] = stage[2].astype(BF16)

        @pl.when(next_e_ref[i] >= 0)
        def _():
            fetch(next_e_ref[i], False)

    d = wg_bf.shape[0]
    rows = blk_rows_ref[i]
    part = x_ref.shape[0] // parts
    spans = [slice(p * part, (p + 1) * part) for p in range(parts)]

    def gate_up(rs):
        x = x_ref[rs, :d].astype(BF16)
        return (jnp.dot(x, wg_bf[...], preferred_element_type=F32), jnp.dot(x, wu_bf[...], preferred_element_type=F32))

    def finish(rs, gate, up):
        act = (gate * jax.nn.sigmoid(gate) * up).astype(BF16)
        o_ref[rs, :] = jnp.dot(act, wd_bf[...], preferred_element_type=F32) * x_ref[rs, d:d + 1]

    all_parts = rows > (parts - 1) * part

    @pl.when(all_parts)
    def _():
        nxt = gate_up(spans[0])
        for p, rs in enumerate(spans):
            cur = nxt
            if p + 1 < parts:
                nxt = gate_up(spans[p + 1])
            finish(rs, *cur)

    for p, rs in enumerate(spans):
        @pl.when(jnp.logical_not(all_parts) & (rows > p * part))
        def _(rs=rs):
            finish(rs, *gate_up(rs))

        @pl.when(rows <= p * part)
        def _(rs=rs):
            o_ref[rs, :] = jnp.zeros((part, d), F32)


def _ffn(blk_e, blk_rows, next_e, n_used, xs, wg, wu, wd, layer, bm, parts):
    n_slots, xw = xs.shape
    d, ff = wg.shape[2:]
    assert d == ff
    xrow = lambda i, be, br, ne, nu: (jnp.minimum(i, nu[0] - 1), 0)
    hbm = pl.BlockSpec(memory_space=pl.ANY)
    return pl.pallas_call(
        functools.partial(_ffn_kernel, layer=layer, parts=parts),
        grid_spec=pltpu.PrefetchScalarGridSpec(
            num_scalar_prefetch=4, grid=(n_slots // bm,),
            in_specs=[pl.BlockSpec((bm, xw), xrow), hbm, hbm, hbm],
            out_specs=pl.BlockSpec((bm, d), lambda i, be, br, ne, nu: (i, 0)),
            scratch_shapes=[pltpu.VMEM((3, d, ff), F32), pltpu.VMEM((d, ff), BF16), pltpu.VMEM((d, ff), BF16),
                            pltpu.VMEM((ff, d), BF16), pltpu.SemaphoreType.DMA((3,))]),
        out_shape=jax.ShapeDtypeStruct((n_slots, d), F32),
        compiler_params=_cparams(1), name="moe_ffn",
    )(blk_e, blk_rows, next_e, n_used, xs, wg, wu, wd)


def _combine_kernel(tab_ref, ys_ref, pos_ref, x1_ref, gate_ref, lng_ref, lnb_ref, o_ref, ybuf, sem, *, tt, n_tiles):
    i = pl.program_id(0)
    sb = ybuf.shape[1]
    ne = N_EXPERTS
    slot = i % 2

    def tile_copies(tile, slot, wait):
        for e in range(ne):
            off = tab_ref[tile * ne + e]
            n = tab_ref[(n_tiles + tile) * ne + e]
            src = tab_ref[(2 * n_tiles + tile) * ne + e]
            _run_copies(n, tt, lambda s, z: pltpu.make_async_copy(
                _rows(ys_ref, src + s, z), _rows(ybuf.at[slot], off + s, z), sem.at[slot]), wait)

    @pl.when(i == 0)
    def _():
        ybuf[...] = jnp.zeros_like(ybuf)
        tile_copies(0, 0, False)

    @pl.when(i + 1 < n_tiles)
    def _():
        tile_copies(i + 1, 1 - slot, False)

    total = tab_ref[3 * n_tiles * ne + 2 * ne + 1 + i]
    _wait_rows(total, tt * 2, lambda z: pltpu.make_async_copy(
        ys_ref.at[pl.ds(0, z)], ybuf.at[slot, pl.ds(0, z)], sem.at[slot]))
    scol = lax.broadcasted_iota(I32, (1, sb), 1)
    yb = ybuf[slot].astype(BF16)

    def moe_rows(r0):
        pos = pos_ref[r0:r0 + LANES, :]
        unperm = (jnp.where(scol == pos[:, 0:1], 1.0, 0.0) + jnp.where(scol == pos[:, 1:2], 1.0, 0.0)).astype(BF16)
        return jnp.dot(unperm, yb, preferred_element_type=F32)

    starts = list(range(0, tt, LANES))
    nxt = moe_rows(starts[0])
    for j, r0 in enumerate(starts):
        f = nxt
        if j + 1 < len(starts):
            nxt = moe_rows(starts[j + 1])
        o_ref[r0:r0 + LANES, :] = _layer_norm(DEEPNORM_ALPHA * x1_ref[r0:r0 + LANES, :] + gate_ref[...] * f,
                                              lng_ref[...], lnb_ref[...])


def _combine(tab, ys, pos_t, x1, gate, ln_g, ln_b, rows_per_mod, tt, sb):
    t, d = x1.shape
    tiles_per_mod = rows_per_mod // tt
    nmod = gate.shape[0]
    return pl.pallas_call(
        functools.partial(_combine_kernel, tt=tt, n_tiles=t // tt),
        grid_spec=pltpu.PrefetchScalarGridSpec(
            num_scalar_prefetch=1, grid=(t // tt,),
            in_specs=[pl.BlockSpec(memory_space=pl.ANY),
                      pl.BlockSpec((tt, 2), lambda i, s: (i, 0)),
                      pl.BlockSpec((tt, d), lambda i, s: (i, 0)),
                      pl.BlockSpec((None, 1, d), lambda i, s: (i // tiles_per_mod, 0, 0)),
                      pl.BlockSpec((1, d), lambda i, s: (0, 0)), pl.BlockSpec((1, d), lambda i, s: (0, 0))],
            out_specs=pl.BlockSpec((tt, d), lambda i, s: (i, 0)),
            scratch_shapes=[pltpu.VMEM((2, sb, d), F32), pltpu.SemaphoreType.DMA((2,))]),
        out_shape=jax.ShapeDtypeStruct((t, d), F32),
        compiler_params=_cparams(1), name="moe_combine",
    )(tab, ys, pos_t, x1, gate.reshape(nmod, 1, d), ln_g.reshape(1, d), ln_b.reshape(1, d))


def _moe(h2, logits_t, x1, gate, ln_g, ln_b, router_bias, wg, wu, wd, layer, rows_per_mod):
    t, d = h2.shape
    bm, tt, ne = MOE_BM, MOE_TILE, N_EXPERTS
    n_tiles = t // tt
    pad = ne * (ROW_ALIGN - 1)
    sb = -(-(2 * tt + pad) // LANES) * LANES
    pos, wts, tab3, cnt = _route(logits_t, router_bias, tr=tt)
    rows = cnt[:, 0].astype(I32)
    prows = (rows + bm - 1) // bm * bm
    pends = jnp.cumsum(prows)
    pstarts = pends - prows
    n_blk = -(-(2 * t + n_tiles * pad + ne * (bm - 1)) // bm)
    n_used = pends[-1] // bm
    blk_ids = jnp.minimum(jnp.arange(n_blk, dtype=I32), n_used - 1)
    blk_e = jnp.minimum(jnp.sum((blk_ids[:, None] * bm >= pends[None, :]).astype(I32), 1), ne - 1)
    eids = jnp.arange(ne, dtype=I32)
    of_blk = lambda per_expert: jnp.sum(jnp.where(blk_e[:, None] == eids[None, :], per_expert[None, :], 0), 1)
    raw_ids = jnp.arange(n_blk, dtype=I32)
    blk_rows = jnp.where(raw_ids < n_used, jnp.clip(of_blk(pstarts + rows) - raw_ids * bm, 0, bm), 0)
    later_used = (eids[None, :] > eids[:, None]) & (prows[None, :] > 0)
    next_used = jnp.min(jnp.where(later_used, eids[None, :], ne), 1)
    next_e = of_blk(jnp.where(next_used < ne, next_used, -1))
    tab3 = tab3[:, :, :, 0]
    tab = jnp.concatenate([tab3[:, 0].reshape(-1), tab3[:, 1].reshape(-1),
                           (tab3[:, 2] + pstarts[None, :]).reshape(-1), pstarts + rows, prows - rows,
                           n_used.reshape(1), jnp.sum(tab3[:, 1], 1)]).astype(I32)
    xs = _dispatch(tab, h2, pos, wts, n_blk * bm, tt, sb, bm)
    ys = _ffn(blk_e, blk_rows.astype(I32), next_e.astype(I32), n_used.reshape(1).astype(I32), xs, wg, wu, wd,
              layer, bm, parts=bm // MOE_PART)
    return _combine(tab, ys, pos.T, x1, gate, ln_g, ln_b, rows_per_mod, tt, sb)


def _filter_mlp_kernel(z_ref, w1_ref, w2_ref, w3_ref, b_ref, sf_ref, o_ref):
    hp = lax.Precision.HIGHEST
    b, sf = b_ref[...], sf_ref[...]
    h = jnp.sin(sf[0:1] * (jnp.dot(z_ref[...], w1_ref[...], precision=hp, preferred_element_type=F32) + b[0:1]))
    h = jnp.sin(sf[1:2] * (jnp.dot(h, w2_ref[...], precision=hp, preferred_element_type=F32) + b[1:2]))
    o_ref[...] = jnp.sin(sf[2:3] * (jnp.dot(h, w3_ref[...], precision=hp, preferred_element_type=F32) + b[2:3]))


def _filter_mlp(zp, w_in_p, w_hid, b, sf):
    n = zp.shape[0]
    return pl.pallas_call(
        _filter_mlp_kernel, out_shape=jax.ShapeDtypeStruct((n, FILTER_HID), F32),
        compiler_params=pltpu.CompilerParams(vmem_limit_bytes=VMEM_LIMIT), name="hyena_filter_mlp",
    )(zp, w_in_p, w_hid[0], w_hid[1], b, sf)


def _filter_spec_kernel(hid_ref, wf_ref, wb_ref, dec_ref, fwd_ref, hr_ref, g_ref, nyq_ref, p_scr, q_scr, *, fb, kps):
    k = pl.program_id(2)
    n = hid_ref.shape[0]

    @pl.when(k == 0)
    def _():
        hp = lax.Precision.HIGHEST
        hid, dec = hid_ref[...], dec_ref[...]
        row = lax.broadcasted_iota(I32, (n, 1), 0)
        fw = jnp.dot(hid, wf_ref[...], precision=hp, preferred_element_type=F32) * dec
        bw = jnp.dot(hid, wb_ref[...], precision=hp, preferred_element_type=F32) * dec
        bw = jnp.where(row == 0, 0.0, bw)
        p = fw + bw
        p_scr[...] = p.astype(BF16)
        q_scr[...] = (fw - bw).astype(BF16)
        sign = (1 - 2 * (row & 1)).astype(F32)
        nyq_ref[...] = jnp.sum(p * sign, 0, keepdims=True)

    rowb = lax.broadcasted_iota(I32, (fb, 1), 0)
    for j in range(kps):
        hr_ref[j * fb:(j + 1) * fb, :] = jnp.dot(fwd_ref[j, 0:fb, :], p_scr[...], preferred_element_type=F32)
        gg = jnp.dot(fwd_ref[j, fb:2 * fb, :], q_scr[...], preferred_element_type=F32)
        if j == 0:
            gg = jnp.where((rowb == 0) & (k == 0), 0.0, gg)
        g_ref[j * fb:(j + 1) * fb, :] = gg


def _filter_spec(hid, w_out4, decay, fwd_tab, fb, tc, kps):
    n = hid.shape[0]
    d = D_MODEL
    kb = n // (fb * kps)
    return pl.pallas_call(
        functools.partial(_filter_spec_kernel, fb=fb, kps=kps), grid=(2, d // tc, kb),
        in_specs=[pl.BlockSpec((n, FILTER_HID), lambda o, c, k: (0, 0)),
                  pl.BlockSpec((None, None, FILTER_HID, tc), lambda o, c, k: (0, o, 0, c)),
                  pl.BlockSpec((None, None, FILTER_HID, tc), lambda o, c, k: (1, o, 0, c)),
                  pl.BlockSpec((n, tc), lambda o, c, k: (0, c)),
                  pl.BlockSpec((kps, 2 * fb, n), lambda o, c, k: (k, 0, 0))],
        out_specs=[pl.BlockSpec((None, kps * fb, tc), lambda o, c, k: (o, k, c)),
                   pl.BlockSpec((None, kps * fb, tc), lambda o, c, k: (o, k, c)),
                   pl.BlockSpec((None, 1, tc), lambda o, c, k: (o, 0, c))],
        out_shape=[jax.ShapeDtypeStruct((2, n, d), F32), jax.ShapeDtypeStruct((2, n, d), F32),
                   jax.ShapeDtypeStruct((2, 1, d), F32)],
        scratch_shapes=[pltpu.VMEM((n, tc), BF16), pltpu.VMEM((n, tc), BF16)],
        compiler_params=_cparams(3), name="hyena_filter_spec",
    )(hid, w_out4, w_out4, decay, fwd_tab)


def _short_conv(u_ref, cw_ref, cb_ref):
    n = u_ref.shape[0]
    u = u_ref[...].astype(F32)
    row = lax.broadcasted_iota(I32, (n, 1), 0)
    prev = jnp.where(row == 0, 0.0, pltpu.roll(u, 1, 0))
    nxt = jnp.where(row == n - 1, 0.0, pltpu.roll(u, n - 1, 0))
    cw = cw_ref[...]
    return prev * cw[0:1] + u * cw[1:2] + nxt * cw[2:3] + cb_ref[...]


def _hyena_conv_kernel(uv_ref, ux1_ref, ux2_ref, cwv_ref, cwx1_ref, cwx2_ref, cbv_ref, cbx1_ref, cbx2_ref,
                       fwd_ref, inv_ref, hr_ref, g_ref, nyq_ref, sk_ref, o_ref, zb_scr, z32_scr, acc_scr, *, fb, kb,
                       kps):
    o = pl.program_id(2)
    k = pl.program_id(3)

    @pl.when((o == 0) & (k == 0))
    def _():
        v = _short_conv(uv_ref, cwv_ref, cbv_ref)
        z32_scr[...] = v
        zb_scr[...] = v.astype(BF16)

    @pl.when(k == 0)
    def _():
        acc_scr[...] = jnp.zeros_like(acc_scr)

    zb = zb_scr[...]
    zfs = [jnp.dot(fwd_ref[j], zb, preferred_element_type=F32) for j in range(kps)]
    rowb = lax.broadcasted_iota(I32, (fb, 1), 0)
    for j in range(kps):
        zr, zi = zfs[j][:fb], zfs[j][fb:]
        hr, gg = hr_ref[j * fb:(j + 1) * fb, :], g_ref[j * fb:(j + 1) * fb, :]
        hb = jnp.where((rowb == 0) & (k == 0), nyq_ref[...], hr) if j == 0 else hr
        y = jnp.concatenate([zr * hr - zi * gg, zr * gg + zi * hb], 0).astype(BF16)
        acc_scr[...] += jnp.dot(inv_ref[j], y, preferred_element_type=F32)

    @pl.when((k == kb - 1) & (o == 0))
    def _():
        zn = _short_conv(ux1_ref, cwx1_ref, cbx1_ref) * (acc_scr[...] + z32_scr[...] * sk_ref[...])
        z32_scr[...] = zn
        zb_scr[...] = zn.astype(BF16)

    @pl.when((k == kb - 1) & (o == 1))
    def _():
        zn = _short_conv(ux2_ref, cwx2_ref, cbx2_ref) * (acc_scr[...] + z32_scr[...] * sk_ref[...])
        o_ref[...] = zn.astype(o_ref.dtype)


def _hyena_conv(u, conv_w, conv_b, fwd_tab, inv_tab, hr, gg, nyq, skip, fb, tc, kps):
    b, n, _ = u.shape
    d = D_MODEL
    kb = n // (fb * kps)
    ncb = d // tc
    ucol = lambda part: (lambda bi, c, o, k: (bi, 0, part * ncb + c))
    wcol = lambda part: (lambda bi, c, o, k: (0, part * ncb + c))
    spec = lambda bi, c, o, k: (o, k, c)
    per_o = lambda bi, c, o, k: (o, 0, c)
    return pl.pallas_call(
        functools.partial(_hyena_conv_kernel, fb=fb, kb=kb, kps=kps), grid=(b, ncb, 2, kb),
        in_specs=[pl.BlockSpec((None, n, tc), ucol(0)), pl.BlockSpec((None, n, tc), ucol(1)),
                  pl.BlockSpec((None, n, tc), ucol(2)),
                  pl.BlockSpec((3, tc), wcol(0)), pl.BlockSpec((3, tc), wcol(1)), pl.BlockSpec((3, tc), wcol(2)),
                  pl.BlockSpec((1, tc), wcol(0)), pl.BlockSpec((1, tc), wcol(1)), pl.BlockSpec((1, tc), wcol(2)),
                  pl.BlockSpec((kps, 2 * fb, n), lambda bi, c, o, k: (k, 0, 0)),
                  pl.BlockSpec((kps, n, 2 * fb), lambda bi, c, o, k: (k, 0, 0)),
                  pl.BlockSpec((None, kps * fb, tc), spec), pl.BlockSpec((None, kps * fb, tc), spec),
                  pl.BlockSpec((None, 1, tc), per_o), pl.BlockSpec((None, 1, tc), per_o)],
        out_specs=pl.BlockSpec((None, n, tc), lambda bi, c, o, k: (bi, 0, c)),
        out_shape=jax.ShapeDtypeStruct((b, n, d), BF16),
        scratch_shapes=[pltpu.VMEM((n, tc), BF16), pltpu.VMEM((n, tc), F32), pltpu.VMEM((n, tc), F32)],
        compiler_params=_cparams(4), name="hyena_conv",
    )(u, u, u, conv_w, conv_w, conv_w, conv_b, conv_b, conv_b, fwd_tab, inv_tab, hr, gg, nyq,
      skip.reshape(2, 1, d))


def _lag_spec_kernel(hid_ref, wf_ref, wb_ref, dec_ref, ph_ref, h_ref, fw_scr, bw_scr, *, s, nb):
    lag = pl.program_id(2)
    n = hid_ref.shape[0]

    @pl.when(lag == 0)
    def _():
        def split(v):
            hi = v.astype(BF16)
            return hi, (v - hi.astype(F32)).astype(BF16)

        def precise_dot(a_hi, a_lo, w):
            w_hi, w_lo = split(w)
            return (jnp.dot(a_hi, w_hi, preferred_element_type=F32) + jnp.dot(a_lo, w_hi, preferred_element_type=F32)
                    + jnp.dot(a_hi, w_lo, preferred_element_type=F32))

        dec = dec_ref[...]
        hid_hi, hid_lo = split(hid_ref[...])
        row = lax.broadcasted_iota(I32, (n, 1), 0)
        fw = precise_dot(hid_hi, hid_lo, wf_ref[...]) * dec
        bw = precise_dot(hid_hi, hid_lo, wb_ref[...]) * dec
        fw_scr[...] = fw.astype(BF16)
        bw_scr[...] = jnp.where(row == 0, 0.0, bw).astype(BF16)

    def piece(ph, scr, j):
        return jnp.dot(ph_ref[ph], scr[j * s:(j + 1) * s, :], preferred_element_type=F32)

    for idx in range(2 * nb - 1):
        m = idx - (nb - 1)

        @pl.when(lag == idx)
        def _(m=m):
            if m >= 1:
                h_ref[...] = piece(0, fw_scr, m) + piece(1, fw_scr, m - 1)
            elif m == 0:
                h_ref[...] = piece(0, fw_scr, 0) + piece(2, bw_scr, 0)
            else:
                h_ref[...] = piece(2, bw_scr, -m) + piece(3, bw_scr, -m - 1)


def _lag_spec(hid, w_out4, decay, phases, s, tc):
    n = hid.shape[0]
    d = D_MODEL
    nb = n // s
    return pl.pallas_call(
        functools.partial(_lag_spec_kernel, s=s, nb=nb), grid=(2, d // tc, 2 * nb - 1),
        in_specs=[pl.BlockSpec((n, FILTER_HID), lambda o, c, l: (0, 0)),
                  pl.BlockSpec((None, None, FILTER_HID, tc), lambda o, c, l: (0, o, 0, c)),
                  pl.BlockSpec((None, None, FILTER_HID, tc), lambda o, c, l: (1, o, 0, c)),
                  pl.BlockSpec((n, tc), lambda o, c, l: (0, c)),
                  pl.BlockSpec((4, 2 * s, s), lambda o, c, l: (0, 0, 0))],
        out_specs=pl.BlockSpec((None, None, 2 * s, tc), lambda o, c, l: (o, l, 0, c)),
        out_shape=jax.ShapeDtypeStruct((2, 2 * nb - 1, 2 * s, d), F32),
        scratch_shapes=[pltpu.VMEM((n, tc), BF16), pltpu.VMEM((n, tc), BF16)],
        compiler_params=_cparams(3), name="hyena_lag_spec",
    )(hid, w_out4, w_out4, decay, phases)


def _part_conv_kernel(uv_ref, ux1_ref, ux2_ref, f_ref, finv_ref, h_ref, sk_ref, o_ref, src_scr, zb_scr, zall, ybuf,
                      *, s, nb, rc):
    src_scr[...] = uv_ref[...].astype(F32)
    gates = (ux1_ref, ux2_ref)
    for o, gate_ref in enumerate(gates):
        zb_scr[...] = src_scr[...].astype(BF16)
        for j in range(nb):
            zall[j] = jnp.dot(f_ref[...], zb_scr[j * s:(j + 1) * s, :], preferred_element_type=F32)
        sk = sk_ref[o]
        for i in range(nb):
            pairs = [(j, i - j + nb - 1) for j in range(nb)]
            for r0 in range(0, s, rc):
                re, im = slice(r0, r0 + rc), slice(s + r0, s + r0 + rc)
                yr = yi = None
                for j, m in pairs:
                    zr, zi, hr, g = zall[j, re, :], zall[j, im, :], h_ref[o, m, re, :], h_ref[o, m, im, :]
                    tr, ti = zr * hr - zi * g, zr * g + zi * hr
                    yr, yi = (tr, ti) if yr is None else (yr + tr, yi + ti)
                ybuf[re, :] = yr
                ybuf[im, :] = yi
            dc = nyq = None
            for j, m in pairs:
                t0 = zall[j, 0:1, :] * h_ref[o, m, 0:1, :]
                t1 = zall[j, s:s + 1, :] * h_ref[o, m, s:s + 1, :]
                dc, nyq = (t0, t1) if dc is None else (dc + t0, nyq + t1)
            ybuf[0:1, :] = dc
            ybuf[s:s + 1, :] = nyq
            y = jnp.dot(finv_ref[...], ybuf[...].astype(BF16), preferred_element_type=F32)
            rows = slice(i * s, (i + 1) * s)
            zn = gate_ref[rows, :].astype(F32) * (y + src_scr[rows, :] * sk)
            if o + 1 < len(gates):
                src_scr[rows, :] = zn
            else:
                o_ref[rows, :] = zn.astype(o_ref.dtype)


def _part_conv(u, f_tab, finv_tab, spec, skip, s, tc):
    b, n, _ = u.shape
    d = D_MODEL
    ncb = d // tc
    nb = n // s
    col = lambda part: (lambda c, bi: (bi, 0, part * ncb + c))
    const = lambda c, bi: (0, 0)
    in_specs = ([pl.BlockSpec((None, n, tc), col(p)) for p in range(3)]
                + [pl.BlockSpec((2 * s, s), const), pl.BlockSpec((s, 2 * s), const),
                   pl.BlockSpec((2, 2 * nb - 1, 2 * s, tc), lambda c, bi: (0, 0, 0, c), pipeline_mode=pl.Buffered(1)),
                   pl.BlockSpec((2, 1, tc), lambda c, bi: (0, 0, c))])
    return pl.pallas_call(
        functools.partial(_part_conv_kernel, s=s, nb=nb, rc=32), grid=(ncb, b),
        in_specs=in_specs,
        out_specs=pl.BlockSpec((None, n, tc), lambda c, bi: (bi, 0, c)),
        out_shape=jax.ShapeDtypeStruct((b, n, d), BF16),
        scratch_shapes=[pltpu.VMEM((n, tc), F32), pltpu.VMEM((n, tc), BF16),
                        pltpu.VMEM((nb, 2 * s, tc), F32), pltpu.VMEM((2 * s, tc), F32)],
        compiler_params=_cparams(2), name="hyena_part_conv",
    )(u, u, u, f_tab, finv_tab, spec, skip.reshape(2, 1, d))


def kernel(x, c, ctx, c_ctx, mod_w, mod_b, ln_g, ln_b, attn_w_in, attn_lambda, attn_subln_g, attn_sink, attn_w_out,
           hy_w_in, hy_conv_w, hy_conv_b, hy_ffn_w_in, hy_ffn_w_hid, hy_ffn_b, hy_sin_freq, hy_ffn_w_out, hy_skip,
           hy_w_out, router_w, router_bias, exp_w_gate, exp_w_up, exp_w_down):
    b, n, d = x.shape
    nc = ctx.shape[1]
    t = b * n
    assert d == D_MODEL and b + 1 <= MOD_ROWS and n % 512 == 0 and nc % 256 == 0

    c_rows = jnp.zeros((MOD_ROWS, d), F32).at[:b].set(c).at[b].set(c_ctx)
    mods = _mods(c_rows, mod_w, mod_b)
    router_wt = jnp.pad(router_w, ((0, 0), (0, LANES - N_EXPERTS)))
    x2d = x.reshape(t, d)

    sh1, sc1, g1, sh2, sc2, g2 = jnp.split(mods[0], 6, axis=-1)
    cos, sin = _rope_tables(n)
    w_in = attn_w_in[0].astype(BF16)
    q_groups = tuple(range(Q_W // LANES))
    rope_groups = q_groups + tuple(range(Q_W // LANES, (Q_W + DIFF_QK_W) // LANES)) + (
        (Q_W + DIFF_QK_W + DIFF_VW) // LANES,)
    q_scales = tuple((g, HEAD_DIM ** -0.5 * LOG2_E) for g in q_groups)
    proj = _modmm(x2d, sc1[:b], sh1[:b], w_in, rows_per_mod=n, tm=1024,
                  rope=(jnp.asarray(cos), jnp.asarray(sin), rope_groups, q_scales), name="attn_in_proj")
    proj_c = _modmm(ctx.reshape(b * nc, d), sc1[b:b + 1], sh1[b:b + 1], w_in[:, Q_W:], rows_per_mod=b * nc, tm=256,
                    name="ctx_in_proj")
    proj = proj.reshape(b, n, ATTN_PROJ_W)
    proj_c = proj_c.reshape(b, nc, KV_W)
    lam_init = 0.8 - 0.6 * math.exp(-0.3 * 0)
    oa = _diff_attn(proj, proj_c, attn_lambda[0], attn_subln_g[0], lam_init, tq=1024, sub=128)
    ow = _win_attn(proj, proj_c, attn_sink[0], tq=256)
    w_out = attn_w_out[0].astype(BF16)
    x1, h2, lgt = _proj_ln([oa.reshape(t, DIFF_VW), ow.reshape(t, WIN_Q_W)], [w_out[:DIFF_VW], w_out[DIFF_VW:]],
                           x2d, g1[:b], ln_g[0, 0], ln_b[0, 0], sc2[:b], sh2[:b], router_wt, rows_per_mod=n, tm=1024,
                           name="attn_out_proj_ln")
    x2 = _moe(h2, lgt, x1, g2[:b], ln_g[0, 1], ln_b[0, 1], router_bias, exp_w_gate, exp_w_up, exp_w_down, layer=0,
              rows_per_mod=n)

    sh1, sc1, g1, sh2, sc2, g2 = jnp.split(mods[1], 6, axis=-1)
    u = _modmm(x2, sc1[:b], sh1[:b], hy_w_in[0].astype(BF16), rows_per_mod=n, tm=1024,
               conv=(hy_conv_w[0], hy_conv_b[0].reshape(1, 3 * d)), name="hyena_in_proj")
    part, tc = HYENA_PART, 256
    phases_np, finv_np = _partition_tables(part)
    phases, finv_tab = jnp.asarray(phases_np).astype(BF16), jnp.asarray(finv_np).astype(BF16)
    zp, decay = _filter_tables(n)
    w_in_p = jnp.zeros((FILTER_HID, FILTER_HID), F32).at[:FILTER_EMB].set(hy_ffn_w_in[0])
    hid = _filter_mlp(jnp.asarray(zp), w_in_p, hy_ffn_w_hid[0], hy_ffn_b[0], hy_sin_freq[0])
    spec = _lag_spec(hid, hy_ffn_w_out[0].reshape(FILTER_HID, 2, 2, d).transpose(1, 2, 0, 3), jnp.asarray(decay),
                     phases, part, tc=512)
    z2 = _part_conv(u.reshape(b, n, 3 * d), phases[0], finv_tab, spec, hy_skip[0], part, tc)
    x3, h2, lgt = _proj_ln([z2.reshape(t, d)], [hy_w_out[0].astype(BF16)], x2, g1[:b], ln_g[1, 0], ln_b[1, 0],
                           sc2[:b], sh2[:b], router_wt, rows_per_mod=n, tm=1024, name="hyena_out_proj_ln")
    x4 = _moe(h2, lgt, x3, g2[:b], ln_g[1, 1], ln_b[1, 1], router_bias, exp_w_gate, exp_w_up, exp_w_down, layer=1,
              rows_per_mod=n)
    return x4.reshape(b, n, d)
```

```python
import functools
import math

import jax
import jax.numpy as jnp
import numpy as np
from jax import lax
from jax.experimental import pallas as pl
from jax.experimental.pallas import tpu as pltpu

F32 = jnp.float32
BF16 = jnp.bfloat16
I32 = jnp.int32

D_MODEL = 1024
DEPTH = 2
GRID_W = 64
HEAD_DIM = 64
DIFF_HEADS = 4
WIN_Q_HEADS = 8
WIN_KV_HEADS = 2
WINDOW = 128
WIN_BLOCK = 128
ROPE_BASE = 10000.0
DIFF_QK_W = DIFF_HEADS * 2 * HEAD_DIM
DIFF_VW = DIFF_HEADS * 2 * HEAD_DIM
WIN_Q_W = WIN_Q_HEADS * HEAD_DIM
WIN_KV_W = WIN_KV_HEADS * HEAD_DIM
Q_W = DIFF_QK_W + WIN_Q_W
KV_W = DIFF_QK_W + DIFF_VW + 2 * WIN_KV_W
ATTN_PROJ_W = Q_W + KV_W
FILTER_EMB = 33
FILTER_HID = 64
DECAY_TARGET = 1e-2
FAST_DECAY_PCT = 0.3
SLOW_DECAY_PCT = 1.5
N_EXPERTS = 16
N_GROUPS = 4
EXPERTS_PER_GROUP = N_EXPERTS // N_GROUPS
EXPERT_FF = 1024
LN_EPS = 1e-5
DEEPNORM_ALPHA = (2 * DEPTH) ** 0.25
NEG_INF = -1e30
LOG2_E = 1.4426950408889634

LANES = 128
MOD_ROWS = 16
VMEM_LIMIT = 60 * 1024 * 1024
MOE_BM = 512
MOE_TILE = 512
HYENA_PART = 512
ROW_ALIGN = 8
NT_DIMS = (((1,), (1,)), ((), ()))


def _cparams(n_axes):
    return pltpu.CompilerParams(dimension_semantics=("arbitrary",) * n_axes, vmem_limit_bytes=VMEM_LIMIT)


@functools.lru_cache(maxsize=None)
def _rope_tables(n):
    rows = n // GRID_W
    r, col = np.meshgrid(np.arange(rows, dtype=np.float32), np.arange(GRID_W, dtype=np.float32), indexing="ij")
    axis_dim = HEAD_DIM // 2
    inv_freq = (ROPE_BASE ** (-np.arange(0, axis_dim, 2, dtype=np.float32) / axis_dim)).astype(np.float32)
    ang = np.concatenate([r.reshape(-1, 1) * inv_freq, col.reshape(-1, 1) * inv_freq], -1)
    ang = np.concatenate([ang, ang], -1).astype(np.float32)
    cos, sin = np.cos(ang), np.sin(ang)
    half = np.arange(HEAD_DIM) < HEAD_DIM // 2
    sin_signed = np.where(half[None, :], -sin, sin)
    reps = LANES // HEAD_DIM
    return (np.tile(cos, (1, reps)).astype(np.float32), np.tile(sin_signed, (1, reps)).astype(np.float32))


@functools.lru_cache(maxsize=None)
def _dft_tables(n, fb):
    big = 2 * n
    k = np.arange(n, dtype=np.int64)[:, None]
    t = np.arange(n, dtype=np.int64)[None, :]
    ang = ((k * t) % big).astype(np.float64) * (2.0 * math.pi / big)
    c, s = np.cos(ang), np.sin(ang)
    alt = (1 - 2 * (np.arange(n) & 1)).astype(np.float64)
    s_f = s.copy()
    s_f[0, :] = alt
    kb = n // fb
    fwd = np.concatenate([c.reshape(kb, fb, n), s_f.reshape(kb, fb, n)], axis=1)
    ci = c.T * (2.0 / big)
    ci[:, 0] = 1.0 / big
    si = s.T * (2.0 / big)
    si[:, 0] = alt / big
    inv = np.concatenate([ci.reshape(n, kb, fb).transpose(1, 0, 2), si.reshape(n, kb, fb).transpose(1, 0, 2)], axis=2)
    return fwd.astype(np.float32), inv.astype(np.float32)


@functools.lru_cache(maxsize=None)
def _filter_tables(n):
    t = np.linspace(0.0, 1.0, n, dtype=np.float32)[:, None]
    bands = (FILTER_EMB - 1) // 2
    w = (2.0 * math.pi * np.arange(n, dtype=np.float32)[:, None] / n).astype(np.float32)
    fr = np.linspace(1e-4, bands - 1, bands, dtype=np.float32)[None, :]
    z = np.concatenate([t, np.cos(fr * w), -np.sin(fr * w)], -1).astype(np.float32)
    zp = np.zeros((n, FILTER_HID), np.float32)
    zp[:, :FILTER_EMB] = z
    deltas = np.abs(np.linspace(math.log(DECAY_TARGET) / SLOW_DECAY_PCT, math.log(DECAY_TARGET) / FAST_DECAY_PCT,
                                D_MODEL, dtype=np.float32))
    decay = np.exp(-t * deltas[None, :]).astype(np.float32)
    return zp, decay


@functools.lru_cache(maxsize=None)
def _partition_tables(s):
    big = 2 * s
    k = np.arange(s, dtype=np.int64)[:, None]
    r = np.arange(s, dtype=np.int64)[None, :]

    def phase(pr, sin_sign, drop_r0):
        ang = ((k * pr) % big).astype(np.float64) * (2.0 * math.pi / big)
        c, sn = np.cos(ang), sin_sign * np.sin(ang)
        sn[0, :] = 1.0 - 2.0 * (pr[0] & 1)
        if drop_r0:
            c[:, 0] = 0.0
            sn[:, 0] = 0.0
        return np.concatenate([c, sn], 0)

    phases = np.stack([phase(r, 1.0, False), phase(s - r, -1.0, True), phase(r, -1.0, False), phase(s - r, 1.0, True)])
    ang = ((r.T * k.T) % big).astype(np.float64) * (2.0 * math.pi / big)
    ci = np.cos(ang) * (2.0 / big)
    ci[:, 0] = 1.0 / big
    si = np.sin(ang) * (2.0 / big)
    si[:, 0] = (1.0 - 2.0 * (np.arange(s) & 1)) / big
    return phases.astype(np.float32), np.concatenate([ci, si], 1).astype(np.float32)


def _mods_kernel(c_ref, w_ref, b_ref, o_ref):
    c = c_ref[...]
    a = (c * jax.nn.sigmoid(c)).astype(BF16)
    o_ref[...] = jnp.dot(a, w_ref[...].astype(BF16), preferred_element_type=F32) + b_ref[...]


def _mods(c_rows, mod_w, mod_b):
    d = D_MODEL
    tn = 1536
    return pl.pallas_call(
        _mods_kernel,
        grid=(DEPTH, 6 * d // tn),
        in_specs=[pl.BlockSpec((MOD_ROWS, d), lambda l, j: (0, 0)),
                  pl.BlockSpec((None, d, tn), lambda l, j: (l, 0, j)),
                  pl.BlockSpec((None, 1, tn), lambda l, j: (l, 0, j))],
        out_specs=pl.BlockSpec((None, MOD_ROWS, tn), lambda l, j: (l, 0, j)),
        out_shape=jax.ShapeDtypeStruct((DEPTH, MOD_ROWS, 6 * d), F32),
        compiler_params=_cparams(2), name="mods",
    )(c_rows, mod_w, mod_b.reshape(DEPTH, 1, 6 * d))


def _modmm_kernel(*refs, n_groups, rope_groups, scaled_groups, chunk):
    if rope_groups:
        x_ref, sc_ref, sh_ref, w_ref, cos_ref, sin_ref, o_ref = refs
        cos, sin = cos_ref[...], sin_ref[...]
        lane = lax.broadcasted_iota(I32, (1, LANES), 1)
        first_half = (lane % HEAD_DIM) < HEAD_DIM // 2
    else:
        x_ref, sc_ref, sh_ref, w_ref, o_ref = refs
    h = (x_ref[...] * (1.0 + sc_ref[...]) + sh_ref[...]).astype(BF16)
    scales = dict(scaled_groups)
    gpc = chunk // LANES
    for c in range(n_groups // gpc):
        acc = jnp.dot(h, w_ref[:, c * chunk:(c + 1) * chunk], preferred_element_type=F32)
        for j in range(gpc):
            g = c * gpc + j
            blk = acc[:, j * LANES:(j + 1) * LANES]
            if g in rope_groups:
                rot = jnp.where(first_half, pltpu.roll(blk, LANES - HEAD_DIM // 2, 1), pltpu.roll(blk, HEAD_DIM // 2, 1))
                blk = blk * cos + rot * sin
                if g in scales:
                    blk = blk * scales[g]
            o_ref[:, g * LANES:(g + 1) * LANES] = blk.astype(o_ref.dtype)


def _modmm(x2d, sc, sh, w, rows_per_mod, tm, rope=None, name="modmm"):
    t, d = x2d.shape
    n = w.shape[1]
    tiles_per_mod = rows_per_mod // tm
    nmod = sc.shape[0]
    in_specs = [pl.BlockSpec((tm, d), lambda i: (i, 0)),
                pl.BlockSpec((None, 1, d), lambda i: (i // tiles_per_mod, 0, 0)),
                pl.BlockSpec((None, 1, d), lambda i: (i // tiles_per_mod, 0, 0)),
                pl.BlockSpec((d, n), lambda i: (0, 0))]
    args = [x2d, sc.reshape(nmod, 1, d), sh.reshape(nmod, 1, d), w]
    rope_groups, scaled_groups = (), ()
    if rope is not None:
        cos, sin, rope_groups, scaled_groups = rope
        in_specs += [pl.BlockSpec((tm, LANES), lambda i: (i % tiles_per_mod, 0)),
                     pl.BlockSpec((tm, LANES), lambda i: (i % tiles_per_mod, 0))]
        args += [cos, sin]
    kern = functools.partial(_modmm_kernel, n_groups=n // LANES, rope_groups=tuple(rope_groups),
                             scaled_groups=tuple(scaled_groups), chunk=256)
    return pl.pallas_call(
        kern, grid=(t // tm,), in_specs=in_specs,
        out_specs=pl.BlockSpec((tm, n), lambda i: (i, 0)),
        out_shape=jax.ShapeDtypeStruct((t, n), BF16),
        compiler_params=_cparams(1), name=name,
    )(*args)


def _diff_attn_kernel(q_ref, k_ref, v_ref, kc_ref, vc_ref, lam_ref, g_ref, o_ref, *, lam_init, sub):
    lane = lax.broadcasted_iota(I32, (1, LANES), 1)
    k, kc = k_ref[...], kc_ref[...]
    v, vc = v_ref[...], vc_ref[...]

    def scores(r0):
        q = q_ref[r0:r0 + sub, :]
        zero = jnp.zeros_like(q)
        out = []
        for qm in (jnp.where(lane < HEAD_DIM, q, zero), jnp.where(lane >= HEAD_DIM, q, zero)):
            out.append((lax.dot_general(qm, k, NT_DIMS, preferred_element_type=F32),
                        lax.dot_general(qm, kc, NT_DIMS, preferred_element_type=F32)))
        return out

    def probs(sl, sc):
        m = jnp.maximum(jnp.max(sl, -1, keepdims=True), jnp.max(sc, -1, keepdims=True))
        pl_, pc = jnp.exp2(sl - m), jnp.exp2(sc - m)
        den = jnp.sum(pl_, -1, keepdims=True) + jnp.sum(pc, -1, keepdims=True)
        return pl_, pc, 1.0 / den

    lv = lam_ref[...]
    lam = (jnp.exp(jnp.sum(lv[0:1] * lv[1:2], keepdims=True)) - jnp.exp(jnp.sum(lv[2:3] * lv[3:4], keepdims=True))
           + lam_init)
    starts = list(range(0, q_ref.shape[0], sub))
    nxt = scores(starts[0])
    for i, r0 in enumerate(starts):
        cur = nxt
        if i + 1 < len(starts):
            nxt = scores(starts[i + 1])
        p1l, p1c, r1 = probs(*cur[0])
        p2l, p2c, r2 = probs(*cur[1])
        w2 = lam * r2
        al = (p1l * r1 - p2l * w2).astype(BF16)
        ac = (p1c * r1 - p2c * w2).astype(BF16)
        o = jnp.dot(al, v, preferred_element_type=F32) + jnp.dot(ac, vc, preferred_element_type=F32)
        ms = jnp.mean(o * o, -1, keepdims=True)
        o_ref[r0:r0 + sub, :] = (o * lax.rsqrt(ms + LN_EPS) * g_ref[...] * (1.0 - lam_init)).astype(o_ref.dtype)


def _diff_attn(proj, proj_c, lam_vec, subln_g, lam_init, tq, sub):
    b, n, _ = proj.shape
    nc = proj_c.shape[1]
    kcol = Q_W // LANES
    vcol = (Q_W + DIFF_QK_W) // LANES
    vccol = DIFF_QK_W // LANES
    kern = functools.partial(_diff_attn_kernel, lam_init=lam_init, sub=sub)
    return pl.pallas_call(
        kern, grid=(b, DIFF_HEADS, n // tq),
        in_specs=[pl.BlockSpec((None, tq, LANES), lambda bi, h, i: (bi, i, h)),
                  pl.BlockSpec((None, n, LANES), lambda bi, h, i: (bi, 0, kcol + h)),
                  pl.BlockSpec((None, n, LANES), lambda bi, h, i: (bi, 0, vcol + h)),
                  pl.BlockSpec((None, nc, LANES), lambda bi, h, i: (bi, 0, h)),
                  pl.BlockSpec((None, nc, LANES), lambda bi, h, i: (bi, 0, vccol + h)),
                  pl.BlockSpec((4, HEAD_DIM), lambda bi, h, i: (0, 0)),
                  pl.BlockSpec((1, LANES), lambda bi, h, i: (0, 0))],
        out_specs=pl.BlockSpec((None, tq, LANES), lambda bi, h, i: (bi, i, h)),
        out_shape=jax.ShapeDtypeStruct((b, n, DIFF_VW), BF16),
        compiler_params=_cparams(3), name="diff_attn",
    )(proj, proj, proj, proj_c, proj_c, lam_vec, subln_g.reshape(1, LANES))


def _win_attn_kernel(sink_ref, q_ref, k_ref, v_ref, kc_ref, vc_ref, o_ref, *, seq, tq):
    for r0 in range(0, q_ref.shape[0], tq):
        _win_attn_rows(sink_ref, q_ref, k_ref, v_ref, kc_ref, vc_ref, o_ref, r0, seq=seq, tq=tq)


def _win_attn_rows(sink_ref, q_ref, k_ref, v_ref, kc_ref, vc_ref, o_ref, r0, *, seq, tq):
    first = pl.program_id(1) * q_ref.shape[0] + r0
    gq = WIN_Q_HEADS // WIN_KV_HEADS
    kw = tq + 2 * WINDOW
    start = pl.multiple_of(jnp.clip(first - WINDOW, 0, seq - kw), WINDOW)
    k_win, v_win = k_ref[pl.ds(start, kw), :], v_ref[pl.ds(start, kw), :]
    kc, vc = kc_ref[...], vc_ref[...]
    lane = lax.broadcasted_iota(I32, (1, LANES), 1)
    row = lax.broadcasted_iota(I32, (gq * tq, 1), 0)
    q_abs = first + row % tq
    k_abs = start + lax.broadcasted_iota(I32, (1, kw), 1)
    allowed = jnp.abs(q_abs - k_abs) <= WINDOW
    head = row // tq
    q = q_ref[r0:r0 + tq, :].astype(F32)
    for g in range(WIN_KV_HEADS):
        in_g = (lane // HEAD_DIM) == g
        parts = []
        for j in range(gq):
            hq = g * gq + j
            x = q[:, (hq // 2) * LANES:(hq // 2 + 1) * LANES]
            if hq % 2 != g:
                x = pltpu.roll(x, HEAD_DIM, 1)
            parts.append(jnp.where(in_g, x, 0.0))
        qs = jnp.concatenate(parts, 0).astype(BF16)
        s_loc = lax.dot_general(qs, k_win, NT_DIMS, preferred_element_type=F32)
        s_ctx = lax.dot_general(qs, kc, NT_DIMS, preferred_element_type=F32)
        s_loc = jnp.where(allowed, s_loc, NEG_INF)
        sk = jnp.zeros((gq * tq, 1), F32)
        for j in range(gq):
            sk = jnp.where(head == j, sink_ref[g * gq + j] * LOG2_E, sk)
        m = jnp.maximum(jnp.maximum(jnp.max(s_loc, -1, keepdims=True), jnp.max(s_ctx, -1, keepdims=True)), sk)
        p_loc, p_ctx = jnp.exp2(s_loc - m), jnp.exp2(s_ctx - m)
        den = jnp.sum(p_loc, -1, keepdims=True) + jnp.sum(p_ctx, -1, keepdims=True) + jnp.exp2(sk - m)
        o = (jnp.dot(p_loc.astype(BF16), v_win, preferred_element_type=F32)
             + jnp.dot(p_ctx.astype(BF16), vc, preferred_element_type=F32)) * (1.0 / den)
        for cb in range(gq // 2):
            pair = []
            for half in range(2):
                piece = o[(2 * cb + half) * tq:(2 * cb + half + 1) * tq]
                pair.append(piece if half == g else pltpu.roll(piece, HEAD_DIM, 1))
            col = (g * gq // 2 + cb) * LANES
            o_ref[r0:r0 + tq, col:col + LANES] = jnp.where(lane < HEAD_DIM, pair[0], pair[1]).astype(o_ref.dtype)


def _win_attn(proj, proj_c, sink, step, tq):
    b, n, _ = proj.shape
    nc = proj_c.shape[1]
    qcol = DIFF_QK_W // WIN_Q_W
    kcol = (Q_W + DIFF_QK_W + DIFF_VW) // LANES
    kccol = (DIFF_QK_W + DIFF_VW) // LANES
    kern = functools.partial(_win_attn_kernel, seq=n, tq=tq)
    return pl.pallas_call(
        kern, grid=(b, n // step),
        in_specs=[pl.BlockSpec(memory_space=pltpu.SMEM),
                  pl.BlockSpec((None, step, WIN_Q_W), lambda bi, i: (bi, i, qcol)),
                  pl.BlockSpec((None, n, LANES), lambda bi, i: (bi, 0, kcol)),
                  pl.BlockSpec((None, n, LANES), lambda bi, i: (bi, 0, kcol + 1)),
                  pl.BlockSpec((None, nc, LANES), lambda bi, i: (bi, 0, kccol)),
                  pl.BlockSpec((None, nc, LANES), lambda bi, i: (bi, 0, kccol + 1))],
        out_specs=pl.BlockSpec((None, step, WIN_Q_W), lambda bi, i: (bi, i, 0)),
        out_shape=jax.ShapeDtypeStruct((b, n, WIN_Q_W), BF16),
        compiler_params=_cparams(2), name="win_attn",
    )(sink, proj, proj, proj, proj_c, proj_c)


def _layer_norm(r, g, b):
    mu = jnp.mean(r, -1, keepdims=True)
    dlt = r - mu
    var = jnp.mean(dlt * dlt, -1, keepdims=True)
    return dlt * lax.rsqrt(var + LN_EPS) * g + b


def _proj_ln_kernel(*refs, n_in, sub):
    a_refs = refs[:n_in]
    w_refs = refs[n_in:2 * n_in]
    x_ref, gate_ref, lng_ref, lnb_ref, sc_ref, sh_ref, rw_ref, x1_ref, h2_ref, lg_ref = refs[2 * n_in:]

    def split(v):
        hi = v.astype(BF16)
        return hi, (v - hi.astype(F32)).astype(BF16)

    def project(r0):
        y = jnp.dot(a_refs[0][r0:r0 + sub, :], w_refs[0][...], preferred_element_type=F32)
        for a_ref, w_ref in zip(a_refs[1:], w_refs[1:]):
            y = y + jnp.dot(a_ref[r0:r0 + sub, :], w_ref[...], preferred_element_type=F32)
        return y

    rw_hi, rw_lo = split(rw_ref[...])
    rw_both = jnp.concatenate([rw_hi, rw_lo], 1)
    starts = list(range(0, x_ref.shape[0], sub))
    nxt = project(starts[0])
    for i, r0 in enumerate(starts):
        y = nxt
        if i + 1 < len(starts):
            nxt = project(starts[i + 1])
        xn = _layer_norm(DEEPNORM_ALPHA * x_ref[r0:r0 + sub, :] + gate_ref[...] * y, lng_ref[...], lnb_ref[...])
        x1_ref[r0:r0 + sub, :] = xn
        h2 = xn * (1.0 + sc_ref[...]) + sh_ref[...]
        h2_ref[r0:r0 + sub, :] = h2.astype(h2_ref.dtype)
        h_hi, h_lo = split(h2)
        hh = jnp.dot(h_hi, rw_both, preferred_element_type=F32)
        lg = hh[:, :LANES] + hh[:, LANES:] + jnp.dot(h_lo, rw_hi, preferred_element_type=F32)
        lg_ref[:, r0:r0 + sub] = lg.T[:N_EXPERTS, :]


def _proj_ln(acts, ws, x2d, gate, ln_g, ln_b, sc, sh, router_wt, rows_per_mod, tm, name):
    t, d = x2d.shape
    n_in = len(acts)
    tiles_per_mod = rows_per_mod // tm
    nmod = gate.shape[0]
    row = lambda i: (i, 0)
    full = lambda i: (0, 0)
    mod = lambda i: (i // tiles_per_mod, 0, 0)
    in_specs = ([pl.BlockSpec((tm, a.shape[1]), row) for a in acts]
                + [pl.BlockSpec(w.shape, full) for w in ws]
                + [pl.BlockSpec((tm, d), row), pl.BlockSpec((None, 1, d), mod),
                   pl.BlockSpec((1, d), full), pl.BlockSpec((1, d), full),
                   pl.BlockSpec((None, 1, d), mod), pl.BlockSpec((None, 1, d), mod),
                   pl.BlockSpec((d, LANES), full)])
    return pl.pallas_call(
        functools.partial(_proj_ln_kernel, n_in=n_in, sub=LANES), grid=(t // tm,), in_specs=in_specs,
        out_specs=[pl.BlockSpec((tm, d), row), pl.BlockSpec((tm, d), row),
                   pl.BlockSpec((N_EXPERTS, tm), lambda i: (0, i))],
        out_shape=[jax.ShapeDtypeStruct((t, d), F32), jax.ShapeDtypeStruct((t, d), BF16),
                   jax.ShapeDtypeStruct((N_EXPERTS, t), F32)],
        compiler_params=_cparams(1), name=name,
    )(*acts, *ws, x2d, gate.reshape(nmod, 1, d), ln_g.reshape(1, d), ln_b.reshape(1, d),
      sc.reshape(nmod, 1, d), sh.reshape(nmod, 1, d), router_wt)


def _first_argmax(vals):
    idx = jnp.zeros(vals[0].shape, I32)
    best = vals[0]
    for j in range(1, len(vals)):
        upd = vals[j] > best
        idx = jnp.where(upd, j, idx)
        best = jnp.where(upd, vals[j], best)
    return idx, best


def _route_kernel(lg_ref, bias_ref, pos_ref, w_ref, tab_ref, cnt_ref, carry_ref, off_ref, *, tr):
    @pl.when(pl.program_id(0) == 0)
    def _():
        carry_ref[...] = jnp.zeros_like(carry_ref)

    lg = lg_ref[...]
    ex = jnp.exp(lg - jnp.max(lg, 0, keepdims=True))
    scores = ex / jnp.sum(ex, 0, keepdims=True)
    sel = scores + bias_ref[...]
    rows = [sel[e:e + 1] for e in range(N_EXPERTS)]
    group_scores = []
    for g in range(N_GROUPS):
        r = rows[g * EXPERTS_PER_GROUP:(g + 1) * EXPERTS_PER_GROUP]
        best = None
        for i in range(EXPERTS_PER_GROUP):
            for j in range(i + 1, EXPERTS_PER_GROUP):
                s = r[i] + r[j]
                best = s if best is None else jnp.maximum(best, s)
        group_scores.append(best)
    grp, _ = _first_argmax(group_scores)
    vals = []
    for j in range(EXPERTS_PER_GROUP):
        v = rows[(N_GROUPS - 1) * EXPERTS_PER_GROUP + j]
        for g in range(N_GROUPS - 2, -1, -1):
            v = jnp.where(grp == g, rows[g * EXPERTS_PER_GROUP + j], v)
        vals.append(v)
    i0, _ = _first_argmax(vals)
    i1, _ = _first_argmax([jnp.where(i0 == j, -jnp.inf, vals[j]) for j in range(EXPERTS_PER_GROUP)])
    e0 = grp * EXPERTS_PER_GROUP + i0
    e1 = grp * EXPERTS_PER_GROUP + i1
    eid = lax.broadcasted_iota(I32, (N_EXPERTS, 1), 0)
    oh0, oh1 = eid == e0, eid == e1
    s0 = jnp.sum(jnp.where(oh0, scores, 0.0), 0, keepdims=True)
    s1 = jnp.sum(jnp.where(oh1, scores, 0.0), 0, keepdims=True)
    den = s0 + s1
    member = jnp.where(oh0 | oh1, 1.0, 0.0)
    before = lax.broadcasted_iota(I32, (tr, tr), 0) < lax.broadcasted_iota(I32, (tr, tr), 1)
    upper = jnp.where(before, 1.0, 0.0).astype(BF16)
    cnt = jnp.dot(member.astype(BF16), upper, preferred_element_type=F32)
    run = jnp.sum(member, 1, keepdims=True)
    run = jnp.floor((run + (ROW_ALIGN - 1)) * (1.0 / ROW_ALIGN)) * ROW_ALIGN
    run = jnp.broadcast_to(run, (N_EXPERTS, LANES))
    acc = jnp.zeros((1, LANES), F32)
    for e in range(N_EXPERTS):
        off_ref[e:e + 1, :] = acc
        acc = acc + run[e:e + 1]
    off = off_ref[...]
    at = off[:, 0:1] + cnt
    pos_ref[0:1, :] = jnp.sum(jnp.where(oh0, at, 0.0), 0, keepdims=True).astype(I32)
    pos_ref[1:2, :] = jnp.sum(jnp.where(oh1, at, 0.0), 0, keepdims=True).astype(I32)
    w_ref[0:1, :] = s0 / den
    w_ref[1:2, :] = s1 / den
    tab_ref[0] = off.astype(I32)
    tab_ref[1] = run.astype(I32)
    tab_ref[2] = carry_ref[...].astype(I32)
    carry_ref[...] = carry_ref[...] + run
    cnt_ref[...] = carry_ref[...]


def _route(logits_t, router_bias, tr):
    t = logits_t.shape[1]
    tok = lambda i: (0, i)
    return pl.pallas_call(
        functools.partial(_route_kernel, tr=tr), grid=(t // tr,),
        in_specs=[pl.BlockSpec((N_EXPERTS, tr), tok), pl.BlockSpec((N_EXPERTS, 1), lambda i: (0, 0))],
        out_specs=[pl.BlockSpec((2, tr), tok), pl.BlockSpec((2, tr), tok),
                   pl.BlockSpec((None, 3, N_EXPERTS, LANES), lambda i: (i, 0, 0, 0)),
                   pl.BlockSpec((N_EXPERTS, LANES), lambda i: (0, 0))],
        out_shape=[jax.ShapeDtypeStruct((2, t), I32), jax.ShapeDtypeStruct((2, t), F32),
                   jax.ShapeDtypeStruct((t // tr, 3, N_EXPERTS, LANES), I32),
                   jax.ShapeDtypeStruct((N_EXPERTS, LANES), F32)],
        scratch_shapes=[pltpu.VMEM((N_EXPERTS, LANES), F32), pltpu.VMEM((N_EXPERTS, LANES), F32)],
        compiler_params=_cparams(1), name="route",
    )(logits_t, router_bias.reshape(N_EXPERTS, 1))


def _run_copies(n, max_rows, make_copy, wait):
    sz = max_rows
    while sz >= ROW_ALIGN:
        start = (n // (2 * sz)) * (2 * sz)

        @pl.when((n & sz) != 0)
        def _(start=start, sz=sz):
            cp = make_copy(start, sz)
            cp.wait() if wait else cp.start()

        sz //= 2


def _wait_rows(total, max_rows, make_copy):
    sz = max_rows
    while sz >= ROW_ALIGN:
        @pl.when((total & sz) != 0)
        def _(sz=sz):
            make_copy(sz).wait()

        sz //= 2


def _rows(ref, start, size):
    return ref.at[pl.ds(pl.multiple_of(start, ROW_ALIGN), size)]


def _dispatch_kernel(tab_ref, h_ref, pos_ref, w_ref, xs_ref, buf, zbuf, sem, *, tt, n_tiles, bm):
    i = pl.program_id(0)
    d = h_ref.shape[1]
    sb = buf.shape[1]
    ne = N_EXPERTS
    fill = 3 * n_tiles * ne

    @pl.when(i == 0)
    def _():
        zbuf[...] = jnp.zeros_like(zbuf)
        for wait in (False, True):
            for e in range(ne):
                dst, n = tab_ref[fill + e], tab_ref[fill + ne + e]
                _run_copies(n, bm // 2, lambda s, z: pltpu.make_async_copy(
                    zbuf.at[pl.ds(0, z)], _rows(xs_ref, dst + s, z), sem.at[2]), wait)

        def zero_block(j, carry):
            for half in range(2):
                cp = pltpu.make_async_copy(zbuf, _rows(xs_ref, j * bm + half * (bm // 2), bm // 2), sem.at[2])
                cp.start()
                cp.wait()
            return carry

        lax.fori_loop(tab_ref[fill + 2 * ne], xs_ref.shape[0] // bm, zero_block, 0)

    def tile_copies(tile, slot, wait):
        for e in range(ne):
            off = tab_ref[tile * ne + e]
            n = tab_ref[(n_tiles + tile) * ne + e]
            dst = tab_ref[(2 * n_tiles + tile) * ne + e]
            _run_copies(n, tt, lambda s, z: pltpu.make_async_copy(
                _rows(buf.at[slot], off + s, z), _rows(xs_ref, dst + s, z), sem.at[slot]), wait)

    slot = i % 2
    pos, w = pos_ref[...], w_ref[...]
    hb = h_ref[...].astype(BF16)
    for r0 in range(0, sb, LANES):
        srow = r0 + lax.broadcasted_iota(I32, (LANES, 1), 0)
        m0, m1 = srow == pos[0:1], srow == pos[1:2]
        perm = (jnp.where(m0, 1.0, 0.0) + jnp.where(m1, 1.0, 0.0)).astype(BF16)
        buf[slot, r0:r0 + LANES, :d] = jnp.dot(perm, hb, preferred_element_type=F32)
        wrow = jnp.sum(jnp.where(m0, w[0:1], 0.0) + jnp.where(m1, w[1:2], 0.0), 1, keepdims=True)
        buf[slot, r0:r0 + LANES, d:] = jnp.broadcast_to(wrow, (LANES, LANES))

    def wait_tile(tile, slot):
        total = tab_ref[fill + 2 * ne + 1 + tile]
        _wait_rows(total, tt * 2, lambda z: pltpu.make_async_copy(
            buf.at[slot, pl.ds(0, z)], xs_ref.at[pl.ds(0, z)], sem.at[slot]))

    @pl.when(i > 0)
    def _():
        wait_tile(i - 1, 1 - slot)

    tile_copies(i, slot, False)

    @pl.when(i == n_tiles - 1)
    def _():
        wait_tile(i, slot)


def _dispatch(tab, h2, pos, wts, n_slots, tt, sb, bm):
    t, d = h2.shape
    n_tiles = t // tt
    return pl.pallas_call(
        functools.partial(_dispatch_kernel, tt=tt, n_tiles=n_tiles, bm=bm),
        grid_spec=pltpu.PrefetchScalarGridSpec(
            num_scalar_prefetch=1, grid=(n_tiles,),
            in_specs=[pl.BlockSpec((tt, d), lambda i, s: (i, 0)), pl.BlockSpec((2, tt), lambda i, s: (0, i)),
                      pl.BlockSpec((2, tt), lambda i, s: (0, i))],
            out_specs=pl.BlockSpec(memory_space=pl.ANY),
            scratch_shapes=[pltpu.VMEM((2, sb, d + LANES), F32), pltpu.VMEM((bm // 2, d + LANES), F32),
                            pltpu.SemaphoreType.DMA((3,))]),
        out_shape=jax.ShapeDtypeStruct((n_slots, d + LANES), F32),
        compiler_params=_cparams(1), name="moe_dispatch",
    )(tab, h2, pos, wts)


def _ffn_kernel(blk_e_ref, blk_rows_ref, next_e_ref, nused_ref, x_ref, wg_hbm, wu_hbm, wd_hbm, o_ref,
                stage, wg_bf, wu_bf, wd_bf, sem, *, layer, parts):
    del nused_ref
    i = pl.program_id(0)
    e = blk_e_ref[i]
    new_expert = (i == 0) | (e != blk_e_ref[jnp.maximum(i - 1, 0)])

    def fetch(expert, wait):
        for j, src in enumerate((wg_hbm, wu_hbm, wd_hbm)):
            cp = pltpu.make_async_copy(src.at[layer, expert], stage.at[j], sem.at[j])
            cp.wait() if wait else cp.start()

    @pl.when(i == 0)
    def _():
        fetch(e, False)

    @pl.when(new_expert)
    def _():
        fetch(e, True)
        wg_bf[...] = stage[0].astype(BF16)
        wu_bf[...] = stage[1].astype(BF16)
        wd_bf[...] = stage[2].astype(BF16)

        @pl.when(next_e_ref[i] >= 0)
        def _():
            fetch(next_e_ref[i], False)

    d = wg_bf.shape[0]
    rows = blk_rows_ref[i]
    part = x_ref.shape[0] // parts
    spans = [slice(p * part, (p + 1) * part) for p in range(parts)]

    def gate_up(rs):
        x = x_ref[rs, :d].astype(BF16)
        return (jnp.dot(x, wg_bf[...], preferred_element_type=F32), jnp.dot(x, wu_bf[...], preferred_element_type=F32))

    def finish(rs, gate, up):
        act = (gate * jax.nn.sigmoid(gate) * up).astype(BF16)
        o_ref[rs, :] = jnp.dot(act, wd_bf[...], preferred_element_type=F32) * x_ref[rs, d:d + 1]

    all_parts = rows > (parts - 1) * part

    @pl.when(all_parts)
    def _():
        nxt = gate_up(spans[0])
        for p, rs in enumerate(spans):
            cur = nxt
            if p + 1 < parts:
                nxt = gate_up(spans[p + 1])
            finish(rs, *cur)

    for p, rs in enumerate(spans):
        @pl.when(jnp.logical_not(all_parts) & (rows > p * part))
        def _(rs=rs):
            finish(rs, *gate_up(rs))

        @pl.when(rows <= p * part)
        def _(rs=rs):
            o_ref[rs, :] = jnp.zeros((part, d), F32)


def _ffn(blk_e, blk_rows, next_e, n_used, xs, wg, wu, wd, layer, bm, parts):
    n_slots, xw = xs.shape
    d, ff = wg.shape[2:]
    assert d == ff
    xrow = lambda i, be, br, ne, nu: (jnp.minimum(i, nu[0] - 1), 0)
    hbm = pl.BlockSpec(memory_space=pl.ANY)
    return pl.pallas_call(
        functools.partial(_ffn_kernel, layer=layer, parts=parts),
        grid_spec=pltpu.PrefetchScalarGridSpec(
            num_scalar_prefetch=4, grid=(n_slots // bm,),
            in_specs=[pl.BlockSpec((bm, xw), xrow), hbm, hbm, hbm],
            out_specs=pl.BlockSpec((bm, d), lambda i, be, br, ne, nu: (i, 0)),
            scratch_shapes=[pltpu.VMEM((3, d, ff), F32), pltpu.VMEM((d, ff), BF16), pltpu.VMEM((d, ff), BF16),
                            pltpu.VMEM((ff, d), BF16), pltpu.SemaphoreType.DMA((3,))]),
        out_shape=jax.ShapeDtypeStruct((n_slots, d), F32),
        compiler_params=_cparams(1), name="moe_ffn",
    )(blk_e, blk_rows, next_e, n_used, xs, wg, wu, wd)


def _combine_kernel(tab_ref, ys_ref, pos_ref, x1_ref, gate_ref, lng_ref, lnb_ref, o_ref, ybuf, sem, *, tt, n_tiles):
    i = pl.program_id(0)
    sb = ybuf.shape[1]
    ne = N_EXPERTS
    slot = i % 2

    def tile_copies(tile, slot, wait):
        for e in range(ne):
            off = tab_ref[tile * ne + e]
            n = tab_ref[(n_tiles + tile) * ne + e]
            src = tab_ref[(2 * n_tiles + tile) * ne + e]
            _run_copies(n, tt, lambda s, z: pltpu.make_async_copy(
                _rows(ys_ref, src + s, z), _rows(ybuf.at[slot], off + s, z), sem.at[slot]), wait)

    @pl.when(i == 0)
    def _():
        ybuf[...] = jnp.zeros_like(ybuf)
        tile_copies(0, 0, False)

    @pl.when(i + 1 < n_tiles)
    def _():
        tile_copies(i + 1, 1 - slot, False)

    total = tab_ref[3 * n_tiles * ne + 2 * ne + 1 + i]
    _wait_rows(total, tt * 2, lambda z: pltpu.make_async_copy(
        ys_ref.at[pl.ds(0, z)], ybuf.at[slot, pl.ds(0, z)], sem.at[slot]))
    scol = lax.broadcasted_iota(I32, (1, sb), 1)
    yb = ybuf[slot].astype(BF16)

    def moe_rows(r0):
        pos = pos_ref[r0:r0 + LANES, :]
        unperm = (jnp.where(scol == pos[:, 0:1], 1.0, 0.0) + jnp.where(scol == pos[:, 1:2], 1.0, 0.0)).astype(BF16)
        return jnp.dot(unperm, yb, preferred_element_type=F32)

    starts = list(range(0, tt, LANES))
    nxt = moe_rows(starts[0])
    for j, r0 in enumerate(starts):
        f = nxt
        if j + 1 < len(starts):
            nxt = moe_rows(starts[j + 1])
        o_ref[r0:r0 + LANES, :] = _layer_norm(DEEPNORM_ALPHA * x1_ref[r0:r0 + LANES, :] + gate_ref[...] * f,
                                              lng_ref[...], lnb_ref[...])


def _combine(tab, ys, pos_t, x1, gate, ln_g, ln_b, rows_per_mod, tt, sb):
    t, d = x1.shape
    tiles_per_mod = rows_per_mod // tt
    nmod = gate.shape[0]
    return pl.pallas_call(
        functools.partial(_combine_kernel, tt=tt, n_tiles=t // tt),
        grid_spec=pltpu.PrefetchScalarGridSpec(
            num_scalar_prefetch=1, grid=(t // tt,),
            in_specs=[pl.BlockSpec(memory_space=pl.ANY),
                      pl.BlockSpec((tt, 2), lambda i, s: (i, 0)),
                      pl.BlockSpec((tt, d), lambda i, s: (i, 0)),
                      pl.BlockSpec((None, 1, d), lambda i, s: (i // tiles_per_mod, 0, 0)),
                      pl.BlockSpec((1, d), lambda i, s: (0, 0)), pl.BlockSpec((1, d), lambda i, s: (0, 0))],
            out_specs=pl.BlockSpec((tt, d), lambda i, s: (i, 0)),
            scratch_shapes=[pltpu.VMEM((2, sb, d), F32), pltpu.SemaphoreType.DMA((2,))]),
        out_shape=jax.ShapeDtypeStruct((t, d), F32),
        compiler_params=_cparams(1), name="moe_combine",
    )(tab, ys, pos_t, x1, gate.reshape(nmod, 1, d), ln_g.reshape(1, d), ln_b.reshape(1, d))


def _moe(h2, logits_t, x1, gate, ln_g, ln_b, router_bias, wg, wu, wd, layer, rows_per_mod):
    t, d = h2.shape
    bm, tt, ne = MOE_BM, MOE_TILE, N_EXPERTS
    n_tiles = t // tt
    pad = ne * (ROW_ALIGN - 1)
    sb = -(-(2 * tt + pad) // LANES) * LANES
    pos, wts, tab3, cnt = _route(logits_t, router_bias, tr=tt)
    rows = cnt[:, 0].astype(I32)
    prows = (rows + bm - 1) // bm * bm
    pends = jnp.cumsum(prows)
    pstarts = pends - prows
    n_blk = -(-(2 * t + n_tiles * pad + ne * (bm - 1)) // bm)
    n_used = pends[-1] // bm
    blk_ids = jnp.minimum(jnp.arange(n_blk, dtype=I32), n_used - 1)
    blk_e = jnp.minimum(jnp.sum((blk_ids[:, None] * bm >= pends[None, :]).astype(I32), 1), ne - 1)
    eids = jnp.arange(ne, dtype=I32)
    of_blk = lambda per_expert: jnp.sum(jnp.where(blk_e[:, None] == eids[None, :], per_expert[None, :], 0), 1)
    raw_ids = jnp.arange(n_blk, dtype=I32)
    blk_rows = jnp.where(raw_ids < n_used, jnp.clip(of_blk(pstarts + rows) - raw_ids * bm, 0, bm), 0)
    later_used = (eids[None, :] > eids[:, None]) & (prows[None, :] > 0)
    next_used = jnp.min(jnp.where(later_used, eids[None, :], ne), 1)
    next_e = of_blk(jnp.where(next_used < ne, next_used, -1))
    tab3 = tab3[:, :, :, 0]
    tab = jnp.concatenate([tab3[:, 0].reshape(-1), tab3[:, 1].reshape(-1),
                           (tab3[:, 2] + pstarts[None, :]).reshape(-1), pstarts + rows, prows - rows,
                           n_used.reshape(1), jnp.sum(tab3[:, 1], 1)]).astype(I32)
    xs = _dispatch(tab, h2, pos, wts, n_blk * bm, tt, sb, bm)
    ys = _ffn(blk_e, blk_rows.astype(I32), next_e.astype(I32), n_used.reshape(1).astype(I32), xs, wg, wu, wd,
              layer, bm, parts=2)
    return _combine(tab, ys, pos.T, x1, gate, ln_g, ln_b, rows_per_mod, tt, sb)


def _filter_mlp_kernel(z_ref, w1_ref, w2_ref, w3_ref, b_ref, sf_ref, o_ref):
    hp = lax.Precision.HIGHEST
    b, sf = b_ref[...], sf_ref[...]
    h = jnp.sin(sf[0:1] * (jnp.dot(z_ref[...], w1_ref[...], precision=hp, preferred_element_type=F32) + b[0:1]))
    h = jnp.sin(sf[1:2] * (jnp.dot(h, w2_ref[...], precision=hp, preferred_element_type=F32) + b[1:2]))
    o_ref[...] = jnp.sin(sf[2:3] * (jnp.dot(h, w3_ref[...], precision=hp, preferred_element_type=F32) + b[2:3]))


def _filter_mlp(zp, w_in_p, w_hid, b, sf):
    n = zp.shape[0]
    return pl.pallas_call(
        _filter_mlp_kernel, out_shape=jax.ShapeDtypeStruct((n, FILTER_HID), F32),
        compiler_params=pltpu.CompilerParams(vmem_limit_bytes=VMEM_LIMIT), name="hyena_filter_mlp",
    )(zp, w_in_p, w_hid[0], w_hid[1], b, sf)


def _filter_spec_kernel(hid_ref, wf_ref, wb_ref, dec_ref, fwd_ref, hr_ref, g_ref, nyq_ref, p_scr, q_scr, *, fb, kps):
    k = pl.program_id(2)
    n = hid_ref.shape[0]

    @pl.when(k == 0)
    def _():
        hp = lax.Precision.HIGHEST
        hid, dec = hid_ref[...], dec_ref[...]
        row = lax.broadcasted_iota(I32, (n, 1), 0)
        fw = jnp.dot(hid, wf_ref[...], precision=hp, preferred_element_type=F32) * dec
        bw = jnp.dot(hid, wb_ref[...], precision=hp, preferred_element_type=F32) * dec
        bw = jnp.where(row == 0, 0.0, bw)
        p = fw + bw
        p_scr[...] = p.astype(BF16)
        q_scr[...] = (fw - bw).astype(BF16)
        sign = (1 - 2 * (row & 1)).astype(F32)
        nyq_ref[...] = jnp.sum(p * sign, 0, keepdims=True)

    rowb = lax.broadcasted_iota(I32, (fb, 1), 0)
    for j in range(kps):
        hr_ref[j * fb:(j + 1) * fb, :] = jnp.dot(fwd_ref[j, 0:fb, :], p_scr[...], preferred_element_type=F32)
        gg = jnp.dot(fwd_ref[j, fb:2 * fb, :], q_scr[...], preferred_element_type=F32)
        if j == 0:
            gg = jnp.where((rowb == 0) & (k == 0), 0.0, gg)
        g_ref[j * fb:(j + 1) * fb, :] = gg


def _filter_spec(hid, w_out4, decay, fwd_tab, fb, tc, kps):
    n = hid.shape[0]
    d = D_MODEL
    kb = n // (fb * kps)
    return pl.pallas_call(
        functools.partial(_filter_spec_kernel, fb=fb, kps=kps), grid=(2, d // tc, kb),
        in_specs=[pl.BlockSpec((n, FILTER_HID), lambda o, c, k: (0, 0)),
                  pl.BlockSpec((None, None, FILTER_HID, tc), lambda o, c, k: (0, o, 0, c)),
                  pl.BlockSpec((None, None, FILTER_HID, tc), lambda o, c, k: (1, o, 0, c)),
                  pl.BlockSpec((n, tc), lambda o, c, k: (0, c)),
                  pl.BlockSpec((kps, 2 * fb, n), lambda o, c, k: (k, 0, 0))],
        out_specs=[pl.BlockSpec((None, kps * fb, tc), lambda o, c, k: (o, k, c)),
                   pl.BlockSpec((None, kps * fb, tc), lambda o, c, k: (o, k, c)),
                   pl.BlockSpec((None, 1, tc), lambda o, c, k: (o, 0, c))],
        out_shape=[jax.ShapeDtypeStruct((2, n, d), F32), jax.ShapeDtypeStruct((2, n, d), F32),
                   jax.ShapeDtypeStruct((2, 1, d), F32)],
        scratch_shapes=[pltpu.VMEM((n, tc), BF16), pltpu.VMEM((n, tc), BF16)],
        compiler_params=_cparams(3), name="hyena_filter_spec",
    )(hid, w_out4, w_out4, decay, fwd_tab)


def _short_conv(u_ref, cw_ref, cb_ref):
    n = u_ref.shape[0]
    u = u_ref[...].astype(F32)
    row = lax.broadcasted_iota(I32, (n, 1), 0)
    prev = jnp.where(row == 0, 0.0, pltpu.roll(u, 1, 0))
    nxt = jnp.where(row == n - 1, 0.0, pltpu.roll(u, n - 1, 0))
    cw = cw_ref[...]
    return prev * cw[0:1] + u * cw[1:2] + nxt * cw[2:3] + cb_ref[...]


def _hyena_conv_kernel(uv_ref, ux1_ref, ux2_ref, cwv_ref, cwx1_ref, cwx2_ref, cbv_ref, cbx1_ref, cbx2_ref,
                       fwd_ref, inv_ref, hr_ref, g_ref, nyq_ref, sk_ref, o_ref, zb_scr, z32_scr, acc_scr, *, fb, kb,
                       kps):
    o = pl.program_id(2)
    k = pl.program_id(3)

    @pl.when((o == 0) & (k == 0))
    def _():
        v = _short_conv(uv_ref, cwv_ref, cbv_ref)
        z32_scr[...] = v
        zb_scr[...] = v.astype(BF16)

    @pl.when(k == 0)
    def _():
        acc_scr[...] = jnp.zeros_like(acc_scr)

    zb = zb_scr[...]
    zfs = [jnp.dot(fwd_ref[j], zb, preferred_element_type=F32) for j in range(kps)]
    rowb = lax.broadcasted_iota(I32, (fb, 1), 0)
    for j in range(kps):
        zr, zi = zfs[j][:fb], zfs[j][fb:]
        hr, gg = hr_ref[j * fb:(j + 1) * fb, :], g_ref[j * fb:(j + 1) * fb, :]
        hb = jnp.where((rowb == 0) & (k == 0), nyq_ref[...], hr) if j == 0 else hr
        y = jnp.concatenate([zr * hr - zi * gg, zr * gg + zi * hb], 0).astype(BF16)
        acc_scr[...] += jnp.dot(inv_ref[j], y, preferred_element_type=F32)

    @pl.when((k == kb - 1) & (o == 0))
    def _():
        zn = _short_conv(ux1_ref, cwx1_ref, cbx1_ref) * (acc_scr[...] + z32_scr[...] * sk_ref[...])
        z32_scr[...] = zn
        zb_scr[...] = zn.astype(BF16)

    @pl.when((k == kb - 1) & (o == 1))
    def _():
        zn = _short_conv(ux2_ref, cwx2_ref, cbx2_ref) * (acc_scr[...] + z32_scr[...] * sk_ref[...])
        o_ref[...] = zn.astype(o_ref.dtype)


def _hyena_conv(u, conv_w, conv_b, fwd_tab, inv_tab, hr, gg, nyq, skip, fb, tc, kps):
    b, n, _ = u.shape
    d = D_MODEL
    kb = n // (fb * kps)
    ncb = d // tc
    ucol = lambda part: (lambda bi, c, o, k: (bi, 0, part * ncb + c))
    wcol = lambda part: (lambda bi, c, o, k: (0, part * ncb + c))
    spec = lambda bi, c, o, k: (o, k, c)
    per_o = lambda bi, c, o, k: (o, 0, c)
    return pl.pallas_call(
        functools.partial(_hyena_conv_kernel, fb=fb, kb=kb, kps=kps), grid=(b, ncb, 2, kb),
        in_specs=[pl.BlockSpec((None, n, tc), ucol(0)), pl.BlockSpec((None, n, tc), ucol(1)),
                  pl.BlockSpec((None, n, tc), ucol(2)),
                  pl.BlockSpec((3, tc), wcol(0)), pl.BlockSpec((3, tc), wcol(1)), pl.BlockSpec((3, tc), wcol(2)),
                  pl.BlockSpec((1, tc), wcol(0)), pl.BlockSpec((1, tc), wcol(1)), pl.BlockSpec((1, tc), wcol(2)),
                  pl.BlockSpec((kps, 2 * fb, n), lambda bi, c, o, k: (k, 0, 0)),
                  pl.BlockSpec((kps, n, 2 * fb), lambda bi, c, o, k: (k, 0, 0)),
                  pl.BlockSpec((None, kps * fb, tc), spec), pl.BlockSpec((None, kps * fb, tc), spec),
                  pl.BlockSpec((None, 1, tc), per_o), pl.BlockSpec((None, 1, tc), per_o)],
        out_specs=pl.BlockSpec((None, n, tc), lambda bi, c, o, k: (bi, 0, c)),
        out_shape=jax.ShapeDtypeStruct((b, n, d), BF16),
        scratch_shapes=[pltpu.VMEM((n, tc), BF16), pltpu.VMEM((n, tc), F32), pltpu.VMEM((n, tc), F32)],
        compiler_params=_cparams(4), name="hyena_conv",
    )(u, u, u, conv_w, conv_w, conv_w, conv_b, conv_b, conv_b, fwd_tab, inv_tab, hr, gg, nyq,
      skip.reshape(2, 1, d))


def _lag_spec_kernel(hid_ref, wf_ref, wb_ref, dec_ref, ph_ref, h_ref, fw_scr, bw_scr, *, s, nb):
    lag = pl.program_id(2)
    n = hid_ref.shape[0]

    @pl.when(lag == 0)
    def _():
        def split(v):
            hi = v.astype(BF16)
            return hi, (v - hi.astype(F32)).astype(BF16)

        def precise_dot(a_hi, a_lo, w):
            w_hi, w_lo = split(w)
            return (jnp.dot(a_hi, w_hi, preferred_element_type=F32) + jnp.dot(a_lo, w_hi, preferred_element_type=F32)
                    + jnp.dot(a_hi, w_lo, preferred_element_type=F32))

        dec = dec_ref[...]
        hid_hi, hid_lo = split(hid_ref[...])
        row = lax.broadcasted_iota(I32, (n, 1), 0)
        fw = precise_dot(hid_hi, hid_lo, wf_ref[...]) * dec
        bw = precise_dot(hid_hi, hid_lo, wb_ref[...]) * dec
        fw_scr[...] = fw.astype(BF16)
        bw_scr[...] = jnp.where(row == 0, 0.0, bw).astype(BF16)

    def piece(ph, scr, j):
        return jnp.dot(ph_ref[ph], scr[j * s:(j + 1) * s, :], preferred_element_type=F32)

    for idx in range(2 * nb - 1):
        m = idx - (nb - 1)

        @pl.when(lag == idx)
        def _(m=m):
            if m >= 1:
                h = piece(0, fw_scr, m) + piece(1, fw_scr, m - 1)
            elif m == 0:
                h = piece(0, fw_scr, 0) + piece(2, bw_scr, 0)
            else:
                h = piece(2, bw_scr, -m) + piece(3, bw_scr, -m - 1)
            h_ref[...] = h.astype(h_ref.dtype)


def _lag_spec(hid, w_out4, decay, phases, s, tc):
    n = hid.shape[0]
    d = D_MODEL
    nb = n // s
    return pl.pallas_call(
        functools.partial(_lag_spec_kernel, s=s, nb=nb), grid=(2, d // tc, 2 * nb - 1),
        in_specs=[pl.BlockSpec((n, FILTER_HID), lambda o, c, l: (0, 0)),
                  pl.BlockSpec((None, None, FILTER_HID, tc), lambda o, c, l: (0, o, 0, c)),
                  pl.BlockSpec((None, None, FILTER_HID, tc), lambda o, c, l: (1, o, 0, c)),
                  pl.BlockSpec((n, tc), lambda o, c, l: (0, c)),
                  pl.BlockSpec((4, 2 * s, s), lambda o, c, l: (0, 0, 0))],
        out_specs=pl.BlockSpec((None, None, 2 * s, tc), lambda o, c, l: (o, l, 0, c)),
        out_shape=jax.ShapeDtypeStruct((2, 2 * nb - 1, 2 * s, d), BF16),
        scratch_shapes=[pltpu.VMEM((n, tc), BF16), pltpu.VMEM((n, tc), BF16)],
        compiler_params=_cparams(3), name="hyena_lag_spec",
    )(hid, w_out4, w_out4, decay, phases)


def _part_conv_kernel(uv_ref, ux1_ref, ux2_ref, cwv_ref, cwx1_ref, cwx2_ref, cbv_ref, cbx1_ref, cbx2_ref,
                      f_ref, finv_ref, h_ref, sk_ref, o_ref, src_scr, gate_scr, zb_scr, zall, ybuf, *, s, nb, rc):
    src_scr[...] = _short_conv(uv_ref, cwv_ref, cbv_ref)
    gates = ((ux1_ref, cwx1_ref, cbx1_ref), (ux2_ref, cwx2_ref, cbx2_ref))
    for o, gate in enumerate(gates):
        zb_scr[...] = src_scr[...].astype(BF16)
        gate_scr[...] = _short_conv(*gate)
        for j in range(nb):
            zall[j] = jnp.dot(f_ref[...], zb_scr[j * s:(j + 1) * s, :], preferred_element_type=F32).astype(BF16)
        sk = sk_ref[o]
        top = lax.broadcasted_iota(I32, (rc, 1), 0) == 0
        for i in range(nb):
            pairs = [(j, i - j + nb - 1) for j in range(nb)]
            dc = nyq = None
            for j, m in pairs:
                t0 = zall[j, 0:rc, :].astype(F32)[0:1] * h_ref[o, m, 0:rc, :].astype(F32)[0:1]
                t1 = zall[j, s:s + rc, :].astype(F32)[0:1] * h_ref[o, m, s:s + rc, :].astype(F32)[0:1]
                dc, nyq = (t0, t1) if dc is None else (dc + t0, nyq + t1)
            for r0 in range(0, s, rc):
                re, im = slice(r0, r0 + rc), slice(s + r0, s + r0 + rc)
                yr = yi = None
                for j, m in pairs:
                    zr, zi, hr, g = zall[j, re, :], zall[j, im, :], h_ref[o, m, re, :], h_ref[o, m, im, :]
                    tr, ti = zr * hr - zi * g, zr * g + zi * hr
                    yr, yi = (tr, ti) if yr is None else (yr + tr, yi + ti)
                if r0 == 0:
                    yr, yi = jnp.where(top, dc.astype(BF16), yr), jnp.where(top, nyq.astype(BF16), yi)
                ybuf[re, :] = yr
                ybuf[im, :] = yi
            y = jnp.dot(finv_ref[...], ybuf[...], preferred_element_type=F32)
            rows = slice(i * s, (i + 1) * s)
            zn = gate_scr[rows, :] * (y + src_scr[rows, :] * sk)
            if o + 1 < len(gates):
                src_scr[rows, :] = zn
            else:
                o_ref[rows, :] = zn.astype(o_ref.dtype)


def _part_conv(u, conv_w, conv_b, f_tab, finv_tab, spec, skip, s, tc):
    b, n, _ = u.shape
    d = D_MODEL
    ncb = d // tc
    nb = n // s
    col = lambda part: (lambda c, bi: (bi, 0, part * ncb + c))
    wcol = lambda part: (lambda c, bi: (0, part * ncb + c))
    const = lambda c, bi: (0, 0)
    in_specs = ([pl.BlockSpec((None, n, tc), col(p)) for p in range(3)]
                + [pl.BlockSpec((3, tc), wcol(p)) for p in range(3)]
                + [pl.BlockSpec((1, tc), wcol(p)) for p in range(3)]
                + [pl.BlockSpec((2 * s, s), const), pl.BlockSpec((s, 2 * s), const),
                   pl.BlockSpec((2, 2 * nb - 1, 2 * s, tc), lambda c, bi: (0, 0, 0, c), pipeline_mode=pl.Buffered(1)),
                   pl.BlockSpec((2, 1, tc), lambda c, bi: (0, 0, c))])
    return pl.pallas_call(
        functools.partial(_part_conv_kernel, s=s, nb=nb, rc=32), grid=(ncb, b),
        in_specs=in_specs,
        out_specs=pl.BlockSpec((None, n, tc), lambda c, bi: (bi, 0, c)),
        out_shape=jax.ShapeDtypeStruct((b, n, d), BF16),
        scratch_shapes=[pltpu.VMEM((n, tc), F32), pltpu.VMEM((n, tc), F32), pltpu.VMEM((n, tc), BF16),
                        pltpu.VMEM((nb, 2 * s, tc), BF16), pltpu.VMEM((2 * s, tc), BF16)],
        compiler_params=_cparams(2), name="hyena_part_conv",
    )(u, u, u, conv_w, conv_w, conv_w, conv_b, conv_b, conv_b, f_tab, finv_tab, spec, skip.reshape(2, 1, d))


def kernel(x, c, ctx, c_ctx, mod_w, mod_b, ln_g, ln_b, attn_w_in, attn_lambda, attn_subln_g, attn_sink, attn_w_out,
           hy_w_in, hy_conv_w, hy_conv_b, hy_ffn_w_in, hy_ffn_w_hid, hy_ffn_b, hy_sin_freq, hy_ffn_w_out, hy_skip,
           hy_w_out, router_w, router_bias, exp_w_gate, exp_w_up, exp_w_down):
    b, n, d = x.shape
    nc = ctx.shape[1]
    t = b * n
    assert d == D_MODEL and b + 1 <= MOD_ROWS and n % 512 == 0 and nc % 256 == 0

    c_rows = jnp.zeros((MOD_ROWS, d), F32).at[:b].set(c).at[b].set(c_ctx)
    mods = _mods(c_rows, mod_w, mod_b)
    router_wt = jnp.pad(router_w, ((0, 0), (0, LANES - N_EXPERTS)))
    x2d = x.reshape(t, d)

    sh1, sc1, g1, sh2, sc2, g2 = jnp.split(mods[0], 6, axis=-1)
    cos, sin = _rope_tables(n)
    w_in = attn_w_in[0].astype(BF16)
    q_groups = tuple(range(Q_W // LANES))
    rope_groups = q_groups + tuple(range(Q_W // LANES, (Q_W + DIFF_QK_W) // LANES)) + (
        (Q_W + DIFF_QK_W + DIFF_VW) // LANES,)
    q_scales = tuple((g, HEAD_DIM ** -0.5 * LOG2_E) for g in q_groups)
    proj = _modmm(x2d, sc1[:b], sh1[:b], w_in, rows_per_mod=n, tm=1024,
                  rope=(jnp.asarray(cos), jnp.asarray(sin), rope_groups, q_scales), name="attn_in_proj")
    proj_c = _modmm(ctx.reshape(b * nc, d), sc1[b:b + 1], sh1[b:b + 1], w_in[:, Q_W:], rows_per_mod=b * nc, tm=256,
                    name="ctx_in_proj")
    proj = proj.reshape(b, n, ATTN_PROJ_W)
    proj_c = proj_c.reshape(b, nc, KV_W)
    lam_init = 0.8 - 0.6 * math.exp(-0.3 * 0)
    oa = _diff_attn(proj, proj_c, attn_lambda[0], attn_subln_g[0], lam_init, tq=1024, sub=128)
    ow = _win_attn(proj, proj_c, attn_sink[0], step=512, tq=128)
    w_out = attn_w_out[0].astype(BF16)
    x1, h2, lgt = _proj_ln([oa.reshape(t, DIFF_VW), ow.reshape(t, WIN_Q_W)], [w_out[:DIFF_VW], w_out[DIFF_VW:]],
                           x2d, g1[:b], ln_g[0, 0], ln_b[0, 0], sc2[:b], sh2[:b], router_wt, rows_per_mod=n, tm=1024,
                           name="attn_out_proj_ln")
    x2 = _moe(h2, lgt, x1, g2[:b], ln_g[0, 1], ln_b[0, 1], router_bias, exp_w_gate, exp_w_up, exp_w_down, layer=0,
              rows_per_mod=n)

    sh1, sc1, g1, sh2, sc2, g2 = jnp.split(mods[1], 6, axis=-1)
    u = _modmm(x2, sc1[:b], sh1[:b], hy_w_in[0].astype(BF16), rows_per_mod=n, tm=1024, name="hyena_in_proj")
    part, tc = HYENA_PART, 256
    phases_np, finv_np = _partition_tables(part)
    phases, finv_tab = jnp.asarray(phases_np).astype(BF16), jnp.asarray(finv_np).astype(BF16)
    zp, decay = _filter_tables(n)
    w_in_p = jnp.zeros((FILTER_HID, FILTER_HID), F32).at[:FILTER_EMB].set(hy_ffn_w_in[0])
    hid = _filter_mlp(jnp.asarray(zp), w_in_p, hy_ffn_w_hid[0], hy_ffn_b[0], hy_sin_freq[0])
    spec = _lag_spec(hid, hy_ffn_w_out[0].reshape(FILTER_HID, 2, 2, d).transpose(1, 2, 0, 3), jnp.asarray(decay),
                     phases, part, tc=512)
    u3 = u.reshape(b, n, 3 * d)
    conv_b = hy_conv_b[0].reshape(1, 3 * d)
    z2 = _part_conv(u3, hy_conv_w[0], conv_b, phases[0], finv_tab, spec, hy_skip[0], part, tc)
    x3, h2, lgt = _proj_ln([z2.reshape(t, d)], [hy_w_out[0].astype(BF16)], x2, g1[:b], ln_g[1, 0], ln_b[1, 0],
                           sc2[:b], sh2[:b], router_wt, rows_per_mod=n, tm=1024, name="hyena_out_proj_ln")
    x4 = _moe(h2, lgt, x3, g2[:b], ln_g[1, 1], ln_b[1, 1], router_bias, exp_w_gate, exp_w_up, exp_w_down, layer=1,
              rows_per_mod=n)
    return x4.reshape(b, n, d)
```

```python
import functools
import math

import jax
import jax.numpy as jnp
import numpy as np
from jax import lax
from jax.experimental import pallas as pl
from jax.experimental.pallas import tpu as pltpu

F32 = jnp.float32
BF16 = jnp.bfloat16
I32 = jnp.int32

D_MODEL = 1024
DEPTH = 2
GRID_W = 64
HEAD_DIM = 64
DIFF_HEADS = 4
WIN_Q_HEADS = 8
WIN_KV_HEADS = 2
WINDOW = 128
WIN_BLOCK = 128
ROPE_BASE = 10000.0
DIFF_QK_W = DIFF_HEADS * 2 * HEAD_DIM
DIFF_VW = DIFF_HEADS * 2 * HEAD_DIM
WIN_Q_W = WIN_Q_HEADS * HEAD_DIM
WIN_KV_W = WIN_KV_HEADS * HEAD_DIM
Q_W = DIFF_QK_W + WIN_Q_W
KV_W = DIFF_QK_W + DIFF_VW + 2 * WIN_KV_W
ATTN_PROJ_W = Q_W + KV_W
FILTER_EMB = 33
FILTER_HID = 64
DECAY_TARGET = 1e-2
FAST_DECAY_PCT = 0.3
SLOW_DECAY_PCT = 1.5
N_EXPERTS = 16
N_GROUPS = 4
EXPERTS_PER_GROUP = N_EXPERTS // N_GROUPS
EXPERT_FF = 1024
LN_EPS = 1e-5
DEEPNORM_ALPHA = (2 * DEPTH) ** 0.25
NEG_INF = -1e30
LOG2_E = 1.4426950408889634

LANES = 128
MOD_ROWS = 16
VMEM_LIMIT = 60 * 1024 * 1024
MOE_BM = 512
MOE_TILE = 512
HYENA_PART = 512
ROW_ALIGN = 16
NT_DIMS = (((1,), (1,)), ((), ()))


def _cparams(n_axes):
    return pltpu.CompilerParams(dimension_semantics=("arbitrary",) * n_axes, vmem_limit_bytes=VMEM_LIMIT)


@functools.lru_cache(maxsize=None)
def _rope_tables(n):
    rows = n // GRID_W
    r, col = np.meshgrid(np.arange(rows, dtype=np.float32), np.arange(GRID_W, dtype=np.float32), indexing="ij")
    axis_dim = HEAD_DIM // 2
    inv_freq = (ROPE_BASE ** (-np.arange(0, axis_dim, 2, dtype=np.float32) / axis_dim)).astype(np.float32)
    ang = np.concatenate([r.reshape(-1, 1) * inv_freq, col.reshape(-1, 1) * inv_freq], -1)
    ang = np.concatenate([ang, ang], -1).astype(np.float32)
    cos, sin = np.cos(ang), np.sin(ang)
    half = np.arange(HEAD_DIM) < HEAD_DIM // 2
    sin_signed = np.where(half[None, :], -sin, sin)
    reps = LANES // HEAD_DIM
    return (np.tile(cos, (1, reps)).astype(np.float32), np.tile(sin_signed, (1, reps)).astype(np.float32))


@functools.lru_cache(maxsize=None)
def _filter_tables(n):
    t = np.linspace(0.0, 1.0, n, dtype=np.float32)[:, None]
    bands = (FILTER_EMB - 1) // 2
    w = (2.0 * math.pi * np.arange(n, dtype=np.float32)[:, None] / n).astype(np.float32)
    fr = np.linspace(1e-4, bands - 1, bands, dtype=np.float32)[None, :]
    z = np.concatenate([t, np.cos(fr * w), -np.sin(fr * w)], -1).astype(np.float32)
    zp = np.zeros((n, FILTER_HID), np.float32)
    zp[:, :FILTER_EMB] = z
    deltas = np.abs(np.linspace(math.log(DECAY_TARGET) / SLOW_DECAY_PCT, math.log(DECAY_TARGET) / FAST_DECAY_PCT,
                                D_MODEL, dtype=np.float32))
    decay = np.exp(-t * deltas[None, :]).astype(np.float32)
    return zp, decay


@functools.lru_cache(maxsize=None)
def _partition_tables(s):
    big = 2 * s
    k = np.arange(s, dtype=np.int64)[:, None]
    r = np.arange(s, dtype=np.int64)[None, :]

    def phase(pr, sin_sign, drop_r0):
        ang = ((k * pr) % big).astype(np.float64) * (2.0 * math.pi / big)
        c, sn = np.cos(ang), sin_sign * np.sin(ang)
        sn[0, :] = 1.0 - 2.0 * (pr[0] & 1)
        if drop_r0:
            c[:, 0] = 0.0
            sn[:, 0] = 0.0
        return np.concatenate([c, sn], 0)

    phases = np.stack([phase(r, 1.0, False), phase(s - r, -1.0, True), phase(r, -1.0, False), phase(s - r, 1.0, True)])
    ang = ((r.T * k.T) % big).astype(np.float64) * (2.0 * math.pi / big)
    ci = np.cos(ang) * (2.0 / big)
    ci[:, 0] = 1.0 / big
    si = np.sin(ang) * (2.0 / big)
    si[:, 0] = (1.0 - 2.0 * (np.arange(s) & 1)) / big
    return phases.astype(np.float32), np.concatenate([ci, si], 1).astype(np.float32)


def _mods_kernel(c_ref, w_ref, b_ref, o_ref):
    c = c_ref[...]
    a = (c * jax.nn.sigmoid(c)).astype(BF16)
    o_ref[...] = jnp.dot(a, w_ref[...].astype(BF16), preferred_element_type=F32) + b_ref[...]


def _mods(c_rows, mod_w, mod_b):
    d = D_MODEL
    tn = 1536
    return pl.pallas_call(
        _mods_kernel,
        grid=(DEPTH, 6 * d // tn),
        in_specs=[pl.BlockSpec((MOD_ROWS, d), lambda l, j: (0, 0)),
                  pl.BlockSpec((None, d, tn), lambda l, j: (l, 0, j)),
                  pl.BlockSpec((None, 1, tn), lambda l, j: (l, 0, j))],
        out_specs=pl.BlockSpec((None, MOD_ROWS, tn), lambda l, j: (l, 0, j)),
        out_shape=jax.ShapeDtypeStruct((DEPTH, MOD_ROWS, 6 * d), F32),
        compiler_params=_cparams(2), name="mods",
    )(c_rows, mod_w, mod_b.reshape(DEPTH, 1, 6 * d))


def _modmm_kernel(*refs, n_groups, rope_groups, scaled_groups, chunk):
    if rope_groups:
        x_ref, sc_ref, sh_ref, w_ref, cos_ref, sin_ref, o_ref = refs
        cos, sin = cos_ref[...], sin_ref[...]
        lane = lax.broadcasted_iota(I32, (1, LANES), 1)
        first_half = (lane % HEAD_DIM) < HEAD_DIM // 2
    else:
        x_ref, sc_ref, sh_ref, w_ref, o_ref = refs
    h = (x_ref[...] * (1.0 + sc_ref[...]) + sh_ref[...]).astype(BF16)
    scales = dict(scaled_groups)
    gpc = chunk // LANES
    for c in range(n_groups // gpc):
        acc = jnp.dot(h, w_ref[:, c * chunk:(c + 1) * chunk], preferred_element_type=F32)
        for j in range(gpc):
            g = c * gpc + j
            blk = acc[:, j * LANES:(j + 1) * LANES]
            if g in rope_groups:
                rot = jnp.where(first_half, pltpu.roll(blk, LANES - HEAD_DIM // 2, 1), pltpu.roll(blk, HEAD_DIM // 2, 1))
                blk = blk * cos + rot * sin
                if g in scales:
                    blk = blk * scales[g]
            o_ref[:, g * LANES:(g + 1) * LANES] = blk.astype(o_ref.dtype)


def _modmm(x2d, sc, sh, w, rows_per_mod, tm, rope=None, name="modmm"):
    t, d = x2d.shape
    n = w.shape[1]
    tiles_per_mod = rows_per_mod // tm
    nmod = sc.shape[0]
    in_specs = [pl.BlockSpec((tm, d), lambda i: (i, 0)),
                pl.BlockSpec((None, 1, d), lambda i: (i // tiles_per_mod, 0, 0)),
                pl.BlockSpec((None, 1, d), lambda i: (i // tiles_per_mod, 0, 0)),
                pl.BlockSpec((d, n), lambda i: (0, 0))]
    args = [x2d, sc.reshape(nmod, 1, d), sh.reshape(nmod, 1, d), w]
    rope_groups, scaled_groups = (), ()
    if rope is not None:
        cos, sin, rope_groups, scaled_groups = rope
        in_specs += [pl.BlockSpec((tm, LANES), lambda i: (i % tiles_per_mod, 0)),
                     pl.BlockSpec((tm, LANES), lambda i: (i % tiles_per_mod, 0))]
        args += [cos, sin]
    kern = functools.partial(_modmm_kernel, n_groups=n // LANES, rope_groups=tuple(rope_groups),
                             scaled_groups=tuple(scaled_groups), chunk=256)
    return pl.pallas_call(
        kern, grid=(t // tm,), in_specs=in_specs,
        out_specs=pl.BlockSpec((tm, n), lambda i: (i, 0)),
        out_shape=jax.ShapeDtypeStruct((t, n), BF16),
        compiler_params=_cparams(1), name=name,
    )(*args)


def _diff_attn_kernel(q_ref, k_ref, v_ref, kc_ref, vc_ref, lam_ref, g_ref, o_ref, *, lam_init, sub):
    lane = lax.broadcasted_iota(I32, (1, LANES), 1)
    k, kc = k_ref[...], kc_ref[...]
    v, vc = v_ref[...], vc_ref[...]

    def scores(r0):
        q = q_ref[r0:r0 + sub, :]
        zero = jnp.zeros_like(q)
        out = []
        for qm in (jnp.where(lane < HEAD_DIM, q, zero), jnp.where(lane >= HEAD_DIM, q, zero)):
            out.append((lax.dot_general(qm, k, NT_DIMS, preferred_element_type=F32),
                        lax.dot_general(qm, kc, NT_DIMS, preferred_element_type=F32)))
        return out

    def probs(sl, sc):
        m = jnp.maximum(jnp.max(sl, -1, keepdims=True), jnp.max(sc, -1, keepdims=True))
        pl_, pc = jnp.exp2(sl - m), jnp.exp2(sc - m)
        den = jnp.sum(pl_, -1, keepdims=True) + jnp.sum(pc, -1, keepdims=True)
        return pl_, pc, 1.0 / den

    lv = lam_ref[...]
    lam = (jnp.exp(jnp.sum(lv[0:1] * lv[1:2], keepdims=True)) - jnp.exp(jnp.sum(lv[2:3] * lv[3:4], keepdims=True))
           + lam_init)
    starts = list(range(0, q_ref.shape[0], sub))
    nxt = scores(starts[0])
    for i, r0 in enumerate(starts):
        cur = nxt
        if i + 1 < len(starts):
            nxt = scores(starts[i + 1])
        p1l, p1c, r1 = probs(*cur[0])
        p2l, p2c, r2 = probs(*cur[1])
        w2 = lam * r2
        al = (p1l * r1 - p2l * w2).astype(BF16)
        ac = (p1c * r1 - p2c * w2).astype(BF16)
        o = jnp.dot(al, v, preferred_element_type=F32) + jnp.dot(ac, vc, preferred_element_type=F32)
        ms = jnp.mean(o * o, -1, keepdims=True)
        o_ref[r0:r0 + sub, :] = (o * lax.rsqrt(ms + LN_EPS) * g_ref[...] * (1.0 - lam_init)).astype(o_ref.dtype)


def _diff_attn(proj, proj_c, lam_vec, subln_g, lam_init, tq, sub):
    b, n, _ = proj.shape
    nc = proj_c.shape[1]
    kcol = Q_W // LANES
    vcol = (Q_W + DIFF_QK_W) // LANES
    vccol = DIFF_QK_W // LANES
    kern = functools.partial(_diff_attn_kernel, lam_init=lam_init, sub=sub)
    return pl.pallas_call(
        kern, grid=(b, DIFF_HEADS, n // tq),
        in_specs=[pl.BlockSpec((None, tq, LANES), lambda bi, h, i: (bi, i, h)),
                  pl.BlockSpec((None, n, LANES), lambda bi, h, i: (bi, 0, kcol + h)),
                  pl.BlockSpec((None, n, LANES), lambda bi, h, i: (bi, 0, vcol + h)),
                  pl.BlockSpec((None, nc, LANES), lambda bi, h, i: (bi, 0, h)),
                  pl.BlockSpec((None, nc, LANES), lambda bi, h, i: (bi, 0, vccol + h)),
                  pl.BlockSpec((4, HEAD_DIM), lambda bi, h, i: (0, 0)),
                  pl.BlockSpec((1, LANES), lambda bi, h, i: (0, 0))],
        out_specs=pl.BlockSpec((None, tq, LANES), lambda bi, h, i: (bi, i, h)),
        out_shape=jax.ShapeDtypeStruct((b, n, DIFF_VW), BF16),
        compiler_params=_cparams(3), name="diff_attn",
    )(proj, proj, proj, proj_c, proj_c, lam_vec, subln_g.reshape(1, LANES))


def _win_attn_kernel(sink_ref, q_ref, k_ref, v_ref, kc_ref, vc_ref, o_ref, *, seq, tq):
    for r0 in range(0, q_ref.shape[0], tq):
        _win_attn_rows(sink_ref, q_ref, k_ref, v_ref, kc_ref, vc_ref, o_ref, r0, seq=seq, tq=tq)


def _win_attn_rows(sink_ref, q_ref, k_ref, v_ref, kc_ref, vc_ref, o_ref, r0, *, seq, tq):
    first = pl.program_id(1) * q_ref.shape[0] + r0
    gq = WIN_Q_HEADS // WIN_KV_HEADS
    kw = tq + 2 * WINDOW
    start = pl.multiple_of(jnp.clip(first - WINDOW, 0, seq - kw), WINDOW)
    k_win, v_win = k_ref[pl.ds(start, kw), :], v_ref[pl.ds(start, kw), :]
    kc, vc = kc_ref[...], vc_ref[...]
    lane = lax.broadcasted_iota(I32, (1, LANES), 1)
    row = lax.broadcasted_iota(I32, (gq * tq, 1), 0)
    q_abs = first + row % tq
    k_abs = start + lax.broadcasted_iota(I32, (1, kw), 1)
    allowed = jnp.abs(q_abs - k_abs) <= WINDOW
    head = row // tq
    q = q_ref[r0:r0 + tq, :].astype(F32)
    for g in range(WIN_KV_HEADS):
        in_g = (lane // HEAD_DIM) == g
        parts = []
        for j in range(gq):
            hq = g * gq + j
            x = q[:, (hq // 2) * LANES:(hq // 2 + 1) * LANES]
            if hq % 2 != g:
                x = pltpu.roll(x, HEAD_DIM, 1)
            parts.append(jnp.where(in_g, x, 0.0))
        qs = jnp.concatenate(parts, 0).astype(BF16)
        s_loc = lax.dot_general(qs, k_win, NT_DIMS, preferred_element_type=F32)
        s_ctx = lax.dot_general(qs, kc, NT_DIMS, preferred_element_type=F32)
        s_loc = jnp.where(allowed, s_loc, NEG_INF)
        sk = jnp.zeros((gq * tq, 1), F32)
        for j in range(gq):
            sk = jnp.where(head == j, sink_ref[g * gq + j] * LOG2_E, sk)
        m = jnp.maximum(jnp.maximum(jnp.max(s_loc, -1, keepdims=True), jnp.max(s_ctx, -1, keepdims=True)), sk)
        p_loc, p_ctx = jnp.exp2(s_loc - m), jnp.exp2(s_ctx - m)
        den = jnp.sum(p_loc, -1, keepdims=True) + jnp.sum(p_ctx, -1, keepdims=True) + jnp.exp2(sk - m)
        o = (jnp.dot(p_loc.astype(BF16), v_win, preferred_element_type=F32)
             + jnp.dot(p_ctx.astype(BF16), vc, preferred_element_type=F32)) * (1.0 / den)
        for cb in range(gq // 2):
            pair = []
            for half in range(2):
                piece = o[(2 * cb + half) * tq:(2 * cb + half + 1) * tq]
                pair.append(piece if half == g else pltpu.roll(piece, HEAD_DIM, 1))
            col = (g * gq // 2 + cb) * LANES
            o_ref[r0:r0 + tq, col:col + LANES] = jnp.where(lane < HEAD_DIM, pair[0], pair[1]).astype(o_ref.dtype)


def _win_attn(proj, proj_c, sink, step, tq):
    b, n, _ = proj.shape
    nc = proj_c.shape[1]
    qcol = DIFF_QK_W // WIN_Q_W
    kcol = (Q_W + DIFF_QK_W + DIFF_VW) // LANES
    kccol = (DIFF_QK_W + DIFF_VW) // LANES
    kern = functools.partial(_win_attn_kernel, seq=n, tq=tq)
    return pl.pallas_call(
        kern, grid=(b, n // step),
        in_specs=[pl.BlockSpec(memory_space=pltpu.SMEM),
                  pl.BlockSpec((None, step, WIN_Q_W), lambda bi, i: (bi, i, qcol)),
                  pl.BlockSpec((None, n, LANES), lambda bi, i: (bi, 0, kcol)),
                  pl.BlockSpec((None, n, LANES), lambda bi, i: (bi, 0, kcol + 1)),
                  pl.BlockSpec((None, nc, LANES), lambda bi, i: (bi, 0, kccol)),
                  pl.BlockSpec((None, nc, LANES), lambda bi, i: (bi, 0, kccol + 1))],
        out_specs=pl.BlockSpec((None, step, WIN_Q_W), lambda bi, i: (bi, i, 0)),
        out_shape=jax.ShapeDtypeStruct((b, n, WIN_Q_W), BF16),
        compiler_params=_cparams(2), name="win_attn",
    )(sink, proj, proj, proj, proj_c, proj_c)


def _layer_norm(r, g, b):
    mu = jnp.mean(r, -1, keepdims=True)
    dlt = r - mu
    var = jnp.mean(dlt * dlt, -1, keepdims=True)
    return dlt * lax.rsqrt(var + LN_EPS) * g + b


def _proj_ln_kernel(*refs, n_in, sub):
    a_refs = refs[:n_in]
    w_refs = refs[n_in:2 * n_in]
    x_ref, gate_ref, lng_ref, lnb_ref, sc_ref, sh_ref, rw_ref, x1_ref, h2_ref, lg_ref = refs[2 * n_in:]

    def split(v):
        hi = v.astype(BF16)
        return hi, (v - hi.astype(F32)).astype(BF16)

    def project(r0):
        y = jnp.dot(a_refs[0][r0:r0 + sub, :], w_refs[0][...], preferred_element_type=F32)
        for a_ref, w_ref in zip(a_refs[1:], w_refs[1:]):
            y = y + jnp.dot(a_ref[r0:r0 + sub, :], w_ref[...], preferred_element_type=F32)
        return y

    rw_hi, rw_lo = split(rw_ref[...])
    rw_both = jnp.concatenate([rw_hi, rw_lo], 1)
    starts = list(range(0, x_ref.shape[0], sub))
    nxt = project(starts[0])
    for i, r0 in enumerate(starts):
        y = nxt
        if i + 1 < len(starts):
            nxt = project(starts[i + 1])
        xn = _layer_norm(DEEPNORM_ALPHA * x_ref[r0:r0 + sub, :] + gate_ref[...] * y, lng_ref[...], lnb_ref[...])
        x1_ref[r0:r0 + sub, :] = xn
        h2 = xn * (1.0 + sc_ref[...]) + sh_ref[...]
        h2_ref[r0:r0 + sub, :] = h2.astype(h2_ref.dtype)
        h_hi, h_lo = split(h2)
        hh = jnp.dot(h_hi, rw_both, preferred_element_type=F32)
        lg = hh[:, :LANES] + hh[:, LANES:] + jnp.dot(h_lo, rw_hi, preferred_element_type=F32)
        lg_ref[:, r0:r0 + sub] = lg.T[:N_EXPERTS, :]


def _proj_ln(acts, ws, x2d, gate, ln_g, ln_b, sc, sh, router_wt, rows_per_mod, tm, name):
    t, d = x2d.shape
    n_in = len(acts)
    tiles_per_mod = rows_per_mod // tm
    nmod = gate.shape[0]
    row = lambda i: (i, 0)
    full = lambda i: (0, 0)
    mod = lambda i: (i // tiles_per_mod, 0, 0)
    in_specs = ([pl.BlockSpec((tm, a.shape[1]), row) for a in acts]
                + [pl.BlockSpec(w.shape, full) for w in ws]
                + [pl.BlockSpec((tm, d), row), pl.BlockSpec((None, 1, d), mod),
                   pl.BlockSpec((1, d), full), pl.BlockSpec((1, d), full),
                   pl.BlockSpec((None, 1, d), mod), pl.BlockSpec((None, 1, d), mod),
                   pl.BlockSpec((d, LANES), full)])
    return pl.pallas_call(
        functools.partial(_proj_ln_kernel, n_in=n_in, sub=LANES), grid=(t // tm,), in_specs=in_specs,
        out_specs=[pl.BlockSpec((tm, d), row), pl.BlockSpec((tm, d), row),
                   pl.BlockSpec((N_EXPERTS, tm), lambda i: (0, i))],
        out_shape=[jax.ShapeDtypeStruct((t, d), F32), jax.ShapeDtypeStruct((t, d), BF16),
                   jax.ShapeDtypeStruct((N_EXPERTS, t), F32)],
        compiler_params=_cparams(1), name=name,
    )(*acts, *ws, x2d, gate.reshape(nmod, 1, d), ln_g.reshape(1, d), ln_b.reshape(1, d),
      sc.reshape(nmod, 1, d), sh.reshape(nmod, 1, d), router_wt)


def _first_argmax(vals):
    idx = jnp.zeros(vals[0].shape, I32)
    best = vals[0]
    for j in range(1, len(vals)):
        upd = vals[j] > best
        idx = jnp.where(upd, j, idx)
        best = jnp.where(upd, vals[j], best)
    return idx, best


def _route_kernel(lg_ref, bias_ref, pos_ref, w_ref, tab_ref, cnt_ref, carry_ref, off_ref, *, tr):
    @pl.when(pl.program_id(0) == 0)
    def _():
        carry_ref[...] = jnp.zeros_like(carry_ref)

    lg = lg_ref[...]
    ex = jnp.exp(lg - jnp.max(lg, 0, keepdims=True))
    scores = ex / jnp.sum(ex, 0, keepdims=True)
    sel = scores + bias_ref[...]
    rows = [sel[e:e + 1] for e in range(N_EXPERTS)]
    group_scores = []
    for g in range(N_GROUPS):
        r = rows[g * EXPERTS_PER_GROUP:(g + 1) * EXPERTS_PER_GROUP]
        best = None
        for i in range(EXPERTS_PER_GROUP):
            for j in range(i + 1, EXPERTS_PER_GROUP):
                s = r[i] + r[j]
                best = s if best is None else jnp.maximum(best, s)
        group_scores.append(best)
    grp, _ = _first_argmax(group_scores)
    vals = []
    for j in range(EXPERTS_PER_GROUP):
        v = rows[(N_GROUPS - 1) * EXPERTS_PER_GROUP + j]
        for g in range(N_GROUPS - 2, -1, -1):
            v = jnp.where(grp == g, rows[g * EXPERTS_PER_GROUP + j], v)
        vals.append(v)
    i0, _ = _first_argmax(vals)
    i1, _ = _first_argmax([jnp.where(i0 == j, -jnp.inf, vals[j]) for j in range(EXPERTS_PER_GROUP)])
    e0 = grp * EXPERTS_PER_GROUP + i0
    e1 = grp * EXPERTS_PER_GROUP + i1
    eid = lax.broadcasted_iota(I32, (N_EXPERTS, 1), 0)
    oh0, oh1 = eid == e0, eid == e1
    s0 = jnp.sum(jnp.where(oh0, scores, 0.0), 0, keepdims=True)
    s1 = jnp.sum(jnp.where(oh1, scores, 0.0), 0, keepdims=True)
    den = s0 + s1
    member = jnp.where(oh0 | oh1, 1.0, 0.0)
    before = lax.broadcasted_iota(I32, (tr, tr), 0) < lax.broadcasted_iota(I32, (tr, tr), 1)
    upper = jnp.where(before, 1.0, 0.0).astype(BF16)
    cnt = jnp.dot(member.astype(BF16), upper, preferred_element_type=F32)
    run = jnp.sum(member, 1, keepdims=True)
    run = jnp.floor((run + (ROW_ALIGN - 1)) * (1.0 / ROW_ALIGN)) * ROW_ALIGN
    run = jnp.broadcast_to(run, (N_EXPERTS, LANES))
    acc = jnp.zeros((1, LANES), F32)
    for e in range(N_EXPERTS):
        off_ref[e:e + 1, :] = acc
        acc = acc + run[e:e + 1]
    off = off_ref[...]
    at = off[:, 0:1] + cnt
    pos_ref[0:1, :] = jnp.sum(jnp.where(oh0, at, 0.0), 0, keepdims=True).astype(I32)
    pos_ref[1:2, :] = jnp.sum(jnp.where(oh1, at, 0.0), 0, keepdims=True).astype(I32)
    w_ref[0:1, :] = s0 / den
    w_ref[1:2, :] = s1 / den
    tab_ref[0] = off.astype(I32)
    tab_ref[1] = run.astype(I32)
    tab_ref[2] = carry_ref[...].astype(I32)
    carry_ref[...] = carry_ref[...] + run
    cnt_ref[...] = carry_ref[...]


def _route(logits_t, router_bias, tr):
    t = logits_t.shape[1]
    tok = lambda i: (0, i)
    return pl.pallas_call(
        functools.partial(_route_kernel, tr=tr), grid=(t // tr,),
        in_specs=[pl.BlockSpec((N_EXPERTS, tr), tok), pl.BlockSpec((N_EXPERTS, 1), lambda i: (0, 0))],
        out_specs=[pl.BlockSpec((2, tr), tok), pl.BlockSpec((2, tr), tok),
                   pl.BlockSpec((None, 3, N_EXPERTS, LANES), lambda i: (i, 0, 0, 0)),
                   pl.BlockSpec((N_EXPERTS, LANES), lambda i: (0, 0))],
        out_shape=[jax.ShapeDtypeStruct((2, t), I32), jax.ShapeDtypeStruct((2, t), F32),
                   jax.ShapeDtypeStruct((t // tr, 3, N_EXPERTS, LANES), I32),
                   jax.ShapeDtypeStruct((N_EXPERTS, LANES), F32)],
        scratch_shapes=[pltpu.VMEM((N_EXPERTS, LANES), F32), pltpu.VMEM((N_EXPERTS, LANES), F32)],
        compiler_params=_cparams(1), name="route",
    )(logits_t, router_bias.reshape(N_EXPERTS, 1))


def _run_copies(n, max_rows, make_copy, wait):
    sz = max_rows
    while sz >= ROW_ALIGN:
        start = (n // (2 * sz)) * (2 * sz)

        @pl.when((n & sz) != 0)
        def _(start=start, sz=sz):
            cp = make_copy(start, sz)
            cp.wait() if wait else cp.start()

        sz //= 2


def _wait_rows(total, max_rows, make_copy):
    sz = max_rows
    while sz >= ROW_ALIGN:
        @pl.when((total & sz) != 0)
        def _(sz=sz):
            make_copy(sz).wait()

        sz //= 2


def _rows(ref, start, size):
    return ref.at[pl.ds(pl.multiple_of(start, ROW_ALIGN), size)]


def _dispatch_kernel(tab_ref, h_ref, pos_ref, w_ref, xs_ref, buf, zbuf, sem, *, tt, n_tiles, bm):
    i = pl.program_id(0)
    d = h_ref.shape[1]
    sb = buf.shape[1]
    ne = N_EXPERTS
    fill = 3 * n_tiles * ne

    @pl.when(i == 0)
    def _():
        zbuf[...] = jnp.zeros_like(zbuf)
        for wait in (False, True):
            for e in range(ne):
                dst, n = tab_ref[fill + e], tab_ref[fill + ne + e]
                _run_copies(n, bm // 2, lambda s, z: pltpu.make_async_copy(
                    zbuf.at[pl.ds(0, z)], _rows(xs_ref, dst + s, z), sem.at[2]), wait)

        def zero_block(j, carry):
            for half in range(2):
                cp = pltpu.make_async_copy(zbuf, _rows(xs_ref, j * bm + half * (bm // 2), bm // 2), sem.at[2])
                cp.start()
                cp.wait()
            return carry

        lax.fori_loop(tab_ref[fill + 2 * ne], xs_ref.shape[0] // bm, zero_block, 0)

    def tile_copies(tile, slot, wait):
        for e in range(ne):
            off = tab_ref[tile * ne + e]
            n = tab_ref[(n_tiles + tile) * ne + e]
            dst = tab_ref[(2 * n_tiles + tile) * ne + e]
            _run_copies(n, tt, lambda s, z: pltpu.make_async_copy(
                _rows(buf.at[slot], off + s, z), _rows(xs_ref, dst + s, z), sem.at[slot]), wait)

    slot = i % 2
    pos, w = pos_ref[...], w_ref[...]
    hb = h_ref[...].astype(BF16)
    for r0 in range(0, sb, LANES):
        srow = r0 + lax.broadcasted_iota(I32, (LANES, 1), 0)
        m0, m1 = srow == pos[0:1], srow == pos[1:2]
        perm = (jnp.where(m0, 1.0, 0.0) + jnp.where(m1, 1.0, 0.0)).astype(BF16)
        buf[slot, r0:r0 + LANES, :d] = jnp.dot(perm, hb, preferred_element_type=F32).astype(BF16)
        wrow = jnp.sum(jnp.where(m0, w[0:1], 0.0) + jnp.where(m1, w[1:2], 0.0), 1, keepdims=True)
        w_hi = wrow.astype(BF16).astype(F32)
        w_lo = (wrow - w_hi).astype(BF16).astype(F32)
        lane = lax.broadcasted_iota(I32, (1, LANES), 1)
        buf[slot, r0:r0 + LANES, d:] = jnp.where(lane < LANES // 2, w_hi, w_lo).astype(BF16)

    def wait_tile(tile, slot):
        total = tab_ref[fill + 2 * ne + 1 + tile]
        _wait_rows(total, tt * 2, lambda z: pltpu.make_async_copy(
            buf.at[slot, pl.ds(0, z)], xs_ref.at[pl.ds(0, z)], sem.at[slot]))

    @pl.when(i > 0)
    def _():
        wait_tile(i - 1, 1 - slot)

    tile_copies(i, slot, False)

    @pl.when(i == n_tiles - 1)
    def _():
        wait_tile(i, slot)


def _dispatch(tab, h2, pos, wts, n_slots, tt, sb, bm):
    t, d = h2.shape
    n_tiles = t // tt
    return pl.pallas_call(
        functools.partial(_dispatch_kernel, tt=tt, n_tiles=n_tiles, bm=bm),
        grid_spec=pltpu.PrefetchScalarGridSpec(
            num_scalar_prefetch=1, grid=(n_tiles,),
            in_specs=[pl.BlockSpec((tt, d), lambda i, s: (i, 0)), pl.BlockSpec((2, tt), lambda i, s: (0, i)),
                      pl.BlockSpec((2, tt), lambda i, s: (0, i))],
            out_specs=pl.BlockSpec(memory_space=pl.ANY),
            scratch_shapes=[pltpu.VMEM((2, sb, d + LANES), BF16), pltpu.VMEM((bm // 2, d + LANES), BF16),
                            pltpu.SemaphoreType.DMA((3,))]),
        out_shape=jax.ShapeDtypeStruct((n_slots, d + LANES), BF16),
        compiler_params=_cparams(1), name="moe_dispatch",
    )(tab, h2, pos, wts)


def _ffn_kernel(blk_e_ref, blk_rows_ref, next_e_ref, nused_ref, x_ref, wg_hbm, wu_hbm, wd_hbm, o_ref,
                stage, wg_bf, wu_bf, wd_bf, sem, *, layer, parts):
    del nused_ref
    i = pl.program_id(0)
    e = blk_e_ref[i]
    new_expert = (i == 0) | (e != blk_e_ref[jnp.maximum(i - 1, 0)])

    def fetch(expert, wait):
        for j, src in enumerate((wg_hbm, wu_hbm, wd_hbm)):
            cp = pltpu.make_async_copy(src.at[layer, expert], stage.at[j], sem.at[j])
            cp.wait() if wait else cp.start()

    @pl.when(i == 0)
    def _():
        fetch(e, False)

    @pl.when(new_expert)
    def _():
        fetch(e, True)
        wg_bf[...] = stage[0].astype(BF16)
        wu_bf[...] = stage[1].astype(BF16)
        wd_bf[...] = stage[2].astype(BF16)

        @pl.when(next_e_ref[i] >= 0)
        def _():
            fetch(next_e_ref[i], False)

    d = wg_bf.shape[0]
    rows = blk_rows_ref[i]
    part = x_ref.shape[0] // parts
    spans = [slice(p * part, (p + 1) * part) for p in range(parts)]

    def gate_up(rs):
        x = x_ref[rs, :d].astype(BF16)
        return (jnp.dot(x, wg_bf[...], preferred_element_type=F32), jnp.dot(x, wu_bf[...], preferred_element_type=F32))

    def finish(rs, gate, up):
        act = (gate * jax.nn.sigmoid(gate) * up).astype(BF16)
        half = (x_ref.shape[1] - d) // 2
        weight = x_ref[rs, d:d + 1].astype(F32) + x_ref[rs, d + half:d + half + 1].astype(F32)
        o_ref[rs, :] = (jnp.dot(act, wd_bf[...], preferred_element_type=F32) * weight).astype(o_ref.dtype)

    all_parts = rows > (parts - 1) * part

    @pl.when(all_parts)
    def _():
        nxt = gate_up(spans[0])
        for p, rs in enumerate(spans):
            cur = nxt
            if p + 1 < parts:
                nxt = gate_up(spans[p + 1])
            finish(rs, *cur)

    for p, rs in enumerate(spans):
        @pl.when(jnp.logical_not(all_parts) & (rows > p * part))
        def _(rs=rs):
            finish(rs, *gate_up(rs))

        @pl.when(rows <= p * part)
        def _(rs=rs):
            o_ref[rs, :] = jnp.zeros((part, d), o_ref.dtype)


def _ffn(blk_e, blk_rows, next_e, n_used, xs, wg, wu, wd, layer, bm, parts):
    n_slots, xw = xs.shape
    d, ff = wg.shape[2:]
    assert d == ff
    xrow = lambda i, be, br, ne, nu: (jnp.minimum(i, nu[0] - 1), 0)
    hbm = pl.BlockSpec(memory_space=pl.ANY)
    return pl.pallas_call(
        functools.partial(_ffn_kernel, layer=layer, parts=parts),
        grid_spec=pltpu.PrefetchScalarGridSpec(
            num_scalar_prefetch=4, grid=(n_slots // bm,),
            in_specs=[pl.BlockSpec((bm, xw), xrow), hbm, hbm, hbm],
            out_specs=pl.BlockSpec((bm, d), lambda i, be, br, ne, nu: (i, 0)),
            scratch_shapes=[pltpu.VMEM((3, d, ff), F32), pltpu.VMEM((d, ff), BF16), pltpu.VMEM((d, ff), BF16),
                            pltpu.VMEM((ff, d), BF16), pltpu.SemaphoreType.DMA((3,))]),
        out_shape=jax.ShapeDtypeStruct((n_slots, d), BF16),
        compiler_params=_cparams(1), name="moe_ffn",
    )(blk_e, blk_rows, next_e, n_used, xs, wg, wu, wd)


def _combine_kernel(tab_ref, ys_ref, pos_ref, x1_ref, gate_ref, lng_ref, lnb_ref, o_ref, ybuf, sem, *, tt, n_tiles):
    i = pl.program_id(0)
    sb = ybuf.shape[1]
    ne = N_EXPERTS
    slot = i % 2

    def tile_copies(tile, slot, wait):
        for e in range(ne):
            off = tab_ref[tile * ne + e]
            n = tab_ref[(n_tiles + tile) * ne + e]
            src = tab_ref[(2 * n_tiles + tile) * ne + e]
            _run_copies(n, tt, lambda s, z: pltpu.make_async_copy(
                _rows(ys_ref, src + s, z), _rows(ybuf.at[slot], off + s, z), sem.at[slot]), wait)

    @pl.when(i == 0)
    def _():
        ybuf[...] = jnp.zeros_like(ybuf)
        tile_copies(0, 0, False)

    @pl.when(i + 1 < n_tiles)
    def _():
        tile_copies(i + 1, 1 - slot, False)

    total = tab_ref[3 * n_tiles * ne + 2 * ne + 1 + i]
    _wait_rows(total, tt * 2, lambda z: pltpu.make_async_copy(
        ys_ref.at[pl.ds(0, z)], ybuf.at[slot, pl.ds(0, z)], sem.at[slot]))
    scol = lax.broadcasted_iota(I32, (1, sb), 1)
    yb = ybuf[slot]

    def moe_rows(r0):
        pos = pos_ref[r0:r0 + LANES, :]
        unperm = (jnp.where(scol == pos[:, 0:1], 1.0, 0.0) + jnp.where(scol == pos[:, 1:2], 1.0, 0.0)).astype(BF16)
        return jnp.dot(unperm, yb, preferred_element_type=F32)

    starts = list(range(0, tt, LANES))
    nxt = moe_rows(starts[0])
    for j, r0 in enumerate(starts):
        f = nxt
        if j + 1 < len(starts):
            nxt = moe_rows(starts[j + 1])
        o_ref[r0:r0 + LANES, :] = _layer_norm(DEEPNORM_ALPHA * x1_ref[r0:r0 + LANES, :] + gate_ref[...] * f,
                                              lng_ref[...], lnb_ref[...])


def _combine(tab, ys, pos_t, x1, gate, ln_g, ln_b, rows_per_mod, tt, sb):
    t, d = x1.shape
    tiles_per_mod = rows_per_mod // tt
    nmod = gate.shape[0]
    return pl.pallas_call(
        functools.partial(_combine_kernel, tt=tt, n_tiles=t // tt),
        grid_spec=pltpu.PrefetchScalarGridSpec(
            num_scalar_prefetch=1, grid=(t // tt,),
            in_specs=[pl.BlockSpec(memory_space=pl.ANY),
                      pl.BlockSpec((tt, 2), lambda i, s: (i, 0)),
                      pl.BlockSpec((tt, d), lambda i, s: (i, 0)),
                      pl.BlockSpec((None, 1, d), lambda i, s: (i // tiles_per_mod, 0, 0)),
                      pl.BlockSpec((1, d), lambda i, s: (0, 0)), pl.BlockSpec((1, d), lambda i, s: (0, 0))],
            out_specs=pl.BlockSpec((tt, d), lambda i, s: (i, 0)),
            scratch_shapes=[pltpu.VMEM((2, sb, d), BF16), pltpu.SemaphoreType.DMA((2,))]),
        out_shape=jax.ShapeDtypeStruct((t, d), F32),
        compiler_params=_cparams(1), name="moe_combine",
    )(tab, ys, pos_t, x1, gate.reshape(nmod, 1, d), ln_g.reshape(1, d), ln_b.reshape(1, d))


def _moe(h2, logits_t, x1, gate, ln_g, ln_b, router_bias, wg, wu, wd, layer, rows_per_mod):
    t, d = h2.shape
    bm, tt, ne = MOE_BM, MOE_TILE, N_EXPERTS
    n_tiles = t // tt
    pad = ne * (ROW_ALIGN - 1)
    sb = -(-(2 * tt + pad) // LANES) * LANES
    pos, wts, tab3, cnt = _route(logits_t, router_bias, tr=tt)
    rows = cnt[:, 0].astype(I32)
    prows = (rows + bm - 1) // bm * bm
    pends = jnp.cumsum(prows)
    pstarts = pends - prows
    n_blk = -(-(2 * t + n_tiles * pad + ne * (bm - 1)) // bm)
    n_used = pends[-1] // bm
    blk_ids = jnp.minimum(jnp.arange(n_blk, dtype=I32), n_used - 1)
    blk_e = jnp.minimum(jnp.sum((blk_ids[:, None] * bm >= pends[None, :]).astype(I32), 1), ne - 1)
    eids = jnp.arange(ne, dtype=I32)
    of_blk = lambda per_expert: jnp.sum(jnp.where(blk_e[:, None] == eids[None, :], per_expert[None, :], 0), 1)
    raw_ids = jnp.arange(n_blk, dtype=I32)
    blk_rows = jnp.where(raw_ids < n_used, jnp.clip(of_blk(pstarts + rows) - raw_ids * bm, 0, bm), 0)
    later_used = (eids[None, :] > eids[:, None]) & (prows[None, :] > 0)
    next_used = jnp.min(jnp.where(later_used, eids[None, :], ne), 1)
    next_e = of_blk(jnp.where(next_used < ne, next_used, -1))
    tab3 = tab3[:, :, :, 0]
    tab = jnp.concatenate([tab3[:, 0].reshape(-1), tab3[:, 1].reshape(-1),
                           (tab3[:, 2] + pstarts[None, :]).reshape(-1), pstarts + rows, prows - rows,
                           n_used.reshape(1), jnp.sum(tab3[:, 1], 1)]).astype(I32)
    xs = _dispatch(tab, h2, pos, wts, n_blk * bm, tt, sb, bm)
    ys = _ffn(blk_e, blk_rows.astype(I32), next_e.astype(I32), n_used.reshape(1).astype(I32), xs, wg, wu, wd,
              layer, bm, parts=2)
    return _combine(tab, ys, pos.T, x1, gate, ln_g, ln_b, rows_per_mod, tt, sb)


def _filter_mlp_kernel(z_ref, w1_ref, w2_ref, w3_ref, b_ref, sf_ref, o_ref):
    hp = lax.Precision.HIGHEST
    b, sf = b_ref[...], sf_ref[...]
    h = jnp.sin(sf[0:1] * (jnp.dot(z_ref[...], w1_ref[...], precision=hp, preferred_element_type=F32) + b[0:1]))
    h = jnp.sin(sf[1:2] * (jnp.dot(h, w2_ref[...], precision=hp, preferred_element_type=F32) + b[1:2]))
    o_ref[...] = jnp.sin(sf[2:3] * (jnp.dot(h, w3_ref[...], precision=hp, preferred_element_type=F32) + b[2:3]))


def _filter_mlp(zp, w_in_p, w_hid, b, sf):
    n = zp.shape[0]
    return pl.pallas_call(
        _filter_mlp_kernel, out_shape=jax.ShapeDtypeStruct((n, FILTER_HID), F32),
        compiler_params=pltpu.CompilerParams(vmem_limit_bytes=VMEM_LIMIT), name="hyena_filter_mlp",
    )(zp, w_in_p, w_hid[0], w_hid[1], b, sf)


def _short_conv(u_ref, cw_ref, cb_ref):
    n = u_ref.shape[0]
    u = u_ref[...].astype(F32)
    row = lax.broadcasted_iota(I32, (n, 1), 0)
    prev = jnp.where(row == 0, 0.0, pltpu.roll(u, 1, 0))
    nxt = jnp.where(row == n - 1, 0.0, pltpu.roll(u, n - 1, 0))
    cw = cw_ref[...]
    return prev * cw[0:1] + u * cw[1:2] + nxt * cw[2:3] + cb_ref[...]


def _lag_spec_kernel(hid_ref, wf_ref, wb_ref, dec_ref, ph_ref, h_ref, fw_scr, bw_scr, *, s, nb):
    lag = pl.program_id(2)
    n = hid_ref.shape[0]

    @pl.when(lag == 0)
    def _():
        def split(v):
            hi = v.astype(BF16)
            return hi, (v - hi.astype(F32)).astype(BF16)

        def precise_dot(a_hi, a_lo, w):
            w_hi, w_lo = split(w)
            return (jnp.dot(a_hi, w_hi, preferred_element_type=F32) + jnp.dot(a_lo, w_hi, preferred_element_type=F32)
                    + jnp.dot(a_hi, w_lo, preferred_element_type=F32))

        dec = dec_ref[...]
        hid_hi, hid_lo = split(hid_ref[...])
        row = lax.broadcasted_iota(I32, (n, 1), 0)
        fw = precise_dot(hid_hi, hid_lo, wf_ref[...]) * dec
        bw = precise_dot(hid_hi, hid_lo, wb_ref[...]) * dec
        fw_scr[...] = fw.astype(BF16)
        bw_scr[...] = jnp.where(row == 0, 0.0, bw).astype(BF16)

    def piece(ph, scr, j):
        return jnp.dot(ph_ref[ph], scr[j * s:(j + 1) * s, :], preferred_element_type=F32)

    for idx in range(2 * nb - 1):
        m = idx - (nb - 1)

        @pl.when(lag == idx)
        def _(m=m):
            if m >= 1:
                h = piece(0, fw_scr, m) + piece(1, fw_scr, m - 1)
            elif m == 0:
                h = piece(0, fw_scr, 0) + piece(2, bw_scr, 0)
            else:
                h = piece(2, bw_scr, -m) + piece(3, bw_scr, -m - 1)
            h_ref[...] = h.astype(h_ref.dtype)


def _lag_spec(hid, w_out4, decay, phases, s, tc):
    n = hid.shape[0]
    d = D_MODEL
    nb = n // s
    return pl.pallas_call(
        functools.partial(_lag_spec_kernel, s=s, nb=nb), grid=(2, d // tc, 2 * nb - 1),
        in_specs=[pl.BlockSpec((n, FILTER_HID), lambda o, c, l: (0, 0)),
                  pl.BlockSpec((None, None, FILTER_HID, tc), lambda o, c, l: (0, o, 0, c)),
                  pl.BlockSpec((None, None, FILTER_HID, tc), lambda o, c, l: (1, o, 0, c)),
                  pl.BlockSpec((n, tc), lambda o, c, l: (0, c)),
                  pl.BlockSpec((4, 2 * s, s), lambda o, c, l: (0, 0, 0))],
        out_specs=pl.BlockSpec((None, None, 2 * s, tc), lambda o, c, l: (o, l, 0, c)),
        out_shape=jax.ShapeDtypeStruct((2, 2 * nb - 1, 2 * s, d), BF16),
        scratch_shapes=[pltpu.VMEM((n, tc), BF16), pltpu.VMEM((n, tc), BF16)],
        compiler_params=_cparams(3), name="hyena_lag_spec",
    )(hid, w_out4, w_out4, decay, phases)


def _part_conv_kernel(uv_ref, ux1_ref, ux2_ref, cwv_ref, cwx1_ref, cwx2_ref, cbv_ref, cbx1_ref, cbx2_ref,
                      f_ref, finv_ref, h_ref, sk_ref, o_ref, src_scr, gate_scr, zb_scr, zall, ybuf, *, s, nb, rc):
    src_scr[...] = _short_conv(uv_ref, cwv_ref, cbv_ref)
    gates = ((ux1_ref, cwx1_ref, cbx1_ref), (ux2_ref, cwx2_ref, cbx2_ref))
    for o, gate in enumerate(gates):
        zb_scr[...] = src_scr[...].astype(BF16)
        gate_scr[...] = _short_conv(*gate)
        for j in range(nb):
            zall[j] = jnp.dot(f_ref[...], zb_scr[j * s:(j + 1) * s, :], preferred_element_type=F32).astype(BF16)
        sk = sk_ref[o]
        top = lax.broadcasted_iota(I32, (rc, 1), 0) == 0
        for i in range(nb):
            pairs = [(j, i - j + nb - 1) for j in range(nb)]
            dc = nyq = None
            for j, m in pairs:
                t0 = zall[j, 0:rc, :].astype(F32)[0:1] * h_ref[o, m, 0:rc, :].astype(F32)[0:1]
                t1 = zall[j, s:s + rc, :].astype(F32)[0:1] * h_ref[o, m, s:s + rc, :].astype(F32)[0:1]
                dc, nyq = (t0, t1) if dc is None else (dc + t0, nyq + t1)
            for r0 in range(0, s, rc):
                re, im = slice(r0, r0 + rc), slice(s + r0, s + r0 + rc)
                yr = yi = None
                for j, m in pairs:
                    zr, zi, hr, g = zall[j, re, :], zall[j, im, :], h_ref[o, m, re, :], h_ref[o, m, im, :]
                    tr, ti = zr * hr - zi * g, zr * g + zi * hr
                    yr, yi = (tr, ti) if yr is None else (yr + tr, yi + ti)
                if r0 == 0:
                    yr, yi = jnp.where(top, dc.astype(BF16), yr), jnp.where(top, nyq.astype(BF16), yi)
                ybuf[re, :] = yr
                ybuf[im, :] = yi
            y = jnp.dot(finv_ref[...], ybuf[...], preferred_element_type=F32)
            rows = slice(i * s, (i + 1) * s)
            zn = gate_scr[rows, :] * (y + src_scr[rows, :] * sk)
            if o + 1 < len(gates):
                src_scr[rows, :] = zn
            else:
                o_ref[rows, :] = zn.astype(o_ref.dtype)


def _part_conv(u, conv_w, conv_b, f_tab, finv_tab, spec, skip, s, tc):
    b, n, _ = u.shape
    d = D_MODEL
    ncb = d // tc
    nb = n // s
    col = lambda part: (lambda c, bi: (bi, 0, part * ncb + c))
    wcol = lambda part: (lambda c, bi: (0, part * ncb + c))
    const = lambda c, bi: (0, 0)
    in_specs = ([pl.BlockSpec((None, n, tc), col(p)) for p in range(3)]
                + [pl.BlockSpec((3, tc), wcol(p)) for p in range(3)]
                + [pl.BlockSpec((1, tc), wcol(p)) for p in range(3)]
                + [pl.BlockSpec((2 * s, s), const), pl.BlockSpec((s, 2 * s), const),
                   pl.BlockSpec((2, 2 * nb - 1, 2 * s, tc), lambda c, bi: (0, 0, 0, c), pipeline_mode=pl.Buffered(1)),
                   pl.BlockSpec((2, 1, tc), lambda c, bi: (0, 0, c))])
    return pl.pallas_call(
        functools.partial(_part_conv_kernel, s=s, nb=nb, rc=32), grid=(ncb, b),
        in_specs=in_specs,
        out_specs=pl.BlockSpec((None, n, tc), lambda c, bi: (bi, 0, c)),
        out_shape=jax.ShapeDtypeStruct((b, n, d), BF16),
        scratch_shapes=[pltpu.VMEM((n, tc), F32), pltpu.VMEM((n, tc), F32), pltpu.VMEM((n, tc), BF16),
                        pltpu.VMEM((nb, 2 * s, tc), BF16), pltpu.VMEM((2 * s, tc), BF16)],
        compiler_params=_cparams(2), name="hyena_part_conv",
    )(u, u, u, conv_w, conv_w, conv_w, conv_b, conv_b, conv_b, f_tab, finv_tab, spec, skip.reshape(2, 1, d))


def kernel(x, c, ctx, c_ctx, mod_w, mod_b, ln_g, ln_b, attn_w_in, attn_lambda, attn_subln_g, attn_sink, attn_w_out,
           hy_w_in, hy_conv_w, hy_conv_b, hy_ffn_w_in, hy_ffn_w_hid, hy_ffn_b, hy_sin_freq, hy_ffn_w_out, hy_skip,
           hy_w_out, router_w, router_bias, exp_w_gate, exp_w_up, exp_w_down):
    b, n, d = x.shape
    nc = ctx.shape[1]
    t = b * n
    assert d == D_MODEL and b + 1 <= MOD_ROWS and n % 512 == 0 and nc % 256 == 0

    c_rows = jnp.zeros((MOD_ROWS, d), F32).at[:b].set(c).at[b].set(c_ctx)
    mods = _mods(c_rows, mod_w, mod_b)
    router_wt = jnp.pad(router_w, ((0, 0), (0, LANES - N_EXPERTS)))
    x2d = x.reshape(t, d)

    sh1, sc1, g1, sh2, sc2, g2 = jnp.split(mods[0], 6, axis=-1)
    cos, sin = _rope_tables(n)
    w_in = attn_w_in[0].astype(BF16)
    q_groups = tuple(range(Q_W // LANES))
    rope_groups = q_groups + tuple(range(Q_W // LANES, (Q_W + DIFF_QK_W) // LANES)) + (
        (Q_W + DIFF_QK_W + DIFF_VW) // LANES,)
    q_scales = tuple((g, HEAD_DIM ** -0.5 * LOG2_E) for g in q_groups)
    proj = _modmm(x2d, sc1[:b], sh1[:b], w_in, rows_per_mod=n, tm=1024,
                  rope=(jnp.asarray(cos), jnp.asarray(sin), rope_groups, q_scales), name="attn_in_proj")
    proj_c = _modmm(ctx.reshape(b * nc, d), sc1[b:b + 1], sh1[b:b + 1], w_in[:, Q_W:], rows_per_mod=b * nc, tm=256,
                    name="ctx_in_proj")
    proj = proj.reshape(b, n, ATTN_PROJ_W)
    proj_c = proj_c.reshape(b, nc, KV_W)
    lam_init = 0.8 - 0.6 * math.exp(-0.3 * 0)
    oa = _diff_attn(proj, proj_c, attn_lambda[0], attn_subln_g[0], lam_init, tq=1024, sub=128)
    ow = _win_attn(proj, proj_c, attn_sink[0], step=512, tq=128)
    w_out = attn_w_out[0].astype(BF16)
    x1, h2, lgt = _proj_ln([oa.reshape(t, DIFF_VW), ow.reshape(t, WIN_Q_W)], [w_out[:DIFF_VW], w_out[DIFF_VW:]],
                           x2d, g1[:b], ln_g[0, 0], ln_b[0, 0], sc2[:b], sh2[:b], router_wt, rows_per_mod=n, tm=1024,
                           name="attn_out_proj_ln")
    x2 = _moe(h2, lgt, x1, g2[:b], ln_g[0, 1], ln_b[0, 1], router_bias, exp_w_gate, exp_w_up, exp_w_down, layer=0,
              rows_per_mod=n)

    sh1, sc1, g1, sh2, sc2, g2 = jnp.split(mods[1], 6, axis=-1)
    u = _modmm(x2, sc1[:b], sh1[:b], hy_w_in[0].astype(BF16), rows_per_mod=n, tm=1024, name="hyena_in_proj")
    part, tc = HYENA_PART, 256
    phases_np, finv_np = _partition_tables(part)
    phases, finv_tab = jnp.asarray(phases_np).astype(BF16), jnp.asarray(finv_np).astype(BF16)
    zp, decay = _filter_tables(n)
    w_in_p = jnp.zeros((FILTER_HID, FILTER_HID), F32).at[:FILTER_EMB].set(hy_ffn_w_in[0])
    hid = _filter_mlp(jnp.asarray(zp), w_in_p, hy_ffn_w_hid[0], hy_ffn_b[0], hy_sin_freq[0])
    spec = _lag_spec(hid, hy_ffn_w_out[0].reshape(FILTER_HID, 2, 2, d).transpose(1, 2, 0, 3), jnp.asarray(decay),
                     phases, part, tc=512)
    u3 = u.reshape(b, n, 3 * d)
    conv_b = hy_conv_b[0].reshape(1, 3 * d)
    z2 = _part_conv(u3, hy_conv_w[0], conv_b, phases[0], finv_tab, spec, hy_skip[0], part, tc)
    x3, h2, lgt = _proj_ln([z2.reshape(t, d)], [hy_w_out[0].astype(BF16)], x2, g1[:b], ln_g[1, 0], ln_b[1, 0],
                           sc2[:b], sh2[:b], router_wt, rows_per_mod=n, tm=1024, name="hyena_out_proj_ln")
    x4 = _moe(h2, lgt, x3, g2[:b], ln_g[1, 1], ln_b[1, 1], router_bias, exp_w_gate, exp_w_up, exp_w_down, layer=1,
              rows_per_mod=n)
    return x4.reshape(b, n, d)
```

```python
import functools
import math

import jax
import jax.numpy as jnp
import numpy as np
from jax import lax
from jax.experimental import pallas as pl
from jax.experimental.pallas import tpu as pltpu

F32 = jnp.float32
BF16 = jnp.bfloat16
I32 = jnp.int32

D_MODEL = 1024
DEPTH = 2
GRID_W = 64
HEAD_DIM = 64
DIFF_HEADS = 4
WIN_Q_HEADS = 8
WIN_KV_HEADS = 2
WINDOW = 128
WIN_BLOCK = 128
ROPE_BASE = 10000.0
DIFF_QK_W = DIFF_HEADS * 2 * HEAD_DIM
DIFF_VW = DIFF_HEADS * 2 * HEAD_DIM
WIN_Q_W = WIN_Q_HEADS * HEAD_DIM
WIN_KV_W = WIN_KV_HEADS * HEAD_DIM
Q_W = DIFF_QK_W + WIN_Q_W
KV_W = DIFF_QK_W + DIFF_VW + 2 * WIN_KV_W
ATTN_PROJ_W = Q_W + KV_W
FILTER_EMB = 33
FILTER_HID = 64
DECAY_TARGET = 1e-2
FAST_DECAY_PCT = 0.3
SLOW_DECAY_PCT = 1.5
N_EXPERTS = 16
N_GROUPS = 4
EXPERTS_PER_GROUP = N_EXPERTS // N_GROUPS
EXPERT_FF = 1024
LN_EPS = 1e-5
DEEPNORM_ALPHA = (2 * DEPTH) ** 0.25
NEG_INF = -1e30
LOG2_E = 1.4426950408889634

LANES = 128
MOD_ROWS = 16
VMEM_LIMIT = 60 * 1024 * 1024
MOE_BM = 512
MOE_TILE = 512
HYENA_PART = 512
ROW_ALIGN = 8
NT_DIMS = (((1,), (1,)), ((), ()))


def _cparams(n_axes):
    return pltpu.CompilerParams(dimension_semantics=("arbitrary",) * n_axes, vmem_limit_bytes=VMEM_LIMIT)


@functools.lru_cache(maxsize=None)
def _rope_tables(n):
    rows = n // GRID_W
    r, col = np.meshgrid(np.arange(rows, dtype=np.float32), np.arange(GRID_W, dtype=np.float32), indexing="ij")
    axis_dim = HEAD_DIM // 2
    inv_freq = (ROPE_BASE ** (-np.arange(0, axis_dim, 2, dtype=np.float32) / axis_dim)).astype(np.float32)
    ang = np.concatenate([r.reshape(-1, 1) * inv_freq, col.reshape(-1, 1) * inv_freq], -1)
    ang = np.concatenate([ang, ang], -1).astype(np.float32)
    cos, sin = np.cos(ang), np.sin(ang)
    half = np.arange(HEAD_DIM) < HEAD_DIM // 2
    sin_signed = np.where(half[None, :], -sin, sin)
    reps = LANES // HEAD_DIM
    return (np.tile(cos, (1, reps)).astype(np.float32), np.tile(sin_signed, (1, reps)).astype(np.float32))


@functools.lru_cache(maxsize=None)
def _filter_tables(n):
    t = np.linspace(0.0, 1.0, n, dtype=np.float32)[:, None]
    bands = (FILTER_EMB - 1) // 2
    w = (2.0 * math.pi * np.arange(n, dtype=np.float32)[:, None] / n).astype(np.float32)
    fr = np.linspace(1e-4, bands - 1, bands, dtype=np.float32)[None, :]
    z = np.concatenate([t, np.cos(fr * w), -np.sin(fr * w)], -1).astype(np.float32)
    zp = np.zeros((n, FILTER_HID), np.float32)
    zp[:, :FILTER_EMB] = z
    deltas = np.abs(np.linspace(math.log(DECAY_TARGET) / SLOW_DECAY_PCT, math.log(DECAY_TARGET) / FAST_DECAY_PCT,
                                D_MODEL, dtype=np.float32))
    decay = np.exp(-t * deltas[None, :]).astype(np.float32)
    return zp, decay


@functools.lru_cache(maxsize=None)
def _partition_tables(s):
    big = 2 * s
    k = np.arange(s, dtype=np.int64)[:, None]
    r = np.arange(s, dtype=np.int64)[None, :]

    def phase(pr, sin_sign, drop_r0):
        ang = ((k * pr) % big).astype(np.float64) * (2.0 * math.pi / big)
        c, sn = np.cos(ang), sin_sign * np.sin(ang)
        sn[0, :] = 1.0 - 2.0 * (pr[0] & 1)
        if drop_r0:
            c[:, 0] = 0.0
            sn[:, 0] = 0.0
        return np.concatenate([c, sn], 0)

    phases = np.stack([phase(r, 1.0, False), phase(s - r, -1.0, True), phase(r, -1.0, False), phase(s - r, 1.0, True)])
    ang = ((r.T * k.T) % big).astype(np.float64) * (2.0 * math.pi / big)
    ci = np.cos(ang) * (2.0 / big)
    ci[:, 0] = 1.0 / big
    si = np.sin(ang) * (2.0 / big)
    si[:, 0] = (1.0 - 2.0 * (np.arange(s) & 1)) / big
    return phases.astype(np.float32), np.concatenate([ci, si], 1).astype(np.float32)


def _mods_kernel(c_ref, w_ref, b_ref, o_ref):
    c = c_ref[...]
    a = (c * jax.nn.sigmoid(c)).astype(BF16)
    o_ref[...] = jnp.dot(a, w_ref[...].astype(BF16), preferred_element_type=F32) + b_ref[...]


def _mods(c_rows, mod_w, mod_b):
    d = D_MODEL
    tn = 1536
    return pl.pallas_call(
        _mods_kernel,
        grid=(DEPTH, 6 * d // tn),
        in_specs=[pl.BlockSpec((MOD_ROWS, d), lambda l, j: (0, 0)),
                  pl.BlockSpec((None, d, tn), lambda l, j: (l, 0, j)),
                  pl.BlockSpec((None, 1, tn), lambda l, j: (l, 0, j))],
        out_specs=pl.BlockSpec((None, MOD_ROWS, tn), lambda l, j: (l, 0, j)),
        out_shape=jax.ShapeDtypeStruct((DEPTH, MOD_ROWS, 6 * d), F32),
        compiler_params=_cparams(2), name="mods",
    )(c_rows, mod_w, mod_b.reshape(DEPTH, 1, 6 * d))


def _modmm_kernel(*refs, n_groups, rope_groups, scaled_groups, chunk):
    if rope_groups:
        x_ref, sc_ref, sh_ref, w_ref, cos_ref, sin_ref, o_ref = refs
        cos, sin = cos_ref[...], sin_ref[...]
        lane = lax.broadcasted_iota(I32, (1, LANES), 1)
        first_half = (lane % HEAD_DIM) < HEAD_DIM // 2
    else:
        x_ref, sc_ref, sh_ref, w_ref, o_ref = refs
    h = (x_ref[...] * (1.0 + sc_ref[...]) + sh_ref[...]).astype(BF16)
    scales = dict(scaled_groups)
    gpc = chunk // LANES
    for c in range(n_groups // gpc):
        acc = jnp.dot(h, w_ref[:, c * chunk:(c + 1) * chunk], preferred_element_type=F32)
        for j in range(gpc):
            g = c * gpc + j
            blk = acc[:, j * LANES:(j + 1) * LANES]
            if g in rope_groups:
                rot = jnp.where(first_half, pltpu.roll(blk, LANES - HEAD_DIM // 2, 1), pltpu.roll(blk, HEAD_DIM // 2, 1))
                blk = blk * cos + rot * sin
                if g in scales:
                    blk = blk * scales[g]
            o_ref[:, g * LANES:(g + 1) * LANES] = blk.astype(o_ref.dtype)


def _modmm(x2d, sc, sh, w, rows_per_mod, tm, rope=None, name="modmm"):
    t, d = x2d.shape
    n = w.shape[1]
    tiles_per_mod = rows_per_mod // tm
    nmod = sc.shape[0]
    in_specs = [pl.BlockSpec((tm, d), lambda i: (i, 0)),
                pl.BlockSpec((None, 1, d), lambda i: (i // tiles_per_mod, 0, 0)),
                pl.BlockSpec((None, 1, d), lambda i: (i // tiles_per_mod, 0, 0)),
                pl.BlockSpec((d, n), lambda i: (0, 0))]
    args = [x2d, sc.reshape(nmod, 1, d), sh.reshape(nmod, 1, d), w]
    rope_groups, scaled_groups = (), ()
    if rope is not None:
        cos, sin, rope_groups, scaled_groups = rope
        in_specs += [pl.BlockSpec((tm, LANES), lambda i: (i % tiles_per_mod, 0)),
                     pl.BlockSpec((tm, LANES), lambda i: (i % tiles_per_mod, 0))]
        args += [cos, sin]
    kern = functools.partial(_modmm_kernel, n_groups=n // LANES, rope_groups=tuple(rope_groups),
                             scaled_groups=tuple(scaled_groups), chunk=256)
    return pl.pallas_call(
        kern, grid=(t // tm,), in_specs=in_specs,
        out_specs=pl.BlockSpec((tm, n), lambda i: (i, 0)),
        out_shape=jax.ShapeDtypeStruct((t, n), BF16),
        compiler_params=_cparams(1), name=name,
    )(*args)


def _diff_attn_kernel(q_ref, k_ref, v_ref, kc_ref, vc_ref, lam_ref, g_ref, o_ref, *, lam_init, sub):
    lane = lax.broadcasted_iota(I32, (1, LANES), 1)
    k, kc = k_ref[...], kc_ref[...]
    v, vc = v_ref[...], vc_ref[...]

    def scores(r0):
        q = q_ref[r0:r0 + sub, :]
        zero = jnp.zeros_like(q)
        out = []
        for qm in (jnp.where(lane < HEAD_DIM, q, zero), jnp.where(lane >= HEAD_DIM, q, zero)):
            out.append((lax.dot_general(qm, k, NT_DIMS, preferred_element_type=F32),
                        lax.dot_general(qm, kc, NT_DIMS, preferred_element_type=F32)))
        return out

    def probs(sl, sc):
        m = jnp.maximum(jnp.max(sl, -1, keepdims=True), jnp.max(sc, -1, keepdims=True))
        pl_, pc = jnp.exp2(sl - m), jnp.exp2(sc - m)
        den = jnp.sum(pl_, -1, keepdims=True) + jnp.sum(pc, -1, keepdims=True)
        return pl_, pc, 1.0 / den

    lv = lam_ref[...]
    lam = (jnp.exp(jnp.sum(lv[0:1] * lv[1:2], keepdims=True)) - jnp.exp(jnp.sum(lv[2:3] * lv[3:4], keepdims=True))
           + lam_init)
    starts = list(range(0, q_ref.shape[0], sub))
    nxt = scores(starts[0])
    for i, r0 in enumerate(starts):
        cur = nxt
        if i + 1 < len(starts):
            nxt = scores(starts[i + 1])
        p1l, p1c, r1 = probs(*cur[0])
        p2l, p2c, r2 = probs(*cur[1])
        w2 = lam * r2
        al = (p1l * r1 - p2l * w2).astype(BF16)
        ac = (p1c * r1 - p2c * w2).astype(BF16)
        o = jnp.dot(al, v, preferred_element_type=F32) + jnp.dot(ac, vc, preferred_element_type=F32)
        ms = jnp.mean(o * o, -1, keepdims=True)
        o_ref[r0:r0 + sub, :] = (o * lax.rsqrt(ms + LN_EPS) * g_ref[...] * (1.0 - lam_init)).astype(o_ref.dtype)


def _diff_attn(proj, proj_c, lam_vec, subln_g, lam_init, tq, sub):
    b, n, _ = proj.shape
    nc = proj_c.shape[1]
    kcol = Q_W // LANES
    vcol = (Q_W + DIFF_QK_W) // LANES
    vccol = DIFF_QK_W // LANES
    kern = functools.partial(_diff_attn_kernel, lam_init=lam_init, sub=sub)
    return pl.pallas_call(
        kern, grid=(b, DIFF_HEADS, n // tq),
        in_specs=[pl.BlockSpec((None, tq, LANES), lambda bi, h, i: (bi, i, h)),
                  pl.BlockSpec((None, n, LANES), lambda bi, h, i: (bi, 0, kcol + h)),
                  pl.BlockSpec((None, n, LANES), lambda bi, h, i: (bi, 0, vcol + h)),
                  pl.BlockSpec((None, nc, LANES), lambda bi, h, i: (bi, 0, h)),
                  pl.BlockSpec((None, nc, LANES), lambda bi, h, i: (bi, 0, vccol + h)),
                  pl.BlockSpec((4, HEAD_DIM), lambda bi, h, i: (0, 0)),
                  pl.BlockSpec((1, LANES), lambda bi, h, i: (0, 0))],
        out_specs=pl.BlockSpec((None, tq, LANES), lambda bi, h, i: (bi, i, h)),
        out_shape=jax.ShapeDtypeStruct((b, n, DIFF_VW), BF16),
        compiler_params=_cparams(3), name="diff_attn",
    )(proj, proj, proj, proj_c, proj_c, lam_vec, subln_g.reshape(1, LANES))


def _win_attn_kernel(sink_ref, q_ref, k_ref, v_ref, kc_ref, vc_ref, o_ref, *, seq, tq):
    for r0 in range(0, q_ref.shape[0], tq):
        _win_attn_rows(sink_ref, q_ref, k_ref, v_ref, kc_ref, vc_ref, o_ref, r0, seq=seq, tq=tq)


def _win_attn_rows(sink_ref, q_ref, k_ref, v_ref, kc_ref, vc_ref, o_ref, r0, *, seq, tq):
    first = pl.program_id(1) * q_ref.shape[0] + r0
    gq = WIN_Q_HEADS // WIN_KV_HEADS
    kw = tq + 2 * WINDOW
    start = pl.multiple_of(jnp.clip(first - WINDOW, 0, seq - kw), WINDOW)
    k_win, v_win = k_ref[pl.ds(start, kw), :], v_ref[pl.ds(start, kw), :]
    kc, vc = kc_ref[...], vc_ref[...]
    lane = lax.broadcasted_iota(I32, (1, LANES), 1)
    row = lax.broadcasted_iota(I32, (gq * tq, 1), 0)
    q_abs = first + row % tq
    k_abs = start + lax.broadcasted_iota(I32, (1, kw), 1)
    allowed = jnp.abs(q_abs - k_abs) <= WINDOW
    head = row // tq
    q = q_ref[r0:r0 + tq, :].astype(F32)
    for g in range(WIN_KV_HEADS):
        in_g = (lane // HEAD_DIM) == g
        parts = []
        for j in range(gq):
            hq = g * gq + j
            x = q[:, (hq // 2) * LANES:(hq // 2 + 1) * LANES]
            if hq % 2 != g:
                x = pltpu.roll(x, HEAD_DIM, 1)
            parts.append(jnp.where(in_g, x, 0.0))
        qs = jnp.concatenate(parts, 0).astype(BF16)
        s_loc = lax.dot_general(qs, k_win, NT_DIMS, preferred_element_type=F32)
        s_ctx = lax.dot_general(qs, kc, NT_DIMS, preferred_element_type=F32)
        s_loc = jnp.where(allowed, s_loc, NEG_INF)
        sk = jnp.zeros((gq * tq, 1), F32)
        for j in range(gq):
            sk = jnp.where(head == j, sink_ref[g * gq + j] * LOG2_E, sk)
        m = jnp.maximum(jnp.maximum(jnp.max(s_loc, -1, keepdims=True), jnp.max(s_ctx, -1, keepdims=True)), sk)
        p_loc, p_ctx = jnp.exp2(s_loc - m), jnp.exp2(s_ctx - m)
        den = jnp.sum(p_loc, -1, keepdims=True) + jnp.sum(p_ctx, -1, keepdims=True) + jnp.exp2(sk - m)
        o = (jnp.dot(p_loc.astype(BF16), v_win, preferred_element_type=F32)
             + jnp.dot(p_ctx.astype(BF16), vc, preferred_element_type=F32)) * (1.0 / den)
        for cb in range(gq // 2):
            pair = []
            for half in range(2):
                piece = o[(2 * cb + half) * tq:(2 * cb + half + 1) * tq]
                pair.append(piece if half == g else pltpu.roll(piece, HEAD_DIM, 1))
            col = (g * gq // 2 + cb) * LANES
            o_ref[r0:r0 + tq, col:col + LANES] = jnp.where(lane < HEAD_DIM, pair[0], pair[1]).astype(o_ref.dtype)


def _win_attn(proj, proj_c, sink, step, tq):
    b, n, _ = proj.shape
    nc = proj_c.shape[1]
    qcol = DIFF_QK_W // WIN_Q_W
    kcol = (Q_W + DIFF_QK_W + DIFF_VW) // LANES
    kccol = (DIFF_QK_W + DIFF_VW) // LANES
    kern = functools.partial(_win_attn_kernel, seq=n, tq=tq)
    return pl.pallas_call(
        kern, grid=(b, n // step),
        in_specs=[pl.BlockSpec(memory_space=pltpu.SMEM),
                  pl.BlockSpec((None, step, WIN_Q_W), lambda bi, i: (bi, i, qcol)),
                  pl.BlockSpec((None, n, LANES), lambda bi, i: (bi, 0, kcol)),
                  pl.BlockSpec((None, n, LANES), lambda bi, i: (bi, 0, kcol + 1)),
                  pl.BlockSpec((None, nc, LANES), lambda bi, i: (bi, 0, kccol)),
                  pl.BlockSpec((None, nc, LANES), lambda bi, i: (bi, 0, kccol + 1))],
        out_specs=pl.BlockSpec((None, step, WIN_Q_W), lambda bi, i: (bi, i, 0)),
        out_shape=jax.ShapeDtypeStruct((b, n, WIN_Q_W), BF16),
        compiler_params=_cparams(2), name="win_attn",
    )(sink, proj, proj, proj, proj_c, proj_c)


def _layer_norm(r, g, b):
    mu = jnp.mean(r, -1, keepdims=True)
    dlt = r - mu
    var = jnp.mean(dlt * dlt, -1, keepdims=True)
    return dlt * lax.rsqrt(var + LN_EPS) * g + b


def _proj_ln_kernel(*refs, n_in, sub):
    a_refs = refs[:n_in]
    w_refs = refs[n_in:2 * n_in]
    x_ref, gate_ref, lng_ref, lnb_ref, sc_ref, sh_ref, rw_ref, x1_ref, h2_ref, lg_ref = refs[2 * n_in:]

    def split(v):
        hi = v.astype(BF16)
        return hi, (v - hi.astype(F32)).astype(BF16)

    def project(r0):
        y = jnp.dot(a_refs[0][r0:r0 + sub, :], w_refs[0][...], preferred_element_type=F32)
        for a_ref, w_ref in zip(a_refs[1:], w_refs[1:]):
            y = y + jnp.dot(a_ref[r0:r0 + sub, :], w_ref[...], preferred_element_type=F32)
        return y

    rw_hi, rw_lo = split(rw_ref[...])
    rw_both = jnp.concatenate([rw_hi, rw_lo], 1)
    starts = list(range(0, x_ref.shape[0], sub))
    nxt = project(starts[0])
    for i, r0 in enumerate(starts):
        y = nxt
        if i + 1 < len(starts):
            nxt = project(starts[i + 1])
        xn = _layer_norm(DEEPNORM_ALPHA * x_ref[r0:r0 + sub, :] + gate_ref[...] * y, lng_ref[...], lnb_ref[...])
        x1_ref[r0:r0 + sub, :] = xn
        h2 = xn * (1.0 + sc_ref[...]) + sh_ref[...]
        h2_ref[r0:r0 + sub, :] = h2.astype(h2_ref.dtype)
        h_hi, h_lo = split(h2)
        hh = jnp.dot(h_hi, rw_both, preferred_element_type=F32)
        lg = hh[:, :LANES] + hh[:, LANES:] + jnp.dot(h_lo, rw_hi, preferred_element_type=F32)
        lg_ref[:, r0:r0 + sub] = lg.T[:N_EXPERTS, :]


def _proj_ln(acts, ws, x2d, gate, ln_g, ln_b, sc, sh, router_wt, rows_per_mod, tm, name):
    t, d = x2d.shape
    n_in = len(acts)
    tiles_per_mod = rows_per_mod // tm
    nmod = gate.shape[0]
    row = lambda i: (i, 0)
    full = lambda i: (0, 0)
    mod = lambda i: (i // tiles_per_mod, 0, 0)
    in_specs = ([pl.BlockSpec((tm, a.shape[1]), row) for a in acts]
                + [pl.BlockSpec(w.shape, full) for w in ws]
                + [pl.BlockSpec((tm, d), row), pl.BlockSpec((None, 1, d), mod),
                   pl.BlockSpec((1, d), full), pl.BlockSpec((1, d), full),
                   pl.BlockSpec((None, 1, d), mod), pl.BlockSpec((None, 1, d), mod),
                   pl.BlockSpec((d, LANES), full)])
    return pl.pallas_call(
        functools.partial(_proj_ln_kernel, n_in=n_in, sub=LANES), grid=(t // tm,), in_specs=in_specs,
        out_specs=[pl.BlockSpec((tm, d), row), pl.BlockSpec((tm, d), row),
                   pl.BlockSpec((N_EXPERTS, tm), lambda i: (0, i))],
        out_shape=[jax.ShapeDtypeStruct((t, d), F32), jax.ShapeDtypeStruct((t, d), BF16),
                   jax.ShapeDtypeStruct((N_EXPERTS, t), F32)],
        compiler_params=_cparams(1), name=name,
    )(*acts, *ws, x2d, gate.reshape(nmod, 1, d), ln_g.reshape(1, d), ln_b.reshape(1, d),
      sc.reshape(nmod, 1, d), sh.reshape(nmod, 1, d), router_wt)


def _first_argmax(vals):
    idx = jnp.zeros(vals[0].shape, I32)
    best = vals[0]
    for j in range(1, len(vals)):
        upd = vals[j] > best
        idx = jnp.where(upd, j, idx)
        best = jnp.where(upd, vals[j], best)
    return idx, best


def _route_kernel(lg_ref, bias_ref, pos_ref, w_ref, tab_ref, cnt_ref, carry_ref, off_ref, *, tr):
    @pl.when(pl.program_id(0) == 0)
    def _():
        carry_ref[...] = jnp.zeros_like(carry_ref)

    lg = lg_ref[...]
    ex = jnp.exp(lg - jnp.max(lg, 0, keepdims=True))
    scores = ex / jnp.sum(ex, 0, keepdims=True)
    sel = scores + bias_ref[...]
    rows = [sel[e:e + 1] for e in range(N_EXPERTS)]
    group_scores = []
    for g in range(N_GROUPS):
        r = rows[g * EXPERTS_PER_GROUP:(g + 1) * EXPERTS_PER_GROUP]
        best = None
        for i in range(EXPERTS_PER_GROUP):
            for j in range(i + 1, EXPERTS_PER_GROUP):
                s = r[i] + r[j]
                best = s if best is None else jnp.maximum(best, s)
        group_scores.append(best)
    grp, _ = _first_argmax(group_scores)
    vals = []
    for j in range(EXPERTS_PER_GROUP):
        v = rows[(N_GROUPS - 1) * EXPERTS_PER_GROUP + j]
        for g in range(N_GROUPS - 2, -1, -1):
            v = jnp.where(grp == g, rows[g * EXPERTS_PER_GROUP + j], v)
        vals.append(v)
    i0, _ = _first_argmax(vals)
    i1, _ = _first_argmax([jnp.where(i0 == j, -jnp.inf, vals[j]) for j in range(EXPERTS_PER_GROUP)])
    e0 = grp * EXPERTS_PER_GROUP + i0
    e1 = grp * EXPERTS_PER_GROUP + i1
    eid = lax.broadcasted_iota(I32, (N_EXPERTS, 1), 0)
    oh0, oh1 = eid == e0, eid == e1
    s0 = jnp.sum(jnp.where(oh0, scores, 0.0), 0, keepdims=True)
    s1 = jnp.sum(jnp.where(oh1, scores, 0.0), 0, keepdims=True)
    den = s0 + s1
    member = jnp.where(oh0 | oh1, 1.0, 0.0)
    before = lax.broadcasted_iota(I32, (tr, tr), 0) < lax.broadcasted_iota(I32, (tr, tr), 1)
    upper = jnp.where(before, 1.0, 0.0).astype(BF16)
    cnt = jnp.dot(member.astype(BF16), upper, preferred_element_type=F32)
    run = jnp.sum(member, 1, keepdims=True)
    run = jnp.floor((run + (ROW_ALIGN - 1)) * (1.0 / ROW_ALIGN)) * ROW_ALIGN
    run = jnp.broadcast_to(run, (N_EXPERTS, LANES))
    acc = jnp.zeros((1, LANES), F32)
    for e in range(N_EXPERTS):
        off_ref[e:e + 1, :] = acc
        acc = acc + run[e:e + 1]
    off = off_ref[...]
    at = off[:, 0:1] + cnt
    pos_ref[0:1, :] = jnp.sum(jnp.where(oh0, at, 0.0), 0, keepdims=True).astype(I32)
    pos_ref[1:2, :] = jnp.sum(jnp.where(oh1, at, 0.0), 0, keepdims=True).astype(I32)
    w_ref[0:1, :] = s0 / den
    w_ref[1:2, :] = s1 / den
    tab_ref[0] = off.astype(I32)
    tab_ref[1] = run.astype(I32)
    tab_ref[2] = carry_ref[...].astype(I32)
    carry_ref[...] = carry_ref[...] + run
    cnt_ref[...] = carry_ref[...]


def _route(logits_t, router_bias, tr):
    t = logits_t.shape[1]
    tok = lambda i: (0, i)
    return pl.pallas_call(
        functools.partial(_route_kernel, tr=tr), grid=(t // tr,),
        in_specs=[pl.BlockSpec((N_EXPERTS, tr), tok), pl.BlockSpec((N_EXPERTS, 1), lambda i: (0, 0))],
        out_specs=[pl.BlockSpec((2, tr), tok), pl.BlockSpec((2, tr), tok),
                   pl.BlockSpec((None, 3, N_EXPERTS, LANES), lambda i: (i, 0, 0, 0)),
                   pl.BlockSpec((N_EXPERTS, LANES), lambda i: (0, 0))],
        out_shape=[jax.ShapeDtypeStruct((2, t), I32), jax.ShapeDtypeStruct((2, t), F32),
                   jax.ShapeDtypeStruct((t // tr, 3, N_EXPERTS, LANES), I32),
                   jax.ShapeDtypeStruct((N_EXPERTS, LANES), F32)],
        scratch_shapes=[pltpu.VMEM((N_EXPERTS, LANES), F32), pltpu.VMEM((N_EXPERTS, LANES), F32)],
        compiler_params=_cparams(1), name="route",
    )(logits_t, router_bias.reshape(N_EXPERTS, 1))


def _run_copies(n, max_rows, make_copy, wait):
    sz = max_rows
    while sz >= ROW_ALIGN:
        start = (n // (2 * sz)) * (2 * sz)

        @pl.when((n & sz) != 0)
        def _(start=start, sz=sz):
            cp = make_copy(start, sz)
            cp.wait() if wait else cp.start()

        sz //= 2


def _wait_rows(total, max_rows, make_copy):
    sz = max_rows
    while sz >= ROW_ALIGN:
        @pl.when((total & sz) != 0)
        def _(sz=sz):
            make_copy(sz).wait()

        sz //= 2


def _rows(ref, start, size):
    return ref.at[pl.ds(pl.multiple_of(start, ROW_ALIGN), size)]


def _dispatch_kernel(tab_ref, h_ref, pos_ref, w_ref, xs_ref, buf, zbuf, sem, *, tt, n_tiles, bm):
    i = pl.program_id(0)
    d = h_ref.shape[1]
    sb = buf.shape[1]
    ne = N_EXPERTS
    fill = 3 * n_tiles * ne

    @pl.when(i == 0)
    def _():
        zbuf[...] = jnp.zeros_like(zbuf)
        for wait in (False, True):
            for e in range(ne):
                dst, n = tab_ref[fill + e], tab_ref[fill + ne + e]
                _run_copies(n, bm // 2, lambda s, z: pltpu.make_async_copy(
                    zbuf.at[pl.ds(0, z)], _rows(xs_ref, dst + s, z), sem.at[2]), wait)

        def zero_block(j, carry):
            for half in range(2):
                cp = pltpu.make_async_copy(zbuf, _rows(xs_ref, j * bm + half * (bm // 2), bm // 2), sem.at[2])
                cp.start()
                cp.wait()
            return carry

        lax.fori_loop(tab_ref[fill + 2 * ne], xs_ref.shape[0] // bm, zero_block, 0)

    def tile_copies(tile, slot, wait):
        for e in range(ne):
            off = tab_ref[tile * ne + e]
            n = tab_ref[(n_tiles + tile) * ne + e]
            dst = tab_ref[(2 * n_tiles + tile) * ne + e]
            _run_copies(n, tt, lambda s, z: pltpu.make_async_copy(
                _rows(buf.at[slot], off + s, z), _rows(xs_ref, dst + s, z), sem.at[slot]), wait)

    slot = i % 2
    pos, w = pos_ref[...], w_ref[...]
    hb = h_ref[...].astype(BF16)
    for r0 in range(0, sb, LANES):
        srow = r0 + lax.broadcasted_iota(I32, (LANES, 1), 0)
        m0, m1 = srow == pos[0:1], srow == pos[1:2]
        perm = (jnp.where(m0, 1.0, 0.0) + jnp.where(m1, 1.0, 0.0)).astype(BF16)
        buf[slot, r0:r0 + LANES, :d] = jnp.dot(perm, hb, preferred_element_type=F32)
        wrow = jnp.sum(jnp.where(m0, w[0:1], 0.0) + jnp.where(m1, w[1:2], 0.0), 1, keepdims=True)
        buf[slot, r0:r0 + LANES, d:] = jnp.broadcast_to(wrow, (LANES, LANES))

    def wait_tile(tile, slot):
        total = tab_ref[fill + 2 * ne + 1 + tile]
        _wait_rows(total, tt * 2, lambda z: pltpu.make_async_copy(
            buf.at[slot, pl.ds(0, z)], xs_ref.at[pl.ds(0, z)], sem.at[slot]))

    @pl.when(i > 0)
    def _():
        wait_tile(i - 1, 1 - slot)

    tile_copies(i, slot, False)

    @pl.when(i == n_tiles - 1)
    def _():
        wait_tile(i, slot)


def _dispatch(tab, h2, pos, wts, n_slots, tt, sb, bm):
    t, d = h2.shape
    n_tiles = t // tt
    return pl.pallas_call(
        functools.partial(_dispatch_kernel, tt=tt, n_tiles=n_tiles, bm=bm),
        grid_spec=pltpu.PrefetchScalarGridSpec(
            num_scalar_prefetch=1, grid=(n_tiles,),
            in_specs=[pl.BlockSpec((tt, d), lambda i, s: (i, 0)), pl.BlockSpec((2, tt), lambda i, s: (0, i)),
                      pl.BlockSpec((2, tt), lambda i, s: (0, i))],
            out_specs=pl.BlockSpec(memory_space=pl.ANY),
            scratch_shapes=[pltpu.VMEM((2, sb, d + LANES), F32), pltpu.VMEM((bm // 2, d + LANES), F32),
                            pltpu.SemaphoreType.DMA((3,))]),
        out_shape=jax.ShapeDtypeStruct((n_slots, d + LANES), F32),
        compiler_params=_cparams(1), name="moe_dispatch",
    )(tab, h2, pos, wts)


def _ffn_kernel(blk_e_ref, blk_rows_ref, next_e_ref, nused_ref, x_ref, wg_hbm, wu_hbm, wd_hbm, o_ref,
                stage, wg_bf, wu_bf, wd_bf, sem, *, layer, parts):
    del nused_ref
    i = pl.program_id(0)
    e = blk_e_ref[i]
    new_expert = (i == 0) | (e != blk_e_ref[jnp.maximum(i - 1, 0)])

    def fetch(expert, wait):
        for j, src in enumerate((wg_hbm, wu_hbm, wd_hbm)):
            cp = pltpu.make_async_copy(src.at[layer, expert], stage.at[j], sem.at[j])
            cp.wait() if wait else cp.start()

    @pl.when(i == 0)
    def _():
        fetch(e, False)

    @pl.when(new_expert)
    def _():
        fetch(e, True)
        wg_bf[...] = stage[0].astype(BF16)
        wu_bf[...] = stage[1].astype(BF16)
        wd_bf[...] = stage[2].astype(BF16)

        @pl.when(next_e_ref[i] >= 0)
        def _():
            fetch(next_e_ref[i], False)

    d = wg_bf.shape[0]
    rows = blk_rows_ref[i]
    part = x_ref.shape[0] // parts
    spans = [slice(p * part, (p + 1) * part) for p in range(parts)]

    def gate_up(rs):
        x = x_ref[rs, :d].astype(BF16)
        return (jnp.dot(x, wg_bf[...], preferred_element_type=F32), jnp.dot(x, wu_bf[...], preferred_element_type=F32))

    def finish(rs, gate, up):
        act = (gate * jax.nn.sigmoid(gate) * up).astype(BF16)
        o_ref[rs, :] = jnp.dot(act, wd_bf[...], preferred_element_type=F32) * x_ref[rs, d:d + 1]

    all_parts = rows > (parts - 1) * part

    @pl.when(all_parts)
    def _():
        nxt = gate_up(spans[0])
        for p, rs in enumerate(spans):
            cur = nxt
            if p + 1 < parts:
                nxt = gate_up(spans[p + 1])
            finish(rs, *cur)

    for p, rs in enumerate(spans):
        @pl.when(jnp.logical_not(all_parts) & (rows > p * part))
        def _(rs=rs):
            finish(rs, *gate_up(rs))

        @pl.when(rows <= p * part)
        def _(rs=rs):
            o_ref[rs, :] = jnp.zeros((part, d), F32)


def _ffn(blk_e, blk_rows, next_e, n_used, xs, wg, wu, wd, layer, bm, parts):
    n_slots, xw = xs.shape
    d, ff = wg.shape[2:]
    assert d == ff
    xrow = lambda i, be, br, ne, nu: (jnp.minimum(i, nu[0] - 1), 0)
    hbm = pl.BlockSpec(memory_space=pl.ANY)
    return pl.pallas_call(
        functools.partial(_ffn_kernel, layer=layer, parts=parts),
        grid_spec=pltpu.PrefetchScalarGridSpec(
            num_scalar_prefetch=4, grid=(n_slots // bm,),
            in_specs=[pl.BlockSpec((bm, xw), xrow), hbm, hbm, hbm],
            out_specs=pl.BlockSpec((bm, d), lambda i, be, br, ne, nu: (i, 0)),
            scratch_shapes=[pltpu.VMEM((3, d, ff), F32), pltpu.VMEM((d, ff), BF16), pltpu.VMEM((d, ff), BF16),
                            pltpu.VMEM((ff, d), BF16), pltpu.SemaphoreType.DMA((3,))]),
        out_shape=jax.ShapeDtypeStruct((n_slots, d), F32),
        compiler_params=_cparams(1), name="moe_ffn",
    )(blk_e, blk_rows, next_e, n_used, xs, wg, wu, wd)


def _combine_kernel(tab_ref, ys_ref, pos_ref, x1_ref, gate_ref, lng_ref, lnb_ref, o_ref, ybuf, sem, *, tt, n_tiles):
    i = pl.program_id(0)
    sb = ybuf.shape[1]
    ne = N_EXPERTS
    slot = i % 2

    def tile_copies(tile, slot, wait):
        for e in range(ne):
            off = tab_ref[tile * ne + e]
            n = tab_ref[(n_tiles + tile) * ne + e]
            src = tab_ref[(2 * n_tiles + tile) * ne + e]
            _run_copies(n, tt, lambda s, z: pltpu.make_async_copy(
                _rows(ys_ref, src + s, z), _rows(ybuf.at[slot], off + s, z), sem.at[slot]), wait)

    @pl.when(i == 0)
    def _():
        ybuf[...] = jnp.zeros_like(ybuf)
        tile_copies(0, 0, False)

    @pl.when(i + 1 < n_tiles)
    def _():
        tile_copies(i + 1, 1 - slot, False)

    total = tab_ref[3 * n_tiles * ne + 2 * ne + 1 + i]
    _wait_rows(total, tt * 2, lambda z: pltpu.make_async_copy(
        ys_ref.at[pl.ds(0, z)], ybuf.at[slot, pl.ds(0, z)], sem.at[slot]))
    scol = lax.broadcasted_iota(I32, (1, sb), 1)
    yb = ybuf[slot].astype(BF16)

    def moe_rows(r0):
        pos = pos_ref[r0:r0 + LANES, :]
        unperm = (jnp.where(scol == pos[:, 0:1], 1.0, 0.0) + jnp.where(scol == pos[:, 1:2], 1.0, 0.0)).astype(BF16)
        return jnp.dot(unperm, yb, preferred_element_type=F32)

    starts = list(range(0, tt, LANES))
    nxt = moe_rows(starts[0])
    for j, r0 in enumerate(starts):
        f = nxt
        if j + 1 < len(starts):
            nxt = moe_rows(starts[j + 1])
        o_ref[r0:r0 + LANES, :] = _layer_norm(DEEPNORM_ALPHA * x1_ref[r0:r0 + LANES, :] + gate_ref[...] * f,
                                              lng_ref[...], lnb_ref[...])


def _combine(tab, ys, pos_t, x1, gate, ln_g, ln_b, rows_per_mod, tt, sb):
    t, d = x1.shape
    tiles_per_mod = rows_per_mod // tt
    nmod = gate.shape[0]
    return pl.pallas_call(
        functools.partial(_combine_kernel, tt=tt, n_tiles=t // tt),
        grid_spec=pltpu.PrefetchScalarGridSpec(
            num_scalar_prefetch=1, grid=(t // tt,),
            in_specs=[pl.BlockSpec(memory_space=pl.ANY),
                      pl.BlockSpec((tt, 2), lambda i, s: (i, 0)),
                      pl.BlockSpec((tt, d), lambda i, s: (i, 0)),
                      pl.BlockSpec((None, 1, d), lambda i, s: (i // tiles_per_mod, 0, 0)),
                      pl.BlockSpec((1, d), lambda i, s: (0, 0)), pl.BlockSpec((1, d), lambda i, s: (0, 0))],
            out_specs=pl.BlockSpec((tt, d), lambda i, s: (i, 0)),
            scratch_shapes=[pltpu.VMEM((2, sb, d), F32), pltpu.SemaphoreType.DMA((2,))]),
        out_shape=jax.ShapeDtypeStruct((t, d), F32),
        compiler_params=_cparams(1), name="moe_combine",
    )(tab, ys, pos_t, x1, gate.reshape(nmod, 1, d), ln_g.reshape(1, d), ln_b.reshape(1, d))


def _moe(h2, logits_t, x1, gate, ln_g, ln_b, router_bias, wg, wu, wd, layer, rows_per_mod):
    t, d = h2.shape
    bm, tt, ne = MOE_BM, MOE_TILE, N_EXPERTS
    n_tiles = t // tt
    pad = ne * (ROW_ALIGN - 1)
    sb = -(-(2 * tt + pad) // LANES) * LANES
    pos, wts, tab3, cnt = _route(logits_t, router_bias, tr=tt)
    rows = cnt[:, 0].astype(I32)
    prows = (rows + bm - 1) // bm * bm
    pends = jnp.cumsum(prows)
    pstarts = pends - prows
    n_blk = -(-(2 * t + n_tiles * pad + ne * (bm - 1)) // bm)
    n_used = pends[-1] // bm
    blk_ids = jnp.minimum(jnp.arange(n_blk, dtype=I32), n_used - 1)
    blk_e = jnp.minimum(jnp.sum((blk_ids[:, None] * bm >= pends[None, :]).astype(I32), 1), ne - 1)
    eids = jnp.arange(ne, dtype=I32)
    of_blk = lambda per_expert: jnp.sum(jnp.where(blk_e[:, None] == eids[None, :], per_expert[None, :], 0), 1)
    raw_ids = jnp.arange(n_blk, dtype=I32)
    blk_rows = jnp.where(raw_ids < n_used, jnp.clip(of_blk(pstarts + rows) - raw_ids * bm, 0, bm), 0)
    later_used = (eids[None, :] > eids[:, None]) & (prows[None, :] > 0)
    next_used = jnp.min(jnp.where(later_used, eids[None, :], ne), 1)
    next_e = of_blk(jnp.where(next_used < ne, next_used, -1))
    tab3 = tab3[:, :, :, 0]
    tab = jnp.concatenate([tab3[:, 0].reshape(-1), tab3[:, 1].reshape(-1),
                           (tab3[:, 2] + pstarts[None, :]).reshape(-1), pstarts + rows, prows - rows,
                           n_used.reshape(1), jnp.sum(tab3[:, 1], 1)]).astype(I32)
    xs = _dispatch(tab, h2, pos, wts, n_blk * bm, tt, sb, bm)
    ys = _ffn(blk_e, blk_rows.astype(I32), next_e.astype(I32), n_used.reshape(1).astype(I32), xs, wg, wu, wd,
              layer, bm, parts=2)
    return _combine(tab, ys, pos.T, x1, gate, ln_g, ln_b, rows_per_mod, tt, sb)


def _filter_mlp_kernel(z_ref, w1_ref, w2_ref, w3_ref, b_ref, sf_ref, o_ref):
    hp = lax.Precision.HIGHEST
    b, sf = b_ref[...], sf_ref[...]
    h = jnp.sin(sf[0:1] * (jnp.dot(z_ref[...], w1_ref[...], precision=hp, preferred_element_type=F32) + b[0:1]))
    h = jnp.sin(sf[1:2] * (jnp.dot(h, w2_ref[...], precision=hp, preferred_element_type=F32) + b[1:2]))
    o_ref[...] = jnp.sin(sf[2:3] * (jnp.dot(h, w3_ref[...], precision=hp, preferred_element_type=F32) + b[2:3]))


def _filter_mlp(zp, w_in_p, w_hid, b, sf):
    n = zp.shape[0]
    return pl.pallas_call(
        _filter_mlp_kernel, out_shape=jax.ShapeDtypeStruct((n, FILTER_HID), F32),
        compiler_params=pltpu.CompilerParams(vmem_limit_bytes=VMEM_LIMIT), name="hyena_filter_mlp",
    )(zp, w_in_p, w_hid[0], w_hid[1], b, sf)


def _short_conv(u_ref, cw_ref, cb_ref):
    n = u_ref.shape[0]
    u = u_ref[...].astype(F32)
    row = lax.broadcasted_iota(I32, (n, 1), 0)
    prev = jnp.where(row == 0, 0.0, pltpu.roll(u, 1, 0))
    nxt = jnp.where(row == n - 1, 0.0, pltpu.roll(u, n - 1, 0))
    cw = cw_ref[...]
    return prev * cw[0:1] + u * cw[1:2] + nxt * cw[2:3] + cb_ref[...]


def _lag_spec_kernel(hid_ref, wf_ref, wb_ref, dec_ref, ph_ref, h_ref, fw_scr, bw_scr, *, s, nb):
    lag = pl.program_id(2)
    n = hid_ref.shape[0]

    @pl.when(lag == 0)
    def _():
        def split(v):
            hi = v.astype(BF16)
            return hi, (v - hi.astype(F32)).astype(BF16)

        def precise_dot(a_hi, a_lo, w):
            w_hi, w_lo = split(w)
            return (jnp.dot(a_hi, w_hi, preferred_element_type=F32) + jnp.dot(a_lo, w_hi, preferred_element_type=F32)
                    + jnp.dot(a_hi, w_lo, preferred_element_type=F32))

        dec = dec_ref[...]
        hid_hi, hid_lo = split(hid_ref[...])
        row = lax.broadcasted_iota(I32, (n, 1), 0)
        fw = precise_dot(hid_hi, hid_lo, wf_ref[...]) * dec
        bw = precise_dot(hid_hi, hid_lo, wb_ref[...]) * dec
        fw_scr[...] = fw.astype(BF16)
        bw_scr[...] = jnp.where(row == 0, 0.0, bw).astype(BF16)

    def piece(ph, scr, j):
        return jnp.dot(ph_ref[ph], scr[j * s:(j + 1) * s, :], preferred_element_type=F32)

    for idx in range(2 * nb - 1):
        m = idx - (nb - 1)

        @pl.when(lag == idx)
        def _(m=m):
            if m >= 1:
                h = piece(0, fw_scr, m) + piece(1, fw_scr, m - 1)
            elif m == 0:
                h = piece(0, fw_scr, 0) + piece(2, bw_scr, 0)
            else:
                h = piece(2, bw_scr, -m) + piece(3, bw_scr, -m - 1)
            h_ref[...] = h.astype(h_ref.dtype)


def _lag_spec(hid, w_out4, decay, phases, s, tc):
    n = hid.shape[0]
    d = D_MODEL
    nb = n // s
    return pl.pallas_call(
        functools.partial(_lag_spec_kernel, s=s, nb=nb), grid=(2, d // tc, 2 * nb - 1),
        in_specs=[pl.BlockSpec((n, FILTER_HID), lambda o, c, l: (0, 0)),
                  pl.BlockSpec((None, None, FILTER_HID, tc), lambda o, c, l: (0, o, 0, c)),
                  pl.BlockSpec((None, None, FILTER_HID, tc), lambda o, c, l: (1, o, 0, c)),
                  pl.BlockSpec((n, tc), lambda o, c, l: (0, c)),
                  pl.BlockSpec((4, 2 * s, s), lambda o, c, l: (0, 0, 0))],
        out_specs=pl.BlockSpec((None, None, 2 * s, tc), lambda o, c, l: (o, l, 0, c)),
        out_shape=jax.ShapeDtypeStruct((2, 2 * nb - 1, 2 * s, d), BF16),
        scratch_shapes=[pltpu.VMEM((n, tc), BF16), pltpu.VMEM((n, tc), BF16)],
        compiler_params=_cparams(3), name="hyena_lag_spec",
    )(hid, w_out4, w_out4, decay, phases)


def _part_conv_kernel(uv_ref, ux1_ref, ux2_ref, cwv_ref, cwx1_ref, cwx2_ref, cbv_ref, cbx1_ref, cbx2_ref,
                      f_ref, finv_ref, h_ref, sk_ref, o_ref, src_scr, gate_scr, zb_scr, zall, ybuf, *, s, nb, rc):
    src_scr[...] = _short_conv(uv_ref, cwv_ref, cbv_ref)
    gates = ((ux1_ref, cwx1_ref, cbx1_ref), (ux2_ref, cwx2_ref, cbx2_ref))
    for o, gate in enumerate(gates):
        zb_scr[...] = src_scr[...].astype(BF16)
        gate_scr[...] = _short_conv(*gate)
        for j in range(nb):
            zall[j] = jnp.dot(f_ref[...], zb_scr[j * s:(j + 1) * s, :], preferred_element_type=F32).astype(BF16)
        sk = sk_ref[o]
        top = lax.broadcasted_iota(I32, (rc, 1), 0) == 0
        for i in range(nb):
            pairs = [(j, i - j + nb - 1) for j in range(nb)]
            dc = nyq = None
            for j, m in pairs:
                t0 = zall[j, 0:rc, :].astype(F32)[0:1] * h_ref[o, m, 0:rc, :].astype(F32)[0:1]
                t1 = zall[j, s:s + rc, :].astype(F32)[0:1] * h_ref[o, m, s:s + rc, :].astype(F32)[0:1]
                dc, nyq = (t0, t1) if dc is None else (dc + t0, nyq + t1)
            for r0 in range(0, s, rc):
                re, im = slice(r0, r0 + rc), slice(s + r0, s + r0 + rc)
                yr = yi = None
                for j, m in pairs:
                    zr, zi, hr, g = zall[j, re, :], zall[j, im, :], h_ref[o, m, re, :], h_ref[o, m, im, :]
                    tr, ti = zr * hr - zi * g, zr * g + zi * hr
                    yr, yi = (tr, ti) if yr is None else (yr + tr, yi + ti)
                if r0 == 0:
                    yr, yi = jnp.where(top, dc.astype(BF16), yr), jnp.where(top, nyq.astype(BF16), yi)
                ybuf[re, :] = yr
                ybuf[im, :] = yi
            y = jnp.dot(finv_ref[...], ybuf[...], preferred_element_type=F32)
            rows = slice(i * s, (i + 1) * s)
            zn = gate_scr[rows, :] * (y + src_scr[rows, :] * sk)
            if o + 1 < len(gates):
                src_scr[rows, :] = zn
            else:
                o_ref[rows, :] = zn.astype(o_ref.dtype)


def _part_conv(u, conv_w, conv_b, f_tab, finv_tab, spec, skip, s, tc):
    b, n, _ = u.shape
    d = D_MODEL
    ncb = d // tc
    nb = n // s
    col = lambda part: (lambda c, bi: (bi, 0, part * ncb + c))
    wcol = lambda part: (lambda c, bi: (0, part * ncb + c))
    const = lambda c, bi: (0, 0)
    in_specs = ([pl.BlockSpec((None, n, tc), col(p)) for p in range(3)]
                + [pl.BlockSpec((3, tc), wcol(p)) for p in range(3)]
                + [pl.BlockSpec((1, tc), wcol(p)) for p in range(3)]
                + [pl.BlockSpec((2 * s, s), const), pl.BlockSpec((s, 2 * s), const),
                   pl.BlockSpec((2, 2 * nb - 1, 2 * s, tc), lambda c, bi: (0, 0, 0, c), pipeline_mode=pl.Buffered(1)),
                   pl.BlockSpec((2, 1, tc), lambda c, bi: (0, 0, c))])
    return pl.pallas_call(
        functools.partial(_part_conv_kernel, s=s, nb=nb, rc=32), grid=(ncb, b),
        in_specs=in_specs,
        out_specs=pl.BlockSpec((None, n, tc), lambda c, bi: (bi, 0, c)),
        out_shape=jax.ShapeDtypeStruct((b, n, d), BF16),
        scratch_shapes=[pltpu.VMEM((n, tc), F32), pltpu.VMEM((n, tc), F32), pltpu.VMEM((n, tc), BF16),
                        pltpu.VMEM((nb, 2 * s, tc), BF16), pltpu.VMEM((2 * s, tc), BF16)],
        compiler_params=_cparams(2), name="hyena_part_conv",
    )(u, u, u, conv_w, conv_w, conv_w, conv_b, conv_b, conv_b, f_tab, finv_tab, spec, skip.reshape(2, 1, d))


def kernel(x, c, ctx, c_ctx, mod_w, mod_b, ln_g, ln_b, attn_w_in, attn_lambda, attn_subln_g, attn_sink, attn_w_out,
           hy_w_in, hy_conv_w, hy_conv_b, hy_ffn_w_in, hy_ffn_w_hid, hy_ffn_b, hy_sin_freq, hy_ffn_w_out, hy_skip,
           hy_w_out, router_w, router_bias, exp_w_gate, exp_w_up, exp_w_down):
    b, n, d = x.shape
    nc = ctx.shape[1]
    t = b * n
    assert d == D_MODEL and b + 1 <= MOD_ROWS and n % 512 == 0 and nc % 256 == 0

    c_rows = jnp.zeros((MOD_ROWS, d), F32).at[:b].set(c).at[b].set(c_ctx)
    mods = _mods(c_rows, mod_w, mod_b)
    router_wt = jnp.pad(router_w, ((0, 0), (0, LANES - N_EXPERTS)))
    x2d = x.reshape(t, d)

    sh1, sc1, g1, sh2, sc2, g2 = jnp.split(mods[0], 6, axis=-1)
    cos, sin = _rope_tables(n)
    w_in = attn_w_in[0].astype(BF16)
    q_groups = tuple(range(Q_W // LANES))
    rope_groups = q_groups + tuple(range(Q_W // LANES, (Q_W + DIFF_QK_W) // LANES)) + (
        (Q_W + DIFF_QK_W + DIFF_VW) // LANES,)
    q_scales = tuple((g, HEAD_DIM ** -0.5 * LOG2_E) for g in q_groups)
    proj = _modmm(x2d, sc1[:b], sh1[:b], w_in, rows_per_mod=n, tm=1024,
                  rope=(jnp.asarray(cos), jnp.asarray(sin), rope_groups, q_scales), name="attn_in_proj")
    proj_c = _modmm(ctx.reshape(b * nc, d), sc1[b:b + 1], sh1[b:b + 1], w_in[:, Q_W:], rows_per_mod=b * nc, tm=256,
                    name="ctx_in_proj")
    proj = proj.reshape(b, n, ATTN_PROJ_W)
    proj_c = proj_c.reshape(b, nc, KV_W)
    lam_init = 0.8 - 0.6 * math.exp(-0.3 * 0)
    oa = _diff_attn(proj, proj_c, attn_lambda[0], attn_subln_g[0], lam_init, tq=2048, sub=128)
    ow = _win_attn(proj, proj_c, attn_sink[0], step=512, tq=128)
    w_out = attn_w_out[0].astype(BF16)
    x1, h2, lgt = _proj_ln([oa.reshape(t, DIFF_VW), ow.reshape(t, WIN_Q_W)], [w_out[:DIFF_VW], w_out[DIFF_VW:]],
                           x2d, g1[:b], ln_g[0, 0], ln_b[0, 0], sc2[:b], sh2[:b], router_wt, rows_per_mod=n, tm=1024,
                           name="attn_out_proj_ln")
    x2 = _moe(h2, lgt, x1, g2[:b], ln_g[0, 1], ln_b[0, 1], router_bias, exp_w_gate, exp_w_up, exp_w_down, layer=0,
              rows_per_mod=n)

    sh1, sc1, g1, sh2, sc2, g2 = jnp.split(mods[1], 6, axis=-1)
    u = _modmm(x2, sc1[:b], sh1[:b], hy_w_in[0].astype(BF16), rows_per_mod=n, tm=1024, name="hyena_in_proj")
    part, tc = HYENA_PART, 256
    phases_np, finv_np = _partition_tables(part)
    phases, finv_tab = jnp.asarray(phases_np).astype(BF16), jnp.asarray(finv_np).astype(BF16)
    zp, decay = _filter_tables(n)
    w_in_p = jnp.zeros((FILTER_HID, FILTER_HID), F32).at[:FILTER_EMB].set(hy_ffn_w_in[0])
    hid = _filter_mlp(jnp.asarray(zp), w_in_p, hy_ffn_w_hid[0], hy_ffn_b[0], hy_sin_freq[0])
    spec = _lag_spec(hid, hy_ffn_w_out[0].reshape(FILTER_HID, 2, 2, d).transpose(1, 2, 0, 3), jnp.asarray(decay),
                     phases, part, tc=512)
    u3 = u.reshape(b, n, 3 * d)
    conv_b = hy_conv_b[0].reshape(1, 3 * d)
    z2 = _part_conv(u3, hy_conv_w[0], conv_b, phases[0], finv_tab, spec, hy_skip[0], part, tc)
    x3, h2, lgt = _proj_ln([z2.reshape(t, d)], [hy_w_out[0].astype(BF16)], x2, g1[:b], ln_g[1, 0], ln_b[1, 0],
                           sc2[:b], sh2[:b], router_wt, rows_per_mod=n, tm=1024, name="hyena_out_proj_ln")
    x4 = _moe(h2, lgt, x3, g2[:b], ln_g[1, 1], ln_b[1, 1], router_bias, exp_w_gate, exp_w_up, exp_w_down, layer=1,
              rows_per_mod=n)
    return x4.reshape(b, n, d)
```

```python
import functools
import math

import jax
import jax.numpy as jnp
import numpy as np
from jax import lax
from jax.experimental import pallas as pl
from jax.experimental.pallas import tpu as pltpu

F32 = jnp.float32
BF16 = jnp.bfloat16
I32 = jnp.int32

D_MODEL = 1024
DEPTH = 2
GRID_W = 64
HEAD_DIM = 64
DIFF_HEADS = 4
WIN_Q_HEADS = 8
WIN_KV_HEADS = 2
WINDOW = 128
ROPE_BASE = 10000.0
DIFF_QK_W = DIFF_HEADS * 2 * HEAD_DIM
DIFF_VW = DIFF_HEADS * 2 * HEAD_DIM
WIN_Q_W = WIN_Q_HEADS * HEAD_DIM
WIN_KV_W = WIN_KV_HEADS * HEAD_DIM
Q_W = DIFF_QK_W + WIN_Q_W
KV_W = DIFF_QK_W + DIFF_VW + 2 * WIN_KV_W
ATTN_PROJ_W = Q_W + KV_W
FILTER_EMB = 33
FILTER_HID = 64
DECAY_TARGET = 1e-2
FAST_DECAY_PCT = 0.3
SLOW_DECAY_PCT = 1.5
N_EXPERTS = 16
N_GROUPS = 4
EXPERTS_PER_GROUP = N_EXPERTS // N_GROUPS
LN_EPS = 1e-5
DEEPNORM_ALPHA = (2 * DEPTH) ** 0.25
NEG_INF = -1e30
LOG2_E = 1.4426950408889634

LANES = 128
MOD_ROWS = 16
VMEM_LIMIT = 60 * 1024 * 1024
MOE_BM = 512
MOE_TILE = 512
HYENA_PART = 512
HYENA_TC = 256
HYENA_ROWS = 32
LAG_SPEC_TC = 512
PROJ_TM = 1024
PROJ_CHUNK = 256
CTX_TM = 256
MODS_TN = 1536
DIFF_TQ, DIFF_SUB = 2048, 128
WIN_STEP, WIN_TQ = 512, 128
ROW_ALIGN = 8
NT_DIMS = (((1,), (1,)), ((), ()))


def _cparams(n_axes):
    return pltpu.CompilerParams(dimension_semantics=("arbitrary",) * n_axes, vmem_limit_bytes=VMEM_LIMIT)


@functools.lru_cache(maxsize=None)
def _rope_tables(n):
    rows = n // GRID_W
    r, col = np.meshgrid(np.arange(rows, dtype=np.float32), np.arange(GRID_W, dtype=np.float32), indexing="ij")
    axis_dim = HEAD_DIM // 2
    inv_freq = (ROPE_BASE ** (-np.arange(0, axis_dim, 2, dtype=np.float32) / axis_dim)).astype(np.float32)
    ang = np.concatenate([r.reshape(-1, 1) * inv_freq, col.reshape(-1, 1) * inv_freq], -1)
    ang = np.concatenate([ang, ang], -1).astype(np.float32)
    cos, sin = np.cos(ang), np.sin(ang)
    half = np.arange(HEAD_DIM) < HEAD_DIM // 2
    sin_signed = np.where(half[None, :], -sin, sin)
    reps = LANES // HEAD_DIM
    return (np.tile(cos, (1, reps)).astype(np.float32), np.tile(sin_signed, (1, reps)).astype(np.float32))


@functools.lru_cache(maxsize=None)
def _filter_tables(n):
    t = np.linspace(0.0, 1.0, n, dtype=np.float32)[:, None]
    bands = (FILTER_EMB - 1) // 2
    w = (2.0 * math.pi * np.arange(n, dtype=np.float32)[:, None] / n).astype(np.float32)
    fr = np.linspace(1e-4, bands - 1, bands, dtype=np.float32)[None, :]
    z = np.concatenate([t, np.cos(fr * w), -np.sin(fr * w)], -1).astype(np.float32)
    zp = np.zeros((n, FILTER_HID), np.float32)
    zp[:, :FILTER_EMB] = z
    deltas = np.abs(np.linspace(math.log(DECAY_TARGET) / SLOW_DECAY_PCT, math.log(DECAY_TARGET) / FAST_DECAY_PCT,
                                D_MODEL, dtype=np.float32))
    decay = np.exp(-t * deltas[None, :]).astype(np.float32)
    return zp, decay


@functools.lru_cache(maxsize=None)
def _partition_tables(s):
    big = 2 * s
    k = np.arange(s, dtype=np.int64)[:, None]
    r = np.arange(s, dtype=np.int64)[None, :]

    def phase(pr, sin_sign, drop_r0):
        ang = ((k * pr) % big).astype(np.float64) * (2.0 * math.pi / big)
        c, sn = np.cos(ang), sin_sign * np.sin(ang)
        sn[0, :] = 1.0 - 2.0 * (pr[0] & 1)
        if drop_r0:
            c[:, 0] = 0.0
            sn[:, 0] = 0.0
        return np.concatenate([c, sn], 0)

    phases = np.stack([phase(r, 1.0, False), phase(s - r, -1.0, True), phase(r, -1.0, False), phase(s - r, 1.0, True)])
    ang = ((r.T * k.T) % big).astype(np.float64) * (2.0 * math.pi / big)
    ci = np.cos(ang) * (2.0 / big)
    ci[:, 0] = 1.0 / big
    si = np.sin(ang) * (2.0 / big)
    si[:, 0] = (1.0 - 2.0 * (np.arange(s) & 1)) / big
    return phases.astype(np.float32), np.concatenate([ci, si], 1).astype(np.float32)


def _mods_kernel(c_ref, w_ref, b_ref, o_ref):
    c = c_ref[...]
    a = (c * jax.nn.sigmoid(c)).astype(BF16)
    o_ref[...] = jnp.dot(a, w_ref[...].astype(BF16), preferred_element_type=F32) + b_ref[...]


def _mods(c_rows, mod_w, mod_b):
    d = D_MODEL
    tn = MODS_TN
    return pl.pallas_call(
        _mods_kernel,
        grid=(DEPTH, 6 * d // tn),
        in_specs=[pl.BlockSpec((MOD_ROWS, d), lambda l, j: (0, 0)),
                  pl.BlockSpec((None, d, tn), lambda l, j: (l, 0, j)),
                  pl.BlockSpec((None, 1, tn), lambda l, j: (l, 0, j))],
        out_specs=pl.BlockSpec((None, MOD_ROWS, tn), lambda l, j: (l, 0, j)),
        out_shape=jax.ShapeDtypeStruct((DEPTH, MOD_ROWS, 6 * d), F32),
        compiler_params=_cparams(2), name="mods",
    )(c_rows, mod_w, mod_b.reshape(DEPTH, 1, 6 * d))


def _modmm_kernel(*refs, n_groups, rope_groups, scaled_groups, chunk):
    if rope_groups:
        x_ref, sc_ref, sh_ref, w_ref, cos_ref, sin_ref, o_ref = refs
        cos, sin = cos_ref[...], sin_ref[...]
        lane = lax.broadcasted_iota(I32, (1, LANES), 1)
        first_half = (lane % HEAD_DIM) < HEAD_DIM // 2
    else:
        x_ref, sc_ref, sh_ref, w_ref, o_ref = refs
    h = (x_ref[...] * (1.0 + sc_ref[...]) + sh_ref[...]).astype(BF16)
    scales = dict(scaled_groups)
    gpc = chunk // LANES
    for c in range(n_groups // gpc):
        acc = jnp.dot(h, w_ref[:, c * chunk:(c + 1) * chunk], preferred_element_type=F32)
        for j in range(gpc):
            g = c * gpc + j
            blk = acc[:, j * LANES:(j + 1) * LANES]
            if g in rope_groups:
                rot = jnp.where(first_half, pltpu.roll(blk, LANES - HEAD_DIM // 2, 1), pltpu.roll(blk, HEAD_DIM // 2, 1))
                blk = blk * cos + rot * sin
                if g in scales:
                    blk = blk * scales[g]
            o_ref[:, g * LANES:(g + 1) * LANES] = blk.astype(o_ref.dtype)


def _modmm(x2d, sc, sh, w, rows_per_mod, tm, rope=None, name="modmm"):
    t, d = x2d.shape
    n = w.shape[1]
    tiles_per_mod = rows_per_mod // tm
    nmod = sc.shape[0]
    in_specs = [pl.BlockSpec((tm, d), lambda i: (i, 0)),
                pl.BlockSpec((None, 1, d), lambda i: (i // tiles_per_mod, 0, 0)),
                pl.BlockSpec((None, 1, d), lambda i: (i // tiles_per_mod, 0, 0)),
                pl.BlockSpec((d, n), lambda i: (0, 0))]
    args = [x2d, sc.reshape(nmod, 1, d), sh.reshape(nmod, 1, d), w]
    rope_groups, scaled_groups = (), ()
    if rope is not None:
        cos, sin, rope_groups, scaled_groups = rope
        in_specs += [pl.BlockSpec((tm, LANES), lambda i: (i % tiles_per_mod, 0)),
                     pl.BlockSpec((tm, LANES), lambda i: (i % tiles_per_mod, 0))]
        args += [cos, sin]
    kern = functools.partial(_modmm_kernel, n_groups=n // LANES, rope_groups=tuple(rope_groups),
                             scaled_groups=tuple(scaled_groups), chunk=PROJ_CHUNK)
    return pl.pallas_call(
        kern, grid=(t // tm,), in_specs=in_specs,
        out_specs=pl.BlockSpec((tm, n), lambda i: (i, 0)),
        out_shape=jax.ShapeDtypeStruct((t, n), BF16),
        compiler_params=_cparams(1), name=name,
    )(*args)


def _diff_attn_kernel(q_ref, k_ref, v_ref, kc_ref, vc_ref, lam_ref, g_ref, o_ref, *, lam_init, sub):
    lane = lax.broadcasted_iota(I32, (1, LANES), 1)
    k, kc = k_ref[...], kc_ref[...]
    v, vc = v_ref[...], vc_ref[...]

    def scores(r0):
        q = q_ref[r0:r0 + sub, :]
        zero = jnp.zeros_like(q)
        out = []
        for qm in (jnp.where(lane < HEAD_DIM, q, zero), jnp.where(lane >= HEAD_DIM, q, zero)):
            out.append((lax.dot_general(qm, k, NT_DIMS, preferred_element_type=F32),
                        lax.dot_general(qm, kc, NT_DIMS, preferred_element_type=F32)))
        return out

    def probs(sl, sc):
        m = jnp.maximum(jnp.max(sl, -1, keepdims=True), jnp.max(sc, -1, keepdims=True))
        pl_, pc = jnp.exp2(sl - m), jnp.exp2(sc - m)
        den = jnp.sum(pl_, -1, keepdims=True) + jnp.sum(pc, -1, keepdims=True)
        return pl_, pc, 1.0 / den

    lv = lam_ref[...]
    lam = (jnp.exp(jnp.sum(lv[0:1] * lv[1:2], keepdims=True)) - jnp.exp(jnp.sum(lv[2:3] * lv[3:4], keepdims=True))
           + lam_init)
    starts = list(range(0, q_ref.shape[0], sub))
    nxt = scores(starts[0])
    for i, r0 in enumerate(starts):
        cur = nxt
        if i + 1 < len(starts):
            nxt = scores(starts[i + 1])
        p1l, p1c, r1 = probs(*cur[0])
        p2l, p2c, r2 = probs(*cur[1])
        w2 = lam * r2
        al = (p1l * r1 - p2l * w2).astype(BF16)
        ac = (p1c * r1 - p2c * w2).astype(BF16)
        o = jnp.dot(al, v, preferred_element_type=F32) + jnp.dot(ac, vc, preferred_element_type=F32)
        ms = jnp.mean(o * o, -1, keepdims=True)
        o_ref[r0:r0 + sub, :] = (o * lax.rsqrt(ms + LN_EPS) * g_ref[...] * (1.0 - lam_init)).astype(o_ref.dtype)


def _diff_attn(proj, proj_c, lam_vec, subln_g, lam_init, tq, sub):
    b, n, _ = proj.shape
    nc = proj_c.shape[1]
    kcol = Q_W // LANES
    vcol = (Q_W + DIFF_QK_W) // LANES
    vccol = DIFF_QK_W // LANES
    kern = functools.partial(_diff_attn_kernel, lam_init=lam_init, sub=sub)
    return pl.pallas_call(
        kern, grid=(b, DIFF_HEADS, n // tq),
        in_specs=[pl.BlockSpec((None, tq, LANES), lambda bi, h, i: (bi, i, h)),
                  pl.BlockSpec((None, n, LANES), lambda bi, h, i: (bi, 0, kcol + h)),
                  pl.BlockSpec((None, n, LANES), lambda bi, h, i: (bi, 0, vcol + h)),
                  pl.BlockSpec((None, nc, LANES), lambda bi, h, i: (bi, 0, h)),
                  pl.BlockSpec((None, nc, LANES), lambda bi, h, i: (bi, 0, vccol + h)),
                  pl.BlockSpec((4, HEAD_DIM), lambda bi, h, i: (0, 0)),
                  pl.BlockSpec((1, LANES), lambda bi, h, i: (0, 0))],
        out_specs=pl.BlockSpec((None, tq, LANES), lambda bi, h, i: (bi, i, h)),
        out_shape=jax.ShapeDtypeStruct((b, n, DIFF_VW), BF16),
        compiler_params=_cparams(3), name="diff_attn",
    )(proj, proj, proj, proj_c, proj_c, lam_vec, subln_g.reshape(1, LANES))


def _win_attn_kernel(sink_ref, q_ref, k_ref, v_ref, kc_ref, vc_ref, o_ref, *, seq, tq):
    for r0 in range(0, q_ref.shape[0], tq):
        _win_attn_rows(sink_ref, q_ref, k_ref, v_ref, kc_ref, vc_ref, o_ref, r0, seq=seq, tq=tq)


def _win_attn_rows(sink_ref, q_ref, k_ref, v_ref, kc_ref, vc_ref, o_ref, r0, *, seq, tq):
    first = pl.program_id(1) * q_ref.shape[0] + r0
    gq = WIN_Q_HEADS // WIN_KV_HEADS
    kw = tq + 2 * WINDOW
    start = pl.multiple_of(jnp.clip(first - WINDOW, 0, seq - kw), WINDOW)
    k_win, v_win = k_ref[pl.ds(start, kw), :], v_ref[pl.ds(start, kw), :]
    kc, vc = kc_ref[...], vc_ref[...]
    lane = lax.broadcasted_iota(I32, (1, LANES), 1)
    row = lax.broadcasted_iota(I32, (gq * tq, 1), 0)
    q_abs = first + row % tq
    k_abs = start + lax.broadcasted_iota(I32, (1, kw), 1)
    allowed = jnp.abs(q_abs - k_abs) <= WINDOW
    head = row // tq
    q = q_ref[r0:r0 + tq, :].astype(F32)
    for g in range(WIN_KV_HEADS):
        in_g = (lane // HEAD_DIM) == g
        parts = []
        for j in range(gq):
            hq = g * gq + j
            x = q[:, (hq // 2) * LANES:(hq // 2 + 1) * LANES]
            if hq % 2 != g:
                x = pltpu.roll(x, HEAD_DIM, 1)
            parts.append(jnp.where(in_g, x, 0.0))
        qs = jnp.concatenate(parts, 0).astype(BF16)
        s_loc = lax.dot_general(qs, k_win, NT_DIMS, preferred_element_type=F32)
        s_ctx = lax.dot_general(qs, kc, NT_DIMS, preferred_element_type=F32)
        s_loc = jnp.where(allowed, s_loc, NEG_INF)
        sk = jnp.zeros((gq * tq, 1), F32)
        for j in range(gq):
            sk = jnp.where(head == j, sink_ref[g * gq + j] * LOG2_E, sk)
        m = jnp.maximum(jnp.maximum(jnp.max(s_loc, -1, keepdims=True), jnp.max(s_ctx, -1, keepdims=True)), sk)
        p_loc, p_ctx = jnp.exp2(s_loc - m), jnp.exp2(s_ctx - m)
        den = jnp.sum(p_loc, -1, keepdims=True) + jnp.sum(p_ctx, -1, keepdims=True) + jnp.exp2(sk - m)
        o = (jnp.dot(p_loc.astype(BF16), v_win, preferred_element_type=F32)
             + jnp.dot(p_ctx.astype(BF16), vc, preferred_element_type=F32)) * (1.0 / den)
        for cb in range(gq // 2):
            pair = []
            for half in range(2):
                piece = o[(2 * cb + half) * tq:(2 * cb + half + 1) * tq]
                pair.append(piece if half == g else pltpu.roll(piece, HEAD_DIM, 1))
            col = (g * gq // 2 + cb) * LANES
            o_ref[r0:r0 + tq, col:col + LANES] = jnp.where(lane < HEAD_DIM, pair[0], pair[1]).astype(o_ref.dtype)


def _win_attn(proj, proj_c, sink, step, tq):
    b, n, _ = proj.shape
    nc = proj_c.shape[1]
    qcol = DIFF_QK_W // WIN_Q_W
    kcol = (Q_W + DIFF_QK_W + DIFF_VW) // LANES
    kccol = (DIFF_QK_W + DIFF_VW) // LANES
    kern = functools.partial(_win_attn_kernel, seq=n, tq=tq)
    return pl.pallas_call(
        kern, grid=(b, n // step),
        in_specs=[pl.BlockSpec(memory_space=pltpu.SMEM),
                  pl.BlockSpec((None, step, WIN_Q_W), lambda bi, i: (bi, i, qcol)),
                  pl.BlockSpec((None, n, LANES), lambda bi, i: (bi, 0, kcol)),
                  pl.BlockSpec((None, n, LANES), lambda bi, i: (bi, 0, kcol + 1)),
                  pl.BlockSpec((None, nc, LANES), lambda bi, i: (bi, 0, kccol)),
                  pl.BlockSpec((None, nc, LANES), lambda bi, i: (bi, 0, kccol + 1))],
        out_specs=pl.BlockSpec((None, step, WIN_Q_W), lambda bi, i: (bi, i, 0)),
        out_shape=jax.ShapeDtypeStruct((b, n, WIN_Q_W), BF16),
        compiler_params=_cparams(2), name="win_attn",
    )(sink, proj, proj, proj, proj_c, proj_c)


def _layer_norm(r, g, b):
    mu = jnp.mean(r, -1, keepdims=True)
    dlt = r - mu
    var = jnp.mean(dlt * dlt, -1, keepdims=True)
    return dlt * lax.rsqrt(var + LN_EPS) * g + b


def _proj_ln_kernel(*refs, n_in, sub):
    a_refs = refs[:n_in]
    w_refs = refs[n_in:2 * n_in]
    x_ref, gate_ref, lng_ref, lnb_ref, sc_ref, sh_ref, rw_ref, x1_ref, h2_ref, lg_ref = refs[2 * n_in:]

    def split(v):
        hi = v.astype(BF16)
        return hi, (v - hi.astype(F32)).astype(BF16)

    def project(r0):
        y = jnp.dot(a_refs[0][r0:r0 + sub, :], w_refs[0][...], preferred_element_type=F32)
        for a_ref, w_ref in zip(a_refs[1:], w_refs[1:]):
            y = y + jnp.dot(a_ref[r0:r0 + sub, :], w_ref[...], preferred_element_type=F32)
        return y

    rw_hi, rw_lo = split(rw_ref[...])
    rw_both = jnp.concatenate([rw_hi, rw_lo], 1)
    starts = list(range(0, x_ref.shape[0], sub))
    nxt = project(starts[0])
    for i, r0 in enumerate(starts):
        y = nxt
        if i + 1 < len(starts):
            nxt = project(starts[i + 1])
        xn = _layer_norm(DEEPNORM_ALPHA * x_ref[r0:r0 + sub, :] + gate_ref[...] * y, lng_ref[...], lnb_ref[...])
        x1_ref[r0:r0 + sub, :] = xn
        h2 = xn * (1.0 + sc_ref[...]) + sh_ref[...]
        h2_ref[r0:r0 + sub, :] = h2.astype(h2_ref.dtype)
        h_hi, h_lo = split(h2)
        hh = jnp.dot(h_hi, rw_both, preferred_element_type=F32)
        lg = hh[:, :LANES] + hh[:, LANES:] + jnp.dot(h_lo, rw_hi, preferred_element_type=F32)
        lg_ref[:, r0:r0 + sub] = lg.T[:N_EXPERTS, :]


def _proj_ln(acts, ws, x2d, gate, ln_g, ln_b, sc, sh, router_wt, rows_per_mod, tm, name):
    t, d = x2d.shape
    n_in = len(acts)
    tiles_per_mod = rows_per_mod // tm
    nmod = gate.shape[0]
    row = lambda i: (i, 0)
    full = lambda i: (0, 0)
    mod = lambda i: (i // tiles_per_mod, 0, 0)
    in_specs = ([pl.BlockSpec((tm, a.shape[1]), row) for a in acts]
                + [pl.BlockSpec(w.shape, full) for w in ws]
                + [pl.BlockSpec((tm, d), row), pl.BlockSpec((None, 1, d), mod),
                   pl.BlockSpec((1, d), full), pl.BlockSpec((1, d), full),
                   pl.BlockSpec((None, 1, d), mod), pl.BlockSpec((None, 1, d), mod),
                   pl.BlockSpec((d, LANES), full)])
    return pl.pallas_call(
        functools.partial(_proj_ln_kernel, n_in=n_in, sub=LANES), grid=(t // tm,), in_specs=in_specs,
        out_specs=[pl.BlockSpec((tm, d), row), pl.BlockSpec((tm, d), row),
                   pl.BlockSpec((N_EXPERTS, tm), lambda i: (0, i))],
        out_shape=[jax.ShapeDtypeStruct((t, d), F32), jax.ShapeDtypeStruct((t, d), BF16),
                   jax.ShapeDtypeStruct((N_EXPERTS, t), F32)],
        compiler_params=_cparams(1), name=name,
    )(*acts, *ws, x2d, gate.reshape(nmod, 1, d), ln_g.reshape(1, d), ln_b.reshape(1, d),
      sc.reshape(nmod, 1, d), sh.reshape(nmod, 1, d), router_wt)


def _first_argmax(vals):
    idx = jnp.zeros(vals[0].shape, I32)
    best = vals[0]
    for j in range(1, len(vals)):
        upd = vals[j] > best
        idx = jnp.where(upd, j, idx)
        best = jnp.where(upd, vals[j], best)
    return idx, best


def _route_kernel(lg_ref, bias_ref, pos_ref, w_ref, tab_ref, cnt_ref, carry_ref, off_ref, *, tr):
    @pl.when(pl.program_id(0) == 0)
    def _():
        carry_ref[...] = jnp.zeros_like(carry_ref)

    lg = lg_ref[...]
    ex = jnp.exp(lg - jnp.max(lg, 0, keepdims=True))
    scores = ex / jnp.sum(ex, 0, keepdims=True)
    sel = scores + bias_ref[...]
    rows = [sel[e:e + 1] for e in range(N_EXPERTS)]
    group_scores = []
    for g in range(N_GROUPS):
        r = rows[g * EXPERTS_PER_GROUP:(g + 1) * EXPERTS_PER_GROUP]
        best = None
        for i in range(EXPERTS_PER_GROUP):
            for j in range(i + 1, EXPERTS_PER_GROUP):
                s = r[i] + r[j]
                best = s if best is None else jnp.maximum(best, s)
        group_scores.append(best)
    grp, _ = _first_argmax(group_scores)
    vals = []
    for j in range(EXPERTS_PER_GROUP):
        v = rows[(N_GROUPS - 1) * EXPERTS_PER_GROUP + j]
        for g in range(N_GROUPS - 2, -1, -1):
            v = jnp.where(grp == g, rows[g * EXPERTS_PER_GROUP + j], v)
        vals.append(v)
    i0, _ = _first_argmax(vals)
    i1, _ = _first_argmax([jnp.where(i0 == j, -jnp.inf, vals[j]) for j in range(EXPERTS_PER_GROUP)])
    e0 = grp * EXPERTS_PER_GROUP + i0
    e1 = grp * EXPERTS_PER_GROUP + i1
    eid = lax.broadcasted_iota(I32, (N_EXPERTS, 1), 0)
    oh0, oh1 = eid == e0, eid == e1
    s0 = jnp.sum(jnp.where(oh0, scores, 0.0), 0, keepdims=True)
    s1 = jnp.sum(jnp.where(oh1, scores, 0.0), 0, keepdims=True)
    den = s0 + s1
    member = jnp.where(oh0 | oh1, 1.0, 0.0)
    before = lax.broadcasted_iota(I32, (tr, tr), 0) < lax.broadcasted_iota(I32, (tr, tr), 1)
    upper = jnp.where(before, 1.0, 0.0).astype(BF16)
    cnt = jnp.dot(member.astype(BF16), upper, preferred_element_type=F32)
    run = jnp.sum(member, 1, keepdims=True)
    run = jnp.floor((run + (ROW_ALIGN - 1)) * (1.0 / ROW_ALIGN)) * ROW_ALIGN
    run = jnp.broadcast_to(run, (N_EXPERTS, LANES))
    acc = jnp.zeros((1, LANES), F32)
    for e in range(N_EXPERTS):
        off_ref[e:e + 1, :] = acc
        acc = acc + run[e:e + 1]
    off = off_ref[...]
    at = off[:, 0:1] + cnt
    pos_ref[0:1, :] = jnp.sum(jnp.where(oh0, at, 0.0), 0, keepdims=True).astype(I32)
    pos_ref[1:2, :] = jnp.sum(jnp.where(oh1, at, 0.0), 0, keepdims=True).astype(I32)
    w_ref[0:1, :] = s0 / den
    w_ref[1:2, :] = s1 / den
    tab_ref[0] = off.astype(I32)
    tab_ref[1] = run.astype(I32)
    tab_ref[2] = carry_ref[...].astype(I32)
    carry_ref[...] = carry_ref[...] + run
    cnt_ref[...] = carry_ref[...]


def _route(logits_t, router_bias, tr):
    t = logits_t.shape[1]
    tok = lambda i: (0, i)
    return pl.pallas_call(
        functools.partial(_route_kernel, tr=tr), grid=(t // tr,),
        in_specs=[pl.BlockSpec((N_EXPERTS, tr), tok), pl.BlockSpec((N_EXPERTS, 1), lambda i: (0, 0))],
        out_specs=[pl.BlockSpec((2, tr), tok), pl.BlockSpec((2, tr), tok),
                   pl.BlockSpec((None, 3, N_EXPERTS, LANES), lambda i: (i, 0, 0, 0)),
                   pl.BlockSpec((N_EXPERTS, LANES), lambda i: (0, 0))],
        out_shape=[jax.ShapeDtypeStruct((2, t), I32), jax.ShapeDtypeStruct((2, t), F32),
                   jax.ShapeDtypeStruct((t // tr, 3, N_EXPERTS, LANES), I32),
                   jax.ShapeDtypeStruct((N_EXPERTS, LANES), F32)],
        scratch_shapes=[pltpu.VMEM((N_EXPERTS, LANES), F32), pltpu.VMEM((N_EXPERTS, LANES), F32)],
        compiler_params=_cparams(1), name="route",
    )(logits_t, router_bias.reshape(N_EXPERTS, 1))


def _run_copies(n, max_rows, make_copy, wait):
    sz = max_rows
    while sz >= ROW_ALIGN:
        start = (n // (2 * sz)) * (2 * sz)

        @pl.when((n & sz) != 0)
        def _(start=start, sz=sz):
            cp = make_copy(start, sz)
            cp.wait() if wait else cp.start()

        sz //= 2


def _wait_rows(total, max_rows, make_copy):
    sz = max_rows
    while sz >= ROW_ALIGN:
        @pl.when((total & sz) != 0)
        def _(sz=sz):
            make_copy(sz).wait()

        sz //= 2


def _rows(ref, start, size):
    return ref.at[pl.ds(pl.multiple_of(start, ROW_ALIGN), size)]


def _dispatch_kernel(tab_ref, h_ref, pos_ref, w_ref, xs_ref, buf, zbuf, sem, *, tt, n_tiles, bm):
    i = pl.program_id(0)
    d = h_ref.shape[1]
    sb = buf.shape[1]
    ne = N_EXPERTS
    fill = 3 * n_tiles * ne

    @pl.when(i == 0)
    def _():
        zbuf[...] = jnp.zeros_like(zbuf)
        for wait in (False, True):
            for e in range(ne):
                dst, n = tab_ref[fill + e], tab_ref[fill + ne + e]
                _run_copies(n, bm // 2, lambda s, z: pltpu.make_async_copy(
                    zbuf.at[pl.ds(0, z)], _rows(xs_ref, dst + s, z), sem.at[2]), wait)

        def zero_block(j, carry):
            for half in range(2):
                cp = pltpu.make_async_copy(zbuf, _rows(xs_ref, j * bm + half * (bm // 2), bm // 2), sem.at[2])
                cp.start()
                cp.wait()
            return carry

        lax.fori_loop(tab_ref[fill + 2 * ne], xs_ref.shape[0] // bm, zero_block, 0)

    def tile_copies(tile, slot, wait):
        for e in range(ne):
            off = tab_ref[tile * ne + e]
            n = tab_ref[(n_tiles + tile) * ne + e]
            dst = tab_ref[(2 * n_tiles + tile) * ne + e]
            _run_copies(n, tt, lambda s, z: pltpu.make_async_copy(
                _rows(buf.at[slot], off + s, z), _rows(xs_ref, dst + s, z), sem.at[slot]), wait)

    slot = i % 2
    pos, w = pos_ref[...], w_ref[...]
    hb = h_ref[...].astype(BF16)
    for r0 in range(0, sb, LANES):
        srow = r0 + lax.broadcasted_iota(I32, (LANES, 1), 0)
        m0, m1 = srow == pos[0:1], srow == pos[1:2]
        perm = (jnp.where(m0, 1.0, 0.0) + jnp.where(m1, 1.0, 0.0)).astype(BF16)
        buf[slot, r0:r0 + LANES, :d] = jnp.dot(perm, hb, preferred_element_type=F32)
        wrow = jnp.sum(jnp.where(m0, w[0:1], 0.0) + jnp.where(m1, w[1:2], 0.0), 1, keepdims=True)
        buf[slot, r0:r0 + LANES, d:] = jnp.broadcast_to(wrow, (LANES, LANES))

    def wait_tile(tile, slot):
        total = tab_ref[fill + 2 * ne + 1 + tile]
        _wait_rows(total, tt * 2, lambda z: pltpu.make_async_copy(
            buf.at[slot, pl.ds(0, z)], xs_ref.at[pl.ds(0, z)], sem.at[slot]))

    @pl.when(i > 0)
    def _():
        wait_tile(i - 1, 1 - slot)

    tile_copies(i, slot, False)

    @pl.when(i == n_tiles - 1)
    def _():
        wait_tile(i, slot)


def _dispatch(tab, h2, pos, wts, n_slots, tt, sb, bm):
    t, d = h2.shape
    n_tiles = t // tt
    return pl.pallas_call(
        functools.partial(_dispatch_kernel, tt=tt, n_tiles=n_tiles, bm=bm),
        grid_spec=pltpu.PrefetchScalarGridSpec(
            num_scalar_prefetch=1, grid=(n_tiles,),
            in_specs=[pl.BlockSpec((tt, d), lambda i, s: (i, 0)), pl.BlockSpec((2, tt), lambda i, s: (0, i)),
                      pl.BlockSpec((2, tt), lambda i, s: (0, i))],
            out_specs=pl.BlockSpec(memory_space=pl.ANY),
            scratch_shapes=[pltpu.VMEM((2, sb, d + LANES), F32), pltpu.VMEM((bm // 2, d + LANES), F32),
                            pltpu.SemaphoreType.DMA((3,))]),
        out_shape=jax.ShapeDtypeStruct((n_slots, d + LANES), F32),
        compiler_params=_cparams(1), name="moe_dispatch",
    )(tab, h2, pos, wts)


def _ffn_kernel(blk_e_ref, blk_rows_ref, next_e_ref, nused_ref, x_ref, wg_hbm, wu_hbm, wd_hbm, o_ref,
                stage, wg_bf, wu_bf, wd_bf, sem, *, layer, parts):
    del nused_ref
    i = pl.program_id(0)
    e = blk_e_ref[i]
    new_expert = (i == 0) | (e != blk_e_ref[jnp.maximum(i - 1, 0)])

    def fetch(expert, wait):
        for j, src in enumerate((wg_hbm, wu_hbm, wd_hbm)):
            cp = pltpu.make_async_copy(src.at[layer, expert], stage.at[j], sem.at[j])
            cp.wait() if wait else cp.start()

    @pl.when(i == 0)
    def _():
        fetch(e, False)

    @pl.when(new_expert)
    def _():
        fetch(e, True)
        wg_bf[...] = stage[0].astype(BF16)
        wu_bf[...] = stage[1].astype(BF16)
        wd_bf[...] = stage[2].astype(BF16)

        @pl.when(next_e_ref[i] >= 0)
        def _():
            fetch(next_e_ref[i], False)

    d = wg_bf.shape[0]
    rows = blk_rows_ref[i]
    part = x_ref.shape[0] // parts
    spans = [slice(p * part, (p + 1) * part) for p in range(parts)]

    def gate_up(rs):
        x = x_ref[rs, :d].astype(BF16)
        return (jnp.dot(x, wg_bf[...], preferred_element_type=F32), jnp.dot(x, wu_bf[...], preferred_element_type=F32))

    def finish(rs, gate, up):
        act = (gate * jax.nn.sigmoid(gate) * up).astype(BF16)
        o_ref[rs, :] = jnp.dot(act, wd_bf[...], preferred_element_type=F32) * x_ref[rs, d:d + 1]

    all_parts = rows > (parts - 1) * part

    @pl.when(all_parts)
    def _():
        nxt = gate_up(spans[0])
        for p, rs in enumerate(spans):
            cur = nxt
            if p + 1 < parts:
                nxt = gate_up(spans[p + 1])
            finish(rs, *cur)

    for p, rs in enumerate(spans):
        @pl.when(jnp.logical_not(all_parts) & (rows > p * part))
        def _(rs=rs):
            finish(rs, *gate_up(rs))

        @pl.when(rows <= p * part)
        def _(rs=rs):
            o_ref[rs, :] = jnp.zeros((part, d), F32)


def _ffn(blk_e, blk_rows, next_e, n_used, xs, wg, wu, wd, layer, bm, parts):
    n_slots, xw = xs.shape
    d, ff = wg.shape[2:]
    assert d == ff
    xrow = lambda i, be, br, ne, nu: (jnp.minimum(i, nu[0] - 1), 0)
    hbm = pl.BlockSpec(memory_space=pl.ANY)
    return pl.pallas_call(
        functools.partial(_ffn_kernel, layer=layer, parts=parts),
        grid_spec=pltpu.PrefetchScalarGridSpec(
            num_scalar_prefetch=4, grid=(n_slots // bm,),
            in_specs=[pl.BlockSpec((bm, xw), xrow), hbm, hbm, hbm],
            out_specs=pl.BlockSpec((bm, d), lambda i, be, br, ne, nu: (i, 0)),
            scratch_shapes=[pltpu.VMEM((3, d, ff), F32), pltpu.VMEM((d, ff), BF16), pltpu.VMEM((d, ff), BF16),
                            pltpu.VMEM((ff, d), BF16), pltpu.SemaphoreType.DMA((3,))]),
        out_shape=jax.ShapeDtypeStruct((n_slots, d), F32),
        compiler_params=_cparams(1), name="moe_ffn",
    )(blk_e, blk_rows, next_e, n_used, xs, wg, wu, wd)


def _combine_kernel(tab_ref, ys_ref, pos_ref, x1_ref, gate_ref, lng_ref, lnb_ref, o_ref, ybuf, sem, *, tt, n_tiles):
    i = pl.program_id(0)
    sb = ybuf.shape[1]
    ne = N_EXPERTS
    slot = i % 2

    def tile_copies(tile, slot, wait):
        for e in range(ne):
            off = tab_ref[tile * ne + e]
            n = tab_ref[(n_tiles + tile) * ne + e]
            src = tab_ref[(2 * n_tiles + tile) * ne + e]
            _run_copies(n, tt, lambda s, z: pltpu.make_async_copy(
                _rows(ys_ref, src + s, z), _rows(ybuf.at[slot], off + s, z), sem.at[slot]), wait)

    @pl.when(i == 0)
    def _():
        ybuf[...] = jnp.zeros_like(ybuf)
        tile_copies(0, 0, False)

    @pl.when(i + 1 < n_tiles)
    def _():
        tile_copies(i + 1, 1 - slot, False)

    total = tab_ref[3 * n_tiles * ne + 2 * ne + 1 + i]
    _wait_rows(total, tt * 2, lambda z: pltpu.make_async_copy(
        ys_ref.at[pl.ds(0, z)], ybuf.at[slot, pl.ds(0, z)], sem.at[slot]))
    scol = lax.broadcasted_iota(I32, (1, sb), 1)
    yb = ybuf[slot].astype(BF16)

    def moe_rows(r0):
        pos = pos_ref[r0:r0 + LANES, :]
        unperm = (jnp.where(scol == pos[:, 0:1], 1.0, 0.0) + jnp.where(scol == pos[:, 1:2], 1.0, 0.0)).astype(BF16)
        return jnp.dot(unperm, yb, preferred_element_type=F32)

    starts = list(range(0, tt, LANES))
    nxt = moe_rows(starts[0])
    for j, r0 in enumerate(starts):
        f = nxt
        if j + 1 < len(starts):
            nxt = moe_rows(starts[j + 1])
        o_ref[r0:r0 + LANES, :] = _layer_norm(DEEPNORM_ALPHA * x1_ref[r0:r0 + LANES, :] + gate_ref[...] * f,
                                              lng_ref[...], lnb_ref[...])


def _combine(tab, ys, pos_t, x1, gate, ln_g, ln_b, rows_per_mod, tt, sb):
    t, d = x1.shape
    tiles_per_mod = rows_per_mod // tt
    nmod = gate.shape[0]
    return pl.pallas_call(
        functools.partial(_combine_kernel, tt=tt, n_tiles=t // tt),
        grid_spec=pltpu.PrefetchScalarGridSpec(
            num_scalar_prefetch=1, grid=(t // tt,),
            in_specs=[pl.BlockSpec(memory_space=pl.ANY),
                      pl.BlockSpec((tt, 2), lambda i, s: (i, 0)),
                      pl.BlockSpec((tt, d), lambda i, s: (i, 0)),
                      pl.BlockSpec((None, 1, d), lambda i, s: (i // tiles_per_mod, 0, 0)),
                      pl.BlockSpec((1, d), lambda i, s: (0, 0)), pl.BlockSpec((1, d), lambda i, s: (0, 0))],
            out_specs=pl.BlockSpec((tt, d), lambda i, s: (i, 0)),
            scratch_shapes=[pltpu.VMEM((2, sb, d), F32), pltpu.SemaphoreType.DMA((2,))]),
        out_shape=jax.ShapeDtypeStruct((t, d), F32),
        compiler_params=_cparams(1), name="moe_combine",
    )(tab, ys, pos_t, x1, gate.reshape(nmod, 1, d), ln_g.reshape(1, d), ln_b.reshape(1, d))


def _moe(h2, logits_t, x1, gate, ln_g, ln_b, router_bias, wg, wu, wd, layer, rows_per_mod):
    t, d = h2.shape
    bm, tt, ne = MOE_BM, MOE_TILE, N_EXPERTS
    n_tiles = t // tt
    pad = ne * (ROW_ALIGN - 1)
    sb = -(-(2 * tt + pad) // LANES) * LANES
    pos, wts, tab3, cnt = _route(logits_t, router_bias, tr=tt)
    rows = cnt[:, 0].astype(I32)
    prows = (rows + bm - 1) // bm * bm
    pends = jnp.cumsum(prows)
    pstarts = pends - prows
    n_blk = -(-(2 * t + n_tiles * pad + ne * (bm - 1)) // bm)
    n_used = pends[-1] // bm
    blk_ids = jnp.minimum(jnp.arange(n_blk, dtype=I32), n_used - 1)
    blk_e = jnp.minimum(jnp.sum((blk_ids[:, None] * bm >= pends[None, :]).astype(I32), 1), ne - 1)
    eids = jnp.arange(ne, dtype=I32)
    of_blk = lambda per_expert: jnp.sum(jnp.where(blk_e[:, None] == eids[None, :], per_expert[None, :], 0), 1)
    raw_ids = jnp.arange(n_blk, dtype=I32)
    blk_rows = jnp.where(raw_ids < n_used, jnp.clip(of_blk(pstarts + rows) - raw_ids * bm, 0, bm), 0)
    later_used = (eids[None, :] > eids[:, None]) & (prows[None, :] > 0)
    next_used = jnp.min(jnp.where(later_used, eids[None, :], ne), 1)
    next_e = of_blk(jnp.where(next_used < ne, next_used, -1))
    tab3 = tab3[:, :, :, 0]
    tab = jnp.concatenate([tab3[:, 0].reshape(-1), tab3[:, 1].reshape(-1),
                           (tab3[:, 2] + pstarts[None, :]).reshape(-1), pstarts + rows, prows - rows,
                           n_used.reshape(1), jnp.sum(tab3[:, 1], 1)]).astype(I32)
    xs = _dispatch(tab, h2, pos, wts, n_blk * bm, tt, sb, bm)
    ys = _ffn(blk_e, blk_rows.astype(I32), next_e.astype(I32), n_used.reshape(1).astype(I32), xs, wg, wu, wd,
              layer, bm, parts=2)
    return _combine(tab, ys, pos.T, x1, gate, ln_g, ln_b, rows_per_mod, tt, sb)


def _filter_mlp_kernel(z_ref, w1_ref, w2_ref, w3_ref, b_ref, sf_ref, o_ref):
    hp = lax.Precision.HIGHEST
    b, sf = b_ref[...], sf_ref[...]
    h = jnp.sin(sf[0:1] * (jnp.dot(z_ref[...], w1_ref[...], precision=hp, preferred_element_type=F32) + b[0:1]))
    h = jnp.sin(sf[1:2] * (jnp.dot(h, w2_ref[...], precision=hp, preferred_element_type=F32) + b[1:2]))
    o_ref[...] = jnp.sin(sf[2:3] * (jnp.dot(h, w3_ref[...], precision=hp, preferred_element_type=F32) + b[2:3]))


def _filter_mlp(zp, w_in_p, w_hid, b, sf):
    n = zp.shape[0]
    return pl.pallas_call(
        _filter_mlp_kernel, out_shape=jax.ShapeDtypeStruct((n, FILTER_HID), F32),
        compiler_params=pltpu.CompilerParams(vmem_limit_bytes=VMEM_LIMIT), name="hyena_filter_mlp",
    )(zp, w_in_p, w_hid[0], w_hid[1], b, sf)


def _short_conv(u_ref, cw_ref, cb_ref):
    n = u_ref.shape[0]
    u = u_ref[...].astype(F32)
    row = lax.broadcasted_iota(I32, (n, 1), 0)
    prev = jnp.where(row == 0, 0.0, pltpu.roll(u, 1, 0))
    nxt = jnp.where(row == n - 1, 0.0, pltpu.roll(u, n - 1, 0))
    cw = cw_ref[...]
    return prev * cw[0:1] + u * cw[1:2] + nxt * cw[2:3] + cb_ref[...]


def _lag_spec_kernel(hid_ref, wf_ref, wb_ref, dec_ref, ph_ref, h_ref, fw_scr, bw_scr, *, s, nb):
    lag = pl.program_id(2)
    n = hid_ref.shape[0]

    @pl.when(lag == 0)
    def _():
        def split(v):
            hi = v.astype(BF16)
            return hi, (v - hi.astype(F32)).astype(BF16)

        def precise_dot(a_hi, a_lo, w):
            w_hi, w_lo = split(w)
            return (jnp.dot(a_hi, w_hi, preferred_element_type=F32) + jnp.dot(a_lo, w_hi, preferred_element_type=F32)
                    + jnp.dot(a_hi, w_lo, preferred_element_type=F32))

        dec = dec_ref[...]
        hid_hi, hid_lo = split(hid_ref[...])
        row = lax.broadcasted_iota(I32, (n, 1), 0)
        fw = precise_dot(hid_hi, hid_lo, wf_ref[...]) * dec
        bw = precise_dot(hid_hi, hid_lo, wb_ref[...]) * dec
        fw_scr[...] = fw.astype(BF16)
        bw_scr[...] = jnp.where(row == 0, 0.0, bw).astype(BF16)

    def piece(ph, scr, j):
        return jnp.dot(ph_ref[ph], scr[j * s:(j + 1) * s, :], preferred_element_type=F32)

    for idx in range(2 * nb - 1):
        m = idx - (nb - 1)

        @pl.when(lag == idx)
        def _(m=m):
            if m >= 1:
                h = piece(0, fw_scr, m) + piece(1, fw_scr, m - 1)
            elif m == 0:
                h = piece(0, fw_scr, 0) + piece(2, bw_scr, 0)
            else:
                h = piece(2, bw_scr, -m) + piece(3, bw_scr, -m - 1)
            h_ref[...] = h.astype(h_ref.dtype)


def _lag_spec(hid, w_out4, decay, phases, s, tc):
    n = hid.shape[0]
    d = D_MODEL
    nb = n // s
    return pl.pallas_call(
        functools.partial(_lag_spec_kernel, s=s, nb=nb), grid=(2, d // tc, 2 * nb - 1),
        in_specs=[pl.BlockSpec((n, FILTER_HID), lambda o, c, l: (0, 0)),
                  pl.BlockSpec((None, None, FILTER_HID, tc), lambda o, c, l: (0, o, 0, c)),
                  pl.BlockSpec((None, None, FILTER_HID, tc), lambda o, c, l: (1, o, 0, c)),
                  pl.BlockSpec((n, tc), lambda o, c, l: (0, c)),
                  pl.BlockSpec((4, 2 * s, s), lambda o, c, l: (0, 0, 0))],
        out_specs=pl.BlockSpec((None, None, 2 * s, tc), lambda o, c, l: (o, l, 0, c)),
        out_shape=jax.ShapeDtypeStruct((2, 2 * nb - 1, 2 * s, d), BF16),
        scratch_shapes=[pltpu.VMEM((n, tc), BF16), pltpu.VMEM((n, tc), BF16)],
        compiler_params=_cparams(3), name="hyena_lag_spec",
    )(hid, w_out4, w_out4, decay, phases)


def _part_conv_kernel(uv_ref, ux1_ref, ux2_ref, cwv_ref, cwx1_ref, cwx2_ref, cbv_ref, cbx1_ref, cbx2_ref,
                      f_ref, finv_ref, h_ref, sk_ref, o_ref, src_scr, gate_scr, zb_scr, zall, ybuf, *, s, nb, rc):
    src_scr[...] = _short_conv(uv_ref, cwv_ref, cbv_ref)
    gates = ((ux1_ref, cwx1_ref, cbx1_ref), (ux2_ref, cwx2_ref, cbx2_ref))
    for o, gate in enumerate(gates):
        zb_scr[...] = src_scr[...].astype(BF16)
        gate_scr[...] = _short_conv(*gate)
        for j in range(nb):
            zall[j] = jnp.dot(f_ref[...], zb_scr[j * s:(j + 1) * s, :], preferred_element_type=F32).astype(BF16)
        sk = sk_ref[o]
        top = lax.broadcasted_iota(I32, (rc, 1), 0) == 0
        for i in range(nb):
            pairs = [(j, i - j + nb - 1) for j in range(nb)]
            dc = nyq = None
            for j, m in pairs:
                t0 = zall[j, 0:rc, :].astype(F32)[0:1] * h_ref[o, m, 0:rc, :].astype(F32)[0:1]
                t1 = zall[j, s:s + rc, :].astype(F32)[0:1] * h_ref[o, m, s:s + rc, :].astype(F32)[0:1]
                dc, nyq = (t0, t1) if dc is None else (dc + t0, nyq + t1)
            for r0 in range(0, s, rc):
                re, im = slice(r0, r0 + rc), slice(s + r0, s + r0 + rc)
                yr = yi = None
                for j, m in pairs:
                    zr, zi, hr, g = zall[j, re, :], zall[j, im, :], h_ref[o, m, re, :], h_ref[o, m, im, :]
                    tr, ti = zr * hr - zi * g, zr * g + zi * hr
                    yr, yi = (tr, ti) if yr is None else (yr + tr, yi + ti)
                if r0 == 0:
                    yr, yi = jnp.where(top, dc.astype(BF16), yr), jnp.where(top, nyq.astype(BF16), yi)
                ybuf[re, :] = yr
                ybuf[im, :] = yi
            y = jnp.dot(finv_ref[...], ybuf[...], preferred_element_type=F32)
            rows = slice(i * s, (i + 1) * s)
            zn = gate_scr[rows, :] * (y + src_scr[rows, :] * sk)
            if o + 1 < len(gates):
                src_scr[rows, :] = zn
            else:
                o_ref[rows, :] = zn.astype(o_ref.dtype)


def _part_conv(u, conv_w, conv_b, f_tab, finv_tab, spec, skip, s, tc):
    b, n, _ = u.shape
    d = D_MODEL
    ncb = d // tc
    nb = n // s
    col = lambda part: (lambda c, bi: (bi, 0, part * ncb + c))
    wcol = lambda part: (lambda c, bi: (0, part * ncb + c))
    const = lambda c, bi: (0, 0)
    in_specs = ([pl.BlockSpec((None, n, tc), col(p)) for p in range(3)]
                + [pl.BlockSpec((3, tc), wcol(p)) for p in range(3)]
                + [pl.BlockSpec((1, tc), wcol(p)) for p in range(3)]
                + [pl.BlockSpec((2 * s, s), const), pl.BlockSpec((s, 2 * s), const),
                   pl.BlockSpec((2, 2 * nb - 1, 2 * s, tc), lambda c, bi: (0, 0, 0, c), pipeline_mode=pl.Buffered(1)),
                   pl.BlockSpec((2, 1, tc), lambda c, bi: (0, 0, c))])
    return pl.pallas_call(
        functools.partial(_part_conv_kernel, s=s, nb=nb, rc=HYENA_ROWS), grid=(ncb, b),
        in_specs=in_specs,
        out_specs=pl.BlockSpec((None, n, tc), lambda c, bi: (bi, 0, c)),
        out_shape=jax.ShapeDtypeStruct((b, n, d), BF16),
        scratch_shapes=[pltpu.VMEM((n, tc), F32), pltpu.VMEM((n, tc), F32), pltpu.VMEM((n, tc), BF16),
                        pltpu.VMEM((nb, 2 * s, tc), BF16), pltpu.VMEM((2 * s, tc), BF16)],
        compiler_params=_cparams(2), name="hyena_part_conv",
    )(u, u, u, conv_w, conv_w, conv_w, conv_b, conv_b, conv_b, f_tab, finv_tab, spec, skip.reshape(2, 1, d))


def kernel(x, c, ctx, c_ctx, mod_w, mod_b, ln_g, ln_b, attn_w_in, attn_lambda, attn_subln_g, attn_sink, attn_w_out,
           hy_w_in, hy_conv_w, hy_conv_b, hy_ffn_w_in, hy_ffn_w_hid, hy_ffn_b, hy_sin_freq, hy_ffn_w_out, hy_skip,
           hy_w_out, router_w, router_bias, exp_w_gate, exp_w_up, exp_w_down):
    b, n, d = x.shape
    nc = ctx.shape[1]
    t = b * n
    assert d == D_MODEL and b + 1 <= MOD_ROWS and n % 512 == 0 and nc % 256 == 0

    c_rows = jnp.zeros((MOD_ROWS, d), F32).at[:b].set(c).at[b].set(c_ctx)
    mods = _mods(c_rows, mod_w, mod_b)
    router_wt = jnp.pad(router_w, ((0, 0), (0, LANES - N_EXPERTS)))
    x2d = x.reshape(t, d)

    sh1, sc1, g1, sh2, sc2, g2 = jnp.split(mods[0], 6, axis=-1)
    cos, sin = _rope_tables(n)
    w_in = attn_w_in[0].astype(BF16)
    q_groups = tuple(range(Q_W // LANES))
    rope_groups = q_groups + tuple(range(Q_W // LANES, (Q_W + DIFF_QK_W) // LANES)) + (
        (Q_W + DIFF_QK_W + DIFF_VW) // LANES,)
    q_scales = tuple((g, HEAD_DIM ** -0.5 * LOG2_E) for g in q_groups)
    proj = _modmm(x2d, sc1[:b], sh1[:b], w_in, rows_per_mod=n, tm=PROJ_TM,
                  rope=(jnp.asarray(cos), jnp.asarray(sin), rope_groups, q_scales), name="attn_in_proj")
    proj_c = _modmm(ctx.reshape(b * nc, d), sc1[b:b + 1], sh1[b:b + 1], w_in[:, Q_W:], rows_per_mod=b * nc, tm=CTX_TM,
                    name="ctx_in_proj")
    proj = proj.reshape(b, n, ATTN_PROJ_W)
    proj_c = proj_c.reshape(b, nc, KV_W)
    lam_init = 0.8 - 0.6 * math.exp(-0.3 * 0)
    oa = _diff_attn(proj, proj_c, attn_lambda[0], attn_subln_g[0], lam_init, tq=DIFF_TQ, sub=DIFF_SUB)
    ow = _win_attn(proj, proj_c, attn_sink[0], step=WIN_STEP, tq=WIN_TQ)
    w_out = attn_w_out[0].astype(BF16)
    x1, h2, lgt = _proj_ln([oa.reshape(t, DIFF_VW), ow.reshape(t, WIN_Q_W)], [w_out[:DIFF_VW], w_out[DIFF_VW:]],
                           x2d, g1[:b], ln_g[0, 0], ln_b[0, 0], sc2[:b], sh2[:b], router_wt, rows_per_mod=n, tm=PROJ_TM,
                           name="attn_out_proj_ln")
    x2 = _moe(h2, lgt, x1, g2[:b], ln_g[0, 1], ln_b[0, 1], router_bias, exp_w_gate, exp_w_up, exp_w_down, layer=0,
              rows_per_mod=n)

    sh1, sc1, g1, sh2, sc2, g2 = jnp.split(mods[1], 6, axis=-1)
    u = _modmm(x2, sc1[:b], sh1[:b], hy_w_in[0].astype(BF16), rows_per_mod=n, tm=PROJ_TM, name="hyena_in_proj")
    part, tc = HYENA_PART, HYENA_TC
    phases_np, finv_np = _partition_tables(part)
    phases, finv_tab = jnp.asarray(phases_np).astype(BF16), jnp.asarray(finv_np).astype(BF16)
    zp, decay = _filter_tables(n)
    w_in_p = jnp.zeros((FILTER_HID, FILTER_HID), F32).at[:FILTER_EMB].set(hy_ffn_w_in[0])
    hid = _filter_mlp(jnp.asarray(zp), w_in_p, hy_ffn_w_hid[0], hy_ffn_b[0], hy_sin_freq[0])
    spec = _lag_spec(hid, hy_ffn_w_out[0].reshape(FILTER_HID, 2, 2, d).transpose(1, 2, 0, 3), jnp.asarray(decay),
                     phases, part, tc=LAG_SPEC_TC)
    u3 = u.reshape(b, n, 3 * d)
    conv_b = hy_conv_b[0].reshape(1, 3 * d)
    z2 = _part_conv(u3, hy_conv_w[0], conv_b, phases[0], finv_tab, spec, hy_skip[0], part, tc)
    x3, h2, lgt = _proj_ln([z2.reshape(t, d)], [hy_w_out[0].astype(BF16)], x2, g1[:b], ln_g[1, 0], ln_b[1, 0],
                           sc2[:b], sh2[:b], router_wt, rows_per_mod=n, tm=PROJ_TM, name="hyena_out_proj_ln")
    x4 = _moe(h2, lgt, x3, g2[:b], ln_g[1, 1], ln_b[1, 1], router_bias, exp_w_gate, exp_w_up, exp_w_down, layer=1,
              rows_per_mod=n)
    return x4.reshape(b, n, d)
```
